```python
import math
import jax, jax.numpy as jnp
from jax import lax
import numpy as np

D_MODEL = 1024
BATCH = 16
SEQ = 2048
DEPTH = 1

N_SUBLAYERS = 3
N_MOD = 3
D_FF = 2816
POOL_WINDOWS = (2, 4, 8, 16)
POOL_GROUPS = len(POOL_WINDOWS)
POOL_WIDTH = D_MODEL // 2
POOL_GROUP_DIM = POOL_WIDTH // POOL_GROUPS
N_HEADS = 8
QK_NOPE_DIM = 64
QK_ROPE_DIM = 32
V_HEAD_DIM = 64
QK_HEAD_DIM = QK_NOPE_DIM + QK_ROPE_DIM
Q_LORA_RANK = 384
KV_LORA_RANK = 256
MLA_WIDTH = N_HEADS * V_HEAD_DIM
ROPE_THETA = 10000.0
Q_BLOCK = 128
ATTN_SCALE = 1.0 / math.sqrt(QK_HEAD_DIM)
NORM_EPS = 1e-6
IN_SPLITS = (POOL_WIDTH, Q_LORA_RANK, KV_LORA_RANK, QK_ROPE_DIM, D_MODEL, D_MODEL)
IN_WIDTH = sum(IN_SPLITS)

kernel_name = "hybrid_pool_mla_macaron_adaln"


def rms_norm(x, g):
    xf = x.astype(jnp.float32)
    y = xf * lax.rsqrt(jnp.mean(xf * xf, axis=-1, keepdims=True) + NORM_EPS)
    return (y * g.astype(jnp.float32)).astype(x.dtype)


def modulate(h, shift, scale):
    return h * (1.0 + scale[:, None, :]) + shift[:, None, :]


def swiglu(h, w_in, w_out):
    gu = h @ w_in
    g, u = jnp.split(gu, 2, axis=-1)
    return (jax.nn.silu(g) * u) @ w_out


def rope_tables(positions):
    inv_freq = ROPE_THETA ** (-jnp.arange(0, QK_ROPE_DIM, 2, dtype=jnp.float32) / QK_ROPE_DIM)
    ang = positions.astype(jnp.float32)[..., None] * inv_freq
    ang = jnp.concatenate([ang, ang], axis=-1)
    return jnp.cos(ang), jnp.sin(ang)


def apply_rope(x, cos, sin):
    xf = x.astype(jnp.float32)
    x1, x2 = jnp.split(xf, 2, axis=-1)
    rot = jnp.concatenate([-x2, x1], axis=-1)
    return (xf * cos + rot * sin).astype(x.dtype)


def causal_multiscale_pool(u):
    B, S, _ = u.shape
    uf = u.astype(jnp.float32).reshape(B, S, POOL_GROUPS, POOL_GROUP_DIM)
    cs = jnp.pad(jnp.cumsum(uf, axis=1), ((0, 0), (1, 0), (0, 0), (0, 0)))
    t = jnp.arange(S)
    outs = []
    for g, w in enumerate(POOL_WINDOWS):
        hi = cs[:, 1:, g]
        lo = cs[:, jnp.maximum(t + 1 - w, 0), g]
        cnt = jnp.minimum(t + 1, w).astype(jnp.float32)
        outs.append((hi - lo) / cnt[None, :, None])
    pooled = jnp.stack(outs, axis=2)
    return (pooled - uf).astype(u.dtype)


def causal_mla_attention(q_nope, q_rope, k_nope, k_rope, v):
    S = q_nope.shape[1]
    outs = []
    for i in range(S // Q_BLOCK):
        q0, q1 = i * Q_BLOCK, (i + 1) * Q_BLOCK
        s = (jnp.einsum('bqhd,bkhd->bhqk', q_nope[:, q0:q1], k_nope[:, :q1])
             + jnp.einsum('bqhr,bkr->bhqk', q_rope[:, q0:q1], k_rope[:, :q1]))
        s = s.astype(jnp.float32) * ATTN_SCALE
        mask = jnp.arange(q1)[None, :] <= (q0 + jnp.arange(Q_BLOCK))[:, None]
        s = jnp.where(mask, s, jnp.float32(-1e30))
        p = jax.nn.softmax(s, axis=-1).astype(v.dtype)
        outs.append(jnp.einsum('bhqk,bkhd->bqhd', p, v[:, :q1]))
    return jnp.concatenate(outs, axis=1)


def _fwd_setup_inputs(seed: int = 0) -> dict:
    key = jax.random.key(seed)
    ks = jax.random.split(key, 32)
    f32 = jnp.float32

    def w(k, shape, fan_in, gain=1.0):
        return jax.random.normal(k, shape, f32) * (gain * fan_in ** -0.5)

    def gain(k, shape):
        return jnp.ones(shape, f32) + 0.05 * jax.random.normal(k, shape, f32)

    L = DEPTH
    x = jax.random.normal(ks[0], (BATCH, SEQ, D_MODEL), f32)
    c = jax.random.normal(ks[1], (BATCH, D_MODEL), f32)
    offs = jax.random.randint(ks[2], (BATCH, 1), 0, 256, dtype=jnp.int32)
    positions = jnp.arange(SEQ, dtype=jnp.int32)[None, :] + offs
    return {
        "x": x,
        "c": c,
        "positions": positions,
        "w_ada": w(ks[3], (L, D_MODEL, N_SUBLAYERS * N_MOD * D_MODEL), D_MODEL, 0.5),
        "b_ada": 0.05 * jax.random.normal(ks[4], (L, N_SUBLAYERS * N_MOD * D_MODEL), f32),
        "norm_ffn1": gain(ks[5], (L, D_MODEL)),
        "w_ffn1_in": w(ks[6], (L, D_MODEL, 2 * D_FF), D_MODEL),
        "w_ffn1_out": w(ks[7], (L, D_FF, D_MODEL), D_FF),
        "norm_mix": gain(ks[8], (L, D_MODEL)),
        "w_in": w(ks[9], (L, D_MODEL, IN_WIDTH), D_MODEL),
        "pool_grp": w(ks[10], (L, POOL_GROUPS, POOL_GROUP_DIM, POOL_GROUP_DIM), POOL_GROUP_DIM),
        "pool_scale": gain(ks[11], (L, POOL_WIDTH)),
        "w_pool_proj": w(ks[12], (L, POOL_WIDTH, D_MODEL), POOL_WIDTH),
        "q_a_norm": gain(ks[13], (L, Q_LORA_RANK)),
        "w_q_up": w(ks[14], (L, Q_LORA_RANK, N_HEADS * QK_HEAD_DIM), Q_LORA_RANK),
        "kv_a_norm": gain(ks[15], (L, KV_LORA_RANK)),
        "w_kv_up": w(ks[16], (L, KV_LORA_RANK, N_HEADS * (QK_NOPE_DIM + V_HEAD_DIM)), KV_LORA_RANK),
        "q_norm_nope": gain(ks[17], (L, QK_NOPE_DIM)),
        "q_norm_rope": gain(ks[18], (L, QK_ROPE_DIM)),
        "k_norm_nope": gain(ks[19], (L, QK_NOPE_DIM)),
        "k_norm_rope": gain(ks[20], (L, QK_ROPE_DIM)),
        "w_mla_proj": w(ks[21], (L, MLA_WIDTH, D_MODEL), MLA_WIDTH),
        "w_out": w(ks[22], (L, D_MODEL, D_MODEL), D_MODEL),
        "norm_ffn2": gain(ks[23], (L, D_MODEL)),
        "w_ffn2_in": w(ks[24], (L, D_MODEL, 2 * D_FF), D_MODEL),
        "w_ffn2_out": w(ks[25], (L, D_FF, D_MODEL), D_FF),
    }


def _fwd_reference(x, c, positions, w_ada, b_ada, norm_ffn1, w_ffn1_in, w_ffn1_out,
              norm_mix, w_in, pool_grp, pool_scale, w_pool_proj, q_a_norm, w_q_up,
              kv_a_norm, w_kv_up, q_norm_nope, q_norm_rope, k_norm_nope, k_norm_rope,
              w_mla_proj, w_out, norm_ffn2, w_ffn2_in, w_ffn2_out):
    B, S, D = x.shape
    cos, sin = rope_tables(positions)
    c_act = jax.nn.silu(c)
    split_pts = list(np.cumsum(IN_SPLITS)[:-1])

    for l in range(DEPTH):
        mod = (c_act @ w_ada[l] + b_ada[l]).reshape(B, N_SUBLAYERS, N_MOD, D)

        h = modulate(rms_norm(x, norm_ffn1[l]), mod[:, 0, 0], mod[:, 0, 1])
        x = x + 0.5 * mod[:, 0, 2][:, None, :] * swiglu(h, w_ffn1_in[l], w_ffn1_out[l])

        h = modulate(rms_norm(x, norm_mix[l]), mod[:, 1, 0], mod[:, 1, 1])
        u_pool, q_lat, kv_lat, k_rope, g_pool, g_mla = jnp.split(h @ w_in[l], split_pts, axis=-1)

        pooled = causal_multiscale_pool(u_pool).reshape(B, S, POOL_GROUPS, POOL_GROUP_DIM)
        pooled = jnp.einsum('bsgc,gcd->bsgd', pooled, pool_grp[l]).reshape(B, S, POOL_WIDTH)
        br_pool = (pooled * pool_scale[l]) @ w_pool_proj[l]

        q = (rms_norm(q_lat, q_a_norm[l]) @ w_q_up[l]).reshape(B, S, N_HEADS, QK_HEAD_DIM)
        q_nope, q_rope = q[..., :QK_NOPE_DIM], q[..., QK_NOPE_DIM:]
        kv = (rms_norm(kv_lat, kv_a_norm[l]) @ w_kv_up[l]).reshape(
            B, S, N_HEADS, QK_NOPE_DIM + V_HEAD_DIM)
        k_nope, v = kv[..., :QK_NOPE_DIM], kv[..., QK_NOPE_DIM:]
        q_nope = rms_norm(q_nope, q_norm_nope[l])
        k_nope = rms_norm(k_nope, k_norm_nope[l])
        q_rope = apply_rope(rms_norm(q_rope, q_norm_rope[l]), cos[:, :, None, :], sin[:, :, None, :])
        k_rope = apply_rope(rms_norm(k_rope, k_norm_rope[l]), cos, sin)
        attn = causal_mla_attention(q_nope, q_rope, k_nope, k_rope, v).reshape(B, S, MLA_WIDTH)
        br_mla = attn @ w_mla_proj[l]

        merged = jax.nn.sigmoid(g_pool) * br_pool + jax.nn.sigmoid(g_mla) * br_mla
        x = x + mod[:, 1, 2][:, None, :] * (merged @ w_out[l])

        h = modulate(rms_norm(x, norm_ffn2[l]), mod[:, 2, 0], mod[:, 2, 1])
        x = x + 0.5 * mod[:, 2, 2][:, None, :] * swiglu(h, w_ffn2_in[l], w_ffn2_out[l])

    return x


import jax as _jax
import jax.numpy as _jnp

TWIN_FORMAT = 'train_step'
FWD_PARAMS = ['x', 'c', 'positions', 'w_ada', 'b_ada', 'norm_ffn1', 'w_ffn1_in', 'w_ffn1_out', 'norm_mix', 'w_in', 'pool_grp', 'pool_scale', 'w_pool_proj', 'q_a_norm', 'w_q_up', 'kv_a_norm', 'w_kv_up', 'q_norm_nope', 'q_norm_rope', 'k_norm_nope', 'k_norm_rope', 'w_mla_proj', 'w_out', 'norm_ffn2', 'w_ffn2_in', 'w_ffn2_out']
TWIN_WEIGHTS = ['w_ada', 'b_ada', 'norm_ffn1', 'w_ffn1_in', 'w_ffn1_out', 'norm_mix', 'w_in', 'pool_grp', 'pool_scale', 'w_pool_proj', 'q_a_norm', 'w_q_up', 'kv_a_norm', 'w_kv_up', 'q_norm_nope', 'q_norm_rope', 'k_norm_nope', 'k_norm_rope', 'w_mla_proj', 'w_out', 'norm_ffn2', 'w_ffn2_in', 'w_ffn2_out']
TWIN_DIFF_INPUT = 'x'
TWIN_INPUTS = ['x', 'c', 'positions', 'w_ada', 'b_ada', 'norm_ffn1', 'w_ffn1_in', 'w_ffn1_out', 'norm_mix', 'w_in', 'pool_grp', 'pool_scale', 'w_pool_proj', 'q_a_norm', 'w_q_up', 'kv_a_norm', 'w_kv_up', 'q_norm_nope', 'q_norm_rope', 'k_norm_nope', 'k_norm_rope', 'w_mla_proj', 'w_out', 'norm_ffn2', 'w_ffn2_in', 'w_ffn2_out', 'loss_target', 'm_w_ada', 'm_b_ada', 'm_norm_ffn1', 'm_w_ffn1_in', 'm_w_ffn1_out', 'm_norm_mix', 'm_w_in', 'm_pool_grp', 'm_pool_scale', 'm_w_pool_proj', 'm_q_a_norm', 'm_w_q_up', 'm_kv_a_norm', 'm_w_kv_up', 'm_q_norm_nope', 'm_q_norm_rope', 'm_k_norm_nope', 'm_k_norm_rope', 'm_w_mla_proj', 'm_w_out', 'm_norm_ffn2', 'm_w_ffn2_in', 'm_w_ffn2_out', 'v_w_ada', 'v_b_ada', 'v_norm_ffn1', 'v_w_ffn1_in', 'v_w_ffn1_out', 'v_norm_mix', 'v_w_in', 'v_pool_grp', 'v_pool_scale', 'v_w_pool_proj', 'v_q_a_norm', 'v_w_q_up', 'v_kv_a_norm', 'v_w_kv_up', 'v_q_norm_nope', 'v_q_norm_rope', 'v_k_norm_nope', 'v_k_norm_rope', 'v_w_mla_proj', 'v_w_out', 'v_norm_ffn2', 'v_w_ffn2_in', 'v_w_ffn2_out']
TWIN_OUTPUTS = ['loss', 'grad_x', 'grad_w_ada', 'grad_b_ada', 'grad_norm_ffn1', 'grad_w_ffn1_in', 'grad_w_ffn1_out', 'grad_norm_mix', 'grad_w_in', 'grad_pool_grp', 'grad_pool_scale', 'grad_w_pool_proj', 'grad_q_a_norm', 'grad_w_q_up', 'grad_kv_a_norm', 'grad_w_kv_up', 'grad_q_norm_nope', 'grad_q_norm_rope', 'grad_k_norm_nope', 'grad_k_norm_rope', 'grad_w_mla_proj', 'grad_w_out', 'grad_norm_ffn2', 'grad_w_ffn2_in', 'grad_w_ffn2_out', 'delta_w_ada', 'delta_b_ada', 'delta_norm_ffn1', 'delta_w_ffn1_in', 'delta_w_ffn1_out', 'delta_norm_mix', 'delta_w_in', 'delta_pool_grp', 'delta_pool_scale', 'delta_w_pool_proj', 'delta_q_a_norm', 'delta_w_q_up', 'delta_kv_a_norm', 'delta_w_kv_up', 'delta_q_norm_nope', 'delta_q_norm_rope', 'delta_k_norm_nope', 'delta_k_norm_rope', 'delta_w_mla_proj', 'delta_w_out', 'delta_norm_ffn2', 'delta_w_ffn2_in', 'delta_w_ffn2_out', 'new_m_w_ada', 'new_m_b_ada', 'new_m_norm_ffn1', 'new_m_w_ffn1_in', 'new_m_w_ffn1_out', 'new_m_norm_mix', 'new_m_w_in', 'new_m_pool_grp', 'new_m_pool_scale', 'new_m_w_pool_proj', 'new_m_q_a_norm', 'new_m_w_q_up', 'new_m_kv_a_norm', 'new_m_w_kv_up', 'new_m_q_norm_nope', 'new_m_q_norm_rope', 'new_m_k_norm_nope', 'new_m_k_norm_rope', 'new_m_w_mla_proj', 'new_m_w_out', 'new_m_norm_ffn2', 'new_m_w_ffn2_in', 'new_m_w_ffn2_out', 'new_v_w_ada', 'new_v_b_ada', 'new_v_norm_ffn1', 'new_v_w_ffn1_in', 'new_v_w_ffn1_out', 'new_v_norm_mix', 'new_v_w_in', 'new_v_pool_grp', 'new_v_pool_scale', 'new_v_w_pool_proj', 'new_v_q_a_norm', 'new_v_w_q_up', 'new_v_kv_a_norm', 'new_v_w_kv_up', 'new_v_q_norm_nope', 'new_v_q_norm_rope', 'new_v_k_norm_nope', 'new_v_k_norm_rope', 'new_v_w_mla_proj', 'new_v_w_out', 'new_v_norm_ffn2', 'new_v_w_ffn2_in', 'new_v_w_ffn2_out']
TWIN_LEAF_KINDS = {'loss': 'loss', 'grad_x': 'grad_x', 'grad_w_ada': 'grad_w', 'grad_b_ada': 'grad_w', 'grad_norm_ffn1': 'grad_w', 'grad_w_ffn1_in': 'grad_w', 'grad_w_ffn1_out': 'grad_w', 'grad_norm_mix': 'grad_w', 'grad_w_in': 'grad_w', 'grad_pool_grp': 'grad_w', 'grad_pool_scale': 'grad_w', 'grad_w_pool_proj': 'grad_w', 'grad_q_a_norm': 'grad_w', 'grad_w_q_up': 'grad_w', 'grad_kv_a_norm': 'grad_w', 'grad_w_kv_up': 'grad_w', 'grad_q_norm_nope': 'grad_w', 'grad_q_norm_rope': 'grad_w', 'grad_k_norm_nope': 'grad_w', 'grad_k_norm_rope': 'grad_w', 'grad_w_mla_proj': 'grad_w', 'grad_w_out': 'grad_w', 'grad_norm_ffn2': 'grad_w', 'grad_w_ffn2_in': 'grad_w', 'grad_w_ffn2_out': 'grad_w', 'delta_w_ada': 'delta_w', 'delta_b_ada': 'delta_w', 'delta_norm_ffn1': 'delta_w', 'delta_w_ffn1_in': 'delta_w', 'delta_w_ffn1_out': 'delta_w', 'delta_norm_mix': 'delta_w', 'delta_w_in': 'delta_w', 'delta_pool_grp': 'delta_w', 'delta_pool_scale': 'delta_w', 'delta_w_pool_proj': 'delta_w', 'delta_q_a_norm': 'delta_w', 'delta_w_q_up': 'delta_w', 'delta_kv_a_norm': 'delta_w', 'delta_w_kv_up': 'delta_w', 'delta_q_norm_nope': 'delta_w', 'delta_q_norm_rope': 'delta_w', 'delta_k_norm_nope': 'delta_w', 'delta_k_norm_rope': 'delta_w', 'delta_w_mla_proj': 'delta_w', 'delta_w_out': 'delta_w', 'delta_norm_ffn2': 'delta_w', 'delta_w_ffn2_in': 'delta_w', 'delta_w_ffn2_out': 'delta_w', 'new_m_w_ada': 'new_m', 'new_m_b_ada': 'new_m', 'new_m_norm_ffn1': 'new_m', 'new_m_w_ffn1_in': 'new_m', 'new_m_w_ffn1_out': 'new_m', 'new_m_norm_mix': 'new_m', 'new_m_w_in': 'new_m', 'new_m_pool_grp': 'new_m', 'new_m_pool_scale': 'new_m', 'new_m_w_pool_proj': 'new_m', 'new_m_q_a_norm': 'new_m', 'new_m_w_q_up': 'new_m', 'new_m_kv_a_norm': 'new_m', 'new_m_w_kv_up': 'new_m', 'new_m_q_norm_nope': 'new_m', 'new_m_q_norm_rope': 'new_m', 'new_m_k_norm_nope': 'new_m', 'new_m_k_norm_rope': 'new_m', 'new_m_w_mla_proj': 'new_m', 'new_m_w_out': 'new_m', 'new_m_norm_ffn2': 'new_m', 'new_m_w_ffn2_in': 'new_m', 'new_m_w_ffn2_out': 'new_m', 'new_v_w_ada': 'new_v', 'new_v_b_ada': 'new_v', 'new_v_norm_ffn1': 'new_v', 'new_v_w_ffn1_in': 'new_v', 'new_v_w_ffn1_out': 'new_v', 'new_v_norm_mix': 'new_v', 'new_v_w_in': 'new_v', 'new_v_pool_grp': 'new_v', 'new_v_pool_scale': 'new_v', 'new_v_w_pool_proj': 'new_v', 'new_v_q_a_norm': 'new_v', 'new_v_w_q_up': 'new_v', 'new_v_kv_a_norm': 'new_v', 'new_v_w_kv_up': 'new_v', 'new_v_q_norm_nope': 'new_v', 'new_v_q_norm_rope': 'new_v', 'new_v_k_norm_nope': 'new_v', 'new_v_k_norm_rope': 'new_v', 'new_v_w_mla_proj': 'new_v', 'new_v_w_out': 'new_v', 'new_v_norm_ffn2': 'new_v', 'new_v_w_ffn2_in': 'new_v', 'new_v_w_ffn2_out': 'new_v'}


def _forward(args):
    return _fwd_reference(*[args[k] for k in FWD_PARAMS])


def _output_shape():
    out = _jax.eval_shape(lambda: _forward(_fwd_setup_inputs(0)))
    return out.shape, out.dtype

N_MICROBATCH = 1
ADAM_LR = 0.001
ADAM_B1 = 0.9
ADAM_B2 = 0.999
ADAM_EPS = 1e-08
ADAM_WD = 0.01
ADAM_STEP = 10
PER_EXAMPLE_BATCH_AXIS = {'x': 0, 'c': 0, 'positions': 0, 'loss_target': 0}
SHARED_INPUTS = []
_WEIGHT_DTYPES = {'w_ada': _jnp.float32, 'b_ada': _jnp.float32, 'norm_ffn1': _jnp.float32, 'w_ffn1_in': _jnp.float32, 'w_ffn1_out': _jnp.float32, 'norm_mix': _jnp.float32, 'w_in': _jnp.float32, 'pool_grp': _jnp.float32, 'pool_scale': _jnp.float32, 'w_pool_proj': _jnp.float32, 'q_a_norm': _jnp.float32, 'w_q_up': _jnp.float32, 'kv_a_norm': _jnp.float32, 'w_kv_up': _jnp.float32, 'q_norm_nope': _jnp.float32, 'q_norm_rope': _jnp.float32, 'k_norm_nope': _jnp.float32, 'k_norm_rope': _jnp.float32, 'w_mla_proj': _jnp.float32, 'w_out': _jnp.float32, 'norm_ffn2': _jnp.float32, 'w_ffn2_in': _jnp.float32, 'w_ffn2_out': _jnp.float32}
MOMENT_SCALE = {'w_ada': 3.705721e-01, 'b_ada': 8.089251e-01, 'norm_ffn1': 8.518198e-01, 'w_ffn1_in': 2.263210e-02, 'w_ffn1_out': 3.524631e-02, 'norm_mix': 8.785708e-01, 'w_in': 6.640274e-02, 'pool_grp': 1.552796e-01, 'pool_scale': 1.660091e+00, 'w_pool_proj': 8.404325e-02, 'q_a_norm': 1.288839e-02, 'w_q_up': 9.424331e-03, 'kv_a_norm': 3.273086e-01, 'w_kv_up': 6.625441e-02, 'q_norm_nope': 5.295966e-02, 'q_norm_rope': 5.195546e-02, 'k_norm_nope': 5.267510e-02, 'k_norm_rope': 5.337163e-02, 'w_mla_proj': 5.051801e-02, 'w_out': 8.672694e-02, 'norm_ffn2': 8.552326e-01, 'w_ffn2_in': 2.241556e-02, 'w_ffn2_out': 3.495679e-02}


def _to_microbatches(a, axis):
    t = _jnp.moveaxis(a, axis, 0)
    t = t.reshape((N_MICROBATCH, t.shape[0] // N_MICROBATCH) + t.shape[1:])
    return _jnp.moveaxis(t, 1, axis + 1)


def setup_inputs(seed: int = 0) -> dict:
    inp = _fwd_setup_inputs(seed)
    key = _jax.random.fold_in(_jax.random.key(seed), 7919)
    shape, _ = _output_shape()
    out = dict(inp)
    out["loss_target"] = _jax.random.normal(_jax.random.fold_in(key, 0), shape, _jnp.float32)
    for i, name in enumerate(TWIN_WEIGHTS):
        w = inp[name].astype(_jnp.float32)
        if MOMENT_SCALE is None:
            s = _jnp.sqrt(_jnp.mean(_jnp.square(w)) + 1e-30)
        else:
            s = MOMENT_SCALE[name]
        km, kv = _jax.random.split(_jax.random.fold_in(key, i + 1))
        out[name] = w
        out["m_" + name] = s * _jax.random.normal(km, w.shape, _jnp.float32)
        out["v_" + name] = (s * s) * _jax.random.uniform(kv, w.shape, _jnp.float32, 0.5, 1.5)
    if N_MICROBATCH > 1:
        for name, axis in PER_EXAMPLE_BATCH_AXIS.items():
            out[name] = _to_microbatches(out[name], axis)
    return {'x': out['x'], 'c': out['c'], 'positions': out['positions'], 'w_ada': out['w_ada'], 'b_ada': out['b_ada'], 'norm_ffn1': out['norm_ffn1'], 'w_ffn1_in': out['w_ffn1_in'], 'w_ffn1_out': out['w_ffn1_out'], 'norm_mix': out['norm_mix'], 'w_in': out['w_in'], 'pool_grp': out['pool_grp'], 'pool_scale': out['pool_scale'], 'w_pool_proj': out['w_pool_proj'], 'q_a_norm': out['q_a_norm'], 'w_q_up': out['w_q_up'], 'kv_a_norm': out['kv_a_norm'], 'w_kv_up': out['w_kv_up'], 'q_norm_nope': out['q_norm_nope'], 'q_norm_rope': out['q_norm_rope'], 'k_norm_nope': out['k_norm_nope'], 'k_norm_rope': out['k_norm_rope'], 'w_mla_proj': out['w_mla_proj'], 'w_out': out['w_out'], 'norm_ffn2': out['norm_ffn2'], 'w_ffn2_in': out['w_ffn2_in'], 'w_ffn2_out': out['w_ffn2_out'], 'loss_target': out['loss_target'], 'm_w_ada': out['m_w_ada'], 'm_b_ada': out['m_b_ada'], 'm_norm_ffn1': out['m_norm_ffn1'], 'm_w_ffn1_in': out['m_w_ffn1_in'], 'm_w_ffn1_out': out['m_w_ffn1_out'], 'm_norm_mix': out['m_norm_mix'], 'm_w_in': out['m_w_in'], 'm_pool_grp': out['m_pool_grp'], 'm_pool_scale': out['m_pool_scale'], 'm_w_pool_proj': out['m_w_pool_proj'], 'm_q_a_norm': out['m_q_a_norm'], 'm_w_q_up': out['m_w_q_up'], 'm_kv_a_norm': out['m_kv_a_norm'], 'm_w_kv_up': out['m_w_kv_up'], 'm_q_norm_nope': out['m_q_norm_nope'], 'm_q_norm_rope': out['m_q_norm_rope'], 'm_k_norm_nope': out['m_k_norm_nope'], 'm_k_norm_rope': out['m_k_norm_rope'], 'm_w_mla_proj': out['m_w_mla_proj'], 'm_w_out': out['m_w_out'], 'm_norm_ffn2': out['m_norm_ffn2'], 'm_w_ffn2_in': out['m_w_ffn2_in'], 'm_w_ffn2_out': out['m_w_ffn2_out'], 'v_w_ada': out['v_w_ada'], 'v_b_ada': out['v_b_ada'], 'v_norm_ffn1': out['v_norm_ffn1'], 'v_w_ffn1_in': out['v_w_ffn1_in'], 'v_w_ffn1_out': out['v_w_ffn1_out'], 'v_norm_mix': out['v_norm_mix'], 'v_w_in': out['v_w_in'], 'v_pool_grp': out['v_pool_grp'], 'v_pool_scale': out['v_pool_scale'], 'v_w_pool_proj': out['v_w_pool_proj'], 'v_q_a_norm': out['v_q_a_norm'], 'v_w_q_up': out['v_w_q_up'], 'v_kv_a_norm': out['v_kv_a_norm'], 'v_w_kv_up': out['v_w_kv_up'], 'v_q_norm_nope': out['v_q_norm_nope'], 'v_q_norm_rope': out['v_q_norm_rope'], 'v_k_norm_nope': out['v_k_norm_nope'], 'v_k_norm_rope': out['v_k_norm_rope'], 'v_w_mla_proj': out['v_w_mla_proj'], 'v_w_out': out['v_w_out'], 'v_norm_ffn2': out['v_norm_ffn2'], 'v_w_ffn2_in': out['v_w_ffn2_in'], 'v_w_ffn2_out': out['v_w_ffn2_out']}


def _loss(weights, diff, rest, loss_target):
    with _jax.named_scope("forward"):
        args = {**rest, TWIN_DIFF_INPUT: diff, **{k: w.astype(_WEIGHT_DTYPES[k]) for k, w in weights.items()}}
        y = _forward(args)
    with _jax.named_scope("loss_head"):
        err = _jnp.square(y.astype(_jnp.float32) - loss_target)
        return 0.5 * _jnp.sum(_jnp.mean(err, axis=-1)) if err.ndim else 0.5 * err


def _adamw(w, g, m, v):
    m = ADAM_B1 * m + (1.0 - ADAM_B1) * g
    v = ADAM_B2 * v + (1.0 - ADAM_B2) * _jnp.square(g)
    m_hat = m / (1.0 - ADAM_B1 ** ADAM_STEP)
    v_hat = v / (1.0 - ADAM_B2 ** ADAM_STEP)
    delta = -ADAM_LR * (m_hat / (_jnp.sqrt(v_hat) + ADAM_EPS) + ADAM_WD * w)
    return delta, m, v


def reference(x, c, positions, w_ada, b_ada, norm_ffn1, w_ffn1_in, w_ffn1_out, norm_mix, w_in, pool_grp, pool_scale, w_pool_proj, q_a_norm, w_q_up, kv_a_norm, w_kv_up, q_norm_nope, q_norm_rope, k_norm_nope, k_norm_rope, w_mla_proj, w_out, norm_ffn2, w_ffn2_in, w_ffn2_out, loss_target, m_w_ada, m_b_ada, m_norm_ffn1, m_w_ffn1_in, m_w_ffn1_out, m_norm_mix, m_w_in, m_pool_grp, m_pool_scale, m_w_pool_proj, m_q_a_norm, m_w_q_up, m_kv_a_norm, m_w_kv_up, m_q_norm_nope, m_q_norm_rope, m_k_norm_nope, m_k_norm_rope, m_w_mla_proj, m_w_out, m_norm_ffn2, m_w_ffn2_in, m_w_ffn2_out, v_w_ada, v_b_ada, v_norm_ffn1, v_w_ffn1_in, v_w_ffn1_out, v_norm_mix, v_w_in, v_pool_grp, v_pool_scale, v_w_pool_proj, v_q_a_norm, v_w_q_up, v_kv_a_norm, v_w_kv_up, v_q_norm_nope, v_q_norm_rope, v_k_norm_nope, v_k_norm_rope, v_w_mla_proj, v_w_out, v_norm_ffn2, v_w_ffn2_in, v_w_ffn2_out):
    given = dict(x=x, c=c, positions=positions, w_ada=w_ada, b_ada=b_ada, norm_ffn1=norm_ffn1, w_ffn1_in=w_ffn1_in, w_ffn1_out=w_ffn1_out, norm_mix=norm_mix, w_in=w_in, pool_grp=pool_grp, pool_scale=pool_scale, w_pool_proj=w_pool_proj, q_a_norm=q_a_norm, w_q_up=w_q_up, kv_a_norm=kv_a_norm, w_kv_up=w_kv_up, q_norm_nope=q_norm_nope, q_norm_rope=q_norm_rope, k_norm_nope=k_norm_nope, k_norm_rope=k_norm_rope, w_mla_proj=w_mla_proj, w_out=w_out, norm_ffn2=norm_ffn2, w_ffn2_in=w_ffn2_in, w_ffn2_out=w_ffn2_out, loss_target=loss_target, m_w_ada=m_w_ada, m_b_ada=m_b_ada, m_norm_ffn1=m_norm_ffn1, m_w_ffn1_in=m_w_ffn1_in, m_w_ffn1_out=m_w_ffn1_out, m_norm_mix=m_norm_mix, m_w_in=m_w_in, m_pool_grp=m_pool_grp, m_pool_scale=m_pool_scale, m_w_pool_proj=m_w_pool_proj, m_q_a_norm=m_q_a_norm, m_w_q_up=m_w_q_up, m_kv_a_norm=m_kv_a_norm, m_w_kv_up=m_w_kv_up, m_q_norm_nope=m_q_norm_nope, m_q_norm_rope=m_q_norm_rope, m_k_norm_nope=m_k_norm_nope, m_k_norm_rope=m_k_norm_rope, m_w_mla_proj=m_w_mla_proj, m_w_out=m_w_out, m_norm_ffn2=m_norm_ffn2, m_w_ffn2_in=m_w_ffn2_in, m_w_ffn2_out=m_w_ffn2_out, v_w_ada=v_w_ada, v_b_ada=v_b_ada, v_norm_ffn1=v_norm_ffn1, v_w_ffn1_in=v_w_ffn1_in, v_w_ffn1_out=v_w_ffn1_out, v_norm_mix=v_norm_mix, v_w_in=v_w_in, v_pool_grp=v_pool_grp, v_pool_scale=v_pool_scale, v_w_pool_proj=v_w_pool_proj, v_q_a_norm=v_q_a_norm, v_w_q_up=v_w_q_up, v_kv_a_norm=v_kv_a_norm, v_w_kv_up=v_w_kv_up, v_q_norm_nope=v_q_norm_nope, v_q_norm_rope=v_q_norm_rope, v_k_norm_nope=v_k_norm_nope, v_k_norm_rope=v_k_norm_rope, v_w_mla_proj=v_w_mla_proj, v_w_out=v_w_out, v_norm_ffn2=v_norm_ffn2, v_w_ffn2_in=v_w_ffn2_in, v_w_ffn2_out=v_w_ffn2_out)
    weights = {n: given[n] for n in TWIN_WEIGHTS}
    shared = {n: given[n] for n in SHARED_INPUTS}
    per_example = {n: given[n] for n in ['x', 'c', 'positions']}
    grad_fn = _jax.value_and_grad(_loss, argnums=(0, 1))

    def one_microbatch(ex, loss_target):
        ex = dict(ex)
        diff = ex.pop(TWIN_DIFF_INPUT)
        return grad_fn(weights, diff, {**shared, **ex}, loss_target)

    if N_MICROBATCH == 1:
        loss, (grad_w, grad_x) = one_microbatch(per_example, given["loss_target"])
    else:
        def body(carry, xs):
            loss_sum, grad_sum = carry
            l_k, (gw_k, gx_k) = one_microbatch(xs[0], xs[1])
            with _jax.named_scope("update"):
                return (loss_sum + l_k, _jax.tree.map(_jnp.add, grad_sum, gw_k)), gx_k

        init = (_jnp.zeros((), _jnp.float32), _jax.tree.map(_jnp.zeros_like, weights))
        (loss, grad_w), grad_x = _jax.lax.scan(body, init, (per_example, given["loss_target"]))
    with _jax.named_scope("update"):
        delta_w, new_m, new_v = {}, {}, {}
        for n in TWIN_WEIGHTS:
            delta_w[n], new_m[n], new_v[n] = _adamw(weights[n], grad_w[n], given["m_" + n], given["v_" + n])
    return (loss, grad_x, *[grad_w[n] for n in TWIN_WEIGHTS], *[delta_w[n] for n in TWIN_WEIGHTS],
            *[new_m[n] for n in TWIN_WEIGHTS], *[new_v[n] for n in TWIN_WEIGHTS])
```

```python
import functools
import math

import jax
import jax.numpy as jnp
from jax import lax
from jax.experimental import pallas as pl
from jax.experimental.pallas import tpu as pltpu

F32 = jnp.float32
BF16 = jnp.bfloat16
MESH = pl.DeviceIdType.MESH
AXES = ("x", "y", "c")
N_DEV = 8

D_MODEL = 1024
D_FF = 2816
N_HEADS = 8
HEAD_SLAB = 128
QK_NOPE = 64
QK_ROPE = 32
POOL_WIDTH = 512
POOL_GROUPS = 4
POOL_GROUP_DIM = 128
Q_LORA = 384
KV_LORA = 256
ROPE_THETA = 10000.0
ATTN_SCALE = 1.0 / math.sqrt(QK_NOPE + QK_ROPE)
NORM_EPS = 1e-6
ADAM_LR, ADAM_B1, ADAM_B2, ADAM_EPS, ADAM_WD, ADAM_STEP = 0.001, 0.9, 0.999, 1e-08, 0.01, 10

LANES = 128
SUBLANES = 8
VMEM_LIMIT = 52 * 1024 * 1024
PACK_ROWS = 512
ADAMW_WHOLE_BYTES = 3 << 19

BIG = ("w_ffn1_in", "w_ffn1_out", "w_in", "w_pool_proj", "w_q_up", "w_kv_up",
       "w_mla_proj", "w_out", "w_ffn2_in", "w_ffn2_out")
ROW_SHARDED = ("w_ffn1_out", "w_out", "w_ffn2_out")
SMALL = ("norm_ffn1", "norm_mix", "norm_ffn2", "pool_grp", "pool_scale", "q_a_norm",
         "kv_a_norm", "q_norm_nope", "q_norm_rope", "k_norm_nope", "k_norm_rope")
WEIGHTS = ("w_ada", "b_ada", "norm_ffn1", "w_ffn1_in", "w_ffn1_out", "norm_mix", "w_in",
           "pool_grp", "pool_scale", "w_pool_proj", "q_a_norm", "w_q_up", "kv_a_norm",
           "w_kv_up", "q_norm_nope", "q_norm_rope", "k_norm_nope", "k_norm_rope",
           "w_mla_proj", "w_out", "norm_ffn2", "w_ffn2_in", "w_ffn2_out")


def _params(*sem):
    return pltpu.CompilerParams(dimension_semantics=sem, vmem_limit_bytes=VMEM_LIMIT)


def _tile(n, cands):
    for c in cands:
        if n % c == 0:
            return c
    return n


def _my_pos():
    return lax.axis_index("x"), lax.axis_index("y"), lax.axis_index("c")


def _flip(pos, k):
    x, y, c = pos
    fx, fy, fc = (k >> 2) & 1, (k >> 1) & 1, k & 1
    return ((1 - x) if fx else x, (1 - y) if fy else y, (1 - c) if fc else c)


def _index(pos):
    x, y, c = pos
    return 4 * x + 2 * y + c


def _all_gather(x, name):
    def body(x_ref, o_ref, send_sems, recv_sems, local_sem):
        me = _my_pos()
        mine = pltpu.make_async_copy(x_ref, o_ref.at[_index(me)], local_sem)
        mine.start()
        sends = []
        for k in range(1, N_DEV):
            cp = pltpu.make_async_remote_copy(
                src_ref=x_ref, dst_ref=o_ref.at[_index(me)],
                send_sem=send_sems.at[k - 1], recv_sem=recv_sems.at[k - 1],
                device_id=_flip(me, k), device_id_type=MESH)
            cp.start()
            sends.append(cp)
        for k in range(1, N_DEV):
            peer = _flip(me, k)
            pltpu.make_async_remote_copy(
                src_ref=x_ref, dst_ref=o_ref.at[_index(peer)],
                send_sem=send_sems.at[k - 1], recv_sem=recv_sems.at[k - 1],
                device_id=peer, device_id_type=MESH).wait_recv()
        for cp in sends:
            cp.wait_send()
        mine.wait()

    return pl.pallas_call(
        body, name=name,
        out_shape=jax.ShapeDtypeStruct((N_DEV,) + x.shape, x.dtype),
        in_specs=[pl.BlockSpec(memory_space=pl.ANY)],
        out_specs=pl.BlockSpec(memory_space=pl.ANY),
        scratch_shapes=[pltpu.SemaphoreType.DMA((N_DEV - 1,)),
                        pltpu.SemaphoreType.DMA((N_DEV - 1,)),
                        pltpu.SemaphoreType.DMA],
    )(x)


def _all_to_all(x, name):
    def body(x_ref, o_ref, send_sems, recv_sems, local_sem):
        me = _my_pos()
        mine = pltpu.make_async_copy(x_ref.at[_index(me)], o_ref.at[_index(me)], local_sem)
        mine.start()
        sends = []
        for k in range(1, N_DEV):
            peer = _flip(me, k)
            cp = pltpu.make_async_remote_copy(
                src_ref=x_ref.at[_index(peer)], dst_ref=o_ref.at[_index(me)],
                send_sem=send_sems.at[k - 1], recv_sem=recv_sems.at[k - 1],
                device_id=peer, device_id_type=MESH)
            cp.start()
            sends.append(cp)
        for k in range(1, N_DEV):
            peer = _flip(me, k)
            pltpu.make_async_remote_copy(
                src_ref=x_ref.at[_index(me)], dst_ref=o_ref.at[_index(peer)],
                send_sem=send_sems.at[k - 1], recv_sem=recv_sems.at[k - 1],
                device_id=peer, device_id_type=MESH).wait_recv()
        for cp in sends:
            cp.wait_send()
        mine.wait()

    return pl.pallas_call(
        body, name=name,
        out_shape=jax.ShapeDtypeStruct(x.shape, x.dtype),
        in_specs=[pl.BlockSpec(memory_space=pl.ANY)],
        out_specs=pl.BlockSpec(memory_space=pl.ANY),
        scratch_shapes=[pltpu.SemaphoreType.DMA((N_DEV - 1,)),
                        pltpu.SemaphoreType.DMA((N_DEV - 1,)),
                        pltpu.SemaphoreType.DMA],
    )(x)


def _sum_blocks(x, name):
    n, rows, cols = x.shape
    tr = _tile(rows, (PACK_ROWS, 256, 128, 64, 32, 16, 8))

    def body(x_ref, o_ref):
        acc = x_ref[0].astype(F32)
        for d in range(1, n):
            acc = acc + x_ref[d].astype(F32)
        o_ref[...] = acc

    return pl.pallas_call(
        body, name=name,
        out_shape=jax.ShapeDtypeStruct((rows, cols), F32),
        grid=(rows // tr,),
        in_specs=[pl.BlockSpec((n, tr, cols), lambda i: (0, i, 0))],
        out_specs=pl.BlockSpec((tr, cols), lambda i: (i, 0)),
        compiler_params=_params("parallel"),
    )(x)


_DIMS = {"nn": (((1,), (0,)), ((), ())), "nt": (((1,), (1,)), ((), ())), "tn": (((0,), (0,)), ((), ()))}


def _mm(a, b, mode, name, out_dtype=F32, tm=None, tn=None, add=None, res=None, gate=None, seq=None):
    if mode == "tn":
        kdim, m = a.shape
    else:
        m, kdim = a.shape
    n = b.shape[0] if mode == "nt" else b.shape[1]
    tm = tm or _tile(m, (512, 256, 128))
    tn = tn or _tile(n, (512, 256, 128))
    if res is not None:
        assert seq % tm == 0
    dims = _DIMS[mode]

    def body(*refs):
        a_ref, b_ref = refs[0], refs[1]
        acc = lax.dot_general(a_ref[...].astype(BF16), b_ref[...].astype(BF16), dims,
                              preferred_element_type=F32)
        if add is not None:
            refs[3][...] = (acc + refs[2][...]).astype(out_dtype)
        elif res is not None:
            res_ref, gate_ref, o_ref, p_ref = refs[2:]
            o_ref[...] = res_ref[...] + gate_ref[0] * acc
            p_ref[...] = acc.astype(BF16)
        else:
            refs[2][...] = acc.astype(out_dtype)

    a_spec = (pl.BlockSpec((kdim, tm), lambda i, j: (0, i)) if mode == "tn"
              else pl.BlockSpec((tm, kdim), lambda i, j: (i, 0)))
    b_spec = (pl.BlockSpec((tn, kdim), lambda i, j: (j, 0)) if mode == "nt"
              else pl.BlockSpec((kdim, tn), lambda i, j: (0, j)))
    o_spec = pl.BlockSpec((tm, tn), lambda i, j: (i, j))
    in_specs, args = [a_spec, b_spec], [a, b]
    out_shape, out_specs = jax.ShapeDtypeStruct((m, n), out_dtype), o_spec
    if add is not None:
        in_specs.append(o_spec)
        args.append(add)
    if res is not None:
        per_seq = seq // tm
        in_specs += [o_spec, pl.BlockSpec((1, 1, tn), lambda i, j: (i // per_seq, 0, j))]
        args += [res, gate]
        out_shape = (jax.ShapeDtypeStruct((m, n), F32), jax.ShapeDtypeStruct((m, n), BF16))
        out_specs = (o_spec, o_spec)
    return pl.pallas_call(
        body, name=name, out_shape=out_shape, grid=(m // tm, n // tn),
        in_specs=in_specs, out_specs=out_specs,
        compiler_params=_params("parallel", "parallel"),
    )(*args)


def _rowmap(name, fn, seq, rows, bats=(), vecs=(), row_outs=(), bat_outs=(), vec_outs=(), ts=None):
    rows = [r if isinstance(r, tuple) else (r, r.shape[1], 0) for r in rows]
    tokens = rows[0][0].shape[0]
    nseq = tokens // seq
    ts = ts or _tile(seq, (256, 128, 64, 32, 16, 8))
    nt = seq // ts
    n_r, n_b, n_v = len(rows), len(bats), len(vecs)
    n_ro, n_bo = len(row_outs), len(bat_outs)

    def accumulate(ref, val, first):
        @pl.when(first)
        def _():
            ref[...] = val.reshape(ref.shape)

        @pl.when(jnp.logical_not(first))
        def _():
            ref[...] += val.reshape(ref.shape)

    def body(*refs):
        ins, outs = refs[:n_r + n_b + n_v], refs[n_r + n_b + n_v:]
        r_vals = [r[...] for r in ins[:n_r]]
        b_vals = [r[0] for r in ins[n_r:n_r + n_b]]
        v_vals = [r[...] for r in ins[n_r + n_b:]]
        ro, bo, vo = fn(r_vals, b_vals, v_vals)
        for ref, val in zip(outs[:n_ro], ro):
            ref[...] = val.astype(ref.dtype)
        b, i = pl.program_id(0), pl.program_id(1)
        for ref, val in zip(outs[n_ro:n_ro + n_bo], bo):
            accumulate(ref, val, i == 0)
        for ref, val in zip(outs[n_ro + n_bo:], vo):
            accumulate(ref, val, jnp.logical_and(i == 0, b == 0))

    in_specs = [pl.BlockSpec((ts, w), functools.partial(lambda b, i, cb: (b * nt + i, cb), cb=cb))
                for _, w, cb in rows]
    in_specs += [pl.BlockSpec((1, 1, v.shape[2]), lambda b, i: (b, 0, 0)) for v in bats]
    in_specs += [pl.BlockSpec((1, v.shape[1]), lambda b, i: (0, 0)) for v in vecs]
    out_shape = [jax.ShapeDtypeStruct((tokens, f), dt) for f, dt in row_outs]
    out_specs = [pl.BlockSpec((ts, f), lambda b, i: (b * nt + i, 0)) for f, _ in row_outs]
    out_shape += [jax.ShapeDtypeStruct((nseq, 1, f), F32) for f in bat_outs]
    out_specs += [pl.BlockSpec((1, 1, f), lambda b, i: (b, 0, 0)) for f in bat_outs]
    out_shape += [jax.ShapeDtypeStruct((1, f), F32) for f in vec_outs]
    out_specs += [pl.BlockSpec((1, f), lambda b, i: (0, 0)) for f in vec_outs]
    return pl.pallas_call(
        body, name=name, out_shape=tuple(out_shape), grid=(nseq, nt),
        in_specs=in_specs, out_specs=tuple(out_specs),
        compiler_params=_params("arbitrary", "arbitrary"),
    )(*([r[0] for r in rows] + list(bats) + list(vecs)))


def _colsum(v):
    return jnp.sum(v, axis=0, keepdims=True)


def _rstd(x, width=None):
    width = width or x.shape[-1]
    return lax.rsqrt(jnp.sum(x * x, axis=-1, keepdims=True) * (1.0 / width) + NORM_EPS)


def _norm_bwd(dy, x, r, g, width=None):
    width = width or x.shape[-1]
    xhat = x * r
    dxhat = dy * g
    dx = r * (dxhat - xhat * (jnp.sum(dxhat * xhat, axis=-1, keepdims=True) * (1.0 / width)))
    return dx, dy * xhat


def _sigmoid(x):
    return 1.0 / (1.0 + jnp.exp(-x))


def _norm_mod_fwd(x, gamma, shift, scale, seq, name):
    def fn(rows, bats, vecs):
        (xv,), (sh, sc), (g,) = rows, bats, vecs
        return [xv * _rstd(xv) * g * (1.0 + sc) + sh], [], []
    return _rowmap(name, fn, seq, [x], [shift, scale], [gamma], row_outs=[(D_MODEL, BF16)])[0]


def _norm_mod_bwd(dh, x, dres, gamma, scale, seq, name):
    def fn(rows, bats, vecs):
        (dhv, xv, dr), (sc,), (g,) = rows, bats, vecs
        r = _rstd(xv)
        dx, dg = _norm_bwd(dhv * (1.0 + sc), xv, r, g)
        return [dr + dx], [_colsum(dhv), _colsum(dhv * (xv * r * g))], [_colsum(dg)]
    return _rowmap(name, fn, seq, [dh, x, dres], [scale], [gamma], row_outs=[(D_MODEL, F32)],
                   bat_outs=[D_MODEL, D_MODEL], vec_outs=[D_MODEL])


def _ffn_fwd(x, p, seq, tag):
    h = _norm_mod_fwd(x, p["gamma"], p["shift"], p["scale"], seq, f"{tag}_norm")
    gu = _mm(h, p["w_in"], "nn", f"{tag}_in", out_dtype=BF16, tm=_tile(x.shape[0], (1024, 512, 256)))

    def act(rows, bats, vecs):
        g, u = rows[0][:, :D_FF].astype(F32), rows[0][:, D_FF:].astype(F32)
        return [g * _sigmoid(g) * u], [], []
    a = _rowmap(f"{tag}_act", act, seq, [gu], row_outs=[(D_FF, BF16)])[0]
    x_new, f = _mm(a, p["w_out"], "nn", f"{tag}_out", res=x, gate=p["gate"], seq=seq,
                   tm=_tile(seq, (512, 256, 128)))
    return x_new, (x, h, gu, f)


def _ffn_bwd(dxo, saved, p, seq, tag):
    x, h, gu, f = saved

    def pre(rows, bats, vecs):
        (dv, fv), (gate,) = rows, bats
        return [gate * dv], [_colsum(dv * fv.astype(F32))], []
    df, dgate = _rowmap(f"{tag}_bwd_pre", pre, seq, [dxo, f], [p["gate"]],
                        row_outs=[(D_MODEL, BF16)], bat_outs=[D_MODEL])
    da = _mm(df, p["w_out"], "nt", f"{tag}_bwd_da")

    def act_bwd(rows, bats, vecs):
        dav, guv = rows
        g, u = guv[:, :D_FF].astype(F32), guv[:, D_FF:].astype(F32)
        sg = _sigmoid(g)
        silu = g * sg
        dg = dav * u * (sg * (1.0 + g * (1.0 - sg)))
        return [silu * u, jnp.concatenate([dg, dav * silu], axis=1)], [], []
    a, dgu = _rowmap(f"{tag}_bwd_act", act_bwd, seq, [da, gu],
                     row_outs=[(D_FF, BF16), (2 * D_FF, BF16)])
    dw_out = _mm(a, df, "tn", f"{tag}_bwd_wout", tm=256, tn=D_MODEL)
    dw_in = _mm(h, dgu, "tn", f"{tag}_bwd_win", tm=D_MODEL, tn=512)
    dh = _mm(dgu, p["w_in"], "nt", f"{tag}_bwd_dh", tm=_tile(x.shape[0], (1024, 512, 256)), tn=256)
    dx, dshift, dscale, dgamma = _norm_mod_bwd(dh, x, dxo, p["gamma"], p["scale"], seq, f"{tag}_bwd_norm")
    return dx, dw_in, dw_out, dgamma, dshift, dscale, dgate


def _shift_rows(v, k, forward):
    n = v.shape[0]
    row = lax.broadcasted_iota(jnp.int32, v.shape, 0)
    if forward:
        return jnp.where(row >= k, pltpu.roll(v, k, 0), 0.0)
    return jnp.where(row < n - k, pltpu.roll(v, n - k, 0), 0.0)


def _window_sums(v, forward):
    out, s, k = [], v, 1
    for _ in range(POOL_GROUPS):
        s = s + _shift_rows(s, k, forward)
        out.append(s)
        k *= 2
    return out


def _by_group(vals, g):
    out = vals[-1]
    for idx in range(len(vals) - 2, -1, -1):
        out = jnp.where(g == idx, vals[idx], out)
    return out


def _inv_count(shape, g):
    t1 = lax.broadcasted_iota(jnp.int32, shape, 0) + 1
    window = _by_group([jnp.int32(2 ** (i + 1)) for i in range(POOL_GROUPS)], g)
    return 1.0 / jnp.minimum(t1, window).astype(F32)


def _pool_fwd(u, grp, scale, seq):
    tokens = u.shape[0]

    def body(u_ref, grp_ref, sc_ref, pooled_ref, pg_ref, ps_ref):
        g = pl.program_id(1)
        uv = u_ref[...]
        sums = _by_group(_window_sums(uv, True), g)
        pooled = (sums * _inv_count(uv.shape, g) - uv).astype(BF16)
        pg = jnp.dot(pooled, grp_ref[0].astype(BF16), preferred_element_type=F32)
        pooled_ref[...] = pooled
        pg_ref[...] = pg
        ps_ref[...] = (pg * sc_ref[...]).astype(BF16)

    blk = pl.BlockSpec((seq, POOL_GROUP_DIM), lambda b, g: (b, g))
    return pl.pallas_call(
        body, name="pool_fwd", grid=(tokens // seq, POOL_GROUPS),
        out_shape=(jax.ShapeDtypeStruct(u.shape, BF16), jax.ShapeDtypeStruct(u.shape, F32),
                   jax.ShapeDtypeStruct(u.shape, BF16)),
        in_specs=[blk, pl.BlockSpec((1, POOL_GROUP_DIM, POOL_GROUP_DIM), lambda b, g: (g, 0, 0)),
                  pl.BlockSpec((1, POOL_GROUP_DIM), lambda b, g: (0, g))],
        out_specs=(blk, blk, blk),
        compiler_params=_params("parallel", "parallel"),
    )(u, grp, scale)


def _pool_bwd(dps, pooled, pg, grp, scale, seq):
    tokens = dps.shape[0]

    def body(dps_ref, pooled_ref, pg_ref, grp_ref, sc_ref, du_ref, dgrp_ref, dsc_ref):
        g, b = pl.program_id(0), pl.program_id(1)
        dpsv = dps_ref[...]
        dpg = (dpsv * sc_ref[...]).astype(BF16)
        dsc = _colsum(dpsv * pg_ref[...])
        dgrp = lax.dot_general(pooled_ref[...], dpg, _DIMS["tn"], preferred_element_type=F32)

        @pl.when(b == 0)
        def _():
            dsc_ref[...] = dsc
            dgrp_ref[0] = dgrp

        @pl.when(b > 0)
        def _():
            dsc_ref[...] += dsc
            dgrp_ref[0] += dgrp

        dpool = lax.dot_general(dpg, grp_ref[0].astype(BF16), _DIMS["nt"], preferred_element_type=F32)
        sums = _by_group(_window_sums(dpool * _inv_count(dpool.shape, g), False), g)
        du_ref[...] = (sums - dpool).astype(BF16)

    blk = pl.BlockSpec((seq, POOL_GROUP_DIM), lambda g, b: (b, g))
    grp_spec = pl.BlockSpec((1, POOL_GROUP_DIM, POOL_GROUP_DIM), lambda g, b: (g, 0, 0))
    vec_spec = pl.BlockSpec((1, POOL_GROUP_DIM), lambda g, b: (0, g))
    return pl.pallas_call(
        body, name="pool_bwd", grid=(POOL_GROUPS, tokens // seq),
        out_shape=(jax.ShapeDtypeStruct(dps.shape, BF16), jax.ShapeDtypeStruct(grp.shape, F32),
                   jax.ShapeDtypeStruct(scale.shape, F32)),
        in_specs=[blk, blk, blk, grp_spec, vec_spec],
        out_specs=(blk, grp_spec, vec_spec),
        compiler_params=_params("arbitrary", "arbitrary"),
    )(dps, pooled, pg, grp, scale)


def _lane(shape):
    return lax.broadcasted_iota(jnp.int32, shape, len(shape) - 1)


def _rot(y):
    lane = _lane(y.shape)
    r = jnp.where(lane < QK_NOPE + QK_ROPE // 2,
                  -pltpu.roll(y, HEAD_SLAB - QK_ROPE // 2, 1), pltpu.roll(y, QK_ROPE // 2, 1))
    return jnp.where(jnp.logical_and(lane >= QK_NOPE, lane < QK_NOPE + QK_ROPE), r, 0.0)


def _part_rstd(x):
    sq = x * x
    nope = _lane(x.shape) < QK_NOPE
    s_nope = jnp.sum(jnp.where(nope, sq, 0.0), axis=-1, keepdims=True)
    s_rope = jnp.sum(sq, axis=-1, keepdims=True) - s_nope
    return jnp.where(nope, lax.rsqrt(s_nope * (1.0 / QK_NOPE) + NORM_EPS),
                     lax.rsqrt(s_rope * (1.0 / QK_ROPE) + NORM_EPS))


def _part_norm_bwd(dy, x, r, g):
    nope = _lane(x.shape) < QK_NOPE
    xhat = x * r
    dxhat = dy * g
    prod = dxhat * xhat
    m_nope = jnp.sum(jnp.where(nope, prod, 0.0), axis=-1, keepdims=True)
    m_rope = jnp.sum(prod, axis=-1, keepdims=True) - m_nope
    mean = jnp.where(nope, m_nope * (1.0 / QK_NOPE), m_rope * (1.0 / QK_ROPE))
    return r * (dxhat - xhat * mean), dy * xhat


def _latent_norm_fwd(z_a, g_q, g_kv, seq):
    def fn(rows, bats, vecs):
        q, kv = rows[0][:, :Q_LORA], rows[0][:, Q_LORA:Q_LORA + KV_LORA]
        return [q * _rstd(q) * vecs[0], kv * _rstd(kv) * vecs[1]], [], []
    return _rowmap("latent_norm", fn, seq, [z_a], vecs=[g_q, g_kv],
                   row_outs=[(Q_LORA, BF16), (KV_LORA, BF16)])


def _latent_norm_bwd(dqn, dkvn, dkr, z_a, g_q, g_kv, seq):
    def fn(rows, bats, vecs):
        dq, dkv, dkrv, z = rows
        q, kv = z[:, :Q_LORA], z[:, Q_LORA:Q_LORA + KV_LORA]
        dxq, dgq = _norm_bwd(dq, q, _rstd(q), vecs[0])
        dxkv, dgkv = _norm_bwd(dkv, kv, _rstd(kv), vecs[1])
        return [jnp.concatenate([dxq, dxkv, dkrv], axis=1)], [], [_colsum(dgq), _colsum(dgkv)]
    return _rowmap("latent_norm_bwd", fn, seq, [dqn, dkvn, dkr, z_a], vecs=[g_q, g_kv],
                   row_outs=[(Q_LORA + KV_LORA + HEAD_SLAB, BF16)], vec_outs=[Q_LORA, KV_LORA])


def _qk_prep_fwd(qp, kv, z_a, pos, g_q, g_kn, g_kr, inv_freq, seq):
    def fn(rows, bats, vecs):
        qv, kvv, kr, p = rows
        gq, gkn, gkr, invf = vecs
        ang = p * invf
        cos, sin = jnp.cos(ang), jnp.sin(ang)
        nope = _lane(kr.shape) < QK_NOPE
        krn = kr * _rstd(kr, QK_ROPE) * gkr
        krr = krn * cos + _rot(krn) * sin
        qs, ks, vs = [], [], []
        for h in range(N_HEADS):
            xq = qv[:, h * HEAD_SLAB:(h + 1) * HEAD_SLAB]
            y = xq * _part_rstd(xq) * gq
            qs.append(y * cos + _rot(y) * sin)
            xk = kvv[:, h * HEAD_SLAB:(h + 1) * HEAD_SLAB]
            kn = jnp.where(nope, xk, 0.0)
            ks.append(jnp.where(nope, kn * _rstd(kn, QK_NOPE) * gkn, krr))
            vs.append(jnp.where(nope, 0.0, xk))
        return [jnp.concatenate(v, axis=1) for v in (qs, ks, vs)], [], []
    width = N_HEADS * HEAD_SLAB
    return _rowmap("qk_prep", fn, seq, [qp, kv, (z_a, HEAD_SLAB, 5), pos], vecs=[g_q, g_kn, g_kr, inv_freq],
                   row_outs=[(width, BF16)] * 3, ts=_tile(seq, (128, 64, 32, 16, 8)))


def _qk_prep_bwd(dqc, dkc, dvp, qp, kv, z_a, pos, g_q, g_kn, g_kr, inv_freq, seq):
    def fn(rows, bats, vecs):
        dq, dk, dv, qv, kvv, kr, p = rows
        gq, gkn, gkr, invf = vecs
        ang = p * invf
        cos, sin = jnp.cos(ang), jnp.sin(ang)
        nope = _lane(kr.shape) < QK_NOPE
        dqs, dkvs = [], []
        dgq = jnp.zeros((1, HEAD_SLAB), F32)
        dgkn = jnp.zeros((1, HEAD_SLAB), F32)
        dkrr = jnp.zeros(kr.shape, F32)
        for h in range(N_HEADS):
            sl = slice(h * HEAD_SLAB, (h + 1) * HEAD_SLAB)
            dyr = dq[:, sl]
            dy = dyr * cos - _rot(dyr * sin)
            xq = qv[:, sl]
            dx, dg = _part_norm_bwd(dy, xq, _part_rstd(xq), gq)
            dqs.append(dx)
            dgq = dgq + _colsum(dg)
            dkh = dk[:, sl]
            dkrr = dkrr + jnp.where(nope, 0.0, dkh)
            kn = jnp.where(nope, kvv[:, sl], 0.0)
            dxk, dgk = _norm_bwd(jnp.where(nope, dkh, 0.0), kn, _rstd(kn, QK_NOPE), gkn, QK_NOPE)
            dgkn = dgkn + _colsum(dgk)
            dkvs.append(jnp.where(nope, dxk, dv[:, sl]))
        dkrn = dkrr * cos - _rot(dkrr * sin)
        dkr, dgkr = _norm_bwd(dkrn, kr, _rstd(kr, QK_ROPE), gkr, QK_ROPE)
        return ([jnp.concatenate(dqs, axis=1), jnp.concatenate(dkvs, axis=1), dkr], [],
                [dgq, dgkn, _colsum(dgkr)])
    width = N_HEADS * HEAD_SLAB
    return _rowmap("qk_prep_bwd", fn, seq, [dqc, dkc, dvp, qp, kv, (z_a, HEAD_SLAB, 5), pos],
                   vecs=[g_q, g_kn, g_kr, inv_freq],
                   row_outs=[(width, BF16), (width, BF16), (HEAD_SLAB, F32)],
                   vec_outs=[HEAD_SLAB] * 3, ts=_tile(seq, (128, 64, 32, 16, 8)))


def _causal(s, qi, kj, tq, tk):
    qpos = qi * tq + lax.broadcasted_iota(jnp.int32, s.shape, 0)
    kpos = kj * tk + lax.broadcasted_iota(jnp.int32, s.shape, 1)
    return jnp.where(kpos <= qpos, s, -1e30)


def _attn_fwd(qc, kc, vp, seq):
    tokens = qc.shape[0]
    tq = tk = _tile(seq, (256, 128))
    nq = seq // tq

    def body(q_ref, k_ref, v_ref, o_ref, lse_ref):
        i = pl.program_id(2)
        q = q_ref[...]

        def step(j, carry):
            m, l, acc = carry
            rows = pl.ds(pl.multiple_of(j * tk, tk), tk)
            s = lax.dot_general(q, k_ref[rows, :], _DIMS["nt"], preferred_element_type=F32) * ATTN_SCALE
            s = _causal(s, i, j, tq, tk)
            m_new = jnp.maximum(m, jnp.max(s, axis=-1, keepdims=True))
            alpha = jnp.exp(m - m_new)
            p = jnp.exp(s - m_new)
            l_new = alpha * l + jnp.sum(p, axis=-1, keepdims=True)
            acc_new = alpha * acc + jnp.dot(p.astype(BF16), v_ref[rows, :], preferred_element_type=F32)
            return m_new, l_new, acc_new

        m0 = jnp.full((tq, 1), -1e30, F32)
        l0 = jnp.zeros((tq, 1), F32)
        acc0 = jnp.zeros((tq, HEAD_SLAB), F32)
        m, l, acc = lax.fori_loop(0, i + 1, step, (m0, l0, acc0))
        o_ref[...] = (acc / l).astype(BF16)
        lse_ref[...] = jnp.broadcast_to(m + jnp.log(l), (tq, HEAD_SLAB))

    q_spec = pl.BlockSpec((tq, HEAD_SLAB), lambda b, h, i: (b * nq + i, h))
    kv_spec = pl.BlockSpec((seq, HEAD_SLAB), lambda b, h, i: (b, h))
    return pl.pallas_call(
        body, name="attn_fwd", grid=(tokens // seq, N_HEADS, nq),
        out_shape=(jax.ShapeDtypeStruct(qc.shape, BF16), jax.ShapeDtypeStruct(qc.shape, F32)),
        in_specs=[q_spec, kv_spec, kv_spec], out_specs=(q_spec, q_spec),
        compiler_params=_params("parallel", "parallel", "arbitrary"),
    )(qc, kc, vp)


def _attn_bwd(qc, kc, vp, o, lse, do, seq):
    tokens = qc.shape[0]
    tq = tk = _tile(seq, (256, 128))
    nq = seq // tq
    reps = tk // HEAD_SLAB

    def body(q_ref, k_ref, v_ref, o_ref, lse_ref, do_ref, dq_ref, dk_ref, dv_ref, delta_ref):
        delta = jnp.sum(do_ref[...].astype(F32) * o_ref[...].astype(F32), axis=-1, keepdims=True)
        delta_ref[...] = jnp.broadcast_to(delta, delta_ref.shape)
        dq_ref[...] = jnp.zeros(dq_ref.shape, F32)

        def key_tile(j, _):
            krows = pl.ds(pl.multiple_of(j * tk, tk), tk)
            k, v = k_ref[krows, :], v_ref[krows, :]

            def query_tile(i, carry):
                dk, dv = carry
                qrows = pl.ds(pl.multiple_of(i * tq, tq), tq)
                q, dov = q_ref[qrows, :], do_ref[qrows, :]
                s = lax.dot_general(q, k, _DIMS["nt"], preferred_element_type=F32) * ATTN_SCALE
                s = _causal(s, i, j, tq, tk)
                p = jnp.exp(s - jnp.tile(lse_ref[qrows, :], (1, reps)))
                dp = lax.dot_general(dov, v, _DIMS["nt"], preferred_element_type=F32)
                ds = (p * (dp - jnp.tile(delta_ref[qrows, :], (1, reps))) * ATTN_SCALE).astype(BF16)
                dv = dv + lax.dot_general(p.astype(BF16), dov, _DIMS["tn"], preferred_element_type=F32)
                dk = dk + lax.dot_general(ds, q, _DIMS["tn"], preferred_element_type=F32)
                dq_ref[qrows, :] += jnp.dot(ds, k, preferred_element_type=F32)
                return dk, dv

            zero = jnp.zeros((tk, HEAD_SLAB), F32)
            dk, dv = lax.fori_loop(j, nq, query_tile, (zero, zero))
            dk_ref[krows, :] = dk
            dv_ref[krows, :] = dv
            return 0

        lax.fori_loop(0, nq, key_tile, 0)

    spec = pl.BlockSpec((seq, HEAD_SLAB), lambda b, h: (b, h))
    out = jax.ShapeDtypeStruct(qc.shape, F32)
    return pl.pallas_call(
        body, name="attn_bwd", grid=(tokens // seq, N_HEADS),
        out_shape=(out, out, out), in_specs=[spec] * 6, out_specs=(spec, spec, spec),
        scratch_shapes=[pltpu.VMEM((seq, HEAD_SLAB), F32)],
        compiler_params=_params("parallel", "parallel"),
    )(qc, kc, vp, o, lse, do)


def _adamw(w, g, m, v, name):
    rows, cols = w.shape
    whole = rows * cols * 4 <= ADAMW_WHOLE_BYTES
    tr = rows if whole else _tile(rows, (256, 128, 64, 32, 16, 8))
    c1 = 1.0 - ADAM_B1 ** ADAM_STEP
    c2 = 1.0 - ADAM_B2 ** ADAM_STEP

    def body(w_ref, g_ref, m_ref, v_ref, d_ref, nm_ref, nv_ref):
        gv = g_ref[...]
        nm = ADAM_B1 * m_ref[...] + (1.0 - ADAM_B1) * gv
        nv = ADAM_B2 * v_ref[...] + (1.0 - ADAM_B2) * (gv * gv)
        d_ref[...] = -ADAM_LR * ((nm / c1) / (jnp.sqrt(nv / c2) + ADAM_EPS) + ADAM_WD * w_ref[...])
        nm_ref[...] = nm
        nv_ref[...] = nv

    spec = pl.BlockSpec((tr, cols), lambda i: (i, 0))
    out = jax.ShapeDtypeStruct(w.shape, F32)
    return pl.pallas_call(
        body, name=name, grid=(rows // tr,), out_shape=(out, out, out),
        in_specs=[spec] * 4, out_specs=(spec, spec, spec),
        compiler_params=_params("parallel"),
    )(w, g, m, v)


def _mod_cols(c_all, w_ada, b_cols):
    def body(c_ref, w_ref, b_ref, act_ref, mod_ref):
        cv = c_ref[...]
        act = cv * _sigmoid(cv)
        act_ref[...] = act
        mod_ref[...] = jnp.dot(act.astype(BF16), w_ref[...].astype(BF16),
                               preferred_element_type=F32) + b_ref[...]

    n = w_ada.shape[1]
    return pl.pallas_call(
        body, name="mod_cols",
        out_shape=(jax.ShapeDtypeStruct(c_all.shape, F32), jax.ShapeDtypeStruct((c_all.shape[0], n), F32)),
        compiler_params=pltpu.CompilerParams(vmem_limit_bytes=VMEM_LIMIT),
    )(c_all, w_ada, b_cols)


def _ada_grads(c_act, dmod_all, dmod_cols):
    def body(c_ref, d_ref, dc_ref, gw_ref, gb_ref):
        gw_ref[...] = lax.dot_general(c_ref[...].astype(BF16), dc_ref[...].astype(BF16), _DIMS["tn"],
                                      preferred_element_type=F32)
        gb_ref[...] = _colsum(d_ref[...])

    return pl.pallas_call(
        body, name="ada_grads",
        out_shape=(jax.ShapeDtypeStruct((c_act.shape[1], dmod_cols.shape[1]), F32),
                   jax.ShapeDtypeStruct((1, dmod_all.shape[1]), F32)),
        compiler_params=pltpu.CompilerParams(vmem_limit_bytes=VMEM_LIMIT),
    )(c_act, dmod_all, dmod_cols)


def _loss_grad(y, target, seq):
    def fn(rows, bats, vecs):
        err = rows[0] - rows[1]
        return [err * (1.0 / D_MODEL)], [], [_colsum(err * err)]
    return _rowmap("loss", fn, seq, [y, target], row_outs=[(D_MODEL, F32)], vec_outs=[D_MODEL])


def _pad_rows(a, granule):
    pad = (-a.shape[-2]) % granule
    if pad:
        widths = [(0, 0)] * (a.ndim - 2) + [(0, pad), (0, 0)]
        a = jnp.pad(a, widths)
    return a


def _flat_rows(a):
    flat = a.reshape(-1)
    pad = (-flat.shape[0]) % (LANES * SUBLANES)
    if pad:
        flat = jnp.pad(flat, (0, pad))
    return flat.reshape(-1, LANES)


def _gather_weights(shards):
    parts = [_flat_rows(shards[n].astype(BF16)) for n in BIG]
    packed = _pad_rows(jnp.concatenate(parts, axis=0), PACK_ROWS)
    gathered = _all_gather(packed, "gather_weights")
    out, row = {}, 0
    for n, part in zip(BIG, parts):
        k, cols = shards[n].shape
        block = gathered[:, row:row + part.shape[0]].reshape(N_DEV, -1)[:, :k * cols].reshape(N_DEV, k, cols)
        row += part.shape[0]
        if n in ROW_SHARDED:
            out[n] = block.reshape(N_DEV * k, cols)
        else:
            out[n] = block.transpose(1, 0, 2).reshape(k, N_DEV * cols)
    return out


def _scatter_grads(grads, shards):
    parts = []
    for n in BIG:
        k, cols = shards[n].shape
        g = grads[n]
        if n in ROW_SHARDED:
            blocks = g.reshape(N_DEV, k * cols)
        else:
            blocks = g.reshape(k, N_DEV, cols).transpose(1, 0, 2).reshape(N_DEV, k * cols)
        pad = (-k * cols) % (LANES * SUBLANES)
        if pad:
            blocks = jnp.pad(blocks, ((0, 0), (0, pad)))
        parts.append(blocks.astype(BF16).reshape(N_DEV, -1, LANES))
    packed = _pad_rows(jnp.concatenate(parts, axis=1), PACK_ROWS)
    summed = _sum_blocks(_all_to_all(packed, "scatter_grads"), "sum_grads")
    out, row = {}, 0
    for n, part in zip(BIG, parts):
        k, cols = shards[n].shape
        out[n] = summed[row:row + part.shape[1]].reshape(-1)[:k * cols].reshape(k, cols)
        row += part.shape[1]
    return out


def _pack_small(vals):
    return jnp.concatenate([_flat_rows(v.astype(F32)) for v in vals], axis=0)


def _unpack_small(packed, like):
    out, row = [], 0
    for v in like:
        rows = _flat_rows(v).shape[0]
        out.append(packed[row:row + rows].reshape(-1)[:v.size].reshape(v.shape))
        row += rows
    return out


def _lanes128(*parts):
    out = jnp.zeros((HEAD_SLAB,), F32)
    for off, v in parts:
        out = lax.dynamic_update_slice(out, v.reshape(-1).astype(F32), (off,))
    return out.reshape(1, HEAD_SLAB)


def _step(x, c, positions, w, m, v, loss_target):
    nseq, seq, _ = x.shape
    tokens = nseq * seq
    me = _index(_my_pos())
    strip = lambda d: {n: (a[0] if a.ndim > 2 else a) for n, a in d.items()}
    shapes = {n: a.shape for n, a in w.items()}
    w, m, v = strip(w), strip(m), strip(v)

    c_all = _all_gather(c.reshape(-1, LANES), "gather_c").reshape(N_DEV * nseq, D_MODEL)
    n_ada = w["w_ada"].shape[1]
    b_cols = lax.dynamic_slice(w["b_ada"], (0, me * n_ada), (1, n_ada))
    c_act, mod_cols = _mod_cols(c_all, w["w_ada"], b_cols)
    mod_all = _all_gather(mod_cols, "gather_mod")
    mod = lax.dynamic_slice(mod_all, (0, me * nseq, 0), (N_DEV, nseq, n_ada))
    mod = mod.transpose(1, 0, 2).reshape(nseq, 3, 3, 1, D_MODEL)

    full = _gather_weights({n: w[n] for n in BIG})

    w_in = full["w_in"]
    zeros = lambda cols: jnp.zeros((D_MODEL, cols), BF16)
    w_p = w_in[:, :512]
    w_a = jnp.concatenate([w_in[:, 512:1152], zeros(QK_NOPE), w_in[:, 1152:1184], zeros(32)], axis=1)
    w_g = w_in[:, 1184:]
    wq_pad = jnp.pad(full["w_q_up"].reshape(Q_LORA, N_HEADS, 96), ((0, 0), (0, 0), (0, 32))).reshape(Q_LORA, -1)
    wmla_pad = jnp.pad(full["w_mla_proj"].reshape(N_HEADS, 64, D_MODEL), ((0, 0), (64, 0), (0, 0))).reshape(-1, D_MODEL)
    g_q = _lanes128((0, w["q_norm_nope"]), (QK_NOPE, w["q_norm_rope"]))
    g_kn = _lanes128((0, w["k_norm_nope"]))
    g_kr = _lanes128((QK_NOPE, w["k_norm_rope"]))
    freq = ROPE_THETA ** (-jnp.arange(0, QK_ROPE, 2, dtype=F32) / QK_ROPE)
    inv_freq = _lanes128((QK_NOPE, jnp.concatenate([freq, freq])))
    pos = positions.reshape(tokens, 1).astype(F32)

    def sub(k, gamma, coef, w_in_, w_out_):
        return dict(gamma=w[gamma], shift=mod[:, k, 0], scale=mod[:, k, 1], gate=coef * mod[:, k, 2],
                    w_in=w_in_, w_out=w_out_)
    p1 = sub(0, "norm_ffn1", 0.5, full["w_ffn1_in"], full["w_ffn1_out"])
    pm = sub(1, "norm_mix", 1.0, None, full["w_out"])
    p2 = sub(2, "norm_ffn2", 0.5, full["w_ffn2_in"], full["w_ffn2_out"])

    x0 = x.reshape(tokens, D_MODEL)
    x1, saved1 = _ffn_fwd(x0, p1, seq, "ffn1")

    h2 = _norm_mod_fwd(x1, pm["gamma"], pm["shift"], pm["scale"], seq, "mix_norm")
    tm_tok = _tile(tokens, (1024, 512, 256))
    z_a = _mm(h2, w_a, "nn", "mix_in_a", tm=tm_tok, tn=256)
    z_p = _mm(h2, w_p, "nn", "mix_in_p", tm=tm_tok)
    z_g = _mm(h2, w_g, "nn", "mix_in_g", tm=tm_tok)
    pooled, pg, ps = _pool_fwd(z_p, w["pool_grp"], w["pool_scale"], seq)
    br_pool = _mm(ps, full["w_pool_proj"], "nn", "pool_proj", tm=tm_tok)
    qn, kvn = _latent_norm_fwd(z_a, w["q_a_norm"], w["kv_a_norm"], seq)
    qp = _mm(qn, wq_pad, "nn", "q_up", tm=tm_tok)
    kv = _mm(kvn, full["w_kv_up"], "nn", "kv_up", tm=tm_tok)
    qc, kc, vp = _qk_prep_fwd(qp, kv, z_a, pos, g_q, g_kn, g_kr, inv_freq, seq)
    attn, lse = _attn_fwd(qc, kc, vp, seq)
    br_mla = _mm(attn, wmla_pad, "nn", "mla_proj", tm=tm_tok)

    def merge(rows, bats, vecs):
        zg, bp, bm = rows
        return [_sigmoid(zg[:, :D_MODEL]) * bp + _sigmoid(zg[:, D_MODEL:]) * bm], [], []
    merged = _rowmap("merge", merge, seq, [z_g, br_pool, br_mla], row_outs=[(D_MODEL, BF16)])[0]
    x2, o_mix = _mm(merged, pm["w_out"], "nn", "mix_out", res=x1, gate=pm["gate"], seq=seq,
                    tm=_tile(seq, (512, 256, 128)))

    x3, saved2 = _ffn_fwd(x2, p2, seq, "ffn2")
    dy, sq_err = _loss_grad(x3, loss_target.reshape(tokens, D_MODEL), seq)
    loss = lax.psum(0.5 * jnp.sum(sq_err) * (1.0 / D_MODEL), AXES)

    grads = {}
    dx2, grads["w_ffn2_in"], grads["w_ffn2_out"], dg_ffn2, dsh2, dsc2, dgate2 = _ffn_bwd(dy, saved2, p2, seq, "ffn2")

    def mix_pre(rows, bats, vecs):
        (dv, ov), (gate,) = rows, bats
        return [gate * dv], [_colsum(dv * ov.astype(F32))], []
    do_mix, dgate_m = _rowmap("mix_bwd_pre", mix_pre, seq, [dx2, o_mix], [pm["gate"]],
                              row_outs=[(D_MODEL, BF16)], bat_outs=[D_MODEL])
    dmerged = _mm(do_mix, pm["w_out"], "nt", "mix_bwd_dmerged", tm=tm_tok)
    grads["w_out"] = _mm(merged, do_mix, "tn", "mix_bwd_wout", tm=512, tn=512)

    def merge_bwd(rows, bats, vecs):
        dmv, zg, bp, bm = rows
        s_p, s_m = _sigmoid(zg[:, :D_MODEL]), _sigmoid(zg[:, D_MODEL:])
        dzg = jnp.concatenate([dmv * bp * s_p * (1.0 - s_p), dmv * bm * s_m * (1.0 - s_m)], axis=1)
        return [dmv * s_p, dmv * s_m, dzg], [], []
    dbr_pool, dbr_mla, dz_g = _rowmap("merge_bwd", merge_bwd, seq, [dmerged, z_g, br_pool, br_mla],
                                      row_outs=[(D_MODEL, BF16), (D_MODEL, BF16), (2 * D_MODEL, BF16)])

    grads["w_pool_proj"] = _mm(ps, dbr_pool, "tn", "pool_bwd_wproj", tm=512, tn=512)
    dps = _mm(dbr_pool, full["w_pool_proj"], "nt", "pool_bwd_dps", tm=tm_tok)
    dz_p, dgrp, dpool_scale = _pool_bwd(dps, pooled, pg, w["pool_grp"], w["pool_scale"], seq)

    dwmla_pad = _mm(attn, dbr_mla, "tn", "mla_bwd_wproj", tm=512, tn=512)
    grads["w_mla_proj"] = dwmla_pad.reshape(N_HEADS, HEAD_SLAB, D_MODEL)[:, 64:].reshape(-1, D_MODEL)
    d_attn = _mm(dbr_mla, wmla_pad, "nt", "mla_bwd_dattn", out_dtype=BF16, tm=tm_tok)
    dqc, dkc, dvp = _attn_bwd(qc, kc, vp, attn, lse, d_attn, seq)
    dqp, dkv, dkr, dg_q, dg_kn, dg_kr = _qk_prep_bwd(dqc, dkc, dvp, qp, kv, z_a, pos, g_q, g_kn, g_kr, inv_freq, seq)
    dwq_pad = _mm(qn, dqp, "tn", "q_up_bwd_w", tm=Q_LORA, tn=512)
    grads["w_q_up"] = dwq_pad.reshape(Q_LORA, N_HEADS, HEAD_SLAB)[:, :, :96].reshape(Q_LORA, -1)
    grads["w_kv_up"] = _mm(kvn, dkv, "tn", "kv_up_bwd_w", tm=KV_LORA, tn=512)
    dqn = _mm(dqp, wq_pad, "nt", "q_up_bwd_x", tm=tm_tok, tn=Q_LORA)
    dkvn = _mm(dkv, full["w_kv_up"], "nt", "kv_up_bwd_x", tm=tm_tok, tn=KV_LORA)
    dz_a, dg_qa, dg_kva = _latent_norm_bwd(dqn, dkvn, dkr, z_a, w["q_a_norm"], w["kv_a_norm"], seq)

    dw_a = _mm(h2, dz_a, "tn", "mix_in_bwd_wa", tm=D_MODEL, tn=256)
    dw_p = _mm(h2, dz_p, "tn", "mix_in_bwd_wp", tm=D_MODEL, tn=512)
    dw_g = _mm(h2, dz_g, "tn", "mix_in_bwd_wg", tm=D_MODEL, tn=512)
    grads["w_in"] = jnp.concatenate([dw_p, dw_a[:, :640], dw_a[:, 704:736], dw_g], axis=1)
    dh2 = _mm(dz_a, w_a, "nt", "mix_in_bwd_xa", tm=tm_tok)
    dh2 = _mm(dz_p, w_p, "nt", "mix_in_bwd_xp", tm=tm_tok, add=dh2)
    dh2 = _mm(dz_g, w_g, "nt", "mix_in_bwd_xg", tm=tm_tok, add=dh2)
    dx1, dsh_m, dsc_m, dg_mix = _norm_mod_bwd(dh2, x1, dx2, pm["gamma"], pm["scale"], seq, "mix_bwd_norm")

    dx0, grads["w_ffn1_in"], grads["w_ffn1_out"], dg_ffn1, dsh1, dsc1, dgate1 = _ffn_bwd(dx1, saved1, p1, seq, "ffn1")

    dmod = jnp.stack([jnp.stack([dsh1, dsc1, 0.5 * dgate1], axis=1),
                      jnp.stack([dsh_m, dsc_m, dgate_m], axis=1),
                      jnp.stack([dsh2, dsc2, 0.5 * dgate2], axis=1)], axis=1)
    dmod_all = _all_gather(dmod.reshape(-1, LANES), "gather_dmod").reshape(N_DEV * nseq, 9 * D_MODEL)
    dmod_cols = lax.dynamic_slice(dmod_all, (0, me * n_ada), (N_DEV * nseq, n_ada))
    g_w_ada, g_b_ada = _ada_grads(c_act, dmod_all, dmod_cols)

    sharded = _scatter_grads(grads, {n: w[n] for n in BIG})
    small_local = [dg_ffn1.reshape(w["norm_ffn1"].shape), dg_mix.reshape(w["norm_mix"].shape),
                   dg_ffn2.reshape(w["norm_ffn2"].shape), dgrp, dpool_scale, dg_qa, dg_kva,
                   dg_q[:, :QK_NOPE], dg_q[:, QK_NOPE:QK_NOPE + QK_ROPE], dg_kn[:, :QK_NOPE],
                   dg_kr[:, QK_NOPE:QK_NOPE + QK_ROPE]]
    small_sum = _sum_blocks(_all_gather(_pack_small(small_local), "gather_small"), "sum_small")
    small = dict(zip(SMALL, _unpack_small(small_sum, [w[n] for n in SMALL])))

    grad_w = dict(sharded, **small)
    grad_w["w_ada"], grad_w["b_ada"] = g_w_ada, g_b_ada

    delta, new_m, new_v = {}, {}, {}
    for n in ("w_ada",) + BIG:
        delta[n], new_m[n], new_v[n] = _adamw(w[n], grad_w[n], m[n], v[n], f"adamw_{n}")
    rep = ("b_ada",) + SMALL
    d_s, m_s, v_s = _adamw(_pack_small([w[n] for n in rep]), _pack_small([grad_w[n] for n in rep]),
                           _pack_small([m[n] for n in rep]), _pack_small([v[n] for n in rep]), "adamw_small")
    like = [w[n] for n in rep]
    for dst, packed in ((delta, d_s), (new_m, m_s), (new_v, v_s)):
        dst.update(zip(rep, _unpack_small(packed, like)))

    lead = lambda d: [d[n].reshape(shapes[n]) for n in WEIGHTS]
    return (loss, dx0.reshape(x.shape), *lead(grad_w), *lead(delta), *lead(new_m), *lead(new_v))


def kernel(x, c, positions, w_ada, b_ada, norm_ffn1, w_ffn1_in, w_ffn1_out, norm_mix, w_in, pool_grp, pool_scale, w_pool_proj, q_a_norm, w_q_up, kv_a_norm, w_kv_up, q_norm_nope, q_norm_rope, k_norm_nope, k_norm_rope, w_mla_proj, w_out, norm_ffn2, w_ffn2_in, w_ffn2_out, loss_target, m_w_ada, m_b_ada, m_norm_ffn1, m_w_ffn1_in, m_w_ffn1_out, m_norm_mix, m_w_in, m_pool_grp, m_pool_scale, m_w_pool_proj, m_q_a_norm, m_w_q_up, m_kv_a_norm, m_w_kv_up, m_q_norm_nope, m_q_norm_rope, m_k_norm_nope, m_k_norm_rope, m_w_mla_proj, m_w_out, m_norm_ffn2, m_w_ffn2_in, m_w_ffn2_out, v_w_ada, v_b_ada, v_norm_ffn1, v_w_ffn1_in, v_w_ffn1_out, v_norm_mix, v_w_in, v_pool_grp, v_pool_scale, v_w_pool_proj, v_q_a_norm, v_w_q_up, v_kv_a_norm, v_w_kv_up, v_q_norm_nope, v_q_norm_rope, v_k_norm_nope, v_k_norm_rope, v_w_mla_proj, v_w_out, v_norm_ffn2, v_w_ffn2_in, v_w_ffn2_out):
    w = dict(w_ada=w_ada, b_ada=b_ada, norm_ffn1=norm_ffn1, w_ffn1_in=w_ffn1_in, w_ffn1_out=w_ffn1_out, norm_mix=norm_mix, w_in=w_in, pool_grp=pool_grp, pool_scale=pool_scale, w_pool_proj=w_pool_proj, q_a_norm=q_a_norm, w_q_up=w_q_up, kv_a_norm=kv_a_norm, w_kv_up=w_kv_up, q_norm_nope=q_norm_nope, q_norm_rope=q_norm_rope, k_norm_nope=k_norm_nope, k_norm_rope=k_norm_rope, w_mla_proj=w_mla_proj, w_out=w_out, norm_ffn2=norm_ffn2, w_ffn2_in=w_ffn2_in, w_ffn2_out=w_ffn2_out)
    m = dict(w_ada=m_w_ada, b_ada=m_b_ada, norm_ffn1=m_norm_ffn1, w_ffn1_in=m_w_ffn1_in, w_ffn1_out=m_w_ffn1_out, norm_mix=m_norm_mix, w_in=m_w_in, pool_grp=m_pool_grp, pool_scale=m_pool_scale, w_pool_proj=m_w_pool_proj, q_a_norm=m_q_a_norm, w_q_up=m_w_q_up, kv_a_norm=m_kv_a_norm, w_kv_up=m_w_kv_up, q_norm_nope=m_q_norm_nope, q_norm_rope=m_q_norm_rope, k_norm_nope=m_k_norm_nope, k_norm_rope=m_k_norm_rope, w_mla_proj=m_w_mla_proj, w_out=m_w_out, norm_ffn2=m_norm_ffn2, w_ffn2_in=m_w_ffn2_in, w_ffn2_out=m_w_ffn2_out)
    v = dict(w_ada=v_w_ada, b_ada=v_b_ada, norm_ffn1=v_norm_ffn1, w_ffn1_in=v_w_ffn1_in, w_ffn1_out=v_w_ffn1_out, norm_mix=v_norm_mix, w_in=v_w_in, pool_grp=v_pool_grp, pool_scale=v_pool_scale, w_pool_proj=v_w_pool_proj, q_a_norm=v_q_a_norm, w_q_up=v_w_q_up, kv_a_norm=v_kv_a_norm, w_kv_up=v_w_kv_up, q_norm_nope=v_q_norm_nope, q_norm_rope=v_q_norm_rope, k_norm_nope=v_k_norm_nope, k_norm_rope=v_k_norm_rope, w_mla_proj=v_w_mla_proj, w_out=v_w_out, norm_ffn2=v_norm_ffn2, w_ffn2_in=v_w_ffn2_in, w_ffn2_out=v_w_ffn2_out)
    return _step(x, c, positions, w, m, v, loss_target)
```

```python
import functools
import math

import jax
import jax.numpy as jnp
from jax import lax
from jax.experimental import pallas as pl
from jax.experimental.pallas import tpu as pltpu

F32 = jnp.float32
BF16 = jnp.bfloat16
MESH = pl.DeviceIdType.MESH
AXES = ("x", "y", "c")
N_DEV = 8

D_MODEL = 1024
D_FF = 2816
N_HEADS = 8
HEAD_SLAB = 128
QK_NOPE = 64
QK_ROPE = 32
POOL_WIDTH = 512
POOL_GROUPS = 4
POOL_GROUP_DIM = 128
Q_LORA = 384
KV_LORA = 256
ROPE_THETA = 10000.0
ATTN_SCALE = 1.0 / math.sqrt(QK_NOPE + QK_ROPE)
NORM_EPS = 1e-6
ADAM_LR, ADAM_B1, ADAM_B2, ADAM_EPS, ADAM_WD, ADAM_STEP = 0.001, 0.9, 0.999, 1e-08, 0.01, 10

LANES = 128
SUBLANES = 8
VMEM_LIMIT = 52 * 1024 * 1024
ADAMW_WHOLE_BYTES = 3 << 19

BIG = ("w_ffn1_in", "w_ffn1_out", "w_in", "w_pool_proj", "w_q_up", "w_kv_up",
       "w_mla_proj", "w_out", "w_ffn2_in", "w_ffn2_out")
ROW_SHARDED = ("w_ffn1_out", "w_out", "w_ffn2_out")
SMALL = ("norm_ffn1", "norm_mix", "norm_ffn2", "pool_grp", "pool_scale", "q_a_norm",
         "kv_a_norm", "q_norm_nope", "q_norm_rope", "k_norm_nope", "k_norm_rope")
WEIGHTS = ("w_ada", "b_ada", "norm_ffn1", "w_ffn1_in", "w_ffn1_out", "norm_mix", "w_in",
           "pool_grp", "pool_scale", "w_pool_proj", "q_a_norm", "w_q_up", "kv_a_norm",
           "w_kv_up", "q_norm_nope", "q_norm_rope", "k_norm_nope", "k_norm_rope",
           "w_mla_proj", "w_out", "norm_ffn2", "w_ffn2_in", "w_ffn2_out")


def _params(*sem):
    return pltpu.CompilerParams(dimension_semantics=sem, vmem_limit_bytes=VMEM_LIMIT)


def _tile(n, cands):
    for c in cands:
        if n % c == 0:
            return c
    return n


def _my_pos():
    return lax.axis_index("x"), lax.axis_index("y"), lax.axis_index("c")


def _flip(pos, k):
    x, y, c = pos
    fx, fy, fc = (k >> 2) & 1, (k >> 1) & 1, k & 1
    return ((1 - x) if fx else x, (1 - y) if fy else y, (1 - c) if fc else c)


def _index(pos):
    x, y, c = pos
    return 4 * x + 2 * y + c


def _exchange(arrays, name, scatter=False):
    n = len(arrays)

    def body(*refs):
        ins, outs = refs[:n], refs[n:2 * n]
        send_sems, recv_sems, local_sems = refs[2 * n:]
        me = _my_pos()
        mine, sends = [], []
        for a in range(n):
            own = ins[a].at[_index(me)] if scatter else ins[a]
            cp = pltpu.make_async_copy(own, outs[a].at[_index(me)], local_sems.at[a])
            cp.start()
            mine.append(cp)
        for k in range(1, N_DEV):
            peer = _flip(me, k)
            for a in range(n):
                cp = pltpu.make_async_remote_copy(
                    src_ref=ins[a].at[_index(peer)] if scatter else ins[a],
                    dst_ref=outs[a].at[_index(me)],
                    send_sem=send_sems.at[a, k - 1], recv_sem=recv_sems.at[a, k - 1],
                    device_id=peer, device_id_type=MESH)
                cp.start()
                sends.append(cp)
        for k in range(1, N_DEV):
            peer = _flip(me, k)
            for a in range(n):
                pltpu.make_async_remote_copy(
                    src_ref=ins[a].at[_index(me)] if scatter else ins[a],
                    dst_ref=outs[a].at[_index(peer)],
                    send_sem=send_sems.at[a, k - 1], recv_sem=recv_sems.at[a, k - 1],
                    device_id=peer, device_id_type=MESH).wait_recv()
        for cp in sends:
            cp.wait_send()
        for cp in mine:
            cp.wait()

    shape = lambda x: x.shape if scatter else (N_DEV,) + x.shape
    return pl.pallas_call(
        body, name=name,
        out_shape=tuple(jax.ShapeDtypeStruct(shape(x), x.dtype) for x in arrays),
        in_specs=[pl.BlockSpec(memory_space=pl.ANY)] * n,
        out_specs=tuple(pl.BlockSpec(memory_space=pl.ANY) for _ in arrays),
        scratch_shapes=[pltpu.SemaphoreType.DMA((n, N_DEV - 1)),
                        pltpu.SemaphoreType.DMA((n, N_DEV - 1)),
                        pltpu.SemaphoreType.DMA((n,))],
    )(*arrays)


def _all_gather(x, name):
    return _exchange([x], name)[0]


def _sum_blocks(x, name):
    n, rows, cols = x.shape
    tr = _tile(rows, (512, 256, 128, 64, 32, 16, 8))

    def body(x_ref, o_ref):
        acc = x_ref[0].astype(F32)
        for d in range(1, n):
            acc = acc + x_ref[d].astype(F32)
        o_ref[...] = acc

    return pl.pallas_call(
        body, name=name,
        out_shape=jax.ShapeDtypeStruct((rows, cols), F32),
        grid=(rows // tr,),
        in_specs=[pl.BlockSpec((n, tr, cols), lambda i: (0, i, 0))],
        out_specs=pl.BlockSpec((tr, cols), lambda i: (i, 0)),
        compiler_params=_params("parallel"),
    )(x)


_DIMS = {"nn": (((1,), (0,)), ((), ())), "nt": (((1,), (1,)), ((), ())), "tn": (((0,), (0,)), ((), ()))}


def _mm(a, b, mode, name, out_dtype=F32, tm=None, tn=None, add=None, res=None, gate=None, seq=None):
    if mode == "tn":
        kdim, m = a.shape
    else:
        m, kdim = a.shape
    n = b.shape[0] if mode == "nt" else b.shape[1]
    tm = tm or _tile(m, (512, 256, 128))
    tn = tn or _tile(n, (512, 256, 128))
    if res is not None:
        assert seq % tm == 0
    dims = _DIMS[mode]

    def body(*refs):
        a_ref, b_ref = refs[0], refs[1]
        acc = lax.dot_general(a_ref[...].astype(BF16), b_ref[...].astype(BF16), dims,
                              preferred_element_type=F32)
        if add is not None:
            refs[3][...] = (acc + refs[2][...]).astype(out_dtype)
        elif res is not None:
            res_ref, gate_ref, o_ref, p_ref = refs[2:]
            o_ref[...] = res_ref[...] + gate_ref[0] * acc
            p_ref[...] = acc.astype(BF16)
        else:
            refs[2][...] = acc.astype(out_dtype)

    a_spec = (pl.BlockSpec((kdim, tm), lambda i, j: (0, i)) if mode == "tn"
              else pl.BlockSpec((tm, kdim), lambda i, j: (i, 0)))
    b_spec = (pl.BlockSpec((tn, kdim), lambda i, j: (j, 0)) if mode == "nt"
              else pl.BlockSpec((kdim, tn), lambda i, j: (0, j)))
    o_spec = pl.BlockSpec((tm, tn), lambda i, j: (i, j))
    in_specs, args = [a_spec, b_spec], [a, b]
    out_shape, out_specs = jax.ShapeDtypeStruct((m, n), out_dtype), o_spec
    if add is not None:
        in_specs.append(o_spec)
        args.append(add)
    if res is not None:
        per_seq = seq // tm
        in_specs += [o_spec, pl.BlockSpec((1, 1, tn), lambda i, j: (i // per_seq, 0, j))]
        args += [res, gate]
        out_shape = (jax.ShapeDtypeStruct((m, n), F32), jax.ShapeDtypeStruct((m, n), BF16))
        out_specs = (o_spec, o_spec)
    return pl.pallas_call(
        body, name=name, out_shape=out_shape, grid=(m // tm, n // tn),
        in_specs=in_specs, out_specs=out_specs,
        compiler_params=_params("parallel", "parallel"),
    )(*args)


def _rowmap(name, fn, seq, rows, bats=(), vecs=(), row_outs=(), bat_outs=(), vec_outs=(), ts=None):
    rows = [r if isinstance(r, tuple) else (r, r.shape[1], 0) for r in rows]
    tokens = rows[0][0].shape[0]
    nseq = tokens // seq
    ts = ts or _tile(seq, (256, 128, 64, 32, 16, 8))
    nt = seq // ts
    n_r, n_b, n_v = len(rows), len(bats), len(vecs)
    n_ro, n_bo = len(row_outs), len(bat_outs)

    def accumulate(ref, val, first):
        @pl.when(first)
        def _():
            ref[...] = val.reshape(ref.shape)

        @pl.when(jnp.logical_not(first))
        def _():
            ref[...] += val.reshape(ref.shape)

    def body(*refs):
        ins, outs = refs[:n_r + n_b + n_v], refs[n_r + n_b + n_v:]
        r_vals = [r[...] for r in ins[:n_r]]
        b_vals = [r[0] for r in ins[n_r:n_r + n_b]]
        v_vals = [r[...] for r in ins[n_r + n_b:]]
        ro, bo, vo = fn(r_vals, b_vals, v_vals)
        for ref, val in zip(outs[:n_ro], ro):
            ref[...] = val.astype(ref.dtype)
        b, i = pl.program_id(0), pl.program_id(1)
        for ref, val in zip(outs[n_ro:n_ro + n_bo], bo):
            accumulate(ref, val, i == 0)
        for ref, val in zip(outs[n_ro + n_bo:], vo):
            accumulate(ref, val, jnp.logical_and(i == 0, b == 0))

    in_specs = [pl.BlockSpec((ts, w), functools.partial(lambda b, i, cb: (b * nt + i, cb), cb=cb))
                for _, w, cb in rows]
    in_specs += [pl.BlockSpec((1, 1, v.shape[2]), lambda b, i: (b, 0, 0)) for v in bats]
    in_specs += [pl.BlockSpec((1, v.shape[1]), lambda b, i: (0, 0)) for v in vecs]
    out_shape = [jax.ShapeDtypeStruct((tokens, f), dt) for f, dt in row_outs]
    out_specs = [pl.BlockSpec((ts, f), lambda b, i: (b * nt + i, 0)) for f, _ in row_outs]
    out_shape += [jax.ShapeDtypeStruct((nseq, 1, f), F32) for f in bat_outs]
    out_specs += [pl.BlockSpec((1, 1, f), lambda b, i: (b, 0, 0)) for f in bat_outs]
    out_shape += [jax.ShapeDtypeStruct((1, f), F32) for f in vec_outs]
    out_specs += [pl.BlockSpec((1, f), lambda b, i: (0, 0)) for f in vec_outs]
    return pl.pallas_call(
        body, name=name, out_shape=tuple(out_shape), grid=(nseq, nt),
        in_specs=in_specs, out_specs=tuple(out_specs),
        compiler_params=_params("arbitrary", "arbitrary"),
    )(*([r[0] for r in rows] + list(bats) + list(vecs)))


def _colsum(v):
    return jnp.sum(v, axis=0, keepdims=True)


def _rstd(x, width=None):
    width = width or x.shape[-1]
    return lax.rsqrt(jnp.sum(x * x, axis=-1, keepdims=True) * (1.0 / width) + NORM_EPS)


def _norm_bwd(dy, x, r, g, width=None):
    width = width or x.shape[-1]
    xhat = x * r
    dxhat = dy * g
    dx = r * (dxhat - xhat * (jnp.sum(dxhat * xhat, axis=-1, keepdims=True) * (1.0 / width)))
    return dx, dy * xhat


def _sigmoid(x):
    return 1.0 / (1.0 + jnp.exp(-x))


def _norm_mod_fwd(x, gamma, shift, scale, seq, name):
    def fn(rows, bats, vecs):
        (xv,), (sh, sc), (g,) = rows, bats, vecs
        return [xv * _rstd(xv) * g * (1.0 + sc) + sh], [], []
    return _rowmap(name, fn, seq, [x], [shift, scale], [gamma], row_outs=[(D_MODEL, BF16)])[0]


def _norm_mod_bwd(dh, x, dres, gamma, scale, seq, name):
    def fn(rows, bats, vecs):
        (dhv, xv, dr), (sc,), (g,) = rows, bats, vecs
        r = _rstd(xv)
        dx, dg = _norm_bwd(dhv * (1.0 + sc), xv, r, g)
        return [dr + dx], [_colsum(dhv), _colsum(dhv * (xv * r * g))], [_colsum(dg)]
    return _rowmap(name, fn, seq, [dh, x, dres], [scale], [gamma], row_outs=[(D_MODEL, F32)],
                   bat_outs=[D_MODEL, D_MODEL], vec_outs=[D_MODEL])


def _ffn_fwd(x, p, seq, tag):
    tokens = x.shape[0]
    h = _norm_mod_fwd(x, p["gamma"], p["shift"], p["scale"], seq, f"{tag}_norm")
    gu = _mm(h, p["w_in"], "nt", f"{tag}_in", out_dtype=BF16, tm=_tile(tokens, (2048, 1024, 512)), tn=512)

    def act(rows, bats, vecs):
        g, u = rows[0][:, :D_FF].astype(F32), rows[0][:, D_FF:].astype(F32)
        return [g * _sigmoid(g) * u], [], []
    a = _rowmap(f"{tag}_act", act, seq, [gu], row_outs=[(D_FF, BF16)])[0]
    x_new, f = _mm(a, p["w_out"], "nn", f"{tag}_out", res=x, gate=p["gate"], seq=seq,
                   tm=_tile(seq, (512, 256, 128)), tn=D_MODEL)
    return x_new, (x, h, gu, f)


def _ffn_bwd(dxo, saved, p, seq, tag):
    x, h, gu, f = saved
    tokens = x.shape[0]

    def pre(rows, bats, vecs):
        (dv, fv), (gate,) = rows, bats
        return [gate * dv], [_colsum(dv * fv.astype(F32))], []
    df, dgate = _rowmap(f"{tag}_bwd_pre", pre, seq, [dxo, f], [p["gate"]],
                        row_outs=[(D_MODEL, BF16)], bat_outs=[D_MODEL])
    da = _mm(df, p["w_out"], "nt", f"{tag}_bwd_da", tm=_tile(tokens, (512, 256)), tn=D_FF)

    def act_bwd(rows, bats, vecs):
        dav, guv = rows
        g, u = guv[:, :D_FF].astype(F32), guv[:, D_FF:].astype(F32)
        sg = _sigmoid(g)
        silu = g * sg
        dg = dav * u * (sg * (1.0 + g * (1.0 - sg)))
        return [silu * u, jnp.concatenate([dg, dav * silu], axis=1)], [], []
    a, dgu = _rowmap(f"{tag}_bwd_act", act_bwd, seq, [da, gu],
                     row_outs=[(D_FF, BF16), (2 * D_FF, BF16)])
    dw_out = _mm(a, df, "tn", f"{tag}_bwd_wout", out_dtype=BF16, tm=256, tn=D_MODEL)
    dwt_in = _mm(dgu, h, "tn", f"{tag}_bwd_win", out_dtype=BF16, tm=512, tn=D_MODEL)
    dh = _mm(dgu, p["w_in"], "nn", f"{tag}_bwd_dh", tm=_tile(tokens, (512, 256)), tn=D_MODEL)
    dx, dshift, dscale, dgamma = _norm_mod_bwd(dh, x, dxo, p["gamma"], p["scale"], seq, f"{tag}_bwd_norm")
    return dx, dwt_in, dw_out, dgamma, dshift, dscale, dgate


def _shift_rows(v, k, forward):
    n = v.shape[0]
    row = lax.broadcasted_iota(jnp.int32, v.shape, 0)
    if forward:
        return jnp.where(row >= k, pltpu.roll(v, k, 0), 0.0)
    return jnp.where(row < n - k, pltpu.roll(v, n - k, 0), 0.0)


def _window_sums(v, forward):
    out, s, k = [], v, 1
    for _ in range(POOL_GROUPS):
        s = s + _shift_rows(s, k, forward)
        out.append(s)
        k *= 2
    return out


def _by_group(vals, g):
    out = vals[-1]
    for idx in range(len(vals) - 2, -1, -1):
        out = jnp.where(g == idx, vals[idx], out)
    return out


def _inv_count(shape, g):
    t1 = lax.broadcasted_iota(jnp.int32, shape, 0) + 1
    window = _by_group([jnp.int32(2 ** (i + 1)) for i in range(POOL_GROUPS)], g)
    return 1.0 / jnp.minimum(t1, window).astype(F32)


def _pool_fwd(u, grp, scale, seq):
    tokens = u.shape[0]

    def body(u_ref, grp_ref, sc_ref, pooled_ref, pg_ref, ps_ref):
        g = pl.program_id(1)
        uv = u_ref[...]
        sums = _by_group(_window_sums(uv, True), g)
        pooled = (sums * _inv_count(uv.shape, g) - uv).astype(BF16)
        pg = jnp.dot(pooled, grp_ref[0].astype(BF16), preferred_element_type=F32)
        pooled_ref[...] = pooled
        pg_ref[...] = pg
        ps_ref[...] = (pg * sc_ref[...]).astype(BF16)

    blk = pl.BlockSpec((seq, POOL_GROUP_DIM), lambda b, g: (b, g))
    return pl.pallas_call(
        body, name="pool_fwd", grid=(tokens // seq, POOL_GROUPS),
        out_shape=(jax.ShapeDtypeStruct(u.shape, BF16), jax.ShapeDtypeStruct(u.shape, F32),
                   jax.ShapeDtypeStruct(u.shape, BF16)),
        in_specs=[blk, pl.BlockSpec((1, POOL_GROUP_DIM, POOL_GROUP_DIM), lambda b, g: (g, 0, 0)),
                  pl.BlockSpec((1, POOL_GROUP_DIM), lambda b, g: (0, g))],
        out_specs=(blk, blk, blk),
        compiler_params=_params("parallel", "parallel"),
    )(u, grp, scale)


def _pool_bwd(dps, pooled, pg, grp, scale, seq):
    tokens = dps.shape[0]

    def body(dps_ref, pooled_ref, pg_ref, grp_ref, sc_ref, du_ref, dgrp_ref, dsc_ref):
        g, b = pl.program_id(0), pl.program_id(1)
        dpsv = dps_ref[...]
        dpg = (dpsv * sc_ref[...]).astype(BF16)
        dsc = _colsum(dpsv * pg_ref[...])
        dgrp = lax.dot_general(pooled_ref[...], dpg, _DIMS["tn"], preferred_element_type=F32)

        @pl.when(b == 0)
        def _():
            dsc_ref[...] = dsc
            dgrp_ref[0] = dgrp

        @pl.when(b > 0)
        def _():
            dsc_ref[...] += dsc
            dgrp_ref[0] += dgrp

        dpool = lax.dot_general(dpg, grp_ref[0].astype(BF16), _DIMS["nt"], preferred_element_type=F32)
        sums = _by_group(_window_sums(dpool * _inv_count(dpool.shape, g), False), g)
        du_ref[...] = (sums - dpool).astype(BF16)

    blk = pl.BlockSpec((seq, POOL_GROUP_DIM), lambda g, b: (b, g))
    grp_spec = pl.BlockSpec((1, POOL_GROUP_DIM, POOL_GROUP_DIM), lambda g, b: (g, 0, 0))
    vec_spec = pl.BlockSpec((1, POOL_GROUP_DIM), lambda g, b: (0, g))
    return pl.pallas_call(
        body, name="pool_bwd", grid=(POOL_GROUPS, tokens // seq),
        out_shape=(jax.ShapeDtypeStruct(dps.shape, BF16), jax.ShapeDtypeStruct(grp.shape, F32),
                   jax.ShapeDtypeStruct(scale.shape, F32)),
        in_specs=[blk, blk, blk, grp_spec, vec_spec],
        out_specs=(blk, grp_spec, vec_spec),
        compiler_params=_params("arbitrary", "arbitrary"),
    )(dps, pooled, pg, grp, scale)


def _lane(shape):
    return lax.broadcasted_iota(jnp.int32, shape, len(shape) - 1)


def _rot(y):
    lane = _lane(y.shape)
    r = jnp.where(lane < QK_NOPE + QK_ROPE // 2,
                  -pltpu.roll(y, HEAD_SLAB - QK_ROPE // 2, 1), pltpu.roll(y, QK_ROPE // 2, 1))
    return jnp.where(jnp.logical_and(lane >= QK_NOPE, lane < QK_NOPE + QK_ROPE), r, 0.0)


def _part_rstd(x):
    sq = x * x
    nope = _lane(x.shape) < QK_NOPE
    s_nope = jnp.sum(jnp.where(nope, sq, 0.0), axis=-1, keepdims=True)
    s_rope = jnp.sum(sq, axis=-1, keepdims=True) - s_nope
    return jnp.where(nope, lax.rsqrt(s_nope * (1.0 / QK_NOPE) + NORM_EPS),
                     lax.rsqrt(s_rope * (1.0 / QK_ROPE) + NORM_EPS))


def _part_norm_bwd(dy, x, r, g):
    nope = _lane(x.shape) < QK_NOPE
    xhat = x * r
    dxhat = dy * g
    prod = dxhat * xhat
    m_nope = jnp.sum(jnp.where(nope, prod, 0.0), axis=-1, keepdims=True)
    m_rope = jnp.sum(prod, axis=-1, keepdims=True) - m_nope
    mean = jnp.where(nope, m_nope * (1.0 / QK_NOPE), m_rope * (1.0 / QK_ROPE))
    return r * (dxhat - xhat * mean), dy * xhat


def _latent_norm_fwd(z_a, g_q, g_kv, seq):
    def fn(rows, bats, vecs):
        q, kv = rows[0][:, :Q_LORA], rows[0][:, Q_LORA:Q_LORA + KV_LORA]
        return [q * _rstd(q) * vecs[0], kv * _rstd(kv) * vecs[1]], [], []
    return _rowmap("latent_norm", fn, seq, [z_a], vecs=[g_q, g_kv],
                   row_outs=[(Q_LORA, BF16), (KV_LORA, BF16)])


def _latent_norm_bwd(dqn, dkvn, dkr, z_a, g_q, g_kv, seq):
    def fn(rows, bats, vecs):
        dq, dkv, dkrv, z = rows
        q, kv = z[:, :Q_LORA], z[:, Q_LORA:Q_LORA + KV_LORA]
        dxq, dgq = _norm_bwd(dq, q, _rstd(q), vecs[0])
        dxkv, dgkv = _norm_bwd(dkv, kv, _rstd(kv), vecs[1])
        return [jnp.concatenate([dxq, dxkv, dkrv], axis=1)], [], [_colsum(dgq), _colsum(dgkv)]
    return _rowmap("latent_norm_bwd", fn, seq, [dqn, dkvn, dkr, z_a], vecs=[g_q, g_kv],
                   row_outs=[(Q_LORA + KV_LORA + HEAD_SLAB, BF16)], vec_outs=[Q_LORA, KV_LORA])


def _qk_prep_fwd(qp, kv, z_a, pos, g_q, g_kn, g_kr, inv_freq, seq):
    def fn(rows, bats, vecs):
        qv, kvv, kr, p = rows
        gq, gkn, gkr, invf = vecs
        ang = p * invf
        cos, sin = jnp.cos(ang), jnp.sin(ang)
        nope = _lane(kr.shape) < QK_NOPE
        krn = kr * _rstd(kr, QK_ROPE) * gkr
        krr = krn * cos + _rot(krn) * sin
        qs, ks, vs = [], [], []
        for h in range(N_HEADS):
            xq = qv[:, h * HEAD_SLAB:(h + 1) * HEAD_SLAB]
            y = xq * _part_rstd(xq) * gq
            qs.append(y * cos + _rot(y) * sin)
            xk = kvv[:, h * HEAD_SLAB:(h + 1) * HEAD_SLAB]
            kn = jnp.where(nope, xk, 0.0)
            ks.append(jnp.where(nope, kn * _rstd(kn, QK_NOPE) * gkn, krr))
            vs.append(jnp.where(nope, 0.0, xk))
        return [jnp.concatenate(v, axis=1) for v in (qs, ks, vs)], [], []
    width = N_HEADS * HEAD_SLAB
    return _rowmap("qk_prep", fn, seq, [qp, kv, (z_a, HEAD_SLAB, 5), pos], vecs=[g_q, g_kn, g_kr, inv_freq],
                   row_outs=[(width, BF16)] * 3, ts=_tile(seq, (128, 64, 32, 16, 8)))


def _qk_prep_bwd(dqc, dkc, dvp, qp, kv, z_a, pos, g_q, g_kn, g_kr, inv_freq, seq):
    def fn(rows, bats, vecs):
        dq, dk, dv, qv, kvv, kr, p = rows
        gq, gkn, gkr, invf = vecs
        ang = p * invf
        cos, sin = jnp.cos(ang), jnp.sin(ang)
        nope = _lane(kr.shape) < QK_NOPE
        dqs, dkvs = [], []
        dgq = jnp.zeros((1, HEAD_SLAB), F32)
        dgkn = jnp.zeros((1, HEAD_SLAB), F32)
        dkrr = jnp.zeros(kr.shape, F32)
        for h in range(N_HEADS):
            sl = slice(h * HEAD_SLAB, (h + 1) * HEAD_SLAB)
            dyr = dq[:, sl]
            dy = dyr * cos - _rot(dyr * sin)
            xq = qv[:, sl]
            dx, dg = _part_norm_bwd(dy, xq, _part_rstd(xq), gq)
            dqs.append(dx)
            dgq = dgq + _colsum(dg)
            dkh = dk[:, sl]
            dkrr = dkrr + jnp.where(nope, 0.0, dkh)
            kn = jnp.where(nope, kvv[:, sl], 0.0)
            dxk, dgk = _norm_bwd(jnp.where(nope, dkh, 0.0), kn, _rstd(kn, QK_NOPE), gkn, QK_NOPE)
            dgkn = dgkn + _colsum(dgk)
            dkvs.append(jnp.where(nope, dxk, dv[:, sl]))
        dkrn = dkrr * cos - _rot(dkrr * sin)
        dkr, dgkr = _norm_bwd(dkrn, kr, _rstd(kr, QK_ROPE), gkr, QK_ROPE)
        return ([jnp.concatenate(dqs, axis=1), jnp.concatenate(dkvs, axis=1), dkr], [],
                [dgq, dgkn, _colsum(dgkr)])
    width = N_HEADS * HEAD_SLAB
    return _rowmap("qk_prep_bwd", fn, seq, [dqc, dkc, dvp, qp, kv, (z_a, HEAD_SLAB, 5), pos],
                   vecs=[g_q, g_kn, g_kr, inv_freq],
                   row_outs=[(width, BF16), (width, BF16), (HEAD_SLAB, F32)],
                   vec_outs=[HEAD_SLAB] * 3, ts=_tile(seq, (128, 64, 32, 16, 8)))


def _scores(q, k_ref, keys, tq):
    s = lax.dot_general(q, k_ref[0:keys, :], _DIMS["nt"], preferred_element_type=F32) * ATTN_SCALE
    row = lax.broadcasted_iota(jnp.int32, (tq, tq), 0)
    col = lax.broadcasted_iota(jnp.int32, (tq, tq), 1)
    diag = jnp.where(col <= row, s[:, keys - tq:], -1e30)
    return diag if keys == tq else jnp.concatenate([s[:, :keys - tq], diag], axis=1)


def _attn_fwd(qc, kc, vp, seq):
    tokens = qc.shape[0]
    tq = _tile(seq, (256, 128))
    nq = seq // tq

    def body(q_ref, k_ref, v_ref, o_ref, lse_ref):
        for i in range(nq):
            rows, keys = slice(i * tq, (i + 1) * tq), (i + 1) * tq
            s = _scores(q_ref[rows, :], k_ref, keys, tq)
            m = jnp.max(s, axis=-1, keepdims=True)
            p = jnp.exp(s - m)
            l = jnp.sum(p, axis=-1, keepdims=True)
            acc = jnp.dot(p.astype(BF16), v_ref[0:keys, :], preferred_element_type=F32)
            o_ref[rows, :] = (acc / l).astype(BF16)
            lse_ref[rows, :] = jnp.broadcast_to(m + jnp.log(l), (tq, HEAD_SLAB))

    spec = pl.BlockSpec((seq, HEAD_SLAB), lambda b, h: (b, h))
    return pl.pallas_call(
        body, name="attn_fwd", grid=(tokens // seq, N_HEADS),
        out_shape=(jax.ShapeDtypeStruct(qc.shape, BF16), jax.ShapeDtypeStruct(qc.shape, F32)),
        in_specs=[spec] * 3, out_specs=(spec, spec),
        compiler_params=_params("parallel", "parallel"),
    )(qc, kc, vp)


def _attn_bwd(qc, kc, vp, o, lse, do, seq):
    tokens = qc.shape[0]
    tq = _tile(seq, (256, 128))
    nq = seq // tq

    def body(q_ref, k_ref, v_ref, o_ref, lse_ref, do_ref, dq_ref, dk_ref, dv_ref):
        dk_ref[...] = jnp.zeros(dk_ref.shape, F32)
        dv_ref[...] = jnp.zeros(dv_ref.shape, F32)
        for i in range(nq):
            rows, keys = slice(i * tq, (i + 1) * tq), (i + 1) * tq
            q, dov = q_ref[rows, :], do_ref[rows, :]
            delta = jnp.sum(dov.astype(F32) * o_ref[rows, :].astype(F32), axis=-1, keepdims=True)
            s = _scores(q, k_ref, keys, tq)
            p = jnp.exp(s - jnp.tile(lse_ref[rows, :], (1, keys // HEAD_SLAB)))
            dp = lax.dot_general(dov, v_ref[0:keys, :], _DIMS["nt"], preferred_element_type=F32)
            ds = (p * (dp - delta) * ATTN_SCALE).astype(BF16)
            dq_ref[rows, :] = jnp.dot(ds, k_ref[0:keys, :], preferred_element_type=F32)
            dk_ref[0:keys, :] += lax.dot_general(ds, q, _DIMS["tn"], preferred_element_type=F32)
            dv_ref[0:keys, :] += lax.dot_general(p.astype(BF16), dov, _DIMS["tn"], preferred_element_type=F32)

    spec = pl.BlockSpec((seq, HEAD_SLAB), lambda b, h: (b, h))
    out = jax.ShapeDtypeStruct(qc.shape, F32)
    return pl.pallas_call(
        body, name="attn_bwd", grid=(tokens // seq, N_HEADS),
        out_shape=(out, out, out), in_specs=[spec] * 6, out_specs=(spec, spec, spec),
        compiler_params=_params("parallel", "parallel"),
    )(qc, kc, vp, o, lse, do)


def _adamw(w, g, m, v, name):
    rows, cols = w.shape
    whole = rows * cols * 4 <= ADAMW_WHOLE_BYTES
    tr = rows if whole else _tile(rows, (256, 128, 64, 32, 16, 8))
    c1 = 1.0 - ADAM_B1 ** ADAM_STEP
    c2 = 1.0 - ADAM_B2 ** ADAM_STEP

    def body(w_ref, g_ref, m_ref, v_ref, d_ref, nm_ref, nv_ref):
        gv = g_ref[...]
        nm = ADAM_B1 * m_ref[...] + (1.0 - ADAM_B1) * gv
        nv = ADAM_B2 * v_ref[...] + (1.0 - ADAM_B2) * (gv * gv)
        d_ref[...] = -ADAM_LR * ((nm / c1) / (jnp.sqrt(nv / c2) + ADAM_EPS) + ADAM_WD * w_ref[...])
        nm_ref[...] = nm
        nv_ref[...] = nv

    spec = pl.BlockSpec((tr, cols), lambda i: (i, 0))
    out = jax.ShapeDtypeStruct(w.shape, F32)
    return pl.pallas_call(
        body, name=name, grid=(rows // tr,), out_shape=(out, out, out),
        in_specs=[spec] * 4, out_specs=(spec, spec, spec),
        compiler_params=_params("parallel"),
    )(w, g, m, v)


def _mod_cols(c_all, w_ada, b_cols):
    def body(c_ref, w_ref, b_ref, act_ref, mod_ref):
        cv = c_ref[...]
        act = cv * _sigmoid(cv)
        act_ref[...] = act
        mod_ref[...] = jnp.dot(act.astype(BF16), w_ref[...].astype(BF16),
                               preferred_element_type=F32) + b_ref[...]

    n = w_ada.shape[1]
    return pl.pallas_call(
        body, name="mod_cols",
        out_shape=(jax.ShapeDtypeStruct(c_all.shape, F32), jax.ShapeDtypeStruct((c_all.shape[0], n), F32)),
        compiler_params=pltpu.CompilerParams(vmem_limit_bytes=VMEM_LIMIT),
    )(c_all, w_ada, b_cols)


def _ada_grads(c_act, dmod_all, dmod_cols):
    def body(c_ref, d_ref, dc_ref, gw_ref, gb_ref):
        gw_ref[...] = lax.dot_general(c_ref[...].astype(BF16), dc_ref[...].astype(BF16), _DIMS["tn"],
                                      preferred_element_type=F32)
        gb_ref[...] = _colsum(d_ref[...])

    return pl.pallas_call(
        body, name="ada_grads",
        out_shape=(jax.ShapeDtypeStruct((c_act.shape[1], dmod_cols.shape[1]), F32),
                   jax.ShapeDtypeStruct((1, dmod_all.shape[1]), F32)),
        compiler_params=pltpu.CompilerParams(vmem_limit_bytes=VMEM_LIMIT),
    )(c_act, dmod_all, dmod_cols)


def _loss_grad(y, target, seq):
    def fn(rows, bats, vecs):
        err = rows[0] - rows[1]
        return [err * (1.0 / D_MODEL)], [], [_colsum(err * err)]
    return _rowmap("loss", fn, seq, [y, target], row_outs=[(D_MODEL, F32)], vec_outs=[D_MODEL])


def _flat_rows(a):
    flat = a.reshape(-1)
    pad = (-flat.shape[0]) % (LANES * SUBLANES)
    if pad:
        flat = jnp.pad(flat, (0, pad))
    return flat.reshape(-1, LANES)


def _gather_weights(w):
    shards = [(w[n] if n in ROW_SHARDED else w[n].T).astype(BF16) for n in BIG]
    gathered = _exchange(shards, "gather_weights")
    return {n: g.reshape(-1, g.shape[2]) for n, g in zip(BIG, gathered)}


def _scatter_grads(grads):
    blocks = [grads[n].reshape(N_DEV, -1, grads[n].shape[1]) for n in BIG]
    landed = _exchange(blocks, "scatter_grads", scatter=True)
    out = {}
    for n, x in zip(BIG, landed):
        g = _sum_blocks(x, f"sum_{n}")
        out[n] = g if n in ROW_SHARDED else g.T
    return out


def _pack_small(vals):
    return jnp.concatenate([_flat_rows(v.astype(F32)) for v in vals], axis=0)


def _unpack_small(packed, like):
    out, row = [], 0
    for v in like:
        rows = _flat_rows(v).shape[0]
        out.append(packed[row:row + rows].reshape(-1)[:v.size].reshape(v.shape))
        row += rows
    return out


def _lanes128(*parts):
    out = jnp.zeros((HEAD_SLAB,), F32)
    for off, v in parts:
        out = lax.dynamic_update_slice(out, v.reshape(-1).astype(F32), (off,))
    return out.reshape(1, HEAD_SLAB)


def _step(x, c, positions, w, m, v, loss_target):
    nseq, seq, _ = x.shape
    tokens = nseq * seq
    me = _index(_my_pos())
    strip = lambda d: {n: (a[0] if a.ndim > 2 else a) for n, a in d.items()}
    shapes = {n: a.shape for n, a in w.items()}
    w, m, v = strip(w), strip(m), strip(v)

    c_all = _all_gather(c.reshape(-1, LANES), "gather_c").reshape(N_DEV * nseq, D_MODEL)
    n_ada = w["w_ada"].shape[1]
    b_cols = lax.dynamic_slice(w["b_ada"], (0, me * n_ada), (1, n_ada))
    c_act, mod_cols = _mod_cols(c_all, w["w_ada"], b_cols)
    mod_all = _all_gather(mod_cols, "gather_mod")
    mod = lax.dynamic_slice(mod_all, (0, me * nseq, 0), (N_DEV, nseq, n_ada))
    mod = mod.transpose(1, 0, 2).reshape(nseq, 3, 3, 1, D_MODEL)

    full = _gather_weights(w)

    wt_in = full["w_in"]
    zero_rows = lambda rows: jnp.zeros((rows, D_MODEL), BF16)
    wt_p = wt_in[:512]
    wt_a = jnp.concatenate([wt_in[512:1152], zero_rows(QK_NOPE), wt_in[1152:1184], zero_rows(32)], axis=0)
    wt_g = wt_in[1184:]
    wtq_pad = jnp.pad(full["w_q_up"].reshape(N_HEADS, 96, Q_LORA), ((0, 0), (0, 32), (0, 0))).reshape(-1, Q_LORA)
    wtmla_pad = jnp.pad(full["w_mla_proj"].reshape(D_MODEL, N_HEADS, 64), ((0, 0), (0, 0), (64, 0))).reshape(D_MODEL, -1)
    wt_pool, wt_kv = full["w_pool_proj"], full["w_kv_up"]
    g_q = _lanes128((0, w["q_norm_nope"]), (QK_NOPE, w["q_norm_rope"]))
    g_kn = _lanes128((0, w["k_norm_nope"]))
    g_kr = _lanes128((QK_NOPE, w["k_norm_rope"]))
    freq = ROPE_THETA ** (-jnp.arange(0, QK_ROPE, 2, dtype=F32) / QK_ROPE)
    inv_freq = _lanes128((QK_NOPE, jnp.concatenate([freq, freq])))
    pos = positions.reshape(tokens, 1).astype(F32)

    def sub(k, gamma, coef, w_in_, w_out_):
        return dict(gamma=w[gamma], shift=mod[:, k, 0], scale=mod[:, k, 1], gate=coef * mod[:, k, 2],
                    w_in=w_in_, w_out=w_out_)
    p1 = sub(0, "norm_ffn1", 0.5, full["w_ffn1_in"], full["w_ffn1_out"])
    pm = sub(1, "norm_mix", 1.0, None, full["w_out"])
    p2 = sub(2, "norm_ffn2", 0.5, full["w_ffn2_in"], full["w_ffn2_out"])
    t_big = _tile(tokens, (2048, 1024, 512))
    t_mid = _tile(tokens, (1024, 512))

    x0 = x.reshape(tokens, D_MODEL)
    x1, saved1 = _ffn_fwd(x0, p1, seq, "ffn1")

    h2 = _norm_mod_fwd(x1, pm["gamma"], pm["shift"], pm["scale"], seq, "mix_norm")
    z_a = _mm(h2, wt_a, "nt", "mix_in_a", tm=t_big, tn=wt_a.shape[0])
    z_p = _mm(h2, wt_p, "nt", "mix_in_p", tm=t_big, tn=512)
    z_g = _mm(h2, wt_g, "nt", "mix_in_g", tm=t_big, tn=512)
    pooled, pg, ps = _pool_fwd(z_p, w["pool_grp"], w["pool_scale"], seq)
    br_pool = _mm(ps, wt_pool, "nt", "pool_proj", tm=t_big, tn=D_MODEL)
    qn, kvn = _latent_norm_fwd(z_a, w["q_a_norm"], w["kv_a_norm"], seq)
    qp = _mm(qn, wtq_pad, "nt", "q_up", tm=t_big, tn=D_MODEL)
    kv = _mm(kvn, wt_kv, "nt", "kv_up", tm=t_big, tn=D_MODEL)
    qc, kc, vp = _qk_prep_fwd(qp, kv, z_a, pos, g_q, g_kn, g_kr, inv_freq, seq)
    attn, lse = _attn_fwd(qc, kc, vp, seq)
    br_mla = _mm(attn, wtmla_pad, "nt", "mla_proj", tm=t_mid, tn=D_MODEL)

    def merge(rows, bats, vecs):
        zg, bp, bm = rows
        return [_sigmoid(zg[:, :D_MODEL]) * bp + _sigmoid(zg[:, D_MODEL:]) * bm], [], []
    merged = _rowmap("merge", merge, seq, [z_g, br_pool, br_mla], row_outs=[(D_MODEL, BF16)])[0]
    x2, o_mix = _mm(merged, pm["w_out"], "nn", "mix_out", res=x1, gate=pm["gate"], seq=seq,
                    tm=_tile(seq, (1024, 512, 256, 128)), tn=D_MODEL)

    x3, saved2 = _ffn_fwd(x2, p2, seq, "ffn2")
    dy, sq_err = _loss_grad(x3, loss_target.reshape(tokens, D_MODEL), seq)
    loss = lax.psum(0.5 * jnp.sum(sq_err) * (1.0 / D_MODEL), AXES)

    grads = {}
    dx2, grads["w_ffn2_in"], grads["w_ffn2_out"], dg_ffn2, dsh2, dsc2, dgate2 = _ffn_bwd(dy, saved2, p2, seq, "ffn2")

    def mix_pre(rows, bats, vecs):
        (dv, ov), (gate,) = rows, bats
        return [gate * dv], [_colsum(dv * ov.astype(F32))], []
    do_mix, dgate_m = _rowmap("mix_bwd_pre", mix_pre, seq, [dx2, o_mix], [pm["gate"]],
                              row_outs=[(D_MODEL, BF16)], bat_outs=[D_MODEL])
    dmerged = _mm(do_mix, pm["w_out"], "nt", "mix_bwd_dmerged", tm=t_mid, tn=D_MODEL)
    grads["w_out"] = _mm(merged, do_mix, "tn", "mix_bwd_wout", out_dtype=BF16, tm=512, tn=D_MODEL)

    def merge_bwd(rows, bats, vecs):
        dmv, zg, bp, bm = rows
        s_p, s_m = _sigmoid(zg[:, :D_MODEL]), _sigmoid(zg[:, D_MODEL:])
        dzg = jnp.concatenate([dmv * bp * s_p * (1.0 - s_p), dmv * bm * s_m * (1.0 - s_m)], axis=1)
        return [dmv * s_p, dmv * s_m, dzg], [], []
    dbr_pool, dbr_mla, dz_g = _rowmap("merge_bwd", merge_bwd, seq, [dmerged, z_g, br_pool, br_mla],
                                      row_outs=[(D_MODEL, BF16), (D_MODEL, BF16), (2 * D_MODEL, BF16)])

    grads["w_pool_proj"] = _mm(dbr_pool, ps, "tn", "pool_bwd_wproj", out_dtype=BF16, tm=512, tn=POOL_WIDTH)
    dps = _mm(dbr_pool, wt_pool, "nn", "pool_bwd_dps", tm=t_big, tn=POOL_WIDTH)
    dz_p, dgrp, dpool_scale = _pool_bwd(dps, pooled, pg, w["pool_grp"], w["pool_scale"], seq)

    dwtmla_pad = _mm(dbr_mla, attn, "tn", "mla_bwd_wproj", out_dtype=BF16, tm=512, tn=D_MODEL)
    grads["w_mla_proj"] = dwtmla_pad.reshape(D_MODEL, N_HEADS, HEAD_SLAB)[:, :, 64:].reshape(D_MODEL, -1)
    d_attn = _mm(dbr_mla, wtmla_pad, "nn", "mla_bwd_dattn", out_dtype=BF16, tm=t_mid, tn=D_MODEL)
    dqc, dkc, dvp = _attn_bwd(qc, kc, vp, attn, lse, d_attn, seq)
    dqp, dkv, dkr, dg_q, dg_kn, dg_kr = _qk_prep_bwd(dqc, dkc, dvp, qp, kv, z_a, pos, g_q, g_kn, g_kr, inv_freq, seq)
    dwtq_pad = _mm(dqp, qn, "tn", "q_up_bwd_w", out_dtype=BF16, tm=512, tn=Q_LORA)
    grads["w_q_up"] = dwtq_pad.reshape(N_HEADS, HEAD_SLAB, Q_LORA)[:, :96].reshape(-1, Q_LORA)
    grads["w_kv_up"] = _mm(dkv, kvn, "tn", "kv_up_bwd_w", out_dtype=BF16, tm=512, tn=KV_LORA)
    dqn = _mm(dqp, wtq_pad, "nn", "q_up_bwd_x", tm=t_big, tn=Q_LORA)
    dkvn = _mm(dkv, wt_kv, "nn", "kv_up_bwd_x", tm=t_big, tn=KV_LORA)
    dz_a, dg_qa, dg_kva = _latent_norm_bwd(dqn, dkvn, dkr, z_a, w["q_a_norm"], w["kv_a_norm"], seq)

    dwt_a = _mm(dz_a, h2, "tn", "mix_in_bwd_wa", out_dtype=BF16, tm=256, tn=D_MODEL)
    dwt_p = _mm(dz_p, h2, "tn", "mix_in_bwd_wp", out_dtype=BF16, tm=512, tn=D_MODEL)
    dwt_g = _mm(dz_g, h2, "tn", "mix_in_bwd_wg", out_dtype=BF16, tm=512, tn=D_MODEL)
    grads["w_in"] = jnp.concatenate([dwt_p, dwt_a[:640], dwt_a[704:736], dwt_g], axis=0)
    dh2 = _mm(dz_a, wt_a, "nn", "mix_in_bwd_xa", tm=t_mid, tn=D_MODEL)
    dh2 = _mm(dz_p, wt_p, "nn", "mix_in_bwd_xp", tm=t_mid, tn=D_MODEL, add=dh2)
    dh2 = _mm(dz_g, wt_g, "nn", "mix_in_bwd_xg", tm=t_mid, tn=D_MODEL, add=dh2)
    dx1, dsh_m, dsc_m, dg_mix = _norm_mod_bwd(dh2, x1, dx2, pm["gamma"], pm["scale"], seq, "mix_bwd_norm")

    dx0, grads["w_ffn1_in"], grads["w_ffn1_out"], dg_ffn1, dsh1, dsc1, dgate1 = _ffn_bwd(dx1, saved1, p1, seq, "ffn1")

    dmod = jnp.stack([jnp.stack([dsh1, dsc1, 0.5 * dgate1], axis=1),
                      jnp.stack([dsh_m, dsc_m, dgate_m], axis=1),
                      jnp.stack([dsh2, dsc2, 0.5 * dgate2], axis=1)], axis=1)
    dmod_all = _all_gather(dmod.reshape(-1, LANES), "gather_dmod").reshape(N_DEV * nseq, 9 * D_MODEL)
    dmod_cols = lax.dynamic_slice(dmod_all, (0, me * n_ada), (N_DEV * nseq, n_ada))
    g_w_ada, g_b_ada = _ada_grads(c_act, dmod_all, dmod_cols)

    sharded = _scatter_grads(grads)
    small_local = [dg_ffn1.reshape(w["norm_ffn1"].shape), dg_mix.reshape(w["norm_mix"].shape),
                   dg_ffn2.reshape(w["norm_ffn2"].shape), dgrp, dpool_scale, dg_qa, dg_kva,
                   dg_q[:, :QK_NOPE], dg_q[:, QK_NOPE:QK_NOPE + QK_ROPE], dg_kn[:, :QK_NOPE],
                   dg_kr[:, QK_NOPE:QK_NOPE + QK_ROPE]]
    small_sum = _sum_blocks(_all_gather(_pack_small(small_local), "gather_small"), "sum_small")
    small = dict(zip(SMALL, _unpack_small(small_sum, [w[n] for n in SMALL])))

    grad_w = dict(sharded, **small)
    grad_w["w_ada"], grad_w["b_ada"] = g_w_ada, g_b_ada

    delta, new_m, new_v = {}, {}, {}
    for n in ("w_ada",) + BIG:
        delta[n], new_m[n], new_v[n] = _adamw(w[n], grad_w[n], m[n], v[n], f"adamw_{n}")
    rep = ("b_ada",) + SMALL
    d_s, m_s, v_s = _adamw(_pack_small([w[n] for n in rep]), _pack_small([grad_w[n] for n in rep]),
                           _pack_small([m[n] for n in rep]), _pack_small([v[n] for n in rep]), "adamw_small")
    like = [w[n] for n in rep]
    for dst, packed in ((delta, d_s), (new_m, m_s), (new_v, v_s)):
        dst.update(zip(rep, _unpack_small(packed, like)))

    lead = lambda d: [d[n].reshape(shapes[n]) for n in WEIGHTS]
    return (loss, dx0.reshape(x.shape), *lead(grad_w), *lead(delta), *lead(new_m), *lead(new_v))


def kernel(x, c, positions, w_ada, b_ada, norm_ffn1, w_ffn1_in, w_ffn1_out, norm_mix, w_in, pool_grp, pool_scale, w_pool_proj, q_a_norm, w_q_up, kv_a_norm, w_kv_up, q_norm_nope, q_norm_rope, k_norm_nope, k_norm_rope, w_mla_proj, w_out, norm_ffn2, w_ffn2_in, w_ffn2_out, loss_target, m_w_ada, m_b_ada, m_norm_ffn1, m_w_ffn1_in, m_w_ffn1_out, m_norm_mix, m_w_in, m_pool_grp, m_pool_scale, m_w_pool_proj, m_q_a_norm, m_w_q_up, m_kv_a_norm, m_w_kv_up, m_q_norm_nope, m_q_norm_rope, m_k_norm_nope, m_k_norm_rope, m_w_mla_proj, m_w_out, m_norm_ffn2, m_w_ffn2_in, m_w_ffn2_out, v_w_ada, v_b_ada, v_norm_ffn1, v_w_ffn1_in, v_w_ffn1_out, v_norm_mix, v_w_in, v_pool_grp, v_pool_scale, v_w_pool_proj, v_q_a_norm, v_w_q_up, v_kv_a_norm, v_w_kv_up, v_q_norm_nope, v_q_norm_rope, v_k_norm_nope, v_k_norm_rope, v_w_mla_proj, v_w_out, v_norm_ffn2, v_w_ffn2_in, v_w_ffn2_out):
    w = dict(w_ada=w_ada, b_ada=b_ada, norm_ffn1=norm_ffn1, w_ffn1_in=w_ffn1_in, w_ffn1_out=w_ffn1_out, norm_mix=norm_mix, w_in=w_in, pool_grp=pool_grp, pool_scale=pool_scale, w_pool_proj=w_pool_proj, q_a_norm=q_a_norm, w_q_up=w_q_up, kv_a_norm=kv_a_norm, w_kv_up=w_kv_up, q_norm_nope=q_norm_nope, q_norm_rope=q_norm_rope, k_norm_nope=k_norm_nope, k_norm_rope=k_norm_rope, w_mla_proj=w_mla_proj, w_out=w_out, norm_ffn2=norm_ffn2, w_ffn2_in=w_ffn2_in, w_ffn2_out=w_ffn2_out)
    m = dict(w_ada=m_w_ada, b_ada=m_b_ada, norm_ffn1=m_norm_ffn1, w_ffn1_in=m_w_ffn1_in, w_ffn1_out=m_w_ffn1_out, norm_mix=m_norm_mix, w_in=m_w_in, pool_grp=m_pool_grp, pool_scale=m_pool_scale, w_pool_proj=m_w_pool_proj, q_a_norm=m_q_a_norm, w_q_up=m_w_q_up, kv_a_norm=m_kv_a_norm, w_kv_up=m_w_kv_up, q_norm_nope=m_q_norm_nope, q_norm_rope=m_q_norm_rope, k_norm_nope=m_k_norm_nope, k_norm_rope=m_k_norm_rope, w_mla_proj=m_w_mla_proj, w_out=m_w_out, norm_ffn2=m_norm_ffn2, w_ffn2_in=m_w_ffn2_in, w_ffn2_out=m_w_ffn2_out)
    v = dict(w_ada=v_w_ada, b_ada=v_b_ada, norm_ffn1=v_norm_ffn1, w_ffn1_in=v_w_ffn1_in, w_ffn1_out=v_w_ffn1_out, norm_mix=v_norm_mix, w_in=v_w_in, pool_grp=v_pool_grp, pool_scale=v_pool_scale, w_pool_proj=v_w_pool_proj, q_a_norm=v_q_a_norm, w_q_up=v_w_q_up, kv_a_norm=v_kv_a_norm, w_kv_up=v_w_kv_up, q_norm_nope=v_q_norm_nope, q_norm_rope=v_q_norm_rope, k_norm_nope=v_k_norm_nope, k_norm_rope=v_k_norm_rope, w_mla_proj=v_w_mla_proj, w_out=v_w_out, norm_ffn2=v_norm_ffn2, w_ffn2_in=v_w_ffn2_in, w_ffn2_out=v_w_ffn2_out)
    return _step(x, c, positions, w, m, v, loss_target)
```

```python
import functools
import math

import jax
import jax.numpy as jnp
from jax import lax
from jax.experimental import pallas as pl
from jax.experimental.pallas import tpu as pltpu

F32 = jnp.float32
BF16 = jnp.bfloat16
MESH = pl.DeviceIdType.MESH
AXES = ("x", "y", "c")
N_DEV = 8

D_MODEL = 1024
D_FF = 2816
N_HEADS = 8
HEAD_SLAB = 128
QK_NOPE = 64
QK_ROPE = 32
POOL_WIDTH = 512
POOL_GROUPS = 4
POOL_GROUP_DIM = 128
Q_LORA = 384
KV_LORA = 256
ROPE_THETA = 10000.0
ATTN_SCALE = 1.0 / math.sqrt(QK_NOPE + QK_ROPE)
NORM_EPS = 1e-6
ADAM_LR, ADAM_B1, ADAM_B2, ADAM_EPS, ADAM_WD, ADAM_STEP = 0.001, 0.9, 0.999, 1e-08, 0.01, 10

LANES = 128
SUBLANES = 8
VMEM_LIMIT = 52 * 1024 * 1024
ADAMW_WHOLE_BYTES = 3 << 19

BIG = ("w_ffn1_in", "w_ffn1_out", "w_in", "w_pool_proj", "w_q_up", "w_kv_up",
       "w_mla_proj", "w_out", "w_ffn2_in", "w_ffn2_out")
ROW_SHARDED = ("w_ffn1_out", "w_out", "w_ffn2_out")
MIXER = ("w_in", "w_pool_proj", "w_q_up", "w_kv_up", "w_mla_proj", "w_out")
SMALL = ("norm_ffn1", "norm_mix", "norm_ffn2", "pool_grp", "pool_scale", "q_a_norm",
         "kv_a_norm", "q_norm_nope", "q_norm_rope", "k_norm_nope", "k_norm_rope")
WEIGHTS = ("w_ada", "b_ada", "norm_ffn1", "w_ffn1_in", "w_ffn1_out", "norm_mix", "w_in",
           "pool_grp", "pool_scale", "w_pool_proj", "q_a_norm", "w_q_up", "kv_a_norm",
           "w_kv_up", "q_norm_nope", "q_norm_rope", "k_norm_nope", "k_norm_rope",
           "w_mla_proj", "w_out", "norm_ffn2", "w_ffn2_in", "w_ffn2_out")


def _params(*sem):
    return pltpu.CompilerParams(dimension_semantics=sem, vmem_limit_bytes=VMEM_LIMIT)


def _tile(n, cands):
    for c in cands:
        if n % c == 0:
            return c
    return n


def _my_pos():
    return lax.axis_index("x"), lax.axis_index("y"), lax.axis_index("c")


def _flip(pos, k):
    x, y, c = pos
    fx, fy, fc = (k >> 2) & 1, (k >> 1) & 1, k & 1
    return ((1 - x) if fx else x, (1 - y) if fy else y, (1 - c) if fc else c)


def _index(pos):
    x, y, c = pos
    return 4 * x + 2 * y + c


def _exchange(arrays, name, scatter=False):
    n = len(arrays)

    def body(*refs):
        ins, outs = refs[:n], refs[n:2 * n]
        send_sems, recv_sems, local_sems = refs[2 * n:]
        me = _my_pos()
        mine, sends = [], []
        for a in range(n):
            own = ins[a].at[_index(me)] if scatter else ins[a]
            cp = pltpu.make_async_copy(own, outs[a].at[_index(me)], local_sems.at[a])
            cp.start()
            mine.append(cp)
        for k in range(1, N_DEV):
            peer = _flip(me, k)
            for a in range(n):
                cp = pltpu.make_async_remote_copy(
                    src_ref=ins[a].at[_index(peer)] if scatter else ins[a],
                    dst_ref=outs[a].at[_index(me)],
                    send_sem=send_sems.at[a, k - 1], recv_sem=recv_sems.at[a, k - 1],
                    device_id=peer, device_id_type=MESH)
                cp.start()
                sends.append(cp)
        for k in range(1, N_DEV):
            peer = _flip(me, k)
            for a in range(n):
                pltpu.make_async_remote_copy(
                    src_ref=ins[a].at[_index(me)] if scatter else ins[a],
                    dst_ref=outs[a].at[_index(peer)],
                    send_sem=send_sems.at[a, k - 1], recv_sem=recv_sems.at[a, k - 1],
                    device_id=peer, device_id_type=MESH).wait_recv()
        for cp in sends:
            cp.wait_send()
        for cp in mine:
            cp.wait()

    shape = lambda x: x.shape if scatter else (N_DEV,) + x.shape
    return pl.pallas_call(
        body, name=name,
        out_shape=tuple(jax.ShapeDtypeStruct(shape(x), x.dtype) for x in arrays),
        in_specs=[pl.BlockSpec(memory_space=pl.ANY)] * n,
        out_specs=tuple(pl.BlockSpec(memory_space=pl.ANY) for _ in arrays),
        scratch_shapes=[pltpu.SemaphoreType.DMA((n, N_DEV - 1)),
                        pltpu.SemaphoreType.DMA((n, N_DEV - 1)),
                        pltpu.SemaphoreType.DMA((n,))],
    )(*arrays)


def _all_gather(x, name):
    return _exchange([x], name)[0]


_HBM = pl.BlockSpec(memory_space=pltpu.HBM)
_SEM = pl.BlockSpec(memory_space=pltpu.SEMAPHORE)
_ANY = pl.BlockSpec(memory_space=pl.ANY)
_EFFECT = pltpu.SideEffectType.DATAFLOW_SIDE_EFFECTING


def _split_copy(ins, lands, send_sems, recv_sems, a, k, me, scatter, incoming):
    peer = _flip(me, k)
    block = me if incoming else peer
    return pltpu.make_async_remote_copy(
        src_ref=ins[a].at[_index(block)] if scatter else ins[a],
        dst_ref=lands[a].at[_index(peer if incoming else me)],
        send_sem=send_sems.at[a * (N_DEV - 1) + k - 1], recv_sem=recv_sems.at[a * (N_DEV - 1) + k - 1],
        device_id=peer, device_id_type=MESH)


def _exchange_start(arrays, name, scatter=False, after=None):
    n = len(arrays)
    after = jnp.zeros((SUBLANES, LANES), F32) if after is None else after

    def body(*refs):
        ins, lands = refs[:n], refs[n:2 * n]
        send_sems, recv_sems = refs[2 * n + 1], refs[2 * n + 2]
        me = _my_pos()
        for k in range(1, N_DEV):
            for a in range(n):
                _split_copy(ins, lands, send_sems, recv_sems, a, k, me, scatter, False).start()
        refs[-1][...] = jnp.zeros((SUBLANES, LANES), F32)

    shape = lambda x: x.shape if scatter else (N_DEV,) + x.shape
    hbm = lambda x: pltpu.with_memory_space_constraint(x, pltpu.HBM)
    srcs = [hbm(x) for x in arrays]
    zones = [hbm(lax.empty(shape(x), x.dtype)) for x in arrays]
    out = pl.pallas_call(
        body, name=name,
        out_shape=(pltpu.SemaphoreType.DMA((n * (N_DEV - 1),)), pltpu.SemaphoreType.DMA((n * (N_DEV - 1),)),
                   *[pltpu.HBM(x.shape, x.dtype) for x in srcs + zones],
                   jax.ShapeDtypeStruct((SUBLANES, LANES), F32)),
        in_specs=[_HBM] * (2 * n) + [_ANY],
        out_specs=(_SEM, _SEM, *[_HBM] * (2 * n), pl.BlockSpec(memory_space=pltpu.VMEM)),
        input_output_aliases={i: 2 + i for i in range(2 * n)},
        compiler_params=pltpu.CompilerParams(has_side_effects=_EFFECT),
    )(*srcs, *zones, after)
    return out[:-1], out[-1]


def _exchange_wait(handle, name, scatter=False, after=None):
    send_sems, recv_sems = handle[0], handle[1]
    n = (len(handle) - 2) // 2
    after = jnp.zeros((SUBLANES, LANES), F32) if after is None else after

    def body(*refs):
        ins, lands = refs[:n], refs[n:2 * n]
        send, recv = refs[2 * n], refs[2 * n + 1]
        me = _my_pos()
        for k in range(1, N_DEV):
            for a in range(n):
                _split_copy(ins, lands, send, recv, a, k, me, scatter, False).wait_send()
                _split_copy(ins, lands, send, recv, a, k, me, scatter, True).wait_recv()

    bufs = handle[2:]
    out = pl.pallas_call(
        body, name=name,
        out_shape=tuple(pltpu.HBM(x.shape, x.dtype) for x in bufs),
        in_specs=[_HBM] * (2 * n) + [_SEM, _SEM, _ANY],
        out_specs=tuple([_HBM] * (2 * n)),
        input_output_aliases={i: i for i in range(2 * n)},
        compiler_params=pltpu.CompilerParams(has_side_effects=_EFFECT),
    )(*bufs, send_sems, recv_sems, after)
    me = _index(_my_pos())
    landed = []
    for src, land in zip(out[:n], out[n:]):
        own = lax.dynamic_slice_in_dim(src, me, 1, axis=0) if scatter else src[None]
        landed.append(lax.dynamic_update_slice_in_dim(land, own, me, axis=0))
    return landed


def _sum_blocks(x, name):
    n, rows, cols = x.shape
    tr = _tile(rows, (512, 256, 128, 64, 32, 16, 8))

    def body(x_ref, o_ref):
        acc = x_ref[0].astype(F32)
        for d in range(1, n):
            acc = acc + x_ref[d].astype(F32)
        o_ref[...] = acc

    return pl.pallas_call(
        body, name=name,
        out_shape=jax.ShapeDtypeStruct((rows, cols), F32),
        grid=(rows // tr,),
        in_specs=[pl.BlockSpec((n, tr, cols), lambda i: (0, i, 0))],
        out_specs=pl.BlockSpec((tr, cols), lambda i: (i, 0)),
        compiler_params=_params("parallel"),
    )(x)


_DIMS = {"nn": (((1,), (0,)), ((), ())), "nt": (((1,), (1,)), ((), ())), "tn": (((0,), (0,)), ((), ()))}


def _mm(a, b, mode, name, out_dtype=F32, tm=None, tn=None, add=None, res=None, gate=None, seq=None):
    if mode == "tn":
        kdim, m = a.shape
    else:
        m, kdim = a.shape
    n = b.shape[0] if mode == "nt" else b.shape[1]
    tm = tm or _tile(m, (512, 256, 128))
    tn = tn or _tile(n, (512, 256, 128))
    if res is not None:
        assert seq % tm == 0
    dims = _DIMS[mode]

    def body(*refs):
        a_ref, b_ref = refs[0], refs[1]
        acc = lax.dot_general(a_ref[...].astype(BF16), b_ref[...].astype(BF16), dims,
                              preferred_element_type=F32)
        if add is not None:
            refs[3][...] = (acc + refs[2][...]).astype(out_dtype)
        elif res is not None:
            res_ref, gate_ref, o_ref, p_ref = refs[2:]
            o_ref[...] = res_ref[...] + gate_ref[0] * acc
            p_ref[...] = acc.astype(BF16)
        else:
            refs[2][...] = acc.astype(out_dtype)

    a_spec = (pl.BlockSpec((kdim, tm), lambda i, j: (0, i)) if mode == "tn"
              else pl.BlockSpec((tm, kdim), lambda i, j: (i, 0)))
    b_spec = (pl.BlockSpec((tn, kdim), lambda i, j: (j, 0)) if mode == "nt"
              else pl.BlockSpec((kdim, tn), lambda i, j: (0, j)))
    o_spec = pl.BlockSpec((tm, tn), lambda i, j: (i, j))
    in_specs, args = [a_spec, b_spec], [a, b]
    out_shape, out_specs = jax.ShapeDtypeStruct((m, n), out_dtype), o_spec
    if add is not None:
        in_specs.append(o_spec)
        args.append(add)
    if res is not None:
        per_seq = seq // tm
        in_specs += [o_spec, pl.BlockSpec((1, 1, tn), lambda i, j: (i // per_seq, 0, j))]
        args += [res, gate]
        out_shape = (jax.ShapeDtypeStruct((m, n), F32), jax.ShapeDtypeStruct((m, n), BF16))
        out_specs = (o_spec, o_spec)
    return pl.pallas_call(
        body, name=name, out_shape=out_shape, grid=(m // tm, n // tn),
        in_specs=in_specs, out_specs=out_specs,
        compiler_params=_params("parallel", "parallel"),
    )(*args)


def _rowmap(name, fn, seq, rows, bats=(), vecs=(), row_outs=(), bat_outs=(), vec_outs=(), ts=None):
    rows = [r if isinstance(r, tuple) else (r, r.shape[1], 0) for r in rows]
    tokens = rows[0][0].shape[0]
    nseq = tokens // seq
    ts = ts or _tile(seq, (256, 128, 64, 32, 16, 8))
    nt = seq // ts
    n_r, n_b, n_v = len(rows), len(bats), len(vecs)
    n_ro, n_bo = len(row_outs), len(bat_outs)

    def accumulate(ref, val, first):
        @pl.when(first)
        def _():
            ref[...] = val.reshape(ref.shape)

        @pl.when(jnp.logical_not(first))
        def _():
            ref[...] += val.reshape(ref.shape)

    def body(*refs):
        ins, outs = refs[:n_r + n_b + n_v], refs[n_r + n_b + n_v:]
        r_vals = [r[...] for r in ins[:n_r]]
        b_vals = [r[0] for r in ins[n_r:n_r + n_b]]
        v_vals = [r[...] for r in ins[n_r + n_b:]]
        ro, bo, vo = fn(r_vals, b_vals, v_vals)
        for ref, val in zip(outs[:n_ro], ro):
            ref[...] = val.astype(ref.dtype)
        b, i = pl.program_id(0), pl.program_id(1)
        for ref, val in zip(outs[n_ro:n_ro + n_bo], bo):
            accumulate(ref, val, i == 0)
        for ref, val in zip(outs[n_ro + n_bo:], vo):
            accumulate(ref, val, jnp.logical_and(i == 0, b == 0))

    in_specs = [pl.BlockSpec((ts, w), functools.partial(lambda b, i, cb: (b * nt + i, cb), cb=cb))
                for _, w, cb in rows]
    in_specs += [pl.BlockSpec((1, 1, v.shape[2]), lambda b, i: (b, 0, 0)) for v in bats]
    in_specs += [pl.BlockSpec((1, v.shape[1]), lambda b, i: (0, 0)) for v in vecs]
    out_shape = [jax.ShapeDtypeStruct((tokens, f), dt) for f, dt in row_outs]
    out_specs = [pl.BlockSpec((ts, f), lambda b, i: (b * nt + i, 0)) for f, _ in row_outs]
    out_shape += [jax.ShapeDtypeStruct((nseq, 1, f), F32) for f in bat_outs]
    out_specs += [pl.BlockSpec((1, 1, f), lambda b, i: (b, 0, 0)) for f in bat_outs]
    out_shape += [jax.ShapeDtypeStruct((1, f), F32) for f in vec_outs]
    out_specs += [pl.BlockSpec((1, f), lambda b, i: (0, 0)) for f in vec_outs]
    return pl.pallas_call(
        body, name=name, out_shape=tuple(out_shape), grid=(nseq, nt),
        in_specs=in_specs, out_specs=tuple(out_specs),
        compiler_params=_params("arbitrary", "arbitrary"),
    )(*([r[0] for r in rows] + list(bats) + list(vecs)))


def _colsum(v):
    return jnp.sum(v, axis=0, keepdims=True)


def _rstd(x, width=None):
    width = width or x.shape[-1]
    return lax.rsqrt(jnp.sum(x * x, axis=-1, keepdims=True) * (1.0 / width) + NORM_EPS)


def _norm_bwd(dy, x, r, g, width=None):
    width = width or x.shape[-1]
    xhat = x * r
    dxhat = dy * g
    dx = r * (dxhat - xhat * (jnp.sum(dxhat * xhat, axis=-1, keepdims=True) * (1.0 / width)))
    return dx, dy * xhat


def _sigmoid(x):
    return 1.0 / (1.0 + jnp.exp(-x))


def _norm_mod_fwd(x, gamma, shift, scale, seq, name):
    def fn(rows, bats, vecs):
        (xv,), (sh, sc), (g,) = rows, bats, vecs
        return [xv * _rstd(xv) * g * (1.0 + sc) + sh], [], []
    return _rowmap(name, fn, seq, [x], [shift, scale], [gamma], row_outs=[(D_MODEL, BF16)])[0]


def _norm_mod_bwd(dh, x, dres, gamma, scale, seq, name):
    def fn(rows, bats, vecs):
        (dhv, xv, dr), (sc,), (g,) = rows, bats, vecs
        r = _rstd(xv)
        dx, dg = _norm_bwd(dhv * (1.0 + sc), xv, r, g)
        return [dr + dx], [_colsum(dhv), _colsum(dhv * (xv * r * g))], [_colsum(dg)]
    return _rowmap(name, fn, seq, [dh, x, dres], [scale], [gamma], row_outs=[(D_MODEL, F32)],
                   bat_outs=[D_MODEL, D_MODEL], vec_outs=[D_MODEL])


def _ffn_fwd(x, p, seq, tag):
    tokens = x.shape[0]
    h = _norm_mod_fwd(x, p["gamma"], p["shift"], p["scale"], seq, f"{tag}_norm")
    w_in = p["w_in"](h)
    gu = _mm(h, w_in, "nt", f"{tag}_in", out_dtype=BF16, tm=_tile(tokens, (2048, 1024, 512)), tn=512)

    def act(rows, bats, vecs):
        g, u = rows[0][:, :D_FF].astype(F32), rows[0][:, D_FF:].astype(F32)
        return [g * _sigmoid(g) * u], [], []
    a = _rowmap(f"{tag}_act", act, seq, [gu], row_outs=[(D_FF, BF16)])[0]
    w_out = p["w_out"](a)
    x_new, f = _mm(a, w_out, "nn", f"{tag}_out", res=x, gate=p["gate"], seq=seq,
                   tm=_tile(seq, (512, 256, 128)), tn=D_MODEL)
    return x_new, (x, h, gu, f, w_in, w_out)


def _ffn_bwd_x(dxo, saved, p, gate, seq, tag):
    x, h, gu, f, w_in, w_out = saved
    tokens = x.shape[0]

    def pre(rows, bats, vecs):
        (dv, fv), (gt,) = rows, bats
        return [gt * dv], [_colsum(dv * fv.astype(F32))], []
    df, dgate = _rowmap(f"{tag}_bwd_pre", pre, seq, [dxo, f], [gate],
                        row_outs=[(D_MODEL, BF16)], bat_outs=[D_MODEL])
    da = _mm(df, w_out, "nt", f"{tag}_bwd_da", tm=_tile(tokens, (512, 256)), tn=D_FF)

    def act_bwd(rows, bats, vecs):
        dav, guv = rows
        g, u = guv[:, :D_FF].astype(F32), guv[:, D_FF:].astype(F32)
        sg = _sigmoid(g)
        silu = g * sg
        dg = dav * u * (sg * (1.0 + g * (1.0 - sg)))
        return [silu * u, jnp.concatenate([dg, dav * silu], axis=1)], [], []
    a, dgu = _rowmap(f"{tag}_bwd_act", act_bwd, seq, [da, gu],
                     row_outs=[(D_FF, BF16), (2 * D_FF, BF16)])
    dh = _mm(dgu, w_in, "nn", f"{tag}_bwd_dh", tm=_tile(tokens, (512, 256)), tn=D_MODEL)
    dx, dshift, dscale, dgamma = _norm_mod_bwd(dh, x, dxo, p["gamma"], p["scale"], seq, f"{tag}_bwd_norm")
    return dx, (a, df, dgu, h), dgamma, dshift, dscale, dgate


def _ffn_bwd_w(operands, tag):
    a, df, dgu, h = operands
    dw_out = _mm(a, df, "tn", f"{tag}_bwd_wout", out_dtype=BF16, tm=256, tn=D_MODEL)
    dwt_in = _mm(dgu, h, "tn", f"{tag}_bwd_win", out_dtype=BF16, tm=512, tn=D_MODEL)
    return dwt_in, dw_out


def _shift_rows(v, k, forward):
    n = v.shape[0]
    row = lax.broadcasted_iota(jnp.int32, v.shape, 0)
    if forward:
        return jnp.where(row >= k, pltpu.roll(v, k, 0), 0.0)
    return jnp.where(row < n - k, pltpu.roll(v, n - k, 0), 0.0)


def _window_sums(v, forward):
    out, s, k = [], v, 1
    for _ in range(POOL_GROUPS):
        s = s + _shift_rows(s, k, forward)
        out.append(s)
        k *= 2
    return out


def _by_group(vals, g):
    out = vals[-1]
    for idx in range(len(vals) - 2, -1, -1):
        out = jnp.where(g == idx, vals[idx], out)
    return out


def _inv_count(shape, g):
    t1 = lax.broadcasted_iota(jnp.int32, shape, 0) + 1
    window = _by_group([jnp.int32(2 ** (i + 1)) for i in range(POOL_GROUPS)], g)
    return 1.0 / jnp.minimum(t1, window).astype(F32)


def _pool_fwd(u, grp, scale, seq):
    tokens = u.shape[0]

    def body(u_ref, grp_ref, sc_ref, pooled_ref, pg_ref, ps_ref):
        g = pl.program_id(1)
        uv = u_ref[...]
        sums = _by_group(_window_sums(uv, True), g)
        pooled = (sums * _inv_count(uv.shape, g) - uv).astype(BF16)
        pg = jnp.dot(pooled, grp_ref[0].astype(BF16), preferred_element_type=F32)
        pooled_ref[...] = pooled
        pg_ref[...] = pg
        ps_ref[...] = (pg * sc_ref[...]).astype(BF16)

    blk = pl.BlockSpec((seq, POOL_GROUP_DIM), lambda b, g: (b, g))
    return pl.pallas_call(
        body, name="pool_fwd", grid=(tokens // seq, POOL_GROUPS),
        out_shape=(jax.ShapeDtypeStruct(u.shape, BF16), jax.ShapeDtypeStruct(u.shape, F32),
                   jax.ShapeDtypeStruct(u.shape, BF16)),
        in_specs=[blk, pl.BlockSpec((1, POOL_GROUP_DIM, POOL_GROUP_DIM), lambda b, g: (g, 0, 0)),
                  pl.BlockSpec((1, POOL_GROUP_DIM), lambda b, g: (0, g))],
        out_specs=(blk, blk, blk),
        compiler_params=_params("parallel", "parallel"),
    )(u, grp, scale)


def _pool_bwd(dps, pooled, pg, grp, scale, seq):
    tokens = dps.shape[0]

    def body(dps_ref, pooled_ref, pg_ref, grp_ref, sc_ref, du_ref, dgrp_ref, dsc_ref):
        g, b = pl.program_id(0), pl.program_id(1)
        dpsv = dps_ref[...]
        dpg = (dpsv * sc_ref[...]).astype(BF16)
        dsc = _colsum(dpsv * pg_ref[...])
        dgrp = lax.dot_general(pooled_ref[...], dpg, _DIMS["tn"], preferred_element_type=F32)

        @pl.when(b == 0)
        def _():
            dsc_ref[...] = dsc
            dgrp_ref[0] = dgrp

        @pl.when(b > 0)
        def _():
            dsc_ref[...] += dsc
            dgrp_ref[0] += dgrp

        dpool = lax.dot_general(dpg, grp_ref[0].astype(BF16), _DIMS["nt"], preferred_element_type=F32)
        sums = _by_group(_window_sums(dpool * _inv_count(dpool.shape, g), False), g)
        du_ref[...] = (sums - dpool).astype(BF16)

    blk = pl.BlockSpec((seq, POOL_GROUP_DIM), lambda g, b: (b, g))
    grp_spec = pl.BlockSpec((1, POOL_GROUP_DIM, POOL_GROUP_DIM), lambda g, b: (g, 0, 0))
    vec_spec = pl.BlockSpec((1, POOL_GROUP_DIM), lambda g, b: (0, g))
    return pl.pallas_call(
        body, name="pool_bwd", grid=(POOL_GROUPS, tokens // seq),
        out_shape=(jax.ShapeDtypeStruct(dps.shape, BF16), jax.ShapeDtypeStruct(grp.shape, F32),
                   jax.ShapeDtypeStruct(scale.shape, F32)),
        in_specs=[blk, blk, blk, grp_spec, vec_spec],
        out_specs=(blk, grp_spec, vec_spec),
        compiler_params=_params("arbitrary", "arbitrary"),
    )(dps, pooled, pg, grp, scale)


def _lane(shape):
    return lax.broadcasted_iota(jnp.int32, shape, len(shape) - 1)


def _rot(y):
    lane = _lane(y.shape)
    r = jnp.where(lane < QK_NOPE + QK_ROPE // 2,
                  -pltpu.roll(y, HEAD_SLAB - QK_ROPE // 2, 1), pltpu.roll(y, QK_ROPE // 2, 1))
    return jnp.where(jnp.logical_and(lane >= QK_NOPE, lane < QK_NOPE + QK_ROPE), r, 0.0)


def _part_rstd(x):
    sq = x * x
    nope = _lane(x.shape) < QK_NOPE
    s_nope = jnp.sum(jnp.where(nope, sq, 0.0), axis=-1, keepdims=True)
    s_rope = jnp.sum(sq, axis=-1, keepdims=True) - s_nope
    return jnp.where(nope, lax.rsqrt(s_nope * (1.0 / QK_NOPE) + NORM_EPS),
                     lax.rsqrt(s_rope * (1.0 / QK_ROPE) + NORM_EPS))


def _part_norm_bwd(dy, x, r, g):
    nope = _lane(x.shape) < QK_NOPE
    xhat = x * r
    dxhat = dy * g
    prod = dxhat * xhat
    m_nope = jnp.sum(jnp.where(nope, prod, 0.0), axis=-1, keepdims=True)
    m_rope = jnp.sum(prod, axis=-1, keepdims=True) - m_nope
    mean = jnp.where(nope, m_nope * (1.0 / QK_NOPE), m_rope * (1.0 / QK_ROPE))
    return r * (dxhat - xhat * mean), dy * xhat


def _latent_norm_fwd(z_a, g_q, g_kv, seq):
    def fn(rows, bats, vecs):
        q, kv = rows[0][:, :Q_LORA], rows[0][:, Q_LORA:Q_LORA + KV_LORA]
        return [q * _rstd(q) * vecs[0], kv * _rstd(kv) * vecs[1]], [], []
    return _rowmap("latent_norm", fn, seq, [z_a], vecs=[g_q, g_kv],
                   row_outs=[(Q_LORA, BF16), (KV_LORA, BF16)])


def _latent_norm_bwd(dqn, dkvn, dkr, z_a, g_q, g_kv, seq):
    def fn(rows, bats, vecs):
        dq, dkv, dkrv, z = rows
        q, kv = z[:, :Q_LORA], z[:, Q_LORA:Q_LORA + KV_LORA]
        dxq, dgq = _norm_bwd(dq, q, _rstd(q), vecs[0])
        dxkv, dgkv = _norm_bwd(dkv, kv, _rstd(kv), vecs[1])
        return [jnp.concatenate([dxq, dxkv, dkrv], axis=1)], [], [_colsum(dgq), _colsum(dgkv)]
    return _rowmap("latent_norm_bwd", fn, seq, [dqn, dkvn, dkr, z_a], vecs=[g_q, g_kv],
                   row_outs=[(Q_LORA + KV_LORA + HEAD_SLAB, BF16)], vec_outs=[Q_LORA, KV_LORA])


def _qk_prep_fwd(qp, kv, z_a, pos, g_q, g_kn, g_kr, inv_freq, seq):
    def fn(rows, bats, vecs):
        qv, kvv, kr, p = rows
        gq, gkn, gkr, invf = vecs
        ang = p * invf
        cos, sin = jnp.cos(ang), jnp.sin(ang)
        nope = _lane(kr.shape) < QK_NOPE
        krn = kr * _rstd(kr, QK_ROPE) * gkr
        krr = krn * cos + _rot(krn) * sin
        qs, ks, vs = [], [], []
        for h in range(N_HEADS):
            xq = qv[:, h * HEAD_SLAB:(h + 1) * HEAD_SLAB]
            y = xq * _part_rstd(xq) * gq
            qs.append(y * cos + _rot(y) * sin)
            xk = kvv[:, h * HEAD_SLAB:(h + 1) * HEAD_SLAB]
            kn = jnp.where(nope, xk, 0.0)
            ks.append(jnp.where(nope, kn * _rstd(kn, QK_NOPE) * gkn, krr))
            vs.append(jnp.where(nope, 0.0, xk))
        return [jnp.concatenate(v, axis=1) for v in (qs, ks, vs)], [], []
    width = N_HEADS * HEAD_SLAB
    return _rowmap("qk_prep", fn, seq, [qp, kv, (z_a, HEAD_SLAB, 5), pos], vecs=[g_q, g_kn, g_kr, inv_freq],
                   row_outs=[(width, BF16)] * 3, ts=_tile(seq, (128, 64, 32, 16, 8)))


def _qk_prep_bwd(dqc, dkc, dvp, qp, kv, z_a, pos, g_q, g_kn, g_kr, inv_freq, seq):
    def fn(rows, bats, vecs):
        dq, dk, dv, qv, kvv, kr, p = rows
        gq, gkn, gkr, invf = vecs
        ang = p * invf
        cos, sin = jnp.cos(ang), jnp.sin(ang)
        nope = _lane(kr.shape) < QK_NOPE
        dqs, dkvs = [], []
        dgq = jnp.zeros((1, HEAD_SLAB), F32)
        dgkn = jnp.zeros((1, HEAD_SLAB), F32)
        dkrr = jnp.zeros(kr.shape, F32)
        for h in range(N_HEADS):
            sl = slice(h * HEAD_SLAB, (h + 1) * HEAD_SLAB)
            dyr = dq[:, sl]
            dy = dyr * cos - _rot(dyr * sin)
            xq = qv[:, sl]
            dx, dg = _part_norm_bwd(dy, xq, _part_rstd(xq), gq)
            dqs.append(dx)
            dgq = dgq + _colsum(dg)
            dkh = dk[:, sl]
            dkrr = dkrr + jnp.where(nope, 0.0, dkh)
            kn = jnp.where(nope, kvv[:, sl], 0.0)
            dxk, dgk = _norm_bwd(jnp.where(nope, dkh, 0.0), kn, _rstd(kn, QK_NOPE), gkn, QK_NOPE)
            dgkn = dgkn + _colsum(dgk)
            dkvs.append(jnp.where(nope, dxk, dv[:, sl]))
        dkrn = dkrr * cos - _rot(dkrr * sin)
        dkr, dgkr = _norm_bwd(dkrn, kr, _rstd(kr, QK_ROPE), gkr, QK_ROPE)
        return ([jnp.concatenate(dqs, axis=1), jnp.concatenate(dkvs, axis=1), dkr], [],
                [dgq, dgkn, _colsum(dgkr)])
    width = N_HEADS * HEAD_SLAB
    return _rowmap("qk_prep_bwd", fn, seq, [dqc, dkc, dvp, qp, kv, (z_a, HEAD_SLAB, 5), pos],
                   vecs=[g_q, g_kn, g_kr, inv_freq],
                   row_outs=[(width, BF16), (width, BF16), (HEAD_SLAB, F32)],
                   vec_outs=[HEAD_SLAB] * 3, ts=_tile(seq, (128, 64, 32, 16, 8)))


def _scores(q, k_ref, keys, tq):
    s = lax.dot_general(q, k_ref[0:keys, :], _DIMS["nt"], preferred_element_type=F32) * ATTN_SCALE
    row = lax.broadcasted_iota(jnp.int32, (tq, tq), 0)
    col = lax.broadcasted_iota(jnp.int32, (tq, tq), 1)
    diag = jnp.where(col <= row, s[:, keys - tq:], -1e30)
    return diag if keys == tq else jnp.concatenate([s[:, :keys - tq], diag], axis=1)


def _attn_fwd(qc, kc, vp, seq):
    tokens = qc.shape[0]
    tq = _tile(seq, (256, 128))
    nq = seq // tq

    def body(q_ref, k_ref, v_ref, o_ref, lse_ref):
        for i in range(nq):
            rows, keys = slice(i * tq, (i + 1) * tq), (i + 1) * tq
            s = _scores(q_ref[rows, :], k_ref, keys, tq)
            m = jnp.max(s, axis=-1, keepdims=True)
            p = jnp.exp(s - m)
            l = jnp.sum(p, axis=-1, keepdims=True)
            acc = jnp.dot(p.astype(BF16), v_ref[0:keys, :], preferred_element_type=F32)
            o_ref[rows, :] = (acc / l).astype(BF16)
            lse_ref[rows, :] = jnp.broadcast_to(m + jnp.log(l), (tq, HEAD_SLAB))

    spec = pl.BlockSpec((seq, HEAD_SLAB), lambda b, h: (b, h))
    return pl.pallas_call(
        body, name="attn_fwd", grid=(tokens // seq, N_HEADS),
        out_shape=(jax.ShapeDtypeStruct(qc.shape, BF16), jax.ShapeDtypeStruct(qc.shape, F32)),
        in_specs=[spec] * 3, out_specs=(spec, spec),
        compiler_params=_params("parallel", "parallel"),
    )(qc, kc, vp)


def _attn_bwd(qc, kc, vp, o, lse, do, seq):
    tokens = qc.shape[0]
    tq = _tile(seq, (256, 128))
    nq = seq // tq

    def body(q_ref, k_ref, v_ref, o_ref, lse_ref, do_ref, dq_ref, dk_ref, dv_ref):
        dk_ref[...] = jnp.zeros(dk_ref.shape, F32)
        dv_ref[...] = jnp.zeros(dv_ref.shape, F32)
        for i in range(nq):
            rows, keys = slice(i * tq, (i + 1) * tq), (i + 1) * tq
            q, dov = q_ref[rows, :], do_ref[rows, :]
            delta = jnp.sum(dov.astype(F32) * o_ref[rows, :].astype(F32), axis=-1, keepdims=True)
            s = _scores(q, k_ref, keys, tq)
            p = jnp.exp(s - jnp.tile(lse_ref[rows, :], (1, keys // HEAD_SLAB)))
            dp = lax.dot_general(dov, v_ref[0:keys, :], _DIMS["nt"], preferred_element_type=F32)
            ds = (p * (dp - delta) * ATTN_SCALE).astype(BF16)
            dq_ref[rows, :] = jnp.dot(ds, k_ref[0:keys, :], preferred_element_type=F32)
            dk_ref[0:keys, :] += lax.dot_general(ds, q, _DIMS["tn"], preferred_element_type=F32)
            dv_ref[0:keys, :] += lax.dot_general(p.astype(BF16), dov, _DIMS["tn"], preferred_element_type=F32)

    spec = pl.BlockSpec((seq, HEAD_SLAB), lambda b, h: (b, h))
    out = jax.ShapeDtypeStruct(qc.shape, F32)
    return pl.pallas_call(
        body, name="attn_bwd", grid=(tokens // seq, N_HEADS),
        out_shape=(out, out, out), in_specs=[spec] * 6, out_specs=(spec, spec, spec),
        compiler_params=_params("parallel", "parallel"),
    )(qc, kc, vp, o, lse, do)


def _adamw(w, g, m, v, name):
    rows, cols = w.shape
    whole = rows * cols * 4 <= ADAMW_WHOLE_BYTES
    tr = rows if whole else _tile(rows, (256, 128, 64, 32, 16, 8))
    c1 = 1.0 - ADAM_B1 ** ADAM_STEP
    c2 = 1.0 - ADAM_B2 ** ADAM_STEP

    def body(w_ref, g_ref, m_ref, v_ref, d_ref, nm_ref, nv_ref):
        gv = g_ref[...]
        nm = ADAM_B1 * m_ref[...] + (1.0 - ADAM_B1) * gv
        nv = ADAM_B2 * v_ref[...] + (1.0 - ADAM_B2) * (gv * gv)
        d_ref[...] = -ADAM_LR * ((nm / c1) / (jnp.sqrt(nv / c2) + ADAM_EPS) + ADAM_WD * w_ref[...])
        nm_ref[...] = nm
        nv_ref[...] = nv

    spec = pl.BlockSpec((tr, cols), lambda i: (i, 0))
    out = jax.ShapeDtypeStruct(w.shape, F32)
    return pl.pallas_call(
        body, name=name, grid=(rows // tr,), out_shape=(out, out, out),
        in_specs=[spec] * 4, out_specs=(spec, spec, spec),
        compiler_params=_params("parallel"),
    )(w, g, m, v)


def _mod_cols(c_all, w_ada, b_cols):
    def body(c_ref, w_ref, b_ref, act_ref, mod_ref):
        cv = c_ref[...]
        act = cv * _sigmoid(cv)
        act_ref[...] = act
        mod_ref[...] = jnp.dot(act.astype(BF16), w_ref[...].astype(BF16),
                               preferred_element_type=F32) + b_ref[...]

    n = w_ada.shape[1]
    return pl.pallas_call(
        body, name="mod_cols",
        out_shape=(jax.ShapeDtypeStruct(c_all.shape, F32), jax.ShapeDtypeStruct((c_all.shape[0], n), F32)),
        compiler_params=pltpu.CompilerParams(vmem_limit_bytes=VMEM_LIMIT),
    )(c_all, w_ada, b_cols)


def _ada_grads(c_act, dmod_all, dmod_cols):
    def body(c_ref, d_ref, dc_ref, gw_ref, gb_ref):
        gw_ref[...] = lax.dot_general(c_ref[...].astype(BF16), dc_ref[...].astype(BF16), _DIMS["tn"],
                                      preferred_element_type=F32)
        gb_ref[...] = _colsum(d_ref[...])

    return pl.pallas_call(
        body, name="ada_grads",
        out_shape=(jax.ShapeDtypeStruct((c_act.shape[1], dmod_cols.shape[1]), F32),
                   jax.ShapeDtypeStruct((1, dmod_all.shape[1]), F32)),
        compiler_params=pltpu.CompilerParams(vmem_limit_bytes=VMEM_LIMIT),
    )(c_act, dmod_all, dmod_cols)


def _loss_grad(y, target, seq):
    def fn(rows, bats, vecs):
        err = rows[0] - rows[1]
        return [err * (1.0 / D_MODEL)], [], [_colsum(err * err)]
    return _rowmap("loss", fn, seq, [y, target], row_outs=[(D_MODEL, F32)], vec_outs=[D_MODEL])


def _flat_rows(a):
    flat = a.reshape(-1)
    pad = (-flat.shape[0]) % (LANES * SUBLANES)
    if pad:
        flat = jnp.pad(flat, (0, pad))
    return flat.reshape(-1, LANES)


def _gather_start(w, names, tag, after):
    shards = [(w[n] if n in ROW_SHARDED else w[n].T).astype(BF16) for n in names]
    return _exchange_start(shards, f"gather_{tag}_start", after=after)


def _gather_wait(handle, names, tag, after):
    landed = _exchange_wait(handle, f"gather_{tag}_wait", after=after)
    return {n: g.reshape(-1, g.shape[2]) for n, g in zip(names, landed)}


def _scatter_start(grads, names, tag, after=None):
    blocks = [grads[n].reshape(N_DEV, -1, grads[n].shape[1]) for n in names]
    return _exchange_start(blocks, f"scatter_{tag}_start", scatter=True, after=after)


def _scatter_wait(handle, names, tag, after):
    landed = _exchange_wait(handle, f"scatter_{tag}_wait", scatter=True, after=after)
    out = {}
    for n, x in zip(names, landed):
        g = _sum_blocks(x, f"sum_{n}")
        out[n] = g if n in ROW_SHARDED else g.T
    return out


def _once(fn):
    cache = []

    def get(after):
        if not cache:
            cache.append(fn(after))
        return cache[0]
    return get


def _pack_small(vals):
    return jnp.concatenate([_flat_rows(v.astype(F32)) for v in vals], axis=0)


def _unpack_small(packed, like):
    out, row = [], 0
    for v in like:
        rows = _flat_rows(v).shape[0]
        out.append(packed[row:row + rows].reshape(-1)[:v.size].reshape(v.shape))
        row += rows
    return out


def _lanes128(*parts):
    out = jnp.zeros((HEAD_SLAB,), F32)
    for off, v in parts:
        out = lax.dynamic_update_slice(out, v.reshape(-1).astype(F32), (off,))
    return out.reshape(1, HEAD_SLAB)


def _step(x, c, positions, w, m, v, loss_target):
    nseq, seq, _ = x.shape
    tokens = nseq * seq
    me = _index(_my_pos())
    strip = lambda d: {n: (a[0] if a.ndim > 2 else a) for n, a in d.items()}
    shapes = {n: a.shape for n, a in w.items()}
    w, m, v = strip(w), strip(m), strip(v)

    c_all = _all_gather(c.reshape(-1, LANES), "gather_c").reshape(N_DEV * nseq, D_MODEL)
    n_ada = w["w_ada"].shape[1]
    b_cols = lax.dynamic_slice(w["b_ada"], (0, me * n_ada), (1, n_ada))
    c_act, mod_cols = _mod_cols(c_all, w["w_ada"], b_cols)
    mod_all = _all_gather(mod_cols, "gather_mod")
    mod = lax.dynamic_slice(mod_all, (0, me * nseq, 0), (N_DEV, nseq, n_ada))
    mod = mod.transpose(1, 0, 2).reshape(nseq, 3, 3, 1, D_MODEL)

    h_f1i, tok = _gather_start(w, ("w_ffn1_in",), "ffn1_in", after=mod_all)
    h_f1o, tok = _gather_start(w, ("w_ffn1_out",), "ffn1_out", after=tok)
    h_mix, tok = _gather_start(w, MIXER, "mix", after=tok)
    h_f2, tok = _gather_start(w, ("w_ffn2_in", "w_ffn2_out"), "ffn2", after=tok)
    started = tok[0:1, 0:1]
    ffn2_w = _once(lambda after: _gather_wait(h_f2, ("w_ffn2_in", "w_ffn2_out"), "ffn2", after))

    g_q = _lanes128((0, w["q_norm_nope"]), (QK_NOPE, w["q_norm_rope"]))
    g_kn = _lanes128((0, w["k_norm_nope"]))
    g_kr = _lanes128((QK_NOPE, w["k_norm_rope"]))
    freq = ROPE_THETA ** (-jnp.arange(0, QK_ROPE, 2, dtype=F32) / QK_ROPE)
    inv_freq = _lanes128((QK_NOPE, jnp.concatenate([freq, freq])))
    pos = positions.reshape(tokens, 1).astype(F32)

    def sub(k, gamma, coef, w_in_, w_out_):
        return dict(gamma=w[gamma], shift=mod[:, k, 0] + started, scale=mod[:, k, 1], gate=coef * mod[:, k, 2],
                    w_in=w_in_, w_out=w_out_)
    p1 = sub(0, "norm_ffn1", 0.5,
             _once(lambda after: _gather_wait(h_f1i, ("w_ffn1_in",), "ffn1_in", after)["w_ffn1_in"]),
             _once(lambda after: _gather_wait(h_f1o, ("w_ffn1_out",), "ffn1_out", after)["w_ffn1_out"]))
    pm = sub(1, "norm_mix", 1.0, None, None)
    p2 = sub(2, "norm_ffn2", 0.5, lambda after: ffn2_w(after)["w_ffn2_in"], lambda after: ffn2_w(after)["w_ffn2_out"])
    t_big = _tile(tokens, (2048, 1024, 512))
    t_mid = _tile(tokens, (1024, 512))

    x0 = x.reshape(tokens, D_MODEL)
    x1, saved1 = _ffn_fwd(x0, p1, seq, "ffn1")

    h2 = _norm_mod_fwd(x1, pm["gamma"], pm["shift"], pm["scale"], seq, "mix_norm")
    full = _gather_wait(h_mix, MIXER, "mix", h2)
    wt_in = full["w_in"]
    zero_rows = lambda rows: jnp.zeros((rows, D_MODEL), BF16)
    wt_p = wt_in[:512]
    wt_a = jnp.concatenate([wt_in[512:1152], zero_rows(QK_NOPE), wt_in[1152:1184], zero_rows(32)], axis=0)
    wt_g = wt_in[1184:]
    wtq_pad = jnp.pad(full["w_q_up"].reshape(N_HEADS, 96, Q_LORA), ((0, 0), (0, 32), (0, 0))).reshape(-1, Q_LORA)
    wtmla_pad = jnp.pad(full["w_mla_proj"].reshape(D_MODEL, N_HEADS, 64), ((0, 0), (0, 0), (64, 0))).reshape(D_MODEL, -1)
    wt_pool, wt_kv, w_mix_out = full["w_pool_proj"], full["w_kv_up"], full["w_out"]

    z_a = _mm(h2, wt_a, "nt", "mix_in_a", tm=t_big, tn=wt_a.shape[0])
    z_p = _mm(h2, wt_p, "nt", "mix_in_p", tm=t_big, tn=512)
    z_g = _mm(h2, wt_g, "nt", "mix_in_g", tm=t_big, tn=512)
    pooled, pg, ps = _pool_fwd(z_p, w["pool_grp"], w["pool_scale"], seq)
    br_pool = _mm(ps, wt_pool, "nt", "pool_proj", tm=t_big, tn=D_MODEL)
    qn, kvn = _latent_norm_fwd(z_a, w["q_a_norm"], w["kv_a_norm"], seq)
    qp = _mm(qn, wtq_pad, "nt", "q_up", tm=t_big, tn=D_MODEL)
    kv = _mm(kvn, wt_kv, "nt", "kv_up", tm=t_big, tn=D_MODEL)
    qc, kc, vp = _qk_prep_fwd(qp, kv, z_a, pos, g_q, g_kn, g_kr, inv_freq, seq)
    attn, lse = _attn_fwd(qc, kc, vp, seq)
    br_mla = _mm(attn, wtmla_pad, "nt", "mla_proj", tm=t_mid, tn=D_MODEL)

    def merge(rows, bats, vecs):
        zg, bp, bm = rows
        return [_sigmoid(zg[:, :D_MODEL]) * bp + _sigmoid(zg[:, D_MODEL:]) * bm], [], []
    merged = _rowmap("merge", merge, seq, [z_g, br_pool, br_mla], row_outs=[(D_MODEL, BF16)])[0]
    x2, o_mix = _mm(merged, w_mix_out, "nn", "mix_out", res=x1, gate=pm["gate"], seq=seq,
                    tm=_tile(seq, (1024, 512, 256, 128)), tn=D_MODEL)

    x3, saved2 = _ffn_fwd(x2, p2, seq, "ffn2")
    dy, sq_err = _loss_grad(x3, loss_target.reshape(tokens, D_MODEL), seq)
    loss = lax.psum(0.5 * jnp.sum(sq_err) * (1.0 / D_MODEL), AXES)

    grads = {}
    dx2, ops2, dg_ffn2, dsh2, dsc2, dgate2 = _ffn_bwd_x(dy, saved2, p2, p2["gate"], seq, "ffn2")
    grads["w_ffn2_in"], grads["w_ffn2_out"] = _ffn_bwd_w(ops2, "ffn2")
    s_f2, tok = _scatter_start(grads, ("w_ffn2_in", "w_ffn2_out"), "ffn2")

    def mix_pre(rows, bats, vecs):
        (dv, ov), (gate,) = rows, bats
        return [gate * dv], [_colsum(dv * ov.astype(F32))], []
    do_mix, dgate_m = _rowmap("mix_bwd_pre", mix_pre, seq, [dx2, o_mix], [pm["gate"] + tok[0:1, 0:1]],
                              row_outs=[(D_MODEL, BF16)], bat_outs=[D_MODEL])
    dmerged = _mm(do_mix, w_mix_out, "nt", "mix_bwd_dmerged", tm=t_mid, tn=D_MODEL)
    grads["w_out"] = _mm(merged, do_mix, "tn", "mix_bwd_wout", out_dtype=BF16, tm=512, tn=D_MODEL)

    def merge_bwd(rows, bats, vecs):
        dmv, zg, bp, bm = rows
        s_p, s_m = _sigmoid(zg[:, :D_MODEL]), _sigmoid(zg[:, D_MODEL:])
        dzg = jnp.concatenate([dmv * bp * s_p * (1.0 - s_p), dmv * bm * s_m * (1.0 - s_m)], axis=1)
        return [dmv * s_p, dmv * s_m, dzg], [], []
    dbr_pool, dbr_mla, dz_g = _rowmap("merge_bwd", merge_bwd, seq, [dmerged, z_g, br_pool, br_mla],
                                      row_outs=[(D_MODEL, BF16), (D_MODEL, BF16), (2 * D_MODEL, BF16)])

    grads["w_pool_proj"] = _mm(dbr_pool, ps, "tn", "pool_bwd_wproj", out_dtype=BF16, tm=512, tn=POOL_WIDTH)
    dps = _mm(dbr_pool, wt_pool, "nn", "pool_bwd_dps", tm=t_big, tn=POOL_WIDTH)
    dz_p, dgrp, dpool_scale = _pool_bwd(dps, pooled, pg, w["pool_grp"], w["pool_scale"], seq)

    dwtmla_pad = _mm(dbr_mla, attn, "tn", "mla_bwd_wproj", out_dtype=BF16, tm=512, tn=D_MODEL)
    grads["w_mla_proj"] = dwtmla_pad.reshape(D_MODEL, N_HEADS, HEAD_SLAB)[:, :, 64:].reshape(D_MODEL, -1)
    d_attn = _mm(dbr_mla, wtmla_pad, "nn", "mla_bwd_dattn", out_dtype=BF16, tm=t_mid, tn=D_MODEL)
    dqc, dkc, dvp = _attn_bwd(qc, kc, vp, attn, lse, d_attn, seq)
    dqp, dkv, dkr, dg_q, dg_kn, dg_kr = _qk_prep_bwd(dqc, dkc, dvp, qp, kv, z_a, pos, g_q, g_kn, g_kr, inv_freq, seq)
    dwtq_pad = _mm(dqp, qn, "tn", "q_up_bwd_w", out_dtype=BF16, tm=512, tn=Q_LORA)
    grads["w_q_up"] = dwtq_pad.reshape(N_HEADS, HEAD_SLAB, Q_LORA)[:, :96].reshape(-1, Q_LORA)
    grads["w_kv_up"] = _mm(dkv, kvn, "tn", "kv_up_bwd_w", out_dtype=BF16, tm=512, tn=KV_LORA)
    dqn = _mm(dqp, wtq_pad, "nn", "q_up_bwd_x", tm=t_big, tn=Q_LORA)
    dkvn = _mm(dkv, wt_kv, "nn", "kv_up_bwd_x", tm=t_big, tn=KV_LORA)
    dz_a, dg_qa, dg_kva = _latent_norm_bwd(dqn, dkvn, dkr, z_a, w["q_a_norm"], w["kv_a_norm"], seq)

    dwt_a = _mm(dz_a, h2, "tn", "mix_in_bwd_wa", out_dtype=BF16, tm=256, tn=D_MODEL)
    dwt_p = _mm(dz_p, h2, "tn", "mix_in_bwd_wp", out_dtype=BF16, tm=512, tn=D_MODEL)
    dwt_g = _mm(dz_g, h2, "tn", "mix_in_bwd_wg", out_dtype=BF16, tm=512, tn=D_MODEL)
    grads["w_in"] = jnp.concatenate([dwt_p, dwt_a[:640], dwt_a[704:736], dwt_g], axis=0)
    s_mix, tok = _scatter_start(grads, MIXER, "mix")
    dh2 = _mm(dz_a, wt_a, "nn", "mix_in_bwd_xa", tm=t_mid, tn=D_MODEL)
    dh2 = _mm(dz_p, wt_p, "nn", "mix_in_bwd_xp", tm=t_mid, tn=D_MODEL, add=dh2)
    dh2 = _mm(dz_g, wt_g, "nn", "mix_in_bwd_xg", tm=t_mid, tn=D_MODEL, add=dh2)
    dx1, dsh_m, dsc_m, dg_mix = _norm_mod_bwd(dh2, x1, dx2, pm["gamma"], pm["scale"] + tok[0:1, 0:1], seq, "mix_bwd_norm")

    dx0, ops1, dg_ffn1, dsh1, dsc1, dgate1 = _ffn_bwd_x(dx1, saved1, p1, p1["gate"], seq, "ffn1")

    dmod = jnp.stack([jnp.stack([dsh1, dsc1, 0.5 * dgate1], axis=1),
                      jnp.stack([dsh_m, dsc_m, dgate_m], axis=1),
                      jnp.stack([dsh2, dsc2, 0.5 * dgate2], axis=1)], axis=1)
    dmod_all = _all_gather(dmod.reshape(-1, LANES), "gather_dmod").reshape(N_DEV * nseq, 9 * D_MODEL)
    small_local = [dg_ffn1.reshape(w["norm_ffn1"].shape), dg_mix.reshape(w["norm_mix"].shape),
                   dg_ffn2.reshape(w["norm_ffn2"].shape), dgrp, dpool_scale, dg_qa, dg_kva,
                   dg_q[:, :QK_NOPE], dg_q[:, QK_NOPE:QK_NOPE + QK_ROPE], dg_kn[:, :QK_NOPE],
                   dg_kr[:, QK_NOPE:QK_NOPE + QK_ROPE]]
    small_all = _all_gather(_pack_small(small_local), "gather_small")

    grads["w_ffn1_in"], grads["w_ffn1_out"] = _ffn_bwd_w(ops1, "ffn1")
    s_f1o, tok = _scatter_start(grads, ("w_ffn1_out",), "ffn1_out", after=small_all)
    s_f1i, tok = _scatter_start(grads, ("w_ffn1_in",), "ffn1_in", after=tok)

    dmod_cols = lax.dynamic_slice(dmod_all, (0, me * n_ada), (N_DEV * nseq, n_ada)) + tok[0:1, 0:1]
    g_w_ada, g_b_ada = _ada_grads(c_act, dmod_all, dmod_cols)
    small_sum = _sum_blocks(small_all, "sum_small")
    small = dict(zip(SMALL, _unpack_small(small_sum, [w[n] for n in SMALL])))
    grad_w = dict(small, w_ada=g_w_ada, b_ada=g_b_ada)

    delta, new_m, new_v = {}, {}, {}

    def update(names):
        for n in names:
            delta[n], new_m[n], new_v[n] = _adamw(w[n], grad_w[n], m[n], v[n], f"adamw_{n}")

    update(("w_ada",))
    rep = ("b_ada",) + SMALL
    d_s, m_s, v_s = _adamw(_pack_small([w[n] for n in rep]), _pack_small([grad_w[n] for n in rep]),
                           _pack_small([m[n] for n in rep]), _pack_small([v[n] for n in rep]), "adamw_small")
    like = [w[n] for n in rep]
    for dst, packed in ((delta, d_s), (new_m, m_s), (new_v, v_s)):
        dst.update(zip(rep, _unpack_small(packed, like)))
    grad_w.update(_scatter_wait(s_f2, ("w_ffn2_in", "w_ffn2_out"), "ffn2", after=d_s))
    update(("w_ffn2_in", "w_ffn2_out"))
    grad_w.update(_scatter_wait(s_mix, MIXER, "mix", after=delta["w_ffn2_out"]))
    update(MIXER)
    grad_w.update(_scatter_wait(s_f1o, ("w_ffn1_out",), "ffn1_out", after=delta["w_out"]))
    update(("w_ffn1_out",))
    grad_w.update(_scatter_wait(s_f1i, ("w_ffn1_in",), "ffn1_in", after=delta["w_ffn1_out"]))
    update(("w_ffn1_in",))

    lead = lambda d: [d[n].reshape(shapes[n]) for n in WEIGHTS]
    return (loss, dx0.reshape(x.shape), *lead(grad_w), *lead(delta), *lead(new_m), *lead(new_v))


def kernel(x, c, positions, w_ada, b_ada, norm_ffn1, w_ffn1_in, w_ffn1_out, norm_mix, w_in, pool_grp, pool_scale, w_pool_proj, q_a_norm, w_q_up, kv_a_norm, w_kv_up, q_norm_nope, q_norm_rope, k_norm_nope, k_norm_rope, w_mla_proj, w_out, norm_ffn2, w_ffn2_in, w_ffn2_out, loss_target, m_w_ada, m_b_ada, m_norm_ffn1, m_w_ffn1_in, m_w_ffn1_out, m_norm_mix, m_w_in, m_pool_grp, m_pool_scale, m_w_pool_proj, m_q_a_norm, m_w_q_up, m_kv_a_norm, m_w_kv_up, m_q_norm_nope, m_q_norm_rope, m_k_norm_nope, m_k_norm_rope, m_w_mla_proj, m_w_out, m_norm_ffn2, m_w_ffn2_in, m_w_ffn2_out, v_w_ada, v_b_ada, v_norm_ffn1, v_w_ffn1_in, v_w_ffn1_out, v_norm_mix, v_w_in, v_pool_grp, v_pool_scale, v_w_pool_proj, v_q_a_norm, v_w_q_up, v_kv_a_norm, v_w_kv_up, v_q_norm_nope, v_q_norm_rope, v_k_norm_nope, v_k_norm_rope, v_w_mla_proj, v_w_out, v_norm_ffn2, v_w_ffn2_in, v_w_ffn2_out):
    w = dict(w_ada=w_ada, b_ada=b_ada, norm_ffn1=norm_ffn1, w_ffn1_in=w_ffn1_in, w_ffn1_out=w_ffn1_out, norm_mix=norm_mix, w_in=w_in, pool_grp=pool_grp, pool_scale=pool_scale, w_pool_proj=w_pool_proj, q_a_norm=q_a_norm, w_q_up=w_q_up, kv_a_norm=kv_a_norm, w_kv_up=w_kv_up, q_norm_nope=q_norm_nope, q_norm_rope=q_norm_rope, k_norm_nope=k_norm_nope, k_norm_rope=k_norm_rope, w_mla_proj=w_mla_proj, w_out=w_out, norm_ffn2=norm_ffn2, w_ffn2_in=w_ffn2_in, w_ffn2_out=w_ffn2_out)
    m = dict(w_ada=m_w_ada, b_ada=m_b_ada, norm_ffn1=m_norm_ffn1, w_ffn1_in=m_w_ffn1_in, w_ffn1_out=m_w_ffn1_out, norm_mix=m_norm_mix, w_in=m_w_in, pool_grp=m_pool_grp, pool_scale=m_pool_scale, w_pool_proj=m_w_pool_proj, q_a_norm=m_q_a_norm, w_q_up=m_w_q_up, kv_a_norm=m_kv_a_norm, w_kv_up=m_w_kv_up, q_norm_nope=m_q_norm_nope, q_norm_rope=m_q_norm_rope, k_norm_nope=m_k_norm_nope, k_norm_rope=m_k_norm_rope, w_mla_proj=m_w_mla_proj, w_out=m_w_out, norm_ffn2=m_norm_ffn2, w_ffn2_in=m_w_ffn2_in, w_ffn2_out=m_w_ffn2_out)
    v = dict(w_ada=v_w_ada, b_ada=v_b_ada, norm_ffn1=v_norm_ffn1, w_ffn1_in=v_w_ffn1_in, w_ffn1_out=v_w_ffn1_out, norm_mix=v_norm_mix, w_in=v_w_in, pool_grp=v_pool_grp, pool_scale=v_pool_scale, w_pool_proj=v_w_pool_proj, q_a_norm=v_q_a_norm, w_q_up=v_w_q_up, kv_a_norm=v_kv_a_norm, w_kv_up=v_w_kv_up, q_norm_nope=v_q_norm_nope, q_norm_rope=v_q_norm_rope, k_norm_nope=v_k_norm_nope, k_norm_rope=v_k_norm_rope, w_mla_proj=v_w_mla_proj, w_out=v_w_out, norm_ffn2=v_norm_ffn2, w_ffn2_in=v_w_ffn2_in, w_ffn2_out=v_w_ffn2_out)
    return _step(x, c, positions, w, m, v, loss_target)
```

```python
import functools
import math

import jax
import jax.numpy as jnp
from jax import lax
from jax.experimental import pallas as pl
from jax.experimental.pallas import tpu as pltpu

F32 = jnp.float32
BF16 = jnp.bfloat16
MESH = pl.DeviceIdType.MESH
AXES = ("x", "y", "c")
N_DEV = 8

D_MODEL = 1024
D_FF = 2816
N_HEADS = 8
HEAD_SLAB = 128
QK_NOPE = 64
QK_ROPE = 32
POOL_WIDTH = 512
POOL_GROUPS = 4
POOL_GROUP_DIM = 128
Q_LORA = 384
KV_LORA = 256
ROPE_THETA = 10000.0
ATTN_SCALE = 1.0 / math.sqrt(QK_NOPE + QK_ROPE)
NORM_EPS = 1e-6
ADAM_LR, ADAM_B1, ADAM_B2, ADAM_EPS, ADAM_WD, ADAM_STEP = 0.001, 0.9, 0.999, 1e-08, 0.01, 10

LANES = 128
SUBLANES = 8
VMEM_LIMIT = 52 * 1024 * 1024
ADAMW_WHOLE_BYTES = 3 << 19

BIG = ("w_ffn1_in", "w_ffn1_out", "w_in", "w_pool_proj", "w_q_up", "w_kv_up",
       "w_mla_proj", "w_out", "w_ffn2_in", "w_ffn2_out")
ROW_SHARDED = ("w_ffn1_out", "w_out", "w_ffn2_out")
MIXER = ("w_in", "w_pool_proj", "w_q_up", "w_kv_up", "w_mla_proj", "w_out")
SMALL = ("norm_ffn1", "norm_mix", "norm_ffn2", "pool_grp", "pool_scale", "q_a_norm",
         "kv_a_norm", "q_norm_nope", "q_norm_rope", "k_norm_nope", "k_norm_rope")
WEIGHTS = ("w_ada", "b_ada", "norm_ffn1", "w_ffn1_in", "w_ffn1_out", "norm_mix", "w_in",
           "pool_grp", "pool_scale", "w_pool_proj", "q_a_norm", "w_q_up", "kv_a_norm",
           "w_kv_up", "q_norm_nope", "q_norm_rope", "k_norm_nope", "k_norm_rope",
           "w_mla_proj", "w_out", "norm_ffn2", "w_ffn2_in", "w_ffn2_out")


def _params(*sem):
    return pltpu.CompilerParams(dimension_semantics=sem, vmem_limit_bytes=VMEM_LIMIT)


def _tile(n, cands):
    for c in cands:
        if n % c == 0:
            return c
    return n


def _my_pos():
    return lax.axis_index("x"), lax.axis_index("y"), lax.axis_index("c")


def _flip(pos, k):
    x, y, c = pos
    fx, fy, fc = (k >> 2) & 1, (k >> 1) & 1, k & 1
    return ((1 - x) if fx else x, (1 - y) if fy else y, (1 - c) if fc else c)


def _index(pos):
    x, y, c = pos
    return 4 * x + 2 * y + c


def _exchange(arrays, name, scatter=False):
    n = len(arrays)

    def body(*refs):
        ins, outs = refs[:n], refs[n:2 * n]
        send_sems, recv_sems, local_sems = refs[2 * n:]
        me = _my_pos()
        mine, sends = [], []
        for a in range(n):
            own = ins[a].at[_index(me)] if scatter else ins[a]
            cp = pltpu.make_async_copy(own, outs[a].at[_index(me)], local_sems.at[a])
            cp.start()
            mine.append(cp)
        for k in range(1, N_DEV):
            peer = _flip(me, k)
            for a in range(n):
                cp = pltpu.make_async_remote_copy(
                    src_ref=ins[a].at[_index(peer)] if scatter else ins[a],
                    dst_ref=outs[a].at[_index(me)],
                    send_sem=send_sems.at[a, k - 1], recv_sem=recv_sems.at[a, k - 1],
                    device_id=peer, device_id_type=MESH)
                cp.start()
                sends.append(cp)
        for k in range(1, N_DEV):
            peer = _flip(me, k)
            for a in range(n):
                pltpu.make_async_remote_copy(
                    src_ref=ins[a].at[_index(me)] if scatter else ins[a],
                    dst_ref=outs[a].at[_index(peer)],
                    send_sem=send_sems.at[a, k - 1], recv_sem=recv_sems.at[a, k - 1],
                    device_id=peer, device_id_type=MESH).wait_recv()
        for cp in sends:
            cp.wait_send()
        for cp in mine:
            cp.wait()

    shape = lambda x: x.shape if scatter else (N_DEV,) + x.shape
    return pl.pallas_call(
        body, name=name,
        out_shape=tuple(jax.ShapeDtypeStruct(shape(x), x.dtype) for x in arrays),
        in_specs=[pl.BlockSpec(memory_space=pl.ANY)] * n,
        out_specs=tuple(pl.BlockSpec(memory_space=pl.ANY) for _ in arrays),
        scratch_shapes=[pltpu.SemaphoreType.DMA((n, N_DEV - 1)),
                        pltpu.SemaphoreType.DMA((n, N_DEV - 1)),
                        pltpu.SemaphoreType.DMA((n,))],
    )(*arrays)


def _all_gather(x, name):
    return _exchange([x], name)[0]


_HBM = pl.BlockSpec(memory_space=pltpu.HBM)
_SEM = pl.BlockSpec(memory_space=pltpu.SEMAPHORE)
_ANY = pl.BlockSpec(memory_space=pl.ANY)
_EFFECT = pltpu.SideEffectType.DATAFLOW_SIDE_EFFECTING


def _split_copy(ins, lands, send_sems, recv_sems, a, k, me, scatter, incoming):
    peer = _flip(me, k)
    block = me if incoming else peer
    return pltpu.make_async_remote_copy(
        src_ref=ins[a].at[_index(block)] if scatter else ins[a],
        dst_ref=lands[a].at[_index(peer if incoming else me)],
        send_sem=send_sems.at[a * (N_DEV - 1) + k - 1], recv_sem=recv_sems.at[a * (N_DEV - 1) + k - 1],
        device_id=peer, device_id_type=MESH)


def _exchange_start(arrays, name, scatter=False, after=None):
    n = len(arrays)
    after = jnp.zeros((SUBLANES, LANES), F32) if after is None else after

    def body(*refs):
        ins, lands = refs[:n], refs[n:2 * n]
        send_sems, recv_sems = refs[2 * n + 1], refs[2 * n + 2]
        me = _my_pos()
        for k in range(1, N_DEV):
            for a in range(n):
                _split_copy(ins, lands, send_sems, recv_sems, a, k, me, scatter, False).start()
        refs[-1][...] = jnp.zeros((SUBLANES, LANES), F32)

    shape = lambda x: x.shape if scatter else (N_DEV,) + x.shape
    hbm = lambda x: pltpu.with_memory_space_constraint(x, pltpu.HBM)
    srcs = [hbm(x) for x in arrays]
    zones = [hbm(lax.empty(shape(x), x.dtype)) for x in arrays]
    out = pl.pallas_call(
        body, name=name,
        out_shape=(pltpu.SemaphoreType.DMA((n * (N_DEV - 1),)), pltpu.SemaphoreType.DMA((n * (N_DEV - 1),)),
                   *[pltpu.HBM(x.shape, x.dtype) for x in srcs + zones],
                   jax.ShapeDtypeStruct((SUBLANES, LANES), F32)),
        in_specs=[_HBM] * (2 * n) + [_ANY],
        out_specs=(_SEM, _SEM, *[_HBM] * (2 * n), pl.BlockSpec(memory_space=pltpu.VMEM)),
        input_output_aliases={i: 2 + i for i in range(2 * n)},
        compiler_params=pltpu.CompilerParams(has_side_effects=_EFFECT),
    )(*srcs, *zones, after)
    return out[:-1], out[-1]


def _exchange_wait(handle, name, scatter=False, after=None):
    send_sems, recv_sems = handle[0], handle[1]
    n = (len(handle) - 2) // 2
    after = jnp.zeros((SUBLANES, LANES), F32) if after is None else after

    def body(*refs):
        ins, lands = refs[:n], refs[n:2 * n]
        send, recv = refs[2 * n], refs[2 * n + 1]
        me = _my_pos()
        for k in range(1, N_DEV):
            for a in range(n):
                _split_copy(ins, lands, send, recv, a, k, me, scatter, False).wait_send()
                _split_copy(ins, lands, send, recv, a, k, me, scatter, True).wait_recv()

    bufs = handle[2:]
    out = pl.pallas_call(
        body, name=name,
        out_shape=tuple(pltpu.HBM(x.shape, x.dtype) for x in bufs),
        in_specs=[_HBM] * (2 * n) + [_SEM, _SEM, _ANY],
        out_specs=tuple([_HBM] * (2 * n)),
        input_output_aliases={i: i for i in range(2 * n)},
        compiler_params=pltpu.CompilerParams(has_side_effects=_EFFECT),
    )(*bufs, send_sems, recv_sems, after)
    me = _index(_my_pos())
    landed = []
    for src, land in zip(out[:n], out[n:]):
        own = lax.dynamic_slice_in_dim(src, me, 1, axis=0) if scatter else src[None]
        landed.append(lax.dynamic_update_slice_in_dim(land, own, me, axis=0))
    return landed


def _sum_blocks(x, name):
    n, rows, cols = x.shape
    tr = _tile(rows, (512, 256, 128, 64, 32, 16, 8))

    def body(x_ref, o_ref):
        acc = x_ref[0].astype(F32)
        for d in range(1, n):
            acc = acc + x_ref[d].astype(F32)
        o_ref[...] = acc

    return pl.pallas_call(
        body, name=name,
        out_shape=jax.ShapeDtypeStruct((rows, cols), F32),
        grid=(rows // tr,),
        in_specs=[pl.BlockSpec((n, tr, cols), lambda i: (0, i, 0))],
        out_specs=pl.BlockSpec((tr, cols), lambda i: (i, 0)),
        compiler_params=_params("parallel"),
    )(x)


_DIMS = {"nn": (((1,), (0,)), ((), ())), "nt": (((1,), (1,)), ((), ())), "tn": (((0,), (0,)), ((), ()))}


def _mm(a, b, mode, name, out_dtype=F32, tm=None, tn=None, add=None, res=None, gate=None, seq=None):
    if mode == "tn":
        kdim, m = a.shape
    else:
        m, kdim = a.shape
    n = b.shape[0] if mode == "nt" else b.shape[1]
    tm = tm or _tile(m, (512, 256, 128))
    tn = tn or _tile(n, (512, 256, 128))
    if res is not None:
        assert seq % tm == 0
    dims = _DIMS[mode]

    def body(*refs):
        a_ref, b_ref = refs[0], refs[1]
        acc = lax.dot_general(a_ref[...].astype(BF16), b_ref[...].astype(BF16), dims,
                              preferred_element_type=F32)
        if add is not None:
            refs[3][...] = (acc + refs[2][...]).astype(out_dtype)
        elif res is not None:
            res_ref, gate_ref, o_ref, p_ref = refs[2:]
            o_ref[...] = res_ref[...] + gate_ref[0] * acc
            p_ref[...] = acc.astype(BF16)
        else:
            refs[2][...] = acc.astype(out_dtype)

    a_spec = (pl.BlockSpec((kdim, tm), lambda i, j: (0, i)) if mode == "tn"
              else pl.BlockSpec((tm, kdim), lambda i, j: (i, 0)))
    b_spec = (pl.BlockSpec((tn, kdim), lambda i, j: (j, 0)) if mode == "nt"
              else pl.BlockSpec((kdim, tn), lambda i, j: (0, j)))
    o_spec = pl.BlockSpec((tm, tn), lambda i, j: (i, j))
    in_specs, args = [a_spec, b_spec], [a, b]
    out_shape, out_specs = jax.ShapeDtypeStruct((m, n), out_dtype), o_spec
    if add is not None:
        in_specs.append(o_spec)
        args.append(add)
    if res is not None:
        per_seq = seq // tm
        in_specs += [o_spec, pl.BlockSpec((1, 1, tn), lambda i, j: (i // per_seq, 0, j))]
        args += [res, gate]
        out_shape = (jax.ShapeDtypeStruct((m, n), F32), jax.ShapeDtypeStruct((m, n), BF16))
        out_specs = (o_spec, o_spec)
    return pl.pallas_call(
        body, name=name, out_shape=out_shape, grid=(m // tm, n // tn),
        in_specs=in_specs, out_specs=out_specs,
        compiler_params=_params("parallel", "parallel"),
    )(*args)


def _rowmap(name, fn, seq, rows, bats=(), vecs=(), row_outs=(), bat_outs=(), vec_outs=(), ts=None, mm=None):
    rows = [r if isinstance(r, tuple) else (r, r.shape[1], 0) for r in rows]
    tokens = rows[0][0].shape[0]
    nseq = tokens // seq
    ts = ts or _tile(seq, (256, 128, 64, 32, 16, 8))
    nt = seq // ts
    n_r, n_b, n_v = len(rows), len(bats), len(vecs)
    n_ro, n_bo = len(row_outs), len(bat_outs)

    def accumulate(ref, val, first):
        @pl.when(first)
        def _():
            ref[...] = val.reshape(ref.shape)

        @pl.when(jnp.logical_not(first))
        def _():
            ref[...] += val.reshape(ref.shape)

    def body(*refs):
        n_in = n_r + n_b + n_v + (mm is not None)
        ins, outs = refs[:n_in], refs[n_in:]
        r_vals = [r[...] for r in ins[:n_r]]
        b_vals = [r[0] for r in ins[n_r:n_r + n_b]]
        v_vals = [r[...] for r in ins[n_r + n_b:n_r + n_b + n_v]]
        if mm is not None:
            r_vals[0] = lax.dot_general(r_vals[0].astype(BF16), ins[-1][...].astype(BF16), _DIMS[mm[1]],
                                        preferred_element_type=F32)
        ro, bo, vo = fn(r_vals, b_vals, v_vals)
        for ref, val in zip(outs[:n_ro], ro):
            ref[...] = val.astype(ref.dtype)
        b, i = pl.program_id(0), pl.program_id(1)
        for ref, val in zip(outs[n_ro:n_ro + n_bo], bo):
            accumulate(ref, val, i == 0)
        for ref, val in zip(outs[n_ro + n_bo:], vo):
            accumulate(ref, val, jnp.logical_and(i == 0, b == 0))

    in_specs = [pl.BlockSpec((ts, w), functools.partial(lambda b, i, cb: (b * nt + i, cb), cb=cb))
                for _, w, cb in rows]
    in_specs += [pl.BlockSpec((1, 1, v.shape[2]), lambda b, i: (b, 0, 0)) for v in bats]
    in_specs += [pl.BlockSpec((1, v.shape[1]), lambda b, i: (0, 0)) for v in vecs]
    extra = []
    if mm is not None:
        in_specs.append(pl.BlockSpec(mm[0].shape, lambda b, i: (0, 0)))
        extra.append(mm[0])
    out_shape = [jax.ShapeDtypeStruct((tokens, f), dt) for f, dt in row_outs]
    out_specs = [pl.BlockSpec((ts, f), lambda b, i: (b * nt + i, 0)) for f, _ in row_outs]
    out_shape += [jax.ShapeDtypeStruct((nseq, 1, f), F32) for f in bat_outs]
    out_specs += [pl.BlockSpec((1, 1, f), lambda b, i: (b, 0, 0)) for f in bat_outs]
    out_shape += [jax.ShapeDtypeStruct((1, f), F32) for f in vec_outs]
    out_specs += [pl.BlockSpec((1, f), lambda b, i: (0, 0)) for f in vec_outs]
    return pl.pallas_call(
        body, name=name, out_shape=tuple(out_shape), grid=(nseq, nt),
        in_specs=in_specs, out_specs=tuple(out_specs),
        compiler_params=_params("arbitrary", "arbitrary"),
    )(*([r[0] for r in rows] + list(bats) + list(vecs) + extra))


def _colsum(v):
    return jnp.sum(v, axis=0, keepdims=True)


def _rstd(x, width=None):
    width = width or x.shape[-1]
    return lax.rsqrt(jnp.sum(x * x, axis=-1, keepdims=True) * (1.0 / width) + NORM_EPS)


def _norm_bwd(dy, x, r, g, width=None):
    width = width or x.shape[-1]
    xhat = x * r
    dxhat = dy * g
    dx = r * (dxhat - xhat * (jnp.sum(dxhat * xhat, axis=-1, keepdims=True) * (1.0 / width)))
    return dx, dy * xhat


def _sigmoid(x):
    return 1.0 / (1.0 + jnp.exp(-x))


def _norm_mod(xv, g, sh, sc):
    return xv * _rstd(xv) * g * (1.0 + sc) + sh


def _norm_mod_fwd(x, p, seq, name):
    def fn(rows, bats, vecs):
        return [_norm_mod(rows[0], vecs[0], bats[0], bats[1])], [], []
    return _rowmap(name, fn, seq, [x], [p["shift"], p["scale"]], [p["gamma"]], row_outs=[(D_MODEL, BF16)])[0]


def _norm_mod_bwd(dh, x, dres, p, seq, name, prev=None):
    def fn(rows, bats, vecs):
        dhv, xv, dr = rows[:3]
        sc, g = bats[0], vecs[0]
        r = _rstd(xv)
        dxn, dg = _norm_bwd(dhv * (1.0 + sc), xv, r, g)
        dx = dr + dxn
        ro, bo = [dx], [_colsum(dhv), _colsum(dhv * (xv * r * g))]
        if prev is not None:
            ro.append(bats[1] * dx)
            bo.append(_colsum(dx * rows[3].astype(F32)))
        return ro, bo, [_colsum(dg)]
    more = prev is not None
    return _rowmap(name, fn, seq, [dh, x, dres] + ([prev[0]] if more else []),
                   [p["scale"]] + ([prev[1]] if more else []), [p["gamma"]],
                   row_outs=[(D_MODEL, F32)] + ([(D_MODEL, BF16)] if more else []),
                   bat_outs=[D_MODEL] * (3 if more else 2), vec_outs=[D_MODEL])


def _ffn_in_act(h, wt_in, name):
    tokens = h.shape[0]
    tm, tn = _tile(tokens, (2048, 1024, 512)), 256
    nj = D_FF // tn

    def body(h_ref, wg_ref, wu_ref, g_ref, u_ref, a_ref):
        hv = h_ref[...]
        g = lax.dot_general(hv, wg_ref[...], _DIMS["nt"], preferred_element_type=F32)
        u = lax.dot_general(hv, wu_ref[...], _DIMS["nt"], preferred_element_type=F32)
        g_ref[...] = g.astype(BF16)
        u_ref[...] = u.astype(BF16)
        a_ref[...] = (g * _sigmoid(g) * u).astype(BF16)

    o_spec = pl.BlockSpec((tm, tn), lambda i, j: (i, j))
    out = jax.ShapeDtypeStruct((tokens, D_FF), BF16)
    return pl.pallas_call(
        body, name=name, grid=(tokens // tm, nj), out_shape=(out, out, out),
        in_specs=[pl.BlockSpec((tm, D_MODEL), lambda i, j: (i, 0)),
                  pl.BlockSpec((tn, D_MODEL), lambda i, j: (j, 0)),
                  pl.BlockSpec((tn, D_MODEL), lambda i, j: (j + nj, 0))],
        out_specs=(o_spec, o_spec, o_spec),
        compiler_params=_params("parallel", "parallel"),
    )(h, wt_in, wt_in)


def _out_residual(a, w_out, res, gate, nxt, seq, name):
    def fn(rows, bats, vecs):
        acc, rv = rows
        x_new = rv + bats[0] * acc
        return [x_new, acc, _norm_mod(x_new, vecs[0], bats[1], bats[2])], [], []
    return _rowmap(name, fn, seq, [a, res], [gate, nxt["shift"], nxt["scale"]], [nxt["gamma"]],
                   row_outs=[(D_MODEL, F32), (D_MODEL, BF16), (D_MODEL, BF16)],
                   ts=_tile(seq, (512, 256, 128)), mm=(w_out, "nn"))


def _out_loss(a, w_out, res, gate, target, seq, name):
    def fn(rows, bats, vecs):
        acc, rv, tv = rows
        err = rv + bats[0] * acc - tv
        dy = err * (1.0 / D_MODEL)
        return [dy, bats[0] * dy], [_colsum(dy * acc)], [_colsum(err * err)]
    return _rowmap(name, fn, seq, [a, res, target], [gate], row_outs=[(D_MODEL, F32), (D_MODEL, BF16)],
                   bat_outs=[D_MODEL], vec_outs=[D_MODEL], ts=_tile(seq, (512, 256, 128)), mm=(w_out, "nn"))


def _ffn_bwd_x(df, dres, saved, p, seq, tag, prev=None):
    x, h, g, u, a, w_in, w_out = saved
    tokens = x.shape[0]

    def act_bwd(rows, bats, vecs):
        dav, gv, uv = rows[0], rows[1].astype(F32), rows[2].astype(F32)
        sg = _sigmoid(gv)
        silu = gv * sg
        dg = dav * uv * (sg * (1.0 + gv * (1.0 - sg)))
        return [jnp.concatenate([dg, dav * silu], axis=1)], [], []
    dgu = _rowmap(f"{tag}_bwd_da", act_bwd, seq, [df, g, u], row_outs=[(2 * D_FF, BF16)],
                  ts=_tile(seq, (256, 128)), mm=(w_out, "nt"))[0]
    dh = _mm(dgu, w_in, "nn", f"{tag}_bwd_dh", tm=_tile(tokens, (512, 256)), tn=D_MODEL)
    return _norm_mod_bwd(dh, x, dres, p, seq, f"{tag}_bwd_norm", prev), (a, df, dgu, h)


def _ffn_bwd_wout(operands, tag):
    a, df, _, _ = operands
    return _mm(a, df, "tn", f"{tag}_bwd_wout", out_dtype=BF16, tm=256, tn=D_MODEL)


def _ffn_bwd_win(operands, tag):
    _, _, dgu, h = operands
    return _mm(dgu, h, "tn", f"{tag}_bwd_win", out_dtype=BF16, tm=512, tn=D_MODEL)


def _shift_rows(v, k, forward):
    n = v.shape[0]
    row = lax.broadcasted_iota(jnp.int32, v.shape, 0)
    if forward:
        return jnp.where(row >= k, pltpu.roll(v, k, 0), 0.0)
    return jnp.where(row < n - k, pltpu.roll(v, n - k, 0), 0.0)


def _window_sums(v, forward):
    out, s, k = [], v, 1
    for _ in range(POOL_GROUPS):
        s = s + _shift_rows(s, k, forward)
        out.append(s)
        k *= 2
    return out


def _by_group(vals, g):
    out = vals[-1]
    for idx in range(len(vals) - 2, -1, -1):
        out = jnp.where(g == idx, vals[idx], out)
    return out


def _inv_count(shape, g):
    t1 = lax.broadcasted_iota(jnp.int32, shape, 0) + 1
    window = _by_group([jnp.int32(2 ** (i + 1)) for i in range(POOL_GROUPS)], g)
    return 1.0 / jnp.minimum(t1, window).astype(F32)


def _pool_fwd(u, grp, scale, seq):
    tokens = u.shape[0]

    def body(u_ref, grp_ref, sc_ref, pooled_ref, pg_ref, ps_ref):
        g = pl.program_id(1)
        uv = u_ref[...]
        sums = _by_group(_window_sums(uv, True), g)
        pooled = (sums * _inv_count(uv.shape, g) - uv).astype(BF16)
        pg = jnp.dot(pooled, grp_ref[0].astype(BF16), preferred_element_type=F32)
        pooled_ref[...] = pooled
        pg_ref[...] = pg
        ps_ref[...] = (pg * sc_ref[...]).astype(BF16)

    blk = pl.BlockSpec((seq, POOL_GROUP_DIM), lambda b, g: (b, g))
    return pl.pallas_call(
        body, name="pool_fwd", grid=(tokens // seq, POOL_GROUPS),
        out_shape=(jax.ShapeDtypeStruct(u.shape, BF16), jax.ShapeDtypeStruct(u.shape, F32),
                   jax.ShapeDtypeStruct(u.shape, BF16)),
        in_specs=[blk, pl.BlockSpec((1, POOL_GROUP_DIM, POOL_GROUP_DIM), lambda b, g: (g, 0, 0)),
                  pl.BlockSpec((1, POOL_GROUP_DIM), lambda b, g: (0, g))],
        out_specs=(blk, blk, blk),
        compiler_params=_params("parallel", "parallel"),
    )(u, grp, scale)


def _pool_bwd(dps, pooled, pg, grp, scale, seq):
    tokens = dps.shape[0]

    def body(dps_ref, pooled_ref, pg_ref, grp_ref, sc_ref, du_ref, dgrp_ref, dsc_ref):
        g, b = pl.program_id(0), pl.program_id(1)
        dpsv = dps_ref[...]
        dpg = (dpsv * sc_ref[...]).astype(BF16)
        dsc = _colsum(dpsv * pg_ref[...])
        dgrp = lax.dot_general(pooled_ref[...], dpg, _DIMS["tn"], preferred_element_type=F32)

        @pl.when(b == 0)
        def _():
            dsc_ref[...] = dsc
            dgrp_ref[0] = dgrp

        @pl.when(b > 0)
        def _():
            dsc_ref[...] += dsc
            dgrp_ref[0] += dgrp

        dpool = lax.dot_general(dpg, grp_ref[0].astype(BF16), _DIMS["nt"], preferred_element_type=F32)
        sums = _by_group(_window_sums(dpool * _inv_count(dpool.shape, g), False), g)
        du_ref[...] = (sums - dpool).astype(BF16)

    blk = pl.BlockSpec((seq, POOL_GROUP_DIM), lambda g, b: (b, g))
    grp_spec = pl.BlockSpec((1, POOL_GROUP_DIM, POOL_GROUP_DIM), lambda g, b: (g, 0, 0))
    vec_spec = pl.BlockSpec((1, POOL_GROUP_DIM), lambda g, b: (0, g))
    return pl.pallas_call(
        body, name="pool_bwd", grid=(POOL_GROUPS, tokens // seq),
        out_shape=(jax.ShapeDtypeStruct(dps.shape, BF16), jax.ShapeDtypeStruct(grp.shape, F32),
                   jax.ShapeDtypeStruct(scale.shape, F32)),
        in_specs=[blk, blk, blk, grp_spec, vec_spec],
        out_specs=(blk, grp_spec, vec_spec),
        compiler_params=_params("arbitrary", "arbitrary"),
    )(dps, pooled, pg, grp, scale)


def _lane(shape):
    return lax.broadcasted_iota(jnp.int32, shape, len(shape) - 1)


def _rot(y):
    lane = _lane(y.shape)
    r = jnp.where(lane < QK_NOPE + QK_ROPE // 2,
                  -pltpu.roll(y, HEAD_SLAB - QK_ROPE // 2, 1), pltpu.roll(y, QK_ROPE // 2, 1))
    return jnp.where(jnp.logical_and(lane >= QK_NOPE, lane < QK_NOPE + QK_ROPE), r, 0.0)


def _part_rstd(x):
    sq = x * x
    nope = _lane(x.shape) < QK_NOPE
    s_nope = jnp.sum(jnp.where(nope, sq, 0.0), axis=-1, keepdims=True)
    s_rope = jnp.sum(sq, axis=-1, keepdims=True) - s_nope
    return jnp.where(nope, lax.rsqrt(s_nope * (1.0 / QK_NOPE) + NORM_EPS),
                     lax.rsqrt(s_rope * (1.0 / QK_ROPE) + NORM_EPS))


def _part_norm_bwd(dy, x, r, g):
    nope = _lane(x.shape) < QK_NOPE
    xhat = x * r
    dxhat = dy * g
    prod = dxhat * xhat
    m_nope = jnp.sum(jnp.where(nope, prod, 0.0), axis=-1, keepdims=True)
    m_rope = jnp.sum(prod, axis=-1, keepdims=True) - m_nope
    mean = jnp.where(nope, m_nope * (1.0 / QK_NOPE), m_rope * (1.0 / QK_ROPE))
    return r * (dxhat - xhat * mean), dy * xhat


def _latent_norm_fwd(z_a, g_q, g_kv, seq):
    def fn(rows, bats, vecs):
        q, kv = rows[0][:, :Q_LORA], rows[0][:, Q_LORA:Q_LORA + KV_LORA]
        return [q * _rstd(q) * vecs[0], kv * _rstd(kv) * vecs[1]], [], []
    return _rowmap("latent_norm", fn, seq, [z_a], vecs=[g_q, g_kv],
                   row_outs=[(Q_LORA, BF16), (KV_LORA, BF16)])


def _latent_norm_bwd(dqn, dkvn, dkr, z_a, g_q, g_kv, seq):
    def fn(rows, bats, vecs):
        dq, dkv, dkrv, z = rows
        q, kv = z[:, :Q_LORA], z[:, Q_LORA:Q_LORA + KV_LORA]
        dxq, dgq = _norm_bwd(dq, q, _rstd(q), vecs[0])
        dxkv, dgkv = _norm_bwd(dkv, kv, _rstd(kv), vecs[1])
        return [jnp.concatenate([dxq, dxkv, dkrv], axis=1)], [], [_colsum(dgq), _colsum(dgkv)]
    return _rowmap("latent_norm_bwd", fn, seq, [dqn, dkvn, dkr, z_a], vecs=[g_q, g_kv],
                   row_outs=[(Q_LORA + KV_LORA + HEAD_SLAB, BF16)], vec_outs=[Q_LORA, KV_LORA])


def _qk_prep_fwd(qp, kv, z_a, pos, g_q, g_kn, g_kr, inv_freq, seq):
    def fn(rows, bats, vecs):
        qv, kvv, kr, p = rows
        gq, gkn, gkr, invf = vecs
        ang = p * invf
        cos, sin = jnp.cos(ang), jnp.sin(ang)
        nope = _lane(kr.shape) < QK_NOPE
        krn = kr * _rstd(kr, QK_ROPE) * gkr
        krr = krn * cos + _rot(krn) * sin
        qs, ks, vs = [], [], []
        for h in range(N_HEADS):
            xq = qv[:, h * HEAD_SLAB:(h + 1) * HEAD_SLAB]
            y = xq * _part_rstd(xq) * gq
            qs.append(y * cos + _rot(y) * sin)
            xk = kvv[:, h * HEAD_SLAB:(h + 1) * HEAD_SLAB]
            kn = jnp.where(nope, xk, 0.0)
            ks.append(jnp.where(nope, kn * _rstd(kn, QK_NOPE) * gkn, krr))
            vs.append(jnp.where(nope, 0.0, xk))
        return [jnp.concatenate(v, axis=1) for v in (qs, ks, vs)], [], []
    width = N_HEADS * HEAD_SLAB
    return _rowmap("qk_prep", fn, seq, [qp, kv, (z_a, HEAD_SLAB, 5), pos], vecs=[g_q, g_kn, g_kr, inv_freq],
                   row_outs=[(width, BF16)] * 3, ts=_tile(seq, (128, 64, 32, 16, 8)))


def _qk_prep_bwd(dqc, dkc, dvp, qp, kv, z_a, pos, g_q, g_kn, g_kr, inv_freq, seq):
    def fn(rows, bats, vecs):
        dq, dk, dv, qv, kvv, kr, p = rows
        gq, gkn, gkr, invf = vecs
        ang = p * invf
        cos, sin = jnp.cos(ang), jnp.sin(ang)
        nope = _lane(kr.shape) < QK_NOPE
        dqs, dkvs = [], []
        dgq = jnp.zeros((1, HEAD_SLAB), F32)
        dgkn = jnp.zeros((1, HEAD_SLAB), F32)
        dkrr = jnp.zeros(kr.shape, F32)
        for h in range(N_HEADS):
            sl = slice(h * HEAD_SLAB, (h + 1) * HEAD_SLAB)
            dyr = dq[:, sl]
            dy = dyr * cos - _rot(dyr * sin)
            xq = qv[:, sl]
            dx, dg = _part_norm_bwd(dy, xq, _part_rstd(xq), gq)
            dqs.append(dx)
            dgq = dgq + _colsum(dg)
            dkh = dk[:, sl]
            dkrr = dkrr + jnp.where(nope, 0.0, dkh)
            kn = jnp.where(nope, kvv[:, sl], 0.0)
            dxk, dgk = _norm_bwd(jnp.where(nope, dkh, 0.0), kn, _rstd(kn, QK_NOPE), gkn, QK_NOPE)
            dgkn = dgkn + _colsum(dgk)
            dkvs.append(jnp.where(nope, dxk, dv[:, sl]))
        dkrn = dkrr * cos - _rot(dkrr * sin)
        dkr, dgkr = _norm_bwd(dkrn, kr, _rstd(kr, QK_ROPE), gkr, QK_ROPE)
        return ([jnp.concatenate(dqs, axis=1), jnp.concatenate(dkvs, axis=1), dkr], [],
                [dgq, dgkn, _colsum(dgkr)])
    width = N_HEADS * HEAD_SLAB
    return _rowmap("qk_prep_bwd", fn, seq, [dqc, dkc, dvp, qp, kv, (z_a, HEAD_SLAB, 5), pos],
                   vecs=[g_q, g_kn, g_kr, inv_freq],
                   row_outs=[(width, BF16), (width, BF16), (HEAD_SLAB, F32)],
                   vec_outs=[HEAD_SLAB] * 3, ts=_tile(seq, (128, 64, 32, 16, 8)))


def _scores(q, k_ref, keys, tq):
    s = lax.dot_general(q, k_ref[0:keys, :], _DIMS["nt"], preferred_element_type=F32) * ATTN_SCALE
    row = lax.broadcasted_iota(jnp.int32, (tq, tq), 0)
    col = lax.broadcasted_iota(jnp.int32, (tq, tq), 1)
    diag = jnp.where(col <= row, s[:, keys - tq:], -1e30)
    return diag if keys == tq else jnp.concatenate([s[:, :keys - tq], diag], axis=1)


def _attn_fwd(qc, kc, vp, seq):
    tokens = qc.shape[0]
    tq = _tile(seq, (256, 128))
    nq = seq // tq

    def body(q_ref, k_ref, v_ref, o_ref, lse_ref):
        for i in range(nq):
            rows, keys = slice(i * tq, (i + 1) * tq), (i + 1) * tq
            s = _scores(q_ref[rows, :], k_ref, keys, tq)
            m = jnp.max(s, axis=-1, keepdims=True)
            p = jnp.exp(s - m)
            l = jnp.sum(p, axis=-1, keepdims=True)
            acc = jnp.dot(p.astype(BF16), v_ref[0:keys, :], preferred_element_type=F32)
            o_ref[rows, :] = (acc / l).astype(BF16)
            lse_ref[rows, :] = jnp.broadcast_to(m + jnp.log(l), (tq, HEAD_SLAB))

    spec = pl.BlockSpec((seq, HEAD_SLAB), lambda b, h: (b, h))
    return pl.pallas_call(
        body, name="attn_fwd", grid=(tokens // seq, N_HEADS),
        out_shape=(jax.ShapeDtypeStruct(qc.shape, BF16), jax.ShapeDtypeStruct(qc.shape, F32)),
        in_specs=[spec] * 3, out_specs=(spec, spec),
        compiler_params=_params("parallel", "parallel"),
    )(qc, kc, vp)


def _attn_bwd(qc, kc, vp, o, lse, do, seq):
    tokens = qc.shape[0]
    tq = _tile(seq, (256, 128))
    nq = seq // tq

    def body(q_ref, k_ref, v_ref, o_ref, lse_ref, do_ref, dq_ref, dk_ref, dv_ref):
        dk_ref[...] = jnp.zeros(dk_ref.shape, F32)
        dv_ref[...] = jnp.zeros(dv_ref.shape, F32)
        for i in range(nq):
            rows, keys = slice(i * tq, (i + 1) * tq), (i + 1) * tq
            q, dov = q_ref[rows, :], do_ref[rows, :]
            delta = jnp.sum(dov.astype(F32) * o_ref[rows, :].astype(F32), axis=-1, keepdims=True)
            s = _scores(q, k_ref, keys, tq)
            p = jnp.exp(s - jnp.tile(lse_ref[rows, :], (1, keys // HEAD_SLAB)))
            dp = lax.dot_general(dov, v_ref[0:keys, :], _DIMS["nt"], preferred_element_type=F32)
            ds = (p * (dp - delta) * ATTN_SCALE).astype(BF16)
            dq_ref[rows, :] = jnp.dot(ds, k_ref[0:keys, :], preferred_element_type=F32)
            dk_ref[0:keys, :] += lax.dot_general(ds, q, _DIMS["tn"], preferred_element_type=F32)
            dv_ref[0:keys, :] += lax.dot_general(p.astype(BF16), dov, _DIMS["tn"], preferred_element_type=F32)

    spec = pl.BlockSpec((seq, HEAD_SLAB), lambda b, h: (b, h))
    out = jax.ShapeDtypeStruct(qc.shape, F32)
    return pl.pallas_call(
        body, name="attn_bwd", grid=(tokens // seq, N_HEADS),
        out_shape=(out, out, out), in_specs=[spec] * 6, out_specs=(spec, spec, spec),
        compiler_params=_params("parallel", "parallel"),
    )(qc, kc, vp, o, lse, do)


def _adamw(w, g, m, v, name):
    rows, cols = w.shape
    whole = rows * cols * 4 <= ADAMW_WHOLE_BYTES
    tr = rows if whole else _tile(rows, (256, 128, 64, 32, 16, 8))
    c1 = 1.0 - ADAM_B1 ** ADAM_STEP
    c2 = 1.0 - ADAM_B2 ** ADAM_STEP

    def body(w_ref, g_ref, m_ref, v_ref, d_ref, nm_ref, nv_ref):
        gv = g_ref[...]
        nm = ADAM_B1 * m_ref[...] + (1.0 - ADAM_B1) * gv
        nv = ADAM_B2 * v_ref[...] + (1.0 - ADAM_B2) * (gv * gv)
        d_ref[...] = -ADAM_LR * ((nm / c1) / (jnp.sqrt(nv / c2) + ADAM_EPS) + ADAM_WD * w_ref[...])
        nm_ref[...] = nm
        nv_ref[...] = nv

    spec = pl.BlockSpec((tr, cols), lambda i: (i, 0))
    out = jax.ShapeDtypeStruct(w.shape, F32)
    return pl.pallas_call(
        body, name=name, grid=(rows // tr,), out_shape=(out, out, out),
        in_specs=[spec] * 4, out_specs=(spec, spec, spec),
        compiler_params=_params("parallel"),
    )(w, g, m, v)


def _mod_cols(c_all, w_ada, b_cols):
    def body(c_ref, w_ref, b_ref, act_ref, mod_ref):
        cv = c_ref[...]
        act = cv * _sigmoid(cv)
        act_ref[...] = act
        mod_ref[...] = jnp.dot(act.astype(BF16), w_ref[...].astype(BF16),
                               preferred_element_type=F32) + b_ref[...]

    n = w_ada.shape[1]
    return pl.pallas_call(
        body, name="mod_cols",
        out_shape=(jax.ShapeDtypeStruct(c_all.shape, F32), jax.ShapeDtypeStruct((c_all.shape[0], n), F32)),
        compiler_params=pltpu.CompilerParams(vmem_limit_bytes=VMEM_LIMIT),
    )(c_all, w_ada, b_cols)


def _ada_grads(c_act, dmod_all, dmod_cols):
    def body(c_ref, d_ref, dc_ref, gw_ref, gb_ref):
        gw_ref[...] = lax.dot_general(c_ref[...].astype(BF16), dc_ref[...].astype(BF16), _DIMS["tn"],
                                      preferred_element_type=F32)
        gb_ref[...] = _colsum(d_ref[...])

    return pl.pallas_call(
        body, name="ada_grads",
        out_shape=(jax.ShapeDtypeStruct((c_act.shape[1], dmod_cols.shape[1]), F32),
                   jax.ShapeDtypeStruct((1, dmod_all.shape[1]), F32)),
        compiler_params=pltpu.CompilerParams(vmem_limit_bytes=VMEM_LIMIT),
    )(c_act, dmod_all, dmod_cols)


def _flat_rows(a):
    flat = a.reshape(-1)
    pad = (-flat.shape[0]) % (LANES * SUBLANES)
    if pad:
        flat = jnp.pad(flat, (0, pad))
    return flat.reshape(-1, LANES)


def _gather_start(w, names, tag, after):
    shards = [(w[n] if n in ROW_SHARDED else w[n].T).astype(BF16) for n in names]
    return _exchange_start(shards, f"gather_{tag}_start", after=after)


def _gather_wait(handle, names, tag, after):
    landed = _exchange_wait(handle, f"gather_{tag}_wait", after=after)
    return {n: g.reshape(-1, g.shape[2]) for n, g in zip(names, landed)}


def _scatter_start(grads, names, tag, after=None):
    blocks = [grads[n].reshape(N_DEV, -1, grads[n].shape[1]) for n in names]
    return _exchange_start(blocks, f"scatter_{tag}_start", scatter=True, after=after)


def _scatter_wait(handle, names, tag, after):
    landed = _exchange_wait(handle, f"scatter_{tag}_wait", scatter=True, after=after)
    out = {}
    for n, x in zip(names, landed):
        g = _sum_blocks(x, f"sum_{n}")
        out[n] = g if n in ROW_SHARDED else g.T
    return out


def _pack_small(vals):
    return jnp.concatenate([_flat_rows(v.astype(F32)) for v in vals], axis=0)


def _unpack_small(packed, like):
    out, row = [], 0
    for v in like:
        rows = _flat_rows(v).shape[0]
        out.append(packed[row:row + rows].reshape(-1)[:v.size].reshape(v.shape))
        row += rows
    return out


def _lanes128(*parts):
    out = jnp.zeros((HEAD_SLAB,), F32)
    for off, v in parts:
        out = lax.dynamic_update_slice(out, v.reshape(-1).astype(F32), (off,))
    return out.reshape(1, HEAD_SLAB)


def _step(x, c, positions, w, m, v, loss_target):
    nseq, seq, _ = x.shape
    tokens = nseq * seq
    me = _index(_my_pos())
    strip = lambda d: {n: (a[0] if a.ndim > 2 else a) for n, a in d.items()}
    shapes = {n: a.shape for n, a in w.items()}
    w, m, v = strip(w), strip(m), strip(v)

    c_all = _all_gather(c.reshape(-1, LANES), "gather_c").reshape(N_DEV * nseq, D_MODEL)
    n_ada = w["w_ada"].shape[1]
    b_cols = lax.dynamic_slice(w["b_ada"], (0, me * n_ada), (1, n_ada))
    c_act, mod_cols = _mod_cols(c_all, w["w_ada"], b_cols)
    mod_all = _all_gather(mod_cols, "gather_mod")
    mod = lax.dynamic_slice(mod_all, (0, me * nseq, 0), (N_DEV, nseq, n_ada))
    mod = mod.transpose(1, 0, 2).reshape(nseq, 3, 3, 1, D_MODEL)

    h_f1i, tok = _gather_start(w, ("w_ffn1_in",), "ffn1_in", after=mod_all)
    h_f1o, tok = _gather_start(w, ("w_ffn1_out",), "ffn1_out", after=tok)
    h_mix, tok = _gather_start(w, MIXER, "mix", after=tok)
    h_f2, tok = _gather_start(w, ("w_ffn2_in", "w_ffn2_out"), "ffn2", after=tok)
    started = tok[0:1, 0:1]

    g_q = _lanes128((0, w["q_norm_nope"]), (QK_NOPE, w["q_norm_rope"]))
    g_kn = _lanes128((0, w["k_norm_nope"]))
    g_kr = _lanes128((QK_NOPE, w["k_norm_rope"]))
    freq = ROPE_THETA ** (-jnp.arange(0, QK_ROPE, 2, dtype=F32) / QK_ROPE)
    inv_freq = _lanes128((QK_NOPE, jnp.concatenate([freq, freq])))
    pos = positions.reshape(tokens, 1).astype(F32)

    def sub(k, gamma, coef):
        return dict(gamma=w[gamma], shift=mod[:, k, 0] + started, scale=mod[:, k, 1], gate=coef * mod[:, k, 2])
    p1, pm, p2 = sub(0, "norm_ffn1", 0.5), sub(1, "norm_mix", 1.0), sub(2, "norm_ffn2", 0.5)
    t_big = _tile(tokens, (2048, 1024, 512))
    t_mid = _tile(tokens, (1024, 512))

    x0 = x.reshape(tokens, D_MODEL)
    h1 = _norm_mod_fwd(x0, p1, seq, "ffn1_norm")
    wt_f1i = _gather_wait(h_f1i, ("w_ffn1_in",), "ffn1_in", h1)["w_ffn1_in"]
    g1, u1, a1 = _ffn_in_act(h1, wt_f1i, "ffn1_in")
    w_f1o = _gather_wait(h_f1o, ("w_ffn1_out",), "ffn1_out", a1)["w_ffn1_out"]
    x1, f1, h2 = _out_residual(a1, w_f1o, x0, p1["gate"], pm, seq, "ffn1_out")
    saved1 = (x0, h1, g1, u1, a1, wt_f1i, w_f1o)

    full = _gather_wait(h_mix, MIXER, "mix", h2)
    wt_in = full["w_in"]
    zero_rows = lambda rows: jnp.zeros((rows, D_MODEL), BF16)
    wt_p = wt_in[:512]
    wt_a = jnp.concatenate([wt_in[512:1152], zero_rows(QK_NOPE), wt_in[1152:1184], zero_rows(32)], axis=0)
    wt_g = wt_in[1184:]
    wtq_pad = jnp.pad(full["w_q_up"].reshape(N_HEADS, 96, Q_LORA), ((0, 0), (0, 32), (0, 0))).reshape(-1, Q_LORA)
    wtmla_pad = jnp.pad(full["w_mla_proj"].reshape(D_MODEL, N_HEADS, 64), ((0, 0), (0, 0), (64, 0))).reshape(D_MODEL, -1)
    wt_pool, wt_kv, w_mix_out = full["w_pool_proj"], full["w_kv_up"], full["w_out"]

    z_a = _mm(h2, wt_a, "nt", "mix_in_a", tm=t_big, tn=wt_a.shape[0])
    z_p = _mm(h2, wt_p, "nt", "mix_in_p", tm=t_big, tn=512)
    z_g = _mm(h2, wt_g, "nt", "mix_in_g", tm=t_big, tn=512)
    pooled, pg, ps = _pool_fwd(z_p, w["pool_grp"], w["pool_scale"], seq)
    br_pool = _mm(ps, wt_pool, "nt", "pool_proj", tm=t_big, tn=D_MODEL)
    qn, kvn = _latent_norm_fwd(z_a, w["q_a_norm"], w["kv_a_norm"], seq)
    qp = _mm(qn, wtq_pad, "nt", "q_up", tm=t_big, tn=D_MODEL)
    kv = _mm(kvn, wt_kv, "nt", "kv_up", tm=t_big, tn=D_MODEL)
    qc, kc, vp = _qk_prep_fwd(qp, kv, z_a, pos, g_q, g_kn, g_kr, inv_freq, seq)
    attn, lse = _attn_fwd(qc, kc, vp, seq)
    br_mla = _mm(attn, wtmla_pad, "nt", "mla_proj", tm=t_mid, tn=D_MODEL)

    def merge(rows, bats, vecs):
        zg, bp, bm = rows
        return [_sigmoid(zg[:, :D_MODEL]) * bp + _sigmoid(zg[:, D_MODEL:]) * bm], [], []
    merged = _rowmap("merge", merge, seq, [z_g, br_pool, br_mla], row_outs=[(D_MODEL, BF16)])[0]
    x2, o_mix, h3 = _out_residual(merged, w_mix_out, x1, pm["gate"], p2, seq, "mix_out")

    ffn2_w = _gather_wait(h_f2, ("w_ffn2_in", "w_ffn2_out"), "ffn2", h3)
    g2, u2, a2 = _ffn_in_act(h3, ffn2_w["w_ffn2_in"], "ffn2_in")
    dy, df2, dgate2, sq_err = _out_loss(a2, ffn2_w["w_ffn2_out"], x2, p2["gate"],
                                        loss_target.reshape(tokens, D_MODEL), seq, "ffn2_out")
    saved2 = (x2, h3, g2, u2, a2, ffn2_w["w_ffn2_in"], ffn2_w["w_ffn2_out"])
    loss = lax.psum(0.5 * jnp.sum(sq_err) * (1.0 / D_MODEL), AXES)

    grads = {}
    (dx2, do_mix, dsh2, dsc2, dgate_m, dg_ffn2), ops2 = _ffn_bwd_x(df2, dy, saved2, p2, seq, "ffn2", (o_mix, pm["gate"]))
    grads["w_ffn2_out"], grads["w_ffn2_in"] = _ffn_bwd_wout(ops2, "ffn2"), _ffn_bwd_win(ops2, "ffn2")
    s_f2, tok = _scatter_start(grads, ("w_ffn2_in", "w_ffn2_out"), "ffn2")

    dmerged = _mm(do_mix, w_mix_out, "nt", "mix_bwd_dmerged", tm=t_mid, tn=D_MODEL)
    grads["w_out"] = _mm(merged, do_mix, "tn", "mix_bwd_wout", out_dtype=BF16, tm=512, tn=D_MODEL)

    def merge_bwd(rows, bats, vecs):
        dmv, zg, bp, bm = rows
        s_p, s_m = _sigmoid(zg[:, :D_MODEL]), _sigmoid(zg[:, D_MODEL:])
        dzg = jnp.concatenate([dmv * bp * s_p * (1.0 - s_p), dmv * bm * s_m * (1.0 - s_m)], axis=1)
        return [dmv * s_p, dmv * s_m, dzg], [], []
    dbr_pool, dbr_mla, dz_g = _rowmap("merge_bwd", merge_bwd, seq, [dmerged, z_g, br_pool, br_mla],
                                      row_outs=[(D_MODEL, BF16), (D_MODEL, BF16), (2 * D_MODEL, BF16)])

    grads["w_pool_proj"] = _mm(dbr_pool, ps, "tn", "pool_bwd_wproj", out_dtype=BF16, tm=512, tn=POOL_WIDTH)
    dps = _mm(dbr_pool, wt_pool, "nn", "pool_bwd_dps", tm=t_big, tn=POOL_WIDTH)
    dz_p, dgrp, dpool_scale = _pool_bwd(dps, pooled, pg, w["pool_grp"], w["pool_scale"] + tok[0:1, 0:1], seq)

    dwtmla_pad = _mm(dbr_mla, attn, "tn", "mla_bwd_wproj", out_dtype=BF16, tm=512, tn=D_MODEL)
    grads["w_mla_proj"] = dwtmla_pad.reshape(D_MODEL, N_HEADS, HEAD_SLAB)[:, :, 64:].reshape(D_MODEL, -1)
    d_attn = _mm(dbr_mla, wtmla_pad, "nn", "mla_bwd_dattn", out_dtype=BF16, tm=t_mid, tn=D_MODEL)
    dqc, dkc, dvp = _attn_bwd(qc, kc, vp, attn, lse, d_attn, seq)
    dqp, dkv, dkr, dg_q, dg_kn, dg_kr = _qk_prep_bwd(dqc, dkc, dvp, qp, kv, z_a, pos, g_q, g_kn, g_kr, inv_freq, seq)
    dwtq_pad = _mm(dqp, qn, "tn", "q_up_bwd_w", out_dtype=BF16, tm=512, tn=Q_LORA)
    grads["w_q_up"] = dwtq_pad.reshape(N_HEADS, HEAD_SLAB, Q_LORA)[:, :96].reshape(-1, Q_LORA)
    grads["w_kv_up"] = _mm(dkv, kvn, "tn", "kv_up_bwd_w", out_dtype=BF16, tm=512, tn=KV_LORA)
    dqn = _mm(dqp, wtq_pad, "nn", "q_up_bwd_x", tm=t_big, tn=Q_LORA)
    dkvn = _mm(dkv, wt_kv, "nn", "kv_up_bwd_x", tm=t_big, tn=KV_LORA)
    dz_a, dg_qa, dg_kva = _latent_norm_bwd(dqn, dkvn, dkr, z_a, w["q_a_norm"], w["kv_a_norm"], seq)

    dwt_a = _mm(dz_a, h2, "tn", "mix_in_bwd_wa", out_dtype=BF16, tm=256, tn=D_MODEL)
    dwt_p = _mm(dz_p, h2, "tn", "mix_in_bwd_wp", out_dtype=BF16, tm=512, tn=D_MODEL)
    dwt_g = _mm(dz_g, h2, "tn", "mix_in_bwd_wg", out_dtype=BF16, tm=512, tn=D_MODEL)
    grads["w_in"] = jnp.concatenate([dwt_p, dwt_a[:640], dwt_a[704:736], dwt_g], axis=0)
    s_mix, tok = _scatter_start(grads, MIXER, "mix")
    dh2 = _mm(dz_a, wt_a, "nn", "mix_in_bwd_xa", tm=t_mid, tn=D_MODEL)
    dh2 = _mm(dz_p, wt_p, "nn", "mix_in_bwd_xp", tm=t_mid, tn=D_MODEL, add=dh2)
    dh2 = _mm(dz_g, wt_g, "nn", "mix_in_bwd_xg", tm=t_mid, tn=D_MODEL, add=dh2)
    pm_tied = dict(pm, scale=pm["scale"] + tok[0:1, 0:1])
    dx1, df1, dsh_m, dsc_m, dgate1, dg_mix = _norm_mod_bwd(dh2, x1, dx2, pm_tied, seq, "mix_bwd_norm", (f1, p1["gate"]))

    small_early = [dg_mix.reshape(w["norm_mix"].shape), dg_ffn2.reshape(w["norm_ffn2"].shape), dgrp, dpool_scale,
                   dg_qa, dg_kva, dg_q[:, :QK_NOPE], dg_q[:, QK_NOPE:QK_NOPE + QK_ROPE], dg_kn[:, :QK_NOPE],
                   dg_kr[:, QK_NOPE:QK_NOPE + QK_ROPE]]
    s_small, tok = _exchange_start([_pack_small(small_early)], "gather_small_start", after=tok)

    p1_tied = dict(p1, scale=p1["scale"] + tok[0:1, 0:1])
    (dx0, dsh1, dsc1, dg_ffn1), ops1 = _ffn_bwd_x(df1, dx1, saved1, p1_tied, seq, "ffn1")
    grads["w_ffn1_out"] = _ffn_bwd_wout(ops1, "ffn1")

    dmod = jnp.stack([jnp.stack([dsh1, dsc1, 0.5 * dgate1], axis=1),
                      jnp.stack([dsh_m, dsc_m, dgate_m], axis=1),
                      jnp.stack([dsh2, dsc2, 0.5 * dgate2], axis=1)], axis=1)
    n_dmod = nseq * 9 * D_MODEL // LANES
    tail = _all_gather(jnp.concatenate([dmod.reshape(-1, LANES), _flat_rows(dg_ffn1)], axis=0), "gather_dmod")
    dmod_all = tail[:, :n_dmod].reshape(N_DEV * nseq, 9 * D_MODEL)

    s_f1o, tok = _scatter_start(grads, ("w_ffn1_out",), "ffn1_out", after=tail)
    grads["w_ffn1_in"] = _ffn_bwd_win(ops1, "ffn1")
    s_f1i, tok = _scatter_start(grads, ("w_ffn1_in",), "ffn1_in", after=tok)

    dmod_cols = lax.dynamic_slice(dmod_all, (0, me * n_ada), (N_DEV * nseq, n_ada)) + tok[0:1, 0:1]
    g_w_ada, g_b_ada = _ada_grads(c_act, dmod_all, dmod_cols)
    g_norm_ffn1 = _sum_blocks(tail[:, n_dmod:], "sum_norm_ffn1").reshape(1, D_MODEL)
    small_all = _exchange_wait(s_small, "gather_small_wait", after=g_b_ada)[0]
    small_sum = _sum_blocks(small_all, "sum_small")
    small = dict(zip(SMALL[1:], _unpack_small(small_sum, [w[n] for n in SMALL[1:]])))
    grad_w = dict(small, w_ada=g_w_ada, b_ada=g_b_ada, norm_ffn1=g_norm_ffn1)

    delta, new_m, new_v = {}, {}, {}

    def update(names):
        for n in names:
            delta[n], new_m[n], new_v[n] = _adamw(w[n], grad_w[n], m[n], v[n], f"adamw_{n}")

    update(("w_ada",))
    rep = ("b_ada",) + SMALL
    d_s, m_s, v_s = _adamw(_pack_small([w[n] for n in rep]), _pack_small([grad_w[n] for n in rep]),
                           _pack_small([m[n] for n in rep]), _pack_small([v[n] for n in rep]), "adamw_small")
    like = [w[n] for n in rep]
    for dst, packed in ((delta, d_s), (new_m, m_s), (new_v, v_s)):
        dst.update(zip(rep, _unpack_small(packed, like)))
    grad_w.update(_scatter_wait(s_f2, ("w_ffn2_in", "w_ffn2_out"), "ffn2", after=d_s))
    update(("w_ffn2_in", "w_ffn2_out"))
    grad_w.update(_scatter_wait(s_mix, MIXER, "mix", after=delta["w_ffn2_out"]))
    update(MIXER)
    grad_w.update(_scatter_wait(s_f1o, ("w_ffn1_out",), "ffn1_out", after=delta["w_out"]))
    update(("w_ffn1_out",))
    grad_w.update(_scatter_wait(s_f1i, ("w_ffn1_in",), "ffn1_in", after=delta["w_ffn1_out"]))
    update(("w_ffn1_in",))

    lead = lambda d: [d[n].reshape(shapes[n]) for n in WEIGHTS]
    return (loss, dx0.reshape(x.shape), *lead(grad_w), *lead(delta), *lead(new_m), *lead(new_v))


def kernel(x, c, positions, w_ada, b_ada, norm_ffn1, w_ffn1_in, w_ffn1_out, norm_mix, w_in, pool_grp, pool_scale, w_pool_proj, q_a_norm, w_q_up, kv_a_norm, w_kv_up, q_norm_nope, q_norm_rope, k_norm_nope, k_norm_rope, w_mla_proj, w_out, norm_ffn2, w_ffn2_in, w_ffn2_out, loss_target, m_w_ada, m_b_ada, m_norm_ffn1, m_w_ffn1_in, m_w_ffn1_out, m_norm_mix, m_w_in, m_pool_grp, m_pool_scale, m_w_pool_proj, m_q_a_norm, m_w_q_up, m_kv_a_norm, m_w_kv_up, m_q_norm_nope, m_q_norm_rope, m_k_norm_nope, m_k_norm_rope, m_w_mla_proj, m_w_out, m_norm_ffn2, m_w_ffn2_in, m_w_ffn2_out, v_w_ada, v_b_ada, v_norm_ffn1, v_w_ffn1_in, v_w_ffn1_out, v_norm_mix, v_w_in, v_pool_grp, v_pool_scale, v_w_pool_proj, v_q_a_norm, v_w_q_up, v_kv_a_norm, v_w_kv_up, v_q_norm_nope, v_q_norm_rope, v_k_norm_nope, v_k_norm_rope, v_w_mla_proj, v_w_out, v_norm_ffn2, v_w_ffn2_in, v_w_ffn2_out):
    w = dict(w_ada=w_ada, b_ada=b_ada, norm_ffn1=norm_ffn1, w_ffn1_in=w_ffn1_in, w_ffn1_out=w_ffn1_out, norm_mix=norm_mix, w_in=w_in, pool_grp=pool_grp, pool_scale=pool_scale, w_pool_proj=w_pool_proj, q_a_norm=q_a_norm, w_q_up=w_q_up, kv_a_norm=kv_a_norm, w_kv_up=w_kv_up, q_norm_nope=q_norm_nope, q_norm_rope=q_norm_rope, k_norm_nope=k_norm_nope, k_norm_rope=k_norm_rope, w_mla_proj=w_mla_proj, w_out=w_out, norm_ffn2=norm_ffn2, w_ffn2_in=w_ffn2_in, w_ffn2_out=w_ffn2_out)
    m = dict(w_ada=m_w_ada, b_ada=m_b_ada, norm_ffn1=m_norm_ffn1, w_ffn1_in=m_w_ffn1_in, w_ffn1_out=m_w_ffn1_out, norm_mix=m_norm_mix, w_in=m_w_in, pool_grp=m_pool_grp, pool_scale=m_pool_scale, w_pool_proj=m_w_pool_proj, q_a_norm=m_q_a_norm, w_q_up=m_w_q_up, kv_a_norm=m_kv_a_norm, w_kv_up=m_w_kv_up, q_norm_nope=m_q_norm_nope, q_norm_rope=m_q_norm_rope, k_norm_nope=m_k_norm_nope, k_norm_rope=m_k_norm_rope, w_mla_proj=m_w_mla_proj, w_out=m_w_out, norm_ffn2=m_norm_ffn2, w_ffn2_in=m_w_ffn2_in, w_ffn2_out=m_w_ffn2_out)
    v = dict(w_ada=v_w_ada, b_ada=v_b_ada, norm_ffn1=v_norm_ffn1, w_ffn1_in=v_w_ffn1_in, w_ffn1_out=v_w_ffn1_out, norm_mix=v_norm_mix, w_in=v_w_in, pool_grp=v_pool_grp, pool_scale=v_pool_scale, w_pool_proj=v_w_pool_proj, q_a_norm=v_q_a_norm, w_q_up=v_w_q_up, kv_a_norm=v_kv_a_norm, w_kv_up=v_w_kv_up, q_norm_nope=v_q_norm_nope, q_norm_rope=v_q_norm_rope, k_norm_nope=v_k_norm_nope, k_norm_rope=v_k_norm_rope, w_mla_proj=v_w_mla_proj, w_out=v_w_out, norm_ffn2=v_norm_ffn2, w_ffn2_in=v_w_ffn2_in, w_ffn2_out=v_w_ffn2_out)
    return _step(x, c, positions, w, m, v, loss_target)
```

```python
import functools
import math

import jax
import jax.numpy as jnp
from jax import lax
from jax.experimental import pallas as pl
from jax.experimental.pallas import tpu as pltpu

F32 = jnp.float32
BF16 = jnp.bfloat16
MESH = pl.DeviceIdType.MESH
AXES = ("x", "y", "c")
N_DEV = 8

D_MODEL = 1024
D_FF = 2816
N_HEADS = 8
HEAD_SLAB = 128
QK_NOPE = 64
QK_ROPE = 32
POOL_WIDTH = 512
POOL_GROUPS = 4
POOL_GROUP_DIM = 128
Q_LORA = 384
KV_LORA = 256
ROPE_THETA = 10000.0
ATTN_SCALE = 1.0 / math.sqrt(QK_NOPE + QK_ROPE)
NORM_EPS = 1e-6
ADAM_LR, ADAM_B1, ADAM_B2, ADAM_EPS, ADAM_WD, ADAM_STEP = 0.001, 0.9, 0.999, 1e-08, 0.01, 10

LANES = 128
SUBLANES = 8
VMEM_LIMIT = 52 * 1024 * 1024
ADAMW_WHOLE_BYTES = 3 << 19
SUM_WHOLE_BYTES = 4 << 20

BIG = ("w_ffn1_in", "w_ffn1_out", "w_in", "w_pool_proj", "w_q_up", "w_kv_up",
       "w_mla_proj", "w_out", "w_ffn2_in", "w_ffn2_out")
ROW_SHARDED = ("w_ffn1_out", "w_out", "w_ffn2_out")
MIXER = ("w_in", "w_pool_proj", "w_q_up", "w_kv_up", "w_mla_proj", "w_out")
SMALL = ("norm_ffn1", "norm_mix", "norm_ffn2", "pool_grp", "pool_scale", "q_a_norm",
         "kv_a_norm", "q_norm_nope", "q_norm_rope", "k_norm_nope", "k_norm_rope")
WEIGHTS = ("w_ada", "b_ada", "norm_ffn1", "w_ffn1_in", "w_ffn1_out", "norm_mix", "w_in",
           "pool_grp", "pool_scale", "w_pool_proj", "q_a_norm", "w_q_up", "kv_a_norm",
           "w_kv_up", "q_norm_nope", "q_norm_rope", "k_norm_nope", "k_norm_rope",
           "w_mla_proj", "w_out", "norm_ffn2", "w_ffn2_in", "w_ffn2_out")


def _params(*sem):
    return pltpu.CompilerParams(dimension_semantics=sem, vmem_limit_bytes=VMEM_LIMIT)


def _tile(n, cands):
    for c in cands:
        if n % c == 0:
            return c
    return n


def _my_pos():
    return lax.axis_index("x"), lax.axis_index("y"), lax.axis_index("c")


def _flip(pos, k):
    x, y, c = pos
    fx, fy, fc = (k >> 2) & 1, (k >> 1) & 1, k & 1
    return ((1 - x) if fx else x, (1 - y) if fy else y, (1 - c) if fc else c)


def _index(pos):
    x, y, c = pos
    return 4 * x + 2 * y + c


def _exchange(arrays, name, scatter=False):
    n = len(arrays)

    def body(*refs):
        ins, outs = refs[:n], refs[n:2 * n]
        send_sems, recv_sems, local_sems = refs[2 * n:]
        me = _my_pos()
        mine, sends = [], []
        for a in range(n):
            own = ins[a].at[_index(me)] if scatter else ins[a]
            cp = pltpu.make_async_copy(own, outs[a].at[_index(me)], local_sems.at[a])
            cp.start()
            mine.append(cp)
        for k in range(1, N_DEV):
            peer = _flip(me, k)
            for a in range(n):
                cp = pltpu.make_async_remote_copy(
                    src_ref=ins[a].at[_index(peer)] if scatter else ins[a],
                    dst_ref=outs[a].at[_index(me)],
                    send_sem=send_sems.at[a, k - 1], recv_sem=recv_sems.at[a, k - 1],
                    device_id=peer, device_id_type=MESH)
                cp.start()
                sends.append(cp)
        for k in range(1, N_DEV):
            peer = _flip(me, k)
            for a in range(n):
                pltpu.make_async_remote_copy(
                    src_ref=ins[a].at[_index(me)] if scatter else ins[a],
                    dst_ref=outs[a].at[_index(peer)],
                    send_sem=send_sems.at[a, k - 1], recv_sem=recv_sems.at[a, k - 1],
                    device_id=peer, device_id_type=MESH).wait_recv()
        for cp in sends:
            cp.wait_send()
        for cp in mine:
            cp.wait()

    shape = lambda x: x.shape if scatter else (N_DEV,) + x.shape
    return pl.pallas_call(
        body, name=name,
        out_shape=tuple(jax.ShapeDtypeStruct(shape(x), x.dtype) for x in arrays),
        in_specs=[pl.BlockSpec(memory_space=pl.ANY)] * n,
        out_specs=tuple(pl.BlockSpec(memory_space=pl.ANY) for _ in arrays),
        scratch_shapes=[pltpu.SemaphoreType.DMA((n, N_DEV - 1)),
                        pltpu.SemaphoreType.DMA((n, N_DEV - 1)),
                        pltpu.SemaphoreType.DMA((n,))],
    )(*arrays)


def _all_gather(x, name):
    return _exchange([x], name)[0]


_HBM = pl.BlockSpec(memory_space=pltpu.HBM)
_SEM = pl.BlockSpec(memory_space=pltpu.SEMAPHORE)
_ANY = pl.BlockSpec(memory_space=pl.ANY)
_EFFECT = pltpu.SideEffectType.DATAFLOW_SIDE_EFFECTING


def _split_copy(ins, lands, send_sems, recv_sems, a, k, me, scatter, incoming):
    peer = _flip(me, k)
    block = me if incoming else peer
    return pltpu.make_async_remote_copy(
        src_ref=ins[a].at[_index(block)] if scatter else ins[a],
        dst_ref=lands[a].at[_index(peer if incoming else me)],
        send_sem=send_sems.at[a * (N_DEV - 1) + k - 1], recv_sem=recv_sems.at[a * (N_DEV - 1) + k - 1],
        device_id=peer, device_id_type=MESH)


def _exchange_start(arrays, name, scatter=False, after=None):
    n = len(arrays)
    after = jnp.zeros((SUBLANES, LANES), F32) if after is None else after

    def body(*refs):
        ins, lands = refs[:n], refs[n:2 * n]
        send_sems, recv_sems = refs[2 * n + 1], refs[2 * n + 2]
        me = _my_pos()
        for k in range(1, N_DEV):
            for a in range(n):
                _split_copy(ins, lands, send_sems, recv_sems, a, k, me, scatter, False).start()
        refs[-1][...] = jnp.zeros((SUBLANES, LANES), F32)

    shape = lambda x: x.shape if scatter else (N_DEV,) + x.shape
    hbm = lambda x: pltpu.with_memory_space_constraint(x, pltpu.HBM)
    srcs = [hbm(x) for x in arrays]
    zones = [hbm(lax.empty(shape(x), x.dtype)) for x in arrays]
    out = pl.pallas_call(
        body, name=name,
        out_shape=(pltpu.SemaphoreType.DMA((n * (N_DEV - 1),)), pltpu.SemaphoreType.DMA((n * (N_DEV - 1),)),
                   *[pltpu.HBM(x.shape, x.dtype) for x in srcs + zones],
                   jax.ShapeDtypeStruct((SUBLANES, LANES), F32)),
        in_specs=[_HBM] * (2 * n) + [_ANY],
        out_specs=(_SEM, _SEM, *[_HBM] * (2 * n), pl.BlockSpec(memory_space=pltpu.VMEM)),
        input_output_aliases={i: 2 + i for i in range(2 * n)},
        compiler_params=pltpu.CompilerParams(has_side_effects=_EFFECT),
    )(*srcs, *zones, after)
    return out[:-1], out[-1]


def _exchange_wait(handle, name, scatter=False, after=None):
    send_sems, recv_sems = handle[0], handle[1]
    n = (len(handle) - 2) // 2
    after = jnp.zeros((SUBLANES, LANES), F32) if after is None else after

    def body(*refs):
        ins, lands = refs[:n], refs[n:2 * n]
        send, recv = refs[2 * n], refs[2 * n + 1]
        me = _my_pos()
        for k in range(1, N_DEV):
            for a in range(n):
                _split_copy(ins, lands, send, recv, a, k, me, scatter, False).wait_send()
                _split_copy(ins, lands, send, recv, a, k, me, scatter, True).wait_recv()

    bufs = handle[2:]
    out = pl.pallas_call(
        body, name=name,
        out_shape=tuple(pltpu.HBM(x.shape, x.dtype) for x in bufs),
        in_specs=[_HBM] * (2 * n) + [_SEM, _SEM, _ANY],
        out_specs=tuple([_HBM] * (2 * n)),
        input_output_aliases={i: i for i in range(2 * n)},
        compiler_params=pltpu.CompilerParams(has_side_effects=_EFFECT),
    )(*bufs, send_sems, recv_sems, after)
    me = _index(_my_pos())
    landed = []
    for src, land in zip(out[:n], out[n:]):
        own = lax.dynamic_slice_in_dim(src, me, 1, axis=0) if scatter else src[None]
        landed.append(lax.dynamic_update_slice_in_dim(land, own, me, axis=0))
    return landed


def _sum_blocks(x, name):
    n, rows, cols = x.shape
    whole = x.size * x.dtype.itemsize <= SUM_WHOLE_BYTES
    tr = rows if whole else _tile(rows, (512, 256, 128, 64, 32, 16, 8))

    def body(x_ref, o_ref):
        acc = x_ref[0].astype(F32)
        for d in range(1, n):
            acc = acc + x_ref[d].astype(F32)
        o_ref[...] = acc

    return pl.pallas_call(
        body, name=name,
        out_shape=jax.ShapeDtypeStruct((rows, cols), F32),
        grid=(rows // tr,),
        in_specs=[pl.BlockSpec((n, tr, cols), lambda i: (0, i, 0))],
        out_specs=pl.BlockSpec((tr, cols), lambda i: (i, 0)),
        compiler_params=_params("parallel"),
    )(x)


_DIMS = {"nn": (((1,), (0,)), ((), ())), "nt": (((1,), (1,)), ((), ())), "tn": (((0,), (0,)), ((), ()))}


def _mm(a, b, mode, name, out_dtype=F32, tm=None, tn=None, add=None, after=None):
    if mode == "tn":
        kdim, m = a.shape
    else:
        m, kdim = a.shape
    n = b.shape[0] if mode == "nt" else b.shape[1]
    tm = tm or _tile(m, (512, 256, 128))
    tn = tn or _tile(n, (512, 256, 128))
    dims = _DIMS[mode]

    def body(*refs):
        refs = refs if after is None else refs[1:]
        acc = lax.dot_general(refs[0][...].astype(BF16), refs[1][...].astype(BF16), dims,
                              preferred_element_type=F32)
        if add is not None:
            acc = acc + refs[2][...]
        refs[-1][...] = acc.astype(out_dtype)

    a_spec = (pl.BlockSpec((kdim, tm), lambda i, j: (0, i)) if mode == "tn"
              else pl.BlockSpec((tm, kdim), lambda i, j: (i, 0)))
    b_spec = (pl.BlockSpec((tn, kdim), lambda i, j: (j, 0)) if mode == "nt"
              else pl.BlockSpec((kdim, tn), lambda i, j: (0, j)))
    o_spec = pl.BlockSpec((tm, tn), lambda i, j: (i, j))
    in_specs, args = [a_spec, b_spec], [a, b]
    if add is not None:
        in_specs.append(o_spec)
        args.append(add)
    if after is not None:
        in_specs.insert(0, _ANY)
        args.insert(0, after)
    return pl.pallas_call(
        body, name=name, out_shape=jax.ShapeDtypeStruct((m, n), out_dtype), grid=(m // tm, n // tn),
        in_specs=in_specs, out_specs=o_spec,
        compiler_params=_params("parallel", "parallel"),
    )(*args)


def _rowmap(name, fn, seq, rows, bats=(), vecs=(), row_outs=(), bat_outs=(), vec_outs=(), ts=None, mm=None):
    rows = [r if isinstance(r, tuple) else (r, r.shape[1], 0) for r in rows]
    tokens = rows[0][0].shape[0]
    nseq = tokens // seq
    ts = ts or _tile(seq, (256, 128, 64, 32, 16, 8))
    nt = seq // ts
    n_r, n_b, n_v = len(rows), len(bats), len(vecs)
    n_ro, n_bo = len(row_outs), len(bat_outs)

    def accumulate(ref, val, first):
        @pl.when(first)
        def _():
            ref[...] = val.reshape(ref.shape)

        @pl.when(jnp.logical_not(first))
        def _():
            ref[...] += val.reshape(ref.shape)

    def body(*refs):
        n_in = n_r + n_b + n_v + (mm is not None)
        ins, outs = refs[:n_in], refs[n_in:]
        r_vals = [r[...] for r in ins[:n_r]]
        b_vals = [r[0] for r in ins[n_r:n_r + n_b]]
        v_vals = [r[...] for r in ins[n_r + n_b:n_r + n_b + n_v]]
        if mm is not None:
            r_vals[0] = lax.dot_general(r_vals[0].astype(BF16), ins[-1][...].astype(BF16), _DIMS[mm[1]],
                                        preferred_element_type=F32)
        ro, bo, vo = fn(r_vals, b_vals, v_vals)
        for ref, val in zip(outs[:n_ro], ro):
            ref[...] = val.astype(ref.dtype)
        b, i = pl.program_id(0), pl.program_id(1)
        for ref, val in zip(outs[n_ro:n_ro + n_bo], bo):
            accumulate(ref, val, i == 0)
        for ref, val in zip(outs[n_ro + n_bo:], vo):
            accumulate(ref, val, jnp.logical_and(i == 0, b == 0))

    in_specs = [pl.BlockSpec((ts, w), functools.partial(lambda b, i, cb: (b * nt + i, cb), cb=cb))
                for _, w, cb in rows]
    in_specs += [pl.BlockSpec((1, 1, v.shape[2]), lambda b, i: (b, 0, 0)) for v in bats]
    in_specs += [pl.BlockSpec((1, v.shape[1]), lambda b, i: (0, 0)) for v in vecs]
    extra = []
    if mm is not None:
        in_specs.append(pl.BlockSpec(mm[0].shape, lambda b, i: (0, 0)))
        extra.append(mm[0])
    out_shape = [jax.ShapeDtypeStruct((tokens, f), dt) for f, dt in row_outs]
    out_specs = [pl.BlockSpec((ts, f), lambda b, i: (b * nt + i, 0)) for f, _ in row_outs]
    out_shape += [jax.ShapeDtypeStruct((nseq, 1, f), F32) for f in bat_outs]
    out_specs += [pl.BlockSpec((1, 1, f), lambda b, i: (b, 0, 0)) for f in bat_outs]
    out_shape += [jax.ShapeDtypeStruct((1, f), F32) for f in vec_outs]
    out_specs += [pl.BlockSpec((1, f), lambda b, i: (0, 0)) for f in vec_outs]
    return pl.pallas_call(
        body, name=name, out_shape=tuple(out_shape), grid=(nseq, nt),
        in_specs=in_specs, out_specs=tuple(out_specs),
        compiler_params=_params("arbitrary", "arbitrary"),
    )(*([r[0] for r in rows] + list(bats) + list(vecs) + extra))


def _colsum(v):
    return jnp.sum(v, axis=0, keepdims=True)


def _rstd(x, width=None):
    width = width or x.shape[-1]
    return lax.rsqrt(jnp.sum(x * x, axis=-1, keepdims=True) * (1.0 / width) + NORM_EPS)


def _norm_bwd(dy, x, r, g, width=None):
    width = width or x.shape[-1]
    xhat = x * r
    dxhat = dy * g
    dx = r * (dxhat - xhat * (jnp.sum(dxhat * xhat, axis=-1, keepdims=True) * (1.0 / width)))
    return dx, dy * xhat


def _sigmoid(x):
    return 1.0 / (1.0 + jnp.exp(-x))


def _norm_mod(xv, g, sh, sc):
    return xv * _rstd(xv) * g * (1.0 + sc) + sh


def _norm_mod_fwd(x, p, seq, name):
    def fn(rows, bats, vecs):
        return [_norm_mod(rows[0], vecs[0], bats[0], bats[1])], [], []
    return _rowmap(name, fn, seq, [x], [p["shift"], p["scale"]], [p["gamma"]], row_outs=[(D_MODEL, BF16)])[0]


def _norm_mod_bwd(dh, x, dres, p, seq, name, prev=None):
    def fn(rows, bats, vecs):
        dhv, xv, dr = rows[:3]
        sc, g = bats[0], vecs[0]
        r = _rstd(xv)
        dxn, dg = _norm_bwd(dhv * (1.0 + sc), xv, r, g)
        dx = dr + dxn
        ro, bo = [dx], [_colsum(dhv), _colsum(dhv * (xv * r * g))]
        if prev is not None:
            ro.append(bats[1] * dx)
            bo.append(_colsum(dx * rows[3].astype(F32)))
        return ro, bo, [_colsum(dg)]
    more = prev is not None
    return _rowmap(name, fn, seq, [dh, x, dres] + ([prev[0]] if more else []),
                   [p["scale"]] + ([prev[1]] if more else []), [p["gamma"]],
                   row_outs=[(D_MODEL, F32)] + ([(D_MODEL, BF16)] if more else []),
                   bat_outs=[D_MODEL] * (3 if more else 2), vec_outs=[D_MODEL])


def _ffn_in_act(h, wt_in, name):
    tokens = h.shape[0]
    tm, tn = _tile(tokens, (2048, 1024, 512)), 256
    nj = D_FF // tn

    def body(h_ref, wg_ref, wu_ref, g_ref, u_ref, a_ref):
        hv = h_ref[...]
        g = lax.dot_general(hv, wg_ref[...], _DIMS["nt"], preferred_element_type=F32)
        u = lax.dot_general(hv, wu_ref[...], _DIMS["nt"], preferred_element_type=F32)
        g_ref[...] = g.astype(BF16)
        u_ref[...] = u.astype(BF16)
        a_ref[...] = (g * _sigmoid(g) * u).astype(BF16)

    o_spec = pl.BlockSpec((tm, tn), lambda i, j: (i, j))
    out = jax.ShapeDtypeStruct((tokens, D_FF), BF16)
    return pl.pallas_call(
        body, name=name, grid=(tokens // tm, nj), out_shape=(out, out, out),
        in_specs=[pl.BlockSpec((tm, D_MODEL), lambda i, j: (i, 0)),
                  pl.BlockSpec((tn, D_MODEL), lambda i, j: (j, 0)),
                  pl.BlockSpec((tn, D_MODEL), lambda i, j: (j + nj, 0))],
        out_specs=(o_spec, o_spec, o_spec),
        compiler_params=_params("parallel", "parallel"),
    )(h, wt_in, wt_in)


def _out_residual(a, w_out, res, gate, nxt, seq, name):
    def fn(rows, bats, vecs):
        acc, rv = rows
        x_new = rv + bats[0] * acc
        return [x_new, acc, _norm_mod(x_new, vecs[0], bats[1], bats[2])], [], []
    return _rowmap(name, fn, seq, [a, res], [gate, nxt["shift"], nxt["scale"]], [nxt["gamma"]],
                   row_outs=[(D_MODEL, F32), (D_MODEL, BF16), (D_MODEL, BF16)],
                   ts=_tile(seq, (512, 256, 128)), mm=(w_out, "nn"))


def _out_loss(a, w_out, res, gate, target, seq, name):
    def fn(rows, bats, vecs):
        acc, rv, tv = rows
        err = rv + bats[0] * acc - tv
        dy = err * (1.0 / D_MODEL)
        return [dy, bats[0] * dy], [_colsum(dy * acc)], [_colsum(err * err)]
    return _rowmap(name, fn, seq, [a, res, target], [gate], row_outs=[(D_MODEL, F32), (D_MODEL, BF16)],
                   bat_outs=[D_MODEL], vec_outs=[D_MODEL], ts=_tile(seq, (512, 256, 128)), mm=(w_out, "nn"))


def _ffn_bwd_x(df, dres, saved, p, seq, tag, prev=None, mid=None):
    x, h, g, u, a, w_in, w_out = saved
    tokens = x.shape[0]

    def act_bwd(rows, bats, vecs):
        dav, gv, uv = rows[0], rows[1].astype(F32), rows[2].astype(F32)
        sg = _sigmoid(gv)
        silu = gv * sg
        dg = dav * uv * (sg * (1.0 + gv * (1.0 - sg)))
        return [jnp.concatenate([dg, dav * silu], axis=1)], [], []
    dgu = _rowmap(f"{tag}_bwd_da", act_bwd, seq, [df, g, u], row_outs=[(2 * D_FF, BF16)],
                  ts=_tile(seq, (256, 128)), mm=(w_out, "nt"))[0]
    operands = (a, df, dgu, h)
    after = None if mid is None else mid(operands)
    dh = _mm(dgu, w_in, "nn", f"{tag}_bwd_dh", tm=_tile(tokens, (512, 256)), tn=D_MODEL, after=after)
    return _norm_mod_bwd(dh, x, dres, p, seq, f"{tag}_bwd_norm", prev), operands


def _ffn_bwd_wout(operands, tag):
    a, df, _, _ = operands
    return _mm(a, df, "tn", f"{tag}_bwd_wout", out_dtype=BF16, tm=256, tn=D_MODEL)


def _ffn_bwd_win(operands, tag, after=None):
    _, _, dgu, h = operands
    return _mm(dgu, h, "tn", f"{tag}_bwd_win", out_dtype=BF16, tm=512, tn=D_MODEL, after=after)


def _shift_rows(v, k, forward):
    n = v.shape[0]
    row = lax.broadcasted_iota(jnp.int32, v.shape, 0)
    if forward:
        return jnp.where(row >= k, pltpu.roll(v, k, 0), 0.0)
    return jnp.where(row < n - k, pltpu.roll(v, n - k, 0), 0.0)


def _window_sums(v, forward):
    out, s, k = [], v, 1
    for _ in range(POOL_GROUPS):
        s = s + _shift_rows(s, k, forward)
        out.append(s)
        k *= 2
    return out


def _by_group(vals, g):
    out = vals[-1]
    for idx in range(len(vals) - 2, -1, -1):
        out = jnp.where(g == idx, vals[idx], out)
    return out


def _inv_count(shape, g):
    t1 = lax.broadcasted_iota(jnp.int32, shape, 0) + 1
    window = _by_group([jnp.int32(2 ** (i + 1)) for i in range(POOL_GROUPS)], g)
    return 1.0 / jnp.minimum(t1, window).astype(F32)


def _pool_fwd(u, grp, scale, seq):
    tokens = u.shape[0]

    def body(u_ref, grp_ref, sc_ref, pooled_ref, pg_ref, ps_ref):
        g = pl.program_id(1)
        uv = u_ref[...]
        sums = _by_group(_window_sums(uv, True), g)
        pooled = (sums * _inv_count(uv.shape, g) - uv).astype(BF16)
        pg = jnp.dot(pooled, grp_ref[0].astype(BF16), preferred_element_type=F32)
        pooled_ref[...] = pooled
        pg_ref[...] = pg
        ps_ref[...] = (pg * sc_ref[...]).astype(BF16)

    blk = pl.BlockSpec((seq, POOL_GROUP_DIM), lambda b, g: (b, g))
    return pl.pallas_call(
        body, name="pool_fwd", grid=(tokens // seq, POOL_GROUPS),
        out_shape=(jax.ShapeDtypeStruct(u.shape, BF16), jax.ShapeDtypeStruct(u.shape, F32),
                   jax.ShapeDtypeStruct(u.shape, BF16)),
        in_specs=[blk, pl.BlockSpec((1, POOL_GROUP_DIM, POOL_GROUP_DIM), lambda b, g: (g, 0, 0)),
                  pl.BlockSpec((1, POOL_GROUP_DIM), lambda b, g: (0, g))],
        out_specs=(blk, blk, blk),
        compiler_params=_params("parallel", "parallel"),
    )(u, grp, scale)


def _pool_bwd(dps, pooled, pg, grp, scale, seq):
    tokens = dps.shape[0]

    def body(dps_ref, pooled_ref, pg_ref, grp_ref, sc_ref, du_ref, dgrp_ref, dsc_ref):
        g, b = pl.program_id(0), pl.program_id(1)
        dpsv = dps_ref[...]
        dpg = (dpsv * sc_ref[...]).astype(BF16)
        dsc = _colsum(dpsv * pg_ref[...])
        dgrp = lax.dot_general(pooled_ref[...], dpg, _DIMS["tn"], preferred_element_type=F32)

        @pl.when(b == 0)
        def _():
            dsc_ref[...] = dsc
            dgrp_ref[0] = dgrp

        @pl.when(b > 0)
        def _():
            dsc_ref[...] += dsc
            dgrp_ref[0] += dgrp

        dpool = lax.dot_general(dpg, grp_ref[0].astype(BF16), _DIMS["nt"], preferred_element_type=F32)
        sums = _by_group(_window_sums(dpool * _inv_count(dpool.shape, g), False), g)
        du_ref[...] = (sums - dpool).astype(BF16)

    blk = pl.BlockSpec((seq, POOL_GROUP_DIM), lambda g, b: (b, g))
    grp_spec = pl.BlockSpec((1, POOL_GROUP_DIM, POOL_GROUP_DIM), lambda g, b: (g, 0, 0))
    vec_spec = pl.BlockSpec((1, POOL_GROUP_DIM), lambda g, b: (0, g))
    return pl.pallas_call(
        body, name="pool_bwd", grid=(POOL_GROUPS, tokens // seq),
        out_shape=(jax.ShapeDtypeStruct(dps.shape, BF16), jax.ShapeDtypeStruct(grp.shape, F32),
                   jax.ShapeDtypeStruct(scale.shape, F32)),
        in_specs=[blk, blk, blk, grp_spec, vec_spec],
        out_specs=(blk, grp_spec, vec_spec),
        compiler_params=_params("arbitrary", "arbitrary"),
    )(dps, pooled, pg, grp, scale)


def _lane(shape):
    return lax.broadcasted_iota(jnp.int32, shape, len(shape) - 1)


def _rot(y):
    lane = _lane(y.shape)
    r = jnp.where(lane < QK_NOPE + QK_ROPE // 2,
                  -pltpu.roll(y, HEAD_SLAB - QK_ROPE // 2, 1), pltpu.roll(y, QK_ROPE // 2, 1))
    return jnp.where(jnp.logical_and(lane >= QK_NOPE, lane < QK_NOPE + QK_ROPE), r, 0.0)


def _part_rstd(x):
    sq = x * x
    nope = _lane(x.shape) < QK_NOPE
    s_nope = jnp.sum(jnp.where(nope, sq, 0.0), axis=-1, keepdims=True)
    s_rope = jnp.sum(sq, axis=-1, keepdims=True) - s_nope
    return jnp.where(nope, lax.rsqrt(s_nope * (1.0 / QK_NOPE) + NORM_EPS),
                     lax.rsqrt(s_rope * (1.0 / QK_ROPE) + NORM_EPS))


def _part_norm_bwd(dy, x, r, g):
    nope = _lane(x.shape) < QK_NOPE
    xhat = x * r
    dxhat = dy * g
    prod = dxhat * xhat
    m_nope = jnp.sum(jnp.where(nope, prod, 0.0), axis=-1, keepdims=True)
    m_rope = jnp.sum(prod, axis=-1, keepdims=True) - m_nope
    mean = jnp.where(nope, m_nope * (1.0 / QK_NOPE), m_rope * (1.0 / QK_ROPE))
    return r * (dxhat - xhat * mean), dy * xhat


def _latent_norm_fwd(z_a, g_q, g_kv, seq):
    def fn(rows, bats, vecs):
        q, kv = rows[0][:, :Q_LORA], rows[0][:, Q_LORA:Q_LORA + KV_LORA]
        return [q * _rstd(q) * vecs[0], kv * _rstd(kv) * vecs[1]], [], []
    return _rowmap("latent_norm", fn, seq, [z_a], vecs=[g_q, g_kv],
                   row_outs=[(Q_LORA, BF16), (KV_LORA, BF16)])


def _latent_norm_bwd(dqn, dkvn, dkr, z_a, g_q, g_kv, seq):
    def fn(rows, bats, vecs):
        dq, dkv, dkrv, z = rows
        q, kv = z[:, :Q_LORA], z[:, Q_LORA:Q_LORA + KV_LORA]
        dxq, dgq = _norm_bwd(dq, q, _rstd(q), vecs[0])
        dxkv, dgkv = _norm_bwd(dkv, kv, _rstd(kv), vecs[1])
        return [jnp.concatenate([dxq, dxkv, dkrv], axis=1)], [], [_colsum(dgq), _colsum(dgkv)]
    return _rowmap("latent_norm_bwd", fn, seq, [dqn, dkvn, dkr, z_a], vecs=[g_q, g_kv],
                   row_outs=[(Q_LORA + KV_LORA + HEAD_SLAB, BF16)], vec_outs=[Q_LORA, KV_LORA])


def _qk_prep_fwd(qp, kv, z_a, pos, g_q, g_kn, g_kr, inv_freq, seq):
    def fn(rows, bats, vecs):
        qv, kvv, kr, p = rows
        gq, gkn, gkr, invf = vecs
        ang = p * invf
        cos, sin = jnp.cos(ang), jnp.sin(ang)
        nope = _lane(kr.shape) < QK_NOPE
        krn = kr * _rstd(kr, QK_ROPE) * gkr
        krr = krn * cos + _rot(krn) * sin
        qs, ks, vs = [], [], []
        for h in range(N_HEADS):
            xq = qv[:, h * HEAD_SLAB:(h + 1) * HEAD_SLAB]
            y = xq * _part_rstd(xq) * gq
            qs.append(y * cos + _rot(y) * sin)
            xk = kvv[:, h * HEAD_SLAB:(h + 1) * HEAD_SLAB]
            kn = jnp.where(nope, xk, 0.0)
            ks.append(jnp.where(nope, kn * _rstd(kn, QK_NOPE) * gkn, krr))
            vs.append(jnp.where(nope, 0.0, xk))
        return [jnp.concatenate(v, axis=1) for v in (qs, ks, vs)], [], []
    width = N_HEADS * HEAD_SLAB
    return _rowmap("qk_prep", fn, seq, [qp, kv, (z_a, HEAD_SLAB, 5), pos], vecs=[g_q, g_kn, g_kr, inv_freq],
                   row_outs=[(width, BF16)] * 3, ts=_tile(seq, (128, 64, 32, 16, 8)))


def _qk_prep_bwd(dqc, dkc, dvp, qp, kv, z_a, pos, g_q, g_kn, g_kr, inv_freq, seq):
    def fn(rows, bats, vecs):
        dq, dk, dv, qv, kvv, kr, p = rows
        gq, gkn, gkr, invf = vecs
        ang = p * invf
        cos, sin = jnp.cos(ang), jnp.sin(ang)
        nope = _lane(kr.shape) < QK_NOPE
        dqs, dkvs = [], []
        dgq = jnp.zeros((1, HEAD_SLAB), F32)
        dgkn = jnp.zeros((1, HEAD_SLAB), F32)
        dkrr = jnp.zeros(kr.shape, F32)
        for h in range(N_HEADS):
            sl = slice(h * HEAD_SLAB, (h + 1) * HEAD_SLAB)
            dyr = dq[:, sl]
            dy = dyr * cos - _rot(dyr * sin)
            xq = qv[:, sl]
            dx, dg = _part_norm_bwd(dy, xq, _part_rstd(xq), gq)
            dqs.append(dx)
            dgq = dgq + _colsum(dg)
            dkh = dk[:, sl]
            dkrr = dkrr + jnp.where(nope, 0.0, dkh)
            kn = jnp.where(nope, kvv[:, sl], 0.0)
            dxk, dgk = _norm_bwd(jnp.where(nope, dkh, 0.0), kn, _rstd(kn, QK_NOPE), gkn, QK_NOPE)
            dgkn = dgkn + _colsum(dgk)
            dkvs.append(jnp.where(nope, dxk, dv[:, sl]))
        dkrn = dkrr * cos - _rot(dkrr * sin)
        dkr, dgkr = _norm_bwd(dkrn, kr, _rstd(kr, QK_ROPE), gkr, QK_ROPE)
        return ([jnp.concatenate(dqs, axis=1), jnp.concatenate(dkvs, axis=1), dkr], [],
                [dgq, dgkn, _colsum(dgkr)])
    width = N_HEADS * HEAD_SLAB
    return _rowmap("qk_prep_bwd", fn, seq, [dqc, dkc, dvp, qp, kv, (z_a, HEAD_SLAB, 5), pos],
                   vecs=[g_q, g_kn, g_kr, inv_freq],
                   row_outs=[(width, BF16), (width, BF16), (HEAD_SLAB, F32)],
                   vec_outs=[HEAD_SLAB] * 3, ts=_tile(seq, (128, 64, 32, 16, 8)))


def _scores(q, k_ref, keys, tq):
    s = lax.dot_general(q, k_ref[0:keys, :], _DIMS["nt"], preferred_element_type=F32) * ATTN_SCALE
    row = lax.broadcasted_iota(jnp.int32, (tq, tq), 0)
    col = lax.broadcasted_iota(jnp.int32, (tq, tq), 1)
    diag = jnp.where(col <= row, s[:, keys - tq:], -1e30)
    return diag if keys == tq else jnp.concatenate([s[:, :keys - tq], diag], axis=1)


def _attn_fwd(qc, kc, vp, seq):
    tokens = qc.shape[0]
    tq = _tile(seq, (256, 128))
    nq = seq // tq

    def body(q_ref, k_ref, v_ref, o_ref, lse_ref):
        for i in range(nq):
            rows, keys = slice(i * tq, (i + 1) * tq), (i + 1) * tq
            s = _scores(q_ref[rows, :], k_ref, keys, tq)
            m = jnp.max(s, axis=-1, keepdims=True)
            p = jnp.exp(s - m)
            l = jnp.sum(p, axis=-1, keepdims=True)
            acc = jnp.dot(p.astype(BF16), v_ref[0:keys, :], preferred_element_type=F32)
            o_ref[rows, :] = (acc / l).astype(BF16)
            lse_ref[rows, :] = jnp.broadcast_to(m + jnp.log(l), (tq, HEAD_SLAB))

    spec = pl.BlockSpec((seq, HEAD_SLAB), lambda b, h: (b, h))
    return pl.pallas_call(
        body, name="attn_fwd", grid=(tokens // seq, N_HEADS),
        out_shape=(jax.ShapeDtypeStruct(qc.shape, BF16), jax.ShapeDtypeStruct(qc.shape, F32)),
        in_specs=[spec] * 3, out_specs=(spec, spec),
        compiler_params=_params("parallel", "parallel"),
    )(qc, kc, vp)


def _attn_bwd(qc, kc, vp, o, lse, do, seq):
    tokens = qc.shape[0]
    tq = _tile(seq, (256, 128))
    nq = seq // tq

    def body(q_ref, k_ref, v_ref, o_ref, lse_ref, do_ref, dq_ref, dk_ref, dv_ref):
        dk_ref[...] = jnp.zeros(dk_ref.shape, F32)
        dv_ref[...] = jnp.zeros(dv_ref.shape, F32)
        for i in range(nq):
            rows, keys = slice(i * tq, (i + 1) * tq), (i + 1) * tq
            q, dov = q_ref[rows, :], do_ref[rows, :]
            delta = jnp.sum(dov.astype(F32) * o_ref[rows, :].astype(F32), axis=-1, keepdims=True)
            s = _scores(q, k_ref, keys, tq)
            p = jnp.exp(s - jnp.tile(lse_ref[rows, :], (1, keys // HEAD_SLAB)))
            dp = lax.dot_general(dov, v_ref[0:keys, :], _DIMS["nt"], preferred_element_type=F32)
            ds = (p * (dp - delta) * ATTN_SCALE).astype(BF16)
            dq_ref[rows, :] = jnp.dot(ds, k_ref[0:keys, :], preferred_element_type=F32)
            dk_ref[0:keys, :] += lax.dot_general(ds, q, _DIMS["tn"], preferred_element_type=F32)
            dv_ref[0:keys, :] += lax.dot_general(p.astype(BF16), dov, _DIMS["tn"], preferred_element_type=F32)

    spec = pl.BlockSpec((seq, HEAD_SLAB), lambda b, h: (b, h))
    out = jax.ShapeDtypeStruct(qc.shape, F32)
    return pl.pallas_call(
        body, name="attn_bwd", grid=(tokens // seq, N_HEADS),
        out_shape=(out, out, out), in_specs=[spec] * 6, out_specs=(spec, spec, spec),
        compiler_params=_params("parallel", "parallel"),
    )(qc, kc, vp, o, lse, do)


def _adamw(w, g, m, v, name):
    rows, cols = w.shape
    whole = rows * cols * 4 <= ADAMW_WHOLE_BYTES
    tr = rows if whole else _tile(rows, (256, 128, 64, 32, 16, 8))
    c1 = 1.0 - ADAM_B1 ** ADAM_STEP
    c2 = 1.0 - ADAM_B2 ** ADAM_STEP

    def body(w_ref, g_ref, m_ref, v_ref, d_ref, nm_ref, nv_ref):
        gv = g_ref[...]
        nm = ADAM_B1 * m_ref[...] + (1.0 - ADAM_B1) * gv
        nv = ADAM_B2 * v_ref[...] + (1.0 - ADAM_B2) * (gv * gv)
        d_ref[...] = -ADAM_LR * ((nm / c1) / (jnp.sqrt(nv / c2) + ADAM_EPS) + ADAM_WD * w_ref[...])
        nm_ref[...] = nm
        nv_ref[...] = nv

    spec = pl.BlockSpec((tr, cols), lambda i: (i, 0))
    out = jax.ShapeDtypeStruct(w.shape, F32)
    return pl.pallas_call(
        body, name=name, grid=(rows // tr,), out_shape=(out, out, out),
        in_specs=[spec] * 4, out_specs=(spec, spec, spec),
        compiler_params=_params("parallel"),
    )(w, g, m, v)


def _mod_cols(c_all, w_ada, b_cols):
    def body(c_ref, w_ref, b_ref, act_ref, mod_ref):
        cv = c_ref[...]
        act = cv * _sigmoid(cv)
        act_ref[...] = act
        mod_ref[...] = jnp.dot(act.astype(BF16), w_ref[...].astype(BF16),
                               preferred_element_type=F32) + b_ref[...]

    n = w_ada.shape[1]
    return pl.pallas_call(
        body, name="mod_cols",
        out_shape=(jax.ShapeDtypeStruct(c_all.shape, F32), jax.ShapeDtypeStruct((c_all.shape[0], n), F32)),
        compiler_params=pltpu.CompilerParams(vmem_limit_bytes=VMEM_LIMIT),
    )(c_all, w_ada, b_cols)


def _ada_grads(c_act, dmod_all, dmod_cols):
    def body(c_ref, d_ref, dc_ref, gw_ref, gb_ref):
        gw_ref[...] = lax.dot_general(c_ref[...].astype(BF16), dc_ref[...].astype(BF16), _DIMS["tn"],
                                      preferred_element_type=F32)
        gb_ref[...] = _colsum(d_ref[...])

    return pl.pallas_call(
        body, name="ada_grads",
        out_shape=(jax.ShapeDtypeStruct((c_act.shape[1], dmod_cols.shape[1]), F32),
                   jax.ShapeDtypeStruct((1, dmod_all.shape[1]), F32)),
        compiler_params=pltpu.CompilerParams(vmem_limit_bytes=VMEM_LIMIT),
    )(c_act, dmod_all, dmod_cols)


def _flat_rows(a):
    flat = a.reshape(-1)
    pad = (-flat.shape[0]) % (LANES * SUBLANES)
    if pad:
        flat = jnp.pad(flat, (0, pad))
    return flat.reshape(-1, LANES)


def _gather_start(w, names, tag, after):
    shards = [(w[n] if n in ROW_SHARDED else w[n].T).astype(BF16) for n in names]
    return _exchange_start(shards, f"gather_{tag}_start", after=after)


def _gather_wait(handle, names, tag, after):
    landed = _exchange_wait(handle, f"gather_{tag}_wait", after=after)
    return {n: g.reshape(-1, g.shape[2]) for n, g in zip(names, landed)}


def _scatter_start(grads, names, tag, after=None):
    blocks = [grads[n].reshape(N_DEV, -1, grads[n].shape[1]) for n in names]
    return _exchange_start(blocks, f"scatter_{tag}_start", scatter=True, after=after)


def _scatter_wait(handle, names, tag, after):
    landed = _exchange_wait(handle, f"scatter_{tag}_wait", scatter=True, after=after)
    out = {}
    for n, x in zip(names, landed):
        g = _sum_blocks(x, f"sum_{n}")
        out[n] = g if n in ROW_SHARDED else g.T
    return out


def _pack_small(vals):
    return jnp.concatenate([_flat_rows(v.astype(F32)) for v in vals], axis=0)


def _unpack_small(packed, like):
    out, row = [], 0
    for v in like:
        rows = _flat_rows(v).shape[0]
        out.append(packed[row:row + rows].reshape(-1)[:v.size].reshape(v.shape))
        row += rows
    return out


def _lanes128(*parts):
    out = jnp.zeros((HEAD_SLAB,), F32)
    for off, v in parts:
        out = lax.dynamic_update_slice(out, v.reshape(-1).astype(F32), (off,))
    return out.reshape(1, HEAD_SLAB)


def _step(x, c, positions, w, m, v, loss_target):
    nseq, seq, _ = x.shape
    tokens = nseq * seq
    me = _index(_my_pos())
    strip = lambda d: {n: (a[0] if a.ndim > 2 else a) for n, a in d.items()}
    shapes = {n: a.shape for n, a in w.items()}
    w, m, v = strip(w), strip(m), strip(v)

    c_all = _all_gather(c.reshape(-1, LANES), "gather_c").reshape(N_DEV * nseq, D_MODEL)
    n_ada = w["w_ada"].shape[1]
    b_cols = lax.dynamic_slice(w["b_ada"], (0, me * n_ada), (1, n_ada))
    c_act, mod_cols = _mod_cols(c_all, w["w_ada"], b_cols)
    mod_all = _all_gather(mod_cols, "gather_mod")
    mod = lax.dynamic_slice(mod_all, (0, me * nseq, 0), (N_DEV, nseq, n_ada))
    mod = mod.transpose(1, 0, 2).reshape(nseq, 3, 3, 1, D_MODEL)

    h_f1i, tok = _gather_start(w, ("w_ffn1_in",), "ffn1_in", after=mod_all)
    h_f1o, tok = _gather_start(w, ("w_ffn1_out",), "ffn1_out", after=tok)
    h_mix_in, tok = _gather_start(w, MIXER[:1], "mix_in", after=tok)
    h_mix, tok = _gather_start(w, MIXER[1:], "mix", after=tok)
    h_f2, tok = _gather_start(w, ("w_ffn2_in", "w_ffn2_out"), "ffn2", after=tok)
    started = tok[0:1, 0:1]

    g_q = _lanes128((0, w["q_norm_nope"]), (QK_NOPE, w["q_norm_rope"]))
    g_kn = _lanes128((0, w["k_norm_nope"]))
    g_kr = _lanes128((QK_NOPE, w["k_norm_rope"]))
    freq = ROPE_THETA ** (-jnp.arange(0, QK_ROPE, 2, dtype=F32) / QK_ROPE)
    inv_freq = _lanes128((QK_NOPE, jnp.concatenate([freq, freq])))
    pos = positions.reshape(tokens, 1).astype(F32)

    def sub(k, gamma, coef):
        return dict(gamma=w[gamma], shift=mod[:, k, 0] + started, scale=mod[:, k, 1], gate=coef * mod[:, k, 2])
    p1, pm, p2 = sub(0, "norm_ffn1", 0.5), sub(1, "norm_mix", 1.0), sub(2, "norm_ffn2", 0.5)
    t_big = _tile(tokens, (2048, 1024, 512))
    t_mid = _tile(tokens, (1024, 512))

    x0 = x.reshape(tokens, D_MODEL)
    h1 = _norm_mod_fwd(x0, p1, seq, "ffn1_norm")
    wt_f1i = _gather_wait(h_f1i, ("w_ffn1_in",), "ffn1_in", h1)["w_ffn1_in"]
    g1, u1, a1 = _ffn_in_act(h1, wt_f1i, "ffn1_in")
    w_f1o = _gather_wait(h_f1o, ("w_ffn1_out",), "ffn1_out", a1)["w_ffn1_out"]
    x1, f1, h2 = _out_residual(a1, w_f1o, x0, p1["gate"], pm, seq, "ffn1_out")
    saved1 = (x0, h1, g1, u1, a1, wt_f1i, w_f1o)

    wt_in = _gather_wait(h_mix_in, MIXER[:1], "mix_in", h2)["w_in"]
    zero_rows = lambda rows: jnp.zeros((rows, D_MODEL), BF16)
    wt_p = wt_in[:512]
    wt_a = jnp.concatenate([wt_in[512:1152], zero_rows(QK_NOPE), wt_in[1152:1184], zero_rows(32)], axis=0)
    wt_g = wt_in[1184:]
    z_a = _mm(h2, wt_a, "nt", "mix_in_a", tm=t_big, tn=wt_a.shape[0])
    z_p = _mm(h2, wt_p, "nt", "mix_in_p", tm=t_big, tn=512)
    z_g = _mm(h2, wt_g, "nt", "mix_in_g", tm=t_big, tn=512)

    full = _gather_wait(h_mix, MIXER[1:], "mix", z_g)
    wtq_pad = jnp.pad(full["w_q_up"].reshape(N_HEADS, 96, Q_LORA), ((0, 0), (0, 32), (0, 0))).reshape(-1, Q_LORA)
    wtmla_pad = jnp.pad(full["w_mla_proj"].reshape(D_MODEL, N_HEADS, 64), ((0, 0), (0, 0), (64, 0))).reshape(D_MODEL, -1)
    wt_pool, wt_kv, w_mix_out = full["w_pool_proj"], full["w_kv_up"], full["w_out"]
    pooled, pg, ps = _pool_fwd(z_p, w["pool_grp"], w["pool_scale"], seq)
    br_pool = _mm(ps, wt_pool, "nt", "pool_proj", tm=t_big, tn=D_MODEL)
    qn, kvn = _latent_norm_fwd(z_a, w["q_a_norm"], w["kv_a_norm"], seq)
    qp = _mm(qn, wtq_pad, "nt", "q_up", tm=t_big, tn=D_MODEL)
    kv = _mm(kvn, wt_kv, "nt", "kv_up", tm=t_big, tn=D_MODEL)
    qc, kc, vp = _qk_prep_fwd(qp, kv, z_a, pos, g_q, g_kn, g_kr, inv_freq, seq)
    attn, lse = _attn_fwd(qc, kc, vp, seq)
    br_mla = _mm(attn, wtmla_pad, "nt", "mla_proj", tm=t_mid, tn=D_MODEL)

    def merge(rows, bats, vecs):
        zg, bp, bm = rows
        return [_sigmoid(zg[:, :D_MODEL]) * bp + _sigmoid(zg[:, D_MODEL:]) * bm], [], []
    merged = _rowmap("merge", merge, seq, [z_g, br_pool, br_mla], row_outs=[(D_MODEL, BF16)])[0]
    x2, o_mix, h3 = _out_residual(merged, w_mix_out, x1, pm["gate"], p2, seq, "mix_out")

    ffn2_w = _gather_wait(h_f2, ("w_ffn2_in", "w_ffn2_out"), "ffn2", h3)
    g2, u2, a2 = _ffn_in_act(h3, ffn2_w["w_ffn2_in"], "ffn2_in")
    dy, df2, dgate2, sq_err = _out_loss(a2, ffn2_w["w_ffn2_out"], x2, p2["gate"],
                                        loss_target.reshape(tokens, D_MODEL), seq, "ffn2_out")
    saved2 = (x2, h3, g2, u2, a2, ffn2_w["w_ffn2_in"], ffn2_w["w_ffn2_out"])
    loss = lax.psum(0.5 * jnp.sum(sq_err) * (1.0 / D_MODEL), AXES)

    grads = {}
    (dx2, do_mix, dsh2, dsc2, dgate_m, dg_ffn2), ops2 = _ffn_bwd_x(df2, dy, saved2, p2, seq, "ffn2", (o_mix, pm["gate"]))
    grads["w_ffn2_out"], grads["w_ffn2_in"] = _ffn_bwd_wout(ops2, "ffn2"), _ffn_bwd_win(ops2, "ffn2")
    s_f2, tok = _scatter_start(grads, ("w_ffn2_in", "w_ffn2_out"), "ffn2")

    dmerged = _mm(do_mix, w_mix_out, "nt", "mix_bwd_dmerged", tm=t_mid, tn=D_MODEL)
    grads["w_out"] = _mm(merged, do_mix, "tn", "mix_bwd_wout", out_dtype=BF16, tm=512, tn=D_MODEL)

    def merge_bwd(rows, bats, vecs):
        dmv, zg, bp, bm = rows
        s_p, s_m = _sigmoid(zg[:, :D_MODEL]), _sigmoid(zg[:, D_MODEL:])
        dzg = jnp.concatenate([dmv * bp * s_p * (1.0 - s_p), dmv * bm * s_m * (1.0 - s_m)], axis=1)
        return [dmv * s_p, dmv * s_m, dzg], [], []
    dbr_pool, dbr_mla, dz_g = _rowmap("merge_bwd", merge_bwd, seq, [dmerged, z_g, br_pool, br_mla],
                                      row_outs=[(D_MODEL, BF16), (D_MODEL, BF16), (2 * D_MODEL, BF16)])

    grads["w_pool_proj"] = _mm(dbr_pool, ps, "tn", "pool_bwd_wproj", out_dtype=BF16, tm=512, tn=POOL_WIDTH)
    dps = _mm(dbr_pool, wt_pool, "nn", "pool_bwd_dps", tm=t_big, tn=POOL_WIDTH)
    dz_p, dgrp, dpool_scale = _pool_bwd(dps, pooled, pg, w["pool_grp"], w["pool_scale"] + tok[0:1, 0:1], seq)

    dwtmla_pad = _mm(dbr_mla, attn, "tn", "mla_bwd_wproj", out_dtype=BF16, tm=512, tn=D_MODEL)
    grads["w_mla_proj"] = dwtmla_pad.reshape(D_MODEL, N_HEADS, HEAD_SLAB)[:, :, 64:].reshape(D_MODEL, -1)
    d_attn = _mm(dbr_mla, wtmla_pad, "nn", "mla_bwd_dattn", out_dtype=BF16, tm=t_mid, tn=D_MODEL)
    dqc, dkc, dvp = _attn_bwd(qc, kc, vp, attn, lse, d_attn, seq)
    dqp, dkv, dkr, dg_q, dg_kn, dg_kr = _qk_prep_bwd(dqc, dkc, dvp, qp, kv, z_a, pos, g_q, g_kn, g_kr, inv_freq, seq)
    dwtq_pad = _mm(dqp, qn, "tn", "q_up_bwd_w", out_dtype=BF16, tm=512, tn=Q_LORA)
    grads["w_q_up"] = dwtq_pad.reshape(N_HEADS, HEAD_SLAB, Q_LORA)[:, :96].reshape(-1, Q_LORA)
    grads["w_kv_up"] = _mm(dkv, kvn, "tn", "kv_up_bwd_w", out_dtype=BF16, tm=512, tn=KV_LORA)
    dqn = _mm(dqp, wtq_pad, "nn", "q_up_bwd_x", tm=t_big, tn=Q_LORA)
    dkvn = _mm(dkv, wt_kv, "nn", "kv_up_bwd_x", tm=t_big, tn=KV_LORA)
    dz_a, dg_qa, dg_kva = _latent_norm_bwd(dqn, dkvn, dkr, z_a, w["q_a_norm"], w["kv_a_norm"], seq)

    dwt_a = _mm(dz_a, h2, "tn", "mix_in_bwd_wa", out_dtype=BF16, tm=256, tn=D_MODEL)
    dwt_p = _mm(dz_p, h2, "tn", "mix_in_bwd_wp", out_dtype=BF16, tm=512, tn=D_MODEL)
    dwt_g = _mm(dz_g, h2, "tn", "mix_in_bwd_wg", out_dtype=BF16, tm=512, tn=D_MODEL)
    grads["w_in"] = jnp.concatenate([dwt_p, dwt_a[:640], dwt_a[704:736], dwt_g], axis=0)
    s_mix, tok = _scatter_start(grads, MIXER, "mix")
    dh2 = _mm(dz_a, wt_a, "nn", "mix_in_bwd_xa", tm=t_mid, tn=D_MODEL)
    dh2 = _mm(dz_p, wt_p, "nn", "mix_in_bwd_xp", tm=t_mid, tn=D_MODEL, add=dh2)
    dh2 = _mm(dz_g, wt_g, "nn", "mix_in_bwd_xg", tm=t_mid, tn=D_MODEL, add=dh2)
    pm_tied = dict(pm, scale=pm["scale"] + tok[0:1, 0:1])
    dx1, df1, dsh_m, dsc_m, dgate1, dg_mix = _norm_mod_bwd(dh2, x1, dx2, pm_tied, seq, "mix_bwd_norm", (f1, p1["gate"]))

    small_early = [dg_mix.reshape(w["norm_mix"].shape), dg_ffn2.reshape(w["norm_ffn2"].shape), dgrp, dpool_scale,
                   dg_qa, dg_kva, dg_q[:, :QK_NOPE], dg_q[:, QK_NOPE:QK_NOPE + QK_ROPE], dg_kn[:, :QK_NOPE],
                   dg_kr[:, QK_NOPE:QK_NOPE + QK_ROPE]]
    s_small, tok = _exchange_start([_pack_small(small_early)], "gather_small_start", after=tok)

    handles = {}

    def ffn1_mid(operands):
        grads["w_ffn1_out"] = _ffn_bwd_wout(operands, "ffn1")
        handles["f1o"], token = _scatter_start(grads, ("w_ffn1_out",), "ffn1_out")
        return token

    p1_tied = dict(p1, scale=p1["scale"] + tok[0:1, 0:1])
    (dx0, dsh1, dsc1, dg_ffn1), ops1 = _ffn_bwd_x(df1, dx1, saved1, p1_tied, seq, "ffn1", mid=ffn1_mid)
    s_f1o = handles["f1o"]

    dmod = jnp.stack([jnp.stack([dsh1, dsc1, 0.5 * dgate1], axis=1),
                      jnp.stack([dsh_m, dsc_m, dgate_m], axis=1),
                      jnp.stack([dsh2, dsc2, 0.5 * dgate2], axis=1)], axis=1)
    n_dmod = nseq * 9 * D_MODEL // LANES
    tail = _all_gather(jnp.concatenate([dmod.reshape(-1, LANES), _flat_rows(dg_ffn1)], axis=0), "gather_dmod")
    dmod_all = tail[:, :n_dmod].reshape(N_DEV * nseq, 9 * D_MODEL)

    grads["w_ffn1_in"] = _ffn_bwd_win(ops1, "ffn1", after=tail)
    s_f1i, tok = _scatter_start(grads, ("w_ffn1_in",), "ffn1_in", after=tail)

    dmod_cols = lax.dynamic_slice(dmod_all, (0, me * n_ada), (N_DEV * nseq, n_ada)) + tok[0:1, 0:1]
    g_w_ada, g_b_ada = _ada_grads(c_act, dmod_all, dmod_cols)
    g_norm_ffn1 = _sum_blocks(tail[:, n_dmod:], "sum_norm_ffn1").reshape(1, D_MODEL)
    small_all = _exchange_wait(s_small, "gather_small_wait", after=g_b_ada)[0]
    small_sum = _sum_blocks(small_all, "sum_small")
    small = dict(zip(SMALL[1:], _unpack_small(small_sum, [w[n] for n in SMALL[1:]])))
    grad_w = dict(small, w_ada=g_w_ada, b_ada=g_b_ada, norm_ffn1=g_norm_ffn1)

    delta, new_m, new_v = {}, {}, {}

    def update(names):
        for n in names:
            delta[n], new_m[n], new_v[n] = _adamw(w[n], grad_w[n], m[n], v[n], f"adamw_{n}")

    update(("w_ada",))
    rep = ("b_ada",) + SMALL
    d_s, m_s, v_s = _adamw(_pack_small([w[n] for n in rep]), _pack_small([grad_w[n] for n in rep]),
                           _pack_small([m[n] for n in rep]), _pack_small([v[n] for n in rep]), "adamw_small")
    like = [w[n] for n in rep]
    for dst, packed in ((delta, d_s), (new_m, m_s), (new_v, v_s)):
        dst.update(zip(rep, _unpack_small(packed, like)))
    grad_w.update(_scatter_wait(s_f2, ("w_ffn2_in", "w_ffn2_out"), "ffn2", after=d_s))
    update(("w_ffn2_in", "w_ffn2_out"))
    grad_w.update(_scatter_wait(s_mix, MIXER, "mix", after=delta["w_ffn2_out"]))
    update(MIXER)
    grad_w.update(_scatter_wait(s_f1o, ("w_ffn1_out",), "ffn1_out", after=delta["w_out"]))
    update(("w_ffn1_out",))
    grad_w.update(_scatter_wait(s_f1i, ("w_ffn1_in",), "ffn1_in", after=delta["w_ffn1_out"]))
    update(("w_ffn1_in",))

    lead = lambda d: [d[n].reshape(shapes[n]) for n in WEIGHTS]
    return (loss, dx0.reshape(x.shape), *lead(grad_w), *lead(delta), *lead(new_m), *lead(new_v))


def kernel(x, c, positions, w_ada, b_ada, norm_ffn1, w_ffn1_in, w_ffn1_out, norm_mix, w_in, pool_grp, pool_scale, w_pool_proj, q_a_norm, w_q_up, kv_a_norm, w_kv_up, q_norm_nope, q_norm_rope, k_norm_nope, k_norm_rope, w_mla_proj, w_out, norm_ffn2, w_ffn2_in, w_ffn2_out, loss_target, m_w_ada, m_b_ada, m_norm_ffn1, m_w_ffn1_in, m_w_ffn1_out, m_norm_mix, m_w_in, m_pool_grp, m_pool_scale, m_w_pool_proj, m_q_a_norm, m_w_q_up, m_kv_a_norm, m_w_kv_up, m_q_norm_nope, m_q_norm_rope, m_k_norm_nope, m_k_norm_rope, m_w_mla_proj, m_w_out, m_norm_ffn2, m_w_ffn2_in, m_w_ffn2_out, v_w_ada, v_b_ada, v_norm_ffn1, v_w_ffn1_in, v_w_ffn1_out, v_norm_mix, v_w_in, v_pool_grp, v_pool_scale, v_w_pool_proj, v_q_a_norm, v_w_q_up, v_kv_a_norm, v_w_kv_up, v_q_norm_nope, v_q_norm_rope, v_k_norm_nope, v_k_norm_rope, v_w_mla_proj, v_w_out, v_norm_ffn2, v_w_ffn2_in, v_w_ffn2_out):
    w = dict(w_ada=w_ada, b_ada=b_ada, norm_ffn1=norm_ffn1, w_ffn1_in=w_ffn1_in, w_ffn1_out=w_ffn1_out, norm_mix=norm_mix, w_in=w_in, pool_grp=pool_grp, pool_scale=pool_scale, w_pool_proj=w_pool_proj, q_a_norm=q_a_norm, w_q_up=w_q_up, kv_a_norm=kv_a_norm, w_kv_up=w_kv_up, q_norm_nope=q_norm_nope, q_norm_rope=q_norm_rope, k_norm_nope=k_norm_nope, k_norm_rope=k_norm_rope, w_mla_proj=w_mla_proj, w_out=w_out, norm_ffn2=norm_ffn2, w_ffn2_in=w_ffn2_in, w_ffn2_out=w_ffn2_out)
    m = dict(w_ada=m_w_ada, b_ada=m_b_ada, norm_ffn1=m_norm_ffn1, w_ffn1_in=m_w_ffn1_in, w_ffn1_out=m_w_ffn1_out, norm_mix=m_norm_mix, w_in=m_w_in, pool_grp=m_pool_grp, pool_scale=m_pool_scale, w_pool_proj=m_w_pool_proj, q_a_norm=m_q_a_norm, w_q_up=m_w_q_up, kv_a_norm=m_kv_a_norm, w_kv_up=m_w_kv_up, q_norm_nope=m_q_norm_nope, q_norm_rope=m_q_norm_rope, k_norm_nope=m_k_norm_nope, k_norm_rope=m_k_norm_rope, w_mla_proj=m_w_mla_proj, w_out=m_w_out, norm_ffn2=m_norm_ffn2, w_ffn2_in=m_w_ffn2_in, w_ffn2_out=m_w_ffn2_out)
    v = dict(w_ada=v_w_ada, b_ada=v_b_ada, norm_ffn1=v_norm_ffn1, w_ffn1_in=v_w_ffn1_in, w_ffn1_out=v_w_ffn1_out, norm_mix=v_norm_mix, w_in=v_w_in, pool_grp=v_pool_grp, pool_scale=v_pool_scale, w_pool_proj=v_w_pool_proj, q_a_norm=v_q_a_norm, w_q_up=v_w_q_up, kv_a_norm=v_kv_a_norm, w_kv_up=v_w_kv_up, q_norm_nope=v_q_norm_nope, q_norm_rope=v_q_norm_rope, k_norm_nope=v_k_norm_nope, k_norm_rope=v_k_norm_rope, w_mla_proj=v_w_mla_proj, w_out=v_w_out, norm_ffn2=v_norm_ffn2, w_ffn2_in=v_w_ffn2_in, w_ffn2_out=v_w_ffn2_out)
    return _step(x, c, positions, w, m, v, loss_target)
```

```python
import functools
import math

import jax
import jax.numpy as jnp
from jax import lax
from jax.experimental import pallas as pl
from jax.experimental.pallas import tpu as pltpu

F32 = jnp.float32
BF16 = jnp.bfloat16
MESH = pl.DeviceIdType.MESH
AXES = ("x", "y", "c")
N_DEV = 8

D_MODEL = 1024
D_FF = 2816
N_HEADS = 8
HEAD_SLAB = 128
QK_NOPE = 64
QK_ROPE = 32
POOL_WIDTH = 512
POOL_GROUPS = 4
POOL_GROUP_DIM = 128
Q_LORA = 384
KV_LORA = 256
ROPE_THETA = 10000.0
ATTN_SCALE = 1.0 / math.sqrt(QK_NOPE + QK_ROPE)
NORM_EPS = 1e-6
ADAM_LR, ADAM_B1, ADAM_B2, ADAM_EPS, ADAM_WD, ADAM_STEP = 0.001, 0.9, 0.999, 1e-08, 0.01, 10

LANES = 128
SUBLANES = 8
VMEM_LIMIT = 52 * 1024 * 1024
ADAMW_WHOLE_BYTES = 3 << 19
SUM_WHOLE_BYTES = 4 << 20

BIG = ("w_ffn1_in", "w_ffn1_out", "w_in", "w_pool_proj", "w_q_up", "w_kv_up",
       "w_mla_proj", "w_out", "w_ffn2_in", "w_ffn2_out")
ROW_SHARDED = ("w_ffn1_out", "w_out", "w_ffn2_out")
MIXER = ("w_in", "w_pool_proj", "w_q_up", "w_kv_up", "w_mla_proj", "w_out")
SMALL = ("norm_ffn1", "norm_mix", "norm_ffn2", "pool_grp", "pool_scale", "q_a_norm",
         "kv_a_norm", "q_norm_nope", "q_norm_rope", "k_norm_nope", "k_norm_rope")
WEIGHTS = ("w_ada", "b_ada", "norm_ffn1", "w_ffn1_in", "w_ffn1_out", "norm_mix", "w_in",
           "pool_grp", "pool_scale", "w_pool_proj", "q_a_norm", "w_q_up", "kv_a_norm",
           "w_kv_up", "q_norm_nope", "q_norm_rope", "k_norm_nope", "k_norm_rope",
           "w_mla_proj", "w_out", "norm_ffn2", "w_ffn2_in", "w_ffn2_out")


def _params(*sem):
    return pltpu.CompilerParams(dimension_semantics=sem, vmem_limit_bytes=VMEM_LIMIT)


def _tile(n, cands):
    for c in cands:
        if n % c == 0:
            return c
    return n


def _my_pos():
    return lax.axis_index("x"), lax.axis_index("y"), lax.axis_index("c")


def _flip(pos, k):
    x, y, c = pos
    fx, fy, fc = (k >> 2) & 1, (k >> 1) & 1, k & 1
    return ((1 - x) if fx else x, (1 - y) if fy else y, (1 - c) if fc else c)


def _index(pos):
    x, y, c = pos
    return 4 * x + 2 * y + c


def _exchange(arrays, name, scatter=False):
    n = len(arrays)

    def body(*refs):
        ins, outs = refs[:n], refs[n:2 * n]
        send_sems, recv_sems, local_sems = refs[2 * n:]
        me = _my_pos()
        mine, sends = [], []
        for a in range(n):
            own = ins[a].at[_index(me)] if scatter else ins[a]
            cp = pltpu.make_async_copy(own, outs[a].at[_index(me)], local_sems.at[a])
            cp.start()
            mine.append(cp)
        for k in range(1, N_DEV):
            peer = _flip(me, k)
            for a in range(n):
                cp = pltpu.make_async_remote_copy(
                    src_ref=ins[a].at[_index(peer)] if scatter else ins[a],
                    dst_ref=outs[a].at[_index(me)],
                    send_sem=send_sems.at[a, k - 1], recv_sem=recv_sems.at[a, k - 1],
                    device_id=peer, device_id_type=MESH)
                cp.start()
                sends.append(cp)
        for k in range(1, N_DEV):
            peer = _flip(me, k)
            for a in range(n):
                pltpu.make_async_remote_copy(
                    src_ref=ins[a].at[_index(me)] if scatter else ins[a],
                    dst_ref=outs[a].at[_index(peer)],
                    send_sem=send_sems.at[a, k - 1], recv_sem=recv_sems.at[a, k - 1],
                    device_id=peer, device_id_type=MESH).wait_recv()
        for cp in sends:
            cp.wait_send()
        for cp in mine:
            cp.wait()

    shape = lambda x: x.shape if scatter else (N_DEV,) + x.shape
    return pl.pallas_call(
        body, name=name,
        out_shape=tuple(jax.ShapeDtypeStruct(shape(x), x.dtype) for x in arrays),
        in_specs=[pl.BlockSpec(memory_space=pl.ANY)] * n,
        out_specs=tuple(pl.BlockSpec(memory_space=pl.ANY) for _ in arrays),
        scratch_shapes=[pltpu.SemaphoreType.DMA((n, N_DEV - 1)),
                        pltpu.SemaphoreType.DMA((n, N_DEV - 1)),
                        pltpu.SemaphoreType.DMA((n,))],
    )(*arrays)


def _all_gather(x, name):
    return _exchange([x], name)[0]


_HBM = pl.BlockSpec(memory_space=pltpu.HBM)
_SEM = pl.BlockSpec(memory_space=pltpu.SEMAPHORE)
_ANY = pl.BlockSpec(memory_space=pl.ANY)
_EFFECT = pltpu.SideEffectType.DATAFLOW_SIDE_EFFECTING


def _split_copy(ins, lands, send_sems, recv_sems, a, k, me, scatter, incoming):
    peer = _flip(me, k)
    block = me if incoming else peer
    return pltpu.make_async_remote_copy(
        src_ref=ins[a].at[_index(block)] if scatter else ins[a],
        dst_ref=lands[a].at[_index(peer if incoming else me)],
        send_sem=send_sems.at[a * (N_DEV - 1) + k - 1], recv_sem=recv_sems.at[a * (N_DEV - 1) + k - 1],
        device_id=peer, device_id_type=MESH)


def _exchange_start_groups(groups, name, scatter=False, after=None):
    sizes = [len(g) for g in groups]
    first = [sum(sizes[:i]) for i in range(len(sizes))]
    n, ng = sum(sizes), len(sizes)
    after = jnp.zeros((SUBLANES, LANES), F32) if after is None else after

    def body(*refs):
        ins, lands = refs[:n], refs[n:2 * n]
        sems = refs[2 * n + 1:2 * n + 1 + 2 * ng]
        me = _my_pos()
        for g in range(ng):
            part = slice(first[g], first[g] + sizes[g])
            for k in range(1, N_DEV):
                for a in range(sizes[g]):
                    _split_copy(ins[part], lands[part], sems[2 * g], sems[2 * g + 1], a, k, me, scatter, False).start()
        refs[-1][...] = jnp.zeros((SUBLANES, LANES), F32)

    shape = lambda x: x.shape if scatter else (N_DEV,) + x.shape
    hbm = lambda x: pltpu.with_memory_space_constraint(x, pltpu.HBM)
    srcs = [hbm(x) for g in groups for x in g]
    zones = [hbm(lax.empty(shape(x), x.dtype)) for g in groups for x in g]
    sem_shapes = [pltpu.SemaphoreType.DMA((s * (N_DEV - 1),)) for s in sizes for _ in range(2)]
    out = pl.pallas_call(
        body, name=name,
        out_shape=(*sem_shapes, *[pltpu.HBM(x.shape, x.dtype) for x in srcs + zones],
                   jax.ShapeDtypeStruct((SUBLANES, LANES), F32)),
        in_specs=[_HBM] * (2 * n) + [_ANY],
        out_specs=(*[_SEM] * (2 * ng), *[_HBM] * (2 * n), pl.BlockSpec(memory_space=pltpu.VMEM)),
        input_output_aliases={i: 2 * ng + i for i in range(2 * n)},
        compiler_params=pltpu.CompilerParams(has_side_effects=_EFFECT),
    )(*srcs, *zones, after)
    bufs = out[2 * ng:-1]
    handles = [(out[2 * g], out[2 * g + 1], *bufs[first[g]:first[g] + sizes[g]],
                *bufs[n + first[g]:n + first[g] + sizes[g]]) for g in range(ng)]
    return handles, out[-1]


def _exchange_start(arrays, name, scatter=False, after=None):
    handles, token = _exchange_start_groups([arrays], name, scatter, after)
    return handles[0], token


def _exchange_wait(handle, name, scatter=False, after=None):
    send_sems, recv_sems = handle[0], handle[1]
    n = (len(handle) - 2) // 2
    after = jnp.zeros((SUBLANES, LANES), F32) if after is None else after

    def body(*refs):
        ins, lands = refs[:n], refs[n:2 * n]
        send, recv = refs[2 * n], refs[2 * n + 1]
        me = _my_pos()
        for k in range(1, N_DEV):
            for a in range(n):
                _split_copy(ins, lands, send, recv, a, k, me, scatter, False).wait_send()
                _split_copy(ins, lands, send, recv, a, k, me, scatter, True).wait_recv()

    bufs = handle[2:]
    out = pl.pallas_call(
        body, name=name,
        out_shape=tuple(pltpu.HBM(x.shape, x.dtype) for x in bufs),
        in_specs=[_HBM] * (2 * n) + [_SEM, _SEM, _ANY],
        out_specs=tuple([_HBM] * (2 * n)),
        input_output_aliases={i: i for i in range(2 * n)},
        compiler_params=pltpu.CompilerParams(has_side_effects=_EFFECT),
    )(*bufs, send_sems, recv_sems, after)
    me = _index(_my_pos())
    landed = []
    for src, land in zip(out[:n], out[n:]):
        own = lax.dynamic_slice_in_dim(src, me, 1, axis=0) if scatter else src[None]
        landed.append(lax.dynamic_update_slice_in_dim(land, own, me, axis=0))
    return landed


def _sum_blocks(x, name):
    n, rows, cols = x.shape
    whole = x.size * x.dtype.itemsize <= SUM_WHOLE_BYTES
    tr = rows if whole else _tile(rows, (512, 256, 128, 64, 32, 16, 8))

    def body(x_ref, o_ref):
        acc = x_ref[0].astype(F32)
        for d in range(1, n):
            acc = acc + x_ref[d].astype(F32)
        o_ref[...] = acc

    return pl.pallas_call(
        body, name=name,
        out_shape=jax.ShapeDtypeStruct((rows, cols), F32),
        grid=(rows // tr,),
        in_specs=[pl.BlockSpec((n, tr, cols), lambda i: (0, i, 0))],
        out_specs=pl.BlockSpec((tr, cols), lambda i: (i, 0)),
        compiler_params=_params("parallel"),
    )(x)


_DIMS = {"nn": (((1,), (0,)), ((), ())), "nt": (((1,), (1,)), ((), ())), "tn": (((0,), (0,)), ((), ()))}


def _mm(a, b, mode, name, out_dtype=F32, tm=None, tn=None, add=None, after=None):
    if mode == "tn":
        kdim, m = a.shape
    else:
        m, kdim = a.shape
    n = b.shape[0] if mode == "nt" else b.shape[1]
    tm = tm or _tile(m, (512, 256, 128))
    tn = tn or _tile(n, (512, 256, 128))
    dims = _DIMS[mode]

    def body(*refs):
        refs = refs if after is None else refs[1:]
        acc = lax.dot_general(refs[0][...].astype(BF16), refs[1][...].astype(BF16), dims,
                              preferred_element_type=F32)
        if add is not None:
            acc = acc + refs[2][...]
        refs[-1][...] = acc.astype(out_dtype)

    a_spec = (pl.BlockSpec((kdim, tm), lambda i, j: (0, i)) if mode == "tn"
              else pl.BlockSpec((tm, kdim), lambda i, j: (i, 0)))
    b_spec = (pl.BlockSpec((tn, kdim), lambda i, j: (j, 0)) if mode == "nt"
              else pl.BlockSpec((kdim, tn), lambda i, j: (0, j)))
    o_spec = pl.BlockSpec((tm, tn), lambda i, j: (i, j))
    in_specs, args = [a_spec, b_spec], [a, b]
    if add is not None:
        in_specs.append(o_spec)
        args.append(add)
    if after is not None:
        in_specs.insert(0, _ANY)
        args.insert(0, after)
    return pl.pallas_call(
        body, name=name, out_shape=jax.ShapeDtypeStruct((m, n), out_dtype), grid=(m // tm, n // tn),
        in_specs=in_specs, out_specs=o_spec,
        compiler_params=_params("parallel", "parallel"),
    )(*args)


def _rowmap(name, fn, seq, rows, bats=(), vecs=(), row_outs=(), bat_outs=(), vec_outs=(), ts=None, mm=None, lhs=None):
    rows = [r if isinstance(r, tuple) else (r, r.shape[1], 0) for r in rows]
    tokens = rows[0][0].shape[0]
    nseq = tokens // seq
    ts = ts or _tile(seq, (256, 128, 64, 32, 16, 8))
    nt = seq // ts
    n_r, n_b, n_v = len(rows), len(bats), len(vecs)
    n_ro, n_bo = len(row_outs), len(bat_outs)

    def accumulate(ref, val, first):
        @pl.when(first)
        def _():
            ref[...] = val.reshape(ref.shape)

        @pl.when(jnp.logical_not(first))
        def _():
            ref[...] += val.reshape(ref.shape)

    def body(*refs):
        n_in = n_r + n_b + n_v + (mm is not None)
        ins, outs = refs[:n_in], refs[n_in:]
        b_vals = [r[0] for r in ins[n_r:n_r + n_b]]
        v_vals = [r[...] for r in ins[n_r + n_b:n_r + n_b + n_v]]
        r_vals = [r[...] for r in ins[:n_r]]
        if mm is not None:
            left = r_vals[0] if lhs is None else lhs(r_vals)
            acc = lax.dot_general(left.astype(BF16), ins[-1][...].astype(BF16), _DIMS[mm[1]],
                                  preferred_element_type=F32)
            r_vals = [acc] + r_vals[1:] if lhs is None else [acc, left] + r_vals
        ro, bo, vo = fn(r_vals, b_vals, v_vals)
        for ref, val in zip(outs[:n_ro], ro):
            ref[...] = val.astype(ref.dtype)
        b, i = pl.program_id(0), pl.program_id(1)
        for ref, val in zip(outs[n_ro:n_ro + n_bo], bo):
            accumulate(ref, val, i == 0)
        for ref, val in zip(outs[n_ro + n_bo:], vo):
            accumulate(ref, val, jnp.logical_and(i == 0, b == 0))

    in_specs = [pl.BlockSpec((ts, w), functools.partial(lambda b, i, cb: (b * nt + i, cb), cb=cb))
                for _, w, cb in rows]
    in_specs += [pl.BlockSpec((1, 1, v.shape[2]), lambda b, i: (b, 0, 0)) for v in bats]
    in_specs += [pl.BlockSpec((1, v.shape[1]), lambda b, i: (0, 0)) for v in vecs]
    extra = []
    if mm is not None:
        in_specs.append(pl.BlockSpec(mm[0].shape, lambda b, i: (0, 0)))
        extra.append(mm[0])
    out_shape = [jax.ShapeDtypeStruct((tokens, f), dt) for f, dt in row_outs]
    out_specs = [pl.BlockSpec((ts, f), lambda b, i: (b * nt + i, 0)) for f, _ in row_outs]
    out_shape += [jax.ShapeDtypeStruct((nseq, 1, f), F32) for f in bat_outs]
    out_specs += [pl.BlockSpec((1, 1, f), lambda b, i: (b, 0, 0)) for f in bat_outs]
    out_shape += [jax.ShapeDtypeStruct((1, f), F32) for f in vec_outs]
    out_specs += [pl.BlockSpec((1, f), lambda b, i: (0, 0)) for f in vec_outs]
    return pl.pallas_call(
        body, name=name, out_shape=tuple(out_shape), grid=(nseq, nt),
        in_specs=in_specs, out_specs=tuple(out_specs),
        compiler_params=_params("arbitrary", "arbitrary"),
    )(*([r[0] for r in rows] + list(bats) + list(vecs) + extra))


def _colsum(v):
    return jnp.sum(v, axis=0, keepdims=True)


def _rstd(x, width=None):
    width = width or x.shape[-1]
    return lax.rsqrt(jnp.sum(x * x, axis=-1, keepdims=True) * (1.0 / width) + NORM_EPS)


def _norm_bwd(dy, x, r, g, width=None):
    width = width or x.shape[-1]
    xhat = x * r
    dxhat = dy * g
    dx = r * (dxhat - xhat * (jnp.sum(dxhat * xhat, axis=-1, keepdims=True) * (1.0 / width)))
    return dx, dy * xhat


def _sigmoid(x):
    return 0.5 * jnp.tanh(0.5 * x) + 0.5


def _norm_mod(xv, g, sh, sc):
    return xv * _rstd(xv) * g * (1.0 + sc) + sh


def _norm_mod_fwd(x, p, seq, name):
    def fn(rows, bats, vecs):
        return [_norm_mod(rows[0], vecs[0], bats[0], bats[1])], [], []
    return _rowmap(name, fn, seq, [x], [p["shift"], p["scale"]], [p["gamma"]], row_outs=[(D_MODEL, BF16)])[0]


def _norm_mod_bwd(dh, x, dres, p, seq, name, prev=None):
    def fn(rows, bats, vecs):
        dhv, xv, dr = rows[:3]
        sc, g = bats[0], vecs[0]
        r = _rstd(xv)
        dxn, dg = _norm_bwd(dhv * (1.0 + sc), xv, r, g)
        dx = dr + dxn
        ro, bo = [dx], [_colsum(dhv), _colsum(dhv * (xv * r * g))]
        if prev is not None:
            ro.append(bats[1] * dx)
            bo.append(_colsum(dx * rows[3].astype(F32)))
        return ro, bo, [_colsum(dg)]
    more = prev is not None
    return _rowmap(name, fn, seq, [dh, x, dres] + ([prev[0]] if more else []),
                   [p["scale"]] + ([prev[1]] if more else []), [p["gamma"]],
                   row_outs=[(D_MODEL, F32)] + ([(D_MODEL, BF16)] if more else []),
                   bat_outs=[D_MODEL] * (3 if more else 2), vec_outs=[D_MODEL])


def _ffn_in_act(h, wt_in, name):
    tokens = h.shape[0]
    tm, tn = _tile(tokens, (2048, 1024, 512)), 256
    nj = D_FF // tn

    def body(h_ref, wg_ref, wu_ref, g_ref, u_ref, a_ref):
        hv = h_ref[...]
        g = lax.dot_general(hv, wg_ref[...], _DIMS["nt"], preferred_element_type=F32)
        u = lax.dot_general(hv, wu_ref[...], _DIMS["nt"], preferred_element_type=F32)
        g_ref[...] = g.astype(BF16)
        u_ref[...] = u.astype(BF16)
        a_ref[...] = (g * _sigmoid(g) * u).astype(BF16)

    o_spec = pl.BlockSpec((tm, tn), lambda i, j: (i, j))
    out = jax.ShapeDtypeStruct((tokens, D_FF), BF16)
    return pl.pallas_call(
        body, name=name, grid=(tokens // tm, nj), out_shape=(out, out, out),
        in_specs=[pl.BlockSpec((tm, D_MODEL), lambda i, j: (i, 0)),
                  pl.BlockSpec((tn, D_MODEL), lambda i, j: (j, 0)),
                  pl.BlockSpec((tn, D_MODEL), lambda i, j: (j + nj, 0))],
        out_specs=(o_spec, o_spec, o_spec),
        compiler_params=_params("parallel", "parallel"),
    )(h, wt_in, wt_in)


def _out_residual(a, w_out, res, gate, nxt, seq, name, lhs=None):
    def fn(rows, bats, vecs):
        acc, rv = rows[0], rows[-1]
        x_new = rv + bats[0] * acc
        made = [] if lhs is None else [rows[1]]
        return [x_new, acc, _norm_mod(x_new, vecs[0], bats[1], bats[2])] + made, [], []
    outs = [(D_MODEL, F32), (D_MODEL, BF16), (D_MODEL, BF16)] + ([] if lhs is None else [(D_MODEL, BF16)])
    return _rowmap(name, fn, seq, (a if lhs is not None else [a]) + [res], [gate, nxt["shift"], nxt["scale"]],
                   [nxt["gamma"]], row_outs=outs, ts=_tile(seq, (512, 256, 128)), mm=(w_out, "nn"), lhs=lhs)


def _out_loss(a, w_out, res, gate, target, seq, name):
    def fn(rows, bats, vecs):
        acc, rv, tv = rows
        err = rv + bats[0] * acc - tv
        dy = err * (1.0 / D_MODEL)
        return [dy, bats[0] * dy], [_colsum(dy * acc)], [_colsum(err * err)]
    return _rowmap(name, fn, seq, [a, res, target], [gate], row_outs=[(D_MODEL, F32), (D_MODEL, BF16)],
                   bat_outs=[D_MODEL], vec_outs=[D_MODEL], ts=_tile(seq, (512, 256, 128)), mm=(w_out, "nn"))


def _ffn_bwd_x(df, dres, saved, p, seq, tag, prev=None, mid=None):
    x, h, g, u, a, w_in, w_out = saved
    tokens = x.shape[0]

    def act_bwd(rows, bats, vecs):
        dav, gv, uv = rows[0], rows[1].astype(F32), rows[2].astype(F32)
        sg = _sigmoid(gv)
        silu = gv * sg
        dg = dav * uv * (sg * (1.0 + gv * (1.0 - sg)))
        return [jnp.concatenate([dg, dav * silu], axis=1)], [], []
    dgu = _rowmap(f"{tag}_bwd_da", act_bwd, seq, [df, g, u], row_outs=[(2 * D_FF, BF16)],
                  ts=_tile(seq, (256, 128)), mm=(w_out, "nt"))[0]
    operands = (a, df, dgu, h)
    after = None if mid is None else mid(operands)
    dh = _mm(dgu, w_in, "nn", f"{tag}_bwd_dh", tm=_tile(tokens, (512, 256)), tn=D_MODEL, after=after)
    return _norm_mod_bwd(dh, x, dres, p, seq, f"{tag}_bwd_norm", prev), operands


def _ffn_bwd_wout(operands, tag):
    a, df, _, _ = operands
    return _mm(a, df, "tn", f"{tag}_bwd_wout", out_dtype=BF16, tm=256, tn=D_MODEL)


def _ffn_bwd_win(operands, tag, after=None):
    _, _, dgu, h = operands
    return _mm(dgu, h, "tn", f"{tag}_bwd_win", out_dtype=BF16, tm=512, tn=D_MODEL, after=after)


def _shift_rows(v, k, forward):
    n = v.shape[0]
    row = lax.broadcasted_iota(jnp.int32, v.shape, 0)
    if forward:
        return jnp.where(row >= k, pltpu.roll(v, k, 0), 0.0)
    return jnp.where(row < n - k, pltpu.roll(v, n - k, 0), 0.0)


def _window_sums(v, forward):
    out, s, k = [], v, 1
    for _ in range(POOL_GROUPS):
        s = s + _shift_rows(s, k, forward)
        out.append(s)
        k *= 2
    return out


def _by_group(vals, g):
    out = vals[-1]
    for idx in range(len(vals) - 2, -1, -1):
        out = jnp.where(g == idx, vals[idx], out)
    return out


def _inv_count(shape, g):
    t1 = lax.broadcasted_iota(jnp.int32, shape, 0) + 1
    window = _by_group([jnp.int32(2 ** (i + 1)) for i in range(POOL_GROUPS)], g)
    return 1.0 / jnp.minimum(t1, window).astype(F32)


def _pool_fwd(u, grp, scale, seq):
    tokens = u.shape[0]

    def body(u_ref, grp_ref, sc_ref, pooled_ref, pg_ref, ps_ref):
        g = pl.program_id(1)
        uv = u_ref[...]
        sums = _by_group(_window_sums(uv, True), g)
        pooled = (sums * _inv_count(uv.shape, g) - uv).astype(BF16)
        pg = jnp.dot(pooled, grp_ref[0].astype(BF16), preferred_element_type=F32)
        pooled_ref[...] = pooled
        pg_ref[...] = pg
        ps_ref[...] = (pg * sc_ref[...]).astype(BF16)

    blk = pl.BlockSpec((seq, POOL_GROUP_DIM), lambda b, g: (b, g))
    return pl.pallas_call(
        body, name="pool_fwd", grid=(tokens // seq, POOL_GROUPS),
        out_shape=(jax.ShapeDtypeStruct(u.shape, BF16), jax.ShapeDtypeStruct(u.shape, F32),
                   jax.ShapeDtypeStruct(u.shape, BF16)),
        in_specs=[blk, pl.BlockSpec((1, POOL_GROUP_DIM, POOL_GROUP_DIM), lambda b, g: (g, 0, 0)),
                  pl.BlockSpec((1, POOL_GROUP_DIM), lambda b, g: (0, g))],
        out_specs=(blk, blk, blk),
        compiler_params=_params("parallel", "parallel"),
    )(u, grp, scale)


def _pool_bwd(dps, pooled, pg, grp, scale, seq):
    tokens = dps.shape[0]

    def body(dps_ref, pooled_ref, pg_ref, grp_ref, sc_ref, du_ref, dgrp_ref, dsc_ref):
        g, b = pl.program_id(0), pl.program_id(1)
        dpsv = dps_ref[...]
        dpg = (dpsv * sc_ref[...]).astype(BF16)
        dsc = _colsum(dpsv * pg_ref[...])
        dgrp = lax.dot_general(pooled_ref[...], dpg, _DIMS["tn"], preferred_element_type=F32)

        @pl.when(b == 0)
        def _():
            dsc_ref[...] = dsc
            dgrp_ref[0] = dgrp

        @pl.when(b > 0)
        def _():
            dsc_ref[...] += dsc
            dgrp_ref[0] += dgrp

        dpool = lax.dot_general(dpg, grp_ref[0].astype(BF16), _DIMS["nt"], preferred_element_type=F32)
        sums = _by_group(_window_sums(dpool * _inv_count(dpool.shape, g), False), g)
        du_ref[...] = (sums - dpool).astype(BF16)

    blk = pl.BlockSpec((seq, POOL_GROUP_DIM), lambda g, b: (b, g))
    grp_spec = pl.BlockSpec((1, POOL_GROUP_DIM, POOL_GROUP_DIM), lambda g, b: (g, 0, 0))
    vec_spec = pl.BlockSpec((1, POOL_GROUP_DIM), lambda g, b: (0, g))
    return pl.pallas_call(
        body, name="pool_bwd", grid=(POOL_GROUPS, tokens // seq),
        out_shape=(jax.ShapeDtypeStruct(dps.shape, BF16), jax.ShapeDtypeStruct(grp.shape, F32),
                   jax.ShapeDtypeStruct(scale.shape, F32)),
        in_specs=[blk, blk, blk, grp_spec, vec_spec],
        out_specs=(blk, grp_spec, vec_spec),
        compiler_params=_params("arbitrary", "arbitrary"),
    )(dps, pooled, pg, grp, scale)


def _lane(shape):
    return lax.broadcasted_iota(jnp.int32, shape, len(shape) - 1)


def _rot(y):
    lane = _lane(y.shape)
    r = jnp.where(lane < QK_NOPE + QK_ROPE // 2,
                  -pltpu.roll(y, HEAD_SLAB - QK_ROPE // 2, 1), pltpu.roll(y, QK_ROPE // 2, 1))
    return jnp.where(jnp.logical_and(lane >= QK_NOPE, lane < QK_NOPE + QK_ROPE), r, 0.0)


def _part_rstd(x):
    sq = x * x
    nope = _lane(x.shape) < QK_NOPE
    s_nope = jnp.sum(jnp.where(nope, sq, 0.0), axis=-1, keepdims=True)
    s_rope = jnp.sum(sq, axis=-1, keepdims=True) - s_nope
    return jnp.where(nope, lax.rsqrt(s_nope * (1.0 / QK_NOPE) + NORM_EPS),
                     lax.rsqrt(s_rope * (1.0 / QK_ROPE) + NORM_EPS))


def _part_norm_bwd(dy, x, r, g):
    nope = _lane(x.shape) < QK_NOPE
    xhat = x * r
    dxhat = dy * g
    prod = dxhat * xhat
    m_nope = jnp.sum(jnp.where(nope, prod, 0.0), axis=-1, keepdims=True)
    m_rope = jnp.sum(prod, axis=-1, keepdims=True) - m_nope
    mean = jnp.where(nope, m_nope * (1.0 / QK_NOPE), m_rope * (1.0 / QK_ROPE))
    return r * (dxhat - xhat * mean), dy * xhat


def _latent_norm_fwd(z_a, g_q, g_kv, seq):
    def fn(rows, bats, vecs):
        q, kv = rows[0][:, :Q_LORA], rows[0][:, Q_LORA:Q_LORA + KV_LORA]
        return [q * _rstd(q) * vecs[0], kv * _rstd(kv) * vecs[1]], [], []
    return _rowmap("latent_norm", fn, seq, [z_a], vecs=[g_q, g_kv],
                   row_outs=[(Q_LORA, BF16), (KV_LORA, BF16)])


def _latent_norm_bwd(dqn, dkvn, dkr, z_a, g_q, g_kv, seq):
    def fn(rows, bats, vecs):
        dq, dkv, dkrv, z = rows
        q, kv = z[:, :Q_LORA], z[:, Q_LORA:Q_LORA + KV_LORA]
        dxq, dgq = _norm_bwd(dq, q, _rstd(q), vecs[0])
        dxkv, dgkv = _norm_bwd(dkv, kv, _rstd(kv), vecs[1])
        return [jnp.concatenate([dxq, dxkv, dkrv], axis=1)], [], [_colsum(dgq), _colsum(dgkv)]
    return _rowmap("latent_norm_bwd", fn, seq, [dqn, dkvn, dkr, z_a], vecs=[g_q, g_kv],
                   row_outs=[(Q_LORA + KV_LORA + HEAD_SLAB, BF16)], vec_outs=[Q_LORA, KV_LORA])


def _qk_prep_fwd(qp, kv, z_a, pos, g_q, g_kn, g_kr, inv_freq, seq):
    def fn(rows, bats, vecs):
        qv, kvv, kr, p = rows
        gq, gkn, gkr, invf = vecs
        ang = p * invf
        cos, sin = jnp.cos(ang), jnp.sin(ang)
        nope = _lane(kr.shape) < QK_NOPE
        krn = kr * _rstd(kr, QK_ROPE) * gkr
        krr = krn * cos + _rot(krn) * sin
        qs, ks, vs = [], [], []
        for h in range(N_HEADS):
            xq = qv[:, h * HEAD_SLAB:(h + 1) * HEAD_SLAB]
            y = xq * _part_rstd(xq) * gq
            qs.append(y * cos + _rot(y) * sin)
            xk = kvv[:, h * HEAD_SLAB:(h + 1) * HEAD_SLAB]
            kn = jnp.where(nope, xk, 0.0)
            ks.append(jnp.where(nope, kn * _rstd(kn, QK_NOPE) * gkn, krr))
            vs.append(jnp.where(nope, 0.0, xk))
        return [jnp.concatenate(v, axis=1) for v in (qs, ks, vs)], [], []
    width = N_HEADS * HEAD_SLAB
    return _rowmap("qk_prep", fn, seq, [qp, kv, (z_a, HEAD_SLAB, 5), pos], vecs=[g_q, g_kn, g_kr, inv_freq],
                   row_outs=[(width, BF16)] * 3, ts=_tile(seq, (128, 64, 32, 16, 8)))


def _qk_prep_bwd(dqc, dkc, dvp, qp, kv, z_a, pos, g_q, g_kn, g_kr, inv_freq, seq):
    def fn(rows, bats, vecs):
        dq, dk, dv, qv, kvv, kr, p = rows
        gq, gkn, gkr, invf = vecs
        ang = p * invf
        cos, sin = jnp.cos(ang), jnp.sin(ang)
        nope = _lane(kr.shape) < QK_NOPE
        dqs, dkvs = [], []
        dgq = jnp.zeros((1, HEAD_SLAB), F32)
        dgkn = jnp.zeros((1, HEAD_SLAB), F32)
        dkrr = jnp.zeros(kr.shape, F32)
        for h in range(N_HEADS):
            sl = slice(h * HEAD_SLAB, (h + 1) * HEAD_SLAB)
            dyr = dq[:, sl]
            dy = dyr * cos - _rot(dyr * sin)
            xq = qv[:, sl]
            dx, dg = _part_norm_bwd(dy, xq, _part_rstd(xq), gq)
            dqs.append(dx)
            dgq = dgq + _colsum(dg)
            dkh = dk[:, sl]
            dkrr = dkrr + jnp.where(nope, 0.0, dkh)
            kn = jnp.where(nope, kvv[:, sl], 0.0)
            dxk, dgk = _norm_bwd(jnp.where(nope, dkh, 0.0), kn, _rstd(kn, QK_NOPE), gkn, QK_NOPE)
            dgkn = dgkn + _colsum(dgk)
            dkvs.append(jnp.where(nope, dxk, dv[:, sl]))
        dkrn = dkrr * cos - _rot(dkrr * sin)
        dkr, dgkr = _norm_bwd(dkrn, kr, _rstd(kr, QK_ROPE), gkr, QK_ROPE)
        return ([jnp.concatenate(dqs, axis=1), jnp.concatenate(dkvs, axis=1), dkr], [],
                [dgq, dgkn, _colsum(dgkr)])
    width = N_HEADS * HEAD_SLAB
    return _rowmap("qk_prep_bwd", fn, seq, [dqc, dkc, dvp, qp, kv, (z_a, HEAD_SLAB, 5), pos],
                   vecs=[g_q, g_kn, g_kr, inv_freq],
                   row_outs=[(width, BF16), (width, BF16), (HEAD_SLAB, F32)],
                   vec_outs=[HEAD_SLAB] * 3, ts=_tile(seq, (128, 64, 32, 16, 8)))


def _scores(q, k_ref, keys, tq):
    s = lax.dot_general(q, k_ref[0:keys, :], _DIMS["nt"], preferred_element_type=F32) * ATTN_SCALE
    row = lax.broadcasted_iota(jnp.int32, (tq, tq), 0)
    col = lax.broadcasted_iota(jnp.int32, (tq, tq), 1)
    diag = jnp.where(col <= row, s[:, keys - tq:], -1e30)
    return diag if keys == tq else jnp.concatenate([s[:, :keys - tq], diag], axis=1)


def _attn_fwd(qc, kc, vp, seq):
    tokens = qc.shape[0]
    tq = _tile(seq, (256, 128))
    nq = seq // tq

    def body(q_ref, k_ref, v_ref, o_ref, lse_ref):
        for i in range(nq):
            rows, keys = slice(i * tq, (i + 1) * tq), (i + 1) * tq
            s = _scores(q_ref[rows, :], k_ref, keys, tq)
            m = jnp.max(s, axis=-1, keepdims=True)
            p = jnp.exp(s - m)
            l = jnp.sum(p, axis=-1, keepdims=True)
            acc = jnp.dot(p.astype(BF16), v_ref[0:keys, :], preferred_element_type=F32)
            o_ref[rows, :] = (acc / l).astype(BF16)
            lse_ref[rows, :] = jnp.broadcast_to(m + jnp.log(l), (tq, HEAD_SLAB))

    spec = pl.BlockSpec((seq, HEAD_SLAB), lambda b, h: (b, h))
    return pl.pallas_call(
        body, name="attn_fwd", grid=(tokens // seq, N_HEADS),
        out_shape=(jax.ShapeDtypeStruct(qc.shape, BF16), jax.ShapeDtypeStruct(qc.shape, F32)),
        in_specs=[spec] * 3, out_specs=(spec, spec),
        compiler_params=_params("parallel", "parallel"),
    )(qc, kc, vp)


def _attn_bwd(qc, kc, vp, o, lse, do, seq):
    tokens = qc.shape[0]
    tq = _tile(seq, (256, 128))
    nq = seq // tq

    def body(q_ref, k_ref, v_ref, o_ref, lse_ref, do_ref, dq_ref, dk_ref, dv_ref):
        dk_ref[...] = jnp.zeros(dk_ref.shape, F32)
        dv_ref[...] = jnp.zeros(dv_ref.shape, F32)
        for i in range(nq):
            rows, keys = slice(i * tq, (i + 1) * tq), (i + 1) * tq
            q, dov = q_ref[rows, :], do_ref[rows, :]
            delta = jnp.sum(dov.astype(F32) * o_ref[rows, :].astype(F32), axis=-1, keepdims=True)
            s = _scores(q, k_ref, keys, tq)
            p = jnp.exp(s - jnp.tile(lse_ref[rows, :], (1, keys // HEAD_SLAB)))
            dp = lax.dot_general(dov, v_ref[0:keys, :], _DIMS["nt"], preferred_element_type=F32)
            ds = (p * (dp - delta) * ATTN_SCALE).astype(BF16)
            dq_ref[rows, :] = jnp.dot(ds, k_ref[0:keys, :], preferred_element_type=F32)
            dk_ref[0:keys, :] += lax.dot_general(ds, q, _DIMS["tn"], preferred_element_type=F32)
            dv_ref[0:keys, :] += lax.dot_general(p.astype(BF16), dov, _DIMS["tn"], preferred_element_type=F32)

    spec = pl.BlockSpec((seq, HEAD_SLAB), lambda b, h: (b, h))
    out = jax.ShapeDtypeStruct(qc.shape, F32)
    return pl.pallas_call(
        body, name="attn_bwd", grid=(tokens // seq, N_HEADS),
        out_shape=(out, out, out), in_specs=[spec] * 6, out_specs=(spec, spec, spec),
        compiler_params=_params("parallel", "parallel"),
    )(qc, kc, vp, o, lse, do)


def _adamw(w, g, m, v, name):
    rows, cols = w.shape
    whole = rows * cols * 4 <= ADAMW_WHOLE_BYTES
    tr = rows if whole else _tile(rows, (256, 128, 64, 32, 16, 8))
    c1 = 1.0 - ADAM_B1 ** ADAM_STEP
    c2 = 1.0 - ADAM_B2 ** ADAM_STEP

    def body(w_ref, g_ref, m_ref, v_ref, d_ref, nm_ref, nv_ref):
        gv = g_ref[...]
        nm = ADAM_B1 * m_ref[...] + (1.0 - ADAM_B1) * gv
        nv = ADAM_B2 * v_ref[...] + (1.0 - ADAM_B2) * (gv * gv)
        d_ref[...] = -ADAM_LR * ((nm / c1) / (jnp.sqrt(nv / c2) + ADAM_EPS) + ADAM_WD * w_ref[...])
        nm_ref[...] = nm
        nv_ref[...] = nv

    spec = pl.BlockSpec((tr, cols), lambda i: (i, 0))
    out = jax.ShapeDtypeStruct(w.shape, F32)
    return pl.pallas_call(
        body, name=name, grid=(rows // tr,), out_shape=(out, out, out),
        in_specs=[spec] * 4, out_specs=(spec, spec, spec),
        compiler_params=_params("parallel"),
    )(w, g, m, v)


def _mod_cols(c_all, w_ada, b_cols):
    def body(c_ref, w_ref, b_ref, act_ref, mod_ref):
        cv = c_ref[...]
        act = cv * _sigmoid(cv)
        act_ref[...] = act
        mod_ref[...] = jnp.dot(act.astype(BF16), w_ref[...].astype(BF16),
                               preferred_element_type=F32) + b_ref[...]

    n = w_ada.shape[1]
    return pl.pallas_call(
        body, name="mod_cols",
        out_shape=(jax.ShapeDtypeStruct(c_all.shape, F32), jax.ShapeDtypeStruct((c_all.shape[0], n), F32)),
        compiler_params=pltpu.CompilerParams(vmem_limit_bytes=VMEM_LIMIT),
    )(c_all, w_ada, b_cols)


def _ada_grads(c_act, dmod_all, dmod_cols):
    def body(c_ref, d_ref, dc_ref, gw_ref, gb_ref):
        gw_ref[...] = lax.dot_general(c_ref[...].astype(BF16), dc_ref[...].astype(BF16), _DIMS["tn"],
                                      preferred_element_type=F32)
        gb_ref[...] = _colsum(d_ref[...])

    return pl.pallas_call(
        body, name="ada_grads",
        out_shape=(jax.ShapeDtypeStruct((c_act.shape[1], dmod_cols.shape[1]), F32),
                   jax.ShapeDtypeStruct((1, dmod_all.shape[1]), F32)),
        compiler_params=pltpu.CompilerParams(vmem_limit_bytes=VMEM_LIMIT),
    )(c_act, dmod_all, dmod_cols)


def _flat_rows(a):
    flat = a.reshape(-1)
    pad = (-flat.shape[0]) % (LANES * SUBLANES)
    if pad:
        flat = jnp.pad(flat, (0, pad))
    return flat.reshape(-1, LANES)


def _gather_start(w, groups, after):
    shards = [[(w[n] if n in ROW_SHARDED else w[n].T).astype(BF16) for n in names] for names in groups]
    return _exchange_start_groups(shards, "gather_weights_start", after=after)


def _gather_wait(handle, names, tag, after):
    landed = _exchange_wait(handle, f"gather_{tag}_wait", after=after)
    return {n: g.reshape(-1, g.shape[2]) for n, g in zip(names, landed)}


def _scatter_start(grads, names, tag, after=None):
    blocks = [grads[n].reshape(N_DEV, -1, grads[n].shape[1]) for n in names]
    return _exchange_start(blocks, f"scatter_{tag}_start", scatter=True, after=after)


def _scatter_wait(handle, names, tag, after):
    landed = _exchange_wait(handle, f"scatter_{tag}_wait", scatter=True, after=after)
    out = {}
    for n, x in zip(names, landed):
        g = _sum_blocks(x, f"sum_{n}")
        out[n] = g if n in ROW_SHARDED else g.T
    return out


def _pack_small(vals):
    return jnp.concatenate([_flat_rows(v.astype(F32)) for v in vals], axis=0)


def _unpack_small(packed, like):
    out, row = [], 0
    for v in like:
        rows = _flat_rows(v).shape[0]
        out.append(packed[row:row + rows].reshape(-1)[:v.size].reshape(v.shape))
        row += rows
    return out


def _lanes128(*parts):
    out = jnp.zeros((HEAD_SLAB,), F32)
    for off, v in parts:
        out = lax.dynamic_update_slice(out, v.reshape(-1).astype(F32), (off,))
    return out.reshape(1, HEAD_SLAB)


def _step(x, c, positions, w, m, v, loss_target):
    nseq, seq, _ = x.shape
    tokens = nseq * seq
    me = _index(_my_pos())
    strip = lambda d: {n: (a[0] if a.ndim > 2 else a) for n, a in d.items()}
    shapes = {n: a.shape for n, a in w.items()}
    w, m, v = strip(w), strip(m), strip(v)

    c_all = _all_gather(c.reshape(-1, LANES), "gather_c").reshape(N_DEV * nseq, D_MODEL)
    n_ada = w["w_ada"].shape[1]
    b_cols = lax.dynamic_slice(w["b_ada"], (0, me * n_ada), (1, n_ada))
    c_act, mod_cols = _mod_cols(c_all, w["w_ada"], b_cols)
    mod_all = _all_gather(mod_cols, "gather_mod")
    mod = lax.dynamic_slice(mod_all, (0, me * nseq, 0), (N_DEV, nseq, n_ada))
    mod = mod.transpose(1, 0, 2).reshape(nseq, 3, 3, 1, D_MODEL)

    (h_f1i, h_f1o, h_mix_in, h_mix, h_f2), tok = _gather_start(
        w, (("w_ffn1_in",), ("w_ffn1_out",), MIXER[:1], MIXER[1:], ("w_ffn2_in", "w_ffn2_out")), after=mod_all)
    started = tok[0:1, 0:1]

    g_q = _lanes128((0, w["q_norm_nope"]), (QK_NOPE, w["q_norm_rope"]))
    g_kn = _lanes128((0, w["k_norm_nope"]))
    g_kr = _lanes128((QK_NOPE, w["k_norm_rope"]))
    freq = ROPE_THETA ** (-jnp.arange(0, QK_ROPE, 2, dtype=F32) / QK_ROPE)
    inv_freq = _lanes128((QK_NOPE, jnp.concatenate([freq, freq])))
    pos = positions.reshape(tokens, 1).astype(F32)

    def sub(k, gamma, coef):
        return dict(gamma=w[gamma], shift=mod[:, k, 0] + started, scale=mod[:, k, 1], gate=coef * mod[:, k, 2])
    p1, pm, p2 = sub(0, "norm_ffn1", 0.5), sub(1, "norm_mix", 1.0), sub(2, "norm_ffn2", 0.5)
    t_big = _tile(tokens, (2048, 1024, 512))
    t_mid = _tile(tokens, (1024, 512))

    x0 = x.reshape(tokens, D_MODEL)
    h1 = _norm_mod_fwd(x0, p1, seq, "ffn1_norm")
    wt_f1i = _gather_wait(h_f1i, ("w_ffn1_in",), "ffn1_in", h1)["w_ffn1_in"]
    g1, u1, a1 = _ffn_in_act(h1, wt_f1i, "ffn1_in")
    w_f1o = _gather_wait(h_f1o, ("w_ffn1_out",), "ffn1_out", a1)["w_ffn1_out"]
    x1, f1, h2 = _out_residual(a1, w_f1o, x0, p1["gate"], pm, seq, "ffn1_out")
    saved1 = (x0, h1, g1, u1, a1, wt_f1i, w_f1o)

    wt_in = _gather_wait(h_mix_in, MIXER[:1], "mix_in", h2)["w_in"]
    zero_rows = lambda rows: jnp.zeros((rows, D_MODEL), BF16)
    wt_p = wt_in[:512]
    wt_a = jnp.concatenate([wt_in[512:1152], zero_rows(QK_NOPE), wt_in[1152:1184], zero_rows(32)], axis=0)
    wt_g = wt_in[1184:]
    z_a = _mm(h2, wt_a, "nt", "mix_in_a", tm=t_big, tn=wt_a.shape[0])
    z_p = _mm(h2, wt_p, "nt", "mix_in_p", tm=t_big, tn=512)
    z_g = _mm(h2, wt_g, "nt", "mix_in_g", tm=t_big, tn=512)

    full = _gather_wait(h_mix, MIXER[1:], "mix", z_g)
    wtq_pad = jnp.pad(full["w_q_up"].reshape(N_HEADS, 96, Q_LORA), ((0, 0), (0, 32), (0, 0))).reshape(-1, Q_LORA)
    wtmla_pad = jnp.pad(full["w_mla_proj"].reshape(D_MODEL, N_HEADS, 64), ((0, 0), (0, 0), (64, 0))).reshape(D_MODEL, -1)
    wt_pool, wt_kv, w_mix_out = full["w_pool_proj"], full["w_kv_up"], full["w_out"]
    pooled, pg, ps = _pool_fwd(z_p, w["pool_grp"], w["pool_scale"], seq)
    br_pool = _mm(ps, wt_pool, "nt", "pool_proj", tm=t_big, tn=D_MODEL)
    qn, kvn = _latent_norm_fwd(z_a, w["q_a_norm"], w["kv_a_norm"], seq)
    qp = _mm(qn, wtq_pad, "nt", "q_up", tm=t_big, tn=D_MODEL)
    kv = _mm(kvn, wt_kv, "nt", "kv_up", tm=t_big, tn=D_MODEL)
    qc, kc, vp = _qk_prep_fwd(qp, kv, z_a, pos, g_q, g_kn, g_kr, inv_freq, seq)
    attn, lse = _attn_fwd(qc, kc, vp, seq)
    br_mla = _mm(attn, wtmla_pad, "nt", "mla_proj", tm=t_mid, tn=D_MODEL)

    def merge(rows):
        zg, bp, bm = rows[:3]
        return (_sigmoid(zg[:, :D_MODEL]) * bp + _sigmoid(zg[:, D_MODEL:]) * bm).astype(BF16)
    x2, o_mix, h3, merged = _out_residual([z_g, br_pool, br_mla], w_mix_out, x1, pm["gate"], p2, seq, "mix_out",
                                          lhs=merge)

    ffn2_w = _gather_wait(h_f2, ("w_ffn2_in", "w_ffn2_out"), "ffn2", h3)
    g2, u2, a2 = _ffn_in_act(h3, ffn2_w["w_ffn2_in"], "ffn2_in")
    dy, df2, dgate2, sq_err = _out_loss(a2, ffn2_w["w_ffn2_out"], x2, p2["gate"],
                                        loss_target.reshape(tokens, D_MODEL), seq, "ffn2_out")
    saved2 = (x2, h3, g2, u2, a2, ffn2_w["w_ffn2_in"], ffn2_w["w_ffn2_out"])

    grads = {}
    (dx2, do_mix, dsh2, dsc2, dgate_m, dg_ffn2), ops2 = _ffn_bwd_x(df2, dy, saved2, p2, seq, "ffn2", (o_mix, pm["gate"]))
    grads["w_ffn2_out"], grads["w_ffn2_in"] = _ffn_bwd_wout(ops2, "ffn2"), _ffn_bwd_win(ops2, "ffn2")
    s_f2, tok = _scatter_start(grads, ("w_ffn2_in", "w_ffn2_out"), "ffn2")

    grads["w_out"] = _mm(merged, do_mix, "tn", "mix_bwd_wout", out_dtype=BF16, tm=512, tn=D_MODEL)

    def merge_bwd(rows, bats, vecs):
        dmv, zg, bp, bm = rows
        s_p, s_m = _sigmoid(zg[:, :D_MODEL]), _sigmoid(zg[:, D_MODEL:])
        dzg = jnp.concatenate([dmv * bp * s_p * (1.0 - s_p), dmv * bm * s_m * (1.0 - s_m)], axis=1)
        return [dmv * s_p, dmv * s_m, dzg], [], []
    dbr_pool, dbr_mla, dz_g = _rowmap("mix_bwd_dmerged", merge_bwd, seq, [do_mix, z_g, br_pool, br_mla],
                                      row_outs=[(D_MODEL, BF16), (D_MODEL, BF16), (2 * D_MODEL, BF16)],
                                      mm=(w_mix_out, "nt"))

    grads["w_pool_proj"] = _mm(dbr_pool, ps, "tn", "pool_bwd_wproj", out_dtype=BF16, tm=512, tn=POOL_WIDTH)
    dps = _mm(dbr_pool, wt_pool, "nn", "pool_bwd_dps", tm=t_big, tn=POOL_WIDTH)
    dz_p, dgrp, dpool_scale = _pool_bwd(dps, pooled, pg, w["pool_grp"], w["pool_scale"] + tok[0:1, 0:1], seq)

    dwtmla_pad = _mm(dbr_mla, attn, "tn", "mla_bwd_wproj", out_dtype=BF16, tm=512, tn=D_MODEL)
    grads["w_mla_proj"] = dwtmla_pad.reshape(D_MODEL, N_HEADS, HEAD_SLAB)[:, :, 64:].reshape(D_MODEL, -1)
    d_attn = _mm(dbr_mla, wtmla_pad, "nn", "mla_bwd_dattn", out_dtype=BF16, tm=t_mid, tn=D_MODEL)
    dqc, dkc, dvp = _attn_bwd(qc, kc, vp, attn, lse, d_attn, seq)
    dqp, dkv, dkr, dg_q, dg_kn, dg_kr = _qk_prep_bwd(dqc, dkc, dvp, qp, kv, z_a, pos, g_q, g_kn, g_kr, inv_freq, seq)
    dwtq_pad = _mm(dqp, qn, "tn", "q_up_bwd_w", out_dtype=BF16, tm=512, tn=Q_LORA)
    grads["w_q_up"] = dwtq_pad.reshape(N_HEADS, HEAD_SLAB, Q_LORA)[:, :96].reshape(-1, Q_LORA)
    grads["w_kv_up"] = _mm(dkv, kvn, "tn", "kv_up_bwd_w", out_dtype=BF16, tm=512, tn=KV_LORA)
    dqn = _mm(dqp, wtq_pad, "nn", "q_up_bwd_x", tm=t_big, tn=Q_LORA)
    dkvn = _mm(dkv, wt_kv, "nn", "kv_up_bwd_x", tm=t_big, tn=KV_LORA)
    dz_a, dg_qa, dg_kva = _latent_norm_bwd(dqn, dkvn, dkr, z_a, w["q_a_norm"], w["kv_a_norm"], seq)

    dwt_a = _mm(dz_a, h2, "tn", "mix_in_bwd_wa", out_dtype=BF16, tm=256, tn=D_MODEL)
    dwt_p = _mm(dz_p, h2, "tn", "mix_in_bwd_wp", out_dtype=BF16, tm=512, tn=D_MODEL)
    dwt_g = _mm(dz_g, h2, "tn", "mix_in_bwd_wg", out_dtype=BF16, tm=512, tn=D_MODEL)
    grads["w_in"] = jnp.concatenate([dwt_p, dwt_a[:640], dwt_a[704:736], dwt_g], axis=0)
    s_mix, tok = _scatter_start(grads, MIXER, "mix")
    dh2 = _mm(dz_a, wt_a, "nn", "mix_in_bwd_xa", tm=t_mid, tn=D_MODEL)
    dh2 = _mm(dz_p, wt_p, "nn", "mix_in_bwd_xp", tm=t_mid, tn=D_MODEL, add=dh2)
    dh2 = _mm(dz_g, wt_g, "nn", "mix_in_bwd_xg", tm=t_mid, tn=D_MODEL, add=dh2)
    pm_tied = dict(pm, scale=pm["scale"] + tok[0:1, 0:1])
    dx1, df1, dsh_m, dsc_m, dgate1, dg_mix = _norm_mod_bwd(dh2, x1, dx2, pm_tied, seq, "mix_bwd_norm", (f1, p1["gate"]))

    small_early = [dg_mix.reshape(w["norm_mix"].shape), dg_ffn2.reshape(w["norm_ffn2"].shape), dgrp, dpool_scale,
                   dg_qa, dg_kva, dg_q[:, :QK_NOPE], dg_q[:, QK_NOPE:QK_NOPE + QK_ROPE], dg_kn[:, :QK_NOPE],
                   dg_kr[:, QK_NOPE:QK_NOPE + QK_ROPE]]
    s_small, tok = _exchange_start([_pack_small(small_early)], "gather_small_start", after=tok)

    handles = {}

    def ffn1_mid(operands):
        grads["w_ffn1_out"] = _ffn_bwd_wout(operands, "ffn1")
        handles["f1o"], token = _scatter_start(grads, ("w_ffn1_out",), "ffn1_out")
        return token

    p1_tied = dict(p1, scale=p1["scale"] + tok[0:1, 0:1])
    (dx0, dsh1, dsc1, dg_ffn1), ops1 = _ffn_bwd_x(df1, dx1, saved1, p1_tied, seq, "ffn1", mid=ffn1_mid)
    s_f1o = handles["f1o"]

    dmod = jnp.stack([jnp.stack([dsh1, dsc1, 0.5 * dgate1], axis=1),
                      jnp.stack([dsh_m, dsc_m, dgate_m], axis=1),
                      jnp.stack([dsh2, dsc2, 0.5 * dgate2], axis=1)], axis=1)
    n_dmod = nseq * 9 * D_MODEL // LANES
    tail = _all_gather(jnp.concatenate([dmod.reshape(-1, LANES), _flat_rows(dg_ffn1), _flat_rows(sq_err)], axis=0),
                       "gather_dmod")
    dmod_all = tail[:, :n_dmod].reshape(N_DEV * nseq, 9 * D_MODEL)

    grads["w_ffn1_in"] = _ffn_bwd_win(ops1, "ffn1", after=tail)
    s_f1i, tok = _scatter_start(grads, ("w_ffn1_in",), "ffn1_in", after=tail)

    dmod_cols = lax.dynamic_slice(dmod_all, (0, me * n_ada), (N_DEV * nseq, n_ada)) + tok[0:1, 0:1]
    g_w_ada, g_b_ada = _ada_grads(c_act, dmod_all, dmod_cols)
    tail_sum = _sum_blocks(tail[:, n_dmod:], "sum_tail")
    g_norm_ffn1 = tail_sum[:SUBLANES].reshape(1, D_MODEL)
    loss = 0.5 * jnp.sum(tail_sum[SUBLANES:]) * (1.0 / D_MODEL)
    small_all = _exchange_wait(s_small, "gather_small_wait", after=g_b_ada)[0]
    small_sum = _sum_blocks(small_all, "sum_small")
    small = dict(zip(SMALL[1:], _unpack_small(small_sum, [w[n] for n in SMALL[1:]])))
    grad_w = dict(small, w_ada=g_w_ada, b_ada=g_b_ada, norm_ffn1=g_norm_ffn1)

    delta, new_m, new_v = {}, {}, {}

    def update(names):
        for n in names:
            delta[n], new_m[n], new_v[n] = _adamw(w[n], grad_w[n], m[n], v[n], f"adamw_{n}")

    update(("w_ada",))
    rep = ("b_ada",) + SMALL
    d_s, m_s, v_s = _adamw(_pack_small([w[n] for n in rep]), _pack_small([grad_w[n] for n in rep]),
                           _pack_small([m[n] for n in rep]), _pack_small([v[n] for n in rep]), "adamw_small")
    like = [w[n] for n in rep]
    for dst, packed in ((delta, d_s), (new_m, m_s), (new_v, v_s)):
        dst.update(zip(rep, _unpack_small(packed, like)))
    grad_w.update(_scatter_wait(s_f2, ("w_ffn2_in", "w_ffn2_out"), "ffn2", after=d_s))
    update(("w_ffn2_in", "w_ffn2_out"))
    grad_w.update(_scatter_wait(s_mix, MIXER, "mix", after=delta["w_ffn2_out"]))
    update(MIXER)
    grad_w.update(_scatter_wait(s_f1o, ("w_ffn1_out",), "ffn1_out", after=delta["w_out"]))
    update(("w_ffn1_out",))
    grad_w.update(_scatter_wait(s_f1i, ("w_ffn1_in",), "ffn1_in", after=delta["w_ffn1_out"]))
    update(("w_ffn1_in",))

    lead = lambda d: [d[n].reshape(shapes[n]) for n in WEIGHTS]
    return (loss, dx0.reshape(x.shape), *lead(grad_w), *lead(delta), *lead(new_m), *lead(new_v))


def kernel(x, c, positions, w_ada, b_ada, norm_ffn1, w_ffn1_in, w_ffn1_out, norm_mix, w_in, pool_grp, pool_scale, w_pool_proj, q_a_norm, w_q_up, kv_a_norm, w_kv_up, q_norm_nope, q_norm_rope, k_norm_nope, k_norm_rope, w_mla_proj, w_out, norm_ffn2, w_ffn2_in, w_ffn2_out, loss_target, m_w_ada, m_b_ada, m_norm_ffn1, m_w_ffn1_in, m_w_ffn1_out, m_norm_mix, m_w_in, m_pool_grp, m_pool_scale, m_w_pool_proj, m_q_a_norm, m_w_q_up, m_kv_a_norm, m_w_kv_up, m_q_norm_nope, m_q_norm_rope, m_k_norm_nope, m_k_norm_rope, m_w_mla_proj, m_w_out, m_norm_ffn2, m_w_ffn2_in, m_w_ffn2_out, v_w_ada, v_b_ada, v_norm_ffn1, v_w_ffn1_in, v_w_ffn1_out, v_norm_mix, v_w_in, v_pool_grp, v_pool_scale, v_w_pool_proj, v_q_a_norm, v_w_q_up, v_kv_a_norm, v_w_kv_up, v_q_norm_nope, v_q_norm_rope, v_k_norm_nope, v_k_norm_rope, v_w_mla_proj, v_w_out, v_norm_ffn2, v_w_ffn2_in, v_w_ffn2_out):
    w = dict(w_ada=w_ada, b_ada=b_ada, norm_ffn1=norm_ffn1, w_ffn1_in=w_ffn1_in, w_ffn1_out=w_ffn1_out, norm_mix=norm_mix, w_in=w_in, pool_grp=pool_grp, pool_scale=pool_scale, w_pool_proj=w_pool_proj, q_a_norm=q_a_norm, w_q_up=w_q_up, kv_a_norm=kv_a_norm, w_kv_up=w_kv_up, q_norm_nope=q_norm_nope, q_norm_rope=q_norm_rope, k_norm_nope=k_norm_nope, k_norm_rope=k_norm_rope, w_mla_proj=w_mla_proj, w_out=w_out, norm_ffn2=norm_ffn2, w_ffn2_in=w_ffn2_in, w_ffn2_out=w_ffn2_out)
    m = dict(w_ada=m_w_ada, b_ada=m_b_ada, norm_ffn1=m_norm_ffn1, w_ffn1_in=m_w_ffn1_in, w_ffn1_out=m_w_ffn1_out, norm_mix=m_norm_mix, w_in=m_w_in, pool_grp=m_pool_grp, pool_scale=m_pool_scale, w_pool_proj=m_w_pool_proj, q_a_norm=m_q_a_norm, w_q_up=m_w_q_up, kv_a_norm=m_kv_a_norm, w_kv_up=m_w_kv_up, q_norm_nope=m_q_norm_nope, q_norm_rope=m_q_norm_rope, k_norm_nope=m_k_norm_nope, k_norm_rope=m_k_norm_rope, w_mla_proj=m_w_mla_proj, w_out=m_w_out, norm_ffn2=m_norm_ffn2, w_ffn2_in=m_w_ffn2_in, w_ffn2_out=m_w_ffn2_out)
    v = dict(w_ada=v_w_ada, b_ada=v_b_ada, norm_ffn1=v_norm_ffn1, w_ffn1_in=v_w_ffn1_in, w_ffn1_out=v_w_ffn1_out, norm_mix=v_norm_mix, w_in=v_w_in, pool_grp=v_pool_grp, pool_scale=v_pool_scale, w_pool_proj=v_w_pool_proj, q_a_norm=v_q_a_norm, w_q_up=v_w_q_up, kv_a_norm=v_kv_a_norm, w_kv_up=v_w_kv_up, q_norm_nope=v_q_norm_nope, q_norm_rope=v_q_norm_rope, k_norm_nope=v_k_norm_nope, k_norm_rope=v_k_norm_rope, w_mla_proj=v_w_mla_proj, w_out=v_w_out, norm_ffn2=v_norm_ffn2, w_ffn2_in=v_w_ffn2_in, w_ffn2_out=v_w_ffn2_out)
    return _step(x, c, positions, w, m, v, loss_target)
```

```python
import functools
import math

import jax
import jax.numpy as jnp
from jax import lax
from jax.experimental import pallas as pl
from jax.experimental.pallas import tpu as pltpu

F32 = jnp.float32
BF16 = jnp.bfloat16
MESH = pl.DeviceIdType.MESH
AXES = ("x", "y", "c")
N_DEV = 8

D_MODEL = 1024
D_FF = 2816
N_HEADS = 8
HEAD_SLAB = 128
QK_NOPE = 64
QK_ROPE = 32
POOL_WIDTH = 512
POOL_GROUPS = 4
POOL_GROUP_DIM = 128
Q_LORA = 384
KV_LORA = 256
ROPE_THETA = 10000.0
ATTN_SCALE = 1.0 / math.sqrt(QK_NOPE + QK_ROPE)
NORM_EPS = 1e-6
ADAM_LR, ADAM_B1, ADAM_B2, ADAM_EPS, ADAM_WD, ADAM_STEP = 0.001, 0.9, 0.999, 1e-08, 0.01, 10

LANES = 128
SUBLANES = 8
VMEM_LIMIT = 52 * 1024 * 1024
ADAMW_WHOLE_BYTES = 3 << 19
SUM_WHOLE_BYTES = 4 << 20

BIG = ("w_ffn1_in", "w_ffn1_out", "w_in", "w_pool_proj", "w_q_up", "w_kv_up",
       "w_mla_proj", "w_out", "w_ffn2_in", "w_ffn2_out")
ROW_SHARDED = ("w_ffn1_out", "w_out", "w_ffn2_out")
MIXER = ("w_in", "w_pool_proj", "w_q_up", "w_kv_up", "w_mla_proj", "w_out")
KEPT_TRANSPOSED = ("w_ffn1_in", "w_ffn2_in", "w_q_up")
SMALL = ("norm_ffn1", "norm_mix", "norm_ffn2", "pool_grp", "pool_scale", "q_a_norm",
         "kv_a_norm", "q_norm_nope", "q_norm_rope", "k_norm_nope", "k_norm_rope")
WEIGHTS = ("w_ada", "b_ada", "norm_ffn1", "w_ffn1_in", "w_ffn1_out", "norm_mix", "w_in",
           "pool_grp", "pool_scale", "w_pool_proj", "q_a_norm", "w_q_up", "kv_a_norm",
           "w_kv_up", "q_norm_nope", "q_norm_rope", "k_norm_nope", "k_norm_rope",
           "w_mla_proj", "w_out", "norm_ffn2", "w_ffn2_in", "w_ffn2_out")


def _params(*sem):
    return pltpu.CompilerParams(dimension_semantics=sem, vmem_limit_bytes=VMEM_LIMIT)


def _tile(n, cands):
    for c in cands:
        if n % c == 0:
            return c
    return n


def _my_pos():
    return lax.axis_index("x"), lax.axis_index("y"), lax.axis_index("c")


def _flip(pos, k):
    x, y, c = pos
    fx, fy, fc = (k >> 2) & 1, (k >> 1) & 1, k & 1
    return ((1 - x) if fx else x, (1 - y) if fy else y, (1 - c) if fc else c)


def _index(pos):
    x, y, c = pos
    return 4 * x + 2 * y + c


def _exchange(arrays, name, scatter=False):
    n = len(arrays)

    def body(*refs):
        ins, outs = refs[:n], refs[n:2 * n]
        send_sems, recv_sems, local_sems = refs[2 * n:]
        me = _my_pos()
        mine, sends = [], []
        for a in range(n):
            own = ins[a].at[_index(me)] if scatter else ins[a]
            cp = pltpu.make_async_copy(own, outs[a].at[_index(me)], local_sems.at[a])
            cp.start()
            mine.append(cp)
        for k in range(1, N_DEV):
            peer = _flip(me, k)
            for a in range(n):
                cp = pltpu.make_async_remote_copy(
                    src_ref=ins[a].at[_index(peer)] if scatter else ins[a],
                    dst_ref=outs[a].at[_index(me)],
                    send_sem=send_sems.at[a, k - 1], recv_sem=recv_sems.at[a, k - 1],
                    device_id=peer, device_id_type=MESH)
                cp.start()
                sends.append(cp)
        for k in range(1, N_DEV):
            peer = _flip(me, k)
            for a in range(n):
                pltpu.make_async_remote_copy(
                    src_ref=ins[a].at[_index(me)] if scatter else ins[a],
                    dst_ref=outs[a].at[_index(peer)],
                    send_sem=send_sems.at[a, k - 1], recv_sem=recv_sems.at[a, k - 1],
                    device_id=peer, device_id_type=MESH).wait_recv()
        for cp in sends:
            cp.wait_send()
        for cp in mine:
            cp.wait()

    shape = lambda x: x.shape if scatter else (N_DEV,) + x.shape
    return pl.pallas_call(
        body, name=name,
        out_shape=tuple(jax.ShapeDtypeStruct(shape(x), x.dtype) for x in arrays),
        in_specs=[pl.BlockSpec(memory_space=pl.ANY)] * n,
        out_specs=tuple(pl.BlockSpec(memory_space=pl.ANY) for _ in arrays),
        scratch_shapes=[pltpu.SemaphoreType.DMA((n, N_DEV - 1)),
                        pltpu.SemaphoreType.DMA((n, N_DEV - 1)),
                        pltpu.SemaphoreType.DMA((n,))],
    )(*arrays)


def _all_gather(x, name):
    return _exchange([x], name)[0]


_HBM = pl.BlockSpec(memory_space=pltpu.HBM)
_SEM = pl.BlockSpec(memory_space=pltpu.SEMAPHORE)
_ANY = pl.BlockSpec(memory_space=pl.ANY)
_EFFECT = pltpu.SideEffectType.DATAFLOW_SIDE_EFFECTING


def _split_copy(ins, lands, send_sems, recv_sems, a, k, me, scatter, incoming):
    peer = _flip(me, k)
    block = me if incoming else peer
    return pltpu.make_async_remote_copy(
        src_ref=ins[a].at[_index(block)] if scatter else ins[a],
        dst_ref=lands[a].at[_index(peer if incoming else me)],
        send_sem=send_sems.at[a * (N_DEV - 1) + k - 1], recv_sem=recv_sems.at[a * (N_DEV - 1) + k - 1],
        device_id=peer, device_id_type=MESH)


def _exchange_start_groups(groups, name, scatter=False, after=None):
    sizes = [len(g) for g in groups]
    first = [sum(sizes[:i]) for i in range(len(sizes))]
    n, ng = sum(sizes), len(sizes)
    after = jnp.zeros((SUBLANES, LANES), F32) if after is None else after

    def body(*refs):
        ins, lands = refs[:n], refs[n:2 * n]
        sems = refs[2 * n + 1:2 * n + 1 + 2 * ng]
        me = _my_pos()
        for g in range(ng):
            part = slice(first[g], first[g] + sizes[g])
            for k in range(1, N_DEV):
                for a in range(sizes[g]):
                    _split_copy(ins[part], lands[part], sems[2 * g], sems[2 * g + 1], a, k, me, scatter, False).start()
        refs[-1][...] = jnp.zeros((SUBLANES, LANES), F32)

    shape = lambda x: x.shape if scatter else (N_DEV,) + x.shape
    hbm = lambda x: pltpu.with_memory_space_constraint(x, pltpu.HBM)
    srcs = [hbm(x) for g in groups for x in g]
    zones = [hbm(lax.empty(shape(x), x.dtype)) for g in groups for x in g]
    sem_shapes = [pltpu.SemaphoreType.DMA((s * (N_DEV - 1),)) for s in sizes for _ in range(2)]
    out = pl.pallas_call(
        body, name=name,
        out_shape=(*sem_shapes, *[pltpu.HBM(x.shape, x.dtype) for x in srcs + zones],
                   jax.ShapeDtypeStruct((SUBLANES, LANES), F32)),
        in_specs=[_HBM] * (2 * n) + [_ANY],
        out_specs=(*[_SEM] * (2 * ng), *[_HBM] * (2 * n), pl.BlockSpec(memory_space=pltpu.VMEM)),
        input_output_aliases={i: 2 * ng + i for i in range(2 * n)},
        compiler_params=pltpu.CompilerParams(has_side_effects=_EFFECT),
    )(*srcs, *zones, after)
    bufs = out[2 * ng:-1]
    handles = [(out[2 * g], out[2 * g + 1], *bufs[first[g]:first[g] + sizes[g]],
                *bufs[n + first[g]:n + first[g] + sizes[g]]) for g in range(ng)]
    return handles, out[-1]


def _exchange_start(arrays, name, scatter=False, after=None):
    handles, token = _exchange_start_groups([arrays], name, scatter, after)
    return handles[0], token


def _exchange_wait(handle, name, scatter=False, after=None):
    send_sems, recv_sems = handle[0], handle[1]
    n = (len(handle) - 2) // 2
    after = jnp.zeros((SUBLANES, LANES), F32) if after is None else after

    def body(*refs):
        ins, lands = refs[:n], refs[n:2 * n]
        send, recv = refs[2 * n], refs[2 * n + 1]
        me = _my_pos()
        for k in range(1, N_DEV):
            for a in range(n):
                _split_copy(ins, lands, send, recv, a, k, me, scatter, False).wait_send()
                _split_copy(ins, lands, send, recv, a, k, me, scatter, True).wait_recv()

    bufs = handle[2:]
    out = pl.pallas_call(
        body, name=name,
        out_shape=tuple(pltpu.HBM(x.shape, x.dtype) for x in bufs),
        in_specs=[_HBM] * (2 * n) + [_SEM, _SEM, _ANY],
        out_specs=tuple([_HBM] * (2 * n)),
        input_output_aliases={i: i for i in range(2 * n)},
        compiler_params=pltpu.CompilerParams(has_side_effects=_EFFECT),
    )(*bufs, send_sems, recv_sems, after)
    me = _index(_my_pos())
    landed = []
    for src, land in zip(out[:n], out[n:]):
        own = lax.dynamic_slice_in_dim(src, me, 1, axis=0) if scatter else src[None]
        landed.append(lax.dynamic_update_slice_in_dim(land, own, me, axis=0))
    return landed


def _sum_blocks(x, name):
    n, rows, cols = x.shape
    whole = x.size * x.dtype.itemsize <= SUM_WHOLE_BYTES
    tr = rows if whole else _tile(rows, (512, 256, 128, 64, 32, 16, 8))

    def body(x_ref, o_ref):
        acc = x_ref[0].astype(F32)
        for d in range(1, n):
            acc = acc + x_ref[d].astype(F32)
        o_ref[...] = acc

    return pl.pallas_call(
        body, name=name,
        out_shape=jax.ShapeDtypeStruct((rows, cols), F32),
        grid=(rows // tr,),
        in_specs=[pl.BlockSpec((n, tr, cols), lambda i: (0, i, 0))],
        out_specs=pl.BlockSpec((tr, cols), lambda i: (i, 0)),
        compiler_params=_params("parallel"),
    )(x)


_DIMS = {"nn": (((1,), (0,)), ((), ())), "nt": (((1,), (1,)), ((), ())), "tn": (((0,), (0,)), ((), ()))}


def _mm(a, b, mode, name, out_dtype=F32, tm=None, tn=None, add=None, after=None):
    if mode == "tn":
        kdim, m = a.shape
    else:
        m, kdim = a.shape
    n = b.shape[0] if mode == "nt" else b.shape[1]
    tm = tm or _tile(m, (512, 256, 128))
    tn = tn or _tile(n, (512, 256, 128))
    dims = _DIMS[mode]

    def body(*refs):
        refs = refs if after is None else refs[1:]
        acc = lax.dot_general(refs[0][...].astype(BF16), refs[1][...].astype(BF16), dims,
                              preferred_element_type=F32)
        if add is not None:
            acc = acc + refs[2][...]
        refs[-1][...] = acc.astype(out_dtype)

    a_spec = (pl.BlockSpec((kdim, tm), lambda i, j: (0, i)) if mode == "tn"
              else pl.BlockSpec((tm, kdim), lambda i, j: (i, 0)))
    b_spec = (pl.BlockSpec((tn, kdim), lambda i, j: (j, 0)) if mode == "nt"
              else pl.BlockSpec((kdim, tn), lambda i, j: (0, j)))
    o_spec = pl.BlockSpec((tm, tn), lambda i, j: (i, j))
    in_specs, args = [a_spec, b_spec], [a, b]
    if add is not None:
        in_specs.append(o_spec)
        args.append(add)
    if after is not None:
        in_specs.insert(0, _ANY)
        args.insert(0, after)
    return pl.pallas_call(
        body, name=name, out_shape=jax.ShapeDtypeStruct((m, n), out_dtype), grid=(m // tm, n // tn),
        in_specs=in_specs, out_specs=o_spec,
        compiler_params=_params("parallel", "parallel"),
    )(*args)


def _rowmap(name, fn, seq, rows, bats=(), vecs=(), row_outs=(), bat_outs=(), vec_outs=(), ts=None, mm=None, lhs=None):
    rows = [r if isinstance(r, tuple) else (r, r.shape[1], 0) for r in rows]
    tokens = rows[0][0].shape[0]
    nseq = tokens // seq
    ts = ts or _tile(seq, (256, 128, 64, 32, 16, 8))
    nt = seq // ts
    n_r, n_b, n_v = len(rows), len(bats), len(vecs)
    n_ro, n_bo = len(row_outs), len(bat_outs)

    def accumulate(ref, val, first):
        @pl.when(first)
        def _():
            ref[...] = val.reshape(ref.shape)

        @pl.when(jnp.logical_not(first))
        def _():
            ref[...] += val.reshape(ref.shape)

    def body(*refs):
        n_in = n_r + n_b + n_v + (mm is not None)
        ins, outs = refs[:n_in], refs[n_in:]
        b_vals = [r[0] for r in ins[n_r:n_r + n_b]]
        v_vals = [r[...] for r in ins[n_r + n_b:n_r + n_b + n_v]]
        r_vals = [r[...] for r in ins[:n_r]]
        if mm is not None:
            left = r_vals[0] if lhs is None else lhs(r_vals)
            acc = lax.dot_general(left.astype(BF16), ins[-1][...].astype(BF16), _DIMS[mm[1]],
                                  preferred_element_type=F32)
            r_vals = [acc] + r_vals[1:] if lhs is None else [acc, left] + r_vals
        ro, bo, vo = fn(r_vals, b_vals, v_vals)
        for ref, val in zip(outs[:n_ro], ro):
            ref[...] = val.astype(ref.dtype)
        b, i = pl.program_id(0), pl.program_id(1)
        for ref, val in zip(outs[n_ro:n_ro + n_bo], bo):
            accumulate(ref, val, i == 0)
        for ref, val in zip(outs[n_ro + n_bo:], vo):
            accumulate(ref, val, jnp.logical_and(i == 0, b == 0))

    in_specs = [pl.BlockSpec((ts, w), functools.partial(lambda b, i, cb: (b * nt + i, cb), cb=cb))
                for _, w, cb in rows]
    in_specs += [pl.BlockSpec((1, 1, v.shape[2]), lambda b, i: (b, 0, 0)) for v in bats]
    in_specs += [pl.BlockSpec((1, v.shape[1]), lambda b, i: (0, 0)) for v in vecs]
    extra = []
    if mm is not None:
        in_specs.append(pl.BlockSpec(mm[0].shape, lambda b, i: (0, 0)))
        extra.append(mm[0])
    out_shape = [jax.ShapeDtypeStruct((tokens, f), dt) for f, dt in row_outs]
    out_specs = [pl.BlockSpec((ts, f), lambda b, i: (b * nt + i, 0)) for f, _ in row_outs]
    out_shape += [jax.ShapeDtypeStruct((nseq, 1, f), F32) for f in bat_outs]
    out_specs += [pl.BlockSpec((1, 1, f), lambda b, i: (b, 0, 0)) for f in bat_outs]
    out_shape += [jax.ShapeDtypeStruct((1, f), F32) for f in vec_outs]
    out_specs += [pl.BlockSpec((1, f), lambda b, i: (0, 0)) for f in vec_outs]
    return pl.pallas_call(
        body, name=name, out_shape=tuple(out_shape), grid=(nseq, nt),
        in_specs=in_specs, out_specs=tuple(out_specs),
        compiler_params=_params("arbitrary", "arbitrary"),
    )(*([r[0] for r in rows] + list(bats) + list(vecs) + extra))


def _colsum(v):
    return jnp.sum(v, axis=0, keepdims=True)


def _rstd(x, width=None):
    width = width or x.shape[-1]
    return lax.rsqrt(jnp.sum(x * x, axis=-1, keepdims=True) * (1.0 / width) + NORM_EPS)


def _norm_bwd(dy, x, r, g, width=None):
    width = width or x.shape[-1]
    xhat = x * r
    dxhat = dy * g
    dx = r * (dxhat - xhat * (jnp.sum(dxhat * xhat, axis=-1, keepdims=True) * (1.0 / width)))
    return dx, dy * xhat


def _sigmoid(x):
    return 0.5 * jnp.tanh(0.5 * x) + 0.5


def _norm_mod(xv, g, sh, sc):
    return xv * _rstd(xv) * g * (1.0 + sc) + sh


def _norm_mod_fwd(x, p, seq, name):
    def fn(rows, bats, vecs):
        return [_norm_mod(rows[0], vecs[0], bats[0], bats[1])], [], []
    return _rowmap(name, fn, seq, [x], [p["shift"], p["scale"]], [p["gamma"]], row_outs=[(D_MODEL, BF16)])[0]


def _norm_mod_bwd(dh, x, dres, p, seq, name, prev=None):
    def fn(rows, bats, vecs):
        dhv, xv, dr = rows[:3]
        sc, g = bats[0], vecs[0]
        r = _rstd(xv)
        dxn, dg = _norm_bwd(dhv * (1.0 + sc), xv, r, g)
        dx = dr + dxn
        ro, bo = [dx], [_colsum(dhv), _colsum(dhv * (xv * r * g))]
        if prev is not None:
            ro.append(bats[1] * dx)
            bo.append(_colsum(dx * rows[3].astype(F32)))
        return ro, bo, [_colsum(dg)]
    more = prev is not None
    return _rowmap(name, fn, seq, [dh, x, dres] + ([prev[0]] if more else []),
                   [p["scale"]] + ([prev[1]] if more else []), [p["gamma"]],
                   row_outs=[(D_MODEL, F32)] + ([(D_MODEL, BF16)] if more else []),
                   bat_outs=[D_MODEL] * (3 if more else 2), vec_outs=[D_MODEL])


def _ffn_in_act(h, wt_in, name):
    tokens = h.shape[0]
    tm, tn = _tile(tokens, (2048, 1024, 512)), 256
    nj = D_FF // tn

    def body(h_ref, wg_ref, wu_ref, g_ref, u_ref, a_ref):
        hv = h_ref[...]
        g = lax.dot_general(hv, wg_ref[...], _DIMS["nt"], preferred_element_type=F32)
        u = lax.dot_general(hv, wu_ref[...], _DIMS["nt"], preferred_element_type=F32)
        g_ref[...] = g.astype(BF16)
        u_ref[...] = u.astype(BF16)
        a_ref[...] = (g * _sigmoid(g) * u).astype(BF16)

    o_spec = pl.BlockSpec((tm, tn), lambda i, j: (i, j))
    out = jax.ShapeDtypeStruct((tokens, D_FF), BF16)
    return pl.pallas_call(
        body, name=name, grid=(tokens // tm, nj), out_shape=(out, out, out),
        in_specs=[pl.BlockSpec((tm, D_MODEL), lambda i, j: (i, 0)),
                  pl.BlockSpec((tn, D_MODEL), lambda i, j: (j, 0)),
                  pl.BlockSpec((tn, D_MODEL), lambda i, j: (j + nj, 0))],
        out_specs=(o_spec, o_spec, o_spec),
        compiler_params=_params("parallel", "parallel"),
    )(h, wt_in, wt_in)


def _out_residual(a, w_out, res, gate, nxt, seq, name, lhs=None):
    def fn(rows, bats, vecs):
        acc, rv = rows[0], rows[-1]
        x_new = rv + bats[0] * acc
        made = [] if lhs is None else [rows[1]]
        return [x_new, acc, _norm_mod(x_new, vecs[0], bats[1], bats[2])] + made, [], []
    outs = [(D_MODEL, F32), (D_MODEL, BF16), (D_MODEL, BF16)] + ([] if lhs is None else [(D_MODEL, BF16)])
    return _rowmap(name, fn, seq, (a if lhs is not None else [a]) + [res], [gate, nxt["shift"], nxt["scale"]],
                   [nxt["gamma"]], row_outs=outs, ts=_tile(seq, (512, 256, 128)), mm=(w_out, "nn"), lhs=lhs)


def _out_loss(a, w_out, res, gate, target, seq, name):
    def fn(rows, bats, vecs):
        acc, rv, tv = rows
        err = rv + bats[0] * acc - tv
        dy = err * (1.0 / D_MODEL)
        return [dy, bats[0] * dy], [_colsum(dy * acc)], [_colsum(err * err)]
    return _rowmap(name, fn, seq, [a, res, target], [gate], row_outs=[(D_MODEL, F32), (D_MODEL, BF16)],
                   bat_outs=[D_MODEL], vec_outs=[D_MODEL], ts=_tile(seq, (512, 256, 128)), mm=(w_out, "nn"))


def _ffn_bwd_x(df, dres, saved, p, seq, tag, prev=None, mid=None):
    x, h, g, u, a, w_in, w_out = saved
    tokens = x.shape[0]

    def act_bwd(rows, bats, vecs):
        dav, gv, uv = rows[0], rows[1].astype(F32), rows[2].astype(F32)
        sg = _sigmoid(gv)
        silu = gv * sg
        dg = dav * uv * (sg * (1.0 + gv * (1.0 - sg)))
        return [jnp.concatenate([dg, dav * silu], axis=1)], [], []
    dgu = _rowmap(f"{tag}_bwd_da", act_bwd, seq, [df, g, u], row_outs=[(2 * D_FF, BF16)],
                  ts=_tile(seq, (256, 128)), mm=(w_out, "nt"))[0]
    operands = (a, df, dgu, h)
    after = None if mid is None else mid(operands)
    dh = _mm(dgu, w_in, "nn", f"{tag}_bwd_dh", tm=_tile(tokens, (512, 256)), tn=D_MODEL, after=after)
    return _norm_mod_bwd(dh, x, dres, p, seq, f"{tag}_bwd_norm", prev), operands


def _ffn_bwd_wout(operands, tag):
    a, df, _, _ = operands
    return _mm(a, df, "tn", f"{tag}_bwd_wout", out_dtype=BF16, tm=256, tn=D_MODEL)


def _ffn_bwd_win(operands, tag, after=None):
    _, _, dgu, h = operands
    return _mm(dgu, h, "tn", f"{tag}_bwd_win", out_dtype=BF16, tm=512, tn=D_MODEL, after=after)


def _shift_rows(v, k, forward):
    n = v.shape[0]
    row = lax.broadcasted_iota(jnp.int32, v.shape, 0)
    if forward:
        return jnp.where(row >= k, pltpu.roll(v, k, 0), 0.0)
    return jnp.where(row < n - k, pltpu.roll(v, n - k, 0), 0.0)


def _window_sums(v, forward):
    out, s, k = [], v, 1
    for _ in range(POOL_GROUPS):
        s = s + _shift_rows(s, k, forward)
        out.append(s)
        k *= 2
    return out


def _by_group(vals, g):
    out = vals[-1]
    for idx in range(len(vals) - 2, -1, -1):
        out = jnp.where(g == idx, vals[idx], out)
    return out


def _inv_count(shape, g):
    t1 = lax.broadcasted_iota(jnp.int32, shape, 0) + 1
    window = _by_group([jnp.int32(2 ** (i + 1)) for i in range(POOL_GROUPS)], g)
    return 1.0 / jnp.minimum(t1, window).astype(F32)


def _pool_fwd(u, grp, scale, seq):
    tokens = u.shape[0]

    def body(u_ref, grp_ref, sc_ref, pooled_ref, pg_ref, ps_ref):
        g = pl.program_id(1)
        uv = u_ref[...]
        sums = _by_group(_window_sums(uv, True), g)
        pooled = (sums * _inv_count(uv.shape, g) - uv).astype(BF16)
        pg = jnp.dot(pooled, grp_ref[0].astype(BF16), preferred_element_type=F32)
        pooled_ref[...] = pooled
        pg_ref[...] = pg
        ps_ref[...] = (pg * sc_ref[...]).astype(BF16)

    blk = pl.BlockSpec((seq, POOL_GROUP_DIM), lambda b, g: (b, g))
    return pl.pallas_call(
        body, name="pool_fwd", grid=(tokens // seq, POOL_GROUPS),
        out_shape=(jax.ShapeDtypeStruct(u.shape, BF16), jax.ShapeDtypeStruct(u.shape, F32),
                   jax.ShapeDtypeStruct(u.shape, BF16)),
        in_specs=[blk, pl.BlockSpec((1, POOL_GROUP_DIM, POOL_GROUP_DIM), lambda b, g: (g, 0, 0)),
                  pl.BlockSpec((1, POOL_GROUP_DIM), lambda b, g: (0, g))],
        out_specs=(blk, blk, blk),
        compiler_params=_params("parallel", "parallel"),
    )(u, grp, scale)


def _pool_bwd(dps, pooled, pg, grp, scale, seq):
    tokens = dps.shape[0]

    def body(dps_ref, pooled_ref, pg_ref, grp_ref, sc_ref, du_ref, dgrp_ref, dsc_ref):
        g, b = pl.program_id(0), pl.program_id(1)
        dpsv = dps_ref[...]
        dpg = (dpsv * sc_ref[...]).astype(BF16)
        dsc = _colsum(dpsv * pg_ref[...])
        dgrp = lax.dot_general(pooled_ref[...], dpg, _DIMS["tn"], preferred_element_type=F32)

        @pl.when(b == 0)
        def _():
            dsc_ref[...] = dsc
            dgrp_ref[0] = dgrp

        @pl.when(b > 0)
        def _():
            dsc_ref[...] += dsc
            dgrp_ref[0] += dgrp

        dpool = lax.dot_general(dpg, grp_ref[0].astype(BF16), _DIMS["nt"], preferred_element_type=F32)
        sums = _by_group(_window_sums(dpool * _inv_count(dpool.shape, g), False), g)
        du_ref[...] = (sums - dpool).astype(BF16)

    blk = pl.BlockSpec((seq, POOL_GROUP_DIM), lambda g, b: (b, g))
    grp_spec = pl.BlockSpec((1, POOL_GROUP_DIM, POOL_GROUP_DIM), lambda g, b: (g, 0, 0))
    vec_spec = pl.BlockSpec((1, POOL_GROUP_DIM), lambda g, b: (0, g))
    return pl.pallas_call(
        body, name="pool_bwd", grid=(POOL_GROUPS, tokens // seq),
        out_shape=(jax.ShapeDtypeStruct(dps.shape, BF16), jax.ShapeDtypeStruct(grp.shape, F32),
                   jax.ShapeDtypeStruct(scale.shape, F32)),
        in_specs=[blk, blk, blk, grp_spec, vec_spec],
        out_specs=(blk, grp_spec, vec_spec),
        compiler_params=_params("arbitrary", "arbitrary"),
    )(dps, pooled, pg, grp, scale)


def _lane(shape):
    return lax.broadcasted_iota(jnp.int32, shape, len(shape) - 1)


def _rot(y):
    lane = _lane(y.shape)
    r = jnp.where(lane < QK_NOPE + QK_ROPE // 2,
                  -pltpu.roll(y, HEAD_SLAB - QK_ROPE // 2, 1), pltpu.roll(y, QK_ROPE // 2, 1))
    return jnp.where(jnp.logical_and(lane >= QK_NOPE, lane < QK_NOPE + QK_ROPE), r, 0.0)


def _part_rstd(x):
    sq = x * x
    nope = _lane(x.shape) < QK_NOPE
    s_nope = jnp.sum(jnp.where(nope, sq, 0.0), axis=-1, keepdims=True)
    s_rope = jnp.sum(sq, axis=-1, keepdims=True) - s_nope
    return jnp.where(nope, lax.rsqrt(s_nope * (1.0 / QK_NOPE) + NORM_EPS),
                     lax.rsqrt(s_rope * (1.0 / QK_ROPE) + NORM_EPS))


def _part_norm_bwd(dy, x, r, g):
    nope = _lane(x.shape) < QK_NOPE
    xhat = x * r
    dxhat = dy * g
    prod = dxhat * xhat
    m_nope = jnp.sum(jnp.where(nope, prod, 0.0), axis=-1, keepdims=True)
    m_rope = jnp.sum(prod, axis=-1, keepdims=True) - m_nope
    mean = jnp.where(nope, m_nope * (1.0 / QK_NOPE), m_rope * (1.0 / QK_ROPE))
    return r * (dxhat - xhat * mean), dy * xhat


def _latent_norm_fwd(z_a, g_q, g_kv, seq):
    def fn(rows, bats, vecs):
        q, kv = rows[0][:, :Q_LORA], rows[0][:, Q_LORA:Q_LORA + KV_LORA]
        return [q * _rstd(q) * vecs[0], kv * _rstd(kv) * vecs[1]], [], []
    return _rowmap("latent_norm", fn, seq, [z_a], vecs=[g_q, g_kv],
                   row_outs=[(Q_LORA, BF16), (KV_LORA, BF16)])


def _latent_norm_bwd(dqn, dkvn, dkr, z_a, g_q, g_kv, seq):
    def fn(rows, bats, vecs):
        dq, dkv, dkrv, z = rows
        q, kv = z[:, :Q_LORA], z[:, Q_LORA:Q_LORA + KV_LORA]
        dxq, dgq = _norm_bwd(dq, q, _rstd(q), vecs[0])
        dxkv, dgkv = _norm_bwd(dkv, kv, _rstd(kv), vecs[1])
        return [jnp.concatenate([dxq, dxkv, dkrv], axis=1)], [], [_colsum(dgq), _colsum(dgkv)]
    return _rowmap("latent_norm_bwd", fn, seq, [dqn, dkvn, dkr, z_a], vecs=[g_q, g_kv],
                   row_outs=[(Q_LORA + KV_LORA + HEAD_SLAB, BF16)], vec_outs=[Q_LORA, KV_LORA])


def _qk_prep_fwd(qp, kv, z_a, pos, g_q, g_kn, g_kr, inv_freq, seq):
    def fn(rows, bats, vecs):
        qv, kvv, kr, p = rows
        gq, gkn, gkr, invf = vecs
        ang = p * invf
        cos, sin = jnp.cos(ang), jnp.sin(ang)
        nope = _lane(kr.shape) < QK_NOPE
        krn = kr * _rstd(kr, QK_ROPE) * gkr
        krr = krn * cos + _rot(krn) * sin
        qs, ks, vs = [], [], []
        for h in range(N_HEADS):
            xq = qv[:, h * HEAD_SLAB:(h + 1) * HEAD_SLAB]
            y = xq * _part_rstd(xq) * gq
            qs.append(y * cos + _rot(y) * sin)
            xk = kvv[:, h * HEAD_SLAB:(h + 1) * HEAD_SLAB]
            kn = jnp.where(nope, xk, 0.0)
            ks.append(jnp.where(nope, kn * _rstd(kn, QK_NOPE) * gkn, krr))
            vs.append(jnp.where(nope, 0.0, xk))
        return [jnp.concatenate(v, axis=1) for v in (qs, ks, vs)], [], []
    width = N_HEADS * HEAD_SLAB
    return _rowmap("qk_prep", fn, seq, [qp, kv, (z_a, HEAD_SLAB, 5), pos], vecs=[g_q, g_kn, g_kr, inv_freq],
                   row_outs=[(width, BF16)] * 3, ts=_tile(seq, (128, 64, 32, 16, 8)))


def _qk_prep_bwd(dqc, dkc, dvp, qp, kv, z_a, pos, g_q, g_kn, g_kr, inv_freq, seq):
    def fn(rows, bats, vecs):
        dq, dk, dv, qv, kvv, kr, p = rows
        gq, gkn, gkr, invf = vecs
        ang = p * invf
        cos, sin = jnp.cos(ang), jnp.sin(ang)
        nope = _lane(kr.shape) < QK_NOPE
        dqs, dkvs = [], []
        dgq = jnp.zeros((1, HEAD_SLAB), F32)
        dgkn = jnp.zeros((1, HEAD_SLAB), F32)
        dkrr = jnp.zeros(kr.shape, F32)
        for h in range(N_HEADS):
            sl = slice(h * HEAD_SLAB, (h + 1) * HEAD_SLAB)
            dyr = dq[:, sl]
            dy = dyr * cos - _rot(dyr * sin)
            xq = qv[:, sl]
            dx, dg = _part_norm_bwd(dy, xq, _part_rstd(xq), gq)
            dqs.append(dx)
            dgq = dgq + _colsum(dg)
            dkh = dk[:, sl]
            dkrr = dkrr + jnp.where(nope, 0.0, dkh)
            kn = jnp.where(nope, kvv[:, sl], 0.0)
            dxk, dgk = _norm_bwd(jnp.where(nope, dkh, 0.0), kn, _rstd(kn, QK_NOPE), gkn, QK_NOPE)
            dgkn = dgkn + _colsum(dgk)
            dkvs.append(jnp.where(nope, dxk, dv[:, sl]))
        dkrn = dkrr * cos - _rot(dkrr * sin)
        dkr, dgkr = _norm_bwd(dkrn, kr, _rstd(kr, QK_ROPE), gkr, QK_ROPE)
        return ([jnp.concatenate(dqs, axis=1), jnp.concatenate(dkvs, axis=1), dkr], [],
                [dgq, dgkn, _colsum(dgkr)])
    width = N_HEADS * HEAD_SLAB
    return _rowmap("qk_prep_bwd", fn, seq, [dqc, dkc, dvp, qp, kv, (z_a, HEAD_SLAB, 5), pos],
                   vecs=[g_q, g_kn, g_kr, inv_freq],
                   row_outs=[(width, BF16), (width, BF16), (HEAD_SLAB, F32)],
                   vec_outs=[HEAD_SLAB] * 3, ts=_tile(seq, (128, 64, 32, 16, 8)))


def _scores(q, k_ref, keys, tq):
    s = lax.dot_general(q, k_ref[0:keys, :], _DIMS["nt"], preferred_element_type=F32) * ATTN_SCALE
    row = lax.broadcasted_iota(jnp.int32, (tq, tq), 0)
    col = lax.broadcasted_iota(jnp.int32, (tq, tq), 1)
    diag = jnp.where(col <= row, s[:, keys - tq:], -1e30)
    return diag if keys == tq else jnp.concatenate([s[:, :keys - tq], diag], axis=1)


def _attn_fwd(qc, kc, vp, seq):
    tokens = qc.shape[0]
    tq = _tile(seq, (256, 128))
    nq = seq // tq

    def body(q_ref, k_ref, v_ref, o_ref, lse_ref):
        for i in range(nq):
            rows, keys = slice(i * tq, (i + 1) * tq), (i + 1) * tq
            s = _scores(q_ref[rows, :], k_ref, keys, tq)
            m = jnp.max(s, axis=-1, keepdims=True)
            p = jnp.exp(s - m)
            l = jnp.sum(p, axis=-1, keepdims=True)
            acc = jnp.dot(p.astype(BF16), v_ref[0:keys, :], preferred_element_type=F32)
            o_ref[rows, :] = (acc / l).astype(BF16)
            lse_ref[rows, :] = jnp.broadcast_to(m + jnp.log(l), (tq, HEAD_SLAB))

    spec = pl.BlockSpec((seq, HEAD_SLAB), lambda b, h: (b, h))
    return pl.pallas_call(
        body, name="attn_fwd", grid=(tokens // seq, N_HEADS),
        out_shape=(jax.ShapeDtypeStruct(qc.shape, BF16), jax.ShapeDtypeStruct(qc.shape, F32)),
        in_specs=[spec] * 3, out_specs=(spec, spec),
        compiler_params=_params("parallel", "parallel"),
    )(qc, kc, vp)


def _attn_bwd(qc, kc, vp, o, lse, do, seq):
    tokens = qc.shape[0]
    tq = _tile(seq, (256, 128))
    nq = seq // tq

    def body(q_ref, k_ref, v_ref, o_ref, lse_ref, do_ref, dq_ref, dk_ref, dv_ref):
        dk_ref[...] = jnp.zeros(dk_ref.shape, F32)
        dv_ref[...] = jnp.zeros(dv_ref.shape, F32)
        for i in range(nq):
            rows, keys = slice(i * tq, (i + 1) * tq), (i + 1) * tq
            q, dov = q_ref[rows, :], do_ref[rows, :]
            delta = jnp.sum(dov.astype(F32) * o_ref[rows, :].astype(F32), axis=-1, keepdims=True)
            s = _scores(q, k_ref, keys, tq)
            p = jnp.exp(s - jnp.tile(lse_ref[rows, :], (1, keys // HEAD_SLAB)))
            dp = lax.dot_general(dov, v_ref[0:keys, :], _DIMS["nt"], preferred_element_type=F32)
            ds = (p * (dp - delta) * ATTN_SCALE).astype(BF16)
            dq_ref[rows, :] = jnp.dot(ds, k_ref[0:keys, :], preferred_element_type=F32)
            dk_ref[0:keys, :] += lax.dot_general(ds, q, _DIMS["tn"], preferred_element_type=F32)
            dv_ref[0:keys, :] += lax.dot_general(p.astype(BF16), dov, _DIMS["tn"], preferred_element_type=F32)

    spec = pl.BlockSpec((seq, HEAD_SLAB), lambda b, h: (b, h))
    out = jax.ShapeDtypeStruct(qc.shape, F32)
    return pl.pallas_call(
        body, name="attn_bwd", grid=(tokens // seq, N_HEADS),
        out_shape=(out, out, out), in_specs=[spec] * 6, out_specs=(spec, spec, spec),
        compiler_params=_params("parallel", "parallel"),
    )(qc, kc, vp, o, lse, do)


def _adamw(w, g, m, v, name):
    rows, cols = w.shape
    whole = rows * cols * 4 <= ADAMW_WHOLE_BYTES
    tr = rows if whole else _tile(rows, (256, 128, 64, 32, 16, 8))
    c1 = 1.0 - ADAM_B1 ** ADAM_STEP
    c2 = 1.0 - ADAM_B2 ** ADAM_STEP

    def body(w_ref, g_ref, m_ref, v_ref, d_ref, nm_ref, nv_ref):
        gv = g_ref[...]
        nm = ADAM_B1 * m_ref[...] + (1.0 - ADAM_B1) * gv
        nv = ADAM_B2 * v_ref[...] + (1.0 - ADAM_B2) * (gv * gv)
        d_ref[...] = -ADAM_LR * ((nm / c1) / (jnp.sqrt(nv / c2) + ADAM_EPS) + ADAM_WD * w_ref[...])
        nm_ref[...] = nm
        nv_ref[...] = nv

    spec = pl.BlockSpec((tr, cols), lambda i: (i, 0))
    out = jax.ShapeDtypeStruct(w.shape, F32)
    return pl.pallas_call(
        body, name=name, grid=(rows // tr,), out_shape=(out, out, out),
        in_specs=[spec] * 4, out_specs=(spec, spec, spec),
        compiler_params=_params("parallel"),
    )(w, g, m, v)


def _mod_cols(c_all, w_ada, b_cols):
    def body(c_ref, w_ref, b_ref, act_ref, mod_ref):
        cv = c_ref[...]
        act = cv * _sigmoid(cv)
        act_ref[...] = act
        mod_ref[...] = jnp.dot(act.astype(BF16), w_ref[...].astype(BF16),
                               preferred_element_type=F32) + b_ref[...]

    n = w_ada.shape[1]
    return pl.pallas_call(
        body, name="mod_cols",
        out_shape=(jax.ShapeDtypeStruct(c_all.shape, F32), jax.ShapeDtypeStruct((c_all.shape[0], n), F32)),
        compiler_params=pltpu.CompilerParams(vmem_limit_bytes=VMEM_LIMIT),
    )(c_all, w_ada, b_cols)


def _ada_grads(c_act, dmod_all, dmod_cols):
    def body(c_ref, d_ref, dc_ref, gw_ref, gb_ref):
        gw_ref[...] = lax.dot_general(c_ref[...].astype(BF16), dc_ref[...].astype(BF16), _DIMS["tn"],
                                      preferred_element_type=F32)
        gb_ref[...] = _colsum(d_ref[...])

    return pl.pallas_call(
        body, name="ada_grads",
        out_shape=(jax.ShapeDtypeStruct((c_act.shape[1], dmod_cols.shape[1]), F32),
                   jax.ShapeDtypeStruct((1, dmod_all.shape[1]), F32)),
        compiler_params=pltpu.CompilerParams(vmem_limit_bytes=VMEM_LIMIT),
    )(c_act, dmod_all, dmod_cols)


def _flat_rows(a):
    flat = a.reshape(-1)
    pad = (-flat.shape[0]) % (LANES * SUBLANES)
    if pad:
        flat = jnp.pad(flat, (0, pad))
    return flat.reshape(-1, LANES)


def _gather_start(w, groups, tag, after=None):
    shards = [[(w[n] if n in ROW_SHARDED else w[n].T).astype(BF16) for n in names] for names in groups]
    return _exchange_start_groups(shards, f"gather_{tag}_start", after=after)


def _gather_wait(handle, names, tag, after):
    landed = _exchange_wait(handle, f"gather_{tag}_wait", after=after)
    return {n: g.reshape(-1, g.shape[2]) for n, g in zip(names, landed)}


def _scatter_start(grads, names, tag, after=None):
    blocks = [grads[n].reshape(N_DEV, -1, grads[n].shape[1]) for n in names]
    return _exchange_start(blocks, f"scatter_{tag}_start", scatter=True, after=after)


def _scatter_wait(handle, names, tag, after):
    landed = _exchange_wait(handle, f"scatter_{tag}_wait", scatter=True, after=after)
    out = {}
    for n, x in zip(names, landed):
        g = _sum_blocks(x, f"sum_{n}")
        out[n] = g if (n in ROW_SHARDED or n in KEPT_TRANSPOSED) else g.T
    return out


def _pack_small(vals):
    return jnp.concatenate([_flat_rows(v.astype(F32)) for v in vals], axis=0)


def _unpack_small(packed, like):
    out, row = [], 0
    for v in like:
        rows = _flat_rows(v).shape[0]
        out.append(packed[row:row + rows].reshape(-1)[:v.size].reshape(v.shape))
        row += rows
    return out


def _lanes128(*parts):
    out = jnp.zeros((HEAD_SLAB,), F32)
    for off, v in parts:
        out = lax.dynamic_update_slice(out, v.reshape(-1).astype(F32), (off,))
    return out.reshape(1, HEAD_SLAB)


def _step(x, c, positions, w, m, v, loss_target):
    nseq, seq, _ = x.shape
    tokens = nseq * seq
    me = _index(_my_pos())
    strip = lambda d: {n: (a[0] if a.ndim > 2 else a) for n, a in d.items()}
    shapes = {n: a.shape for n, a in w.items()}
    w, m, v = strip(w), strip(m), strip(v)

    c_all = _all_gather(c.reshape(-1, LANES), "gather_c").reshape(N_DEV * nseq, D_MODEL)
    n_ada = w["w_ada"].shape[1]
    b_cols = lax.dynamic_slice(w["b_ada"], (0, me * n_ada), (1, n_ada))
    c_act, mod_cols = _mod_cols(c_all, w["w_ada"], b_cols)
    mod_all = _all_gather(mod_cols, "gather_mod")
    mod = lax.dynamic_slice(mod_all, (0, me * nseq, 0), (N_DEV, nseq, n_ada))
    mod = mod.transpose(1, 0, 2).reshape(nseq, 3, 3, 1, D_MODEL)

    (h_f1i, h_f1o, h_mix_in, h_mix, h_f2), tok = _gather_start(
        w, (("w_ffn1_in",), ("w_ffn1_out",), MIXER[:1], MIXER[1:], ("w_ffn2_in", "w_ffn2_out")), "weights",
        after=mod_all)
    started = tok[0:1, 0:1]

    g_q = _lanes128((0, w["q_norm_nope"]), (QK_NOPE, w["q_norm_rope"]))
    g_kn = _lanes128((0, w["k_norm_nope"]))
    g_kr = _lanes128((QK_NOPE, w["k_norm_rope"]))
    freq = ROPE_THETA ** (-jnp.arange(0, QK_ROPE, 2, dtype=F32) / QK_ROPE)
    inv_freq = _lanes128((QK_NOPE, jnp.concatenate([freq, freq])))
    pos = positions.reshape(tokens, 1).astype(F32)

    def sub(k, gamma, coef):
        return dict(gamma=w[gamma], shift=mod[:, k, 0] + started, scale=mod[:, k, 1], gate=coef * mod[:, k, 2])
    p1, pm, p2 = sub(0, "norm_ffn1", 0.5), sub(1, "norm_mix", 1.0), sub(2, "norm_ffn2", 0.5)
    t_big = _tile(tokens, (2048, 1024, 512))
    t_mid = _tile(tokens, (1024, 512))

    x0 = x.reshape(tokens, D_MODEL)
    h1 = _norm_mod_fwd(x0, p1, seq, "ffn1_norm")
    wt_f1i = _gather_wait(h_f1i, ("w_ffn1_in",), "ffn1_in", h1)["w_ffn1_in"]
    g1, u1, a1 = _ffn_in_act(h1, wt_f1i, "ffn1_in")
    w_f1o = _gather_wait(h_f1o, ("w_ffn1_out",), "ffn1_out", a1)["w_ffn1_out"]
    x1, f1, h2 = _out_residual(a1, w_f1o, x0, p1["gate"], pm, seq, "ffn1_out")
    saved1 = (x0, h1, g1, u1, a1, wt_f1i, w_f1o)

    wt_in = _gather_wait(h_mix_in, MIXER[:1], "mix_in", h2)["w_in"]
    zero_rows = lambda rows: jnp.zeros((rows, D_MODEL), BF16)
    wt_p = wt_in[:512]
    wt_a = jnp.concatenate([wt_in[512:1152], zero_rows(QK_NOPE), wt_in[1152:1184], zero_rows(32)], axis=0)
    wt_g = wt_in[1184:]
    z_a = _mm(h2, wt_a, "nt", "mix_in_a", tm=t_big, tn=wt_a.shape[0])
    z_p = _mm(h2, wt_p, "nt", "mix_in_p", tm=t_big, tn=512)
    z_g = _mm(h2, wt_g, "nt", "mix_in_g", tm=t_big, tn=512)

    full = _gather_wait(h_mix, MIXER[1:], "mix", z_g)
    wtq_pad = jnp.pad(full["w_q_up"].reshape(N_HEADS, 96, Q_LORA), ((0, 0), (0, 32), (0, 0))).reshape(-1, Q_LORA)
    wtmla_pad = jnp.pad(full["w_mla_proj"].reshape(D_MODEL, N_HEADS, 64), ((0, 0), (0, 0), (64, 0))).reshape(D_MODEL, -1)
    wt_pool, wt_kv, w_mix_out = full["w_pool_proj"], full["w_kv_up"], full["w_out"]
    pooled, pg, ps = _pool_fwd(z_p, w["pool_grp"], w["pool_scale"], seq)
    br_pool = _mm(ps, wt_pool, "nt", "pool_proj", tm=t_big, tn=D_MODEL)
    qn, kvn = _latent_norm_fwd(z_a, w["q_a_norm"], w["kv_a_norm"], seq)
    qp = _mm(qn, wtq_pad, "nt", "q_up", tm=t_big, tn=D_MODEL)
    kv = _mm(kvn, wt_kv, "nt", "kv_up", tm=t_big, tn=D_MODEL)
    qc, kc, vp = _qk_prep_fwd(qp, kv, z_a, pos, g_q, g_kn, g_kr, inv_freq, seq)
    attn, lse = _attn_fwd(qc, kc, vp, seq)
    br_mla = _mm(attn, wtmla_pad, "nt", "mla_proj", tm=t_mid, tn=D_MODEL)

    def merge(rows):
        zg, bp, bm = rows[:3]
        return (_sigmoid(zg[:, :D_MODEL]) * bp + _sigmoid(zg[:, D_MODEL:]) * bm).astype(BF16)
    x2, o_mix, h3, merged = _out_residual([z_g, br_pool, br_mla], w_mix_out, x1, pm["gate"], p2, seq, "mix_out",
                                          lhs=merge)

    ffn2_w = _gather_wait(h_f2, ("w_ffn2_in", "w_ffn2_out"), "ffn2", h3)
    g2, u2, a2 = _ffn_in_act(h3, ffn2_w["w_ffn2_in"], "ffn2_in")
    dy, df2, dgate2, sq_err = _out_loss(a2, ffn2_w["w_ffn2_out"], x2, p2["gate"],
                                        loss_target.reshape(tokens, D_MODEL), seq, "ffn2_out")
    saved2 = (x2, h3, g2, u2, a2, ffn2_w["w_ffn2_in"], ffn2_w["w_ffn2_out"])

    grads = {}
    (dx2, do_mix, dsh2, dsc2, dgate_m, dg_ffn2), ops2 = _ffn_bwd_x(df2, dy, saved2, p2, seq, "ffn2", (o_mix, pm["gate"]))
    grads["w_ffn2_out"], grads["w_ffn2_in"] = _ffn_bwd_wout(ops2, "ffn2"), _ffn_bwd_win(ops2, "ffn2")
    s_f2, tok = _scatter_start(grads, ("w_ffn2_in", "w_ffn2_out"), "ffn2")

    grads["w_out"] = _mm(merged, do_mix, "tn", "mix_bwd_wout", out_dtype=BF16, tm=512, tn=D_MODEL)

    def merge_bwd(rows, bats, vecs):
        dmv, zg, bp, bm = rows
        s_p, s_m = _sigmoid(zg[:, :D_MODEL]), _sigmoid(zg[:, D_MODEL:])
        dzg = jnp.concatenate([dmv * bp * s_p * (1.0 - s_p), dmv * bm * s_m * (1.0 - s_m)], axis=1)
        return [dmv * s_p, dmv * s_m, dzg], [], []
    dbr_pool, dbr_mla, dz_g = _rowmap("mix_bwd_dmerged", merge_bwd, seq, [do_mix, z_g, br_pool, br_mla],
                                      row_outs=[(D_MODEL, BF16), (D_MODEL, BF16), (2 * D_MODEL, BF16)],
                                      mm=(w_mix_out, "nt"))

    grads["w_pool_proj"] = _mm(dbr_pool, ps, "tn", "pool_bwd_wproj", out_dtype=BF16, tm=512, tn=POOL_WIDTH)
    dps = _mm(dbr_pool, wt_pool, "nn", "pool_bwd_dps", tm=t_big, tn=POOL_WIDTH)
    dz_p, dgrp, dpool_scale = _pool_bwd(dps, pooled, pg, w["pool_grp"], w["pool_scale"] + tok[0:1, 0:1], seq)

    dwtmla_pad = _mm(dbr_mla, attn, "tn", "mla_bwd_wproj", out_dtype=BF16, tm=512, tn=D_MODEL)
    grads["w_mla_proj"] = dwtmla_pad.reshape(D_MODEL, N_HEADS, HEAD_SLAB)[:, :, 64:].reshape(D_MODEL, -1)
    d_attn = _mm(dbr_mla, wtmla_pad, "nn", "mla_bwd_dattn", out_dtype=BF16, tm=t_mid, tn=D_MODEL)
    dqc, dkc, dvp = _attn_bwd(qc, kc, vp, attn, lse, d_attn, seq)
    dqp, dkv, dkr, dg_q, dg_kn, dg_kr = _qk_prep_bwd(dqc, dkc, dvp, qp, kv, z_a, pos, g_q, g_kn, g_kr, inv_freq, seq)
    dwtq_pad = _mm(dqp, qn, "tn", "q_up_bwd_w", out_dtype=BF16, tm=512, tn=Q_LORA)
    grads["w_q_up"] = dwtq_pad.reshape(N_HEADS, HEAD_SLAB, Q_LORA)[:, :96].reshape(-1, Q_LORA)
    grads["w_kv_up"] = _mm(dkv, kvn, "tn", "kv_up_bwd_w", out_dtype=BF16, tm=512, tn=KV_LORA)
    dqn = _mm(dqp, wtq_pad, "nn", "q_up_bwd_x", tm=t_big, tn=Q_LORA)
    dkvn = _mm(dkv, wt_kv, "nn", "kv_up_bwd_x", tm=t_big, tn=KV_LORA)
    dz_a, dg_qa, dg_kva = _latent_norm_bwd(dqn, dkvn, dkr, z_a, w["q_a_norm"], w["kv_a_norm"], seq)

    dwt_a = _mm(dz_a, h2, "tn", "mix_in_bwd_wa", out_dtype=BF16, tm=256, tn=D_MODEL)
    dwt_p = _mm(dz_p, h2, "tn", "mix_in_bwd_wp", out_dtype=BF16, tm=512, tn=D_MODEL)
    dwt_g = _mm(dz_g, h2, "tn", "mix_in_bwd_wg", out_dtype=BF16, tm=512, tn=D_MODEL)
    grads["w_in"] = jnp.concatenate([dwt_p, dwt_a[:640], dwt_a[704:736], dwt_g], axis=0)
    s_mix, tok = _scatter_start(grads, MIXER, "mix")
    dh2 = _mm(dz_a, wt_a, "nn", "mix_in_bwd_xa", tm=t_mid, tn=D_MODEL)
    dh2 = _mm(dz_p, wt_p, "nn", "mix_in_bwd_xp", tm=t_mid, tn=D_MODEL, add=dh2)
    dh2 = _mm(dz_g, wt_g, "nn", "mix_in_bwd_xg", tm=t_mid, tn=D_MODEL, add=dh2)
    pm_tied = dict(pm, scale=pm["scale"] + tok[0:1, 0:1])
    dx1, df1, dsh_m, dsc_m, dgate1, dg_mix = _norm_mod_bwd(dh2, x1, dx2, pm_tied, seq, "mix_bwd_norm", (f1, p1["gate"]))

    small_early = [dg_mix.reshape(w["norm_mix"].shape), dg_ffn2.reshape(w["norm_ffn2"].shape), dgrp, dpool_scale,
                   dg_qa, dg_kva, dg_q[:, :QK_NOPE], dg_q[:, QK_NOPE:QK_NOPE + QK_ROPE], dg_kn[:, :QK_NOPE],
                   dg_kr[:, QK_NOPE:QK_NOPE + QK_ROPE]]
    s_small, tok = _exchange_start([_pack_small(small_early)], "gather_small_start", after=tok)

    handles = {}

    def ffn1_mid(operands):
        grads["w_ffn1_out"] = _ffn_bwd_wout(operands, "ffn1")
        handles["f1o"], token = _scatter_start(grads, ("w_ffn1_out",), "ffn1_out")
        grads["w_ffn1_in"] = _ffn_bwd_win(operands, "ffn1", after=token)
        handles["f1i"], token = _scatter_start(grads, ("w_ffn1_in",), "ffn1_in", after=token)
        return token

    p1_tied = dict(p1, scale=p1["scale"] + tok[0:1, 0:1])
    (dx0, dsh1, dsc1, dg_ffn1), ops1 = _ffn_bwd_x(df1, dx1, saved1, p1_tied, seq, "ffn1", mid=ffn1_mid)
    s_f1o, s_f1i = handles["f1o"], handles["f1i"]

    dmod = jnp.stack([jnp.stack([dsh1, dsc1, 0.5 * dgate1], axis=1),
                      jnp.stack([dsh_m, dsc_m, dgate_m], axis=1),
                      jnp.stack([dsh2, dsc2, 0.5 * dgate2], axis=1)], axis=1)
    n_dmod = nseq * 9 * D_MODEL // LANES
    tail = _all_gather(jnp.concatenate([dmod.reshape(-1, LANES), _flat_rows(dg_ffn1), _flat_rows(sq_err)], axis=0),
                       "gather_dmod")
    dmod_all = tail[:, :n_dmod].reshape(N_DEV * nseq, 9 * D_MODEL)

    dmod_cols = lax.dynamic_slice(dmod_all, (0, me * n_ada), (N_DEV * nseq, n_ada))
    g_w_ada, g_b_ada = _ada_grads(c_act, dmod_all, dmod_cols)
    tail_sum = _sum_blocks(tail[:, n_dmod:], "sum_tail")
    g_norm_ffn1 = tail_sum[:SUBLANES].reshape(1, D_MODEL)
    loss = 0.5 * jnp.sum(tail_sum[SUBLANES:]) * (1.0 / D_MODEL)
    small_all = _exchange_wait(s_small, "gather_small_wait", after=g_b_ada)[0]
    small_sum = _sum_blocks(small_all, "sum_small")
    small = dict(zip(SMALL[1:], _unpack_small(small_sum, [w[n] for n in SMALL[1:]])))
    grad_w = dict(small, w_ada=g_w_ada, b_ada=g_b_ada, norm_ffn1=g_norm_ffn1)

    delta, new_m, new_v = {}, {}, {}

    def update(names):
        for n in names:
            if n in KEPT_TRANSPOSED:
                res = _adamw(w[n].T, grad_w[n], m[n].T, v[n].T, f"adamw_{n}")
                delta[n], new_m[n], new_v[n] = (r.T for r in res)
                grad_w[n] = grad_w[n].T
            else:
                delta[n], new_m[n], new_v[n] = _adamw(w[n], grad_w[n], m[n], v[n], f"adamw_{n}")

    update(("w_ada",))
    rep = ("b_ada",) + SMALL
    d_s, m_s, v_s = _adamw(_pack_small([w[n] for n in rep]), _pack_small([grad_w[n] for n in rep]),
                           _pack_small([m[n] for n in rep]), _pack_small([v[n] for n in rep]), "adamw_small")
    like = [w[n] for n in rep]
    for dst, packed in ((delta, d_s), (new_m, m_s), (new_v, v_s)):
        dst.update(zip(rep, _unpack_small(packed, like)))
    grad_w.update(_scatter_wait(s_f2, ("w_ffn2_in", "w_ffn2_out"), "ffn2", after=d_s))
    update(("w_ffn2_in", "w_ffn2_out"))
    grad_w.update(_scatter_wait(s_mix, MIXER, "mix", after=delta["w_ffn2_out"]))
    update(MIXER)
    grad_w.update(_scatter_wait(s_f1o, ("w_ffn1_out",), "ffn1_out", after=delta["w_out"]))
    update(("w_ffn1_out",))
    grad_w.update(_scatter_wait(s_f1i, ("w_ffn1_in",), "ffn1_in", after=delta["w_ffn1_out"]))
    update(("w_ffn1_in",))

    lead = lambda d: [d[n].reshape(shapes[n]) for n in WEIGHTS]
    return (loss, dx0.reshape(x.shape), *lead(grad_w), *lead(delta), *lead(new_m), *lead(new_v))


def kernel(x, c, positions, w_ada, b_ada, norm_ffn1, w_ffn1_in, w_ffn1_out, norm_mix, w_in, pool_grp, pool_scale, w_pool_proj, q_a_norm, w_q_up, kv_a_norm, w_kv_up, q_norm_nope, q_norm_rope, k_norm_nope, k_norm_rope, w_mla_proj, w_out, norm_ffn2, w_ffn2_in, w_ffn2_out, loss_target, m_w_ada, m_b_ada, m_norm_ffn1, m_w_ffn1_in, m_w_ffn1_out, m_norm_mix, m_w_in, m_pool_grp, m_pool_scale, m_w_pool_proj, m_q_a_norm, m_w_q_up, m_kv_a_norm, m_w_kv_up, m_q_norm_nope, m_q_norm_rope, m_k_norm_nope, m_k_norm_rope, m_w_mla_proj, m_w_out, m_norm_ffn2, m_w_ffn2_in, m_w_ffn2_out, v_w_ada, v_b_ada, v_norm_ffn1, v_w_ffn1_in, v_w_ffn1_out, v_norm_mix, v_w_in, v_pool_grp, v_pool_scale, v_w_pool_proj, v_q_a_norm, v_w_q_up, v_kv_a_norm, v_w_kv_up, v_q_norm_nope, v_q_norm_rope, v_k_norm_nope, v_k_norm_rope, v_w_mla_proj, v_w_out, v_norm_ffn2, v_w_ffn2_in, v_w_ffn2_out):
    w = dict(w_ada=w_ada, b_ada=b_ada, norm_ffn1=norm_ffn1, w_ffn1_in=w_ffn1_in, w_ffn1_out=w_ffn1_out, norm_mix=norm_mix, w_in=w_in, pool_grp=pool_grp, pool_scale=pool_scale, w_pool_proj=w_pool_proj, q_a_norm=q_a_norm, w_q_up=w_q_up, kv_a_norm=kv_a_norm, w_kv_up=w_kv_up, q_norm_nope=q_norm_nope, q_norm_rope=q_norm_rope, k_norm_nope=k_norm_nope, k_norm_rope=k_norm_rope, w_mla_proj=w_mla_proj, w_out=w_out, norm_ffn2=norm_ffn2, w_ffn2_in=w_ffn2_in, w_ffn2_out=w_ffn2_out)
    m = dict(w_ada=m_w_ada, b_ada=m_b_ada, norm_ffn1=m_norm_ffn1, w_ffn1_in=m_w_ffn1_in, w_ffn1_out=m_w_ffn1_out, norm_mix=m_norm_mix, w_in=m_w_in, pool_grp=m_pool_grp, pool_scale=m_pool_scale, w_pool_proj=m_w_pool_proj, q_a_norm=m_q_a_norm, w_q_up=m_w_q_up, kv_a_norm=m_kv_a_norm, w_kv_up=m_w_kv_up, q_norm_nope=m_q_norm_nope, q_norm_rope=m_q_norm_rope, k_norm_nope=m_k_norm_nope, k_norm_rope=m_k_norm_rope, w_mla_proj=m_w_mla_proj, w_out=m_w_out, norm_ffn2=m_norm_ffn2, w_ffn2_in=m_w_ffn2_in, w_ffn2_out=m_w_ffn2_out)
    v = dict(w_ada=v_w_ada, b_ada=v_b_ada, norm_ffn1=v_norm_ffn1, w_ffn1_in=v_w_ffn1_in, w_ffn1_out=v_w_ffn1_out, norm_mix=v_norm_mix, w_in=v_w_in, pool_grp=v_pool_grp, pool_scale=v_pool_scale, w_pool_proj=v_w_pool_proj, q_a_norm=v_q_a_norm, w_q_up=v_w_q_up, kv_a_norm=v_kv_a_norm, w_kv_up=v_w_kv_up, q_norm_nope=v_q_norm_nope, q_norm_rope=v_q_norm_rope, k_norm_nope=v_k_norm_nope, k_norm_rope=v_k_norm_rope, w_mla_proj=v_w_mla_proj, w_out=v_w_out, norm_ffn2=v_norm_ffn2, w_ffn2_in=v_w_ffn2_in, w_ffn2_out=v_w_ffn2_out)
    return _step(x, c, positions, w, m, v, loss_target)
```

```python
import functools
import math

import jax
import jax.numpy as jnp
from jax import lax
from jax.experimental import pallas as pl
from jax.experimental.pallas import tpu as pltpu

F32 = jnp.float32
BF16 = jnp.bfloat16
MESH = pl.DeviceIdType.MESH
AXES = ("x", "y", "c")
N_DEV = 8

D_MODEL = 1024
D_FF = 2816
N_HEADS = 8
HEAD_SLAB = 128
QK_NOPE = 64
QK_ROPE = 32
POOL_WIDTH = 512
POOL_GROUPS = 4
POOL_GROUP_DIM = 128
Q_LORA = 384
KV_LORA = 256
ROPE_THETA = 10000.0
ATTN_SCALE = 1.0 / math.sqrt(QK_NOPE + QK_ROPE)
NORM_EPS = 1e-6
ADAM_LR, ADAM_B1, ADAM_B2, ADAM_EPS, ADAM_WD, ADAM_STEP = 0.001, 0.9, 0.999, 1e-08, 0.01, 10

LANES = 128
SUBLANES = 8
VMEM_LIMIT = 52 * 1024 * 1024
ADAMW_WHOLE_BYTES = 3 << 19
SUM_WHOLE_BYTES = 4 << 20

BIG = ("w_ffn1_in", "w_ffn1_out", "w_in", "w_pool_proj", "w_q_up", "w_kv_up",
       "w_mla_proj", "w_out", "w_ffn2_in", "w_ffn2_out")
ROW_SHARDED = ("w_ffn1_out", "w_out", "w_ffn2_out")
MIXER = ("w_in", "w_pool_proj", "w_q_up", "w_kv_up", "w_mla_proj", "w_out")
KEPT_TRANSPOSED = ("w_ffn1_in", "w_ffn2_in", "w_q_up")
SMALL = ("norm_ffn1", "norm_mix", "norm_ffn2", "pool_grp", "pool_scale", "q_a_norm",
         "kv_a_norm", "q_norm_nope", "q_norm_rope", "k_norm_nope", "k_norm_rope")
WEIGHTS = ("w_ada", "b_ada", "norm_ffn1", "w_ffn1_in", "w_ffn1_out", "norm_mix", "w_in",
           "pool_grp", "pool_scale", "w_pool_proj", "q_a_norm", "w_q_up", "kv_a_norm",
           "w_kv_up", "q_norm_nope", "q_norm_rope", "k_norm_nope", "k_norm_rope",
           "w_mla_proj", "w_out", "norm_ffn2", "w_ffn2_in", "w_ffn2_out")


def _params(*sem):
    return pltpu.CompilerParams(dimension_semantics=sem, vmem_limit_bytes=VMEM_LIMIT)


def _tile(n, cands):
    for c in cands:
        if n % c == 0:
            return c
    return n


def _my_pos():
    return lax.axis_index("x"), lax.axis_index("y"), lax.axis_index("c")


def _flip(pos, k):
    x, y, c = pos
    fx, fy, fc = (k >> 2) & 1, (k >> 1) & 1, k & 1
    return ((1 - x) if fx else x, (1 - y) if fy else y, (1 - c) if fc else c)


def _index(pos):
    x, y, c = pos
    return 4 * x + 2 * y + c


def _exchange(arrays, name, scatter=False):
    n = len(arrays)

    def body(*refs):
        ins, outs = refs[:n], refs[n:2 * n]
        send_sems, recv_sems, local_sems = refs[2 * n:]
        me = _my_pos()
        mine, sends = [], []
        for a in range(n):
            own = ins[a].at[_index(me)] if scatter else ins[a]
            cp = pltpu.make_async_copy(own, outs[a].at[_index(me)], local_sems.at[a])
            cp.start()
            mine.append(cp)
        for k in range(1, N_DEV):
            peer = _flip(me, k)
            for a in range(n):
                cp = pltpu.make_async_remote_copy(
                    src_ref=ins[a].at[_index(peer)] if scatter else ins[a],
                    dst_ref=outs[a].at[_index(me)],
                    send_sem=send_sems.at[a, k - 1], recv_sem=recv_sems.at[a, k - 1],
                    device_id=peer, device_id_type=MESH)
                cp.start()
                sends.append(cp)
        for k in range(1, N_DEV):
            peer = _flip(me, k)
            for a in range(n):
                pltpu.make_async_remote_copy(
                    src_ref=ins[a].at[_index(me)] if scatter else ins[a],
                    dst_ref=outs[a].at[_index(peer)],
                    send_sem=send_sems.at[a, k - 1], recv_sem=recv_sems.at[a, k - 1],
                    device_id=peer, device_id_type=MESH).wait_recv()
        for cp in sends:
            cp.wait_send()
        for cp in mine:
            cp.wait()

    shape = lambda x: x.shape if scatter else (N_DEV,) + x.shape
    return pl.pallas_call(
        body, name=name,
        out_shape=tuple(jax.ShapeDtypeStruct(shape(x), x.dtype) for x in arrays),
        in_specs=[pl.BlockSpec(memory_space=pl.ANY)] * n,
        out_specs=tuple(pl.BlockSpec(memory_space=pl.ANY) for _ in arrays),
        scratch_shapes=[pltpu.SemaphoreType.DMA((n, N_DEV - 1)),
                        pltpu.SemaphoreType.DMA((n, N_DEV - 1)),
                        pltpu.SemaphoreType.DMA((n,))],
    )(*arrays)


def _all_gather(x, name):
    return _exchange([x], name)[0]


_HBM = pl.BlockSpec(memory_space=pltpu.HBM)
_SEM = pl.BlockSpec(memory_space=pltpu.SEMAPHORE)
_ANY = pl.BlockSpec(memory_space=pl.ANY)
_EFFECT = pltpu.SideEffectType.DATAFLOW_SIDE_EFFECTING


def _split_copy(ins, lands, send_sems, recv_sems, a, k, me, scatter, incoming):
    peer = _flip(me, k)
    block = me if incoming else peer
    return pltpu.make_async_remote_copy(
        src_ref=ins[a].at[_index(block)] if scatter else ins[a],
        dst_ref=lands[a].at[_index(peer if incoming else me)],
        send_sem=send_sems.at[a * (N_DEV - 1) + k - 1], recv_sem=recv_sems.at[a * (N_DEV - 1) + k - 1],
        device_id=peer, device_id_type=MESH)


def _exchange_start_groups(groups, name, scatter=False, after=None):
    sizes = [len(g) for g in groups]
    first = [sum(sizes[:i]) for i in range(len(sizes))]
    n, ng = sum(sizes), len(sizes)
    after = jnp.zeros((SUBLANES, LANES), F32) if after is None else after

    def body(*refs):
        ins, lands = refs[:n], refs[n:2 * n]
        sems = refs[2 * n + 1:2 * n + 1 + 2 * ng]
        me = _my_pos()
        for g in range(ng):
            part = slice(first[g], first[g] + sizes[g])
            for k in range(1, N_DEV):
                for a in range(sizes[g]):
                    _split_copy(ins[part], lands[part], sems[2 * g], sems[2 * g + 1], a, k, me, scatter, False).start()
        refs[-1][...] = jnp.zeros((SUBLANES, LANES), F32)

    shape = lambda x: x.shape if scatter else (N_DEV,) + x.shape
    hbm = lambda x: pltpu.with_memory_space_constraint(x, pltpu.HBM)
    srcs = [hbm(x) for g in groups for x in g]
    zones = [hbm(lax.empty(shape(x), x.dtype)) for g in groups for x in g]
    sem_shapes = [pltpu.SemaphoreType.DMA((s * (N_DEV - 1),)) for s in sizes for _ in range(2)]
    out = pl.pallas_call(
        body, name=name,
        out_shape=(*sem_shapes, *[pltpu.HBM(x.shape, x.dtype) for x in srcs + zones],
                   jax.ShapeDtypeStruct((SUBLANES, LANES), F32)),
        in_specs=[_HBM] * (2 * n) + [_ANY],
        out_specs=(*[_SEM] * (2 * ng), *[_HBM] * (2 * n), pl.BlockSpec(memory_space=pltpu.VMEM)),
        input_output_aliases={i: 2 * ng + i for i in range(2 * n)},
        compiler_params=pltpu.CompilerParams(has_side_effects=_EFFECT),
    )(*srcs, *zones, after)
    bufs = out[2 * ng:-1]
    handles = [(out[2 * g], out[2 * g + 1], *bufs[first[g]:first[g] + sizes[g]],
                *bufs[n + first[g]:n + first[g] + sizes[g]]) for g in range(ng)]
    return handles, out[-1]


def _exchange_start(arrays, name, scatter=False, after=None):
    handles, token = _exchange_start_groups([arrays], name, scatter, after)
    return handles[0], token


def _exchange_wait(handle, name, scatter=False, after=None):
    send_sems, recv_sems = handle[0], handle[1]
    n = (len(handle) - 2) // 2
    after = jnp.zeros((SUBLANES, LANES), F32) if after is None else after

    def body(*refs):
        ins, lands = refs[:n], refs[n:2 * n]
        send, recv = refs[2 * n], refs[2 * n + 1]
        me = _my_pos()
        for k in range(1, N_DEV):
            for a in range(n):
                _split_copy(ins, lands, send, recv, a, k, me, scatter, False).wait_send()
                _split_copy(ins, lands, send, recv, a, k, me, scatter, True).wait_recv()

    bufs = handle[2:]
    out = pl.pallas_call(
        body, name=name,
        out_shape=tuple(pltpu.HBM(x.shape, x.dtype) for x in bufs),
        in_specs=[_HBM] * (2 * n) + [_SEM, _SEM, _ANY],
        out_specs=tuple([_HBM] * (2 * n)),
        input_output_aliases={i: i for i in range(2 * n)},
        compiler_params=pltpu.CompilerParams(has_side_effects=_EFFECT),
    )(*bufs, send_sems, recv_sems, after)
    me = _index(_my_pos())
    landed = []
    for src, land in zip(out[:n], out[n:]):
        own = lax.dynamic_slice_in_dim(src, me, 1, axis=0) if scatter else src[None]
        landed.append(lax.dynamic_update_slice_in_dim(land, own, me, axis=0))
    return landed


def _sum_blocks(x, name):
    n, rows, cols = x.shape
    whole = x.size * x.dtype.itemsize <= SUM_WHOLE_BYTES
    tr = rows if whole else _tile(rows, (512, 256, 128, 64, 32, 16, 8))

    def body(x_ref, o_ref):
        acc = x_ref[0].astype(F32)
        for d in range(1, n):
            acc = acc + x_ref[d].astype(F32)
        o_ref[...] = acc

    return pl.pallas_call(
        body, name=name,
        out_shape=jax.ShapeDtypeStruct((rows, cols), F32),
        grid=(rows // tr,),
        in_specs=[pl.BlockSpec((n, tr, cols), lambda i: (0, i, 0))],
        out_specs=pl.BlockSpec((tr, cols), lambda i: (i, 0)),
        compiler_params=_params("parallel"),
    )(x)


_DIMS = {"nn": (((1,), (0,)), ((), ())), "nt": (((1,), (1,)), ((), ())), "tn": (((0,), (0,)), ((), ()))}


def _mm(a, b, mode, name, out_dtype=F32, tm=None, tn=None, add=None, after=None, b_cols=None):
    if mode == "tn":
        kdim, m = a.shape
    else:
        m, kdim = a.shape
    n = b.shape[0] if mode == "nt" else b.shape[1]
    tm = tm or _tile(m, (512, 256, 128))
    tn = tn or _tile(n, (512, 256, 128))
    j0 = 0
    if b_cols is not None:
        j0, n = b_cols[0], b_cols[1] * tn
    dims = _DIMS[mode]

    def body(*refs):
        refs = refs if after is None else refs[1:]
        acc = lax.dot_general(refs[0][...].astype(BF16), refs[1][...].astype(BF16), dims,
                              preferred_element_type=F32)
        if add is not None:
            acc = acc + refs[2][...]
        refs[-1][...] = acc.astype(out_dtype)

    a_spec = (pl.BlockSpec((kdim, tm), lambda i, j: (0, i)) if mode == "tn"
              else pl.BlockSpec((tm, kdim), lambda i, j: (i, 0)))
    b_spec = (pl.BlockSpec((tn, kdim), lambda i, j: (j, 0)) if mode == "nt"
              else pl.BlockSpec((kdim, tn), lambda i, j: (0, j + j0)))
    o_spec = pl.BlockSpec((tm, tn), lambda i, j: (i, j))
    in_specs, args = [a_spec, b_spec], [a, b]
    if add is not None:
        in_specs.append(o_spec)
        args.append(add)
    if after is not None:
        in_specs.insert(0, _ANY)
        args.insert(0, after)
    return pl.pallas_call(
        body, name=name, out_shape=jax.ShapeDtypeStruct((m, n), out_dtype), grid=(m // tm, n // tn),
        in_specs=in_specs, out_specs=o_spec,
        compiler_params=_params("parallel", "parallel"),
    )(*args)


def _rowmap(name, fn, seq, rows, bats=(), vecs=(), row_outs=(), bat_outs=(), vec_outs=(), ts=None, mm=None, lhs=None):
    rows = [r if isinstance(r, tuple) else (r, r.shape[1], 0) for r in rows]
    tokens = rows[0][0].shape[0]
    nseq = tokens // seq
    ts = ts or _tile(seq, (256, 128, 64, 32, 16, 8))
    nt = seq // ts
    n_r, n_b, n_v = len(rows), len(bats), len(vecs)
    n_ro, n_bo = len(row_outs), len(bat_outs)

    def accumulate(ref, val, first):
        @pl.when(first)
        def _():
            ref[...] = val.reshape(ref.shape)

        @pl.when(jnp.logical_not(first))
        def _():
            ref[...] += val.reshape(ref.shape)

    def body(*refs):
        n_in = n_r + n_b + n_v + (mm is not None)
        ins, outs = refs[:n_in], refs[n_in:]
        b_vals = [r[0] for r in ins[n_r:n_r + n_b]]
        v_vals = [r[...] for r in ins[n_r + n_b:n_r + n_b + n_v]]
        r_vals = [r[...] for r in ins[:n_r]]
        if mm is not None:
            left = r_vals[0] if lhs is None else lhs(r_vals)
            acc = lax.dot_general(left.astype(BF16), ins[-1][...].astype(BF16), _DIMS[mm[1]],
                                  preferred_element_type=F32)
            r_vals = [acc] + r_vals[1:] if lhs is None else [acc, left] + r_vals
        ro, bo, vo = fn(r_vals, b_vals, v_vals)
        for ref, val in zip(outs[:n_ro], ro):
            ref[...] = val.astype(ref.dtype)
        b, i = pl.program_id(0), pl.program_id(1)
        for ref, val in zip(outs[n_ro:n_ro + n_bo], bo):
            accumulate(ref, val, i == 0)
        for ref, val in zip(outs[n_ro + n_bo:], vo):
            accumulate(ref, val, jnp.logical_and(i == 0, b == 0))

    in_specs = [pl.BlockSpec((ts, w), functools.partial(lambda b, i, cb: (b * nt + i, cb), cb=cb))
                for _, w, cb in rows]
    in_specs += [pl.BlockSpec((1, 1, v.shape[2]), lambda b, i: (b, 0, 0)) for v in bats]
    in_specs += [pl.BlockSpec((1, v.shape[1]), lambda b, i: (0, 0)) for v in vecs]
    extra = []
    if mm is not None:
        in_specs.append(pl.BlockSpec(mm[0].shape, lambda b, i: (0, 0)))
        extra.append(mm[0])
    out_shape = [jax.ShapeDtypeStruct((tokens, f), dt) for f, dt in row_outs]
    out_specs = [pl.BlockSpec((ts, f), lambda b, i: (b * nt + i, 0)) for f, _ in row_outs]
    out_shape += [jax.ShapeDtypeStruct((nseq, 1, f), F32) for f in bat_outs]
    out_specs += [pl.BlockSpec((1, 1, f), lambda b, i: (b, 0, 0)) for f in bat_outs]
    out_shape += [jax.ShapeDtypeStruct((1, f), F32) for f in vec_outs]
    out_specs += [pl.BlockSpec((1, f), lambda b, i: (0, 0)) for f in vec_outs]
    return pl.pallas_call(
        body, name=name, out_shape=tuple(out_shape), grid=(nseq, nt),
        in_specs=in_specs, out_specs=tuple(out_specs),
        compiler_params=_params("arbitrary", "arbitrary"),
    )(*([r[0] for r in rows] + list(bats) + list(vecs) + extra))


def _colsum(v):
    return jnp.sum(v, axis=0, keepdims=True)


def _rstd(x, width=None):
    width = width or x.shape[-1]
    return lax.rsqrt(jnp.sum(x * x, axis=-1, keepdims=True) * (1.0 / width) + NORM_EPS)


def _norm_bwd(dy, x, r, g, width=None):
    width = width or x.shape[-1]
    xhat = x * r
    dxhat = dy * g
    dx = r * (dxhat - xhat * (jnp.sum(dxhat * xhat, axis=-1, keepdims=True) * (1.0 / width)))
    return dx, dy * xhat


def _sigmoid(x):
    return 0.5 * jnp.tanh(0.5 * x) + 0.5


def _norm_mod(xv, g, sh, sc):
    return xv * _rstd(xv) * g * (1.0 + sc) + sh


def _norm_mod_fwd(x, p, seq, name):
    def fn(rows, bats, vecs):
        return [_norm_mod(rows[0], vecs[0], bats[0], bats[1])], [], []
    return _rowmap(name, fn, seq, [x], [p["shift"], p["scale"]], [p["gamma"]], row_outs=[(D_MODEL, BF16)])[0]


def _norm_mod_bwd(dh, x, dres, p, seq, name, prev=None):
    def fn(rows, bats, vecs):
        dhv, xv, dr = rows[:3]
        sc, g = bats[0], vecs[0]
        r = _rstd(xv)
        dxn, dg = _norm_bwd(dhv * (1.0 + sc), xv, r, g)
        dx = dr + dxn
        ro, bo = [dx], [_colsum(dhv), _colsum(dhv * (xv * r * g))]
        if prev is not None:
            ro.append(bats[1] * dx)
            bo.append(_colsum(dx * rows[3].astype(F32)))
        return ro, bo, [_colsum(dg)]
    more = prev is not None
    return _rowmap(name, fn, seq, [dh, x, dres] + ([prev[0]] if more else []),
                   [p["scale"]] + ([prev[1]] if more else []), [p["gamma"]],
                   row_outs=[(D_MODEL, F32)] + ([(D_MODEL, BF16)] if more else []),
                   bat_outs=[D_MODEL] * (3 if more else 2), vec_outs=[D_MODEL])


def _ffn_in_act(h, wt_in, name):
    tokens = h.shape[0]
    tm, tn = _tile(tokens, (2048, 1024, 512)), 256
    nj = D_FF // tn

    def body(h_ref, wg_ref, wu_ref, g_ref, u_ref, a_ref):
        hv = h_ref[...]
        g = lax.dot_general(hv, wg_ref[...], _DIMS["nt"], preferred_element_type=F32)
        u = lax.dot_general(hv, wu_ref[...], _DIMS["nt"], preferred_element_type=F32)
        g_ref[...] = g.astype(BF16)
        u_ref[...] = u.astype(BF16)
        a_ref[...] = (g * _sigmoid(g) * u).astype(BF16)

    o_spec = pl.BlockSpec((tm, tn), lambda i, j: (i, j))
    out = jax.ShapeDtypeStruct((tokens, D_FF), BF16)
    return pl.pallas_call(
        body, name=name, grid=(tokens // tm, nj), out_shape=(out, out, out),
        in_specs=[pl.BlockSpec((tm, D_MODEL), lambda i, j: (i, 0)),
                  pl.BlockSpec((tn, D_MODEL), lambda i, j: (j, 0)),
                  pl.BlockSpec((tn, D_MODEL), lambda i, j: (j + nj, 0))],
        out_specs=(o_spec, o_spec, o_spec),
        compiler_params=_params("parallel", "parallel"),
    )(h, wt_in, wt_in)


def _out_residual(a, w_out, res, gate, nxt, seq, name, lhs=None):
    def fn(rows, bats, vecs):
        acc, rv = rows[0], rows[-1]
        x_new = rv + bats[0] * acc
        made = [] if lhs is None else [rows[1]]
        return [x_new, acc, _norm_mod(x_new, vecs[0], bats[1], bats[2])] + made, [], []
    outs = [(D_MODEL, F32), (D_MODEL, BF16), (D_MODEL, BF16)] + ([] if lhs is None else [(D_MODEL, BF16)])
    return _rowmap(name, fn, seq, (a if lhs is not None else [a]) + [res], [gate, nxt["shift"], nxt["scale"]],
                   [nxt["gamma"]], row_outs=outs, ts=_tile(seq, (512, 256, 128)), mm=(w_out, "nn"), lhs=lhs)


def _out_loss(a, w_out, res, gate, target, seq, name):
    def fn(rows, bats, vecs):
        acc, rv, tv = rows
        err = rv + bats[0] * acc - tv
        dy = err * (1.0 / D_MODEL)
        return [dy, bats[0] * dy], [_colsum(dy * acc)], [_colsum(err * err)]
    return _rowmap(name, fn, seq, [a, res, target], [gate], row_outs=[(D_MODEL, F32), (D_MODEL, BF16)],
                   bat_outs=[D_MODEL], vec_outs=[D_MODEL], ts=_tile(seq, (512, 256, 128)), mm=(w_out, "nn"))


def _ffn_bwd_x(df, dres, saved, p, seq, tag, prev=None, mid=None):
    x, h, g, u, a, w_in, w_out = saved
    tokens = x.shape[0]

    def act_bwd(rows, bats, vecs):
        dav, gv, uv = rows[0], rows[1].astype(F32), rows[2].astype(F32)
        sg = _sigmoid(gv)
        silu = gv * sg
        dg = dav * uv * (sg * (1.0 + gv * (1.0 - sg)))
        return [jnp.concatenate([dg, dav * silu], axis=1)], [], []
    dgu = _rowmap(f"{tag}_bwd_da", act_bwd, seq, [df, g, u], row_outs=[(2 * D_FF, BF16)],
                  ts=_tile(seq, (256, 128)), mm=(w_out, "nt"))[0]
    operands = (a, df, dgu, h)
    after = None if mid is None else mid(operands)
    dh = _mm(dgu, w_in, "nn", f"{tag}_bwd_dh", tm=_tile(tokens, (512, 256)), tn=D_MODEL, after=after)
    return _norm_mod_bwd(dh, x, dres, p, seq, f"{tag}_bwd_norm", prev), operands


def _ffn_bwd_wout(operands, tag):
    a, df, _, _ = operands
    return _mm(a, df, "tn", f"{tag}_bwd_wout", out_dtype=BF16, tm=256, tn=D_MODEL)


def _ffn_bwd_win(operands, tag, after=None, half=None):
    _, _, dgu, h = operands
    if half is None:
        return _mm(dgu, h, "tn", f"{tag}_bwd_win", out_dtype=BF16, tm=512, tn=D_MODEL, after=after)
    return _mm(dgu, h, "tn", f"{tag}_bwd_win{half}", out_dtype=BF16, tm=512, tn=D_MODEL // 2, after=after,
               b_cols=(half, 1))


def _shift_rows(v, k, forward):
    n = v.shape[0]
    row = lax.broadcasted_iota(jnp.int32, v.shape, 0)
    if forward:
        return jnp.where(row >= k, pltpu.roll(v, k, 0), 0.0)
    return jnp.where(row < n - k, pltpu.roll(v, n - k, 0), 0.0)


def _window_sums(v, forward):
    out, s, k = [], v, 1
    for _ in range(POOL_GROUPS):
        s = s + _shift_rows(s, k, forward)
        out.append(s)
        k *= 2
    return out


def _by_group(vals, g):
    out = vals[-1]
    for idx in range(len(vals) - 2, -1, -1):
        out = jnp.where(g == idx, vals[idx], out)
    return out


def _inv_count(shape, g):
    t1 = lax.broadcasted_iota(jnp.int32, shape, 0) + 1
    window = _by_group([jnp.int32(2 ** (i + 1)) for i in range(POOL_GROUPS)], g)
    return 1.0 / jnp.minimum(t1, window).astype(F32)


def _pool_fwd(u, grp, scale, seq):
    tokens = u.shape[0]

    def body(u_ref, grp_ref, sc_ref, pooled_ref, pg_ref, ps_ref):
        g = pl.program_id(1)
        uv = u_ref[...]
        sums = _by_group(_window_sums(uv, True), g)
        pooled = (sums * _inv_count(uv.shape, g) - uv).astype(BF16)
        pg = jnp.dot(pooled, grp_ref[0].astype(BF16), preferred_element_type=F32)
        pooled_ref[...] = pooled
        pg_ref[...] = pg
        ps_ref[...] = (pg * sc_ref[...]).astype(BF16)

    blk = pl.BlockSpec((seq, POOL_GROUP_DIM), lambda b, g: (b, g))
    return pl.pallas_call(
        body, name="pool_fwd", grid=(tokens // seq, POOL_GROUPS),
        out_shape=(jax.ShapeDtypeStruct(u.shape, BF16), jax.ShapeDtypeStruct(u.shape, F32),
                   jax.ShapeDtypeStruct(u.shape, BF16)),
        in_specs=[blk, pl.BlockSpec((1, POOL_GROUP_DIM, POOL_GROUP_DIM), lambda b, g: (g, 0, 0)),
                  pl.BlockSpec((1, POOL_GROUP_DIM), lambda b, g: (0, g))],
        out_specs=(blk, blk, blk),
        compiler_params=_params("parallel", "parallel"),
    )(u, grp, scale)


def _pool_bwd(dps, pooled, pg, grp, scale, seq):
    tokens = dps.shape[0]

    def body(dps_ref, pooled_ref, pg_ref, grp_ref, sc_ref, du_ref, dgrp_ref, dsc_ref):
        g, b = pl.program_id(0), pl.program_id(1)
        dpsv = dps_ref[...]
        dpg = (dpsv * sc_ref[...]).astype(BF16)
        dsc = _colsum(dpsv * pg_ref[...])
        dgrp = lax.dot_general(pooled_ref[...], dpg, _DIMS["tn"], preferred_element_type=F32)

        @pl.when(b == 0)
        def _():
            dsc_ref[...] = dsc
            dgrp_ref[0] = dgrp

        @pl.when(b > 0)
        def _():
            dsc_ref[...] += dsc
            dgrp_ref[0] += dgrp

        dpool = lax.dot_general(dpg, grp_ref[0].astype(BF16), _DIMS["nt"], preferred_element_type=F32)
        sums = _by_group(_window_sums(dpool * _inv_count(dpool.shape, g), False), g)
        du_ref[...] = (sums - dpool).astype(BF16)

    blk = pl.BlockSpec((seq, POOL_GROUP_DIM), lambda g, b: (b, g))
    grp_spec = pl.BlockSpec((1, POOL_GROUP_DIM, POOL_GROUP_DIM), lambda g, b: (g, 0, 0))
    vec_spec = pl.BlockSpec((1, POOL_GROUP_DIM), lambda g, b: (0, g))
    return pl.pallas_call(
        body, name="pool_bwd", grid=(POOL_GROUPS, tokens // seq),
        out_shape=(jax.ShapeDtypeStruct(dps.shape, BF16), jax.ShapeDtypeStruct(grp.shape, F32),
                   jax.ShapeDtypeStruct(scale.shape, F32)),
        in_specs=[blk, blk, blk, grp_spec, vec_spec],
        out_specs=(blk, grp_spec, vec_spec),
        compiler_params=_params("arbitrary", "arbitrary"),
    )(dps, pooled, pg, grp, scale)


def _lane(shape):
    return lax.broadcasted_iota(jnp.int32, shape, len(shape) - 1)


def _rot(y):
    lane = _lane(y.shape)
    r = jnp.where(lane < QK_NOPE + QK_ROPE // 2,
                  -pltpu.roll(y, HEAD_SLAB - QK_ROPE // 2, 1), pltpu.roll(y, QK_ROPE // 2, 1))
    return jnp.where(jnp.logical_and(lane >= QK_NOPE, lane < QK_NOPE + QK_ROPE), r, 0.0)


def _part_rstd(x):
    sq = x * x
    nope = _lane(x.shape) < QK_NOPE
    s_nope = jnp.sum(jnp.where(nope, sq, 0.0), axis=-1, keepdims=True)
    s_rope = jnp.sum(sq, axis=-1, keepdims=True) - s_nope
    return jnp.where(nope, lax.rsqrt(s_nope * (1.0 / QK_NOPE) + NORM_EPS),
                     lax.rsqrt(s_rope * (1.0 / QK_ROPE) + NORM_EPS))


def _part_norm_bwd(dy, x, r, g):
    nope = _lane(x.shape) < QK_NOPE
    xhat = x * r
    dxhat = dy * g
    prod = dxhat * xhat
    m_nope = jnp.sum(jnp.where(nope, prod, 0.0), axis=-1, keepdims=True)
    m_rope = jnp.sum(prod, axis=-1, keepdims=True) - m_nope
    mean = jnp.where(nope, m_nope * (1.0 / QK_NOPE), m_rope * (1.0 / QK_ROPE))
    return r * (dxhat - xhat * mean), dy * xhat


def _latent_norm_fwd(z_a, g_q, g_kv, seq):
    def fn(rows, bats, vecs):
        q, kv = rows[0][:, :Q_LORA], rows[0][:, Q_LORA:Q_LORA + KV_LORA]
        return [q * _rstd(q) * vecs[0], kv * _rstd(kv) * vecs[1]], [], []
    return _rowmap("latent_norm", fn, seq, [z_a], vecs=[g_q, g_kv],
                   row_outs=[(Q_LORA, BF16), (KV_LORA, BF16)])


def _latent_norm_bwd(dqn, dkvn, dkr, z_a, g_q, g_kv, seq):
    def fn(rows, bats, vecs):
        dq, dkv, dkrv, z = rows
        q, kv = z[:, :Q_LORA], z[:, Q_LORA:Q_LORA + KV_LORA]
        dxq, dgq = _norm_bwd(dq, q, _rstd(q), vecs[0])
        dxkv, dgkv = _norm_bwd(dkv, kv, _rstd(kv), vecs[1])
        return [jnp.concatenate([dxq, dxkv, dkrv], axis=1)], [], [_colsum(dgq), _colsum(dgkv)]
    return _rowmap("latent_norm_bwd", fn, seq, [dqn, dkvn, dkr, z_a], vecs=[g_q, g_kv],
                   row_outs=[(Q_LORA + KV_LORA + HEAD_SLAB, BF16)], vec_outs=[Q_LORA, KV_LORA])


def _qk_prep_fwd(qp, kv, z_a, pos, g_q, g_kn, g_kr, inv_freq, seq):
    def fn(rows, bats, vecs):
        qv, kvv, kr, p = rows
        gq, gkn, gkr, invf = vecs
        ang = p * invf
        cos, sin = jnp.cos(ang), jnp.sin(ang)
        nope = _lane(kr.shape) < QK_NOPE
        krn = kr * _rstd(kr, QK_ROPE) * gkr
        krr = krn * cos + _rot(krn) * sin
        qs, ks, vs = [], [], []
        for h in range(N_HEADS):
            xq = qv[:, h * HEAD_SLAB:(h + 1) * HEAD_SLAB]
            y = xq * _part_rstd(xq) * gq
            qs.append(y * cos + _rot(y) * sin)
            xk = kvv[:, h * HEAD_SLAB:(h + 1) * HEAD_SLAB]
            kn = jnp.where(nope, xk, 0.0)
            ks.append(jnp.where(nope, kn * _rstd(kn, QK_NOPE) * gkn, krr))
            vs.append(jnp.where(nope, 0.0, xk))
        return [jnp.concatenate(v, axis=1) for v in (qs, ks, vs)], [], []
    width = N_HEADS * HEAD_SLAB
    return _rowmap("qk_prep", fn, seq, [qp, kv, (z_a, HEAD_SLAB, 5), pos], vecs=[g_q, g_kn, g_kr, inv_freq],
                   row_outs=[(width, BF16)] * 3, ts=_tile(seq, (128, 64, 32, 16, 8)))


def _qk_prep_bwd(dqc, dkc, dvp, qp, kv, z_a, pos, g_q, g_kn, g_kr, inv_freq, seq):
    def fn(rows, bats, vecs):
        dq, dk, dv, qv, kvv, kr, p = rows
        gq, gkn, gkr, invf = vecs
        ang = p * invf
        cos, sin = jnp.cos(ang), jnp.sin(ang)
        nope = _lane(kr.shape) < QK_NOPE
        dqs, dkvs = [], []
        dgq = jnp.zeros((1, HEAD_SLAB), F32)
        dgkn = jnp.zeros((1, HEAD_SLAB), F32)
        dkrr = jnp.zeros(kr.shape, F32)
        for h in range(N_HEADS):
            sl = slice(h * HEAD_SLAB, (h + 1) * HEAD_SLAB)
            dyr = dq[:, sl]
            dy = dyr * cos - _rot(dyr * sin)
            xq = qv[:, sl]
            dx, dg = _part_norm_bwd(dy, xq, _part_rstd(xq), gq)
            dqs.append(dx)
            dgq = dgq + _colsum(dg)
            dkh = dk[:, sl]
            dkrr = dkrr + jnp.where(nope, 0.0, dkh)
            kn = jnp.where(nope, kvv[:, sl], 0.0)
            dxk, dgk = _norm_bwd(jnp.where(nope, dkh, 0.0), kn, _rstd(kn, QK_NOPE), gkn, QK_NOPE)
            dgkn = dgkn + _colsum(dgk)
            dkvs.append(jnp.where(nope, dxk, dv[:, sl]))
        dkrn = dkrr * cos - _rot(dkrr * sin)
        dkr, dgkr = _norm_bwd(dkrn, kr, _rstd(kr, QK_ROPE), gkr, QK_ROPE)
        return ([jnp.concatenate(dqs, axis=1), jnp.concatenate(dkvs, axis=1), dkr], [],
                [dgq, dgkn, _colsum(dgkr)])
    width = N_HEADS * HEAD_SLAB
    return _rowmap("qk_prep_bwd", fn, seq, [dqc, dkc, dvp, qp, kv, (z_a, HEAD_SLAB, 5), pos],
                   vecs=[g_q, g_kn, g_kr, inv_freq],
                   row_outs=[(width, BF16), (width, BF16), (HEAD_SLAB, F32)],
                   vec_outs=[HEAD_SLAB] * 3, ts=_tile(seq, (128, 64, 32, 16, 8)))


def _scores(q, k_ref, keys, tq):
    s = lax.dot_general(q, k_ref[0:keys, :], _DIMS["nt"], preferred_element_type=F32) * ATTN_SCALE
    row = lax.broadcasted_iota(jnp.int32, (tq, tq), 0)
    col = lax.broadcasted_iota(jnp.int32, (tq, tq), 1)
    diag = jnp.where(col <= row, s[:, keys - tq:], -1e30)
    return diag if keys == tq else jnp.concatenate([s[:, :keys - tq], diag], axis=1)


def _attn_fwd(qc, kc, vp, seq):
    tokens = qc.shape[0]
    tq = _tile(seq, (256, 128))
    nq = seq // tq

    def body(q_ref, k_ref, v_ref, o_ref, lse_ref):
        for i in range(nq):
            rows, keys = slice(i * tq, (i + 1) * tq), (i + 1) * tq
            s = _scores(q_ref[rows, :], k_ref, keys, tq)
            m = jnp.max(s, axis=-1, keepdims=True)
            p = jnp.exp(s - m)
            l = jnp.sum(p, axis=-1, keepdims=True)
            acc = jnp.dot(p.astype(BF16), v_ref[0:keys, :], preferred_element_type=F32)
            o_ref[rows, :] = (acc / l).astype(BF16)
            lse_ref[rows, :] = jnp.broadcast_to(m + jnp.log(l), (tq, HEAD_SLAB))

    spec = pl.BlockSpec((seq, HEAD_SLAB), lambda b, h: (b, h))
    return pl.pallas_call(
        body, name="attn_fwd", grid=(tokens // seq, N_HEADS),
        out_shape=(jax.ShapeDtypeStruct(qc.shape, BF16), jax.ShapeDtypeStruct(qc.shape, F32)),
        in_specs=[spec] * 3, out_specs=(spec, spec),
        compiler_params=_params("parallel", "parallel"),
    )(qc, kc, vp)


def _attn_bwd(qc, kc, vp, o, lse, do, seq):
    tokens = qc.shape[0]
    tq = _tile(seq, (256, 128))
    nq = seq // tq

    def body(q_ref, k_ref, v_ref, o_ref, lse_ref, do_ref, dq_ref, dk_ref, dv_ref):
        dk_ref[...] = jnp.zeros(dk_ref.shape, F32)
        dv_ref[...] = jnp.zeros(dv_ref.shape, F32)
        for i in range(nq):
            rows, keys = slice(i * tq, (i + 1) * tq), (i + 1) * tq
            q, dov = q_ref[rows, :], do_ref[rows, :]
            delta = jnp.sum(dov.astype(F32) * o_ref[rows, :].astype(F32), axis=-1, keepdims=True)
            s = _scores(q, k_ref, keys, tq)
            p = jnp.exp(s - jnp.tile(lse_ref[rows, :], (1, keys // HEAD_SLAB)))
            dp = lax.dot_general(dov, v_ref[0:keys, :], _DIMS["nt"], preferred_element_type=F32)
            ds = (p * (dp - delta) * ATTN_SCALE).astype(BF16)
            dq_ref[rows, :] = jnp.dot(ds, k_ref[0:keys, :], preferred_element_type=F32)
            dk_ref[0:keys, :] += lax.dot_general(ds, q, _DIMS["tn"], preferred_element_type=F32)
            dv_ref[0:keys, :] += lax.dot_general(p.astype(BF16), dov, _DIMS["tn"], preferred_element_type=F32)

    spec = pl.BlockSpec((seq, HEAD_SLAB), lambda b, h: (b, h))
    out = jax.ShapeDtypeStruct(qc.shape, F32)
    return pl.pallas_call(
        body, name="attn_bwd", grid=(tokens // seq, N_HEADS),
        out_shape=(out, out, out), in_specs=[spec] * 6, out_specs=(spec, spec, spec),
        compiler_params=_params("parallel", "parallel"),
    )(qc, kc, vp, o, lse, do)


def _adamw(w, g, m, v, name):
    rows, cols = w.shape
    whole = rows * cols * 4 <= ADAMW_WHOLE_BYTES
    tr = rows if whole else _tile(rows, (256, 128, 64, 32, 16, 8))
    c1 = 1.0 - ADAM_B1 ** ADAM_STEP
    c2 = 1.0 - ADAM_B2 ** ADAM_STEP

    def body(w_ref, g_ref, m_ref, v_ref, d_ref, nm_ref, nv_ref):
        gv = g_ref[...]
        nm = ADAM_B1 * m_ref[...] + (1.0 - ADAM_B1) * gv
        nv = ADAM_B2 * v_ref[...] + (1.0 - ADAM_B2) * (gv * gv)
        d_ref[...] = -ADAM_LR * ((nm / c1) / (jnp.sqrt(nv / c2) + ADAM_EPS) + ADAM_WD * w_ref[...])
        nm_ref[...] = nm
        nv_ref[...] = nv

    spec = pl.BlockSpec((tr, cols), lambda i: (i, 0))
    out = jax.ShapeDtypeStruct(w.shape, F32)
    return pl.pallas_call(
        body, name=name, grid=(rows // tr,), out_shape=(out, out, out),
        in_specs=[spec] * 4, out_specs=(spec, spec, spec),
        compiler_params=_params("parallel"),
    )(w, g, m, v)


def _adamw_landed(w, landed, m, v, name):
    rows, cols = w.shape
    tr = _tile(rows, (176, 128, 96, 64, 32, 16, 8))
    c1 = 1.0 - ADAM_B1 ** ADAM_STEP
    c2 = 1.0 - ADAM_B2 ** ADAM_STEP
    n_parts = len(landed)

    def body(*refs):
        w_ref, m_ref, v_ref = refs[:3]
        g_ref, d_ref, nm_ref, nv_ref = refs[3 + n_parts:]
        parts = []
        for x_ref in refs[3:3 + n_parts]:
            acc = x_ref[0].astype(F32)
            for d in range(1, N_DEV):
                acc = acc + x_ref[d].astype(F32)
            parts.append(acc)
        gv = parts[0] if n_parts == 1 else jnp.concatenate(parts, axis=1)
        nm = ADAM_B1 * m_ref[...] + (1.0 - ADAM_B1) * gv
        nv = ADAM_B2 * v_ref[...] + (1.0 - ADAM_B2) * (gv * gv)
        g_ref[...] = gv
        d_ref[...] = -ADAM_LR * ((nm / c1) / (jnp.sqrt(nv / c2) + ADAM_EPS) + ADAM_WD * w_ref[...])
        nm_ref[...] = nm
        nv_ref[...] = nv

    spec = pl.BlockSpec((tr, cols), lambda i: (i, 0))
    out = jax.ShapeDtypeStruct(w.shape, F32)
    return pl.pallas_call(
        body, name=name, grid=(rows // tr,), out_shape=(out, out, out, out),
        in_specs=[spec] * 3 + [pl.BlockSpec((N_DEV, tr, x.shape[2]), lambda i: (0, i, 0)) for x in landed],
        out_specs=(spec, spec, spec, spec),
        compiler_params=_params("parallel"),
    )(w, m, v, *landed)


def _mod_cols(c_all, w_ada, b_cols):
    def body(c_ref, w_ref, b_ref, act_ref, mod_ref):
        cv = c_ref[...]
        act = cv * _sigmoid(cv)
        act_ref[...] = act
        mod_ref[...] = jnp.dot(act.astype(BF16), w_ref[...].astype(BF16),
                               preferred_element_type=F32) + b_ref[...]

    n = w_ada.shape[1]
    return pl.pallas_call(
        body, name="mod_cols",
        out_shape=(jax.ShapeDtypeStruct(c_all.shape, F32), jax.ShapeDtypeStruct((c_all.shape[0], n), F32)),
        compiler_params=pltpu.CompilerParams(vmem_limit_bytes=VMEM_LIMIT),
    )(c_all, w_ada, b_cols)


def _ada_grads(c_act, dmod_all, dmod_cols):
    def body(c_ref, d_ref, dc_ref, gw_ref, gb_ref):
        gw_ref[...] = lax.dot_general(c_ref[...].astype(BF16), dc_ref[...].astype(BF16), _DIMS["tn"],
                                      preferred_element_type=F32)
        gb_ref[...] = _colsum(d_ref[...])

    return pl.pallas_call(
        body, name="ada_grads",
        out_shape=(jax.ShapeDtypeStruct((c_act.shape[1], dmod_cols.shape[1]), F32),
                   jax.ShapeDtypeStruct((1, dmod_all.shape[1]), F32)),
        compiler_params=pltpu.CompilerParams(vmem_limit_bytes=VMEM_LIMIT),
    )(c_act, dmod_all, dmod_cols)


def _flat_rows(a):
    flat = a.reshape(-1)
    pad = (-flat.shape[0]) % (LANES * SUBLANES)
    if pad:
        flat = jnp.pad(flat, (0, pad))
    return flat.reshape(-1, LANES)


def _gather_start(w, groups, tag, after=None):
    shards = [[(w[n] if n in ROW_SHARDED else w[n].T).astype(BF16) for n in names] for names in groups]
    return _exchange_start_groups(shards, f"gather_{tag}_start", after=after)


def _gather_wait(handle, names, tag, after):
    landed = _exchange_wait(handle, f"gather_{tag}_wait", after=after)
    return {n: g.reshape(-1, g.shape[2]) for n, g in zip(names, landed)}


def _scatter_start(grads, names, tag, after=None):
    blocks = [grads[n].reshape(N_DEV, -1, grads[n].shape[1]) for n in names]
    return _exchange_start(blocks, f"scatter_{tag}_start", scatter=True, after=after)


def _scatter_wait(handle, names, tag, after):
    landed = _exchange_wait(handle, f"scatter_{tag}_wait", scatter=True, after=after)
    return {n: [x] for n, x in zip(names, landed)}


def _pack_small(vals):
    return jnp.concatenate([_flat_rows(v.astype(F32)) for v in vals], axis=0)


def _unpack_small(packed, like):
    out, row = [], 0
    for v in like:
        rows = _flat_rows(v).shape[0]
        out.append(packed[row:row + rows].reshape(-1)[:v.size].reshape(v.shape))
        row += rows
    return out


def _lanes128(*parts):
    out = jnp.zeros((HEAD_SLAB,), F32)
    for off, v in parts:
        out = lax.dynamic_update_slice(out, v.reshape(-1).astype(F32), (off,))
    return out.reshape(1, HEAD_SLAB)


def _step(x, c, positions, w, m, v, loss_target):
    nseq, seq, _ = x.shape
    tokens = nseq * seq
    me = _index(_my_pos())
    strip = lambda d: {n: (a[0] if a.ndim > 2 else a) for n, a in d.items()}
    shapes = {n: a.shape for n, a in w.items()}
    w, m, v = strip(w), strip(m), strip(v)

    c_all = _all_gather(c.reshape(-1, LANES), "gather_c").reshape(N_DEV * nseq, D_MODEL)
    n_ada = w["w_ada"].shape[1]
    b_cols = lax.dynamic_slice(w["b_ada"], (0, me * n_ada), (1, n_ada))
    c_act, mod_cols = _mod_cols(c_all, w["w_ada"], b_cols)
    mod_all = _all_gather(mod_cols, "gather_mod")
    mod = lax.dynamic_slice(mod_all, (0, me * nseq, 0), (N_DEV, nseq, n_ada))
    mod = mod.transpose(1, 0, 2).reshape(nseq, 3, 3, 1, D_MODEL)

    (h_f1i, h_f1o, h_mix_in, h_mix, h_f2), tok = _gather_start(
        w, (("w_ffn1_in",), ("w_ffn1_out",), MIXER[:1], MIXER[1:], ("w_ffn2_in", "w_ffn2_out")), "weights",
        after=mod_all)
    started = tok[0:1, 0:1]

    g_q = _lanes128((0, w["q_norm_nope"]), (QK_NOPE, w["q_norm_rope"]))
    g_kn = _lanes128((0, w["k_norm_nope"]))
    g_kr = _lanes128((QK_NOPE, w["k_norm_rope"]))
    freq = ROPE_THETA ** (-jnp.arange(0, QK_ROPE, 2, dtype=F32) / QK_ROPE)
    inv_freq = _lanes128((QK_NOPE, jnp.concatenate([freq, freq])))
    pos = positions.reshape(tokens, 1).astype(F32)

    def sub(k, gamma, coef):
        return dict(gamma=w[gamma], shift=mod[:, k, 0] + started, scale=mod[:, k, 1], gate=coef * mod[:, k, 2])
    p1, pm, p2 = sub(0, "norm_ffn1", 0.5), sub(1, "norm_mix", 1.0), sub(2, "norm_ffn2", 0.5)
    t_big = _tile(tokens, (2048, 1024, 512))
    t_mid = _tile(tokens, (1024, 512))

    x0 = x.reshape(tokens, D_MODEL)
    h1 = _norm_mod_fwd(x0, p1, seq, "ffn1_norm")
    wt_f1i = _gather_wait(h_f1i, ("w_ffn1_in",), "ffn1_in", h1)["w_ffn1_in"]
    g1, u1, a1 = _ffn_in_act(h1, wt_f1i, "ffn1_in")
    w_f1o = _gather_wait(h_f1o, ("w_ffn1_out",), "ffn1_out", a1)["w_ffn1_out"]
    x1, f1, h2 = _out_residual(a1, w_f1o, x0, p1["gate"], pm, seq, "ffn1_out")
    saved1 = (x0, h1, g1, u1, a1, wt_f1i, w_f1o)

    wt_in = _gather_wait(h_mix_in, MIXER[:1], "mix_in", h2)["w_in"]
    zero_rows = lambda rows: jnp.zeros((rows, D_MODEL), BF16)
    wt_p = wt_in[:512]
    wt_a = jnp.concatenate([wt_in[512:1152], zero_rows(QK_NOPE), wt_in[1152:1184], zero_rows(32)], axis=0)
    wt_g = wt_in[1184:]
    z_a = _mm(h2, wt_a, "nt", "mix_in_a", tm=t_big, tn=wt_a.shape[0])
    z_p = _mm(h2, wt_p, "nt", "mix_in_p", tm=t_big, tn=512)
    z_g = _mm(h2, wt_g, "nt", "mix_in_g", tm=t_big, tn=512)

    full = _gather_wait(h_mix, MIXER[1:], "mix", z_g)
    wtq_pad = jnp.pad(full["w_q_up"].reshape(N_HEADS, 96, Q_LORA), ((0, 0), (0, 32), (0, 0))).reshape(-1, Q_LORA)
    wtmla_pad = jnp.pad(full["w_mla_proj"].reshape(D_MODEL, N_HEADS, 64), ((0, 0), (0, 0), (64, 0))).reshape(D_MODEL, -1)
    wt_pool, wt_kv, w_mix_out = full["w_pool_proj"], full["w_kv_up"], full["w_out"]
    pooled, pg, ps = _pool_fwd(z_p, w["pool_grp"], w["pool_scale"], seq)
    br_pool = _mm(ps, wt_pool, "nt", "pool_proj", tm=t_big, tn=D_MODEL)
    qn, kvn = _latent_norm_fwd(z_a, w["q_a_norm"], w["kv_a_norm"], seq)
    qp = _mm(qn, wtq_pad, "nt", "q_up", tm=t_big, tn=D_MODEL)
    kv = _mm(kvn, wt_kv, "nt", "kv_up", tm=t_big, tn=D_MODEL)
    qc, kc, vp = _qk_prep_fwd(qp, kv, z_a, pos, g_q, g_kn, g_kr, inv_freq, seq)
    attn, lse = _attn_fwd(qc, kc, vp, seq)
    br_mla = _mm(attn, wtmla_pad, "nt", "mla_proj", tm=t_mid, tn=D_MODEL)

    def merge(rows):
        zg, bp, bm = rows[:3]
        return (_sigmoid(zg[:, :D_MODEL]) * bp + _sigmoid(zg[:, D_MODEL:]) * bm).astype(BF16)
    x2, o_mix, h3, merged = _out_residual([z_g, br_pool, br_mla], w_mix_out, x1, pm["gate"], p2, seq, "mix_out",
                                          lhs=merge)

    ffn2_w = _gather_wait(h_f2, ("w_ffn2_in", "w_ffn2_out"), "ffn2", h3)
    g2, u2, a2 = _ffn_in_act(h3, ffn2_w["w_ffn2_in"], "ffn2_in")
    dy, df2, dgate2, sq_err = _out_loss(a2, ffn2_w["w_ffn2_out"], x2, p2["gate"],
                                        loss_target.reshape(tokens, D_MODEL), seq, "ffn2_out")
    saved2 = (x2, h3, g2, u2, a2, ffn2_w["w_ffn2_in"], ffn2_w["w_ffn2_out"])

    grads = {}
    (dx2, do_mix, dsh2, dsc2, dgate_m, dg_ffn2), ops2 = _ffn_bwd_x(df2, dy, saved2, p2, seq, "ffn2", (o_mix, pm["gate"]))
    grads["w_ffn2_out"], grads["w_ffn2_in"] = _ffn_bwd_wout(ops2, "ffn2"), _ffn_bwd_win(ops2, "ffn2")
    s_f2, tok = _scatter_start(grads, ("w_ffn2_in", "w_ffn2_out"), "ffn2")

    grads["w_out"] = _mm(merged, do_mix, "tn", "mix_bwd_wout", out_dtype=BF16, tm=512, tn=D_MODEL)

    def merge_bwd(rows, bats, vecs):
        dmv, zg, bp, bm = rows
        s_p, s_m = _sigmoid(zg[:, :D_MODEL]), _sigmoid(zg[:, D_MODEL:])
        dzg = jnp.concatenate([dmv * bp * s_p * (1.0 - s_p), dmv * bm * s_m * (1.0 - s_m)], axis=1)
        return [dmv * s_p, dmv * s_m, dzg], [], []
    dbr_pool, dbr_mla, dz_g = _rowmap("mix_bwd_dmerged", merge_bwd, seq, [do_mix, z_g, br_pool, br_mla],
                                      row_outs=[(D_MODEL, BF16), (D_MODEL, BF16), (2 * D_MODEL, BF16)],
                                      mm=(w_mix_out, "nt"))

    grads["w_pool_proj"] = _mm(dbr_pool, ps, "tn", "pool_bwd_wproj", out_dtype=BF16, tm=512, tn=POOL_WIDTH)
    dps = _mm(dbr_pool, wt_pool, "nn", "pool_bwd_dps", tm=t_big, tn=POOL_WIDTH)
    dz_p, dgrp, dpool_scale = _pool_bwd(dps, pooled, pg, w["pool_grp"], w["pool_scale"] + tok[0:1, 0:1], seq)

    dwtmla_pad = _mm(dbr_mla, attn, "tn", "mla_bwd_wproj", out_dtype=BF16, tm=512, tn=D_MODEL)
    grads["w_mla_proj"] = dwtmla_pad.reshape(D_MODEL, N_HEADS, HEAD_SLAB)[:, :, 64:].reshape(D_MODEL, -1)
    d_attn = _mm(dbr_mla, wtmla_pad, "nn", "mla_bwd_dattn", out_dtype=BF16, tm=t_mid, tn=D_MODEL)
    dqc, dkc, dvp = _attn_bwd(qc, kc, vp, attn, lse, d_attn, seq)
    dqp, dkv, dkr, dg_q, dg_kn, dg_kr = _qk_prep_bwd(dqc, dkc, dvp, qp, kv, z_a, pos, g_q, g_kn, g_kr, inv_freq, seq)
    dwtq_pad = _mm(dqp, qn, "tn", "q_up_bwd_w", out_dtype=BF16, tm=512, tn=Q_LORA)
    grads["w_q_up"] = dwtq_pad.reshape(N_HEADS, HEAD_SLAB, Q_LORA)[:, :96].reshape(-1, Q_LORA)
    grads["w_kv_up"] = _mm(dkv, kvn, "tn", "kv_up_bwd_w", out_dtype=BF16, tm=512, tn=KV_LORA)
    dqn = _mm(dqp, wtq_pad, "nn", "q_up_bwd_x", tm=t_big, tn=Q_LORA)
    dkvn = _mm(dkv, wt_kv, "nn", "kv_up_bwd_x", tm=t_big, tn=KV_LORA)
    dz_a, dg_qa, dg_kva = _latent_norm_bwd(dqn, dkvn, dkr, z_a, w["q_a_norm"], w["kv_a_norm"], seq)

    dwt_a = _mm(dz_a, h2, "tn", "mix_in_bwd_wa", out_dtype=BF16, tm=256, tn=D_MODEL)
    dwt_p = _mm(dz_p, h2, "tn", "mix_in_bwd_wp", out_dtype=BF16, tm=512, tn=D_MODEL)
    dwt_g = _mm(dz_g, h2, "tn", "mix_in_bwd_wg", out_dtype=BF16, tm=512, tn=D_MODEL)
    grads["w_in"] = jnp.concatenate([dwt_p, dwt_a[:640], dwt_a[704:736], dwt_g], axis=0)
    s_mix, tok = _scatter_start(grads, MIXER, "mix")
    dh2 = _mm(dz_a, wt_a, "nn", "mix_in_bwd_xa", tm=t_mid, tn=D_MODEL)
    dh2 = _mm(dz_p, wt_p, "nn", "mix_in_bwd_xp", tm=t_mid, tn=D_MODEL, add=dh2)
    dh2 = _mm(dz_g, wt_g, "nn", "mix_in_bwd_xg", tm=t_mid, tn=D_MODEL, add=dh2)
    pm_tied = dict(pm, scale=pm["scale"] + tok[0:1, 0:1])
    dx1, df1, dsh_m, dsc_m, dgate1, dg_mix = _norm_mod_bwd(dh2, x1, dx2, pm_tied, seq, "mix_bwd_norm", (f1, p1["gate"]))

    small_early = [dg_mix.reshape(w["norm_mix"].shape), dg_ffn2.reshape(w["norm_ffn2"].shape), dgrp, dpool_scale,
                   dg_qa, dg_kva, dg_q[:, :QK_NOPE], dg_q[:, QK_NOPE:QK_NOPE + QK_ROPE], dg_kn[:, :QK_NOPE],
                   dg_kr[:, QK_NOPE:QK_NOPE + QK_ROPE]]
    s_small, tok = _exchange_start([_pack_small(small_early)], "gather_small_start", after=tok)

    handles = {}

    def ffn1_mid(operands):
        grads["w_ffn1_out"] = _ffn_bwd_wout(operands, "ffn1")
        handles["f1o"], token = _scatter_start(grads, ("w_ffn1_out",), "ffn1_out")
        first = _ffn_bwd_win(operands, "ffn1", after=token, half=0)
        handles["f1i0"], token = _exchange_start([first.reshape(N_DEV, -1, first.shape[1])],
                                                 "scatter_ffn1_in0_start", scatter=True, after=token)
        return token

    p1_tied = dict(p1, scale=p1["scale"] + tok[0:1, 0:1])
    (dx0, dsh1, dsc1, dg_ffn1), ops1 = _ffn_bwd_x(df1, dx1, saved1, p1_tied, seq, "ffn1", mid=ffn1_mid)
    s_f1o = handles["f1o"]

    dmod = jnp.stack([jnp.stack([dsh1, dsc1, 0.5 * dgate1], axis=1),
                      jnp.stack([dsh_m, dsc_m, dgate_m], axis=1),
                      jnp.stack([dsh2, dsc2, 0.5 * dgate2], axis=1)], axis=1)
    n_dmod = nseq * 9 * D_MODEL // LANES
    tail = _all_gather(jnp.concatenate([dmod.reshape(-1, LANES), _flat_rows(dg_ffn1), _flat_rows(sq_err)], axis=0),
                       "gather_dmod")
    dmod_all = tail[:, :n_dmod].reshape(N_DEV * nseq, 9 * D_MODEL)

    second = _ffn_bwd_win(ops1, "ffn1", after=tail, half=1)
    s_second, tok = _exchange_start([second.reshape(N_DEV, -1, second.shape[1])], "scatter_ffn1_in1_start",
                                    scatter=True, after=tail)
    s_f1i = (handles["f1i0"], s_second)

    dmod_cols = lax.dynamic_slice(dmod_all, (0, me * n_ada), (N_DEV * nseq, n_ada)) + tok[0:1, 0:1]
    g_w_ada, g_b_ada = _ada_grads(c_act, dmod_all, dmod_cols)
    tail_sum = _sum_blocks(tail[:, n_dmod:], "sum_tail")
    g_norm_ffn1 = tail_sum[:SUBLANES].reshape(1, D_MODEL)
    loss = 0.5 * jnp.sum(tail_sum[SUBLANES:]) * (1.0 / D_MODEL)
    small_all = _exchange_wait(s_small, "gather_small_wait", after=g_b_ada)[0]
    small_sum = _sum_blocks(small_all, "sum_small")
    small = dict(zip(SMALL[1:], _unpack_small(small_sum, [w[n] for n in SMALL[1:]])))
    grad_w = dict(small, w_ada=g_w_ada, b_ada=g_b_ada, norm_ffn1=g_norm_ffn1)

    delta, new_m, new_v = {}, {}, {}

    def update(names, landed=None):
        for n in names:
            if landed is None:
                delta[n], new_m[n], new_v[n] = _adamw(w[n], grad_w[n], m[n], v[n], f"adamw_{n}")
            elif n in KEPT_TRANSPOSED:
                res = _adamw_landed(w[n].T, landed[n], m[n].T, v[n].T, f"adamw_{n}")
                grad_w[n], delta[n], new_m[n], new_v[n] = (r.T for r in res)
            elif n in ROW_SHARDED:
                grad_w[n], delta[n], new_m[n], new_v[n] = _adamw_landed(w[n], landed[n], m[n], v[n], f"adamw_{n}")
            else:
                grad_w[n] = _sum_blocks(landed[n][0], f"sum_{n}").T
                delta[n], new_m[n], new_v[n] = _adamw(w[n], grad_w[n], m[n], v[n], f"adamw_{n}")

    update(("w_ada",))
    rep = ("b_ada",) + SMALL
    d_s, m_s, v_s = _adamw(_pack_small([w[n] for n in rep]), _pack_small([grad_w[n] for n in rep]),
                           _pack_small([m[n] for n in rep]), _pack_small([v[n] for n in rep]), "adamw_small")
    like = [w[n] for n in rep]
    for dst, packed in ((delta, d_s), (new_m, m_s), (new_v, v_s)):
        dst.update(zip(rep, _unpack_small(packed, like)))
    update(("w_ffn2_in", "w_ffn2_out"), _scatter_wait(s_f2, ("w_ffn2_in", "w_ffn2_out"), "ffn2", after=d_s))
    update(MIXER, _scatter_wait(s_mix, MIXER, "mix", after=delta["w_ffn2_out"]))
    update(("w_ffn1_out",), _scatter_wait(s_f1o, ("w_ffn1_out",), "ffn1_out", after=delta["w_out"]))
    halves = [_exchange_wait(h, f"scatter_ffn1_in{i}_wait", scatter=True, after=delta["w_ffn1_out"])[0]
              for i, h in enumerate(s_f1i)]
    update(("w_ffn1_in",), {"w_ffn1_in": halves})

    lead = lambda d: [d[n].reshape(shapes[n]) for n in WEIGHTS]
    return (loss, dx0.reshape(x.shape), *lead(grad_w), *lead(delta), *lead(new_m), *lead(new_v))


def kernel(x, c, positions, w_ada, b_ada, norm_ffn1, w_ffn1_in, w_ffn1_out, norm_mix, w_in, pool_grp, pool_scale, w_pool_proj, q_a_norm, w_q_up, kv_a_norm, w_kv_up, q_norm_nope, q_norm_rope, k_norm_nope, k_norm_rope, w_mla_proj, w_out, norm_ffn2, w_ffn2_in, w_ffn2_out, loss_target, m_w_ada, m_b_ada, m_norm_ffn1, m_w_ffn1_in, m_w_ffn1_out, m_norm_mix, m_w_in, m_pool_grp, m_pool_scale, m_w_pool_proj, m_q_a_norm, m_w_q_up, m_kv_a_norm, m_w_kv_up, m_q_norm_nope, m_q_norm_rope, m_k_norm_nope, m_k_norm_rope, m_w_mla_proj, m_w_out, m_norm_ffn2, m_w_ffn2_in, m_w_ffn2_out, v_w_ada, v_b_ada, v_norm_ffn1, v_w_ffn1_in, v_w_ffn1_out, v_norm_mix, v_w_in, v_pool_grp, v_pool_scale, v_w_pool_proj, v_q_a_norm, v_w_q_up, v_kv_a_norm, v_w_kv_up, v_q_norm_nope, v_q_norm_rope, v_k_norm_nope, v_k_norm_rope, v_w_mla_proj, v_w_out, v_norm_ffn2, v_w_ffn2_in, v_w_ffn2_out):
    w = dict(w_ada=w_ada, b_ada=b_ada, norm_ffn1=norm_ffn1, w_ffn1_in=w_ffn1_in, w_ffn1_out=w_ffn1_out, norm_mix=norm_mix, w_in=w_in, pool_grp=pool_grp, pool_scale=pool_scale, w_pool_proj=w_pool_proj, q_a_norm=q_a_norm, w_q_up=w_q_up, kv_a_norm=kv_a_norm, w_kv_up=w_kv_up, q_norm_nope=q_norm_nope, q_norm_rope=q_norm_rope, k_norm_nope=k_norm_nope, k_norm_rope=k_norm_rope, w_mla_proj=w_mla_proj, w_out=w_out, norm_ffn2=norm_ffn2, w_ffn2_in=w_ffn2_in, w_ffn2_out=w_ffn2_out)
    m = dict(w_ada=m_w_ada, b_ada=m_b_ada, norm_ffn1=m_norm_ffn1, w_ffn1_in=m_w_ffn1_in, w_ffn1_out=m_w_ffn1_out, norm_mix=m_norm_mix, w_in=m_w_in, pool_grp=m_pool_grp, pool_scale=m_pool_scale, w_pool_proj=m_w_pool_proj, q_a_norm=m_q_a_norm, w_q_up=m_w_q_up, kv_a_norm=m_kv_a_norm, w_kv_up=m_w_kv_up, q_norm_nope=m_q_norm_nope, q_norm_rope=m_q_norm_rope, k_norm_nope=m_k_norm_nope, k_norm_rope=m_k_norm_rope, w_mla_proj=m_w_mla_proj, w_out=m_w_out, norm_ffn2=m_norm_ffn2, w_ffn2_in=m_w_ffn2_in, w_ffn2_out=m_w_ffn2_out)
    v = dict(w_ada=v_w_ada, b_ada=v_b_ada, norm_ffn1=v_norm_ffn1, w_ffn1_in=v_w_ffn1_in, w_ffn1_out=v_w_ffn1_out, norm_mix=v_norm_mix, w_in=v_w_in, pool_grp=v_pool_grp, pool_scale=v_pool_scale, w_pool_proj=v_w_pool_proj, q_a_norm=v_q_a_norm, w_q_up=v_w_q_up, kv_a_norm=v_kv_a_norm, w_kv_up=v_w_kv_up, q_norm_nope=v_q_norm_nope, q_norm_rope=v_q_norm_rope, k_norm_nope=v_k_norm_nope, k_norm_rope=v_k_norm_rope, w_mla_proj=v_w_mla_proj, w_out=v_w_out, norm_ffn2=v_norm_ffn2, w_ffn2_in=v_w_ffn2_in, w_ffn2_out=v_w_ffn2_out)
    return _step(x, c, positions, w, m, v, loss_target)
```

```python
import functools
import math

import jax
import jax.numpy as jnp
from jax import lax
from jax.experimental import pallas as pl
from jax.experimental.pallas import tpu as pltpu

F32 = jnp.float32
BF16 = jnp.bfloat16
MESH = pl.DeviceIdType.MESH
AXES = ("x", "y", "c")
N_DEV = 8

D_MODEL = 1024
D_FF = 2816
N_HEADS = 8
HEAD_SLAB = 128
QK_NOPE = 64
QK_ROPE = 32
POOL_WIDTH = 512
POOL_GROUPS = 4
POOL_GROUP_DIM = 128
Q_LORA = 384
KV_LORA = 256
ROPE_THETA = 10000.0
ATTN_SCALE = 1.0 / math.sqrt(QK_NOPE + QK_ROPE)
NORM_EPS = 1e-6
ADAM_LR, ADAM_B1, ADAM_B2, ADAM_EPS, ADAM_WD, ADAM_STEP = 0.001, 0.9, 0.999, 1e-08, 0.01, 10

LANES = 128
SUBLANES = 8
VMEM_LIMIT = 52 * 1024 * 1024
ADAMW_WHOLE_BYTES = 3 << 19
SUM_WHOLE_BYTES = 4 << 20

BIG = ("w_ffn1_in", "w_ffn1_out", "w_in", "w_pool_proj", "w_q_up", "w_kv_up",
       "w_mla_proj", "w_out", "w_ffn2_in", "w_ffn2_out")
ROW_SHARDED = ("w_ffn1_out", "w_out", "w_ffn2_out")
MIXER = ("w_in", "w_pool_proj", "w_q_up", "w_kv_up", "w_mla_proj", "w_out")
KEPT_TRANSPOSED = ("w_ffn1_in", "w_ffn2_in", "w_in", "w_q_up")
SMALL = ("norm_ffn1", "norm_mix", "norm_ffn2", "pool_grp", "pool_scale", "q_a_norm",
         "kv_a_norm", "q_norm_nope", "q_norm_rope", "k_norm_nope", "k_norm_rope")
WEIGHTS = ("w_ada", "b_ada", "norm_ffn1", "w_ffn1_in", "w_ffn1_out", "norm_mix", "w_in",
           "pool_grp", "pool_scale", "w_pool_proj", "q_a_norm", "w_q_up", "kv_a_norm",
           "w_kv_up", "q_norm_nope", "q_norm_rope", "k_norm_nope", "k_norm_rope",
           "w_mla_proj", "w_out", "norm_ffn2", "w_ffn2_in", "w_ffn2_out")


def _params(*sem):
    return pltpu.CompilerParams(dimension_semantics=sem, vmem_limit_bytes=VMEM_LIMIT)


def _tile(n, cands):
    for c in cands:
        if n % c == 0:
            return c
    return n


def _my_pos():
    return lax.axis_index("x"), lax.axis_index("y"), lax.axis_index("c")


def _flip(pos, k):
    x, y, c = pos
    fx, fy, fc = (k >> 2) & 1, (k >> 1) & 1, k & 1
    return ((1 - x) if fx else x, (1 - y) if fy else y, (1 - c) if fc else c)


def _index(pos):
    x, y, c = pos
    return 4 * x + 2 * y + c


def _exchange(arrays, name, scatter=False):
    n = len(arrays)

    def body(*refs):
        ins, outs = refs[:n], refs[n:2 * n]
        send_sems, recv_sems, local_sems = refs[2 * n:]
        me = _my_pos()
        mine, sends = [], []
        for a in range(n):
            own = ins[a].at[_index(me)] if scatter else ins[a]
            cp = pltpu.make_async_copy(own, outs[a].at[_index(me)], local_sems.at[a])
            cp.start()
            mine.append(cp)
        for k in range(1, N_DEV):
            peer = _flip(me, k)
            for a in range(n):
                cp = pltpu.make_async_remote_copy(
                    src_ref=ins[a].at[_index(peer)] if scatter else ins[a],
                    dst_ref=outs[a].at[_index(me)],
                    send_sem=send_sems.at[a, k - 1], recv_sem=recv_sems.at[a, k - 1],
                    device_id=peer, device_id_type=MESH)
                cp.start()
                sends.append(cp)
        for k in range(1, N_DEV):
            peer = _flip(me, k)
            for a in range(n):
                pltpu.make_async_remote_copy(
                    src_ref=ins[a].at[_index(me)] if scatter else ins[a],
                    dst_ref=outs[a].at[_index(peer)],
                    send_sem=send_sems.at[a, k - 1], recv_sem=recv_sems.at[a, k - 1],
                    device_id=peer, device_id_type=MESH).wait_recv()
        for cp in sends:
            cp.wait_send()
        for cp in mine:
            cp.wait()

    shape = lambda x: x.shape if scatter else (N_DEV,) + x.shape
    return pl.pallas_call(
        body, name=name,
        out_shape=tuple(jax.ShapeDtypeStruct(shape(x), x.dtype) for x in arrays),
        in_specs=[pl.BlockSpec(memory_space=pl.ANY)] * n,
        out_specs=tuple(pl.BlockSpec(memory_space=pl.ANY) for _ in arrays),
        scratch_shapes=[pltpu.SemaphoreType.DMA((n, N_DEV - 1)),
                        pltpu.SemaphoreType.DMA((n, N_DEV - 1)),
                        pltpu.SemaphoreType.DMA((n,))],
    )(*arrays)


def _all_gather(x, name):
    return _exchange([x], name)[0]


_HBM = pl.BlockSpec(memory_space=pltpu.HBM)
_SEM = pl.BlockSpec(memory_space=pltpu.SEMAPHORE)
_ANY = pl.BlockSpec(memory_space=pl.ANY)
_EFFECT = pltpu.SideEffectType.DATAFLOW_SIDE_EFFECTING


def _split_copy(ins, lands, send_sems, recv_sems, a, k, me, scatter, incoming):
    peer = _flip(me, k)
    block = me if incoming else peer
    return pltpu.make_async_remote_copy(
        src_ref=ins[a].at[_index(block)] if scatter else ins[a],
        dst_ref=lands[a].at[_index(peer if incoming else me)],
        send_sem=send_sems.at[a * (N_DEV - 1) + k - 1], recv_sem=recv_sems.at[a * (N_DEV - 1) + k - 1],
        device_id=peer, device_id_type=MESH)


def _exchange_start_groups(groups, name, scatter=False, after=None):
    sizes = [len(g) for g in groups]
    first = [sum(sizes[:i]) for i in range(len(sizes))]
    n, ng = sum(sizes), len(sizes)
    after = jnp.zeros((SUBLANES, LANES), F32) if after is None else after

    def body(*refs):
        ins, lands = refs[:n], refs[n:2 * n]
        sems = refs[2 * n + 1:2 * n + 1 + 2 * ng]
        me = _my_pos()
        for g in range(ng):
            part = slice(first[g], first[g] + sizes[g])
            for k in range(1, N_DEV):
                for a in range(sizes[g]):
                    _split_copy(ins[part], lands[part], sems[2 * g], sems[2 * g + 1], a, k, me, scatter, False).start()
        refs[-1][...] = jnp.zeros((SUBLANES, LANES), F32)

    shape = lambda x: x.shape if scatter else (N_DEV,) + x.shape
    hbm = lambda x: pltpu.with_memory_space_constraint(x, pltpu.HBM)
    srcs = [hbm(x) for g in groups for x in g]
    zones = [hbm(lax.empty(shape(x), x.dtype)) for g in groups for x in g]
    sem_shapes = [pltpu.SemaphoreType.DMA((s * (N_DEV - 1),)) for s in sizes for _ in range(2)]
    out = pl.pallas_call(
        body, name=name,
        out_shape=(*sem_shapes, *[pltpu.HBM(x.shape, x.dtype) for x in srcs + zones],
                   jax.ShapeDtypeStruct((SUBLANES, LANES), F32)),
        in_specs=[_HBM] * (2 * n) + [_ANY],
        out_specs=(*[_SEM] * (2 * ng), *[_HBM] * (2 * n), pl.BlockSpec(memory_space=pltpu.VMEM)),
        input_output_aliases={i: 2 * ng + i for i in range(2 * n)},
        compiler_params=pltpu.CompilerParams(has_side_effects=_EFFECT),
    )(*srcs, *zones, after)
    bufs = out[2 * ng:-1]
    handles = [(out[2 * g], out[2 * g + 1], *bufs[first[g]:first[g] + sizes[g]],
                *bufs[n + first[g]:n + first[g] + sizes[g]]) for g in range(ng)]
    return handles, out[-1]


def _exchange_start(arrays, name, scatter=False, after=None):
    handles, token = _exchange_start_groups([arrays], name, scatter, after)
    return handles[0], token


def _exchange_wait(handle, name, scatter=False, after=None):
    send_sems, recv_sems = handle[0], handle[1]
    n = (len(handle) - 2) // 2
    after = jnp.zeros((SUBLANES, LANES), F32) if after is None else after

    def body(*refs):
        ins, lands = refs[:n], refs[n:2 * n]
        send, recv = refs[2 * n], refs[2 * n + 1]
        me = _my_pos()
        for k in range(1, N_DEV):
            for a in range(n):
                _split_copy(ins, lands, send, recv, a, k, me, scatter, False).wait_send()
                _split_copy(ins, lands, send, recv, a, k, me, scatter, True).wait_recv()

    bufs = handle[2:]
    out = pl.pallas_call(
        body, name=name,
        out_shape=tuple(pltpu.HBM(x.shape, x.dtype) for x in bufs),
        in_specs=[_HBM] * (2 * n) + [_SEM, _SEM, _ANY],
        out_specs=tuple([_HBM] * (2 * n)),
        input_output_aliases={i: i for i in range(2 * n)},
        compiler_params=pltpu.CompilerParams(has_side_effects=_EFFECT),
    )(*bufs, send_sems, recv_sems, after)
    me = _index(_my_pos())
    landed = []
    for src, land in zip(out[:n], out[n:]):
        own = lax.dynamic_slice_in_dim(src, me, 1, axis=0) if scatter else src[None]
        landed.append(lax.dynamic_update_slice_in_dim(land, own, me, axis=0))
    return landed


def _sum_blocks(x, name):
    n, rows, cols = x.shape
    whole = x.size * x.dtype.itemsize <= SUM_WHOLE_BYTES
    tr = rows if whole else _tile(rows, (512, 256, 128, 64, 32, 16, 8))

    def body(x_ref, o_ref):
        acc = x_ref[0].astype(F32)
        for d in range(1, n):
            acc = acc + x_ref[d].astype(F32)
        o_ref[...] = acc

    return pl.pallas_call(
        body, name=name,
        out_shape=jax.ShapeDtypeStruct((rows, cols), F32),
        grid=(rows // tr,),
        in_specs=[pl.BlockSpec((n, tr, cols), lambda i: (0, i, 0))],
        out_specs=pl.BlockSpec((tr, cols), lambda i: (i, 0)),
        compiler_params=_params("parallel"),
    )(x)


_DIMS = {"nn": (((1,), (0,)), ((), ())), "nt": (((1,), (1,)), ((), ())), "tn": (((0,), (0,)), ((), ()))}


def _mm(a, b, mode, name, out_dtype=F32, tm=None, tn=None, add=None, after=None, b_cols=None):
    if mode == "tn":
        kdim, m = a.shape
    else:
        m, kdim = a.shape
    n = b.shape[0] if mode == "nt" else b.shape[1]
    tm = tm or _tile(m, (512, 256, 128))
    tn = tn or _tile(n, (512, 256, 128))
    j0 = 0
    if b_cols is not None:
        j0, n = b_cols[0], b_cols[1] * tn
    dims = _DIMS[mode]

    def body(*refs):
        refs = refs if after is None else refs[1:]
        acc = lax.dot_general(refs[0][...].astype(BF16), refs[1][...].astype(BF16), dims,
                              preferred_element_type=F32)
        if add is not None:
            acc = acc + refs[2][...]
        refs[-1][...] = acc.astype(out_dtype)

    a_spec = (pl.BlockSpec((kdim, tm), lambda i, j: (0, i)) if mode == "tn"
              else pl.BlockSpec((tm, kdim), lambda i, j: (i, 0)))
    b_spec = (pl.BlockSpec((tn, kdim), lambda i, j: (j, 0)) if mode == "nt"
              else pl.BlockSpec((kdim, tn), lambda i, j: (0, j + j0)))
    o_spec = pl.BlockSpec((tm, tn), lambda i, j: (i, j))
    in_specs, args = [a_spec, b_spec], [a, b]
    if add is not None:
        in_specs.append(o_spec)
        args.append(add)
    if after is not None:
        in_specs.insert(0, _ANY)
        args.insert(0, after)
    return pl.pallas_call(
        body, name=name, out_shape=jax.ShapeDtypeStruct((m, n), out_dtype), grid=(m // tm, n // tn),
        in_specs=in_specs, out_specs=o_spec,
        compiler_params=_params("parallel", "parallel"),
    )(*args)


def _rowmap(name, fn, seq, rows, bats=(), vecs=(), row_outs=(), bat_outs=(), vec_outs=(), ts=None, mm=None, lhs=None):
    mms = [] if mm is None else (mm if isinstance(mm, list) else [mm])
    rows = [r if isinstance(r, tuple) else (r, r.shape[1], 0) for r in rows]
    tokens = rows[0][0].shape[0]
    nseq = tokens // seq
    ts = ts or _tile(seq, (512, 256, 128, 64, 32, 16, 8))
    nt = seq // ts
    n_r, n_b, n_v = len(rows), len(bats), len(vecs)
    n_ro, n_bo = len(row_outs), len(bat_outs)

    def accumulate(ref, val, first):
        @pl.when(first)
        def _():
            ref[...] = val.reshape(ref.shape)

        @pl.when(jnp.logical_not(first))
        def _():
            ref[...] += val.reshape(ref.shape)

    def body(*refs):
        n_in = n_r + n_b + n_v + len(mms)
        ins, outs = refs[:n_in], refs[n_in:]
        b_vals = [r[0] for r in ins[n_r:n_r + n_b]]
        v_vals = [r[...] for r in ins[n_r + n_b:n_r + n_b + n_v]]
        r_vals = [r[...] for r in ins[:n_r]]
        if mms:
            lefts = r_vals[:len(mms)] if lhs is None else [lhs(r_vals)]
            acc = None
            for left, b_ref, (_, mode) in zip(lefts, ins[n_r + n_b + n_v:], mms):
                part = lax.dot_general(left.astype(BF16), b_ref[...].astype(BF16), _DIMS[mode],
                                       preferred_element_type=F32)
                acc = part if acc is None else acc + part
            r_vals = [acc] + r_vals[len(mms):] if lhs is None else [acc, lefts[0]] + r_vals
        ro, bo, vo = fn(r_vals, b_vals, v_vals)
        for ref, val in zip(outs[:n_ro], ro):
            ref[...] = val.astype(ref.dtype)
        b, i = pl.program_id(0), pl.program_id(1)
        for ref, val in zip(outs[n_ro:n_ro + n_bo], bo):
            accumulate(ref, val, i == 0)
        for ref, val in zip(outs[n_ro + n_bo:], vo):
            accumulate(ref, val, jnp.logical_and(i == 0, b == 0))

    in_specs = [pl.BlockSpec((ts, w), functools.partial(lambda b, i, cb: (b * nt + i, cb), cb=cb))
                for _, w, cb in rows]
    in_specs += [pl.BlockSpec((1, 1, v.shape[2]), lambda b, i: (b, 0, 0)) for v in bats]
    in_specs += [pl.BlockSpec((1, v.shape[1]), lambda b, i: (0, 0)) for v in vecs]
    extra = [b_arr for b_arr, _ in mms]
    in_specs += [pl.BlockSpec(b_arr.shape, lambda b, i: (0, 0)) for b_arr in extra]
    out_shape = [jax.ShapeDtypeStruct((tokens, f), dt) for f, dt in row_outs]
    out_specs = [pl.BlockSpec((ts, f), lambda b, i: (b * nt + i, 0)) for f, _ in row_outs]
    out_shape += [jax.ShapeDtypeStruct((nseq, 1, f), F32) for f in bat_outs]
    out_specs += [pl.BlockSpec((1, 1, f), lambda b, i: (b, 0, 0)) for f in bat_outs]
    out_shape += [jax.ShapeDtypeStruct((1, f), F32) for f in vec_outs]
    out_specs += [pl.BlockSpec((1, f), lambda b, i: (0, 0)) for f in vec_outs]
    return pl.pallas_call(
        body, name=name, out_shape=tuple(out_shape), grid=(nseq, nt),
        in_specs=in_specs, out_specs=tuple(out_specs),
        compiler_params=_params("arbitrary", "arbitrary"),
    )(*([r[0] for r in rows] + list(bats) + list(vecs) + extra))


def _colsum(v):
    return jnp.sum(v, axis=0, keepdims=True)


def _rstd(x, width=None):
    width = width or x.shape[-1]
    return lax.rsqrt(jnp.sum(x * x, axis=-1, keepdims=True) * (1.0 / width) + NORM_EPS)


def _norm_bwd(dy, x, r, g, width=None):
    width = width or x.shape[-1]
    xhat = x * r
    dxhat = dy * g
    dx = r * (dxhat - xhat * (jnp.sum(dxhat * xhat, axis=-1, keepdims=True) * (1.0 / width)))
    return dx, dy * xhat


def _sigmoid(x):
    return 0.5 * jnp.tanh(0.5 * x) + 0.5


def _norm_mod(xv, g, sh, sc):
    return xv * _rstd(xv) * g * (1.0 + sc) + sh


def _norm_mod_fwd(x, p, seq, name):
    def fn(rows, bats, vecs):
        return [_norm_mod(rows[0], vecs[0], bats[0], bats[1])], [], []
    return _rowmap(name, fn, seq, [x], [p["shift"], p["scale"]], [p["gamma"]], row_outs=[(D_MODEL, BF16)])[0]


def _norm_mod_bwd(dh, x, dres, p, seq, name, prev=None):
    products = dh if isinstance(dh, list) else None
    lefts = [l for l, _ in products] if products else [dh]
    def fn(rows, bats, vecs):
        dhv, xv, dr = rows[:3]
        sc, g = bats[0], vecs[0]
        r = _rstd(xv)
        dxn, dg = _norm_bwd(dhv * (1.0 + sc), xv, r, g)
        dx = dr + dxn
        ro, bo = [dx], [_colsum(dhv), _colsum(dhv * (xv * r * g))]
        if prev is not None:
            ro.append(bats[1] * dx)
            bo.append(_colsum(dx * rows[3].astype(F32)))
        return ro, bo, [_colsum(dg)]
    more = prev is not None
    return _rowmap(name, fn, seq, lefts + [x, dres] + ([prev[0]] if more else []),
                   [p["scale"]] + ([prev[1]] if more else []), [p["gamma"]],
                   row_outs=[(D_MODEL, F32)] + ([(D_MODEL, BF16)] if more else []),
                   bat_outs=[D_MODEL] * (3 if more else 2), vec_outs=[D_MODEL],
                   mm=[(r, "nn") for _, r in products] if products else None)


def _ffn_in_act(h, wt_in, name):
    tokens = h.shape[0]
    tm, tn = _tile(tokens, (2048, 1024, 512)), 256
    nj = D_FF // tn

    def body(h_ref, wg_ref, wu_ref, g_ref, u_ref, a_ref):
        hv = h_ref[...]
        g = lax.dot_general(hv, wg_ref[...], _DIMS["nt"], preferred_element_type=F32)
        u = lax.dot_general(hv, wu_ref[...], _DIMS["nt"], preferred_element_type=F32)
        g_ref[...] = g.astype(BF16)
        u_ref[...] = u.astype(BF16)
        a_ref[...] = (g * _sigmoid(g) * u).astype(BF16)

    o_spec = pl.BlockSpec((tm, tn), lambda i, j: (i, j))
    out = jax.ShapeDtypeStruct((tokens, D_FF), BF16)
    return pl.pallas_call(
        body, name=name, grid=(tokens // tm, nj), out_shape=(out, out, out),
        in_specs=[pl.BlockSpec((tm, D_MODEL), lambda i, j: (i, 0)),
                  pl.BlockSpec((tn, D_MODEL), lambda i, j: (j, 0)),
                  pl.BlockSpec((tn, D_MODEL), lambda i, j: (j + nj, 0))],
        out_specs=(o_spec, o_spec, o_spec),
        compiler_params=_params("parallel", "parallel"),
    )(h, wt_in, wt_in)


def _out_residual(a, w_out, res, gate, nxt, seq, name, lhs=None):
    def fn(rows, bats, vecs):
        acc, rv = rows[0], rows[-1]
        x_new = rv + bats[0] * acc
        made = [] if lhs is None else [rows[1]]
        return [x_new, acc, _norm_mod(x_new, vecs[0], bats[1], bats[2])] + made, [], []
    outs = [(D_MODEL, F32), (D_MODEL, BF16), (D_MODEL, BF16)] + ([] if lhs is None else [(D_MODEL, BF16)])
    return _rowmap(name, fn, seq, (a if lhs is not None else [a]) + [res], [gate, nxt["shift"], nxt["scale"]],
                   [nxt["gamma"]], row_outs=outs, ts=_tile(seq, (512, 256, 128)), mm=(w_out, "nn"), lhs=lhs)


def _out_loss(a, w_out, res, gate, target, seq, name):
    def fn(rows, bats, vecs):
        acc, rv, tv = rows
        err = rv + bats[0] * acc - tv
        dy = err * (1.0 / D_MODEL)
        return [dy, bats[0] * dy], [_colsum(dy * acc)], [_colsum(err * err)]
    return _rowmap(name, fn, seq, [a, res, target], [gate], row_outs=[(D_MODEL, F32), (D_MODEL, BF16)],
                   bat_outs=[D_MODEL], vec_outs=[D_MODEL], ts=_tile(seq, (512, 256, 128)), mm=(w_out, "nn"))


def _ffn_bwd_x(df, dres, saved, p, seq, tag, prev=None, mid=None):
    x, h, g, u, a, w_in, w_out = saved
    tokens = x.shape[0]

    def act_bwd(rows, bats, vecs):
        dav, gv, uv = rows[0], rows[1].astype(F32), rows[2].astype(F32)
        sg = _sigmoid(gv)
        silu = gv * sg
        dg = dav * uv * (sg * (1.0 + gv * (1.0 - sg)))
        return [jnp.concatenate([dg, dav * silu], axis=1)], [], []
    dgu = _rowmap(f"{tag}_bwd_da", act_bwd, seq, [df, g, u], row_outs=[(2 * D_FF, BF16)],
                  ts=_tile(seq, (256, 128)), mm=(w_out, "nt"))[0]
    operands = (a, df, dgu, h)
    after = None if mid is None else mid(operands)
    dh = _mm(dgu, w_in, "nn", f"{tag}_bwd_dh", tm=_tile(tokens, (512, 256)), tn=D_MODEL, after=after)
    return _norm_mod_bwd(dh, x, dres, p, seq, f"{tag}_bwd_norm", prev), operands


def _ffn_bwd_wout(operands, tag):
    a, df, _, _ = operands
    return _mm(a, df, "tn", f"{tag}_bwd_wout", out_dtype=BF16, tm=256, tn=D_MODEL)


def _ffn_bwd_win(operands, tag, after=None, half=None):
    _, _, dgu, h = operands
    if half is None:
        return _mm(dgu, h, "tn", f"{tag}_bwd_win", out_dtype=BF16, tm=512, tn=D_MODEL, after=after)
    return _mm(dgu, h, "tn", f"{tag}_bwd_win{half}", out_dtype=BF16, tm=512, tn=D_MODEL // 2, after=after,
               b_cols=(half, 1))


def _shift_rows(v, k, forward):
    n = v.shape[0]
    row = lax.broadcasted_iota(jnp.int32, v.shape, 0)
    if forward:
        return jnp.where(row >= k, pltpu.roll(v, k, 0), 0.0)
    return jnp.where(row < n - k, pltpu.roll(v, n - k, 0), 0.0)


def _window_sums(v, forward):
    out, s, k = [], v, 1
    for _ in range(POOL_GROUPS):
        s = s + _shift_rows(s, k, forward)
        out.append(s)
        k *= 2
    return out


def _by_group(vals, g):
    out = vals[-1]
    for idx in range(len(vals) - 2, -1, -1):
        out = jnp.where(g == idx, vals[idx], out)
    return out


def _inv_count(shape, g):
    t1 = lax.broadcasted_iota(jnp.int32, shape, 0) + 1
    window = _by_group([jnp.int32(2 ** (i + 1)) for i in range(POOL_GROUPS)], g)
    return 1.0 / jnp.minimum(t1, window).astype(F32)


def _pool_fwd(u, grp, scale, seq):
    tokens = u.shape[0]

    def body(u_ref, grp_ref, sc_ref, pooled_ref, pg_ref, ps_ref):
        g = pl.program_id(1)
        uv = u_ref[...]
        sums = _by_group(_window_sums(uv, True), g)
        pooled = (sums * _inv_count(uv.shape, g) - uv).astype(BF16)
        pg = jnp.dot(pooled, grp_ref[0].astype(BF16), preferred_element_type=F32)
        pooled_ref[...] = pooled
        pg_ref[...] = pg
        ps_ref[...] = (pg * sc_ref[...]).astype(BF16)

    blk = pl.BlockSpec((seq, POOL_GROUP_DIM), lambda b, g: (b, g))
    return pl.pallas_call(
        body, name="pool_fwd", grid=(tokens // seq, POOL_GROUPS),
        out_shape=(jax.ShapeDtypeStruct(u.shape, BF16), jax.ShapeDtypeStruct(u.shape, F32),
                   jax.ShapeDtypeStruct(u.shape, BF16)),
        in_specs=[blk, pl.BlockSpec((1, POOL_GROUP_DIM, POOL_GROUP_DIM), lambda b, g: (g, 0, 0)),
                  pl.BlockSpec((1, POOL_GROUP_DIM), lambda b, g: (0, g))],
        out_specs=(blk, blk, blk),
        compiler_params=_params("parallel", "parallel"),
    )(u, grp, scale)


def _pool_bwd(dps, pooled, pg, grp, scale, seq):
    tokens = dps.shape[0]

    def body(dps_ref, pooled_ref, pg_ref, grp_ref, sc_ref, du_ref, dgrp_ref, dsc_ref):
        g, b = pl.program_id(0), pl.program_id(1)
        dpsv = dps_ref[...]
        dpg = (dpsv * sc_ref[...]).astype(BF16)
        dsc = _colsum(dpsv * pg_ref[...])
        dgrp = lax.dot_general(pooled_ref[...], dpg, _DIMS["tn"], preferred_element_type=F32)

        @pl.when(b == 0)
        def _():
            dsc_ref[...] = dsc
            dgrp_ref[0] = dgrp

        @pl.when(b > 0)
        def _():
            dsc_ref[...] += dsc
            dgrp_ref[0] += dgrp

        dpool = lax.dot_general(dpg, grp_ref[0].astype(BF16), _DIMS["nt"], preferred_element_type=F32)
        sums = _by_group(_window_sums(dpool * _inv_count(dpool.shape, g), False), g)
        du_ref[...] = (sums - dpool).astype(BF16)

    blk = pl.BlockSpec((seq, POOL_GROUP_DIM), lambda g, b: (b, g))
    grp_spec = pl.BlockSpec((1, POOL_GROUP_DIM, POOL_GROUP_DIM), lambda g, b: (g, 0, 0))
    vec_spec = pl.BlockSpec((1, POOL_GROUP_DIM), lambda g, b: (0, g))
    return pl.pallas_call(
        body, name="pool_bwd", grid=(POOL_GROUPS, tokens // seq),
        out_shape=(jax.ShapeDtypeStruct(dps.shape, BF16), jax.ShapeDtypeStruct(grp.shape, F32),
                   jax.ShapeDtypeStruct(scale.shape, F32)),
        in_specs=[blk, blk, blk, grp_spec, vec_spec],
        out_specs=(blk, grp_spec, vec_spec),
        compiler_params=_params("arbitrary", "arbitrary"),
    )(dps, pooled, pg, grp, scale)


def _lane(shape):
    return lax.broadcasted_iota(jnp.int32, shape, len(shape) - 1)


def _rot(y):
    lane = _lane(y.shape)
    r = jnp.where(lane < QK_NOPE + QK_ROPE // 2,
                  -pltpu.roll(y, HEAD_SLAB - QK_ROPE // 2, 1), pltpu.roll(y, QK_ROPE // 2, 1))
    return jnp.where(jnp.logical_and(lane >= QK_NOPE, lane < QK_NOPE + QK_ROPE), r, 0.0)


def _part_rstd(x):
    sq = x * x
    nope = _lane(x.shape) < QK_NOPE
    s_nope = jnp.sum(jnp.where(nope, sq, 0.0), axis=-1, keepdims=True)
    s_rope = jnp.sum(sq, axis=-1, keepdims=True) - s_nope
    return jnp.where(nope, lax.rsqrt(s_nope * (1.0 / QK_NOPE) + NORM_EPS),
                     lax.rsqrt(s_rope * (1.0 / QK_ROPE) + NORM_EPS))


def _part_norm_bwd(dy, x, r, g):
    nope = _lane(x.shape) < QK_NOPE
    xhat = x * r
    dxhat = dy * g
    prod = dxhat * xhat
    m_nope = jnp.sum(jnp.where(nope, prod, 0.0), axis=-1, keepdims=True)
    m_rope = jnp.sum(prod, axis=-1, keepdims=True) - m_nope
    mean = jnp.where(nope, m_nope * (1.0 / QK_NOPE), m_rope * (1.0 / QK_ROPE))
    return r * (dxhat - xhat * mean), dy * xhat


def _latent_norm_fwd(z_a, g_q, g_kv, seq):
    def fn(rows, bats, vecs):
        q, kv = rows[0][:, :Q_LORA], rows[0][:, Q_LORA:Q_LORA + KV_LORA]
        return [q * _rstd(q) * vecs[0], kv * _rstd(kv) * vecs[1]], [], []
    return _rowmap("latent_norm", fn, seq, [z_a], vecs=[g_q, g_kv],
                   row_outs=[(Q_LORA, BF16), (KV_LORA, BF16)])


def _latent_norm_bwd(dqn, dkvn, dkr, z_a, g_q, g_kv, seq):
    def fn(rows, bats, vecs):
        dq, dkv, dkrv, z = rows
        q, kv = z[:, :Q_LORA], z[:, Q_LORA:Q_LORA + KV_LORA]
        dxq, dgq = _norm_bwd(dq, q, _rstd(q), vecs[0])
        dxkv, dgkv = _norm_bwd(dkv, kv, _rstd(kv), vecs[1])
        return [jnp.concatenate([dxq, dxkv, dkrv], axis=1)], [], [_colsum(dgq), _colsum(dgkv)]
    return _rowmap("latent_norm_bwd", fn, seq, [dqn, dkvn, dkr, z_a], vecs=[g_q, g_kv],
                   row_outs=[(Q_LORA + KV_LORA + HEAD_SLAB, BF16)], vec_outs=[Q_LORA, KV_LORA])


def _qk_prep_fwd(qp, kv, z_a, pos, g_q, g_kn, g_kr, inv_freq, seq):
    def fn(rows, bats, vecs):
        qv, kvv, kr, p = rows
        gq, gkn, gkr, invf = vecs
        ang = p * invf
        cos, sin = jnp.cos(ang), jnp.sin(ang)
        nope = _lane(kr.shape) < QK_NOPE
        krn = kr * _rstd(kr, QK_ROPE) * gkr
        krr = krn * cos + _rot(krn) * sin
        qs, ks, vs = [], [], []
        for h in range(N_HEADS):
            xq = qv[:, h * HEAD_SLAB:(h + 1) * HEAD_SLAB]
            y = xq * _part_rstd(xq) * gq
            qs.append(y * cos + _rot(y) * sin)
            xk = kvv[:, h * HEAD_SLAB:(h + 1) * HEAD_SLAB]
            kn = jnp.where(nope, xk, 0.0)
            ks.append(jnp.where(nope, kn * _rstd(kn, QK_NOPE) * gkn, krr))
            vs.append(jnp.where(nope, 0.0, xk))
        return [jnp.concatenate(v, axis=1) for v in (qs, ks, vs)], [], []
    width = N_HEADS * HEAD_SLAB
    return _rowmap("qk_prep", fn, seq, [qp, kv, (z_a, HEAD_SLAB, 5), pos], vecs=[g_q, g_kn, g_kr, inv_freq],
                   row_outs=[(width, BF16)] * 3, ts=_tile(seq, (512, 256, 128, 64, 32, 16, 8)))


def _qk_prep_bwd(dqc, dkc, dvp, qp, kv, z_a, pos, g_q, g_kn, g_kr, inv_freq, seq):
    def fn(rows, bats, vecs):
        dq, dk, dv, qv, kvv, kr, p = rows
        gq, gkn, gkr, invf = vecs
        ang = p * invf
        cos, sin = jnp.cos(ang), jnp.sin(ang)
        nope = _lane(kr.shape) < QK_NOPE
        dqs, dkvs = [], []
        dgq = jnp.zeros((1, HEAD_SLAB), F32)
        dgkn = jnp.zeros((1, HEAD_SLAB), F32)
        dkrr = jnp.zeros(kr.shape, F32)
        for h in range(N_HEADS):
            sl = slice(h * HEAD_SLAB, (h + 1) * HEAD_SLAB)
            dyr = dq[:, sl]
            dy = dyr * cos - _rot(dyr * sin)
            xq = qv[:, sl]
            dx, dg = _part_norm_bwd(dy, xq, _part_rstd(xq), gq)
            dqs.append(dx)
            dgq = dgq + _colsum(dg)
            dkh = dk[:, sl]
            dkrr = dkrr + jnp.where(nope, 0.0, dkh)
            kn = jnp.where(nope, kvv[:, sl], 0.0)
            dxk, dgk = _norm_bwd(jnp.where(nope, dkh, 0.0), kn, _rstd(kn, QK_NOPE), gkn, QK_NOPE)
            dgkn = dgkn + _colsum(dgk)
            dkvs.append(jnp.where(nope, dxk, dv[:, sl]))
        dkrn = dkrr * cos - _rot(dkrr * sin)
        dkr, dgkr = _norm_bwd(dkrn, kr, _rstd(kr, QK_ROPE), gkr, QK_ROPE)
        return ([jnp.concatenate(dqs, axis=1), jnp.concatenate(dkvs, axis=1), dkr], [],
                [dgq, dgkn, _colsum(dgkr)])
    width = N_HEADS * HEAD_SLAB
    return _rowmap("qk_prep_bwd", fn, seq, [dqc, dkc, dvp, qp, kv, (z_a, HEAD_SLAB, 5), pos],
                   vecs=[g_q, g_kn, g_kr, inv_freq],
                   row_outs=[(width, BF16), (width, BF16), (HEAD_SLAB, F32)],
                   vec_outs=[HEAD_SLAB] * 3, ts=_tile(seq, (512, 256, 128, 64, 32, 16, 8)))


def _scores(q, k_ref, keys, tq):
    s = lax.dot_general(q, k_ref[0:keys, :], _DIMS["nt"], preferred_element_type=F32) * ATTN_SCALE
    row = lax.broadcasted_iota(jnp.int32, (tq, tq), 0)
    col = lax.broadcasted_iota(jnp.int32, (tq, tq), 1)
    diag = jnp.where(col <= row, s[:, keys - tq:], -1e30)
    return diag if keys == tq else jnp.concatenate([s[:, :keys - tq], diag], axis=1)


def _attn_fwd(qc, kc, vp, seq):
    tokens = qc.shape[0]
    tq = _tile(seq, (256, 128))
    nq = seq // tq

    def body(q_ref, k_ref, v_ref, o_ref, lse_ref):
        for i in range(nq):
            rows, keys = slice(i * tq, (i + 1) * tq), (i + 1) * tq
            s = _scores(q_ref[rows, :], k_ref, keys, tq)
            m = jnp.max(s, axis=-1, keepdims=True)
            p = jnp.exp(s - m)
            l = jnp.sum(p, axis=-1, keepdims=True)
            acc = jnp.dot(p.astype(BF16), v_ref[0:keys, :], preferred_element_type=F32)
            o_ref[rows, :] = (acc / l).astype(BF16)
            lse_ref[rows, :] = jnp.broadcast_to(m + jnp.log(l), (tq, HEAD_SLAB))

    spec = pl.BlockSpec((seq, HEAD_SLAB), lambda b, h: (b, h))
    return pl.pallas_call(
        body, name="attn_fwd", grid=(tokens // seq, N_HEADS),
        out_shape=(jax.ShapeDtypeStruct(qc.shape, BF16), jax.ShapeDtypeStruct(qc.shape, F32)),
        in_specs=[spec] * 3, out_specs=(spec, spec),
        compiler_params=_params("parallel", "parallel"),
    )(qc, kc, vp)


def _attn_bwd(qc, kc, vp, o, lse, do, seq):
    tokens = qc.shape[0]
    tq = _tile(seq, (256, 128))
    nq = seq // tq

    def body(q_ref, k_ref, v_ref, o_ref, lse_ref, do_ref, dq_ref, dk_ref, dv_ref):
        dk_ref[...] = jnp.zeros(dk_ref.shape, F32)
        dv_ref[...] = jnp.zeros(dv_ref.shape, F32)
        for i in range(nq):
            rows, keys = slice(i * tq, (i + 1) * tq), (i + 1) * tq
            q, dov = q_ref[rows, :], do_ref[rows, :]
            delta = jnp.sum(dov.astype(F32) * o_ref[rows, :].astype(F32), axis=-1, keepdims=True)
            s = _scores(q, k_ref, keys, tq)
            p = jnp.exp(s - jnp.tile(lse_ref[rows, :], (1, keys // HEAD_SLAB)))
            dp = lax.dot_general(dov, v_ref[0:keys, :], _DIMS["nt"], preferred_element_type=F32)
            ds = (p * (dp - delta) * ATTN_SCALE).astype(BF16)
            dq_ref[rows, :] = jnp.dot(ds, k_ref[0:keys, :], preferred_element_type=F32)
            dk_ref[0:keys, :] += lax.dot_general(ds, q, _DIMS["tn"], preferred_element_type=F32)
            dv_ref[0:keys, :] += lax.dot_general(p.astype(BF16), dov, _DIMS["tn"], preferred_element_type=F32)

    spec = pl.BlockSpec((seq, HEAD_SLAB), lambda b, h: (b, h))
    out = jax.ShapeDtypeStruct(qc.shape, F32)
    return pl.pallas_call(
        body, name="attn_bwd", grid=(tokens // seq, N_HEADS),
        out_shape=(out, out, out), in_specs=[spec] * 6, out_specs=(spec, spec, spec),
        compiler_params=_params("parallel", "parallel"),
    )(qc, kc, vp, o, lse, do)


def _adamw(w, g, m, v, name):
    rows, cols = w.shape
    whole = rows * cols * 4 <= ADAMW_WHOLE_BYTES
    tr = rows if whole else _tile(rows, (256, 128, 64, 32, 16, 8))
    c1 = 1.0 - ADAM_B1 ** ADAM_STEP
    c2 = 1.0 - ADAM_B2 ** ADAM_STEP

    def body(w_ref, g_ref, m_ref, v_ref, d_ref, nm_ref, nv_ref):
        gv = g_ref[...]
        nm = ADAM_B1 * m_ref[...] + (1.0 - ADAM_B1) * gv
        nv = ADAM_B2 * v_ref[...] + (1.0 - ADAM_B2) * (gv * gv)
        d_ref[...] = -ADAM_LR * ((nm / c1) / (jnp.sqrt(nv / c2) + ADAM_EPS) + ADAM_WD * w_ref[...])
        nm_ref[...] = nm
        nv_ref[...] = nv

    spec = pl.BlockSpec((tr, cols), lambda i: (i, 0))
    out = jax.ShapeDtypeStruct(w.shape, F32)
    return pl.pallas_call(
        body, name=name, grid=(rows // tr,), out_shape=(out, out, out),
        in_specs=[spec] * 4, out_specs=(spec, spec, spec),
        compiler_params=_params("parallel"),
    )(w, g, m, v)


def _adamw_landed(w, landed, m, v, name):
    rows, cols = w.shape
    tr = _tile(rows, (176, 128, 96, 64, 32, 16, 8))
    c1 = 1.0 - ADAM_B1 ** ADAM_STEP
    c2 = 1.0 - ADAM_B2 ** ADAM_STEP
    n_parts = len(landed)

    def body(*refs):
        w_ref, m_ref, v_ref = refs[:3]
        g_ref, d_ref, nm_ref, nv_ref = refs[3 + n_parts:]
        parts = []
        for x_ref in refs[3:3 + n_parts]:
            acc = x_ref[0].astype(F32)
            for d in range(1, N_DEV):
                acc = acc + x_ref[d].astype(F32)
            parts.append(acc)
        gv = parts[0] if n_parts == 1 else jnp.concatenate(parts, axis=1)
        nm = ADAM_B1 * m_ref[...] + (1.0 - ADAM_B1) * gv
        nv = ADAM_B2 * v_ref[...] + (1.0 - ADAM_B2) * (gv * gv)
        g_ref[...] = gv
        d_ref[...] = -ADAM_LR * ((nm / c1) / (jnp.sqrt(nv / c2) + ADAM_EPS) + ADAM_WD * w_ref[...])
        nm_ref[...] = nm
        nv_ref[...] = nv

    spec = pl.BlockSpec((tr, cols), lambda i: (i, 0))
    out = jax.ShapeDtypeStruct(w.shape, F32)
    return pl.pallas_call(
        body, name=name, grid=(rows // tr,), out_shape=(out, out, out, out),
        in_specs=[spec] * 3 + [pl.BlockSpec((N_DEV, tr, x.shape[2]), lambda i: (0, i, 0)) for x in landed],
        out_specs=(spec, spec, spec, spec),
        compiler_params=_params("parallel"),
    )(w, m, v, *landed)


def _mod_cols(c_all, w_ada, b_cols):
    def body(c_ref, w_ref, b_ref, act_ref, mod_ref):
        cv = c_ref[...]
        act = cv * _sigmoid(cv)
        act_ref[...] = act
        mod_ref[...] = jnp.dot(act.astype(BF16), w_ref[...].astype(BF16),
                               preferred_element_type=F32) + b_ref[...]

    n = w_ada.shape[1]
    return pl.pallas_call(
        body, name="mod_cols",
        out_shape=(jax.ShapeDtypeStruct(c_all.shape, F32), jax.ShapeDtypeStruct((c_all.shape[0], n), F32)),
        compiler_params=pltpu.CompilerParams(vmem_limit_bytes=VMEM_LIMIT),
    )(c_all, w_ada, b_cols)


def _ada_grads(c_act, dmod_all, dmod_cols):
    def body(c_ref, d_ref, dc_ref, gw_ref, gb_ref):
        gw_ref[...] = lax.dot_general(c_ref[...].astype(BF16), dc_ref[...].astype(BF16), _DIMS["tn"],
                                      preferred_element_type=F32)
        gb_ref[...] = _colsum(d_ref[...])

    return pl.pallas_call(
        body, name="ada_grads",
        out_shape=(jax.ShapeDtypeStruct((c_act.shape[1], dmod_cols.shape[1]), F32),
                   jax.ShapeDtypeStruct((1, dmod_all.shape[1]), F32)),
        compiler_params=pltpu.CompilerParams(vmem_limit_bytes=VMEM_LIMIT),
    )(c_act, dmod_all, dmod_cols)


def _flat_rows(a):
    flat = a.reshape(-1)
    pad = (-flat.shape[0]) % (LANES * SUBLANES)
    if pad:
        flat = jnp.pad(flat, (0, pad))
    return flat.reshape(-1, LANES)


def _gather_start(w, groups, tag, after=None):
    shards = [[(w[n] if n in ROW_SHARDED else w[n].T).astype(BF16) for n in names] for names in groups]
    return _exchange_start_groups(shards, f"gather_{tag}_start", after=after)


def _gather_wait(handle, names, tag, after):
    landed = _exchange_wait(handle, f"gather_{tag}_wait", after=after)
    return {n: g.reshape(-1, g.shape[2]) for n, g in zip(names, landed)}


def _scatter_start(grads, names, tag, after=None):
    blocks = [grads[n].reshape(N_DEV, -1, grads[n].shape[1]) for n in names]
    return _exchange_start(blocks, f"scatter_{tag}_start", scatter=True, after=after)


def _scatter_wait(handle, names, tag, after):
    landed = _exchange_wait(handle, f"scatter_{tag}_wait", scatter=True, after=after)
    return {n: [x] for n, x in zip(names, landed)}


def _pack_small(vals):
    return jnp.concatenate([_flat_rows(v.astype(F32)) for v in vals], axis=0)


def _unpack_small(packed, like):
    out, row = [], 0
    for v in like:
        rows = _flat_rows(v).shape[0]
        out.append(packed[row:row + rows].reshape(-1)[:v.size].reshape(v.shape))
        row += rows
    return out


def _lanes128(*parts):
    out = jnp.zeros((HEAD_SLAB,), F32)
    for off, v in parts:
        out = lax.dynamic_update_slice(out, v.reshape(-1).astype(F32), (off,))
    return out.reshape(1, HEAD_SLAB)


def _step(x, c, positions, w, m, v, loss_target):
    nseq, seq, _ = x.shape
    tokens = nseq * seq
    me = _index(_my_pos())
    strip = lambda d: {n: (a[0] if a.ndim > 2 else a) for n, a in d.items()}
    shapes = {n: a.shape for n, a in w.items()}
    w, m, v = strip(w), strip(m), strip(v)

    c_all = _all_gather(c.reshape(-1, LANES), "gather_c").reshape(N_DEV * nseq, D_MODEL)
    n_ada = w["w_ada"].shape[1]
    b_cols = lax.dynamic_slice(w["b_ada"], (0, me * n_ada), (1, n_ada))
    c_act, mod_cols = _mod_cols(c_all, w["w_ada"], b_cols)
    mod_all = _all_gather(mod_cols, "gather_mod")
    mod = lax.dynamic_slice(mod_all, (0, me * nseq, 0), (N_DEV, nseq, n_ada))
    mod = mod.transpose(1, 0, 2).reshape(nseq, 3, 3, 1, D_MODEL)

    (h_f1i, h_f1o, h_mix_in, h_mix, h_f2), tok = _gather_start(
        w, (("w_ffn1_in",), ("w_ffn1_out",), MIXER[:1], MIXER[1:], ("w_ffn2_in", "w_ffn2_out")), "weights",
        after=mod_all)
    started = tok[0:1, 0:1]

    g_q = _lanes128((0, w["q_norm_nope"]), (QK_NOPE, w["q_norm_rope"]))
    g_kn = _lanes128((0, w["k_norm_nope"]))
    g_kr = _lanes128((QK_NOPE, w["k_norm_rope"]))
    freq = ROPE_THETA ** (-jnp.arange(0, QK_ROPE, 2, dtype=F32) / QK_ROPE)
    inv_freq = _lanes128((QK_NOPE, jnp.concatenate([freq, freq])))
    pos = positions.reshape(tokens, 1).astype(F32)

    def sub(k, gamma, coef):
        return dict(gamma=w[gamma], shift=mod[:, k, 0] + started, scale=mod[:, k, 1], gate=coef * mod[:, k, 2])
    p1, pm, p2 = sub(0, "norm_ffn1", 0.5), sub(1, "norm_mix", 1.0), sub(2, "norm_ffn2", 0.5)
    t_big = _tile(tokens, (2048, 1024, 512))
    t_mid = _tile(tokens, (1024, 512))

    x0 = x.reshape(tokens, D_MODEL)
    h1 = _norm_mod_fwd(x0, p1, seq, "ffn1_norm")
    wt_f1i = _gather_wait(h_f1i, ("w_ffn1_in",), "ffn1_in", h1)["w_ffn1_in"]
    g1, u1, a1 = _ffn_in_act(h1, wt_f1i, "ffn1_in")
    w_f1o = _gather_wait(h_f1o, ("w_ffn1_out",), "ffn1_out", a1)["w_ffn1_out"]
    x1, f1, h2 = _out_residual(a1, w_f1o, x0, p1["gate"], pm, seq, "ffn1_out")
    saved1 = (x0, h1, g1, u1, a1, wt_f1i, w_f1o)

    wt_in = _gather_wait(h_mix_in, MIXER[:1], "mix_in", h2)["w_in"]
    zero_rows = lambda rows: jnp.zeros((rows, D_MODEL), BF16)
    wt_p = wt_in[:512]
    wt_a = jnp.concatenate([wt_in[512:1152], zero_rows(QK_NOPE), wt_in[1152:1184], zero_rows(32)], axis=0)
    wt_g = wt_in[1184:]
    z_a = _mm(h2, wt_a, "nt", "mix_in_a", tm=t_big, tn=wt_a.shape[0])
    z_p = _mm(h2, wt_p, "nt", "mix_in_p", tm=t_big, tn=512)
    z_g = _mm(h2, wt_g, "nt", "mix_in_g", tm=t_big, tn=512)

    full = _gather_wait(h_mix, MIXER[1:], "mix", z_g)
    wtq_pad = jnp.pad(full["w_q_up"].reshape(N_HEADS, 96, Q_LORA), ((0, 0), (0, 32), (0, 0))).reshape(-1, Q_LORA)
    wtmla_pad = jnp.pad(full["w_mla_proj"].reshape(D_MODEL, N_HEADS, 64), ((0, 0), (0, 0), (64, 0))).reshape(D_MODEL, -1)
    wt_pool, wt_kv, w_mix_out = full["w_pool_proj"], full["w_kv_up"], full["w_out"]
    pooled, pg, ps = _pool_fwd(z_p, w["pool_grp"], w["pool_scale"], seq)
    br_pool = _mm(ps, wt_pool, "nt", "pool_proj", tm=t_big, tn=D_MODEL)
    qn, kvn = _latent_norm_fwd(z_a, w["q_a_norm"], w["kv_a_norm"], seq)
    qp = _mm(qn, wtq_pad, "nt", "q_up", tm=t_big, tn=D_MODEL)
    kv = _mm(kvn, wt_kv, "nt", "kv_up", tm=t_big, tn=D_MODEL)
    qc, kc, vp = _qk_prep_fwd(qp, kv, z_a, pos, g_q, g_kn, g_kr, inv_freq, seq)
    attn, lse = _attn_fwd(qc, kc, vp, seq)
    br_mla = _mm(attn, wtmla_pad, "nt", "mla_proj", tm=t_mid, tn=D_MODEL)

    def merge(rows):
        zg, bp, bm = rows[:3]
        return (_sigmoid(zg[:, :D_MODEL]) * bp + _sigmoid(zg[:, D_MODEL:]) * bm).astype(BF16)
    x2, o_mix, h3, merged = _out_residual([z_g, br_pool, br_mla], w_mix_out, x1, pm["gate"], p2, seq, "mix_out",
                                          lhs=merge)

    ffn2_w = _gather_wait(h_f2, ("w_ffn2_in", "w_ffn2_out"), "ffn2", h3)
    g2, u2, a2 = _ffn_in_act(h3, ffn2_w["w_ffn2_in"], "ffn2_in")
    dy, df2, dgate2, sq_err = _out_loss(a2, ffn2_w["w_ffn2_out"], x2, p2["gate"],
                                        loss_target.reshape(tokens, D_MODEL), seq, "ffn2_out")
    saved2 = (x2, h3, g2, u2, a2, ffn2_w["w_ffn2_in"], ffn2_w["w_ffn2_out"])

    grads = {}
    (dx2, do_mix, dsh2, dsc2, dgate_m, dg_ffn2), ops2 = _ffn_bwd_x(df2, dy, saved2, p2, seq, "ffn2", (o_mix, pm["gate"]))
    grads["w_ffn2_out"], grads["w_ffn2_in"] = _ffn_bwd_wout(ops2, "ffn2"), _ffn_bwd_win(ops2, "ffn2")
    s_f2, tok = _scatter_start(grads, ("w_ffn2_in", "w_ffn2_out"), "ffn2")

    grads["w_out"] = _mm(merged, do_mix, "tn", "mix_bwd_wout", out_dtype=BF16, tm=512, tn=D_MODEL)

    def merge_bwd(rows, bats, vecs):
        dmv, zg, bp, bm = rows
        s_p, s_m = _sigmoid(zg[:, :D_MODEL]), _sigmoid(zg[:, D_MODEL:])
        dzg = jnp.concatenate([dmv * bp * s_p * (1.0 - s_p), dmv * bm * s_m * (1.0 - s_m)], axis=1)
        return [dmv * s_p, dmv * s_m, dzg], [], []
    dbr_pool, dbr_mla, dz_g = _rowmap("mix_bwd_dmerged", merge_bwd, seq, [do_mix, z_g, br_pool, br_mla],
                                      row_outs=[(D_MODEL, BF16), (D_MODEL, BF16), (2 * D_MODEL, BF16)],
                                      mm=(w_mix_out, "nt"))

    grads["w_pool_proj"] = _mm(dbr_pool, ps, "tn", "pool_bwd_wproj", out_dtype=BF16, tm=512, tn=POOL_WIDTH)
    dps = _mm(dbr_pool, wt_pool, "nn", "pool_bwd_dps", tm=t_big, tn=POOL_WIDTH)
    dz_p, dgrp, dpool_scale = _pool_bwd(dps, pooled, pg, w["pool_grp"], w["pool_scale"] + tok[0:1, 0:1], seq)

    dwtmla_pad = _mm(dbr_mla, attn, "tn", "mla_bwd_wproj", out_dtype=BF16, tm=512, tn=D_MODEL)
    grads["w_mla_proj"] = dwtmla_pad.reshape(D_MODEL, N_HEADS, HEAD_SLAB)[:, :, 64:].reshape(D_MODEL, -1)
    d_attn = _mm(dbr_mla, wtmla_pad, "nn", "mla_bwd_dattn", out_dtype=BF16, tm=t_mid, tn=D_MODEL)
    dqc, dkc, dvp = _attn_bwd(qc, kc, vp, attn, lse, d_attn, seq)
    dqp, dkv, dkr, dg_q, dg_kn, dg_kr = _qk_prep_bwd(dqc, dkc, dvp, qp, kv, z_a, pos, g_q, g_kn, g_kr, inv_freq, seq)
    dwtq_pad = _mm(dqp, qn, "tn", "q_up_bwd_w", out_dtype=BF16, tm=512, tn=Q_LORA)
    grads["w_q_up"] = dwtq_pad.reshape(N_HEADS, HEAD_SLAB, Q_LORA)[:, :96].reshape(-1, Q_LORA)
    grads["w_kv_up"] = _mm(dkv, kvn, "tn", "kv_up_bwd_w", out_dtype=BF16, tm=512, tn=KV_LORA)
    dqn = _mm(dqp, wtq_pad, "nn", "q_up_bwd_x", tm=t_big, tn=Q_LORA)
    dkvn = _mm(dkv, wt_kv, "nn", "kv_up_bwd_x", tm=t_big, tn=KV_LORA)
    dz_a, dg_qa, dg_kva = _latent_norm_bwd(dqn, dkvn, dkr, z_a, w["q_a_norm"], w["kv_a_norm"], seq)

    dwt_a = _mm(dz_a, h2, "tn", "mix_in_bwd_wa", out_dtype=BF16, tm=256, tn=D_MODEL)
    dwt_p = _mm(dz_p, h2, "tn", "mix_in_bwd_wp", out_dtype=BF16, tm=512, tn=D_MODEL)
    dwt_g = _mm(dz_g, h2, "tn", "mix_in_bwd_wg", out_dtype=BF16, tm=512, tn=D_MODEL)
    grads["w_in"] = jnp.concatenate([dwt_p, dwt_a[:640], dwt_a[704:736], dwt_g], axis=0)
    s_mix, tok = _scatter_start(grads, MIXER, "mix")
    dh2 = [(dz_a, wt_a), (dz_p, wt_p), (dz_g, wt_g)]
    pm_tied = dict(pm, scale=pm["scale"] + tok[0:1, 0:1])
    dx1, df1, dsh_m, dsc_m, dgate1, dg_mix = _norm_mod_bwd(dh2, x1, dx2, pm_tied, seq, "mix_bwd_norm", (f1, p1["gate"]))

    small_early = [dg_mix.reshape(w["norm_mix"].shape), dg_ffn2.reshape(w["norm_ffn2"].shape), dgrp, dpool_scale,
                   dg_qa, dg_kva, dg_q[:, :QK_NOPE], dg_q[:, QK_NOPE:QK_NOPE + QK_ROPE], dg_kn[:, :QK_NOPE],
                   dg_kr[:, QK_NOPE:QK_NOPE + QK_ROPE]]
    s_small, tok = _exchange_start([_pack_small(small_early)], "gather_small_start", after=tok)

    handles = {}

    def ffn1_mid(operands):
        grads["w_ffn1_out"] = _ffn_bwd_wout(operands, "ffn1")
        handles["f1o"], token = _scatter_start(grads, ("w_ffn1_out",), "ffn1_out")
        first = _ffn_bwd_win(operands, "ffn1", after=token, half=0)
        handles["f1i0"], token = _exchange_start([first.reshape(N_DEV, -1, first.shape[1])],
                                                 "scatter_ffn1_in0_start", scatter=True, after=token)
        return token

    p1_tied = dict(p1, scale=p1["scale"] + tok[0:1, 0:1])
    (dx0, dsh1, dsc1, dg_ffn1), ops1 = _ffn_bwd_x(df1, dx1, saved1, p1_tied, seq, "ffn1", mid=ffn1_mid)
    s_f1o = handles["f1o"]

    dmod = jnp.stack([jnp.stack([dsh1, dsc1, 0.5 * dgate1], axis=1),
                      jnp.stack([dsh_m, dsc_m, dgate_m], axis=1),
                      jnp.stack([dsh2, dsc2, 0.5 * dgate2], axis=1)], axis=1)
    n_dmod = nseq * 9 * D_MODEL // LANES
    tail = _all_gather(jnp.concatenate([dmod.reshape(-1, LANES), _flat_rows(dg_ffn1), _flat_rows(sq_err)], axis=0),
                       "gather_dmod")
    dmod_all = tail[:, :n_dmod].reshape(N_DEV * nseq, 9 * D_MODEL)

    second = _ffn_bwd_win(ops1, "ffn1", after=tail, half=1)
    s_second, tok = _exchange_start([second.reshape(N_DEV, -1, second.shape[1])], "scatter_ffn1_in1_start",
                                    scatter=True, after=tail)
    s_f1i = (handles["f1i0"], s_second)

    dmod_cols = lax.dynamic_slice(dmod_all, (0, me * n_ada), (N_DEV * nseq, n_ada)) + tok[0:1, 0:1]
    g_w_ada, g_b_ada = _ada_grads(c_act, dmod_all, dmod_cols)
    tail_sum = _sum_blocks(tail[:, n_dmod:], "sum_tail")
    g_norm_ffn1 = tail_sum[:SUBLANES].reshape(1, D_MODEL)
    loss = 0.5 * jnp.sum(tail_sum[SUBLANES:]) * (1.0 / D_MODEL)
    small_all = _exchange_wait(s_small, "gather_small_wait", after=g_b_ada)[0]
    small_sum = _sum_blocks(small_all, "sum_small")
    small = dict(zip(SMALL[1:], _unpack_small(small_sum, [w[n] for n in SMALL[1:]])))
    grad_w = dict(small, w_ada=g_w_ada, b_ada=g_b_ada, norm_ffn1=g_norm_ffn1)

    delta, new_m, new_v = {}, {}, {}

    def update(names, landed=None):
        for n in names:
            if landed is None:
                delta[n], new_m[n], new_v[n] = _adamw(w[n], grad_w[n], m[n], v[n], f"adamw_{n}")
            elif n in KEPT_TRANSPOSED:
                res = _adamw_landed(w[n].T, landed[n], m[n].T, v[n].T, f"adamw_{n}")
                grad_w[n], delta[n], new_m[n], new_v[n] = (r.T for r in res)
            elif n in ROW_SHARDED:
                grad_w[n], delta[n], new_m[n], new_v[n] = _adamw_landed(w[n], landed[n], m[n], v[n], f"adamw_{n}")
            else:
                grad_w[n] = _sum_blocks(landed[n][0], f"sum_{n}").T
                delta[n], new_m[n], new_v[n] = _adamw(w[n], grad_w[n], m[n], v[n], f"adamw_{n}")

    update(("w_ada",))
    rep = ("b_ada",) + SMALL
    d_s, m_s, v_s = _adamw(_pack_small([w[n] for n in rep]), _pack_small([grad_w[n] for n in rep]),
                           _pack_small([m[n] for n in rep]), _pack_small([v[n] for n in rep]), "adamw_small")
    like = [w[n] for n in rep]
    for dst, packed in ((delta, d_s), (new_m, m_s), (new_v, v_s)):
        dst.update(zip(rep, _unpack_small(packed, like)))
    update(("w_ffn2_in", "w_ffn2_out"), _scatter_wait(s_f2, ("w_ffn2_in", "w_ffn2_out"), "ffn2", after=d_s))
    update(MIXER, _scatter_wait(s_mix, MIXER, "mix", after=delta["w_ffn2_out"]))
    update(("w_ffn1_out",), _scatter_wait(s_f1o, ("w_ffn1_out",), "ffn1_out", after=delta["w_out"]))
    halves = [_exchange_wait(h, f"scatter_ffn1_in{i}_wait", scatter=True, after=delta["w_ffn1_out"])[0]
              for i, h in enumerate(s_f1i)]
    update(("w_ffn1_in",), {"w_ffn1_in": halves})

    lead = lambda d: [d[n].reshape(shapes[n]) for n in WEIGHTS]
    return (loss, dx0.reshape(x.shape), *lead(grad_w), *lead(delta), *lead(new_m), *lead(new_v))


def kernel(x, c, positions, w_ada, b_ada, norm_ffn1, w_ffn1_in, w_ffn1_out, norm_mix, w_in, pool_grp, pool_scale, w_pool_proj, q_a_norm, w_q_up, kv_a_norm, w_kv_up, q_norm_nope, q_norm_rope, k_norm_nope, k_norm_rope, w_mla_proj, w_out, norm_ffn2, w_ffn2_in, w_ffn2_out, loss_target, m_w_ada, m_b_ada, m_norm_ffn1, m_w_ffn1_in, m_w_ffn1_out, m_norm_mix, m_w_in, m_pool_grp, m_pool_scale, m_w_pool_proj, m_q_a_norm, m_w_q_up, m_kv_a_norm, m_w_kv_up, m_q_norm_nope, m_q_norm_rope, m_k_norm_nope, m_k_norm_rope, m_w_mla_proj, m_w_out, m_norm_ffn2, m_w_ffn2_in, m_w_ffn2_out, v_w_ada, v_b_ada, v_norm_ffn1, v_w_ffn1_in, v_w_ffn1_out, v_norm_mix, v_w_in, v_pool_grp, v_pool_scale, v_w_pool_proj, v_q_a_norm, v_w_q_up, v_kv_a_norm, v_w_kv_up, v_q_norm_nope, v_q_norm_rope, v_k_norm_nope, v_k_norm_rope, v_w_mla_proj, v_w_out, v_norm_ffn2, v_w_ffn2_in, v_w_ffn2_out):
    w = dict(w_ada=w_ada, b_ada=b_ada, norm_ffn1=norm_ffn1, w_ffn1_in=w_ffn1_in, w_ffn1_out=w_ffn1_out, norm_mix=norm_mix, w_in=w_in, pool_grp=pool_grp, pool_scale=pool_scale, w_pool_proj=w_pool_proj, q_a_norm=q_a_norm, w_q_up=w_q_up, kv_a_norm=kv_a_norm, w_kv_up=w_kv_up, q_norm_nope=q_norm_nope, q_norm_rope=q_norm_rope, k_norm_nope=k_norm_nope, k_norm_rope=k_norm_rope, w_mla_proj=w_mla_proj, w_out=w_out, norm_ffn2=norm_ffn2, w_ffn2_in=w_ffn2_in, w_ffn2_out=w_ffn2_out)
    m = dict(w_ada=m_w_ada, b_ada=m_b_ada, norm_ffn1=m_norm_ffn1, w_ffn1_in=m_w_ffn1_in, w_ffn1_out=m_w_ffn1_out, norm_mix=m_norm_mix, w_in=m_w_in, pool_grp=m_pool_grp, pool_scale=m_pool_scale, w_pool_proj=m_w_pool_proj, q_a_norm=m_q_a_norm, w_q_up=m_w_q_up, kv_a_norm=m_kv_a_norm, w_kv_up=m_w_kv_up, q_norm_nope=m_q_norm_nope, q_norm_rope=m_q_norm_rope, k_norm_nope=m_k_norm_nope, k_norm_rope=m_k_norm_rope, w_mla_proj=m_w_mla_proj, w_out=m_w_out, norm_ffn2=m_norm_ffn2, w_ffn2_in=m_w_ffn2_in, w_ffn2_out=m_w_ffn2_out)
    v = dict(w_ada=v_w_ada, b_ada=v_b_ada, norm_ffn1=v_norm_ffn1, w_ffn1_in=v_w_ffn1_in, w_ffn1_out=v_w_ffn1_out, norm_mix=v_norm_mix, w_in=v_w_in, pool_grp=v_pool_grp, pool_scale=v_pool_scale, w_pool_proj=v_w_pool_proj, q_a_norm=v_q_a_norm, w_q_up=v_w_q_up, kv_a_norm=v_kv_a_norm, w_kv_up=v_w_kv_up, q_norm_nope=v_q_norm_nope, q_norm_rope=v_q_norm_rope, k_norm_nope=v_k_norm_nope, k_norm_rope=v_k_norm_rope, w_mla_proj=v_w_mla_proj, w_out=v_w_out, norm_ffn2=v_norm_ffn2, w_ffn2_in=v_w_ffn2_in, w_ffn2_out=v_w_ffn2_out)
    return _step(x, c, positions, w, m, v, loss_target)
```

```python
import functools
import math

import jax
import jax.numpy as jnp
from jax import lax
from jax.experimental import pallas as pl
from jax.experimental.pallas import tpu as pltpu

F32 = jnp.float32
BF16 = jnp.bfloat16
MESH = pl.DeviceIdType.MESH
AXES = ("x", "y", "c")
N_DEV = 8

D_MODEL = 1024
D_FF = 2816
N_HEADS = 8
HEAD_SLAB = 128
QK_NOPE = 64
QK_ROPE = 32
POOL_WIDTH = 512
POOL_GROUPS = 4
POOL_GROUP_DIM = 128
Q_LORA = 384
KV_LORA = 256
ROPE_THETA = 10000.0
ATTN_SCALE = 1.0 / math.sqrt(QK_NOPE + QK_ROPE)
NORM_EPS = 1e-6
ADAM_LR, ADAM_B1, ADAM_B2, ADAM_EPS, ADAM_WD, ADAM_STEP = 0.001, 0.9, 0.999, 1e-08, 0.01, 10

LANES = 128
SUBLANES = 8
VMEM_LIMIT = 52 * 1024 * 1024
ADAMW_WHOLE_BYTES = 3 << 19
SUM_WHOLE_BYTES = 4 << 20

BIG = ("w_ffn1_in", "w_ffn1_out", "w_in", "w_pool_proj", "w_q_up", "w_kv_up",
       "w_mla_proj", "w_out", "w_ffn2_in", "w_ffn2_out")
ROW_SHARDED = ("w_ffn1_out", "w_out", "w_ffn2_out")
MIXER = ("w_in", "w_pool_proj", "w_q_up", "w_kv_up", "w_mla_proj", "w_out")
KEPT_TRANSPOSED = ("w_ffn1_in", "w_ffn2_in", "w_in", "w_q_up")
SMALL = ("norm_ffn1", "norm_mix", "norm_ffn2", "pool_grp", "pool_scale", "q_a_norm",
         "kv_a_norm", "q_norm_nope", "q_norm_rope", "k_norm_nope", "k_norm_rope")
WEIGHTS = ("w_ada", "b_ada", "norm_ffn1", "w_ffn1_in", "w_ffn1_out", "norm_mix", "w_in",
           "pool_grp", "pool_scale", "w_pool_proj", "q_a_norm", "w_q_up", "kv_a_norm",
           "w_kv_up", "q_norm_nope", "q_norm_rope", "k_norm_nope", "k_norm_rope",
           "w_mla_proj", "w_out", "norm_ffn2", "w_ffn2_in", "w_ffn2_out")


def _params(*sem):
    return pltpu.CompilerParams(dimension_semantics=sem, vmem_limit_bytes=VMEM_LIMIT)


def _tile(n, cands):
    for c in cands:
        if n % c == 0:
            return c
    return n


def _my_pos():
    return lax.axis_index("x"), lax.axis_index("y"), lax.axis_index("c")


def _flip(pos, k):
    x, y, c = pos
    fx, fy, fc = (k >> 2) & 1, (k >> 1) & 1, k & 1
    return ((1 - x) if fx else x, (1 - y) if fy else y, (1 - c) if fc else c)


def _index(pos):
    x, y, c = pos
    return 4 * x + 2 * y + c


def _exchange(arrays, name, scatter=False):
    n = len(arrays)

    def body(*refs):
        ins, outs = refs[:n], refs[n:2 * n]
        send_sems, recv_sems, local_sems = refs[2 * n:]
        me = _my_pos()
        mine, sends = [], []
        for a in range(n):
            own = ins[a].at[_index(me)] if scatter else ins[a]
            cp = pltpu.make_async_copy(own, outs[a].at[_index(me)], local_sems.at[a])
            cp.start()
            mine.append(cp)
        for k in range(1, N_DEV):
            peer = _flip(me, k)
            for a in range(n):
                cp = pltpu.make_async_remote_copy(
                    src_ref=ins[a].at[_index(peer)] if scatter else ins[a],
                    dst_ref=outs[a].at[_index(me)],
                    send_sem=send_sems.at[a, k - 1], recv_sem=recv_sems.at[a, k - 1],
                    device_id=peer, device_id_type=MESH)
                cp.start()
                sends.append(cp)
        for k in range(1, N_DEV):
            peer = _flip(me, k)
            for a in range(n):
                pltpu.make_async_remote_copy(
                    src_ref=ins[a].at[_index(me)] if scatter else ins[a],
                    dst_ref=outs[a].at[_index(peer)],
                    send_sem=send_sems.at[a, k - 1], recv_sem=recv_sems.at[a, k - 1],
                    device_id=peer, device_id_type=MESH).wait_recv()
        for cp in sends:
            cp.wait_send()
        for cp in mine:
            cp.wait()

    shape = lambda x: x.shape if scatter else (N_DEV,) + x.shape
    return pl.pallas_call(
        body, name=name,
        out_shape=tuple(jax.ShapeDtypeStruct(shape(x), x.dtype) for x in arrays),
        in_specs=[pl.BlockSpec(memory_space=pl.ANY)] * n,
        out_specs=tuple(pl.BlockSpec(memory_space=pl.ANY) for _ in arrays),
        scratch_shapes=[pltpu.SemaphoreType.DMA((n, N_DEV - 1)),
                        pltpu.SemaphoreType.DMA((n, N_DEV - 1)),
                        pltpu.SemaphoreType.DMA((n,))],
    )(*arrays)


def _all_gather(x, name):
    return _exchange([x], name)[0]


_HBM = pl.BlockSpec(memory_space=pltpu.HBM)
_SEM = pl.BlockSpec(memory_space=pltpu.SEMAPHORE)
_ANY = pl.BlockSpec(memory_space=pl.ANY)
_EFFECT = pltpu.SideEffectType.DATAFLOW_SIDE_EFFECTING


def _split_copy(ins, lands, send_sems, recv_sems, a, k, me, scatter, incoming):
    peer = _flip(me, k)
    block = me if incoming else peer
    return pltpu.make_async_remote_copy(
        src_ref=ins[a].at[_index(block)] if scatter else ins[a],
        dst_ref=lands[a].at[_index(peer if incoming else me)],
        send_sem=send_sems.at[a * (N_DEV - 1) + k - 1], recv_sem=recv_sems.at[a * (N_DEV - 1) + k - 1],
        device_id=peer, device_id_type=MESH)


ALL_PEERS = tuple(range(1, N_DEV))
CHIP_PEERS = (1, 2, 4, 6)


def _exchange_start_groups(groups, name, scatter=False, after=None, peers=None):
    peers = peers or [ALL_PEERS] * len(groups)
    sizes = [len(g) for g in groups]
    first = [sum(sizes[:i]) for i in range(len(sizes))]
    n, ng = sum(sizes), len(sizes)
    after = jnp.zeros((SUBLANES, LANES), F32) if after is None else after

    def body(*refs):
        ins, lands = refs[:n], refs[n:2 * n]
        sems = refs[2 * n + 1:2 * n + 1 + 2 * ng]
        me = _my_pos()
        for g in range(ng):
            part = slice(first[g], first[g] + sizes[g])
            for k in peers[g]:
                for a in range(sizes[g]):
                    _split_copy(ins[part], lands[part], sems[2 * g], sems[2 * g + 1], a, k, me, scatter, False).start()
        refs[-1][...] = jnp.zeros((SUBLANES, LANES), F32)

    shape = lambda x: x.shape if scatter else (N_DEV,) + x.shape
    hbm = lambda x: pltpu.with_memory_space_constraint(x, pltpu.HBM)
    srcs = [hbm(x) for g in groups for x in g]
    zones = [hbm(lax.empty(shape(x), x.dtype)) for g in groups for x in g]
    sem_shapes = [pltpu.SemaphoreType.DMA((s * (N_DEV - 1),)) for s in sizes for _ in range(2)]
    out = pl.pallas_call(
        body, name=name,
        out_shape=(*sem_shapes, *[pltpu.HBM(x.shape, x.dtype) for x in srcs + zones],
                   jax.ShapeDtypeStruct((SUBLANES, LANES), F32)),
        in_specs=[_HBM] * (2 * n) + [_ANY],
        out_specs=(*[_SEM] * (2 * ng), *[_HBM] * (2 * n), pl.BlockSpec(memory_space=pltpu.VMEM)),
        input_output_aliases={i: 2 * ng + i for i in range(2 * n)},
        compiler_params=pltpu.CompilerParams(has_side_effects=_EFFECT),
    )(*srcs, *zones, after)
    bufs = out[2 * ng:-1]
    handles = [(out[2 * g], out[2 * g + 1], *bufs[first[g]:first[g] + sizes[g]],
                *bufs[n + first[g]:n + first[g] + sizes[g]]) for g in range(ng)]
    return handles, out[-1]


def _exchange_start(arrays, name, scatter=False, after=None):
    handles, token = _exchange_start_groups([arrays], name, scatter, after)
    return handles[0], token


def _exchange_wait(handle, name, scatter=False, after=None, peers=ALL_PEERS):
    send_sems, recv_sems = handle[0], handle[1]
    n = (len(handle) - 2) // 2
    after = jnp.zeros((SUBLANES, LANES), F32) if after is None else after

    def body(*refs):
        ins, lands = refs[:n], refs[n:2 * n]
        send, recv = refs[2 * n], refs[2 * n + 1]
        me = _my_pos()
        for k in peers:
            for a in range(n):
                _split_copy(ins, lands, send, recv, a, k, me, scatter, False).wait_send()
                _split_copy(ins, lands, send, recv, a, k, me, scatter, True).wait_recv()

    bufs = handle[2:]
    out = pl.pallas_call(
        body, name=name,
        out_shape=tuple(pltpu.HBM(x.shape, x.dtype) for x in bufs),
        in_specs=[_HBM] * (2 * n) + [_SEM, _SEM, _ANY],
        out_specs=tuple([_HBM] * (2 * n)),
        input_output_aliases={i: i for i in range(2 * n)},
        compiler_params=pltpu.CompilerParams(has_side_effects=_EFFECT),
    )(*bufs, send_sems, recv_sems, after)
    me = _index(_my_pos())
    landed = []
    for src, land in zip(out[:n], out[n:]):
        own = lax.dynamic_slice_in_dim(src, me, 1, axis=0) if scatter else src[None]
        landed.append(lax.dynamic_update_slice_in_dim(land, own, me, axis=0))
    return landed


def _sibling_forward(x, name):
    flips = [k for k in CHIP_PEERS if k != 1]

    def body(x_ref, o_ref, send_sems, recv_sems):
        me = _my_pos()
        sibling = _flip(me, 1)
        sends = []
        for i, k in enumerate(flips):
            block = o_ref.at[_index(_flip(me, k))]
            cp = pltpu.make_async_remote_copy(src_ref=block, dst_ref=block, send_sem=send_sems.at[i],
                                              recv_sem=recv_sems.at[i], device_id=sibling, device_id_type=MESH)
            cp.start()
            sends.append(cp)
        for i, k in enumerate(flips):
            block = o_ref.at[_index(_flip(sibling, k))]
            pltpu.make_async_remote_copy(src_ref=block, dst_ref=block, send_sem=send_sems.at[i],
                                         recv_sem=recv_sems.at[i], device_id=sibling, device_id_type=MESH).wait_recv()
        for cp in sends:
            cp.wait_send()

    return pl.pallas_call(
        body, name=name, out_shape=jax.ShapeDtypeStruct(x.shape, x.dtype),
        in_specs=[_ANY], out_specs=_ANY, input_output_aliases={0: 0},
        scratch_shapes=[pltpu.SemaphoreType.DMA((len(flips),)), pltpu.SemaphoreType.DMA((len(flips),))],
    )(x)


def _sum_blocks(x, name):
    n, rows, cols = x.shape
    whole = x.size * x.dtype.itemsize <= SUM_WHOLE_BYTES
    tr = rows if whole else _tile(rows, (512, 256, 128, 64, 32, 16, 8))

    def body(x_ref, o_ref):
        acc = x_ref[0].astype(F32)
        for d in range(1, n):
            acc = acc + x_ref[d].astype(F32)
        o_ref[...] = acc

    return pl.pallas_call(
        body, name=name,
        out_shape=jax.ShapeDtypeStruct((rows, cols), F32),
        grid=(rows // tr,),
        in_specs=[pl.BlockSpec((n, tr, cols), lambda i: (0, i, 0))],
        out_specs=pl.BlockSpec((tr, cols), lambda i: (i, 0)),
        compiler_params=_params("parallel"),
    )(x)


_DIMS = {"nn": (((1,), (0,)), ((), ())), "nt": (((1,), (1,)), ((), ())), "tn": (((0,), (0,)), ((), ()))}


def _mm(a, b, mode, name, out_dtype=F32, tm=None, tn=None, add=None, after=None, b_cols=None):
    if mode == "tn":
        kdim, m = a.shape
    else:
        m, kdim = a.shape
    n = b.shape[0] if mode == "nt" else b.shape[1]
    tm = tm or _tile(m, (512, 256, 128))
    tn = tn or _tile(n, (512, 256, 128))
    j0 = 0
    if b_cols is not None:
        j0, n = b_cols[0], b_cols[1] * tn
    dims = _DIMS[mode]

    def body(*refs):
        refs = refs if after is None else refs[1:]
        acc = lax.dot_general(refs[0][...].astype(BF16), refs[1][...].astype(BF16), dims,
                              preferred_element_type=F32)
        if add is not None:
            acc = acc + refs[2][...]
        refs[-1][...] = acc.astype(out_dtype)

    a_spec = (pl.BlockSpec((kdim, tm), lambda i, j: (0, i)) if mode == "tn"
              else pl.BlockSpec((tm, kdim), lambda i, j: (i, 0)))
    b_spec = (pl.BlockSpec((tn, kdim), lambda i, j: (j, 0)) if mode == "nt"
              else pl.BlockSpec((kdim, tn), lambda i, j: (0, j + j0)))
    o_spec = pl.BlockSpec((tm, tn), lambda i, j: (i, j))
    in_specs, args = [a_spec, b_spec], [a, b]
    if add is not None:
        in_specs.append(o_spec)
        args.append(add)
    if after is not None:
        in_specs.insert(0, _ANY)
        args.insert(0, after)
    return pl.pallas_call(
        body, name=name, out_shape=jax.ShapeDtypeStruct((m, n), out_dtype), grid=(m // tm, n // tn),
        in_specs=in_specs, out_specs=o_spec,
        compiler_params=_params("parallel", "parallel"),
    )(*args)


def _rowmap(name, fn, seq, rows, bats=(), vecs=(), row_outs=(), bat_outs=(), vec_outs=(), ts=None, mm=None, lhs=None,
            after=None):
    mms = [] if mm is None else (mm if isinstance(mm, list) else [mm])
    rows = [r if isinstance(r, tuple) else (r, r.shape[1], 0) for r in rows]
    tokens = rows[0][0].shape[0]
    nseq = tokens // seq
    ts = ts or _tile(seq, (512, 256, 128, 64, 32, 16, 8))
    nt = seq // ts
    n_r, n_b, n_v = len(rows), len(bats), len(vecs)
    n_ro, n_bo = len(row_outs), len(bat_outs)

    def accumulate(ref, val, first):
        @pl.when(first)
        def _():
            ref[...] = val.reshape(ref.shape)

        @pl.when(jnp.logical_not(first))
        def _():
            ref[...] += val.reshape(ref.shape)

    def body(*refs):
        n_in = n_r + n_b + n_v + len(mms) + (after is not None)
        ins, outs = refs[:n_in], refs[n_in:]
        b_vals = [r[0] for r in ins[n_r:n_r + n_b]]
        v_vals = [r[...] for r in ins[n_r + n_b:n_r + n_b + n_v]]
        r_vals = [r[...] for r in ins[:n_r]]
        if mms:
            lefts = r_vals[:len(mms)] if lhs is None else [lhs(r_vals)]
            acc = None
            for left, b_ref, (_, mode) in zip(lefts, ins[n_r + n_b + n_v:], mms):
                part = lax.dot_general(left.astype(BF16), b_ref[...].astype(BF16), _DIMS[mode],
                                       preferred_element_type=F32)
                acc = part if acc is None else acc + part
            r_vals = [acc] + r_vals[len(mms):] if lhs is None else [acc, lefts[0]] + r_vals
        ro, bo, vo = fn(r_vals, b_vals, v_vals)
        for ref, val in zip(outs[:n_ro], ro):
            ref[...] = val.astype(ref.dtype)
        b, i = pl.program_id(0), pl.program_id(1)
        for ref, val in zip(outs[n_ro:n_ro + n_bo], bo):
            accumulate(ref, val, i == 0)
        for ref, val in zip(outs[n_ro + n_bo:], vo):
            accumulate(ref, val, jnp.logical_and(i == 0, b == 0))

    in_specs = [pl.BlockSpec((ts, w), functools.partial(lambda b, i, cb: (b * nt + i, cb), cb=cb))
                for _, w, cb in rows]
    in_specs += [pl.BlockSpec((1, 1, v.shape[2]), lambda b, i: (b, 0, 0)) for v in bats]
    in_specs += [pl.BlockSpec((1, v.shape[1]), lambda b, i: (0, 0)) for v in vecs]
    extra = [b_arr for b_arr, _ in mms]
    in_specs += [pl.BlockSpec(b_arr.shape, lambda b, i: (0, 0)) for b_arr in extra]
    if after is not None:
        in_specs.append(_ANY)
        extra.append(after)
    out_shape = [jax.ShapeDtypeStruct((tokens, f), dt) for f, dt in row_outs]
    out_specs = [pl.BlockSpec((ts, f), lambda b, i: (b * nt + i, 0)) for f, _ in row_outs]
    out_shape += [jax.ShapeDtypeStruct((nseq, 1, f), F32) for f in bat_outs]
    out_specs += [pl.BlockSpec((1, 1, f), lambda b, i: (b, 0, 0)) for f in bat_outs]
    out_shape += [jax.ShapeDtypeStruct((1, f), F32) for f in vec_outs]
    out_specs += [pl.BlockSpec((1, f), lambda b, i: (0, 0)) for f in vec_outs]
    return pl.pallas_call(
        body, name=name, out_shape=tuple(out_shape), grid=(nseq, nt),
        in_specs=in_specs, out_specs=tuple(out_specs),
        compiler_params=_params("arbitrary", "arbitrary"),
    )(*([r[0] for r in rows] + list(bats) + list(vecs) + extra))


def _colsum(v):
    return jnp.sum(v, axis=0, keepdims=True)


def _rstd(x, width=None):
    width = width or x.shape[-1]
    return lax.rsqrt(jnp.sum(x * x, axis=-1, keepdims=True) * (1.0 / width) + NORM_EPS)


def _norm_bwd(dy, x, r, g, width=None):
    width = width or x.shape[-1]
    xhat = x * r
    dxhat = dy * g
    dx = r * (dxhat - xhat * (jnp.sum(dxhat * xhat, axis=-1, keepdims=True) * (1.0 / width)))
    return dx, dy * xhat


def _sigmoid(x):
    return 0.5 * jnp.tanh(0.5 * x) + 0.5


def _norm_mod(xv, g, sh, sc):
    return xv * _rstd(xv) * g * (1.0 + sc) + sh


def _norm_mod_fwd(x, p, seq, name):
    def fn(rows, bats, vecs):
        return [_norm_mod(rows[0], vecs[0], bats[0], bats[1])], [], []
    return _rowmap(name, fn, seq, [x], [p["shift"], p["scale"]], [p["gamma"]], row_outs=[(D_MODEL, BF16)])[0]


def _norm_mod_bwd(dh, x, dres, p, seq, name, prev=None):
    products = dh if isinstance(dh, list) else None
    lefts = [l for l, _ in products] if products else [dh]
    def fn(rows, bats, vecs):
        dhv, xv, dr = rows[:3]
        sc, g = bats[0], vecs[0]
        r = _rstd(xv)
        dxn, dg = _norm_bwd(dhv * (1.0 + sc), xv, r, g)
        dx = dr + dxn
        ro, bo = [dx], [_colsum(dhv), _colsum(dhv * (xv * r * g))]
        if prev is not None:
            ro.append(bats[1] * dx)
            bo.append(_colsum(dx * rows[3].astype(F32)))
        return ro, bo, [_colsum(dg)]
    more = prev is not None
    return _rowmap(name, fn, seq, lefts + [x, dres] + ([prev[0]] if more else []),
                   [p["scale"]] + ([prev[1]] if more else []), [p["gamma"]],
                   row_outs=[(D_MODEL, F32)] + ([(D_MODEL, BF16)] if more else []),
                   bat_outs=[D_MODEL] * (3 if more else 2), vec_outs=[D_MODEL],
                   mm=[(r, "nn") for _, r in products] if products else None)


def _ffn_in_act(h, wt_in, name):
    tokens = h.shape[0]
    tm, tn = _tile(tokens, (2048, 1024, 512)), 256
    nj = D_FF // tn

    def body(h_ref, wg_ref, wu_ref, g_ref, u_ref, a_ref):
        hv = h_ref[...]
        g = lax.dot_general(hv, wg_ref[...], _DIMS["nt"], preferred_element_type=F32)
        u = lax.dot_general(hv, wu_ref[...], _DIMS["nt"], preferred_element_type=F32)
        g_ref[...] = g.astype(BF16)
        u_ref[...] = u.astype(BF16)
        a_ref[...] = (g * _sigmoid(g) * u).astype(BF16)

    o_spec = pl.BlockSpec((tm, tn), lambda i, j: (i, j))
    out = jax.ShapeDtypeStruct((tokens, D_FF), BF16)
    return pl.pallas_call(
        body, name=name, grid=(tokens // tm, nj), out_shape=(out, out, out),
        in_specs=[pl.BlockSpec((tm, D_MODEL), lambda i, j: (i, 0)),
                  pl.BlockSpec((tn, D_MODEL), lambda i, j: (j, 0)),
                  pl.BlockSpec((tn, D_MODEL), lambda i, j: (j + nj, 0))],
        out_specs=(o_spec, o_spec, o_spec),
        compiler_params=_params("parallel", "parallel"),
    )(h, wt_in, wt_in)


def _out_residual(a, w_out, res, gate, nxt, seq, name, lhs=None):
    def fn(rows, bats, vecs):
        acc, rv = rows[0], rows[-1]
        x_new = rv + bats[0] * acc
        made = [] if lhs is None else [rows[1]]
        return [x_new, acc, _norm_mod(x_new, vecs[0], bats[1], bats[2])] + made, [], []
    outs = [(D_MODEL, F32), (D_MODEL, BF16), (D_MODEL, BF16)] + ([] if lhs is None else [(D_MODEL, BF16)])
    return _rowmap(name, fn, seq, (a if lhs is not None else [a]) + [res], [gate, nxt["shift"], nxt["scale"]],
                   [nxt["gamma"]], row_outs=outs, ts=_tile(seq, (512, 256, 128)), mm=(w_out, "nn"), lhs=lhs)


def _out_loss(a, w_out, res, gate, target, seq, name):
    def fn(rows, bats, vecs):
        acc, rv, tv = rows
        err = rv + bats[0] * acc - tv
        dy = err * (1.0 / D_MODEL)
        return [dy, bats[0] * dy], [_colsum(dy * acc)], [_colsum(err * err)]
    return _rowmap(name, fn, seq, [a, res, target], [gate], row_outs=[(D_MODEL, F32), (D_MODEL, BF16)],
                   bat_outs=[D_MODEL], vec_outs=[D_MODEL], ts=_tile(seq, (512, 256, 128)), mm=(w_out, "nn"))


def _ffn_bwd_x(df, dres, saved, p, seq, tag, prev=None, early=None, mid=None):
    x, h, g, u, a, w_in, w_out = saved
    tokens = x.shape[0]
    first = None if early is None else early(a, df)

    def act_bwd(rows, bats, vecs):
        dav, gv, uv = rows[0], rows[1].astype(F32), rows[2].astype(F32)
        sg = _sigmoid(gv)
        silu = gv * sg
        dg = dav * uv * (sg * (1.0 + gv * (1.0 - sg)))
        return [jnp.concatenate([dg, dav * silu], axis=1)], [], []
    dgu = _rowmap(f"{tag}_bwd_da", act_bwd, seq, [df, g, u], row_outs=[(2 * D_FF, BF16)],
                  ts=_tile(seq, (256, 128)), mm=(w_out, "nt"), after=first)[0]
    operands = (a, df, dgu, h)
    after = None if mid is None else mid(operands)
    dh = _mm(dgu, w_in, "nn", f"{tag}_bwd_dh", tm=_tile(tokens, (512, 256)), tn=D_MODEL, after=after)
    return _norm_mod_bwd(dh, x, dres, p, seq, f"{tag}_bwd_norm", prev), operands


def _ffn_bwd_wout(a, df, tag):
    return _mm(a, df, "tn", f"{tag}_bwd_wout", out_dtype=BF16, tm=256, tn=D_MODEL)


def _ffn_bwd_win(operands, tag, after=None, half=None):
    _, _, dgu, h = operands
    if half is None:
        return _mm(dgu, h, "tn", f"{tag}_bwd_win", out_dtype=BF16, tm=512, tn=D_MODEL, after=after)
    return _mm(dgu, h, "tn", f"{tag}_bwd_win{half}", out_dtype=BF16, tm=512, tn=D_MODEL // 2, after=after,
               b_cols=(half, 1))


def _shift_rows(v, k, forward):
    n = v.shape[0]
    row = lax.broadcasted_iota(jnp.int32, v.shape, 0)
    if forward:
        return jnp.where(row >= k, pltpu.roll(v, k, 0), 0.0)
    return jnp.where(row < n - k, pltpu.roll(v, n - k, 0), 0.0)


def _window_sums(v, forward):
    out, s, k = [], v, 1
    for _ in range(POOL_GROUPS):
        s = s + _shift_rows(s, k, forward)
        out.append(s)
        k *= 2
    return out


def _by_group(vals, g):
    out = vals[-1]
    for idx in range(len(vals) - 2, -1, -1):
        out = jnp.where(g == idx, vals[idx], out)
    return out


def _inv_count(shape, g):
    t1 = lax.broadcasted_iota(jnp.int32, shape, 0) + 1
    window = _by_group([jnp.int32(2 ** (i + 1)) for i in range(POOL_GROUPS)], g)
    return 1.0 / jnp.minimum(t1, window).astype(F32)


def _pool_fwd(u, grp, scale, seq):
    tokens = u.shape[0]

    def body(u_ref, grp_ref, sc_ref, pooled_ref, pg_ref, ps_ref):
        g = pl.program_id(1)
        uv = u_ref[...]
        sums = _by_group(_window_sums(uv, True), g)
        pooled = (sums * _inv_count(uv.shape, g) - uv).astype(BF16)
        pg = jnp.dot(pooled, grp_ref[0].astype(BF16), preferred_element_type=F32)
        pooled_ref[...] = pooled
        pg_ref[...] = pg
        ps_ref[...] = (pg * sc_ref[...]).astype(BF16)

    blk = pl.BlockSpec((seq, POOL_GROUP_DIM), lambda b, g: (b, g))
    return pl.pallas_call(
        body, name="pool_fwd", grid=(tokens // seq, POOL_GROUPS),
        out_shape=(jax.ShapeDtypeStruct(u.shape, BF16), jax.ShapeDtypeStruct(u.shape, F32),
                   jax.ShapeDtypeStruct(u.shape, BF16)),
        in_specs=[blk, pl.BlockSpec((1, POOL_GROUP_DIM, POOL_GROUP_DIM), lambda b, g: (g, 0, 0)),
                  pl.BlockSpec((1, POOL_GROUP_DIM), lambda b, g: (0, g))],
        out_specs=(blk, blk, blk),
        compiler_params=_params("parallel", "parallel"),
    )(u, grp, scale)


def _pool_bwd(dps, pooled, pg, grp, scale, seq):
    tokens = dps.shape[0]

    def body(dps_ref, pooled_ref, pg_ref, grp_ref, sc_ref, du_ref, dgrp_ref, dsc_ref):
        g, b = pl.program_id(0), pl.program_id(1)
        dpsv = dps_ref[...]
        dpg = (dpsv * sc_ref[...]).astype(BF16)
        dsc = _colsum(dpsv * pg_ref[...])
        dgrp = lax.dot_general(pooled_ref[...], dpg, _DIMS["tn"], preferred_element_type=F32)

        @pl.when(b == 0)
        def _():
            dsc_ref[...] = dsc
            dgrp_ref[0] = dgrp

        @pl.when(b > 0)
        def _():
            dsc_ref[...] += dsc
            dgrp_ref[0] += dgrp

        dpool = lax.dot_general(dpg, grp_ref[0].astype(BF16), _DIMS["nt"], preferred_element_type=F32)
        sums = _by_group(_window_sums(dpool * _inv_count(dpool.shape, g), False), g)
        du_ref[...] = (sums - dpool).astype(BF16)

    blk = pl.BlockSpec((seq, POOL_GROUP_DIM), lambda g, b: (b, g))
    grp_spec = pl.BlockSpec((1, POOL_GROUP_DIM, POOL_GROUP_DIM), lambda g, b: (g, 0, 0))
    vec_spec = pl.BlockSpec((1, POOL_GROUP_DIM), lambda g, b: (0, g))
    return pl.pallas_call(
        body, name="pool_bwd", grid=(POOL_GROUPS, tokens // seq),
        out_shape=(jax.ShapeDtypeStruct(dps.shape, BF16), jax.ShapeDtypeStruct(grp.shape, F32),
                   jax.ShapeDtypeStruct(scale.shape, F32)),
        in_specs=[blk, blk, blk, grp_spec, vec_spec],
        out_specs=(blk, grp_spec, vec_spec),
        compiler_params=_params("arbitrary", "arbitrary"),
    )(dps, pooled, pg, grp, scale)


def _lane(shape):
    return lax.broadcasted_iota(jnp.int32, shape, len(shape) - 1)


def _rot(y):
    lane = _lane(y.shape)
    r = jnp.where(lane < QK_NOPE + QK_ROPE // 2,
                  -pltpu.roll(y, HEAD_SLAB - QK_ROPE // 2, 1), pltpu.roll(y, QK_ROPE // 2, 1))
    return jnp.where(jnp.logical_and(lane >= QK_NOPE, lane < QK_NOPE + QK_ROPE), r, 0.0)


def _part_rstd(x):
    sq = x * x
    nope = _lane(x.shape) < QK_NOPE
    s_nope = jnp.sum(jnp.where(nope, sq, 0.0), axis=-1, keepdims=True)
    s_rope = jnp.sum(sq, axis=-1, keepdims=True) - s_nope
    return jnp.where(nope, lax.rsqrt(s_nope * (1.0 / QK_NOPE) + NORM_EPS),
                     lax.rsqrt(s_rope * (1.0 / QK_ROPE) + NORM_EPS))


def _part_norm_bwd(dy, x, r, g):
    nope = _lane(x.shape) < QK_NOPE
    xhat = x * r
    dxhat = dy * g
    prod = dxhat * xhat
    m_nope = jnp.sum(jnp.where(nope, prod, 0.0), axis=-1, keepdims=True)
    m_rope = jnp.sum(prod, axis=-1, keepdims=True) - m_nope
    mean = jnp.where(nope, m_nope * (1.0 / QK_NOPE), m_rope * (1.0 / QK_ROPE))
    return r * (dxhat - xhat * mean), dy * xhat


def _latent_norm_fwd(z_a, g_q, g_kv, seq):
    def fn(rows, bats, vecs):
        q, kv = rows[0][:, :Q_LORA], rows[0][:, Q_LORA:Q_LORA + KV_LORA]
        return [q * _rstd(q) * vecs[0], kv * _rstd(kv) * vecs[1]], [], []
    return _rowmap("latent_norm", fn, seq, [z_a], vecs=[g_q, g_kv],
                   row_outs=[(Q_LORA, BF16), (KV_LORA, BF16)])


def _latent_norm_bwd(dqn, dkvn, dkr, z_a, g_q, g_kv, seq):
    def fn(rows, bats, vecs):
        dq, dkv, dkrv, z = rows
        q, kv = z[:, :Q_LORA], z[:, Q_LORA:Q_LORA + KV_LORA]
        dxq, dgq = _norm_bwd(dq, q, _rstd(q), vecs[0])
        dxkv, dgkv = _norm_bwd(dkv, kv, _rstd(kv), vecs[1])
        return [jnp.concatenate([dxq, dxkv, dkrv], axis=1)], [], [_colsum(dgq), _colsum(dgkv)]
    return _rowmap("latent_norm_bwd", fn, seq, [dqn, dkvn, dkr, z_a], vecs=[g_q, g_kv],
                   row_outs=[(Q_LORA + KV_LORA + HEAD_SLAB, BF16)], vec_outs=[Q_LORA, KV_LORA])


def _qk_prep_fwd(qp, kv, z_a, pos, g_q, g_kn, g_kr, inv_freq, seq):
    def fn(rows, bats, vecs):
        qv, kvv, kr, p = rows
        gq, gkn, gkr, invf = vecs
        ang = p * invf
        cos, sin = jnp.cos(ang), jnp.sin(ang)
        nope = _lane(kr.shape) < QK_NOPE
        krn = kr * _rstd(kr, QK_ROPE) * gkr
        krr = krn * cos + _rot(krn) * sin
        qs, ks, vs = [], [], []
        for h in range(N_HEADS):
            xq = qv[:, h * HEAD_SLAB:(h + 1) * HEAD_SLAB]
            y = xq * _part_rstd(xq) * gq
            qs.append(y * cos + _rot(y) * sin)
            xk = kvv[:, h * HEAD_SLAB:(h + 1) * HEAD_SLAB]
            kn = jnp.where(nope, xk, 0.0)
            ks.append(jnp.where(nope, kn * _rstd(kn, QK_NOPE) * gkn, krr))
            vs.append(jnp.where(nope, 0.0, xk))
        return [jnp.concatenate(v, axis=1) for v in (qs, ks, vs)], [], []
    width = N_HEADS * HEAD_SLAB
    return _rowmap("qk_prep", fn, seq, [qp, kv, (z_a, HEAD_SLAB, 5), pos], vecs=[g_q, g_kn, g_kr, inv_freq],
                   row_outs=[(width, BF16)] * 3, ts=_tile(seq, (512, 256, 128, 64, 32, 16, 8)))


def _qk_prep_bwd(dqc, dkc, dvp, qp, kv, z_a, pos, g_q, g_kn, g_kr, inv_freq, seq):
    def fn(rows, bats, vecs):
        dq, dk, dv, qv, kvv, kr, p = rows
        gq, gkn, gkr, invf = vecs
        ang = p * invf
        cos, sin = jnp.cos(ang), jnp.sin(ang)
        nope = _lane(kr.shape) < QK_NOPE
        dqs, dkvs = [], []
        dgq = jnp.zeros((1, HEAD_SLAB), F32)
        dgkn = jnp.zeros((1, HEAD_SLAB), F32)
        dkrr = jnp.zeros(kr.shape, F32)
        for h in range(N_HEADS):
            sl = slice(h * HEAD_SLAB, (h + 1) * HEAD_SLAB)
            dyr = dq[:, sl]
            dy = dyr * cos - _rot(dyr * sin)
            xq = qv[:, sl]
            dx, dg = _part_norm_bwd(dy, xq, _part_rstd(xq), gq)
            dqs.append(dx)
            dgq = dgq + _colsum(dg)
            dkh = dk[:, sl]
            dkrr = dkrr + jnp.where(nope, 0.0, dkh)
            kn = jnp.where(nope, kvv[:, sl], 0.0)
            dxk, dgk = _norm_bwd(jnp.where(nope, dkh, 0.0), kn, _rstd(kn, QK_NOPE), gkn, QK_NOPE)
            dgkn = dgkn + _colsum(dgk)
            dkvs.append(jnp.where(nope, dxk, dv[:, sl]))
        dkrn = dkrr * cos - _rot(dkrr * sin)
        dkr, dgkr = _norm_bwd(dkrn, kr, _rstd(kr, QK_ROPE), gkr, QK_ROPE)
        return ([jnp.concatenate(dqs, axis=1), jnp.concatenate(dkvs, axis=1), dkr], [],
                [dgq, dgkn, _colsum(dgkr)])
    width = N_HEADS * HEAD_SLAB
    return _rowmap("qk_prep_bwd", fn, seq, [dqc, dkc, dvp, qp, kv, (z_a, HEAD_SLAB, 5), pos],
                   vecs=[g_q, g_kn, g_kr, inv_freq],
                   row_outs=[(width, BF16), (width, BF16), (HEAD_SLAB, F32)],
                   vec_outs=[HEAD_SLAB] * 3, ts=_tile(seq, (512, 256, 128, 64, 32, 16, 8)))


def _scores(q, k_ref, keys, tq):
    s = lax.dot_general(q, k_ref[0:keys, :], _DIMS["nt"], preferred_element_type=F32) * ATTN_SCALE
    row = lax.broadcasted_iota(jnp.int32, (tq, tq), 0)
    col = lax.broadcasted_iota(jnp.int32, (tq, tq), 1)
    diag = jnp.where(col <= row, s[:, keys - tq:], -1e30)
    return diag if keys == tq else jnp.concatenate([s[:, :keys - tq], diag], axis=1)


def _attn_fwd(qc, kc, vp, seq):
    tokens = qc.shape[0]
    tq = _tile(seq, (256, 128))
    nq = seq // tq

    def body(q_ref, k_ref, v_ref, o_ref, lse_ref):
        for i in range(nq):
            rows, keys = slice(i * tq, (i + 1) * tq), (i + 1) * tq
            s = _scores(q_ref[rows, :], k_ref, keys, tq)
            m = jnp.max(s, axis=-1, keepdims=True)
            p = jnp.exp(s - m)
            l = jnp.sum(p, axis=-1, keepdims=True)
            acc = jnp.dot(p.astype(BF16), v_ref[0:keys, :], preferred_element_type=F32)
            o_ref[rows, :] = (acc / l).astype(BF16)
            lse_ref[rows, :] = jnp.broadcast_to(m + jnp.log(l), (tq, HEAD_SLAB))

    spec = pl.BlockSpec((seq, HEAD_SLAB), lambda b, h: (b, h))
    return pl.pallas_call(
        body, name="attn_fwd", grid=(tokens // seq, N_HEADS),
        out_shape=(jax.ShapeDtypeStruct(qc.shape, BF16), jax.ShapeDtypeStruct(qc.shape, F32)),
        in_specs=[spec] * 3, out_specs=(spec, spec),
        compiler_params=_params("parallel", "parallel"),
    )(qc, kc, vp)


def _attn_bwd(qc, kc, vp, o, lse, do, seq):
    tokens = qc.shape[0]
    tq = _tile(seq, (256, 128))
    nq = seq // tq

    def body(q_ref, k_ref, v_ref, o_ref, lse_ref, do_ref, dq_ref, dk_ref, dv_ref):
        dk_ref[...] = jnp.zeros(dk_ref.shape, F32)
        dv_ref[...] = jnp.zeros(dv_ref.shape, F32)
        for i in range(nq):
            rows, keys = slice(i * tq, (i + 1) * tq), (i + 1) * tq
            q, dov = q_ref[rows, :], do_ref[rows, :]
            delta = jnp.sum(dov.astype(F32) * o_ref[rows, :].astype(F32), axis=-1, keepdims=True)
            s = _scores(q, k_ref, keys, tq)
            p = jnp.exp(s - jnp.tile(lse_ref[rows, :], (1, keys // HEAD_SLAB)))
            dp = lax.dot_general(dov, v_ref[0:keys, :], _DIMS["nt"], preferred_element_type=F32)
            ds = (p * (dp - delta) * ATTN_SCALE).astype(BF16)
            dq_ref[rows, :] = jnp.dot(ds, k_ref[0:keys, :], preferred_element_type=F32)
            dk_ref[0:keys, :] += lax.dot_general(ds, q, _DIMS["tn"], preferred_element_type=F32)
            dv_ref[0:keys, :] += lax.dot_general(p.astype(BF16), dov, _DIMS["tn"], preferred_element_type=F32)

    spec = pl.BlockSpec((seq, HEAD_SLAB), lambda b, h: (b, h))
    out = jax.ShapeDtypeStruct(qc.shape, F32)
    return pl.pallas_call(
        body, name="attn_bwd", grid=(tokens // seq, N_HEADS),
        out_shape=(out, out, out), in_specs=[spec] * 6, out_specs=(spec, spec, spec),
        compiler_params=_params("parallel", "parallel"),
    )(qc, kc, vp, o, lse, do)


def _adamw(w, g, m, v, name):
    rows, cols = w.shape
    whole = rows * cols * 4 <= ADAMW_WHOLE_BYTES
    tr = rows if whole else _tile(rows, (256, 128, 64, 32, 16, 8))
    c1 = 1.0 - ADAM_B1 ** ADAM_STEP
    c2 = 1.0 - ADAM_B2 ** ADAM_STEP

    def body(w_ref, g_ref, m_ref, v_ref, d_ref, nm_ref, nv_ref):
        gv = g_ref[...]
        nm = ADAM_B1 * m_ref[...] + (1.0 - ADAM_B1) * gv
        nv = ADAM_B2 * v_ref[...] + (1.0 - ADAM_B2) * (gv * gv)
        d_ref[...] = -ADAM_LR * ((nm / c1) / (jnp.sqrt(nv / c2) + ADAM_EPS) + ADAM_WD * w_ref[...])
        nm_ref[...] = nm
        nv_ref[...] = nv

    spec = pl.BlockSpec((tr, cols), lambda i: (i, 0))
    out = jax.ShapeDtypeStruct(w.shape, F32)
    return pl.pallas_call(
        body, name=name, grid=(rows // tr,), out_shape=(out, out, out),
        in_specs=[spec] * 4, out_specs=(spec, spec, spec),
        compiler_params=_params("parallel"),
    )(w, g, m, v)


def _adamw_landed(w, landed, m, v, name):
    rows, cols = w.shape
    tr = _tile(rows, (176, 128, 96, 64, 32, 16, 8))
    c1 = 1.0 - ADAM_B1 ** ADAM_STEP
    c2 = 1.0 - ADAM_B2 ** ADAM_STEP
    n_parts = len(landed)

    def body(*refs):
        w_ref, m_ref, v_ref = refs[:3]
        g_ref, d_ref, nm_ref, nv_ref = refs[3 + n_parts:]
        parts = []
        for x_ref in refs[3:3 + n_parts]:
            acc = x_ref[0].astype(F32)
            for d in range(1, N_DEV):
                acc = acc + x_ref[d].astype(F32)
            parts.append(acc)
        gv = parts[0] if n_parts == 1 else jnp.concatenate(parts, axis=1)
        nm = ADAM_B1 * m_ref[...] + (1.0 - ADAM_B1) * gv
        nv = ADAM_B2 * v_ref[...] + (1.0 - ADAM_B2) * (gv * gv)
        g_ref[...] = gv
        d_ref[...] = -ADAM_LR * ((nm / c1) / (jnp.sqrt(nv / c2) + ADAM_EPS) + ADAM_WD * w_ref[...])
        nm_ref[...] = nm
        nv_ref[...] = nv

    spec = pl.BlockSpec((tr, cols), lambda i: (i, 0))
    out = jax.ShapeDtypeStruct(w.shape, F32)
    return pl.pallas_call(
        body, name=name, grid=(rows // tr,), out_shape=(out, out, out, out),
        in_specs=[spec] * 3 + [pl.BlockSpec((N_DEV, tr, x.shape[2]), lambda i: (0, i, 0)) for x in landed],
        out_specs=(spec, spec, spec, spec),
        compiler_params=_params("parallel"),
    )(w, m, v, *landed)


def _mod_cols(c_all, w_ada, b_cols):
    def body(c_ref, w_ref, b_ref, act_ref, mod_ref):
        cv = c_ref[...]
        act = cv * _sigmoid(cv)
        act_ref[...] = act
        mod_ref[...] = jnp.dot(act.astype(BF16), w_ref[...].astype(BF16),
                               preferred_element_type=F32) + b_ref[...]

    n = w_ada.shape[1]
    return pl.pallas_call(
        body, name="mod_cols",
        out_shape=(jax.ShapeDtypeStruct(c_all.shape, F32), jax.ShapeDtypeStruct((c_all.shape[0], n), F32)),
        compiler_params=pltpu.CompilerParams(vmem_limit_bytes=VMEM_LIMIT),
    )(c_all, w_ada, b_cols)


def _ada_grads(c_act, dmod_all, dmod_cols):
    def body(c_ref, d_ref, dc_ref, gw_ref, gb_ref):
        gw_ref[...] = lax.dot_general(c_ref[...].astype(BF16), dc_ref[...].astype(BF16), _DIMS["tn"],
                                      preferred_element_type=F32)
        gb_ref[...] = _colsum(d_ref[...])

    return pl.pallas_call(
        body, name="ada_grads",
        out_shape=(jax.ShapeDtypeStruct((c_act.shape[1], dmod_cols.shape[1]), F32),
                   jax.ShapeDtypeStruct((1, dmod_all.shape[1]), F32)),
        compiler_params=pltpu.CompilerParams(vmem_limit_bytes=VMEM_LIMIT),
    )(c_act, dmod_all, dmod_cols)


def _flat_rows(a):
    flat = a.reshape(-1)
    pad = (-flat.shape[0]) % (LANES * SUBLANES)
    if pad:
        flat = jnp.pad(flat, (0, pad))
    return flat.reshape(-1, LANES)


def _gather_start(w, groups, tag, after=None, peers=None):
    shards = [[(w[n] if n in ROW_SHARDED else w[n].T).astype(BF16) for n in names] for names in groups]
    return _exchange_start_groups(shards, f"gather_{tag}_start", after=after, peers=peers)


def _gather_wait(handle, names, tag, after, peers=ALL_PEERS):
    landed = _exchange_wait(handle, f"gather_{tag}_wait", after=after, peers=peers)
    if peers == CHIP_PEERS:
        landed = [_sibling_forward(x, f"gather_{tag}_forward{i}") for i, x in enumerate(landed)]
    return {n: g.reshape(-1, g.shape[2]) for n, g in zip(names, landed)}


def _scatter_start(grads, names, tag, after=None):
    blocks = [grads[n].reshape(N_DEV, -1, grads[n].shape[1]) for n in names]
    return _exchange_start(blocks, f"scatter_{tag}_start", scatter=True, after=after)


def _scatter_wait(handle, names, tag, after):
    landed = _exchange_wait(handle, f"scatter_{tag}_wait", scatter=True, after=after)
    return {n: [x] for n, x in zip(names, landed)}


def _pack_small(vals):
    return jnp.concatenate([_flat_rows(v.astype(F32)) for v in vals], axis=0)


def _unpack_small(packed, like):
    out, row = [], 0
    for v in like:
        rows = _flat_rows(v).shape[0]
        out.append(packed[row:row + rows].reshape(-1)[:v.size].reshape(v.shape))
        row += rows
    return out


def _lanes128(*parts):
    out = jnp.zeros((HEAD_SLAB,), F32)
    for off, v in parts:
        out = lax.dynamic_update_slice(out, v.reshape(-1).astype(F32), (off,))
    return out.reshape(1, HEAD_SLAB)


def _step(x, c, positions, w, m, v, loss_target):
    nseq, seq, _ = x.shape
    tokens = nseq * seq
    me = _index(_my_pos())
    strip = lambda d: {n: (a[0] if a.ndim > 2 else a) for n, a in d.items()}
    shapes = {n: a.shape for n, a in w.items()}
    w, m, v = strip(w), strip(m), strip(v)

    c_all = _all_gather(c.reshape(-1, LANES), "gather_c").reshape(N_DEV * nseq, D_MODEL)
    n_ada = w["w_ada"].shape[1]
    b_cols = lax.dynamic_slice(w["b_ada"], (0, me * n_ada), (1, n_ada))
    c_act, mod_cols = _mod_cols(c_all, w["w_ada"], b_cols)
    mod_all = _all_gather(mod_cols, "gather_mod")
    mod = lax.dynamic_slice(mod_all, (0, me * nseq, 0), (N_DEV, nseq, n_ada))
    mod = mod.transpose(1, 0, 2).reshape(nseq, 3, 3, 1, D_MODEL)

    (h_f1i, h_f1o, h_mix_in, h_mix, h_f2), tok = _gather_start(
        w, (("w_ffn1_in",), ("w_ffn1_out",), MIXER[:1], MIXER[1:], ("w_ffn2_in", "w_ffn2_out")), "weights",
        after=mod_all, peers=[CHIP_PEERS] + [ALL_PEERS] * 4)
    started = tok[0:1, 0:1]

    g_q = _lanes128((0, w["q_norm_nope"]), (QK_NOPE, w["q_norm_rope"]))
    g_kn = _lanes128((0, w["k_norm_nope"]))
    g_kr = _lanes128((QK_NOPE, w["k_norm_rope"]))
    freq = ROPE_THETA ** (-jnp.arange(0, QK_ROPE, 2, dtype=F32) / QK_ROPE)
    inv_freq = _lanes128((QK_NOPE, jnp.concatenate([freq, freq])))
    pos = positions.reshape(tokens, 1).astype(F32)

    def sub(k, gamma, coef):
        return dict(gamma=w[gamma], shift=mod[:, k, 0] + started, scale=mod[:, k, 1], gate=coef * mod[:, k, 2])
    p1, pm, p2 = sub(0, "norm_ffn1", 0.5), sub(1, "norm_mix", 1.0), sub(2, "norm_ffn2", 0.5)
    t_big = _tile(tokens, (2048, 1024, 512))
    t_mid = _tile(tokens, (1024, 512))

    x0 = x.reshape(tokens, D_MODEL)
    h1 = _norm_mod_fwd(x0, p1, seq, "ffn1_norm")
    wt_f1i = _gather_wait(h_f1i, ("w_ffn1_in",), "ffn1_in", h1, peers=CHIP_PEERS)["w_ffn1_in"]
    g1, u1, a1 = _ffn_in_act(h1, wt_f1i, "ffn1_in")
    w_f1o = _gather_wait(h_f1o, ("w_ffn1_out",), "ffn1_out", a1)["w_ffn1_out"]
    x1, f1, h2 = _out_residual(a1, w_f1o, x0, p1["gate"], pm, seq, "ffn1_out")
    saved1 = (x0, h1, g1, u1, a1, wt_f1i, w_f1o)

    wt_in = _gather_wait(h_mix_in, MIXER[:1], "mix_in", h2)["w_in"]
    zero_rows = lambda rows: jnp.zeros((rows, D_MODEL), BF16)
    wt_p = wt_in[:512]
    wt_a = jnp.concatenate([wt_in[512:1152], zero_rows(QK_NOPE), wt_in[1152:1184], zero_rows(32)], axis=0)
    wt_g = wt_in[1184:]
    z_a = _mm(h2, wt_a, "nt", "mix_in_a", tm=t_big, tn=wt_a.shape[0])
    z_p = _mm(h2, wt_p, "nt", "mix_in_p", tm=t_big, tn=512)
    z_g = _mm(h2, wt_g, "nt", "mix_in_g", tm=t_big, tn=512)

    full = _gather_wait(h_mix, MIXER[1:], "mix", z_g)
    wtq_pad = jnp.pad(full["w_q_up"].reshape(N_HEADS, 96, Q_LORA), ((0, 0), (0, 32), (0, 0))).reshape(-1, Q_LORA)
    wtmla_pad = jnp.pad(full["w_mla_proj"].reshape(D_MODEL, N_HEADS, 64), ((0, 0), (0, 0), (64, 0))).reshape(D_MODEL, -1)
    wt_pool, wt_kv, w_mix_out = full["w_pool_proj"], full["w_kv_up"], full["w_out"]
    pooled, pg, ps = _pool_fwd(z_p, w["pool_grp"], w["pool_scale"], seq)
    br_pool = _mm(ps, wt_pool, "nt", "pool_proj", tm=t_big, tn=D_MODEL)
    qn, kvn = _latent_norm_fwd(z_a, w["q_a_norm"], w["kv_a_norm"], seq)
    qp = _mm(qn, wtq_pad, "nt", "q_up", tm=t_big, tn=D_MODEL)
    kv = _mm(kvn, wt_kv, "nt", "kv_up", tm=t_big, tn=D_MODEL)
    qc, kc, vp = _qk_prep_fwd(qp, kv, z_a, pos, g_q, g_kn, g_kr, inv_freq, seq)
    attn, lse = _attn_fwd(qc, kc, vp, seq)
    br_mla = _mm(attn, wtmla_pad, "nt", "mla_proj", tm=t_mid, tn=D_MODEL)

    def merge(rows):
        zg, bp, bm = rows[:3]
        return (_sigmoid(zg[:, :D_MODEL]) * bp + _sigmoid(zg[:, D_MODEL:]) * bm).astype(BF16)
    x2, o_mix, h3, merged = _out_residual([z_g, br_pool, br_mla], w_mix_out, x1, pm["gate"], p2, seq, "mix_out",
                                          lhs=merge)

    ffn2_w = _gather_wait(h_f2, ("w_ffn2_in", "w_ffn2_out"), "ffn2", h3)
    g2, u2, a2 = _ffn_in_act(h3, ffn2_w["w_ffn2_in"], "ffn2_in")
    dy, df2, dgate2, sq_err = _out_loss(a2, ffn2_w["w_ffn2_out"], x2, p2["gate"],
                                        loss_target.reshape(tokens, D_MODEL), seq, "ffn2_out")
    saved2 = (x2, h3, g2, u2, a2, ffn2_w["w_ffn2_in"], ffn2_w["w_ffn2_out"])

    grads = {}
    (dx2, do_mix, dsh2, dsc2, dgate_m, dg_ffn2), ops2 = _ffn_bwd_x(df2, dy, saved2, p2, seq, "ffn2", (o_mix, pm["gate"]))
    grads["w_ffn2_out"], grads["w_ffn2_in"] = _ffn_bwd_wout(ops2[0], ops2[1], "ffn2"), _ffn_bwd_win(ops2, "ffn2")
    s_f2, tok = _scatter_start(grads, ("w_ffn2_in", "w_ffn2_out"), "ffn2")

    grads["w_out"] = _mm(merged, do_mix, "tn", "mix_bwd_wout", out_dtype=BF16, tm=512, tn=D_MODEL)

    def merge_bwd(rows, bats, vecs):
        dmv, zg, bp, bm = rows
        s_p, s_m = _sigmoid(zg[:, :D_MODEL]), _sigmoid(zg[:, D_MODEL:])
        dzg = jnp.concatenate([dmv * bp * s_p * (1.0 - s_p), dmv * bm * s_m * (1.0 - s_m)], axis=1)
        return [dmv * s_p, dmv * s_m, dzg], [], []
    dbr_pool, dbr_mla, dz_g = _rowmap("mix_bwd_dmerged", merge_bwd, seq, [do_mix, z_g, br_pool, br_mla],
                                      row_outs=[(D_MODEL, BF16), (D_MODEL, BF16), (2 * D_MODEL, BF16)],
                                      mm=(w_mix_out, "nt"))

    grads["w_pool_proj"] = _mm(dbr_pool, ps, "tn", "pool_bwd_wproj", out_dtype=BF16, tm=512, tn=POOL_WIDTH)
    dps = _mm(dbr_pool, wt_pool, "nn", "pool_bwd_dps", tm=t_big, tn=POOL_WIDTH)
    dz_p, dgrp, dpool_scale = _pool_bwd(dps, pooled, pg, w["pool_grp"], w["pool_scale"] + tok[0:1, 0:1], seq)

    dwtmla_pad = _mm(dbr_mla, attn, "tn", "mla_bwd_wproj", out_dtype=BF16, tm=512, tn=D_MODEL)
    grads["w_mla_proj"] = dwtmla_pad.reshape(D_MODEL, N_HEADS, HEAD_SLAB)[:, :, 64:].reshape(D_MODEL, -1)
    d_attn = _mm(dbr_mla, wtmla_pad, "nn", "mla_bwd_dattn", out_dtype=BF16, tm=t_mid, tn=D_MODEL)
    dqc, dkc, dvp = _attn_bwd(qc, kc, vp, attn, lse, d_attn, seq)
    dqp, dkv, dkr, dg_q, dg_kn, dg_kr = _qk_prep_bwd(dqc, dkc, dvp, qp, kv, z_a, pos, g_q, g_kn, g_kr, inv_freq, seq)
    dwtq_pad = _mm(dqp, qn, "tn", "q_up_bwd_w", out_dtype=BF16, tm=512, tn=Q_LORA)
    grads["w_q_up"] = dwtq_pad.reshape(N_HEADS, HEAD_SLAB, Q_LORA)[:, :96].reshape(-1, Q_LORA)
    grads["w_kv_up"] = _mm(dkv, kvn, "tn", "kv_up_bwd_w", out_dtype=BF16, tm=512, tn=KV_LORA)
    dqn = _mm(dqp, wtq_pad, "nn", "q_up_bwd_x", tm=t_big, tn=Q_LORA)
    dkvn = _mm(dkv, wt_kv, "nn", "kv_up_bwd_x", tm=t_big, tn=KV_LORA)
    dz_a, dg_qa, dg_kva = _latent_norm_bwd(dqn, dkvn, dkr, z_a, w["q_a_norm"], w["kv_a_norm"], seq)

    dwt_a = _mm(dz_a, h2, "tn", "mix_in_bwd_wa", out_dtype=BF16, tm=256, tn=D_MODEL)
    dwt_p = _mm(dz_p, h2, "tn", "mix_in_bwd_wp", out_dtype=BF16, tm=512, tn=D_MODEL)
    dwt_g = _mm(dz_g, h2, "tn", "mix_in_bwd_wg", out_dtype=BF16, tm=512, tn=D_MODEL)
    grads["w_in"] = jnp.concatenate([dwt_p, dwt_a[:640], dwt_a[704:736], dwt_g], axis=0)
    s_mix, tok = _scatter_start(grads, MIXER, "mix")
    dh2 = [(dz_a, wt_a), (dz_p, wt_p), (dz_g, wt_g)]
    pm_tied = dict(pm, scale=pm["scale"] + tok[0:1, 0:1])
    dx1, df1, dsh_m, dsc_m, dgate1, dg_mix = _norm_mod_bwd(dh2, x1, dx2, pm_tied, seq, "mix_bwd_norm", (f1, p1["gate"]))

    small_early = [dg_mix.reshape(w["norm_mix"].shape), dg_ffn2.reshape(w["norm_ffn2"].shape), dgrp, dpool_scale,
                   dg_qa, dg_kva, dg_q[:, :QK_NOPE], dg_q[:, QK_NOPE:QK_NOPE + QK_ROPE], dg_kn[:, :QK_NOPE],
                   dg_kr[:, QK_NOPE:QK_NOPE + QK_ROPE]]
    s_small, tok = _exchange_start([_pack_small(small_early)], "gather_small_start", after=tok)

    handles = {}

    def ffn1_early(a, df):
        grads["w_ffn1_out"] = _ffn_bwd_wout(a, df, "ffn1")
        handles["f1o"], token = _scatter_start(grads, ("w_ffn1_out",), "ffn1_out")
        return token

    def ffn1_mid(operands):
        first = _ffn_bwd_win(operands, "ffn1", half=0)
        handles["f1i0"], token = _exchange_start([first.reshape(N_DEV, -1, first.shape[1])],
                                                 "scatter_ffn1_in0_start", scatter=True)
        return token

    p1_tied = dict(p1, scale=p1["scale"] + tok[0:1, 0:1])
    (dx0, dsh1, dsc1, dg_ffn1), ops1 = _ffn_bwd_x(df1, dx1, saved1, p1_tied, seq, "ffn1", early=ffn1_early,
                                                     mid=ffn1_mid)
    s_f1o = handles["f1o"]

    dmod = jnp.stack([jnp.stack([dsh1, dsc1, 0.5 * dgate1], axis=1),
                      jnp.stack([dsh_m, dsc_m, dgate_m], axis=1),
                      jnp.stack([dsh2, dsc2, 0.5 * dgate2], axis=1)], axis=1)
    n_dmod = nseq * 9 * D_MODEL // LANES
    tail = _all_gather(jnp.concatenate([dmod.reshape(-1, LANES), _flat_rows(dg_ffn1), _flat_rows(sq_err)], axis=0),
                       "gather_dmod")
    dmod_all = tail[:, :n_dmod].reshape(N_DEV * nseq, 9 * D_MODEL)

    second = _ffn_bwd_win(ops1, "ffn1", after=tail, half=1)
    s_second, tok = _exchange_start([second.reshape(N_DEV, -1, second.shape[1])], "scatter_ffn1_in1_start",
                                    scatter=True, after=tail)
    s_f1i = (handles["f1i0"], s_second)

    dmod_cols = lax.dynamic_slice(dmod_all, (0, me * n_ada), (N_DEV * nseq, n_ada)) + tok[0:1, 0:1]
    g_w_ada, g_b_ada = _ada_grads(c_act, dmod_all, dmod_cols)
    tail_sum = _sum_blocks(tail[:, n_dmod:], "sum_tail")
    g_norm_ffn1 = tail_sum[:SUBLANES].reshape(1, D_MODEL)
    loss = 0.5 * jnp.sum(tail_sum[SUBLANES:]) * (1.0 / D_MODEL)
    small_all = _exchange_wait(s_small, "gather_small_wait", after=g_b_ada)[0]
    small_sum = _sum_blocks(small_all, "sum_small")
    small = dict(zip(SMALL[1:], _unpack_small(small_sum, [w[n] for n in SMALL[1:]])))
    grad_w = dict(small, w_ada=g_w_ada, b_ada=g_b_ada, norm_ffn1=g_norm_ffn1)

    delta, new_m, new_v = {}, {}, {}

    def update(names, landed=None):
        for n in names:
            if landed is None:
                delta[n], new_m[n], new_v[n] = _adamw(w[n], grad_w[n], m[n], v[n], f"adamw_{n}")
            elif n in KEPT_TRANSPOSED:
                res = _adamw_landed(w[n].T, landed[n], m[n].T, v[n].T, f"adamw_{n}")
                grad_w[n], delta[n], new_m[n], new_v[n] = (r.T for r in res)
            elif n in ROW_SHARDED:
                grad_w[n], delta[n], new_m[n], new_v[n] = _adamw_landed(w[n], landed[n], m[n], v[n], f"adamw_{n}")
            else:
                grad_w[n] = _sum_blocks(landed[n][0], f"sum_{n}").T
                delta[n], new_m[n], new_v[n] = _adamw(w[n], grad_w[n], m[n], v[n], f"adamw_{n}")

    update(("w_ada",))
    rep = ("b_ada",) + SMALL
    d_s, m_s, v_s = _adamw(_pack_small([w[n] for n in rep]), _pack_small([grad_w[n] for n in rep]),
                           _pack_small([m[n] for n in rep]), _pack_small([v[n] for n in rep]), "adamw_small")
    like = [w[n] for n in rep]
    for dst, packed in ((delta, d_s), (new_m, m_s), (new_v, v_s)):
        dst.update(zip(rep, _unpack_small(packed, like)))
    update(("w_ffn2_in", "w_ffn2_out"), _scatter_wait(s_f2, ("w_ffn2_in", "w_ffn2_out"), "ffn2", after=d_s))
    update(MIXER, _scatter_wait(s_mix, MIXER, "mix", after=delta["w_ffn2_out"]))
    update(("w_ffn1_out",), _scatter_wait(s_f1o, ("w_ffn1_out",), "ffn1_out", after=delta["w_out"]))
    halves = [_exchange_wait(h, f"scatter_ffn1_in{i}_wait", scatter=True, after=delta["w_ffn1_out"])[0]
              for i, h in enumerate(s_f1i)]
    update(("w_ffn1_in",), {"w_ffn1_in": halves})

    lead = lambda d: [d[n].reshape(shapes[n]) for n in WEIGHTS]
    return (loss, dx0.reshape(x.shape), *lead(grad_w), *lead(delta), *lead(new_m), *lead(new_v))


def kernel(x, c, positions, w_ada, b_ada, norm_ffn1, w_ffn1_in, w_ffn1_out, norm_mix, w_in, pool_grp, pool_scale, w_pool_proj, q_a_norm, w_q_up, kv_a_norm, w_kv_up, q_norm_nope, q_norm_rope, k_norm_nope, k_norm_rope, w_mla_proj, w_out, norm_ffn2, w_ffn2_in, w_ffn2_out, loss_target, m_w_ada, m_b_ada, m_norm_ffn1, m_w_ffn1_in, m_w_ffn1_out, m_norm_mix, m_w_in, m_pool_grp, m_pool_scale, m_w_pool_proj, m_q_a_norm, m_w_q_up, m_kv_a_norm, m_w_kv_up, m_q_norm_nope, m_q_norm_rope, m_k_norm_nope, m_k_norm_rope, m_w_mla_proj, m_w_out, m_norm_ffn2, m_w_ffn2_in, m_w_ffn2_out, v_w_ada, v_b_ada, v_norm_ffn1, v_w_ffn1_in, v_w_ffn1_out, v_norm_mix, v_w_in, v_pool_grp, v_pool_scale, v_w_pool_proj, v_q_a_norm, v_w_q_up, v_kv_a_norm, v_w_kv_up, v_q_norm_nope, v_q_norm_rope, v_k_norm_nope, v_k_norm_rope, v_w_mla_proj, v_w_out, v_norm_ffn2, v_w_ffn2_in, v_w_ffn2_out):
    w = dict(w_ada=w_ada, b_ada=b_ada, norm_ffn1=norm_ffn1, w_ffn1_in=w_ffn1_in, w_ffn1_out=w_ffn1_out, norm_mix=norm_mix, w_in=w_in, pool_grp=pool_grp, pool_scale=pool_scale, w_pool_proj=w_pool_proj, q_a_norm=q_a_norm, w_q_up=w_q_up, kv_a_norm=kv_a_norm, w_kv_up=w_kv_up, q_norm_nope=q_norm_nope, q_norm_rope=q_norm_rope, k_norm_nope=k_norm_nope, k_norm_rope=k_norm_rope, w_mla_proj=w_mla_proj, w_out=w_out, norm_ffn2=norm_ffn2, w_ffn2_in=w_ffn2_in, w_ffn2_out=w_ffn2_out)
    m = dict(w_ada=m_w_ada, b_ada=m_b_ada, norm_ffn1=m_norm_ffn1, w_ffn1_in=m_w_ffn1_in, w_ffn1_out=m_w_ffn1_out, norm_mix=m_norm_mix, w_in=m_w_in, pool_grp=m_pool_grp, pool_scale=m_pool_scale, w_pool_proj=m_w_pool_proj, q_a_norm=m_q_a_norm, w_q_up=m_w_q_up, kv_a_norm=m_kv_a_norm, w_kv_up=m_w_kv_up, q_norm_nope=m_q_norm_nope, q_norm_rope=m_q_norm_rope, k_norm_nope=m_k_norm_nope, k_norm_rope=m_k_norm_rope, w_mla_proj=m_w_mla_proj, w_out=m_w_out, norm_ffn2=m_norm_ffn2, w_ffn2_in=m_w_ffn2_in, w_ffn2_out=m_w_ffn2_out)
    v = dict(w_ada=v_w_ada, b_ada=v_b_ada, norm_ffn1=v_norm_ffn1, w_ffn1_in=v_w_ffn1_in, w_ffn1_out=v_w_ffn1_out, norm_mix=v_norm_mix, w_in=v_w_in, pool_grp=v_pool_grp, pool_scale=v_pool_scale, w_pool_proj=v_w_pool_proj, q_a_norm=v_q_a_norm, w_q_up=v_w_q_up, kv_a_norm=v_kv_a_norm, w_kv_up=v_w_kv_up, q_norm_nope=v_q_norm_nope, q_norm_rope=v_q_norm_rope, k_norm_nope=v_k_norm_nope, k_norm_rope=v_k_norm_rope, w_mla_proj=v_w_mla_proj, w_out=v_w_out, norm_ffn2=v_norm_ffn2, w_ffn2_in=v_w_ffn2_in, w_ffn2_out=v_w_ffn2_out)
    return _step(x, c, positions, w, m, v, loss_target)
```

```python
import functools
import math

import jax
import jax.numpy as jnp
from jax import lax
from jax.experimental import pallas as pl
from jax.experimental.pallas import tpu as pltpu

F32 = jnp.float32
BF16 = jnp.bfloat16
MESH = pl.DeviceIdType.MESH
AXES = ("x", "y", "c")
N_DEV = 8

D_MODEL = 1024
D_FF = 2816
N_HEADS = 8
HEAD_SLAB = 128
QK_NOPE = 64
QK_ROPE = 32
POOL_WIDTH = 512
POOL_GROUPS = 4
POOL_GROUP_DIM = 128
Q_LORA = 384
KV_LORA = 256
ROPE_THETA = 10000.0
ATTN_SCALE = 1.0 / math.sqrt(QK_NOPE + QK_ROPE)
NORM_EPS = 1e-6
ADAM_LR, ADAM_B1, ADAM_B2, ADAM_EPS, ADAM_WD, ADAM_STEP = 0.001, 0.9, 0.999, 1e-08, 0.01, 10

LANES = 128
SUBLANES = 8
VMEM_LIMIT = 52 * 1024 * 1024
ADAMW_WHOLE_BYTES = 3 << 19
SUM_WHOLE_BYTES = 4 << 20

BIG = ("w_ffn1_in", "w_ffn1_out", "w_in", "w_pool_proj", "w_q_up", "w_kv_up",
       "w_mla_proj", "w_out", "w_ffn2_in", "w_ffn2_out")
ROW_SHARDED = ("w_ffn1_out", "w_out", "w_ffn2_out")
MIXER = ("w_in", "w_pool_proj", "w_q_up", "w_kv_up", "w_mla_proj", "w_out")
KEPT_TRANSPOSED = ("w_ffn1_in", "w_ffn2_in", "w_in", "w_q_up")
SMALL = ("norm_ffn1", "norm_mix", "norm_ffn2", "pool_grp", "pool_scale", "q_a_norm",
         "kv_a_norm", "q_norm_nope", "q_norm_rope", "k_norm_nope", "k_norm_rope")
WEIGHTS = ("w_ada", "b_ada", "norm_ffn1", "w_ffn1_in", "w_ffn1_out", "norm_mix", "w_in",
           "pool_grp", "pool_scale", "w_pool_proj", "q_a_norm", "w_q_up", "kv_a_norm",
           "w_kv_up", "q_norm_nope", "q_norm_rope", "k_norm_nope", "k_norm_rope",
           "w_mla_proj", "w_out", "norm_ffn2", "w_ffn2_in", "w_ffn2_out")


def _params(*sem):
    return pltpu.CompilerParams(dimension_semantics=sem, vmem_limit_bytes=VMEM_LIMIT)


def _tile(n, cands):
    for c in cands:
        if n % c == 0:
            return c
    return n


def _my_pos():
    return lax.axis_index("x"), lax.axis_index("y"), lax.axis_index("c")


def _flip(pos, k):
    x, y, c = pos
    fx, fy, fc = (k >> 2) & 1, (k >> 1) & 1, k & 1
    return ((1 - x) if fx else x, (1 - y) if fy else y, (1 - c) if fc else c)


def _index(pos):
    x, y, c = pos
    return 4 * x + 2 * y + c


def _exchange(arrays, name, scatter=False):
    n = len(arrays)

    def body(*refs):
        ins, outs = refs[:n], refs[n:2 * n]
        send_sems, recv_sems, local_sems = refs[2 * n:]
        me = _my_pos()
        mine, sends = [], []
        for a in range(n):
            own = ins[a].at[_index(me)] if scatter else ins[a]
            cp = pltpu.make_async_copy(own, outs[a].at[_index(me)], local_sems.at[a])
            cp.start()
            mine.append(cp)
        for k in range(1, N_DEV):
            peer = _flip(me, k)
            for a in range(n):
                cp = pltpu.make_async_remote_copy(
                    src_ref=ins[a].at[_index(peer)] if scatter else ins[a],
                    dst_ref=outs[a].at[_index(me)],
                    send_sem=send_sems.at[a, k - 1], recv_sem=recv_sems.at[a, k - 1],
                    device_id=peer, device_id_type=MESH)
                cp.start()
                sends.append(cp)
        for k in range(1, N_DEV):
            peer = _flip(me, k)
            for a in range(n):
                pltpu.make_async_remote_copy(
                    src_ref=ins[a].at[_index(me)] if scatter else ins[a],
                    dst_ref=outs[a].at[_index(peer)],
                    send_sem=send_sems.at[a, k - 1], recv_sem=recv_sems.at[a, k - 1],
                    device_id=peer, device_id_type=MESH).wait_recv()
        for cp in sends:
            cp.wait_send()
        for cp in mine:
            cp.wait()

    shape = lambda x: x.shape if scatter else (N_DEV,) + x.shape
    return pl.pallas_call(
        body, name=name,
        out_shape=tuple(jax.ShapeDtypeStruct(shape(x), x.dtype) for x in arrays),
        in_specs=[pl.BlockSpec(memory_space=pl.ANY)] * n,
        out_specs=tuple(pl.BlockSpec(memory_space=pl.ANY) for _ in arrays),
        scratch_shapes=[pltpu.SemaphoreType.DMA((n, N_DEV - 1)),
                        pltpu.SemaphoreType.DMA((n, N_DEV - 1)),
                        pltpu.SemaphoreType.DMA((n,))],
    )(*arrays)


def _all_gather(x, name):
    return _exchange([x], name)[0]


_HBM = pl.BlockSpec(memory_space=pltpu.HBM)
_SEM = pl.BlockSpec(memory_space=pltpu.SEMAPHORE)
_ANY = pl.BlockSpec(memory_space=pl.ANY)
_EFFECT = pltpu.SideEffectType.DATAFLOW_SIDE_EFFECTING


def _split_copy(ins, lands, send_sems, recv_sems, a, k, me, scatter, incoming):
    peer = _flip(me, k)
    block = me if incoming else peer
    return pltpu.make_async_remote_copy(
        src_ref=ins[a].at[_index(block)] if scatter else ins[a],
        dst_ref=lands[a].at[_index(peer if incoming else me)],
        send_sem=send_sems.at[a * (N_DEV - 1) + k - 1], recv_sem=recv_sems.at[a * (N_DEV - 1) + k - 1],
        device_id=peer, device_id_type=MESH)


ALL_PEERS = tuple(range(1, N_DEV))
CHIP_PEERS = (1, 2, 4, 6)


def _exchange_start_groups(groups, name, scatter=False, after=None, peers=None):
    peers = peers or [ALL_PEERS] * len(groups)
    sizes = [len(g) for g in groups]
    first = [sum(sizes[:i]) for i in range(len(sizes))]
    n, ng = sum(sizes), len(sizes)
    after = jnp.zeros((SUBLANES, LANES), F32) if after is None else after

    def body(*refs):
        ins, lands = refs[:n], refs[n:2 * n]
        sems = refs[2 * n + 1:2 * n + 1 + 2 * ng]
        me = _my_pos()
        for g in range(ng):
            part = slice(first[g], first[g] + sizes[g])
            for k in peers[g]:
                for a in range(sizes[g]):
                    _split_copy(ins[part], lands[part], sems[2 * g], sems[2 * g + 1], a, k, me, scatter, False).start()
        refs[-1][...] = jnp.zeros((SUBLANES, LANES), F32)

    shape = lambda x: x.shape if scatter else (N_DEV,) + x.shape
    hbm = lambda x: pltpu.with_memory_space_constraint(x, pltpu.HBM)
    srcs = [hbm(x) for g in groups for x in g]
    zones = [hbm(lax.empty(shape(x), x.dtype)) for g in groups for x in g]
    sem_shapes = [pltpu.SemaphoreType.DMA((s * (N_DEV - 1),)) for s in sizes for _ in range(2)]
    out = pl.pallas_call(
        body, name=name,
        out_shape=(*sem_shapes, *[pltpu.HBM(x.shape, x.dtype) for x in srcs + zones],
                   jax.ShapeDtypeStruct((SUBLANES, LANES), F32)),
        in_specs=[_HBM] * (2 * n) + [_ANY],
        out_specs=(*[_SEM] * (2 * ng), *[_HBM] * (2 * n), pl.BlockSpec(memory_space=pltpu.VMEM)),
        input_output_aliases={i: 2 * ng + i for i in range(2 * n)},
        compiler_params=pltpu.CompilerParams(has_side_effects=_EFFECT),
    )(*srcs, *zones, after)
    bufs = out[2 * ng:-1]
    handles = [(out[2 * g], out[2 * g + 1], *bufs[first[g]:first[g] + sizes[g]],
                *bufs[n + first[g]:n + first[g] + sizes[g]]) for g in range(ng)]
    return handles, out[-1]


def _exchange_start(arrays, name, scatter=False, after=None):
    handles, token = _exchange_start_groups([arrays], name, scatter, after)
    return handles[0], token


def _exchange_wait(handle, name, scatter=False, after=None, peers=ALL_PEERS):
    send_sems, recv_sems = handle[0], handle[1]
    n = (len(handle) - 2) // 2
    after = jnp.zeros((SUBLANES, LANES), F32) if after is None else after

    def body(*refs):
        ins, lands = refs[:n], refs[n:2 * n]
        send, recv = refs[2 * n], refs[2 * n + 1]
        me = _my_pos()
        for k in peers:
            for a in range(n):
                _split_copy(ins, lands, send, recv, a, k, me, scatter, False).wait_send()
                _split_copy(ins, lands, send, recv, a, k, me, scatter, True).wait_recv()

    bufs = handle[2:]
    out = pl.pallas_call(
        body, name=name,
        out_shape=tuple(pltpu.HBM(x.shape, x.dtype) for x in bufs),
        in_specs=[_HBM] * (2 * n) + [_SEM, _SEM, _ANY],
        out_specs=tuple([_HBM] * (2 * n)),
        input_output_aliases={i: i for i in range(2 * n)},
        compiler_params=pltpu.CompilerParams(has_side_effects=_EFFECT),
    )(*bufs, send_sems, recv_sems, after)
    me = _index(_my_pos())
    landed = []
    for src, land in zip(out[:n], out[n:]):
        own = lax.dynamic_slice_in_dim(src, me, 1, axis=0) if scatter else src[None]
        landed.append(lax.dynamic_update_slice_in_dim(land, own, me, axis=0))
    return landed


def _sibling_forward(x, name):
    flips = [k for k in CHIP_PEERS if k != 1]

    def body(x_ref, o_ref, send_sems, recv_sems):
        me = _my_pos()
        sibling = _flip(me, 1)
        sends = []
        for i, k in enumerate(flips):
            block = o_ref.at[_index(_flip(me, k))]
            cp = pltpu.make_async_remote_copy(src_ref=block, dst_ref=block, send_sem=send_sems.at[i],
                                              recv_sem=recv_sems.at[i], device_id=sibling, device_id_type=MESH)
            cp.start()
            sends.append(cp)
        for i, k in enumerate(flips):
            block = o_ref.at[_index(_flip(sibling, k))]
            pltpu.make_async_remote_copy(src_ref=block, dst_ref=block, send_sem=send_sems.at[i],
                                         recv_sem=recv_sems.at[i], device_id=sibling, device_id_type=MESH).wait_recv()
        for cp in sends:
            cp.wait_send()

    return pl.pallas_call(
        body, name=name, out_shape=jax.ShapeDtypeStruct(x.shape, x.dtype),
        in_specs=[_ANY], out_specs=_ANY, input_output_aliases={0: 0},
        scratch_shapes=[pltpu.SemaphoreType.DMA((len(flips),)), pltpu.SemaphoreType.DMA((len(flips),))],
    )(x)


def _sum_blocks(x, name):
    n, rows, cols = x.shape
    whole = x.size * x.dtype.itemsize <= SUM_WHOLE_BYTES
    tr = rows if whole else _tile(rows, (512, 256, 128, 64, 32, 16, 8))

    def body(x_ref, o_ref):
        acc = x_ref[0].astype(F32)
        for d in range(1, n):
            acc = acc + x_ref[d].astype(F32)
        o_ref[...] = acc

    return pl.pallas_call(
        body, name=name,
        out_shape=jax.ShapeDtypeStruct((rows, cols), F32),
        grid=(rows // tr,),
        in_specs=[pl.BlockSpec((n, tr, cols), lambda i: (0, i, 0))],
        out_specs=pl.BlockSpec((tr, cols), lambda i: (i, 0)),
        compiler_params=_params("parallel"),
    )(x)


_DIMS = {"nn": (((1,), (0,)), ((), ())), "nt": (((1,), (1,)), ((), ())), "tn": (((0,), (0,)), ((), ()))}


def _mm(a, b, mode, name, out_dtype=F32, tm=None, tn=None, add=None, after=None, b_cols=None):
    if mode == "tn":
        kdim, m = a.shape
    else:
        m, kdim = a.shape
    n = b.shape[0] if mode == "nt" else b.shape[1]
    tm = tm or _tile(m, (512, 256, 128))
    tn = tn or _tile(n, (512, 256, 128))
    j0 = 0
    if b_cols is not None:
        j0, n = b_cols[0], b_cols[1] * tn
    dims = _DIMS[mode]

    def body(*refs):
        refs = refs if after is None else refs[1:]
        acc = lax.dot_general(refs[0][...].astype(BF16), refs[1][...].astype(BF16), dims,
                              preferred_element_type=F32)
        if add is not None:
            acc = acc + refs[2][...]
        refs[-1][...] = acc.astype(out_dtype)

    a_spec = (pl.BlockSpec((kdim, tm), lambda i, j: (0, i)) if mode == "tn"
              else pl.BlockSpec((tm, kdim), lambda i, j: (i, 0)))
    b_spec = (pl.BlockSpec((tn, kdim), lambda i, j: (j, 0)) if mode == "nt"
              else pl.BlockSpec((kdim, tn), lambda i, j: (0, j + j0)))
    o_spec = pl.BlockSpec((tm, tn), lambda i, j: (i, j))
    in_specs, args = [a_spec, b_spec], [a, b]
    if add is not None:
        in_specs.append(o_spec)
        args.append(add)
    if after is not None:
        in_specs.insert(0, _ANY)
        args.insert(0, after)
    return pl.pallas_call(
        body, name=name, out_shape=jax.ShapeDtypeStruct((m, n), out_dtype), grid=(m // tm, n // tn),
        in_specs=in_specs, out_specs=o_spec,
        compiler_params=_params("parallel", "parallel"),
    )(*args)


def _rowmap(name, fn, seq, rows, bats=(), vecs=(), row_outs=(), bat_outs=(), vec_outs=(), ts=None, mm=None, lhs=None,
            after=None):
    mms = [] if mm is None else (mm if isinstance(mm, list) else [mm])
    rows = [r if isinstance(r, tuple) else (r, r.shape[1], 0) for r in rows]
    tokens = rows[0][0].shape[0]
    nseq = tokens // seq
    ts = ts or _tile(seq, (512, 256, 128, 64, 32, 16, 8))
    nt = seq // ts
    n_r, n_b, n_v = len(rows), len(bats), len(vecs)
    n_ro, n_bo = len(row_outs), len(bat_outs)

    def accumulate(ref, val, first):
        @pl.when(first)
        def _():
            ref[...] = val.reshape(ref.shape)

        @pl.when(jnp.logical_not(first))
        def _():
            ref[...] += val.reshape(ref.shape)

    def body(*refs):
        n_in = n_r + n_b + n_v + len(mms) + (after is not None)
        ins, outs = refs[:n_in], refs[n_in:]
        b_vals = [r[0] for r in ins[n_r:n_r + n_b]]
        v_vals = [r[...] for r in ins[n_r + n_b:n_r + n_b + n_v]]
        r_vals = [r[...] for r in ins[:n_r]]
        if mms:
            lefts = r_vals[:len(mms)] if lhs is None else [lhs(r_vals)]
            acc = None
            for left, b_ref, (_, mode) in zip(lefts, ins[n_r + n_b + n_v:], mms):
                part = lax.dot_general(left.astype(BF16), b_ref[...].astype(BF16), _DIMS[mode],
                                       preferred_element_type=F32)
                acc = part if acc is None else acc + part
            r_vals = [acc] + r_vals[len(mms):] if lhs is None else [acc, lefts[0]] + r_vals
        ro, bo, vo = fn(r_vals, b_vals, v_vals)
        for ref, val in zip(outs[:n_ro], ro):
            ref[...] = val.astype(ref.dtype)
        b, i = pl.program_id(0), pl.program_id(1)
        for ref, val in zip(outs[n_ro:n_ro + n_bo], bo):
            accumulate(ref, val, i == 0)
        for ref, val in zip(outs[n_ro + n_bo:], vo):
            accumulate(ref, val, jnp.logical_and(i == 0, b == 0))

    in_specs = [pl.BlockSpec((ts, w), functools.partial(lambda b, i, cb: (b * nt + i, cb), cb=cb))
                for _, w, cb in rows]
    in_specs += [pl.BlockSpec((1, 1, v.shape[2]), lambda b, i: (b, 0, 0)) for v in bats]
    in_specs += [pl.BlockSpec((1, v.shape[1]), lambda b, i: (0, 0)) for v in vecs]
    extra = [b_arr for b_arr, _ in mms]
    in_specs += [pl.BlockSpec(b_arr.shape, lambda b, i: (0, 0), pipeline_mode=pl.Buffered(1)) for b_arr in extra]
    if after is not None:
        in_specs.append(_ANY)
        extra.append(after)
    out_shape = [jax.ShapeDtypeStruct((tokens, f), dt) for f, dt in row_outs]
    out_specs = [pl.BlockSpec((ts, f), lambda b, i: (b * nt + i, 0)) for f, _ in row_outs]
    out_shape += [jax.ShapeDtypeStruct((nseq, 1, f), F32) for f in bat_outs]
    out_specs += [pl.BlockSpec((1, 1, f), lambda b, i: (b, 0, 0)) for f in bat_outs]
    out_shape += [jax.ShapeDtypeStruct((1, f), F32) for f in vec_outs]
    out_specs += [pl.BlockSpec((1, f), lambda b, i: (0, 0)) for f in vec_outs]
    return pl.pallas_call(
        body, name=name, out_shape=tuple(out_shape), grid=(nseq, nt),
        in_specs=in_specs, out_specs=tuple(out_specs),
        compiler_params=_params("arbitrary", "arbitrary"),
    )(*([r[0] for r in rows] + list(bats) + list(vecs) + extra))


def _colsum(v):
    return jnp.sum(v, axis=0, keepdims=True)


def _rstd(x, width=None):
    width = width or x.shape[-1]
    return lax.rsqrt(jnp.sum(x * x, axis=-1, keepdims=True) * (1.0 / width) + NORM_EPS)


def _norm_bwd(dy, x, r, g, width=None):
    width = width or x.shape[-1]
    xhat = x * r
    dxhat = dy * g
    dx = r * (dxhat - xhat * (jnp.sum(dxhat * xhat, axis=-1, keepdims=True) * (1.0 / width)))
    return dx, dy * xhat


def _sigmoid(x):
    return 0.5 * jnp.tanh(0.5 * x) + 0.5


def _norm_mod(xv, g, sh, sc):
    return xv * _rstd(xv) * g * (1.0 + sc) + sh


def _norm_mod_fwd(x, p, seq, name):
    def fn(rows, bats, vecs):
        return [_norm_mod(rows[0], vecs[0], bats[0], bats[1])], [], []
    return _rowmap(name, fn, seq, [x], [p["shift"], p["scale"]], [p["gamma"]], row_outs=[(D_MODEL, BF16)])[0]


def _norm_mod_bwd(dh, x, dres, p, seq, name, prev=None, after=None):
    products = dh if isinstance(dh, list) else None
    lefts = [l for l, _ in products] if products else [dh]
    def fn(rows, bats, vecs):
        dhv, xv, dr = rows[:3]
        sc, g = bats[0], vecs[0]
        r = _rstd(xv)
        dxn, dg = _norm_bwd(dhv * (1.0 + sc), xv, r, g)
        dx = dr + dxn
        ro, bo = [dx], [_colsum(dhv), _colsum(dhv * (xv * r * g))]
        if prev is not None:
            ro.append(bats[1] * dx)
            bo.append(_colsum(dx * rows[3].astype(F32)))
        return ro, bo, [_colsum(dg)]
    more = prev is not None
    return _rowmap(name, fn, seq, lefts + [x, dres] + ([prev[0]] if more else []),
                   [p["scale"]] + ([prev[1]] if more else []), [p["gamma"]],
                   row_outs=[(D_MODEL, F32)] + ([(D_MODEL, BF16)] if more else []),
                   bat_outs=[D_MODEL] * (3 if more else 2), vec_outs=[D_MODEL],
                   mm=[(r, "nn") for _, r in products] if products else None, after=after)


def _ffn_in_act(h, wt_in, name):
    tokens = h.shape[0]
    tm, tn = _tile(tokens, (2048, 1024, 512)), 256
    nj = D_FF // tn

    def body(h_ref, wg_ref, wu_ref, g_ref, u_ref, a_ref):
        hv = h_ref[...]
        g = lax.dot_general(hv, wg_ref[...], _DIMS["nt"], preferred_element_type=F32)
        u = lax.dot_general(hv, wu_ref[...], _DIMS["nt"], preferred_element_type=F32)
        g_ref[...] = g.astype(BF16)
        u_ref[...] = u.astype(BF16)
        a_ref[...] = (g * _sigmoid(g) * u).astype(BF16)

    o_spec = pl.BlockSpec((tm, tn), lambda i, j: (i, j))
    out = jax.ShapeDtypeStruct((tokens, D_FF), BF16)
    return pl.pallas_call(
        body, name=name, grid=(tokens // tm, nj), out_shape=(out, out, out),
        in_specs=[pl.BlockSpec((tm, D_MODEL), lambda i, j: (i, 0)),
                  pl.BlockSpec((tn, D_MODEL), lambda i, j: (j, 0)),
                  pl.BlockSpec((tn, D_MODEL), lambda i, j: (j + nj, 0))],
        out_specs=(o_spec, o_spec, o_spec),
        compiler_params=_params("parallel", "parallel"),
    )(h, wt_in, wt_in)


def _out_residual(a, w_out, res, gate, nxt, seq, name, lhs=None):
    def fn(rows, bats, vecs):
        acc, rv = rows[0], rows[-1]
        x_new = rv + bats[0] * acc
        made = [] if lhs is None else [rows[1]]
        return [x_new, acc, _norm_mod(x_new, vecs[0], bats[1], bats[2])] + made, [], []
    outs = [(D_MODEL, F32), (D_MODEL, BF16), (D_MODEL, BF16)] + ([] if lhs is None else [(D_MODEL, BF16)])
    return _rowmap(name, fn, seq, (a if lhs is not None else [a]) + [res], [gate, nxt["shift"], nxt["scale"]],
                   [nxt["gamma"]], row_outs=outs, ts=_tile(seq, (512, 256, 128)), mm=(w_out, "nn"), lhs=lhs)


def _out_loss(a, w_out, res, gate, target, seq, name):
    def fn(rows, bats, vecs):
        acc, rv, tv = rows
        err = rv + bats[0] * acc - tv
        dy = err * (1.0 / D_MODEL)
        return [dy, bats[0] * dy], [_colsum(dy * acc)], [_colsum(err * err)]
    return _rowmap(name, fn, seq, [a, res, target], [gate], row_outs=[(D_MODEL, F32), (D_MODEL, BF16)],
                   bat_outs=[D_MODEL], vec_outs=[D_MODEL], ts=_tile(seq, (512, 256, 128)), mm=(w_out, "nn"))


def _ffn_bwd_x(df, dres, saved, p, seq, tag, prev=None, early=None, mid=None):
    x, h, g, u, a, w_in, w_out = saved
    first = None if early is None else early(a, df)

    def act_bwd(rows, bats, vecs):
        dav, gv, uv = rows[0], rows[1].astype(F32), rows[2].astype(F32)
        sg = _sigmoid(gv)
        silu = gv * sg
        dg = dav * uv * (sg * (1.0 + gv * (1.0 - sg)))
        return [jnp.concatenate([dg, dav * silu], axis=1)], [], []
    dgu = _rowmap(f"{tag}_bwd_da", act_bwd, seq, [df, g, u], row_outs=[(2 * D_FF, BF16)],
                  ts=_tile(seq, (256, 128)), mm=(w_out, "nt"), after=first)[0]
    operands = (a, df, dgu, h)
    after = None if mid is None else mid(operands)
    return _norm_mod_bwd([(dgu, w_in)], x, dres, p, seq, f"{tag}_bwd_norm", prev, after=after), operands


def _ffn_bwd_wout(a, df, tag):
    return _mm(a, df, "tn", f"{tag}_bwd_wout", out_dtype=BF16, tm=256, tn=D_MODEL)


def _ffn_bwd_win(operands, tag, after=None, half=None):
    _, _, dgu, h = operands
    if half is None:
        return _mm(dgu, h, "tn", f"{tag}_bwd_win", out_dtype=BF16, tm=512, tn=D_MODEL, after=after)
    return _mm(dgu, h, "tn", f"{tag}_bwd_win{half}", out_dtype=BF16, tm=512, tn=D_MODEL // 2, after=after,
               b_cols=(half, 1))


def _shift_rows(v, k, forward):
    n = v.shape[0]
    row = lax.broadcasted_iota(jnp.int32, v.shape, 0)
    if forward:
        return jnp.where(row >= k, pltpu.roll(v, k, 0), 0.0)
    return jnp.where(row < n - k, pltpu.roll(v, n - k, 0), 0.0)


def _window_sums(v, forward):
    out, s, k = [], v, 1
    for _ in range(POOL_GROUPS):
        s = s + _shift_rows(s, k, forward)
        out.append(s)
        k *= 2
    return out


def _by_group(vals, g):
    out = vals[-1]
    for idx in range(len(vals) - 2, -1, -1):
        out = jnp.where(g == idx, vals[idx], out)
    return out


def _inv_count(shape, g):
    t1 = lax.broadcasted_iota(jnp.int32, shape, 0) + 1
    window = _by_group([jnp.int32(2 ** (i + 1)) for i in range(POOL_GROUPS)], g)
    return 1.0 / jnp.minimum(t1, window).astype(F32)


def _pool_fwd(u, grp, scale, seq):
    tokens = u.shape[0]

    def body(u_ref, grp_ref, sc_ref, pooled_ref, pg_ref, ps_ref):
        g = pl.program_id(1)
        uv = u_ref[...]
        sums = _by_group(_window_sums(uv, True), g)
        pooled = (sums * _inv_count(uv.shape, g) - uv).astype(BF16)
        pg = jnp.dot(pooled, grp_ref[0].astype(BF16), preferred_element_type=F32)
        pooled_ref[...] = pooled
        pg_ref[...] = pg
        ps_ref[...] = (pg * sc_ref[...]).astype(BF16)

    blk = pl.BlockSpec((seq, POOL_GROUP_DIM), lambda b, g: (b, g))
    return pl.pallas_call(
        body, name="pool_fwd", grid=(tokens // seq, POOL_GROUPS),
        out_shape=(jax.ShapeDtypeStruct(u.shape, BF16), jax.ShapeDtypeStruct(u.shape, F32),
                   jax.ShapeDtypeStruct(u.shape, BF16)),
        in_specs=[blk, pl.BlockSpec((1, POOL_GROUP_DIM, POOL_GROUP_DIM), lambda b, g: (g, 0, 0)),
                  pl.BlockSpec((1, POOL_GROUP_DIM), lambda b, g: (0, g))],
        out_specs=(blk, blk, blk),
        compiler_params=_params("parallel", "parallel"),
    )(u, grp, scale)


def _pool_bwd(dps, pooled, pg, grp, scale, seq):
    tokens = dps.shape[0]

    def body(dps_ref, pooled_ref, pg_ref, grp_ref, sc_ref, du_ref, dgrp_ref, dsc_ref):
        g, b = pl.program_id(0), pl.program_id(1)
        dpsv = dps_ref[...]
        dpg = (dpsv * sc_ref[...]).astype(BF16)
        dsc = _colsum(dpsv * pg_ref[...])
        dgrp = lax.dot_general(pooled_ref[...], dpg, _DIMS["tn"], preferred_element_type=F32)

        @pl.when(b == 0)
        def _():
            dsc_ref[...] = dsc
            dgrp_ref[0] = dgrp

        @pl.when(b > 0)
        def _():
            dsc_ref[...] += dsc
            dgrp_ref[0] += dgrp

        dpool = lax.dot_general(dpg, grp_ref[0].astype(BF16), _DIMS["nt"], preferred_element_type=F32)
        sums = _by_group(_window_sums(dpool * _inv_count(dpool.shape, g), False), g)
        du_ref[...] = (sums - dpool).astype(BF16)

    blk = pl.BlockSpec((seq, POOL_GROUP_DIM), lambda g, b: (b, g))
    grp_spec = pl.BlockSpec((1, POOL_GROUP_DIM, POOL_GROUP_DIM), lambda g, b: (g, 0, 0))
    vec_spec = pl.BlockSpec((1, POOL_GROUP_DIM), lambda g, b: (0, g))
    return pl.pallas_call(
        body, name="pool_bwd", grid=(POOL_GROUPS, tokens // seq),
        out_shape=(jax.ShapeDtypeStruct(dps.shape, BF16), jax.ShapeDtypeStruct(grp.shape, F32),
                   jax.ShapeDtypeStruct(scale.shape, F32)),
        in_specs=[blk, blk, blk, grp_spec, vec_spec],
        out_specs=(blk, grp_spec, vec_spec),
        compiler_params=_params("arbitrary", "arbitrary"),
    )(dps, pooled, pg, grp, scale)


def _lane(shape):
    return lax.broadcasted_iota(jnp.int32, shape, len(shape) - 1)


def _rot(y):
    lane = _lane(y.shape)
    r = jnp.where(lane < QK_NOPE + QK_ROPE // 2,
                  -pltpu.roll(y, HEAD_SLAB - QK_ROPE // 2, 1), pltpu.roll(y, QK_ROPE // 2, 1))
    return jnp.where(jnp.logical_and(lane >= QK_NOPE, lane < QK_NOPE + QK_ROPE), r, 0.0)


def _part_rstd(x):
    sq = x * x
    nope = _lane(x.shape) < QK_NOPE
    s_nope = jnp.sum(jnp.where(nope, sq, 0.0), axis=-1, keepdims=True)
    s_rope = jnp.sum(sq, axis=-1, keepdims=True) - s_nope
    return jnp.where(nope, lax.rsqrt(s_nope * (1.0 / QK_NOPE) + NORM_EPS),
                     lax.rsqrt(s_rope * (1.0 / QK_ROPE) + NORM_EPS))


def _part_norm_bwd(dy, x, r, g):
    nope = _lane(x.shape) < QK_NOPE
    xhat = x * r
    dxhat = dy * g
    prod = dxhat * xhat
    m_nope = jnp.sum(jnp.where(nope, prod, 0.0), axis=-1, keepdims=True)
    m_rope = jnp.sum(prod, axis=-1, keepdims=True) - m_nope
    mean = jnp.where(nope, m_nope * (1.0 / QK_NOPE), m_rope * (1.0 / QK_ROPE))
    return r * (dxhat - xhat * mean), dy * xhat


def _latent_norm_fwd(z_a, g_q, g_kv, seq):
    def fn(rows, bats, vecs):
        q, kv = rows[0][:, :Q_LORA], rows[0][:, Q_LORA:Q_LORA + KV_LORA]
        return [q * _rstd(q) * vecs[0], kv * _rstd(kv) * vecs[1]], [], []
    return _rowmap("latent_norm", fn, seq, [z_a], vecs=[g_q, g_kv],
                   row_outs=[(Q_LORA, BF16), (KV_LORA, BF16)])


def _latent_norm_bwd(dqn, dkvn, dkr, z_a, g_q, g_kv, seq):
    def fn(rows, bats, vecs):
        dq, dkv, dkrv, z = rows
        q, kv = z[:, :Q_LORA], z[:, Q_LORA:Q_LORA + KV_LORA]
        dxq, dgq = _norm_bwd(dq, q, _rstd(q), vecs[0])
        dxkv, dgkv = _norm_bwd(dkv, kv, _rstd(kv), vecs[1])
        return [jnp.concatenate([dxq, dxkv, dkrv], axis=1)], [], [_colsum(dgq), _colsum(dgkv)]
    return _rowmap("latent_norm_bwd", fn, seq, [dqn, dkvn, dkr, z_a], vecs=[g_q, g_kv],
                   row_outs=[(Q_LORA + KV_LORA + HEAD_SLAB, BF16)], vec_outs=[Q_LORA, KV_LORA])


def _qk_prep_fwd(qp, kv, z_a, pos, g_q, g_kn, g_kr, inv_freq, seq):
    def fn(rows, bats, vecs):
        qv, kvv, kr, p = rows
        gq, gkn, gkr, invf = vecs
        ang = p * invf
        cos, sin = jnp.cos(ang), jnp.sin(ang)
        nope = _lane(kr.shape) < QK_NOPE
        krn = kr * _rstd(kr, QK_ROPE) * gkr
        krr = krn * cos + _rot(krn) * sin
        qs, ks, vs = [], [], []
        for h in range(N_HEADS):
            xq = qv[:, h * HEAD_SLAB:(h + 1) * HEAD_SLAB]
            y = xq * _part_rstd(xq) * gq
            qs.append(y * cos + _rot(y) * sin)
            xk = kvv[:, h * HEAD_SLAB:(h + 1) * HEAD_SLAB]
            kn = jnp.where(nope, xk, 0.0)
            ks.append(jnp.where(nope, kn * _rstd(kn, QK_NOPE) * gkn, krr))
            vs.append(jnp.where(nope, 0.0, xk))
        return [jnp.concatenate(v, axis=1) for v in (qs, ks, vs)], [], []
    width = N_HEADS * HEAD_SLAB
    return _rowmap("qk_prep", fn, seq, [qp, kv, (z_a, HEAD_SLAB, 5), pos], vecs=[g_q, g_kn, g_kr, inv_freq],
                   row_outs=[(width, BF16)] * 3, ts=_tile(seq, (512, 256, 128, 64, 32, 16, 8)))


def _qk_prep_bwd(dqc, dkc, dvp, qp, kv, z_a, pos, g_q, g_kn, g_kr, inv_freq, seq):
    def fn(rows, bats, vecs):
        dq, dk, dv, qv, kvv, kr, p = rows
        gq, gkn, gkr, invf = vecs
        ang = p * invf
        cos, sin = jnp.cos(ang), jnp.sin(ang)
        nope = _lane(kr.shape) < QK_NOPE
        dqs, dkvs = [], []
        dgq = jnp.zeros((1, HEAD_SLAB), F32)
        dgkn = jnp.zeros((1, HEAD_SLAB), F32)
        dkrr = jnp.zeros(kr.shape, F32)
        for h in range(N_HEADS):
            sl = slice(h * HEAD_SLAB, (h + 1) * HEAD_SLAB)
            dyr = dq[:, sl]
            dy = dyr * cos - _rot(dyr * sin)
            xq = qv[:, sl]
            dx, dg = _part_norm_bwd(dy, xq, _part_rstd(xq), gq)
            dqs.append(dx)
            dgq = dgq + _colsum(dg)
            dkh = dk[:, sl]
            dkrr = dkrr + jnp.where(nope, 0.0, dkh)
            kn = jnp.where(nope, kvv[:, sl], 0.0)
            dxk, dgk = _norm_bwd(jnp.where(nope, dkh, 0.0), kn, _rstd(kn, QK_NOPE), gkn, QK_NOPE)
            dgkn = dgkn + _colsum(dgk)
            dkvs.append(jnp.where(nope, dxk, dv[:, sl]))
        dkrn = dkrr * cos - _rot(dkrr * sin)
        dkr, dgkr = _norm_bwd(dkrn, kr, _rstd(kr, QK_ROPE), gkr, QK_ROPE)
        return ([jnp.concatenate(dqs, axis=1), jnp.concatenate(dkvs, axis=1), dkr], [],
                [dgq, dgkn, _colsum(dgkr)])
    width = N_HEADS * HEAD_SLAB
    return _rowmap("qk_prep_bwd", fn, seq, [dqc, dkc, dvp, qp, kv, (z_a, HEAD_SLAB, 5), pos],
                   vecs=[g_q, g_kn, g_kr, inv_freq],
                   row_outs=[(width, BF16), (width, BF16), (HEAD_SLAB, F32)],
                   vec_outs=[HEAD_SLAB] * 3, ts=_tile(seq, (512, 256, 128, 64, 32, 16, 8)))


def _scores(q, k_ref, keys, tq):
    s = lax.dot_general(q, k_ref[0:keys, :], _DIMS["nt"], preferred_element_type=F32) * ATTN_SCALE
    row = lax.broadcasted_iota(jnp.int32, (tq, tq), 0)
    col = lax.broadcasted_iota(jnp.int32, (tq, tq), 1)
    diag = jnp.where(col <= row, s[:, keys - tq:], -1e30)
    return diag if keys == tq else jnp.concatenate([s[:, :keys - tq], diag], axis=1)


def _attn_fwd(qc, kc, vp, seq):
    tokens = qc.shape[0]
    tq = _tile(seq, (256, 128))
    nq = seq // tq

    def body(q_ref, k_ref, v_ref, o_ref, lse_ref):
        for i in range(nq):
            rows, keys = slice(i * tq, (i + 1) * tq), (i + 1) * tq
            s = _scores(q_ref[rows, :], k_ref, keys, tq)
            m = jnp.max(s, axis=-1, keepdims=True)
            p = jnp.exp(s - m)
            l = jnp.sum(p, axis=-1, keepdims=True)
            acc = jnp.dot(p.astype(BF16), v_ref[0:keys, :], preferred_element_type=F32)
            o_ref[rows, :] = (acc / l).astype(BF16)
            lse_ref[rows, :] = jnp.broadcast_to(m + jnp.log(l), (tq, HEAD_SLAB))

    spec = pl.BlockSpec((seq, HEAD_SLAB), lambda b, h: (b, h))
    return pl.pallas_call(
        body, name="attn_fwd", grid=(tokens // seq, N_HEADS),
        out_shape=(jax.ShapeDtypeStruct(qc.shape, BF16), jax.ShapeDtypeStruct(qc.shape, F32)),
        in_specs=[spec] * 3, out_specs=(spec, spec),
        compiler_params=_params("parallel", "parallel"),
    )(qc, kc, vp)


def _attn_bwd(qc, kc, vp, o, lse, do, seq):
    tokens = qc.shape[0]
    tq = _tile(seq, (256, 128))
    nq = seq // tq

    def body(q_ref, k_ref, v_ref, o_ref, lse_ref, do_ref, dq_ref, dk_ref, dv_ref):
        dk_ref[...] = jnp.zeros(dk_ref.shape, F32)
        dv_ref[...] = jnp.zeros(dv_ref.shape, F32)
        for i in range(nq):
            rows, keys = slice(i * tq, (i + 1) * tq), (i + 1) * tq
            q, dov = q_ref[rows, :], do_ref[rows, :]
            delta = jnp.sum(dov.astype(F32) * o_ref[rows, :].astype(F32), axis=-1, keepdims=True)
            s = _scores(q, k_ref, keys, tq)
            p = jnp.exp(s - jnp.tile(lse_ref[rows, :], (1, keys // HEAD_SLAB)))
            dp = lax.dot_general(dov, v_ref[0:keys, :], _DIMS["nt"], preferred_element_type=F32)
            ds = (p * (dp - delta) * ATTN_SCALE).astype(BF16)
            dq_ref[rows, :] = jnp.dot(ds, k_ref[0:keys, :], preferred_element_type=F32)
            dk_ref[0:keys, :] += lax.dot_general(ds, q, _DIMS["tn"], preferred_element_type=F32)
            dv_ref[0:keys, :] += lax.dot_general(p.astype(BF16), dov, _DIMS["tn"], preferred_element_type=F32)

    spec = pl.BlockSpec((seq, HEAD_SLAB), lambda b, h: (b, h))
    out = jax.ShapeDtypeStruct(qc.shape, F32)
    return pl.pallas_call(
        body, name="attn_bwd", grid=(tokens // seq, N_HEADS),
        out_shape=(out, out, out), in_specs=[spec] * 6, out_specs=(spec, spec, spec),
        compiler_params=_params("parallel", "parallel"),
    )(qc, kc, vp, o, lse, do)


def _adamw(w, g, m, v, name):
    rows, cols = w.shape
    whole = rows * cols * 4 <= ADAMW_WHOLE_BYTES
    tr = rows if whole else _tile(rows, (256, 128, 64, 32, 16, 8))
    c1 = 1.0 - ADAM_B1 ** ADAM_STEP
    c2 = 1.0 - ADAM_B2 ** ADAM_STEP

    def body(w_ref, g_ref, m_ref, v_ref, d_ref, nm_ref, nv_ref):
        gv = g_ref[...]
        nm = ADAM_B1 * m_ref[...] + (1.0 - ADAM_B1) * gv
        nv = ADAM_B2 * v_ref[...] + (1.0 - ADAM_B2) * (gv * gv)
        d_ref[...] = -ADAM_LR * ((nm / c1) / (jnp.sqrt(nv / c2) + ADAM_EPS) + ADAM_WD * w_ref[...])
        nm_ref[...] = nm
        nv_ref[...] = nv

    spec = pl.BlockSpec((tr, cols), lambda i: (i, 0))
    out = jax.ShapeDtypeStruct(w.shape, F32)
    return pl.pallas_call(
        body, name=name, grid=(rows // tr,), out_shape=(out, out, out),
        in_specs=[spec] * 4, out_specs=(spec, spec, spec),
        compiler_params=_params("parallel"),
    )(w, g, m, v)


def _adamw_landed(w, landed, m, v, name):
    rows, cols = w.shape
    tr = _tile(rows, (176, 128, 96, 64, 32, 16, 8))
    c1 = 1.0 - ADAM_B1 ** ADAM_STEP
    c2 = 1.0 - ADAM_B2 ** ADAM_STEP
    n_parts = len(landed)

    def body(*refs):
        w_ref, m_ref, v_ref = refs[:3]
        g_ref, d_ref, nm_ref, nv_ref = refs[3 + n_parts:]
        parts = []
        for x_ref in refs[3:3 + n_parts]:
            acc = x_ref[0].astype(F32)
            for d in range(1, N_DEV):
                acc = acc + x_ref[d].astype(F32)
            parts.append(acc)
        gv = parts[0] if n_parts == 1 else jnp.concatenate(parts, axis=1)
        nm = ADAM_B1 * m_ref[...] + (1.0 - ADAM_B1) * gv
        nv = ADAM_B2 * v_ref[...] + (1.0 - ADAM_B2) * (gv * gv)
        g_ref[...] = gv
        d_ref[...] = -ADAM_LR * ((nm / c1) / (jnp.sqrt(nv / c2) + ADAM_EPS) + ADAM_WD * w_ref[...])
        nm_ref[...] = nm
        nv_ref[...] = nv

    spec = pl.BlockSpec((tr, cols), lambda i: (i, 0))
    out = jax.ShapeDtypeStruct(w.shape, F32)
    return pl.pallas_call(
        body, name=name, grid=(rows // tr,), out_shape=(out, out, out, out),
        in_specs=[spec] * 3 + [pl.BlockSpec((N_DEV, tr, x.shape[2]), lambda i: (0, i, 0)) for x in landed],
        out_specs=(spec, spec, spec, spec),
        compiler_params=_params("parallel"),
    )(w, m, v, *landed)


def _mod_cols(c_all, w_ada, b_cols):
    def body(c_ref, w_ref, b_ref, act_ref, mod_ref):
        cv = c_ref[...]
        act = cv * _sigmoid(cv)
        act_ref[...] = act
        mod_ref[...] = jnp.dot(act.astype(BF16), w_ref[...].astype(BF16),
                               preferred_element_type=F32) + b_ref[...]

    n = w_ada.shape[1]
    return pl.pallas_call(
        body, name="mod_cols",
        out_shape=(jax.ShapeDtypeStruct(c_all.shape, F32), jax.ShapeDtypeStruct((c_all.shape[0], n), F32)),
        compiler_params=pltpu.CompilerParams(vmem_limit_bytes=VMEM_LIMIT),
    )(c_all, w_ada, b_cols)


def _ada_grads(c_act, dmod_all, dmod_cols):
    def body(c_ref, d_ref, dc_ref, gw_ref, gb_ref):
        gw_ref[...] = lax.dot_general(c_ref[...].astype(BF16), dc_ref[...].astype(BF16), _DIMS["tn"],
                                      preferred_element_type=F32)
        gb_ref[...] = _colsum(d_ref[...])

    return pl.pallas_call(
        body, name="ada_grads",
        out_shape=(jax.ShapeDtypeStruct((c_act.shape[1], dmod_cols.shape[1]), F32),
                   jax.ShapeDtypeStruct((1, dmod_all.shape[1]), F32)),
        compiler_params=pltpu.CompilerParams(vmem_limit_bytes=VMEM_LIMIT),
    )(c_act, dmod_all, dmod_cols)


def _flat_rows(a):
    flat = a.reshape(-1)
    pad = (-flat.shape[0]) % (LANES * SUBLANES)
    if pad:
        flat = jnp.pad(flat, (0, pad))
    return flat.reshape(-1, LANES)


def _gather_start(w, groups, tag, after=None, peers=None):
    shards = [[(w[n] if n in ROW_SHARDED else w[n].T).astype(BF16) for n in names] for names in groups]
    return _exchange_start_groups(shards, f"gather_{tag}_start", after=after, peers=peers)


def _gather_wait(handle, names, tag, after, peers=ALL_PEERS):
    landed = _exchange_wait(handle, f"gather_{tag}_wait", after=after, peers=peers)
    if peers == CHIP_PEERS:
        landed = [_sibling_forward(x, f"gather_{tag}_forward{i}") for i, x in enumerate(landed)]
    return {n: g.reshape(-1, g.shape[2]) for n, g in zip(names, landed)}


def _scatter_start(grads, names, tag, after=None):
    blocks = [grads[n].reshape(N_DEV, -1, grads[n].shape[1]) for n in names]
    return _exchange_start(blocks, f"scatter_{tag}_start", scatter=True, after=after)


def _scatter_wait(handle, names, tag, after):
    landed = _exchange_wait(handle, f"scatter_{tag}_wait", scatter=True, after=after)
    return {n: [x] for n, x in zip(names, landed)}


def _pack_small(vals):
    return jnp.concatenate([_flat_rows(v.astype(F32)) for v in vals], axis=0)


def _unpack_small(packed, like):
    out, row = [], 0
    for v in like:
        rows = _flat_rows(v).shape[0]
        out.append(packed[row:row + rows].reshape(-1)[:v.size].reshape(v.shape))
        row += rows
    return out


def _lanes128(*parts):
    out = jnp.zeros((HEAD_SLAB,), F32)
    for off, v in parts:
        out = lax.dynamic_update_slice(out, v.reshape(-1).astype(F32), (off,))
    return out.reshape(1, HEAD_SLAB)


def _step(x, c, positions, w, m, v, loss_target):
    nseq, seq, _ = x.shape
    tokens = nseq * seq
    me = _index(_my_pos())
    strip = lambda d: {n: (a[0] if a.ndim > 2 else a) for n, a in d.items()}
    shapes = {n: a.shape for n, a in w.items()}
    w, m, v = strip(w), strip(m), strip(v)

    c_all = _all_gather(c.reshape(-1, LANES), "gather_c").reshape(N_DEV * nseq, D_MODEL)
    n_ada = w["w_ada"].shape[1]
    b_cols = lax.dynamic_slice(w["b_ada"], (0, me * n_ada), (1, n_ada))
    c_act, mod_cols = _mod_cols(c_all, w["w_ada"], b_cols)
    mod_all = _all_gather(mod_cols, "gather_mod")
    mod = lax.dynamic_slice(mod_all, (0, me * nseq, 0), (N_DEV, nseq, n_ada))
    mod = mod.transpose(1, 0, 2).reshape(nseq, 3, 3, 1, D_MODEL)

    (h_f1i, h_f1o, h_mix_in, h_mix, h_f2), tok = _gather_start(
        w, (("w_ffn1_in",), ("w_ffn1_out",), MIXER[:1], MIXER[1:], ("w_ffn2_in", "w_ffn2_out")), "weights",
        after=mod_all, peers=[CHIP_PEERS] + [ALL_PEERS] * 4)
    started = tok[0:1, 0:1]

    g_q = _lanes128((0, w["q_norm_nope"]), (QK_NOPE, w["q_norm_rope"]))
    g_kn = _lanes128((0, w["k_norm_nope"]))
    g_kr = _lanes128((QK_NOPE, w["k_norm_rope"]))
    freq = ROPE_THETA ** (-jnp.arange(0, QK_ROPE, 2, dtype=F32) / QK_ROPE)
    inv_freq = _lanes128((QK_NOPE, jnp.concatenate([freq, freq])))
    pos = positions.reshape(tokens, 1).astype(F32)

    def sub(k, gamma, coef):
        return dict(gamma=w[gamma], shift=mod[:, k, 0] + started, scale=mod[:, k, 1], gate=coef * mod[:, k, 2])
    p1, pm, p2 = sub(0, "norm_ffn1", 0.5), sub(1, "norm_mix", 1.0), sub(2, "norm_ffn2", 0.5)
    t_big = _tile(tokens, (2048, 1024, 512))
    t_mid = _tile(tokens, (1024, 512))

    x0 = x.reshape(tokens, D_MODEL)
    h1 = _norm_mod_fwd(x0, p1, seq, "ffn1_norm")
    wt_f1i = _gather_wait(h_f1i, ("w_ffn1_in",), "ffn1_in", h1, peers=CHIP_PEERS)["w_ffn1_in"]
    g1, u1, a1 = _ffn_in_act(h1, wt_f1i, "ffn1_in")
    w_f1o = _gather_wait(h_f1o, ("w_ffn1_out",), "ffn1_out", a1)["w_ffn1_out"]
    x1, f1, h2 = _out_residual(a1, w_f1o, x0, p1["gate"], pm, seq, "ffn1_out")
    saved1 = (x0, h1, g1, u1, a1, wt_f1i, w_f1o)

    wt_in = _gather_wait(h_mix_in, MIXER[:1], "mix_in", h2)["w_in"]
    zero_rows = lambda rows: jnp.zeros((rows, D_MODEL), BF16)
    wt_p = wt_in[:512]
    wt_a = jnp.concatenate([wt_in[512:1152], zero_rows(QK_NOPE), wt_in[1152:1184], zero_rows(32)], axis=0)
    wt_g = wt_in[1184:]
    z_a = _mm(h2, wt_a, "nt", "mix_in_a", tm=t_big, tn=wt_a.shape[0])
    z_p = _mm(h2, wt_p, "nt", "mix_in_p", tm=t_big, tn=512)
    z_g = _mm(h2, wt_g, "nt", "mix_in_g", tm=t_big, tn=512)

    full = _gather_wait(h_mix, MIXER[1:], "mix", z_g)
    wtq_pad = jnp.pad(full["w_q_up"].reshape(N_HEADS, 96, Q_LORA), ((0, 0), (0, 32), (0, 0))).reshape(-1, Q_LORA)
    wtmla_pad = jnp.pad(full["w_mla_proj"].reshape(D_MODEL, N_HEADS, 64), ((0, 0), (0, 0), (64, 0))).reshape(D_MODEL, -1)
    wt_pool, wt_kv, w_mix_out = full["w_pool_proj"], full["w_kv_up"], full["w_out"]
    pooled, pg, ps = _pool_fwd(z_p, w["pool_grp"], w["pool_scale"], seq)
    br_pool = _mm(ps, wt_pool, "nt", "pool_proj", tm=t_big, tn=D_MODEL)
    qn, kvn = _latent_norm_fwd(z_a, w["q_a_norm"], w["kv_a_norm"], seq)
    qp = _mm(qn, wtq_pad, "nt", "q_up", tm=t_big, tn=D_MODEL)
    kv = _mm(kvn, wt_kv, "nt", "kv_up", tm=t_big, tn=D_MODEL)
    qc, kc, vp = _qk_prep_fwd(qp, kv, z_a, pos, g_q, g_kn, g_kr, inv_freq, seq)
    attn, lse = _attn_fwd(qc, kc, vp, seq)
    br_mla = _mm(attn, wtmla_pad, "nt", "mla_proj", tm=t_mid, tn=D_MODEL)

    def merge(rows):
        zg, bp, bm = rows[:3]
        return (_sigmoid(zg[:, :D_MODEL]) * bp + _sigmoid(zg[:, D_MODEL:]) * bm).astype(BF16)
    x2, o_mix, h3, merged = _out_residual([z_g, br_pool, br_mla], w_mix_out, x1, pm["gate"], p2, seq, "mix_out",
                                          lhs=merge)

    ffn2_w = _gather_wait(h_f2, ("w_ffn2_in", "w_ffn2_out"), "ffn2", h3)
    g2, u2, a2 = _ffn_in_act(h3, ffn2_w["w_ffn2_in"], "ffn2_in")
    dy, df2, dgate2, sq_err = _out_loss(a2, ffn2_w["w_ffn2_out"], x2, p2["gate"],
                                        loss_target.reshape(tokens, D_MODEL), seq, "ffn2_out")
    saved2 = (x2, h3, g2, u2, a2, ffn2_w["w_ffn2_in"], ffn2_w["w_ffn2_out"])

    grads = {}
    (dx2, do_mix, dsh2, dsc2, dgate_m, dg_ffn2), ops2 = _ffn_bwd_x(df2, dy, saved2, p2, seq, "ffn2", (o_mix, pm["gate"]))
    grads["w_ffn2_out"], grads["w_ffn2_in"] = _ffn_bwd_wout(ops2[0], ops2[1], "ffn2"), _ffn_bwd_win(ops2, "ffn2")
    s_f2, tok = _scatter_start(grads, ("w_ffn2_in", "w_ffn2_out"), "ffn2")

    grads["w_out"] = _mm(merged, do_mix, "tn", "mix_bwd_wout", out_dtype=BF16, tm=512, tn=D_MODEL)

    def merge_bwd(rows, bats, vecs):
        dmv, zg, bp, bm = rows
        s_p, s_m = _sigmoid(zg[:, :D_MODEL]), _sigmoid(zg[:, D_MODEL:])
        dzg = jnp.concatenate([dmv * bp * s_p * (1.0 - s_p), dmv * bm * s_m * (1.0 - s_m)], axis=1)
        return [dmv * s_p, dmv * s_m, dzg], [], []
    dbr_pool, dbr_mla, dz_g = _rowmap("mix_bwd_dmerged", merge_bwd, seq, [do_mix, z_g, br_pool, br_mla],
                                      row_outs=[(D_MODEL, BF16), (D_MODEL, BF16), (2 * D_MODEL, BF16)],
                                      mm=(w_mix_out, "nt"))

    grads["w_pool_proj"] = _mm(dbr_pool, ps, "tn", "pool_bwd_wproj", out_dtype=BF16, tm=512, tn=POOL_WIDTH)
    dps = _mm(dbr_pool, wt_pool, "nn", "pool_bwd_dps", tm=t_big, tn=POOL_WIDTH)
    dz_p, dgrp, dpool_scale = _pool_bwd(dps, pooled, pg, w["pool_grp"], w["pool_scale"] + tok[0:1, 0:1], seq)

    dwtmla_pad = _mm(dbr_mla, attn, "tn", "mla_bwd_wproj", out_dtype=BF16, tm=512, tn=D_MODEL)
    grads["w_mla_proj"] = dwtmla_pad.reshape(D_MODEL, N_HEADS, HEAD_SLAB)[:, :, 64:].reshape(D_MODEL, -1)
    d_attn = _mm(dbr_mla, wtmla_pad, "nn", "mla_bwd_dattn", out_dtype=BF16, tm=t_mid, tn=D_MODEL)
    dqc, dkc, dvp = _attn_bwd(qc, kc, vp, attn, lse, d_attn, seq)
    dqp, dkv, dkr, dg_q, dg_kn, dg_kr = _qk_prep_bwd(dqc, dkc, dvp, qp, kv, z_a, pos, g_q, g_kn, g_kr, inv_freq, seq)
    dwtq_pad = _mm(dqp, qn, "tn", "q_up_bwd_w", out_dtype=BF16, tm=512, tn=Q_LORA)
    grads["w_q_up"] = dwtq_pad.reshape(N_HEADS, HEAD_SLAB, Q_LORA)[:, :96].reshape(-1, Q_LORA)
    grads["w_kv_up"] = _mm(dkv, kvn, "tn", "kv_up_bwd_w", out_dtype=BF16, tm=512, tn=KV_LORA)
    dqn = _mm(dqp, wtq_pad, "nn", "q_up_bwd_x", tm=t_big, tn=Q_LORA)
    dkvn = _mm(dkv, wt_kv, "nn", "kv_up_bwd_x", tm=t_big, tn=KV_LORA)
    dz_a, dg_qa, dg_kva = _latent_norm_bwd(dqn, dkvn, dkr, z_a, w["q_a_norm"], w["kv_a_norm"], seq)

    dwt_a = _mm(dz_a, h2, "tn", "mix_in_bwd_wa", out_dtype=BF16, tm=256, tn=D_MODEL)
    dwt_p = _mm(dz_p, h2, "tn", "mix_in_bwd_wp", out_dtype=BF16, tm=512, tn=D_MODEL)
    dwt_g = _mm(dz_g, h2, "tn", "mix_in_bwd_wg", out_dtype=BF16, tm=512, tn=D_MODEL)
    grads["w_in"] = jnp.concatenate([dwt_p, dwt_a[:640], dwt_a[704:736], dwt_g], axis=0)

    small_early = [dg_ffn2.reshape(w["norm_ffn2"].shape), dgrp, dpool_scale, dg_qa, dg_kva, dg_q[:, :QK_NOPE],
                   dg_q[:, QK_NOPE:QK_NOPE + QK_ROPE], dg_kn[:, :QK_NOPE], dg_kr[:, QK_NOPE:QK_NOPE + QK_ROPE]]
    s_small, tok = _exchange_start([_pack_small(small_early)], "gather_small_start")
    s_mix, tok = _scatter_start(grads, MIXER, "mix", after=tok)
    dh2 = [(dz_a, wt_a), (dz_p, wt_p), (dz_g, wt_g)]
    pm_tied = dict(pm, scale=pm["scale"] + tok[0:1, 0:1])
    dx1, df1, dsh_m, dsc_m, dgate1, dg_mix = _norm_mod_bwd(dh2, x1, dx2, pm_tied, seq, "mix_bwd_norm", (f1, p1["gate"]))

    handles = {}

    def ffn1_early(a, df):
        grads["w_ffn1_out"] = _ffn_bwd_wout(a, df, "ffn1")
        handles["f1o"], token = _scatter_start(grads, ("w_ffn1_out",), "ffn1_out")
        return token

    def ffn1_mid(operands):
        first = _ffn_bwd_win(operands, "ffn1", half=0)
        handles["f1i0"], token = _exchange_start([first.reshape(N_DEV, -1, first.shape[1])],
                                                 "scatter_ffn1_in0_start", scatter=True)
        return token

    (dx0, dsh1, dsc1, dg_ffn1), ops1 = _ffn_bwd_x(df1, dx1, saved1, p1, seq, "ffn1", early=ffn1_early,
                                                     mid=ffn1_mid)
    s_f1o = handles["f1o"]

    dmod = jnp.stack([jnp.stack([dsh1, dsc1, 0.5 * dgate1], axis=1),
                      jnp.stack([dsh_m, dsc_m, dgate_m], axis=1),
                      jnp.stack([dsh2, dsc2, 0.5 * dgate2], axis=1)], axis=1)
    n_dmod = nseq * 9 * D_MODEL // LANES
    tail = _all_gather(jnp.concatenate([dmod.reshape(-1, LANES), _flat_rows(dg_ffn1), _flat_rows(dg_mix),
                                        _flat_rows(sq_err)], axis=0), "gather_dmod")
    dmod_all = tail[:, :n_dmod].reshape(N_DEV * nseq, 9 * D_MODEL)

    second = _ffn_bwd_win(ops1, "ffn1", after=tail, half=1)
    s_second, tok = _exchange_start([second.reshape(N_DEV, -1, second.shape[1])], "scatter_ffn1_in1_start",
                                    scatter=True, after=tail)
    s_f1i = (handles["f1i0"], s_second)

    dmod_cols = lax.dynamic_slice(dmod_all, (0, me * n_ada), (N_DEV * nseq, n_ada)) + tok[0:1, 0:1]
    g_w_ada, g_b_ada = _ada_grads(c_act, dmod_all, dmod_cols)
    tail_sum = _sum_blocks(tail[:, n_dmod:], "sum_tail")
    g_norm_ffn1 = tail_sum[:SUBLANES].reshape(1, D_MODEL)
    g_norm_mix = tail_sum[SUBLANES:2 * SUBLANES].reshape(1, D_MODEL)
    loss = 0.5 * jnp.sum(tail_sum[2 * SUBLANES:]) * (1.0 / D_MODEL)
    small_all = _exchange_wait(s_small, "gather_small_wait", after=g_b_ada)[0]
    small_sum = _sum_blocks(small_all, "sum_small")
    small = dict(zip(SMALL[2:], _unpack_small(small_sum, [w[n] for n in SMALL[2:]])))
    grad_w = dict(small, w_ada=g_w_ada, b_ada=g_b_ada, norm_ffn1=g_norm_ffn1, norm_mix=g_norm_mix)

    delta, new_m, new_v = {}, {}, {}

    def update(names, landed=None):
        for n in names:
            if landed is None:
                delta[n], new_m[n], new_v[n] = _adamw(w[n], grad_w[n], m[n], v[n], f"adamw_{n}")
            elif n in KEPT_TRANSPOSED:
                res = _adamw_landed(w[n].T, landed[n], m[n].T, v[n].T, f"adamw_{n}")
                grad_w[n], delta[n], new_m[n], new_v[n] = (r.T for r in res)
            elif n in ROW_SHARDED:
                grad_w[n], delta[n], new_m[n], new_v[n] = _adamw_landed(w[n], landed[n], m[n], v[n], f"adamw_{n}")
            else:
                grad_w[n] = _sum_blocks(landed[n][0], f"sum_{n}").T
                delta[n], new_m[n], new_v[n] = _adamw(w[n], grad_w[n], m[n], v[n], f"adamw_{n}")

    update(("w_ada",))
    rep = ("b_ada",) + SMALL
    d_s, m_s, v_s = _adamw(_pack_small([w[n] for n in rep]), _pack_small([grad_w[n] for n in rep]),
                           _pack_small([m[n] for n in rep]), _pack_small([v[n] for n in rep]), "adamw_small")
    like = [w[n] for n in rep]
    for dst, packed in ((delta, d_s), (new_m, m_s), (new_v, v_s)):
        dst.update(zip(rep, _unpack_small(packed, like)))
    update(("w_ffn2_in", "w_ffn2_out"), _scatter_wait(s_f2, ("w_ffn2_in", "w_ffn2_out"), "ffn2", after=d_s))
    update(MIXER, _scatter_wait(s_mix, MIXER, "mix", after=delta["w_ffn2_out"]))
    update(("w_ffn1_out",), _scatter_wait(s_f1o, ("w_ffn1_out",), "ffn1_out", after=delta["w_out"]))
    halves = [_exchange_wait(h, f"scatter_ffn1_in{i}_wait", scatter=True, after=delta["w_ffn1_out"])[0]
              for i, h in enumerate(s_f1i)]
    update(("w_ffn1_in",), {"w_ffn1_in": halves})

    lead = lambda d: [d[n].reshape(shapes[n]) for n in WEIGHTS]
    return (loss, dx0.reshape(x.shape), *lead(grad_w), *lead(delta), *lead(new_m), *lead(new_v))


def kernel(x, c, positions, w_ada, b_ada, norm_ffn1, w_ffn1_in, w_ffn1_out, norm_mix, w_in, pool_grp, pool_scale, w_pool_proj, q_a_norm, w_q_up, kv_a_norm, w_kv_up, q_norm_nope, q_norm_rope, k_norm_nope, k_norm_rope, w_mla_proj, w_out, norm_ffn2, w_ffn2_in, w_ffn2_out, loss_target, m_w_ada, m_b_ada, m_norm_ffn1, m_w_ffn1_in, m_w_ffn1_out, m_norm_mix, m_w_in, m_pool_grp, m_pool_scale, m_w_pool_proj, m_q_a_norm, m_w_q_up, m_kv_a_norm, m_w_kv_up, m_q_norm_nope, m_q_norm_rope, m_k_norm_nope, m_k_norm_rope, m_w_mla_proj, m_w_out, m_norm_ffn2, m_w_ffn2_in, m_w_ffn2_out, v_w_ada, v_b_ada, v_norm_ffn1, v_w_ffn1_in, v_w_ffn1_out, v_norm_mix, v_w_in, v_pool_grp, v_pool_scale, v_w_pool_proj, v_q_a_norm, v_w_q_up, v_kv_a_norm, v_w_kv_up, v_q_norm_nope, v_q_norm_rope, v_k_norm_nope, v_k_norm_rope, v_w_mla_proj, v_w_out, v_norm_ffn2, v_w_ffn2_in, v_w_ffn2_out):
    w = dict(w_ada=w_ada, b_ada=b_ada, norm_ffn1=norm_ffn1, w_ffn1_in=w_ffn1_in, w_ffn1_out=w_ffn1_out, norm_mix=norm_mix, w_in=w_in, pool_grp=pool_grp, pool_scale=pool_scale, w_pool_proj=w_pool_proj, q_a_norm=q_a_norm, w_q_up=w_q_up, kv_a_norm=kv_a_norm, w_kv_up=w_kv_up, q_norm_nope=q_norm_nope, q_norm_rope=q_norm_rope, k_norm_nope=k_norm_nope, k_norm_rope=k_norm_rope, w_mla_proj=w_mla_proj, w_out=w_out, norm_ffn2=norm_ffn2, w_ffn2_in=w_ffn2_in, w_ffn2_out=w_ffn2_out)
    m = dict(w_ada=m_w_ada, b_ada=m_b_ada, norm_ffn1=m_norm_ffn1, w_ffn1_in=m_w_ffn1_in, w_ffn1_out=m_w_ffn1_out, norm_mix=m_norm_mix, w_in=m_w_in, pool_grp=m_pool_grp, pool_scale=m_pool_scale, w_pool_proj=m_w_pool_proj, q_a_norm=m_q_a_norm, w_q_up=m_w_q_up, kv_a_norm=m_kv_a_norm, w_kv_up=m_w_kv_up, q_norm_nope=m_q_norm_nope, q_norm_rope=m_q_norm_rope, k_norm_nope=m_k_norm_nope, k_norm_rope=m_k_norm_rope, w_mla_proj=m_w_mla_proj, w_out=m_w_out, norm_ffn2=m_norm_ffn2, w_ffn2_in=m_w_ffn2_in, w_ffn2_out=m_w_ffn2_out)
    v = dict(w_ada=v_w_ada, b_ada=v_b_ada, norm_ffn1=v_norm_ffn1, w_ffn1_in=v_w_ffn1_in, w_ffn1_out=v_w_ffn1_out, norm_mix=v_norm_mix, w_in=v_w_in, pool_grp=v_pool_grp, pool_scale=v_pool_scale, w_pool_proj=v_w_pool_proj, q_a_norm=v_q_a_norm, w_q_up=v_w_q_up, kv_a_norm=v_kv_a_norm, w_kv_up=v_w_kv_up, q_norm_nope=v_q_norm_nope, q_norm_rope=v_q_norm_rope, k_norm_nope=v_k_norm_nope, k_norm_rope=v_k_norm_rope, w_mla_proj=v_w_mla_proj, w_out=v_w_out, norm_ffn2=v_norm_ffn2, w_ffn2_in=v_w_ffn2_in, w_ffn2_out=v_w_ffn2_out)
    return _step(x, c, positions, w, m, v, loss_target)
```

```python
import functools
import math

import jax
import jax.numpy as jnp
from jax import lax
from jax.experimental import pallas as pl
from jax.experimental.pallas import tpu as pltpu

F32 = jnp.float32
BF16 = jnp.bfloat16
MESH = pl.DeviceIdType.MESH
AXES = ("x", "y", "c")
N_DEV = 8

D_MODEL = 1024
D_FF = 2816
N_HEADS = 8
HEAD_SLAB = 128
QK_NOPE = 64
QK_ROPE = 32
POOL_WIDTH = 512
POOL_GROUPS = 4
POOL_GROUP_DIM = 128
Q_LORA = 384
KV_LORA = 256
ROPE_THETA = 10000.0
ATTN_SCALE = 1.0 / math.sqrt(QK_NOPE + QK_ROPE)
NORM_EPS = 1e-6
ADAM_LR, ADAM_B1, ADAM_B2, ADAM_EPS, ADAM_WD, ADAM_STEP = 0.001, 0.9, 0.999, 1e-08, 0.01, 10

LANES = 128
SUBLANES = 8
VMEM_LIMIT = 52 * 1024 * 1024
ADAMW_WHOLE_BYTES = 3 << 19
SUM_WHOLE_BYTES = 4 << 20

BIG = ("w_ffn1_in", "w_ffn1_out", "w_in", "w_pool_proj", "w_q_up", "w_kv_up",
       "w_mla_proj", "w_out", "w_ffn2_in", "w_ffn2_out")
ROW_SHARDED = ("w_ffn1_out", "w_out", "w_ffn2_out")
MIXER = ("w_in", "w_pool_proj", "w_q_up", "w_kv_up", "w_mla_proj", "w_out")
KEPT_TRANSPOSED = ("w_ffn1_in", "w_ffn2_in", "w_in", "w_q_up")
SMALL = ("norm_ffn1", "norm_mix", "norm_ffn2", "pool_grp", "pool_scale", "q_a_norm",
         "kv_a_norm", "q_norm_nope", "q_norm_rope", "k_norm_nope", "k_norm_rope")
WEIGHTS = ("w_ada", "b_ada", "norm_ffn1", "w_ffn1_in", "w_ffn1_out", "norm_mix", "w_in",
           "pool_grp", "pool_scale", "w_pool_proj", "q_a_norm", "w_q_up", "kv_a_norm",
           "w_kv_up", "q_norm_nope", "q_norm_rope", "k_norm_nope", "k_norm_rope",
           "w_mla_proj", "w_out", "norm_ffn2", "w_ffn2_in", "w_ffn2_out")


def _params(*sem):
    return pltpu.CompilerParams(dimension_semantics=sem, vmem_limit_bytes=VMEM_LIMIT)


def _tile(n, cands):
    for c in cands:
        if n % c == 0:
            return c
    return n


def _my_pos():
    return lax.axis_index("x"), lax.axis_index("y"), lax.axis_index("c")


def _flip(pos, k):
    x, y, c = pos
    fx, fy, fc = (k >> 2) & 1, (k >> 1) & 1, k & 1
    return ((1 - x) if fx else x, (1 - y) if fy else y, (1 - c) if fc else c)


def _index(pos):
    x, y, c = pos
    return 4 * x + 2 * y + c


def _exchange(arrays, name, scatter=False):
    n = len(arrays)

    def body(*refs):
        ins, outs = refs[:n], refs[n:2 * n]
        send_sems, recv_sems, local_sems = refs[2 * n:]
        me = _my_pos()
        mine, sends = [], []
        for a in range(n):
            own = ins[a].at[_index(me)] if scatter else ins[a]
            cp = pltpu.make_async_copy(own, outs[a].at[_index(me)], local_sems.at[a])
            cp.start()
            mine.append(cp)
        for k in range(1, N_DEV):
            peer = _flip(me, k)
            for a in range(n):
                cp = pltpu.make_async_remote_copy(
                    src_ref=ins[a].at[_index(peer)] if scatter else ins[a],
                    dst_ref=outs[a].at[_index(me)],
                    send_sem=send_sems.at[a, k - 1], recv_sem=recv_sems.at[a, k - 1],
                    device_id=peer, device_id_type=MESH)
                cp.start()
                sends.append(cp)
        for k in range(1, N_DEV):
            peer = _flip(me, k)
            for a in range(n):
                pltpu.make_async_remote_copy(
                    src_ref=ins[a].at[_index(me)] if scatter else ins[a],
                    dst_ref=outs[a].at[_index(peer)],
                    send_sem=send_sems.at[a, k - 1], recv_sem=recv_sems.at[a, k - 1],
                    device_id=peer, device_id_type=MESH).wait_recv()
        for cp in sends:
            cp.wait_send()
        for cp in mine:
            cp.wait()

    shape = lambda x: x.shape if scatter else (N_DEV,) + x.shape
    return pl.pallas_call(
        body, name=name,
        out_shape=tuple(jax.ShapeDtypeStruct(shape(x), x.dtype) for x in arrays),
        in_specs=[pl.BlockSpec(memory_space=pl.ANY)] * n,
        out_specs=tuple(pl.BlockSpec(memory_space=pl.ANY) for _ in arrays),
        scratch_shapes=[pltpu.SemaphoreType.DMA((n, N_DEV - 1)),
                        pltpu.SemaphoreType.DMA((n, N_DEV - 1)),
                        pltpu.SemaphoreType.DMA((n,))],
    )(*arrays)


def _all_gather(x, name):
    return _exchange([x], name)[0]


_HBM = pl.BlockSpec(memory_space=pltpu.HBM)
_SEM = pl.BlockSpec(memory_space=pltpu.SEMAPHORE)
_ANY = pl.BlockSpec(memory_space=pl.ANY)
_EFFECT = pltpu.SideEffectType.DATAFLOW_SIDE_EFFECTING


def _split_copy(ins, lands, send_sems, recv_sems, a, k, me, scatter, incoming):
    peer = _flip(me, k)
    block = me if incoming else peer
    return pltpu.make_async_remote_copy(
        src_ref=ins[a].at[_index(block)] if scatter else ins[a],
        dst_ref=lands[a].at[_index(peer if incoming else me)],
        send_sem=send_sems.at[a * (N_DEV - 1) + k - 1], recv_sem=recv_sems.at[a * (N_DEV - 1) + k - 1],
        device_id=peer, device_id_type=MESH)


ALL_PEERS = tuple(range(1, N_DEV))
CHIP_PEERS = (1, 2, 4, 6)


def _exchange_start_groups(groups, name, scatter=False, after=None, peers=None):
    peers = peers or [ALL_PEERS] * len(groups)
    sizes = [len(g) for g in groups]
    first = [sum(sizes[:i]) for i in range(len(sizes))]
    n, ng = sum(sizes), len(sizes)
    after = jnp.zeros((SUBLANES, LANES), F32) if after is None else after

    def body(*refs):
        ins, lands = refs[:n], refs[n:2 * n]
        sems = refs[2 * n + 1:2 * n + 1 + 2 * ng]
        me = _my_pos()
        for g in range(ng):
            part = slice(first[g], first[g] + sizes[g])
            for k in peers[g]:
                for a in range(sizes[g]):
                    _split_copy(ins[part], lands[part], sems[2 * g], sems[2 * g + 1], a, k, me, scatter, False).start()
        refs[-1][...] = jnp.zeros((SUBLANES, LANES), F32)

    shape = lambda x: x.shape if scatter else (N_DEV,) + x.shape
    hbm = lambda x: pltpu.with_memory_space_constraint(x, pltpu.HBM)
    srcs = [hbm(x) for g in groups for x in g]
    zones = [hbm(lax.empty(shape(x), x.dtype)) for g in groups for x in g]
    sem_shapes = [pltpu.SemaphoreType.DMA((s * (N_DEV - 1),)) for s in sizes for _ in range(2)]
    out = pl.pallas_call(
        body, name=name,
        out_shape=(*sem_shapes, *[pltpu.HBM(x.shape, x.dtype) for x in srcs + zones],
                   jax.ShapeDtypeStruct((SUBLANES, LANES), F32)),
        in_specs=[_HBM] * (2 * n) + [_ANY],
        out_specs=(*[_SEM] * (2 * ng), *[_HBM] * (2 * n), pl.BlockSpec(memory_space=pltpu.VMEM)),
        input_output_aliases={i: 2 * ng + i for i in range(2 * n)},
        compiler_params=pltpu.CompilerParams(has_side_effects=_EFFECT),
    )(*srcs, *zones, after)
    bufs = out[2 * ng:-1]
    handles = [(out[2 * g], out[2 * g + 1], *bufs[first[g]:first[g] + sizes[g]],
                *bufs[n + first[g]:n + first[g] + sizes[g]]) for g in range(ng)]
    return handles, out[-1]


def _exchange_start(arrays, name, scatter=False, after=None):
    handles, token = _exchange_start_groups([arrays], name, scatter, after)
    return handles[0], token


def _exchange_wait(handle, name, scatter=False, after=None, peers=ALL_PEERS):
    send_sems, recv_sems = handle[0], handle[1]
    n = (len(handle) - 2) // 2
    after = jnp.zeros((SUBLANES, LANES), F32) if after is None else after

    def body(*refs):
        ins, lands = refs[:n], refs[n:2 * n]
        send, recv = refs[2 * n], refs[2 * n + 1]
        me = _my_pos()
        for k in peers:
            for a in range(n):
                _split_copy(ins, lands, send, recv, a, k, me, scatter, False).wait_send()
                _split_copy(ins, lands, send, recv, a, k, me, scatter, True).wait_recv()

    bufs = handle[2:]
    out = pl.pallas_call(
        body, name=name,
        out_shape=tuple(pltpu.HBM(x.shape, x.dtype) for x in bufs),
        in_specs=[_HBM] * (2 * n) + [_SEM, _SEM, _ANY],
        out_specs=tuple([_HBM] * (2 * n)),
        input_output_aliases={i: i for i in range(2 * n)},
        compiler_params=pltpu.CompilerParams(has_side_effects=_EFFECT),
    )(*bufs, send_sems, recv_sems, after)
    me = _index(_my_pos())
    landed = []
    for src, land in zip(out[:n], out[n:]):
        own = lax.dynamic_slice_in_dim(src, me, 1, axis=0) if scatter else src[None]
        landed.append(lax.dynamic_update_slice_in_dim(land, own, me, axis=0))
    return landed


def _sibling_forward(x, name):
    flips = [k for k in CHIP_PEERS if k != 1]

    def body(x_ref, o_ref, send_sems, recv_sems):
        me = _my_pos()
        sibling = _flip(me, 1)
        sends = []
        for i, k in enumerate(flips):
            block = o_ref.at[_index(_flip(me, k))]
            cp = pltpu.make_async_remote_copy(src_ref=block, dst_ref=block, send_sem=send_sems.at[i],
                                              recv_sem=recv_sems.at[i], device_id=sibling, device_id_type=MESH)
            cp.start()
            sends.append(cp)
        for i, k in enumerate(flips):
            block = o_ref.at[_index(_flip(sibling, k))]
            pltpu.make_async_remote_copy(src_ref=block, dst_ref=block, send_sem=send_sems.at[i],
                                         recv_sem=recv_sems.at[i], device_id=sibling, device_id_type=MESH).wait_recv()
        for cp in sends:
            cp.wait_send()

    return pl.pallas_call(
        body, name=name, out_shape=jax.ShapeDtypeStruct(x.shape, x.dtype),
        in_specs=[_ANY], out_specs=_ANY, input_output_aliases={0: 0},
        scratch_shapes=[pltpu.SemaphoreType.DMA((len(flips),)), pltpu.SemaphoreType.DMA((len(flips),))],
    )(x)


def _sum_blocks(x, name):
    n, rows, cols = x.shape
    whole = x.size * x.dtype.itemsize <= SUM_WHOLE_BYTES
    tr = rows if whole else _tile(rows, (512, 256, 128, 64, 32, 16, 8))

    def body(x_ref, o_ref):
        acc = x_ref[0].astype(F32)
        for d in range(1, n):
            acc = acc + x_ref[d].astype(F32)
        o_ref[...] = acc

    return pl.pallas_call(
        body, name=name,
        out_shape=jax.ShapeDtypeStruct((rows, cols), F32),
        grid=(rows // tr,),
        in_specs=[pl.BlockSpec((n, tr, cols), lambda i: (0, i, 0))],
        out_specs=pl.BlockSpec((tr, cols), lambda i: (i, 0)),
        compiler_params=_params("parallel"),
    )(x)


_DIMS = {"nn": (((1,), (0,)), ((), ())), "nt": (((1,), (1,)), ((), ())), "tn": (((0,), (0,)), ((), ()))}


def _mm(a, b, mode, name, out_dtype=F32, tm=None, tn=None, add=None, after=None, b_cols=None):
    if mode == "tn":
        kdim, m = a.shape
    else:
        m, kdim = a.shape
    n = b.shape[0] if mode == "nt" else b.shape[1]
    tm = tm or _tile(m, (512, 256, 128))
    tn = tn or _tile(n, (512, 256, 128))
    j0 = 0
    if b_cols is not None:
        j0, n = b_cols[0], b_cols[1] * tn
    dims = _DIMS[mode]

    def body(*refs):
        refs = refs if after is None else refs[1:]
        acc = lax.dot_general(refs[0][...].astype(BF16), refs[1][...].astype(BF16), dims,
                              preferred_element_type=F32)
        if add is not None:
            acc = acc + refs[2][...]
        refs[-1][...] = acc.astype(out_dtype)

    a_spec = (pl.BlockSpec((kdim, tm), lambda i, j: (0, i)) if mode == "tn"
              else pl.BlockSpec((tm, kdim), lambda i, j: (i, 0)))
    b_spec = (pl.BlockSpec((tn, kdim), lambda i, j: (j, 0)) if mode == "nt"
              else pl.BlockSpec((kdim, tn), lambda i, j: (0, j + j0)))
    o_spec = pl.BlockSpec((tm, tn), lambda i, j: (i, j))
    in_specs, args = [a_spec, b_spec], [a, b]
    if add is not None:
        in_specs.append(o_spec)
        args.append(add)
    if after is not None:
        in_specs.insert(0, _ANY)
        args.insert(0, after)
    return pl.pallas_call(
        body, name=name, out_shape=jax.ShapeDtypeStruct((m, n), out_dtype), grid=(m // tm, n // tn),
        in_specs=in_specs, out_specs=o_spec,
        compiler_params=_params("parallel", "parallel"),
    )(*args)


def _rowmap(name, fn, seq, rows, bats=(), vecs=(), row_outs=(), bat_outs=(), vec_outs=(), ts=None, mm=None, lhs=None,
            after=None, mm_sum=True):
    mms = [] if mm is None else (mm if isinstance(mm, list) else [mm])
    rows = [r if isinstance(r, tuple) else (r, r.shape[1], 0) for r in rows]
    tokens = rows[0][0].shape[0]
    nseq = tokens // seq
    ts = ts or _tile(seq, (512, 256, 128, 64, 32, 16, 8))
    nt = seq // ts
    n_r, n_b, n_v = len(rows), len(bats), len(vecs)
    n_ro, n_bo = len(row_outs), len(bat_outs)

    def accumulate(ref, val, first):
        @pl.when(first)
        def _():
            ref[...] = val.reshape(ref.shape)

        @pl.when(jnp.logical_not(first))
        def _():
            ref[...] += val.reshape(ref.shape)

    def body(*refs):
        n_in = n_r + n_b + n_v + len(mms) + (after is not None)
        ins, outs = refs[:n_in], refs[n_in:]
        b_vals = [r[0] for r in ins[n_r:n_r + n_b]]
        v_vals = [r[...] for r in ins[n_r + n_b:n_r + n_b + n_v]]
        r_vals = [r[...] for r in ins[:n_r]]
        if mms:
            lefts = r_vals[:len(mms)] if lhs is None else [lhs(r_vals)]
            parts = [lax.dot_general(left.astype(BF16), b_ref[...].astype(BF16), _DIMS[mode],
                                     preferred_element_type=F32)
                     for left, b_ref, (_, mode) in zip(lefts, ins[n_r + n_b + n_v:], mms)]
            accs = [functools.reduce(lambda x, y: x + y, parts)] if mm_sum else parts
            r_vals = accs + r_vals[len(mms):] if lhs is None else accs + [lefts[0]] + r_vals
        ro, bo, vo = fn(r_vals, b_vals, v_vals)
        for ref, val in zip(outs[:n_ro], ro):
            ref[...] = val.astype(ref.dtype)
        b, i = pl.program_id(0), pl.program_id(1)
        for ref, val in zip(outs[n_ro:n_ro + n_bo], bo):
            accumulate(ref, val, i == 0)
        for ref, val in zip(outs[n_ro + n_bo:], vo):
            accumulate(ref, val, jnp.logical_and(i == 0, b == 0))

    in_specs = [pl.BlockSpec((ts, w), functools.partial(lambda b, i, cb: (b * nt + i, cb), cb=cb))
                for _, w, cb in rows]
    in_specs += [pl.BlockSpec((1, 1, v.shape[2]), lambda b, i: (b, 0, 0)) for v in bats]
    in_specs += [pl.BlockSpec((1, v.shape[1]), lambda b, i: (0, 0)) for v in vecs]
    extra = [b_arr for b_arr, _ in mms]
    in_specs += [pl.BlockSpec(b_arr.shape, lambda b, i: (0, 0), pipeline_mode=pl.Buffered(1)) for b_arr in extra]
    if after is not None:
        in_specs.append(_ANY)
        extra.append(after)
    out_shape = [jax.ShapeDtypeStruct((tokens, f), dt) for f, dt in row_outs]
    out_specs = [pl.BlockSpec((ts, f), lambda b, i: (b * nt + i, 0)) for f, _ in row_outs]
    out_shape += [jax.ShapeDtypeStruct((nseq, 1, f), F32) for f in bat_outs]
    out_specs += [pl.BlockSpec((1, 1, f), lambda b, i: (b, 0, 0)) for f in bat_outs]
    out_shape += [jax.ShapeDtypeStruct((1, f), F32) for f in vec_outs]
    out_specs += [pl.BlockSpec((1, f), lambda b, i: (0, 0)) for f in vec_outs]
    return pl.pallas_call(
        body, name=name, out_shape=tuple(out_shape), grid=(nseq, nt),
        in_specs=in_specs, out_specs=tuple(out_specs),
        compiler_params=_params("arbitrary", "arbitrary"),
    )(*([r[0] for r in rows] + list(bats) + list(vecs) + extra))


def _colsum(v):
    return jnp.sum(v, axis=0, keepdims=True)


def _rstd(x, width=None):
    width = width or x.shape[-1]
    return lax.rsqrt(jnp.sum(x * x, axis=-1, keepdims=True) * (1.0 / width) + NORM_EPS)


def _norm_bwd(dy, x, r, g, width=None):
    width = width or x.shape[-1]
    xhat = x * r
    dxhat = dy * g
    dx = r * (dxhat - xhat * (jnp.sum(dxhat * xhat, axis=-1, keepdims=True) * (1.0 / width)))
    return dx, dy * xhat


def _sigmoid(x):
    return 0.5 * jnp.tanh(0.5 * x) + 0.5


def _norm_mod(xv, g, sh, sc):
    return xv * _rstd(xv) * g * (1.0 + sc) + sh


def _norm_mod_fwd(x, p, seq, name):
    def fn(rows, bats, vecs):
        return [_norm_mod(rows[0], vecs[0], bats[0], bats[1])], [], []
    return _rowmap(name, fn, seq, [x], [p["shift"], p["scale"]], [p["gamma"]], row_outs=[(D_MODEL, BF16)])[0]


def _norm_mod_bwd(dh, x, dres, p, seq, name, prev=None, after=None):
    products = dh if isinstance(dh, list) else None
    lefts = [l for l, _ in products] if products else [dh]
    def fn(rows, bats, vecs):
        dhv, xv, dr = rows[:3]
        sc, g = bats[0], vecs[0]
        r = _rstd(xv)
        dxn, dg = _norm_bwd(dhv * (1.0 + sc), xv, r, g)
        dx = dr + dxn
        ro, bo = [dx], [_colsum(dhv), _colsum(dhv * (xv * r * g))]
        if prev is not None:
            ro.append(bats[1] * dx)
            bo.append(_colsum(dx * rows[3].astype(F32)))
        return ro, bo, [_colsum(dg)]
    more = prev is not None
    return _rowmap(name, fn, seq, lefts + [x, dres] + ([prev[0]] if more else []),
                   [p["scale"]] + ([prev[1]] if more else []), [p["gamma"]],
                   row_outs=[(D_MODEL, F32)] + ([(D_MODEL, BF16)] if more else []),
                   bat_outs=[D_MODEL] * (3 if more else 2), vec_outs=[D_MODEL],
                   mm=[(r, "nn") for _, r in products] if products else None, after=after)


def _ffn_in_act(h, wt_in, name):
    tokens = h.shape[0]
    tm, tn = _tile(tokens, (2048, 1024, 512)), 256
    nj = D_FF // tn

    def body(h_ref, wg_ref, wu_ref, g_ref, u_ref, a_ref):
        hv = h_ref[...]
        g = lax.dot_general(hv, wg_ref[...], _DIMS["nt"], preferred_element_type=F32)
        u = lax.dot_general(hv, wu_ref[...], _DIMS["nt"], preferred_element_type=F32)
        g_ref[...] = g.astype(BF16)
        u_ref[...] = u.astype(BF16)
        a_ref[...] = (g * _sigmoid(g) * u).astype(BF16)

    o_spec = pl.BlockSpec((tm, tn), lambda i, j: (i, j))
    out = jax.ShapeDtypeStruct((tokens, D_FF), BF16)
    return pl.pallas_call(
        body, name=name, grid=(tokens // tm, nj), out_shape=(out, out, out),
        in_specs=[pl.BlockSpec((tm, D_MODEL), lambda i, j: (i, 0)),
                  pl.BlockSpec((tn, D_MODEL), lambda i, j: (j, 0)),
                  pl.BlockSpec((tn, D_MODEL), lambda i, j: (j + nj, 0))],
        out_specs=(o_spec, o_spec, o_spec),
        compiler_params=_params("parallel", "parallel"),
    )(h, wt_in, wt_in)


def _out_residual(a, w_out, res, gate, nxt, seq, name, lhs=None):
    def fn(rows, bats, vecs):
        acc, rv = rows[0], rows[-1]
        x_new = rv + bats[0] * acc
        made = [] if lhs is None else [rows[1]]
        return [x_new, acc, _norm_mod(x_new, vecs[0], bats[1], bats[2])] + made, [], []
    outs = [(D_MODEL, F32), (D_MODEL, BF16), (D_MODEL, BF16)] + ([] if lhs is None else [(D_MODEL, BF16)])
    return _rowmap(name, fn, seq, (a if lhs is not None else [a]) + [res], [gate, nxt["shift"], nxt["scale"]],
                   [nxt["gamma"]], row_outs=outs, ts=_tile(seq, (512, 256, 128)), mm=(w_out, "nn"), lhs=lhs)


def _out_loss(a, w_out, res, gate, target, seq, name):
    def fn(rows, bats, vecs):
        acc, rv, tv = rows
        err = rv + bats[0] * acc - tv
        dy = err * (1.0 / D_MODEL)
        return [dy, bats[0] * dy], [_colsum(dy * acc)], [_colsum(err * err)]
    return _rowmap(name, fn, seq, [a, res, target], [gate], row_outs=[(D_MODEL, F32), (D_MODEL, BF16)],
                   bat_outs=[D_MODEL], vec_outs=[D_MODEL], ts=_tile(seq, (512, 256, 128)), mm=(w_out, "nn"))


def _ffn_bwd_x(df, dres, saved, p, seq, tag, prev=None, early=None, mid=None):
    x, h, g, u, a, w_in, w_out = saved
    first = None if early is None else early(a, df)

    def act_bwd(rows, bats, vecs):
        dav, gv, uv = rows[0], rows[1].astype(F32), rows[2].astype(F32)
        sg = _sigmoid(gv)
        silu = gv * sg
        dg = dav * uv * (sg * (1.0 + gv * (1.0 - sg)))
        return [jnp.concatenate([dg, dav * silu], axis=1)], [], []
    dgu = _rowmap(f"{tag}_bwd_da", act_bwd, seq, [df, g, u], row_outs=[(2 * D_FF, BF16)],
                  ts=_tile(seq, (512, 256, 128)), mm=(w_out, "nt"), after=first)[0]
    operands = (a, df, dgu, h)
    after = None if mid is None else mid(operands)
    return _norm_mod_bwd([(dgu, w_in)], x, dres, p, seq, f"{tag}_bwd_norm", prev, after=after), operands


def _ffn_bwd_wout(a, df, tag):
    return _mm(a, df, "tn", f"{tag}_bwd_wout", out_dtype=BF16, tm=256, tn=D_MODEL)


def _ffn_bwd_win(operands, tag, after=None, half=None):
    _, _, dgu, h = operands
    if half is None:
        return _mm(dgu, h, "tn", f"{tag}_bwd_win", out_dtype=BF16, tm=512, tn=D_MODEL, after=after)
    return _mm(dgu, h, "tn", f"{tag}_bwd_win{half}", out_dtype=BF16, tm=512, tn=D_MODEL // 2, after=after,
               b_cols=(half, 1))


def _shift_rows(v, k, forward):
    n = v.shape[0]
    row = lax.broadcasted_iota(jnp.int32, v.shape, 0)
    if forward:
        return jnp.where(row >= k, pltpu.roll(v, k, 0), 0.0)
    return jnp.where(row < n - k, pltpu.roll(v, n - k, 0), 0.0)


def _window_sums(v, forward):
    out, s, k = [], v, 1
    for _ in range(POOL_GROUPS):
        s = s + _shift_rows(s, k, forward)
        out.append(s)
        k *= 2
    return out


def _by_group(vals, g):
    out = vals[-1]
    for idx in range(len(vals) - 2, -1, -1):
        out = jnp.where(g == idx, vals[idx], out)
    return out


def _inv_count(shape, g):
    t1 = lax.broadcasted_iota(jnp.int32, shape, 0) + 1
    window = _by_group([jnp.int32(2 ** (i + 1)) for i in range(POOL_GROUPS)], g)
    return 1.0 / jnp.minimum(t1, window).astype(F32)


def _pool_fwd(u, grp, scale, seq):
    tokens = u.shape[0]

    def body(u_ref, grp_ref, sc_ref, pooled_ref, pg_ref, ps_ref):
        g = pl.program_id(1)
        uv = u_ref[...]
        sums = _by_group(_window_sums(uv, True), g)
        pooled = (sums * _inv_count(uv.shape, g) - uv).astype(BF16)
        pg = jnp.dot(pooled, grp_ref[0].astype(BF16), preferred_element_type=F32)
        pooled_ref[...] = pooled
        pg_ref[...] = pg
        ps_ref[...] = (pg * sc_ref[...]).astype(BF16)

    blk = pl.BlockSpec((seq, POOL_GROUP_DIM), lambda b, g: (b, g))
    return pl.pallas_call(
        body, name="pool_fwd", grid=(tokens // seq, POOL_GROUPS),
        out_shape=(jax.ShapeDtypeStruct(u.shape, BF16), jax.ShapeDtypeStruct(u.shape, F32),
                   jax.ShapeDtypeStruct(u.shape, BF16)),
        in_specs=[blk, pl.BlockSpec((1, POOL_GROUP_DIM, POOL_GROUP_DIM), lambda b, g: (g, 0, 0)),
                  pl.BlockSpec((1, POOL_GROUP_DIM), lambda b, g: (0, g))],
        out_specs=(blk, blk, blk),
        compiler_params=_params("parallel", "parallel"),
    )(u, grp, scale)


def _pool_bwd(dps, pooled, pg, grp, scale, seq):
    tokens = dps.shape[0]

    def body(dps_ref, pooled_ref, pg_ref, grp_ref, sc_ref, du_ref, dgrp_ref, dsc_ref):
        g, b = pl.program_id(0), pl.program_id(1)
        dpsv = dps_ref[...]
        dpg = (dpsv * sc_ref[...]).astype(BF16)
        dsc = _colsum(dpsv * pg_ref[...])
        dgrp = lax.dot_general(pooled_ref[...], dpg, _DIMS["tn"], preferred_element_type=F32)

        @pl.when(b == 0)
        def _():
            dsc_ref[...] = dsc
            dgrp_ref[0] = dgrp

        @pl.when(b > 0)
        def _():
            dsc_ref[...] += dsc
            dgrp_ref[0] += dgrp

        dpool = lax.dot_general(dpg, grp_ref[0].astype(BF16), _DIMS["nt"], preferred_element_type=F32)
        sums = _by_group(_window_sums(dpool * _inv_count(dpool.shape, g), False), g)
        du_ref[...] = (sums - dpool).astype(BF16)

    blk = pl.BlockSpec((seq, POOL_GROUP_DIM), lambda g, b: (b, g))
    grp_spec = pl.BlockSpec((1, POOL_GROUP_DIM, POOL_GROUP_DIM), lambda g, b: (g, 0, 0))
    vec_spec = pl.BlockSpec((1, POOL_GROUP_DIM), lambda g, b: (0, g))
    return pl.pallas_call(
        body, name="pool_bwd", grid=(POOL_GROUPS, tokens // seq),
        out_shape=(jax.ShapeDtypeStruct(dps.shape, BF16), jax.ShapeDtypeStruct(grp.shape, F32),
                   jax.ShapeDtypeStruct(scale.shape, F32)),
        in_specs=[blk, blk, blk, grp_spec, vec_spec],
        out_specs=(blk, grp_spec, vec_spec),
        compiler_params=_params("arbitrary", "arbitrary"),
    )(dps, pooled, pg, grp, scale)


def _lane(shape):
    return lax.broadcasted_iota(jnp.int32, shape, len(shape) - 1)


def _rot(y):
    lane = _lane(y.shape)
    r = jnp.where(lane < QK_NOPE + QK_ROPE // 2,
                  -pltpu.roll(y, HEAD_SLAB - QK_ROPE // 2, 1), pltpu.roll(y, QK_ROPE // 2, 1))
    return jnp.where(jnp.logical_and(lane >= QK_NOPE, lane < QK_NOPE + QK_ROPE), r, 0.0)


def _part_rstd(x):
    sq = x * x
    nope = _lane(x.shape) < QK_NOPE
    s_nope = jnp.sum(jnp.where(nope, sq, 0.0), axis=-1, keepdims=True)
    s_rope = jnp.sum(sq, axis=-1, keepdims=True) - s_nope
    return jnp.where(nope, lax.rsqrt(s_nope * (1.0 / QK_NOPE) + NORM_EPS),
                     lax.rsqrt(s_rope * (1.0 / QK_ROPE) + NORM_EPS))


def _part_norm_bwd(dy, x, r, g):
    nope = _lane(x.shape) < QK_NOPE
    xhat = x * r
    dxhat = dy * g
    prod = dxhat * xhat
    m_nope = jnp.sum(jnp.where(nope, prod, 0.0), axis=-1, keepdims=True)
    m_rope = jnp.sum(prod, axis=-1, keepdims=True) - m_nope
    mean = jnp.where(nope, m_nope * (1.0 / QK_NOPE), m_rope * (1.0 / QK_ROPE))
    return r * (dxhat - xhat * mean), dy * xhat


def _latent_proj_norm(h, wt_a, g_q, g_kv, seq):
    def fn(rows, bats, vecs):
        z = rows[0]
        q, kv = z[:, :Q_LORA], z[:, Q_LORA:Q_LORA + KV_LORA]
        return [z, q * _rstd(q) * vecs[0], kv * _rstd(kv) * vecs[1]], [], []
    return _rowmap("mix_in_a", fn, seq, [h], vecs=[g_q, g_kv], mm=(wt_a, "nt"),
                   row_outs=[(wt_a.shape[0], F32), (Q_LORA, BF16), (KV_LORA, BF16)])


def _latent_norm_bwd(dqp, wtq_pad, dkv, wt_kv, dkr, z_a, g_q, g_kv, seq):
    def fn(rows, bats, vecs):
        dq, dkv, dkrv, z = rows
        q, kv = z[:, :Q_LORA], z[:, Q_LORA:Q_LORA + KV_LORA]
        dxq, dgq = _norm_bwd(dq, q, _rstd(q), vecs[0])
        dxkv, dgkv = _norm_bwd(dkv, kv, _rstd(kv), vecs[1])
        return [jnp.concatenate([dxq, dxkv, dkrv], axis=1)], [], [_colsum(dgq), _colsum(dgkv)]
    return _rowmap("latent_norm_bwd", fn, seq, [dqp, dkv, dkr, z_a], vecs=[g_q, g_kv],
                   row_outs=[(Q_LORA + KV_LORA + HEAD_SLAB, BF16)], vec_outs=[Q_LORA, KV_LORA],
                   mm=[(wtq_pad, "nn"), (wt_kv, "nn")], mm_sum=False)


def _qk_prep_fwd(qn, wtq_pad, kvn, wt_kv, z_a, pos, g_q, g_kn, g_kr, inv_freq, seq):
    def fn(rows, bats, vecs):
        qv, kvv, kr, p = rows
        gq, gkn, gkr, invf = vecs
        ang = p * invf
        cos, sin = jnp.cos(ang), jnp.sin(ang)
        nope = _lane(kr.shape) < QK_NOPE
        krn = kr * _rstd(kr, QK_ROPE) * gkr
        krr = krn * cos + _rot(krn) * sin
        qs, ks, vs = [], [], []
        for h in range(N_HEADS):
            xq = qv[:, h * HEAD_SLAB:(h + 1) * HEAD_SLAB]
            y = xq * _part_rstd(xq) * gq
            qs.append(y * cos + _rot(y) * sin)
            xk = kvv[:, h * HEAD_SLAB:(h + 1) * HEAD_SLAB]
            kn = jnp.where(nope, xk, 0.0)
            ks.append(jnp.where(nope, kn * _rstd(kn, QK_NOPE) * gkn, krr))
            vs.append(jnp.where(nope, 0.0, xk))
        return [jnp.concatenate(v, axis=1) for v in (qs, ks, vs)] + [qv, kvv], [], []
    width = N_HEADS * HEAD_SLAB
    return _rowmap("qk_prep", fn, seq, [qn, kvn, (z_a, HEAD_SLAB, 5), pos], vecs=[g_q, g_kn, g_kr, inv_freq],
                   row_outs=[(width, BF16)] * 3 + [(width, F32)] * 2, mm=[(wtq_pad, "nt"), (wt_kv, "nt")],
                   mm_sum=False)


def _qk_prep_bwd(dqc, dkc, dvp, qp, kv, z_a, pos, g_q, g_kn, g_kr, inv_freq, seq):
    def fn(rows, bats, vecs):
        dq, dk, dv, qv, kvv, kr, p = rows
        gq, gkn, gkr, invf = vecs
        ang = p * invf
        cos, sin = jnp.cos(ang), jnp.sin(ang)
        nope = _lane(kr.shape) < QK_NOPE
        dqs, dkvs = [], []
        dgq = jnp.zeros((1, HEAD_SLAB), F32)
        dgkn = jnp.zeros((1, HEAD_SLAB), F32)
        dkrr = jnp.zeros(kr.shape, F32)
        for h in range(N_HEADS):
            sl = slice(h * HEAD_SLAB, (h + 1) * HEAD_SLAB)
            dyr = dq[:, sl]
            dy = dyr * cos - _rot(dyr * sin)
            xq = qv[:, sl]
            dx, dg = _part_norm_bwd(dy, xq, _part_rstd(xq), gq)
            dqs.append(dx)
            dgq = dgq + _colsum(dg)
            dkh = dk[:, sl]
            dkrr = dkrr + jnp.where(nope, 0.0, dkh)
            kn = jnp.where(nope, kvv[:, sl], 0.0)
            dxk, dgk = _norm_bwd(jnp.where(nope, dkh, 0.0), kn, _rstd(kn, QK_NOPE), gkn, QK_NOPE)
            dgkn = dgkn + _colsum(dgk)
            dkvs.append(jnp.where(nope, dxk, dv[:, sl]))
        dkrn = dkrr * cos - _rot(dkrr * sin)
        dkr, dgkr = _norm_bwd(dkrn, kr, _rstd(kr, QK_ROPE), gkr, QK_ROPE)
        return ([jnp.concatenate(dqs, axis=1), jnp.concatenate(dkvs, axis=1), dkr], [],
                [dgq, dgkn, _colsum(dgkr)])
    width = N_HEADS * HEAD_SLAB
    return _rowmap("qk_prep_bwd", fn, seq, [dqc, dkc, dvp, qp, kv, (z_a, HEAD_SLAB, 5), pos],
                   vecs=[g_q, g_kn, g_kr, inv_freq],
                   row_outs=[(width, BF16), (width, BF16), (HEAD_SLAB, F32)],
                   vec_outs=[HEAD_SLAB] * 3, ts=_tile(seq, (512, 256, 128, 64, 32, 16, 8)))


def _scores(q, k_ref, keys, tq):
    s = lax.dot_general(q, k_ref[0:keys, :], _DIMS["nt"], preferred_element_type=F32) * ATTN_SCALE
    row = lax.broadcasted_iota(jnp.int32, (tq, tq), 0)
    col = lax.broadcasted_iota(jnp.int32, (tq, tq), 1)
    diag = jnp.where(col <= row, s[:, keys - tq:], -1e30)
    return diag if keys == tq else jnp.concatenate([s[:, :keys - tq], diag], axis=1)


def _attn_fwd(qc, kc, vp, seq):
    tokens = qc.shape[0]
    tq = _tile(seq, (256, 128))
    nq = seq // tq

    def body(q_ref, k_ref, v_ref, o_ref, lse_ref):
        for i in range(nq):
            rows, keys = slice(i * tq, (i + 1) * tq), (i + 1) * tq
            s = _scores(q_ref[rows, :], k_ref, keys, tq)
            m = jnp.max(s, axis=-1, keepdims=True)
            p = jnp.exp(s - m)
            l = jnp.sum(p, axis=-1, keepdims=True)
            acc = jnp.dot(p.astype(BF16), v_ref[0:keys, :], preferred_element_type=F32)
            o_ref[rows, :] = (acc / l).astype(BF16)
            lse_ref[rows, :] = jnp.broadcast_to(m + jnp.log(l), (tq, HEAD_SLAB))

    spec = pl.BlockSpec((seq, HEAD_SLAB), lambda b, h: (b, h))
    return pl.pallas_call(
        body, name="attn_fwd", grid=(tokens // seq, N_HEADS),
        out_shape=(jax.ShapeDtypeStruct(qc.shape, BF16), jax.ShapeDtypeStruct(qc.shape, F32)),
        in_specs=[spec] * 3, out_specs=(spec, spec),
        compiler_params=_params("parallel", "parallel"),
    )(qc, kc, vp)


def _attn_bwd(qc, kc, vp, o, lse, do, seq):
    tokens = qc.shape[0]
    tq = _tile(seq, (256, 128))
    nq = seq // tq

    def body(q_ref, k_ref, v_ref, o_ref, lse_ref, do_ref, dq_ref, dk_ref, dv_ref):
        dk_ref[...] = jnp.zeros(dk_ref.shape, F32)
        dv_ref[...] = jnp.zeros(dv_ref.shape, F32)
        for i in range(nq):
            rows, keys = slice(i * tq, (i + 1) * tq), (i + 1) * tq
            q, dov = q_ref[rows, :], do_ref[rows, :]
            delta = jnp.sum(dov.astype(F32) * o_ref[rows, :].astype(F32), axis=-1, keepdims=True)
            s = _scores(q, k_ref, keys, tq)
            p = jnp.exp(s - jnp.tile(lse_ref[rows, :], (1, keys // HEAD_SLAB)))
            dp = lax.dot_general(dov, v_ref[0:keys, :], _DIMS["nt"], preferred_element_type=F32)
            ds = (p * (dp - delta) * ATTN_SCALE).astype(BF16)
            dq_ref[rows, :] = jnp.dot(ds, k_ref[0:keys, :], preferred_element_type=F32)
            dk_ref[0:keys, :] += lax.dot_general(ds, q, _DIMS["tn"], preferred_element_type=F32)
            dv_ref[0:keys, :] += lax.dot_general(p.astype(BF16), dov, _DIMS["tn"], preferred_element_type=F32)

    spec = pl.BlockSpec((seq, HEAD_SLAB), lambda b, h: (b, h))
    out = jax.ShapeDtypeStruct(qc.shape, F32)
    return pl.pallas_call(
        body, name="attn_bwd", grid=(tokens // seq, N_HEADS),
        out_shape=(out, out, out), in_specs=[spec] * 6, out_specs=(spec, spec, spec),
        compiler_params=_params("parallel", "parallel"),
    )(qc, kc, vp, o, lse, do)


def _adamw(w, g, m, v, name):
    rows, cols = w.shape
    whole = rows * cols * 4 <= ADAMW_WHOLE_BYTES
    tr = rows if whole else _tile(rows, (256, 128, 64, 32, 16, 8))
    c1 = 1.0 - ADAM_B1 ** ADAM_STEP
    c2 = 1.0 - ADAM_B2 ** ADAM_STEP

    def body(w_ref, g_ref, m_ref, v_ref, d_ref, nm_ref, nv_ref):
        gv = g_ref[...]
        nm = ADAM_B1 * m_ref[...] + (1.0 - ADAM_B1) * gv
        nv = ADAM_B2 * v_ref[...] + (1.0 - ADAM_B2) * (gv * gv)
        d_ref[...] = -ADAM_LR * ((nm / c1) / (jnp.sqrt(nv / c2) + ADAM_EPS) + ADAM_WD * w_ref[...])
        nm_ref[...] = nm
        nv_ref[...] = nv

    spec = pl.BlockSpec((tr, cols), lambda i: (i, 0))
    out = jax.ShapeDtypeStruct(w.shape, F32)
    return pl.pallas_call(
        body, name=name, grid=(rows // tr,), out_shape=(out, out, out),
        in_specs=[spec] * 4, out_specs=(spec, spec, spec),
        compiler_params=_params("parallel"),
    )(w, g, m, v)


def _adamw_landed(w, landed, m, v, name):
    rows, cols = w.shape
    tr = _tile(rows, (176, 128, 96, 64, 32, 16, 8))
    c1 = 1.0 - ADAM_B1 ** ADAM_STEP
    c2 = 1.0 - ADAM_B2 ** ADAM_STEP
    n_parts = len(landed)

    def body(*refs):
        w_ref, m_ref, v_ref = refs[:3]
        g_ref, d_ref, nm_ref, nv_ref = refs[3 + n_parts:]
        parts = []
        for x_ref in refs[3:3 + n_parts]:
            acc = x_ref[0].astype(F32)
            for d in range(1, N_DEV):
                acc = acc + x_ref[d].astype(F32)
            parts.append(acc)
        gv = parts[0] if n_parts == 1 else jnp.concatenate(parts, axis=1)
        nm = ADAM_B1 * m_ref[...] + (1.0 - ADAM_B1) * gv
        nv = ADAM_B2 * v_ref[...] + (1.0 - ADAM_B2) * (gv * gv)
        g_ref[...] = gv
        d_ref[...] = -ADAM_LR * ((nm / c1) / (jnp.sqrt(nv / c2) + ADAM_EPS) + ADAM_WD * w_ref[...])
        nm_ref[...] = nm
        nv_ref[...] = nv

    spec = pl.BlockSpec((tr, cols), lambda i: (i, 0))
    out = jax.ShapeDtypeStruct(w.shape, F32)
    return pl.pallas_call(
        body, name=name, grid=(rows // tr,), out_shape=(out, out, out, out),
        in_specs=[spec] * 3 + [pl.BlockSpec((N_DEV, tr, x.shape[2]), lambda i: (0, i, 0)) for x in landed],
        out_specs=(spec, spec, spec, spec),
        compiler_params=_params("parallel"),
    )(w, m, v, *landed)


def _mod_cols(c_all, w_ada, b_cols):
    def body(c_ref, w_ref, b_ref, act_ref, mod_ref):
        cv = c_ref[...]
        act = cv * _sigmoid(cv)
        act_ref[...] = act
        mod_ref[...] = jnp.dot(act.astype(BF16), w_ref[...].astype(BF16),
                               preferred_element_type=F32) + b_ref[...]

    n = w_ada.shape[1]
    return pl.pallas_call(
        body, name="mod_cols",
        out_shape=(jax.ShapeDtypeStruct(c_all.shape, F32), jax.ShapeDtypeStruct((c_all.shape[0], n), F32)),
        compiler_params=pltpu.CompilerParams(vmem_limit_bytes=VMEM_LIMIT),
    )(c_all, w_ada, b_cols)


def _ada_grads(c_act, dmod_all, dmod_cols):
    def body(c_ref, d_ref, dc_ref, gw_ref, gb_ref):
        gw_ref[...] = lax.dot_general(c_ref[...].astype(BF16), dc_ref[...].astype(BF16), _DIMS["tn"],
                                      preferred_element_type=F32)
        gb_ref[...] = _colsum(d_ref[...])

    return pl.pallas_call(
        body, name="ada_grads",
        out_shape=(jax.ShapeDtypeStruct((c_act.shape[1], dmod_cols.shape[1]), F32),
                   jax.ShapeDtypeStruct((1, dmod_all.shape[1]), F32)),
        compiler_params=pltpu.CompilerParams(vmem_limit_bytes=VMEM_LIMIT),
    )(c_act, dmod_all, dmod_cols)


def _flat_rows(a):
    flat = a.reshape(-1)
    pad = (-flat.shape[0]) % (LANES * SUBLANES)
    if pad:
        flat = jnp.pad(flat, (0, pad))
    return flat.reshape(-1, LANES)


def _gather_start(w, groups, tag, after=None, peers=None):
    shards = [[(w[n] if n in ROW_SHARDED else w[n].T).astype(BF16) for n in names] for names in groups]
    return _exchange_start_groups(shards, f"gather_{tag}_start", after=after, peers=peers)


def _gather_wait(handle, names, tag, after, peers=ALL_PEERS):
    landed = _exchange_wait(handle, f"gather_{tag}_wait", after=after, peers=peers)
    if peers == CHIP_PEERS:
        landed = [_sibling_forward(x, f"gather_{tag}_forward{i}") for i, x in enumerate(landed)]
    return {n: g.reshape(-1, g.shape[2]) for n, g in zip(names, landed)}


def _scatter_start(grads, names, tag, after=None):
    blocks = [grads[n].reshape(N_DEV, -1, grads[n].shape[1]) for n in names]
    return _exchange_start(blocks, f"scatter_{tag}_start", scatter=True, after=after)


def _scatter_wait(handle, names, tag, after):
    landed = _exchange_wait(handle, f"scatter_{tag}_wait", scatter=True, after=after)
    return {n: [x] for n, x in zip(names, landed)}


def _pack_small(vals):
    return jnp.concatenate([_flat_rows(v.astype(F32)) for v in vals], axis=0)


def _unpack_small(packed, like):
    out, row = [], 0
    for v in like:
        rows = _flat_rows(v).shape[0]
        out.append(packed[row:row + rows].reshape(-1)[:v.size].reshape(v.shape))
        row += rows
    return out


def _lanes128(*parts):
    out = jnp.zeros((HEAD_SLAB,), F32)
    for off, v in parts:
        out = lax.dynamic_update_slice(out, v.reshape(-1).astype(F32), (off,))
    return out.reshape(1, HEAD_SLAB)


def _step(x, c, positions, w, m, v, loss_target):
    nseq, seq, _ = x.shape
    tokens = nseq * seq
    me = _index(_my_pos())
    strip = lambda d: {n: (a[0] if a.ndim > 2 else a) for n, a in d.items()}
    shapes = {n: a.shape for n, a in w.items()}
    w, m, v = strip(w), strip(m), strip(v)

    c_all = _all_gather(c.reshape(-1, LANES), "gather_c").reshape(N_DEV * nseq, D_MODEL)
    n_ada = w["w_ada"].shape[1]
    b_cols = lax.dynamic_slice(w["b_ada"], (0, me * n_ada), (1, n_ada))
    c_act, mod_cols = _mod_cols(c_all, w["w_ada"], b_cols)
    mod_all = _all_gather(mod_cols, "gather_mod")
    mod = lax.dynamic_slice(mod_all, (0, me * nseq, 0), (N_DEV, nseq, n_ada))
    mod = mod.transpose(1, 0, 2).reshape(nseq, 3, 3, 1, D_MODEL)

    (h_f1i, h_f1o, h_mix_in, h_mix, h_f2), tok = _gather_start(
        w, (("w_ffn1_in",), ("w_ffn1_out",), MIXER[:1], MIXER[1:], ("w_ffn2_in", "w_ffn2_out")), "weights",
        after=mod_all, peers=[CHIP_PEERS] + [ALL_PEERS] * 4)
    started = tok[0:1, 0:1]

    g_q = _lanes128((0, w["q_norm_nope"]), (QK_NOPE, w["q_norm_rope"]))
    g_kn = _lanes128((0, w["k_norm_nope"]))
    g_kr = _lanes128((QK_NOPE, w["k_norm_rope"]))
    freq = ROPE_THETA ** (-jnp.arange(0, QK_ROPE, 2, dtype=F32) / QK_ROPE)
    inv_freq = _lanes128((QK_NOPE, jnp.concatenate([freq, freq])))
    pos = positions.reshape(tokens, 1).astype(F32)

    def sub(k, gamma, coef):
        return dict(gamma=w[gamma], shift=mod[:, k, 0] + started, scale=mod[:, k, 1], gate=coef * mod[:, k, 2])
    p1, pm, p2 = sub(0, "norm_ffn1", 0.5), sub(1, "norm_mix", 1.0), sub(2, "norm_ffn2", 0.5)
    t_big = _tile(tokens, (2048, 1024, 512))
    t_mid = _tile(tokens, (1024, 512))

    x0 = x.reshape(tokens, D_MODEL)
    h1 = _norm_mod_fwd(x0, p1, seq, "ffn1_norm")
    wt_f1i = _gather_wait(h_f1i, ("w_ffn1_in",), "ffn1_in", h1, peers=CHIP_PEERS)["w_ffn1_in"]
    g1, u1, a1 = _ffn_in_act(h1, wt_f1i, "ffn1_in")
    w_f1o = _gather_wait(h_f1o, ("w_ffn1_out",), "ffn1_out", a1)["w_ffn1_out"]
    x1, f1, h2 = _out_residual(a1, w_f1o, x0, p1["gate"], pm, seq, "ffn1_out")
    saved1 = (x0, h1, g1, u1, a1, wt_f1i, w_f1o)

    wt_in = _gather_wait(h_mix_in, MIXER[:1], "mix_in", h2)["w_in"]
    zero_rows = lambda rows: jnp.zeros((rows, D_MODEL), BF16)
    wt_p = wt_in[:512]
    wt_a = jnp.concatenate([wt_in[512:1152], zero_rows(QK_NOPE), wt_in[1152:1184], zero_rows(32)], axis=0)
    wt_g = wt_in[1184:]
    z_a, qn, kvn = _latent_proj_norm(h2, wt_a, w["q_a_norm"], w["kv_a_norm"], seq)
    z_p = _mm(h2, wt_p, "nt", "mix_in_p", tm=t_big, tn=512)
    z_g = _mm(h2, wt_g, "nt", "mix_in_g", tm=t_big, tn=512)

    full = _gather_wait(h_mix, MIXER[1:], "mix", z_g)
    wtq_pad = jnp.pad(full["w_q_up"].reshape(N_HEADS, 96, Q_LORA), ((0, 0), (0, 32), (0, 0))).reshape(-1, Q_LORA)
    wtmla_pad = jnp.pad(full["w_mla_proj"].reshape(D_MODEL, N_HEADS, 64), ((0, 0), (0, 0), (64, 0))).reshape(D_MODEL, -1)
    wt_pool, wt_kv, w_mix_out = full["w_pool_proj"], full["w_kv_up"], full["w_out"]
    pooled, pg, ps = _pool_fwd(z_p, w["pool_grp"], w["pool_scale"], seq)
    br_pool = _mm(ps, wt_pool, "nt", "pool_proj", tm=t_big, tn=D_MODEL)
    qc, kc, vp, qp, kv = _qk_prep_fwd(qn, wtq_pad, kvn, wt_kv, z_a, pos, g_q, g_kn, g_kr, inv_freq, seq)
    attn, lse = _attn_fwd(qc, kc, vp, seq)
    br_mla = _mm(attn, wtmla_pad, "nt", "mla_proj", tm=t_mid, tn=D_MODEL)

    def merge(rows):
        zg, bp, bm = rows[:3]
        return (_sigmoid(zg[:, :D_MODEL]) * bp + _sigmoid(zg[:, D_MODEL:]) * bm).astype(BF16)
    x2, o_mix, h3, merged = _out_residual([z_g, br_pool, br_mla], w_mix_out, x1, pm["gate"], p2, seq, "mix_out",
                                          lhs=merge)

    ffn2_w = _gather_wait(h_f2, ("w_ffn2_in", "w_ffn2_out"), "ffn2", h3)
    g2, u2, a2 = _ffn_in_act(h3, ffn2_w["w_ffn2_in"], "ffn2_in")
    dy, df2, dgate2, sq_err = _out_loss(a2, ffn2_w["w_ffn2_out"], x2, p2["gate"],
                                        loss_target.reshape(tokens, D_MODEL), seq, "ffn2_out")
    saved2 = (x2, h3, g2, u2, a2, ffn2_w["w_ffn2_in"], ffn2_w["w_ffn2_out"])

    grads = {}
    (dx2, do_mix, dsh2, dsc2, dgate_m, dg_ffn2), ops2 = _ffn_bwd_x(df2, dy, saved2, p2, seq, "ffn2", (o_mix, pm["gate"]))
    grads["w_ffn2_out"], grads["w_ffn2_in"] = _ffn_bwd_wout(ops2[0], ops2[1], "ffn2"), _ffn_bwd_win(ops2, "ffn2")
    s_f2, tok = _scatter_start(grads, ("w_ffn2_in", "w_ffn2_out"), "ffn2")

    grads["w_out"] = _mm(merged, do_mix, "tn", "mix_bwd_wout", out_dtype=BF16, tm=512, tn=D_MODEL)

    def merge_bwd(rows, bats, vecs):
        dmv, zg, bp, bm = rows
        s_p, s_m = _sigmoid(zg[:, :D_MODEL]), _sigmoid(zg[:, D_MODEL:])
        dzg = jnp.concatenate([dmv * bp * s_p * (1.0 - s_p), dmv * bm * s_m * (1.0 - s_m)], axis=1)
        return [dmv * s_p, dmv * s_m, dzg], [], []
    dbr_pool, dbr_mla, dz_g = _rowmap("mix_bwd_dmerged", merge_bwd, seq, [do_mix, z_g, br_pool, br_mla],
                                      row_outs=[(D_MODEL, BF16), (D_MODEL, BF16), (2 * D_MODEL, BF16)],
                                      mm=(w_mix_out, "nt"))

    grads["w_pool_proj"] = _mm(dbr_pool, ps, "tn", "pool_bwd_wproj", out_dtype=BF16, tm=512, tn=POOL_WIDTH)
    dps = _mm(dbr_pool, wt_pool, "nn", "pool_bwd_dps", tm=t_big, tn=POOL_WIDTH)
    dz_p, dgrp, dpool_scale = _pool_bwd(dps, pooled, pg, w["pool_grp"], w["pool_scale"] + tok[0:1, 0:1], seq)

    dwtmla_pad = _mm(dbr_mla, attn, "tn", "mla_bwd_wproj", out_dtype=BF16, tm=512, tn=D_MODEL)
    grads["w_mla_proj"] = dwtmla_pad.reshape(D_MODEL, N_HEADS, HEAD_SLAB)[:, :, 64:].reshape(D_MODEL, -1)
    d_attn = _mm(dbr_mla, wtmla_pad, "nn", "mla_bwd_dattn", out_dtype=BF16, tm=t_mid, tn=D_MODEL)
    dqc, dkc, dvp = _attn_bwd(qc, kc, vp, attn, lse, d_attn, seq)
    dqp, dkv, dkr, dg_q, dg_kn, dg_kr = _qk_prep_bwd(dqc, dkc, dvp, qp, kv, z_a, pos, g_q, g_kn, g_kr, inv_freq, seq)
    dwtq_pad = _mm(dqp, qn, "tn", "q_up_bwd_w", out_dtype=BF16, tm=512, tn=Q_LORA)
    grads["w_q_up"] = dwtq_pad.reshape(N_HEADS, HEAD_SLAB, Q_LORA)[:, :96].reshape(-1, Q_LORA)
    grads["w_kv_up"] = _mm(dkv, kvn, "tn", "kv_up_bwd_w", out_dtype=BF16, tm=512, tn=KV_LORA)
    dz_a, dg_qa, dg_kva = _latent_norm_bwd(dqp, wtq_pad, dkv, wt_kv, dkr, z_a, w["q_a_norm"], w["kv_a_norm"], seq)

    dwt_a = _mm(dz_a, h2, "tn", "mix_in_bwd_wa", out_dtype=BF16, tm=256, tn=D_MODEL)
    dwt_p = _mm(dz_p, h2, "tn", "mix_in_bwd_wp", out_dtype=BF16, tm=512, tn=D_MODEL)
    dwt_g = _mm(dz_g, h2, "tn", "mix_in_bwd_wg", out_dtype=BF16, tm=512, tn=D_MODEL)
    grads["w_in"] = jnp.concatenate([dwt_p, dwt_a[:640], dwt_a[704:736], dwt_g], axis=0)

    small_early = [dg_ffn2.reshape(w["norm_ffn2"].shape), dgrp, dpool_scale, dg_qa, dg_kva, dg_q[:, :QK_NOPE],
                   dg_q[:, QK_NOPE:QK_NOPE + QK_ROPE], dg_kn[:, :QK_NOPE], dg_kr[:, QK_NOPE:QK_NOPE + QK_ROPE]]
    s_small, tok = _exchange_start([_pack_small(small_early)], "gather_small_start")
    s_mix, tok = _scatter_start(grads, MIXER, "mix", after=tok)
    dh2 = [(dz_a, wt_a), (dz_p, wt_p), (dz_g, wt_g)]
    pm_tied = dict(pm, scale=pm["scale"] + tok[0:1, 0:1])
    dx1, df1, dsh_m, dsc_m, dgate1, dg_mix = _norm_mod_bwd(dh2, x1, dx2, pm_tied, seq, "mix_bwd_norm", (f1, p1["gate"]))

    handles = {}

    def ffn1_early(a, df):
        grads["w_ffn1_out"] = _ffn_bwd_wout(a, df, "ffn1")
        handles["f1o"], token = _scatter_start(grads, ("w_ffn1_out",), "ffn1_out")
        return token

    def ffn1_mid(operands):
        first = _ffn_bwd_win(operands, "ffn1", half=0)
        handles["f1i0"], token = _exchange_start([first.reshape(N_DEV, -1, first.shape[1])],
                                                 "scatter_ffn1_in0_start", scatter=True)
        return token

    (dx0, dsh1, dsc1, dg_ffn1), ops1 = _ffn_bwd_x(df1, dx1, saved1, p1, seq, "ffn1", early=ffn1_early,
                                                     mid=ffn1_mid)
    s_f1o = handles["f1o"]

    dmod = jnp.stack([jnp.stack([dsh1, dsc1, 0.5 * dgate1], axis=1),
                      jnp.stack([dsh_m, dsc_m, dgate_m], axis=1),
                      jnp.stack([dsh2, dsc2, 0.5 * dgate2], axis=1)], axis=1)
    n_dmod = nseq * 9 * D_MODEL // LANES
    tail = _all_gather(jnp.concatenate([dmod.reshape(-1, LANES), _flat_rows(dg_ffn1), _flat_rows(dg_mix),
                                        _flat_rows(sq_err)], axis=0), "gather_dmod")
    dmod_all = tail[:, :n_dmod].reshape(N_DEV * nseq, 9 * D_MODEL)

    second = _ffn_bwd_win(ops1, "ffn1", after=tail, half=1)
    s_second, tok = _exchange_start([second.reshape(N_DEV, -1, second.shape[1])], "scatter_ffn1_in1_start",
                                    scatter=True, after=tail)
    s_f1i = (handles["f1i0"], s_second)

    dmod_cols = lax.dynamic_slice(dmod_all, (0, me * n_ada), (N_DEV * nseq, n_ada)) + tok[0:1, 0:1]
    g_w_ada, g_b_ada = _ada_grads(c_act, dmod_all, dmod_cols)
    tail_sum = _sum_blocks(tail[:, n_dmod:], "sum_tail")
    g_norm_ffn1 = tail_sum[:SUBLANES].reshape(1, D_MODEL)
    g_norm_mix = tail_sum[SUBLANES:2 * SUBLANES].reshape(1, D_MODEL)
    loss = 0.5 * jnp.sum(tail_sum[2 * SUBLANES:]) * (1.0 / D_MODEL)
    small_all = _exchange_wait(s_small, "gather_small_wait", after=g_b_ada)[0]
    small_sum = _sum_blocks(small_all, "sum_small")
    small = dict(zip(SMALL[2:], _unpack_small(small_sum, [w[n] for n in SMALL[2:]])))
    grad_w = dict(small, w_ada=g_w_ada, b_ada=g_b_ada, norm_ffn1=g_norm_ffn1, norm_mix=g_norm_mix)

    delta, new_m, new_v = {}, {}, {}

    def update(names, landed=None):
        for n in names:
            if landed is None:
                delta[n], new_m[n], new_v[n] = _adamw(w[n], grad_w[n], m[n], v[n], f"adamw_{n}")
            elif n in KEPT_TRANSPOSED:
                res = _adamw_landed(w[n].T, landed[n], m[n].T, v[n].T, f"adamw_{n}")
                grad_w[n], delta[n], new_m[n], new_v[n] = (r.T for r in res)
            elif n in ROW_SHARDED:
                grad_w[n], delta[n], new_m[n], new_v[n] = _adamw_landed(w[n], landed[n], m[n], v[n], f"adamw_{n}")
            else:
                grad_w[n] = _sum_blocks(landed[n][0], f"sum_{n}").T
                delta[n], new_m[n], new_v[n] = _adamw(w[n], grad_w[n], m[n], v[n], f"adamw_{n}")

    update(("w_ada",))
    rep = ("b_ada",) + SMALL
    d_s, m_s, v_s = _adamw(_pack_small([w[n] for n in rep]), _pack_small([grad_w[n] for n in rep]),
                           _pack_small([m[n] for n in rep]), _pack_small([v[n] for n in rep]), "adamw_small")
    like = [w[n] for n in rep]
    for dst, packed in ((delta, d_s), (new_m, m_s), (new_v, v_s)):
        dst.update(zip(rep, _unpack_small(packed, like)))
    update(("w_ffn2_in", "w_ffn2_out"), _scatter_wait(s_f2, ("w_ffn2_in", "w_ffn2_out"), "ffn2", after=d_s))
    update(MIXER, _scatter_wait(s_mix, MIXER, "mix", after=delta["w_ffn2_out"]))
    update(("w_ffn1_out",), _scatter_wait(s_f1o, ("w_ffn1_out",), "ffn1_out", after=delta["w_out"]))
    halves = [_exchange_wait(h, f"scatter_ffn1_in{i}_wait", scatter=True, after=delta["w_ffn1_out"])[0]
              for i, h in enumerate(s_f1i)]
    update(("w_ffn1_in",), {"w_ffn1_in": halves})

    lead = lambda d: [d[n].reshape(shapes[n]) for n in WEIGHTS]
    return (loss, dx0.reshape(x.shape), *lead(grad_w), *lead(delta), *lead(new_m), *lead(new_v))


def kernel(x, c, positions, w_ada, b_ada, norm_ffn1, w_ffn1_in, w_ffn1_out, norm_mix, w_in, pool_grp, pool_scale, w_pool_proj, q_a_norm, w_q_up, kv_a_norm, w_kv_up, q_norm_nope, q_norm_rope, k_norm_nope, k_norm_rope, w_mla_proj, w_out, norm_ffn2, w_ffn2_in, w_ffn2_out, loss_target, m_w_ada, m_b_ada, m_norm_ffn1, m_w_ffn1_in, m_w_ffn1_out, m_norm_mix, m_w_in, m_pool_grp, m_pool_scale, m_w_pool_proj, m_q_a_norm, m_w_q_up, m_kv_a_norm, m_w_kv_up, m_q_norm_nope, m_q_norm_rope, m_k_norm_nope, m_k_norm_rope, m_w_mla_proj, m_w_out, m_norm_ffn2, m_w_ffn2_in, m_w_ffn2_out, v_w_ada, v_b_ada, v_norm_ffn1, v_w_ffn1_in, v_w_ffn1_out, v_norm_mix, v_w_in, v_pool_grp, v_pool_scale, v_w_pool_proj, v_q_a_norm, v_w_q_up, v_kv_a_norm, v_w_kv_up, v_q_norm_nope, v_q_norm_rope, v_k_norm_nope, v_k_norm_rope, v_w_mla_proj, v_w_out, v_norm_ffn2, v_w_ffn2_in, v_w_ffn2_out):
    w = dict(w_ada=w_ada, b_ada=b_ada, norm_ffn1=norm_ffn1, w_ffn1_in=w_ffn1_in, w_ffn1_out=w_ffn1_out, norm_mix=norm_mix, w_in=w_in, pool_grp=pool_grp, pool_scale=pool_scale, w_pool_proj=w_pool_proj, q_a_norm=q_a_norm, w_q_up=w_q_up, kv_a_norm=kv_a_norm, w_kv_up=w_kv_up, q_norm_nope=q_norm_nope, q_norm_rope=q_norm_rope, k_norm_nope=k_norm_nope, k_norm_rope=k_norm_rope, w_mla_proj=w_mla_proj, w_out=w_out, norm_ffn2=norm_ffn2, w_ffn2_in=w_ffn2_in, w_ffn2_out=w_ffn2_out)
    m = dict(w_ada=m_w_ada, b_ada=m_b_ada, norm_ffn1=m_norm_ffn1, w_ffn1_in=m_w_ffn1_in, w_ffn1_out=m_w_ffn1_out, norm_mix=m_norm_mix, w_in=m_w_in, pool_grp=m_pool_grp, pool_scale=m_pool_scale, w_pool_proj=m_w_pool_proj, q_a_norm=m_q_a_norm, w_q_up=m_w_q_up, kv_a_norm=m_kv_a_norm, w_kv_up=m_w_kv_up, q_norm_nope=m_q_norm_nope, q_norm_rope=m_q_norm_rope, k_norm_nope=m_k_norm_nope, k_norm_rope=m_k_norm_rope, w_mla_proj=m_w_mla_proj, w_out=m_w_out, norm_ffn2=m_norm_ffn2, w_ffn2_in=m_w_ffn2_in, w_ffn2_out=m_w_ffn2_out)
    v = dict(w_ada=v_w_ada, b_ada=v_b_ada, norm_ffn1=v_norm_ffn1, w_ffn1_in=v_w_ffn1_in, w_ffn1_out=v_w_ffn1_out, norm_mix=v_norm_mix, w_in=v_w_in, pool_grp=v_pool_grp, pool_scale=v_pool_scale, w_pool_proj=v_w_pool_proj, q_a_norm=v_q_a_norm, w_q_up=v_w_q_up, kv_a_norm=v_kv_a_norm, w_kv_up=v_w_kv_up, q_norm_nope=v_q_norm_nope, q_norm_rope=v_q_norm_rope, k_norm_nope=v_k_norm_nope, k_norm_rope=v_k_norm_rope, w_mla_proj=v_w_mla_proj, w_out=v_w_out, norm_ffn2=v_norm_ffn2, w_ffn2_in=v_w_ffn2_in, w_ffn2_out=v_w_ffn2_out)
    return _step(x, c, positions, w, m, v, loss_target)
```

```python
import functools
import math

import jax
import jax.numpy as jnp
from jax import lax
from jax.experimental import pallas as pl
from jax.experimental.pallas import tpu as pltpu

F32 = jnp.float32
BF16 = jnp.bfloat16
MESH = pl.DeviceIdType.MESH
AXES = ("x", "y", "c")
N_DEV = 8

D_MODEL = 1024
D_FF = 2816
N_HEADS = 8
HEAD_SLAB = 128
QK_NOPE = 64
QK_ROPE = 32
POOL_WIDTH = 512
POOL_GROUPS = 4
POOL_GROUP_DIM = 128
Q_LORA = 384
KV_LORA = 256
ROPE_THETA = 10000.0
ATTN_SCALE = 1.0 / math.sqrt(QK_NOPE + QK_ROPE)
NORM_EPS = 1e-6
ADAM_LR, ADAM_B1, ADAM_B2, ADAM_EPS, ADAM_WD, ADAM_STEP = 0.001, 0.9, 0.999, 1e-08, 0.01, 10

LANES = 128
SUBLANES = 8
VMEM_LIMIT = 52 * 1024 * 1024
ADAMW_WHOLE_BYTES = 3 << 19
SUM_WHOLE_BYTES = 4 << 20

BIG = ("w_ffn1_in", "w_ffn1_out", "w_in", "w_pool_proj", "w_q_up", "w_kv_up",
       "w_mla_proj", "w_out", "w_ffn2_in", "w_ffn2_out")
ROW_SHARDED = ("w_ffn1_out", "w_out", "w_ffn2_out")
MIXER = ("w_in", "w_pool_proj", "w_q_up", "w_kv_up", "w_mla_proj", "w_out")
KEPT_TRANSPOSED = ("w_ffn1_in", "w_ffn2_in", "w_in", "w_q_up")
SMALL = ("norm_ffn1", "norm_mix", "norm_ffn2", "pool_grp", "pool_scale", "q_a_norm",
         "kv_a_norm", "q_norm_nope", "q_norm_rope", "k_norm_nope", "k_norm_rope")
WEIGHTS = ("w_ada", "b_ada", "norm_ffn1", "w_ffn1_in", "w_ffn1_out", "norm_mix", "w_in",
           "pool_grp", "pool_scale", "w_pool_proj", "q_a_norm", "w_q_up", "kv_a_norm",
           "w_kv_up", "q_norm_nope", "q_norm_rope", "k_norm_nope", "k_norm_rope",
           "w_mla_proj", "w_out", "norm_ffn2", "w_ffn2_in", "w_ffn2_out")


def _params(*sem):
    return pltpu.CompilerParams(dimension_semantics=sem, vmem_limit_bytes=VMEM_LIMIT)


def _tile(n, cands):
    for c in cands:
        if n % c == 0:
            return c
    return n


def _my_pos():
    return lax.axis_index("x"), lax.axis_index("y"), lax.axis_index("c")


def _flip(pos, k):
    x, y, c = pos
    fx, fy, fc = (k >> 2) & 1, (k >> 1) & 1, k & 1
    return ((1 - x) if fx else x, (1 - y) if fy else y, (1 - c) if fc else c)


def _index(pos):
    x, y, c = pos
    return 4 * x + 2 * y + c


def _exchange(arrays, name, scatter=False):
    n = len(arrays)

    def body(*refs):
        ins, outs = refs[:n], refs[n:2 * n]
        send_sems, recv_sems, local_sems = refs[2 * n:]
        me = _my_pos()
        mine, sends = [], []
        for a in range(n):
            own = ins[a].at[_index(me)] if scatter else ins[a]
            cp = pltpu.make_async_copy(own, outs[a].at[_index(me)], local_sems.at[a])
            cp.start()
            mine.append(cp)
        for k in range(1, N_DEV):
            peer = _flip(me, k)
            for a in range(n):
                cp = pltpu.make_async_remote_copy(
                    src_ref=ins[a].at[_index(peer)] if scatter else ins[a],
                    dst_ref=outs[a].at[_index(me)],
                    send_sem=send_sems.at[a, k - 1], recv_sem=recv_sems.at[a, k - 1],
                    device_id=peer, device_id_type=MESH)
                cp.start()
                sends.append(cp)
        for k in range(1, N_DEV):
            peer = _flip(me, k)
            for a in range(n):
                pltpu.make_async_remote_copy(
                    src_ref=ins[a].at[_index(me)] if scatter else ins[a],
                    dst_ref=outs[a].at[_index(peer)],
                    send_sem=send_sems.at[a, k - 1], recv_sem=recv_sems.at[a, k - 1],
                    device_id=peer, device_id_type=MESH).wait_recv()
        for cp in sends:
            cp.wait_send()
        for cp in mine:
            cp.wait()

    shape = lambda x: x.shape if scatter else (N_DEV,) + x.shape
    return pl.pallas_call(
        body, name=name,
        out_shape=tuple(jax.ShapeDtypeStruct(shape(x), x.dtype) for x in arrays),
        in_specs=[pl.BlockSpec(memory_space=pl.ANY)] * n,
        out_specs=tuple(pl.BlockSpec(memory_space=pl.ANY) for _ in arrays),
        scratch_shapes=[pltpu.SemaphoreType.DMA((n, N_DEV - 1)),
                        pltpu.SemaphoreType.DMA((n, N_DEV - 1)),
                        pltpu.SemaphoreType.DMA((n,))],
    )(*arrays)


def _all_gather(x, name):
    return _exchange([x], name)[0]


_HBM = pl.BlockSpec(memory_space=pltpu.HBM)
_SEM = pl.BlockSpec(memory_space=pltpu.SEMAPHORE)
_ANY = pl.BlockSpec(memory_space=pl.ANY)
_EFFECT = pltpu.SideEffectType.DATAFLOW_SIDE_EFFECTING


def _split_copy(ins, lands, send_sems, recv_sems, a, k, me, scatter, incoming):
    peer = _flip(me, k)
    block = me if incoming else peer
    return pltpu.make_async_remote_copy(
        src_ref=ins[a].at[_index(block)] if scatter else ins[a],
        dst_ref=lands[a].at[_index(peer if incoming else me)],
        send_sem=send_sems.at[a * (N_DEV - 1) + k - 1], recv_sem=recv_sems.at[a * (N_DEV - 1) + k - 1],
        device_id=peer, device_id_type=MESH)


ALL_PEERS = tuple(range(1, N_DEV))
CHIP_PEERS = (1, 2, 4, 6)


def _exchange_start_groups(groups, name, scatter=False, after=None, peers=None):
    peers = peers or [ALL_PEERS] * len(groups)
    sizes = [len(g) for g in groups]
    first = [sum(sizes[:i]) for i in range(len(sizes))]
    n, ng = sum(sizes), len(sizes)
    after = jnp.zeros((SUBLANES, LANES), F32) if after is None else after

    def body(*refs):
        ins, lands = refs[:n], refs[n:2 * n]
        sems = refs[2 * n + 1:2 * n + 1 + 2 * ng]
        me = _my_pos()
        for g in range(ng):
            part = slice(first[g], first[g] + sizes[g])
            for k in peers[g]:
                for a in range(sizes[g]):
                    _split_copy(ins[part], lands[part], sems[2 * g], sems[2 * g + 1], a, k, me, scatter, False).start()
        refs[-1][...] = jnp.zeros((SUBLANES, LANES), F32)

    shape = lambda x: x.shape if scatter else (N_DEV,) + x.shape
    hbm = lambda x: pltpu.with_memory_space_constraint(x, pltpu.HBM)
    srcs = [hbm(x) for g in groups for x in g]
    zones = [hbm(lax.empty(shape(x), x.dtype)) for g in groups for x in g]
    sem_shapes = [pltpu.SemaphoreType.DMA((s * (N_DEV - 1),)) for s in sizes for _ in range(2)]
    out = pl.pallas_call(
        body, name=name,
        out_shape=(*sem_shapes, *[pltpu.HBM(x.shape, x.dtype) for x in srcs + zones],
                   jax.ShapeDtypeStruct((SUBLANES, LANES), F32)),
        in_specs=[_HBM] * (2 * n) + [_ANY],
        out_specs=(*[_SEM] * (2 * ng), *[_HBM] * (2 * n), pl.BlockSpec(memory_space=pltpu.VMEM)),
        input_output_aliases={i: 2 * ng + i for i in range(2 * n)},
        compiler_params=pltpu.CompilerParams(has_side_effects=_EFFECT),
    )(*srcs, *zones, after)
    bufs = out[2 * ng:-1]
    handles = [(out[2 * g], out[2 * g + 1], *bufs[first[g]:first[g] + sizes[g]],
                *bufs[n + first[g]:n + first[g] + sizes[g]]) for g in range(ng)]
    return handles, out[-1]


def _exchange_start(arrays, name, scatter=False, after=None):
    handles, token = _exchange_start_groups([arrays], name, scatter, after)
    return handles[0], token


def _exchange_wait(handle, name, scatter=False, after=None, peers=ALL_PEERS):
    send_sems, recv_sems = handle[0], handle[1]
    n = (len(handle) - 2) // 2
    after = jnp.zeros((SUBLANES, LANES), F32) if after is None else after

    def body(*refs):
        ins, lands = refs[:n], refs[n:2 * n]
        send, recv = refs[2 * n], refs[2 * n + 1]
        me = _my_pos()
        for k in peers:
            for a in range(n):
                _split_copy(ins, lands, send, recv, a, k, me, scatter, False).wait_send()
                _split_copy(ins, lands, send, recv, a, k, me, scatter, True).wait_recv()

    bufs = handle[2:]
    out = pl.pallas_call(
        body, name=name,
        out_shape=tuple(pltpu.HBM(x.shape, x.dtype) for x in bufs),
        in_specs=[_HBM] * (2 * n) + [_SEM, _SEM, _ANY],
        out_specs=tuple([_HBM] * (2 * n)),
        input_output_aliases={i: i for i in range(2 * n)},
        compiler_params=pltpu.CompilerParams(has_side_effects=_EFFECT),
    )(*bufs, send_sems, recv_sems, after)
    me = _index(_my_pos())
    landed = []
    for src, land in zip(out[:n], out[n:]):
        own = lax.dynamic_slice_in_dim(src, me, 1, axis=0) if scatter else src[None]
        landed.append(lax.dynamic_update_slice_in_dim(land, own, me, axis=0))
    return landed


def _sibling_forward(x, name):
    flips = [k for k in CHIP_PEERS if k != 1]

    def body(x_ref, o_ref, send_sems, recv_sems):
        me = _my_pos()
        sibling = _flip(me, 1)
        sends = []
        for i, k in enumerate(flips):
            block = o_ref.at[_index(_flip(me, k))]
            cp = pltpu.make_async_remote_copy(src_ref=block, dst_ref=block, send_sem=send_sems.at[i],
                                              recv_sem=recv_sems.at[i], device_id=sibling, device_id_type=MESH)
            cp.start()
            sends.append(cp)
        for i, k in enumerate(flips):
            block = o_ref.at[_index(_flip(sibling, k))]
            pltpu.make_async_remote_copy(src_ref=block, dst_ref=block, send_sem=send_sems.at[i],
                                         recv_sem=recv_sems.at[i], device_id=sibling, device_id_type=MESH).wait_recv()
        for cp in sends:
            cp.wait_send()

    return pl.pallas_call(
        body, name=name, out_shape=jax.ShapeDtypeStruct(x.shape, x.dtype),
        in_specs=[_ANY], out_specs=_ANY, input_output_aliases={0: 0},
        scratch_shapes=[pltpu.SemaphoreType.DMA((len(flips),)), pltpu.SemaphoreType.DMA((len(flips),))],
    )(x)


def _concat_rows(pieces, name):
    total = sum(rows for _, _, rows in pieces)
    arrays = []
    for a, _, _ in pieces:
        if not any(a is b for b in arrays):
            arrays.append(a)
    cols, dtype = arrays[0].shape[1], arrays[0].dtype

    def body(*refs):
        srcs, o_ref, sems = refs[:len(arrays)], refs[len(arrays)], refs[-1]
        copies, row = [], 0
        for i, (a, first, rows) in enumerate(pieces):
            src = srcs[[a is b for b in arrays].index(True)]
            cp = pltpu.make_async_copy(src.at[pl.ds(first, rows)], o_ref.at[pl.ds(row, rows)], sems.at[i])
            cp.start()
            copies.append(cp)
            row += rows
        for cp in copies:
            cp.wait()

    return pl.pallas_call(
        body, name=name, out_shape=jax.ShapeDtypeStruct((total, cols), dtype),
        in_specs=[_ANY] * len(arrays), out_specs=_ANY,
        scratch_shapes=[pltpu.SemaphoreType.DMA((len(pieces),))],
    )(*arrays)


def _sum_blocks(x, name):
    n, rows, cols = x.shape
    whole = x.size * x.dtype.itemsize <= SUM_WHOLE_BYTES
    tr = rows if whole else _tile(rows, (512, 256, 128, 64, 32, 16, 8))

    def body(x_ref, o_ref):
        acc = x_ref[0].astype(F32)
        for d in range(1, n):
            acc = acc + x_ref[d].astype(F32)
        o_ref[...] = acc

    return pl.pallas_call(
        body, name=name,
        out_shape=jax.ShapeDtypeStruct((rows, cols), F32),
        grid=(rows // tr,),
        in_specs=[pl.BlockSpec((n, tr, cols), lambda i: (0, i, 0))],
        out_specs=pl.BlockSpec((tr, cols), lambda i: (i, 0)),
        compiler_params=_params("parallel"),
    )(x)


_DIMS = {"nn": (((1,), (0,)), ((), ())), "nt": (((1,), (1,)), ((), ())), "tn": (((0,), (0,)), ((), ()))}


def _mm(a, b, mode, name, out_dtype=F32, tm=None, tn=None, add=None, after=None, b_cols=None):
    if mode == "tn":
        kdim, m = a.shape
    else:
        m, kdim = a.shape
    n = b.shape[0] if mode == "nt" else b.shape[1]
    tm = tm or _tile(m, (512, 256, 128))
    tn = tn or _tile(n, (512, 256, 128))
    j0 = 0
    if b_cols is not None:
        j0, n = b_cols[0], b_cols[1] * tn
    dims = _DIMS[mode]

    def body(*refs):
        refs = refs if after is None else refs[1:]
        acc = lax.dot_general(refs[0][...].astype(BF16), refs[1][...].astype(BF16), dims,
                              preferred_element_type=F32)
        if add is not None:
            acc = acc + refs[2][...]
        refs[-1][...] = acc.astype(out_dtype)

    a_spec = (pl.BlockSpec((kdim, tm), lambda i, j: (0, i)) if mode == "tn"
              else pl.BlockSpec((tm, kdim), lambda i, j: (i, 0)))
    b_spec = (pl.BlockSpec((tn, kdim), lambda i, j: (j, 0)) if mode == "nt"
              else pl.BlockSpec((kdim, tn), lambda i, j: (0, j + j0)))
    o_spec = pl.BlockSpec((tm, tn), lambda i, j: (i, j))
    in_specs, args = [a_spec, b_spec], [a, b]
    if add is not None:
        in_specs.append(o_spec)
        args.append(add)
    if after is not None:
        in_specs.insert(0, _ANY)
        args.insert(0, after)
    return pl.pallas_call(
        body, name=name, out_shape=jax.ShapeDtypeStruct((m, n), out_dtype), grid=(m // tm, n // tn),
        in_specs=in_specs, out_specs=o_spec,
        compiler_params=_params("parallel", "parallel"),
    )(*args)


def _rowmap(name, fn, seq, rows, bats=(), vecs=(), row_outs=(), bat_outs=(), vec_outs=(), ts=None, mm=None, lhs=None,
            after=None, mm_sum=True):
    mms = [] if mm is None else (mm if isinstance(mm, list) else [mm])
    rows = [r if isinstance(r, tuple) else (r, r.shape[1], 0) for r in rows]
    tokens = rows[0][0].shape[0]
    nseq = tokens // seq
    ts = ts or _tile(seq, (512, 256, 128, 64, 32, 16, 8))
    nt = seq // ts
    n_r, n_b, n_v = len(rows), len(bats), len(vecs)
    n_ro, n_bo = len(row_outs), len(bat_outs)

    def accumulate(ref, val, first):
        @pl.when(first)
        def _():
            ref[...] = val.reshape(ref.shape)

        @pl.when(jnp.logical_not(first))
        def _():
            ref[...] += val.reshape(ref.shape)

    def body(*refs):
        n_in = n_r + n_b + n_v + len(mms) + (after is not None)
        ins, outs = refs[:n_in], refs[n_in:]
        b_vals = [r[0] for r in ins[n_r:n_r + n_b]]
        v_vals = [r[...] for r in ins[n_r + n_b:n_r + n_b + n_v]]
        r_vals = [r[...] for r in ins[:n_r]]
        if mms:
            lefts = r_vals[:len(mms)] if lhs is None else [lhs(r_vals)] * len(mms)
            parts = [lax.dot_general(left.astype(BF16), b_ref[...].astype(BF16), _DIMS[mode],
                                     preferred_element_type=F32)
                     for left, b_ref, (_, mode) in zip(lefts, ins[n_r + n_b + n_v:], mms)]
            accs = [functools.reduce(lambda x, y: x + y, parts)] if mm_sum else parts
            r_vals = accs + r_vals[len(mms):] if lhs is None else accs + [lefts[0]] + r_vals
        ro, bo, vo = fn(r_vals, b_vals, v_vals)
        for ref, val in zip(outs[:n_ro], ro):
            ref[...] = val.astype(ref.dtype)
        b, i = pl.program_id(0), pl.program_id(1)
        for ref, val in zip(outs[n_ro:n_ro + n_bo], bo):
            accumulate(ref, val, i == 0)
        for ref, val in zip(outs[n_ro + n_bo:], vo):
            accumulate(ref, val, jnp.logical_and(i == 0, b == 0))

    in_specs = [pl.BlockSpec((ts, w), functools.partial(lambda b, i, cb: (b * nt + i, cb), cb=cb))
                for _, w, cb in rows]
    in_specs += [pl.BlockSpec((1, 1, v.shape[2]), lambda b, i: (b, 0, 0)) for v in bats]
    in_specs += [pl.BlockSpec((1, v.shape[1]), lambda b, i: (0, 0)) for v in vecs]
    extra = [b_arr for b_arr, _ in mms]
    in_specs += [pl.BlockSpec(b_arr.shape, lambda b, i: (0, 0), pipeline_mode=pl.Buffered(1)) for b_arr in extra]
    if after is not None:
        in_specs.append(_ANY)
        extra.append(after)
    out_shape = [jax.ShapeDtypeStruct((tokens, f), dt) for f, dt in row_outs]
    out_specs = [pl.BlockSpec((ts, f), lambda b, i: (b * nt + i, 0)) for f, _ in row_outs]
    out_shape += [jax.ShapeDtypeStruct((nseq, 1, f), F32) for f in bat_outs]
    out_specs += [pl.BlockSpec((1, 1, f), lambda b, i: (b, 0, 0)) for f in bat_outs]
    out_shape += [jax.ShapeDtypeStruct((1, f), F32) for f in vec_outs]
    out_specs += [pl.BlockSpec((1, f), lambda b, i: (0, 0)) for f in vec_outs]
    return pl.pallas_call(
        body, name=name, out_shape=tuple(out_shape), grid=(nseq, nt),
        in_specs=in_specs, out_specs=tuple(out_specs),
        compiler_params=_params("arbitrary", "arbitrary"),
    )(*([r[0] for r in rows] + list(bats) + list(vecs) + extra))


def _colsum(v):
    return jnp.sum(v, axis=0, keepdims=True)


def _rstd(x, width=None):
    width = width or x.shape[-1]
    return lax.rsqrt(jnp.sum(x * x, axis=-1, keepdims=True) * (1.0 / width) + NORM_EPS)


def _norm_bwd(dy, x, r, g, width=None):
    width = width or x.shape[-1]
    xhat = x * r
    dxhat = dy * g
    dx = r * (dxhat - xhat * (jnp.sum(dxhat * xhat, axis=-1, keepdims=True) * (1.0 / width)))
    return dx, dy * xhat


def _sigmoid(x):
    return 0.5 * jnp.tanh(0.5 * x) + 0.5


def _norm_mod(xv, g, sh, sc):
    return xv * _rstd(xv) * g * (1.0 + sc) + sh


def _norm_mod_fwd(x, p, seq, name):
    def fn(rows, bats, vecs):
        return [_norm_mod(rows[0], vecs[0], bats[0], bats[1])], [], []
    return _rowmap(name, fn, seq, [x], [p["shift"], p["scale"]], [p["gamma"]], row_outs=[(D_MODEL, BF16)])[0]


def _norm_mod_bwd(dh, x, dres, p, seq, name, prev=None, after=None):
    products = dh if isinstance(dh, list) else None
    lefts = [l for l, _ in products] if products else [dh]
    def fn(rows, bats, vecs):
        dhv, xv, dr = rows[:3]
        sc, g = bats[0], vecs[0]
        r = _rstd(xv)
        dxn, dg = _norm_bwd(dhv * (1.0 + sc), xv, r, g)
        dx = dr + dxn
        ro, bo = [dx], [_colsum(dhv), _colsum(dhv * (xv * r * g))]
        if prev is not None:
            ro.append(bats[1] * dx)
            bo.append(_colsum(dx * rows[3].astype(F32)))
        return ro, bo, [_colsum(dg)]
    more = prev is not None
    return _rowmap(name, fn, seq, lefts + [x, dres] + ([prev[0]] if more else []),
                   [p["scale"]] + ([prev[1]] if more else []), [p["gamma"]],
                   row_outs=[(D_MODEL, F32)] + ([(D_MODEL, BF16)] if more else []),
                   bat_outs=[D_MODEL] * (3 if more else 2), vec_outs=[D_MODEL],
                   mm=[(r, "nn") for _, r in products] if products else None, after=after)


def _ffn_in_act(h, wt_in, name):
    tokens = h.shape[0]
    tm, tn = _tile(tokens, (2048, 1024, 512)), 256
    nj = D_FF // tn

    def body(h_ref, wg_ref, wu_ref, g_ref, u_ref, a_ref):
        hv = h_ref[...]
        g = lax.dot_general(hv, wg_ref[...], _DIMS["nt"], preferred_element_type=F32)
        u = lax.dot_general(hv, wu_ref[...], _DIMS["nt"], preferred_element_type=F32)
        g_ref[...] = g.astype(BF16)
        u_ref[...] = u.astype(BF16)
        a_ref[...] = (g * _sigmoid(g) * u).astype(BF16)

    o_spec = pl.BlockSpec((tm, tn), lambda i, j: (i, j))
    out = jax.ShapeDtypeStruct((tokens, D_FF), BF16)
    return pl.pallas_call(
        body, name=name, grid=(tokens // tm, nj), out_shape=(out, out, out),
        in_specs=[pl.BlockSpec((tm, D_MODEL), lambda i, j: (i, 0)),
                  pl.BlockSpec((tn, D_MODEL), lambda i, j: (j, 0)),
                  pl.BlockSpec((tn, D_MODEL), lambda i, j: (j + nj, 0))],
        out_specs=(o_spec, o_spec, o_spec),
        compiler_params=_params("parallel", "parallel"),
    )(h, wt_in, wt_in)


def _out_residual(a, w_out, res, gate, nxt, seq, name, lhs=None):
    def fn(rows, bats, vecs):
        acc, rv = rows[0], rows[-1]
        x_new = rv + bats[0] * acc
        made = [] if lhs is None else [rows[1]]
        return [x_new, acc, _norm_mod(x_new, vecs[0], bats[1], bats[2])] + made, [], []
    outs = [(D_MODEL, F32), (D_MODEL, BF16), (D_MODEL, BF16)] + ([] if lhs is None else [(D_MODEL, BF16)])
    return _rowmap(name, fn, seq, (a if lhs is not None else [a]) + [res], [gate, nxt["shift"], nxt["scale"]],
                   [nxt["gamma"]], row_outs=outs, ts=_tile(seq, (512, 256, 128)), mm=(w_out, "nn"), lhs=lhs)


def _out_loss(a, w_out, res, gate, target, seq, name):
    def fn(rows, bats, vecs):
        acc, rv, tv = rows
        err = rv + bats[0] * acc - tv
        dy = err * (1.0 / D_MODEL)
        return [dy, bats[0] * dy], [_colsum(dy * acc)], [_colsum(err * err)]
    return _rowmap(name, fn, seq, [a, res, target], [gate], row_outs=[(D_MODEL, F32), (D_MODEL, BF16)],
                   bat_outs=[D_MODEL], vec_outs=[D_MODEL], ts=_tile(seq, (512, 256, 128)), mm=(w_out, "nn"))


def _ffn_bwd_x(df, dres, saved, p, seq, tag, prev=None, early=None, mid=None):
    x, h, g, u, a, w_in, w_out = saved
    first = None if early is None else early(a, df)

    def act_bwd(rows, bats, vecs):
        dav, gv, uv = rows[0], rows[1].astype(F32), rows[2].astype(F32)
        sg = _sigmoid(gv)
        silu = gv * sg
        dg = dav * uv * (sg * (1.0 + gv * (1.0 - sg)))
        return [jnp.concatenate([dg, dav * silu], axis=1)], [], []
    dgu = _rowmap(f"{tag}_bwd_da", act_bwd, seq, [df, g, u], row_outs=[(2 * D_FF, BF16)],
                  ts=_tile(seq, (512, 256, 128)), mm=(w_out, "nt"), after=first)[0]
    operands = (a, df, dgu, h)
    after = None if mid is None else mid(operands)
    return _norm_mod_bwd([(dgu, w_in)], x, dres, p, seq, f"{tag}_bwd_norm", prev, after=after), operands


def _ffn_bwd_wout(a, df, tag):
    return _mm(a, df, "tn", f"{tag}_bwd_wout", out_dtype=BF16, tm=256, tn=D_MODEL)


def _ffn_bwd_win(operands, tag, after=None, half=None):
    _, _, dgu, h = operands
    if half is None:
        return _mm(dgu, h, "tn", f"{tag}_bwd_win", out_dtype=BF16, tm=512, tn=D_MODEL, after=after)
    return _mm(dgu, h, "tn", f"{tag}_bwd_win{half}", out_dtype=BF16, tm=512, tn=D_MODEL // 2, after=after,
               b_cols=(half, 1))


def _shift_rows(v, k, forward):
    n = v.shape[0]
    row = lax.broadcasted_iota(jnp.int32, v.shape, 0)
    if forward:
        return jnp.where(row >= k, pltpu.roll(v, k, 0), 0.0)
    return jnp.where(row < n - k, pltpu.roll(v, n - k, 0), 0.0)


def _window_sums(v, forward):
    out, s, k = [], v, 1
    for _ in range(POOL_GROUPS):
        s = s + _shift_rows(s, k, forward)
        out.append(s)
        k *= 2
    return out


def _by_group(vals, g):
    out = vals[-1]
    for idx in range(len(vals) - 2, -1, -1):
        out = jnp.where(g == idx, vals[idx], out)
    return out


def _inv_count(shape, g):
    t1 = lax.broadcasted_iota(jnp.int32, shape, 0) + 1
    window = _by_group([jnp.int32(2 ** (i + 1)) for i in range(POOL_GROUPS)], g)
    return 1.0 / jnp.minimum(t1, window).astype(F32)


def _pool_fwd(u, grp, scale, seq):
    tokens = u.shape[0]

    def body(u_ref, grp_ref, sc_ref, pooled_ref, pg_ref, ps_ref):
        g = pl.program_id(1)
        uv = u_ref[...]
        sums = _by_group(_window_sums(uv, True), g)
        pooled = (sums * _inv_count(uv.shape, g) - uv).astype(BF16)
        pg = jnp.dot(pooled, grp_ref[0].astype(BF16), preferred_element_type=F32)
        pooled_ref[...] = pooled
        pg_ref[...] = pg
        ps_ref[...] = (pg * sc_ref[...]).astype(BF16)

    blk = pl.BlockSpec((seq, POOL_GROUP_DIM), lambda b, g: (b, g))
    return pl.pallas_call(
        body, name="pool_fwd", grid=(tokens // seq, POOL_GROUPS),
        out_shape=(jax.ShapeDtypeStruct(u.shape, BF16), jax.ShapeDtypeStruct(u.shape, F32),
                   jax.ShapeDtypeStruct(u.shape, BF16)),
        in_specs=[blk, pl.BlockSpec((1, POOL_GROUP_DIM, POOL_GROUP_DIM), lambda b, g: (g, 0, 0)),
                  pl.BlockSpec((1, POOL_GROUP_DIM), lambda b, g: (0, g))],
        out_specs=(blk, blk, blk),
        compiler_params=_params("parallel", "parallel"),
    )(u, grp, scale)


def _pool_bwd(dps, pooled, pg, grp, scale, seq):
    tokens = dps.shape[0]

    def body(dps_ref, pooled_ref, pg_ref, grp_ref, sc_ref, du_ref, dgrp_ref, dsc_ref):
        g, b = pl.program_id(0), pl.program_id(1)
        dpsv = dps_ref[...]
        dpg = (dpsv * sc_ref[...]).astype(BF16)
        dsc = _colsum(dpsv * pg_ref[...])
        dgrp = lax.dot_general(pooled_ref[...], dpg, _DIMS["tn"], preferred_element_type=F32)

        @pl.when(b == 0)
        def _():
            dsc_ref[...] = dsc
            dgrp_ref[0] = dgrp

        @pl.when(b > 0)
        def _():
            dsc_ref[...] += dsc
            dgrp_ref[0] += dgrp

        dpool = lax.dot_general(dpg, grp_ref[0].astype(BF16), _DIMS["nt"], preferred_element_type=F32)
        sums = _by_group(_window_sums(dpool * _inv_count(dpool.shape, g), False), g)
        du_ref[...] = (sums - dpool).astype(BF16)

    blk = pl.BlockSpec((seq, POOL_GROUP_DIM), lambda g, b: (b, g))
    grp_spec = pl.BlockSpec((1, POOL_GROUP_DIM, POOL_GROUP_DIM), lambda g, b: (g, 0, 0))
    vec_spec = pl.BlockSpec((1, POOL_GROUP_DIM), lambda g, b: (0, g))
    return pl.pallas_call(
        body, name="pool_bwd", grid=(POOL_GROUPS, tokens // seq),
        out_shape=(jax.ShapeDtypeStruct(dps.shape, BF16), jax.ShapeDtypeStruct(grp.shape, F32),
                   jax.ShapeDtypeStruct(scale.shape, F32)),
        in_specs=[blk, blk, blk, grp_spec, vec_spec],
        out_specs=(blk, grp_spec, vec_spec),
        compiler_params=_params("arbitrary", "arbitrary"),
    )(dps, pooled, pg, grp, scale)


def _lane(shape):
    return lax.broadcasted_iota(jnp.int32, shape, len(shape) - 1)


def _rot(y):
    lane = _lane(y.shape)
    r = jnp.where(lane < QK_NOPE + QK_ROPE // 2,
                  -pltpu.roll(y, HEAD_SLAB - QK_ROPE // 2, 1), pltpu.roll(y, QK_ROPE // 2, 1))
    return jnp.where(jnp.logical_and(lane >= QK_NOPE, lane < QK_NOPE + QK_ROPE), r, 0.0)


def _part_rstd(x):
    sq = x * x
    nope = _lane(x.shape) < QK_NOPE
    s_nope = jnp.sum(jnp.where(nope, sq, 0.0), axis=-1, keepdims=True)
    s_rope = jnp.sum(sq, axis=-1, keepdims=True) - s_nope
    return jnp.where(nope, lax.rsqrt(s_nope * (1.0 / QK_NOPE) + NORM_EPS),
                     lax.rsqrt(s_rope * (1.0 / QK_ROPE) + NORM_EPS))


def _part_norm_bwd(dy, x, r, g):
    nope = _lane(x.shape) < QK_NOPE
    xhat = x * r
    dxhat = dy * g
    prod = dxhat * xhat
    m_nope = jnp.sum(jnp.where(nope, prod, 0.0), axis=-1, keepdims=True)
    m_rope = jnp.sum(prod, axis=-1, keepdims=True) - m_nope
    mean = jnp.where(nope, m_nope * (1.0 / QK_NOPE), m_rope * (1.0 / QK_ROPE))
    return r * (dxhat - xhat * mean), dy * xhat


def _mixer_in(h, wt_a, wt_p, wt_g, g_q, g_kv, seq):
    def fn(rows, bats, vecs):
        z_a, z_p, z_g = rows[:3]
        q, kv = z_a[:, :Q_LORA], z_a[:, Q_LORA:Q_LORA + KV_LORA]
        return [z_a, z_p, z_g, q * _rstd(q) * vecs[0], kv * _rstd(kv) * vecs[1]], [], []
    return _rowmap("mix_in", fn, seq, [h], vecs=[g_q, g_kv], lhs=lambda rows: rows[0],
                   mm=[(wt_a, "nt"), (wt_p, "nt"), (wt_g, "nt")], mm_sum=False,
                   row_outs=[(wt_a.shape[0], F32), (wt_p.shape[0], F32), (wt_g.shape[0], F32),
                             (Q_LORA, BF16), (KV_LORA, BF16)])


def _latent_norm_bwd(dqp, wtq_pad, dkv, wt_kv, dkr, z_a, g_q, g_kv, seq):
    def fn(rows, bats, vecs):
        dq, dkv, dkrv, z = rows
        q, kv = z[:, :Q_LORA], z[:, Q_LORA:Q_LORA + KV_LORA]
        dxq, dgq = _norm_bwd(dq, q, _rstd(q), vecs[0])
        dxkv, dgkv = _norm_bwd(dkv, kv, _rstd(kv), vecs[1])
        return [jnp.concatenate([dxq, dxkv, dkrv], axis=1)], [], [_colsum(dgq), _colsum(dgkv)]
    return _rowmap("latent_norm_bwd", fn, seq, [dqp, dkv, dkr, z_a], vecs=[g_q, g_kv],
                   row_outs=[(Q_LORA + KV_LORA + HEAD_SLAB, BF16)], vec_outs=[Q_LORA, KV_LORA],
                   mm=[(wtq_pad, "nn"), (wt_kv, "nn")], mm_sum=False)


def _qk_prep_fwd(qn, wtq_pad, kvn, wt_kv, z_a, pos, g_q, g_kn, g_kr, inv_freq, seq):
    def fn(rows, bats, vecs):
        qv, kvv, kr, p = rows
        gq, gkn, gkr, invf = vecs
        ang = p * invf
        cos, sin = jnp.cos(ang), jnp.sin(ang)
        nope = _lane(kr.shape) < QK_NOPE
        krn = kr * _rstd(kr, QK_ROPE) * gkr
        krr = krn * cos + _rot(krn) * sin
        qs, ks, vs = [], [], []
        for h in range(N_HEADS):
            xq = qv[:, h * HEAD_SLAB:(h + 1) * HEAD_SLAB]
            y = xq * _part_rstd(xq) * gq
            qs.append(y * cos + _rot(y) * sin)
            xk = kvv[:, h * HEAD_SLAB:(h + 1) * HEAD_SLAB]
            kn = jnp.where(nope, xk, 0.0)
            ks.append(jnp.where(nope, kn * _rstd(kn, QK_NOPE) * gkn, krr))
            vs.append(jnp.where(nope, 0.0, xk))
        return [jnp.concatenate(v, axis=1) for v in (qs, ks, vs)] + [qv, kvv], [], []
    width = N_HEADS * HEAD_SLAB
    return _rowmap("qk_prep", fn, seq, [qn, kvn, (z_a, HEAD_SLAB, 5), pos], vecs=[g_q, g_kn, g_kr, inv_freq],
                   row_outs=[(width, BF16)] * 3 + [(width, F32)] * 2, mm=[(wtq_pad, "nt"), (wt_kv, "nt")],
                   mm_sum=False)


def _qk_prep_bwd(dqc, dkc, dvp, qp, kv, z_a, pos, g_q, g_kn, g_kr, inv_freq, seq):
    def fn(rows, bats, vecs):
        dq, dk, dv, qv, kvv, kr, p = rows
        gq, gkn, gkr, invf = vecs
        ang = p * invf
        cos, sin = jnp.cos(ang), jnp.sin(ang)
        nope = _lane(kr.shape) < QK_NOPE
        dqs, dkvs = [], []
        dgq = jnp.zeros((1, HEAD_SLAB), F32)
        dgkn = jnp.zeros((1, HEAD_SLAB), F32)
        dkrr = jnp.zeros(kr.shape, F32)
        for h in range(N_HEADS):
            sl = slice(h * HEAD_SLAB, (h + 1) * HEAD_SLAB)
            dyr = dq[:, sl]
            dy = dyr * cos - _rot(dyr * sin)
            xq = qv[:, sl]
            dx, dg = _part_norm_bwd(dy, xq, _part_rstd(xq), gq)
            dqs.append(dx)
            dgq = dgq + _colsum(dg)
            dkh = dk[:, sl]
            dkrr = dkrr + jnp.where(nope, 0.0, dkh)
            kn = jnp.where(nope, kvv[:, sl], 0.0)
            dxk, dgk = _norm_bwd(jnp.where(nope, dkh, 0.0), kn, _rstd(kn, QK_NOPE), gkn, QK_NOPE)
            dgkn = dgkn + _colsum(dgk)
            dkvs.append(jnp.where(nope, dxk, dv[:, sl]))
        dkrn = dkrr * cos - _rot(dkrr * sin)
        dkr, dgkr = _norm_bwd(dkrn, kr, _rstd(kr, QK_ROPE), gkr, QK_ROPE)
        return ([jnp.concatenate(dqs, axis=1), jnp.concatenate(dkvs, axis=1), dkr], [],
                [dgq, dgkn, _colsum(dgkr)])
    width = N_HEADS * HEAD_SLAB
    return _rowmap("qk_prep_bwd", fn, seq, [dqc, dkc, dvp, qp, kv, (z_a, HEAD_SLAB, 5), pos],
                   vecs=[g_q, g_kn, g_kr, inv_freq],
                   row_outs=[(width, BF16), (width, BF16), (HEAD_SLAB, F32)],
                   vec_outs=[HEAD_SLAB] * 3, ts=_tile(seq, (512, 256, 128, 64, 32, 16, 8)))


def _scores(q, k_ref, keys, tq):
    s = lax.dot_general(q, k_ref[0:keys, :], _DIMS["nt"], preferred_element_type=F32) * ATTN_SCALE
    row = lax.broadcasted_iota(jnp.int32, (tq, tq), 0)
    col = lax.broadcasted_iota(jnp.int32, (tq, tq), 1)
    diag = jnp.where(col <= row, s[:, keys - tq:], -1e30)
    return diag if keys == tq else jnp.concatenate([s[:, :keys - tq], diag], axis=1)


def _attn_fwd(qc, kc, vp, seq):
    tokens = qc.shape[0]
    tq = _tile(seq, (256, 128))
    nq = seq // tq

    def body(q_ref, k_ref, v_ref, o_ref, lse_ref):
        for i in range(nq):
            rows, keys = slice(i * tq, (i + 1) * tq), (i + 1) * tq
            s = _scores(q_ref[rows, :], k_ref, keys, tq)
            m = jnp.max(s, axis=-1, keepdims=True)
            p = jnp.exp(s - m)
            l = jnp.sum(p, axis=-1, keepdims=True)
            acc = jnp.dot(p.astype(BF16), v_ref[0:keys, :], preferred_element_type=F32)
            o_ref[rows, :] = (acc / l).astype(BF16)
            lse_ref[rows, :] = jnp.broadcast_to(m + jnp.log(l), (tq, HEAD_SLAB))

    spec = pl.BlockSpec((seq, HEAD_SLAB), lambda b, h: (b, h))
    return pl.pallas_call(
        body, name="attn_fwd", grid=(tokens // seq, N_HEADS),
        out_shape=(jax.ShapeDtypeStruct(qc.shape, BF16), jax.ShapeDtypeStruct(qc.shape, F32)),
        in_specs=[spec] * 3, out_specs=(spec, spec),
        compiler_params=_params("parallel", "parallel"),
    )(qc, kc, vp)


def _attn_bwd(qc, kc, vp, o, lse, do, seq):
    tokens = qc.shape[0]
    tq = _tile(seq, (256, 128))
    nq = seq // tq

    def body(q_ref, k_ref, v_ref, o_ref, lse_ref, do_ref, dq_ref, dk_ref, dv_ref):
        dk_ref[...] = jnp.zeros(dk_ref.shape, F32)
        dv_ref[...] = jnp.zeros(dv_ref.shape, F32)
        for i in range(nq):
            rows, keys = slice(i * tq, (i + 1) * tq), (i + 1) * tq
            q, dov = q_ref[rows, :], do_ref[rows, :]
            delta = jnp.sum(dov.astype(F32) * o_ref[rows, :].astype(F32), axis=-1, keepdims=True)
            s = _scores(q, k_ref, keys, tq)
            p = jnp.exp(s - jnp.tile(lse_ref[rows, :], (1, keys // HEAD_SLAB)))
            dp = lax.dot_general(dov, v_ref[0:keys, :], _DIMS["nt"], preferred_element_type=F32)
            ds = (p * (dp - delta) * ATTN_SCALE).astype(BF16)
            dq_ref[rows, :] = jnp.dot(ds, k_ref[0:keys, :], preferred_element_type=F32)
            dk_ref[0:keys, :] += lax.dot_general(ds, q, _DIMS["tn"], preferred_element_type=F32)
            dv_ref[0:keys, :] += lax.dot_general(p.astype(BF16), dov, _DIMS["tn"], preferred_element_type=F32)

    spec = pl.BlockSpec((seq, HEAD_SLAB), lambda b, h: (b, h))
    out = jax.ShapeDtypeStruct(qc.shape, F32)
    return pl.pallas_call(
        body, name="attn_bwd", grid=(tokens // seq, N_HEADS),
        out_shape=(out, out, out), in_specs=[spec] * 6, out_specs=(spec, spec, spec),
        compiler_params=_params("parallel", "parallel"),
    )(qc, kc, vp, o, lse, do)


def _adamw(w, g, m, v, name):
    rows, cols = w.shape
    whole = rows * cols * 4 <= ADAMW_WHOLE_BYTES
    tr = rows if whole else _tile(rows, (256, 128, 64, 32, 16, 8))
    c1 = 1.0 - ADAM_B1 ** ADAM_STEP
    c2 = 1.0 - ADAM_B2 ** ADAM_STEP

    def body(w_ref, g_ref, m_ref, v_ref, d_ref, nm_ref, nv_ref):
        gv = g_ref[...]
        nm = ADAM_B1 * m_ref[...] + (1.0 - ADAM_B1) * gv
        nv = ADAM_B2 * v_ref[...] + (1.0 - ADAM_B2) * (gv * gv)
        d_ref[...] = -ADAM_LR * ((nm / c1) / (jnp.sqrt(nv / c2) + ADAM_EPS) + ADAM_WD * w_ref[...])
        nm_ref[...] = nm
        nv_ref[...] = nv

    spec = pl.BlockSpec((tr, cols), lambda i: (i, 0))
    out = jax.ShapeDtypeStruct(w.shape, F32)
    return pl.pallas_call(
        body, name=name, grid=(rows // tr,), out_shape=(out, out, out),
        in_specs=[spec] * 4, out_specs=(spec, spec, spec),
        compiler_params=_params("parallel"),
    )(w, g, m, v)


def _adamw_landed(w, landed, m, v, name):
    rows, cols = w.shape
    tr = _tile(rows, (176, 128, 96, 64, 32, 16, 8))
    c1 = 1.0 - ADAM_B1 ** ADAM_STEP
    c2 = 1.0 - ADAM_B2 ** ADAM_STEP
    n_parts = len(landed)

    def body(*refs):
        w_ref, m_ref, v_ref = refs[:3]
        g_ref, d_ref, nm_ref, nv_ref = refs[3 + n_parts:]
        parts = []
        for x_ref in refs[3:3 + n_parts]:
            acc = x_ref[0].astype(F32)
            for d in range(1, N_DEV):
                acc = acc + x_ref[d].astype(F32)
            parts.append(acc)
        gv = parts[0] if n_parts == 1 else jnp.concatenate(parts, axis=1)
        nm = ADAM_B1 * m_ref[...] + (1.0 - ADAM_B1) * gv
        nv = ADAM_B2 * v_ref[...] + (1.0 - ADAM_B2) * (gv * gv)
        g_ref[...] = gv
        d_ref[...] = -ADAM_LR * ((nm / c1) / (jnp.sqrt(nv / c2) + ADAM_EPS) + ADAM_WD * w_ref[...])
        nm_ref[...] = nm
        nv_ref[...] = nv

    spec = pl.BlockSpec((tr, cols), lambda i: (i, 0))
    out = jax.ShapeDtypeStruct(w.shape, F32)
    return pl.pallas_call(
        body, name=name, grid=(rows // tr,), out_shape=(out, out, out, out),
        in_specs=[spec] * 3 + [pl.BlockSpec((N_DEV, tr, x.shape[2]), lambda i: (0, i, 0)) for x in landed],
        out_specs=(spec, spec, spec, spec),
        compiler_params=_params("parallel"),
    )(w, m, v, *landed)


def _mod_cols(c_all, w_ada, b_cols):
    def body(c_ref, w_ref, b_ref, act_ref, mod_ref):
        cv = c_ref[...]
        act = cv * _sigmoid(cv)
        act_ref[...] = act
        mod_ref[...] = jnp.dot(act.astype(BF16), w_ref[...].astype(BF16),
                               preferred_element_type=F32) + b_ref[...]

    n = w_ada.shape[1]
    return pl.pallas_call(
        body, name="mod_cols",
        out_shape=(jax.ShapeDtypeStruct(c_all.shape, F32), jax.ShapeDtypeStruct((c_all.shape[0], n), F32)),
        compiler_params=pltpu.CompilerParams(vmem_limit_bytes=VMEM_LIMIT),
    )(c_all, w_ada, b_cols)


def _ada_grads(c_act, dmod_all, dmod_cols):
    def body(c_ref, d_ref, dc_ref, gw_ref, gb_ref):
        gw_ref[...] = lax.dot_general(c_ref[...].astype(BF16), dc_ref[...].astype(BF16), _DIMS["tn"],
                                      preferred_element_type=F32)
        gb_ref[...] = _colsum(d_ref[...])

    return pl.pallas_call(
        body, name="ada_grads",
        out_shape=(jax.ShapeDtypeStruct((c_act.shape[1], dmod_cols.shape[1]), F32),
                   jax.ShapeDtypeStruct((1, dmod_all.shape[1]), F32)),
        compiler_params=pltpu.CompilerParams(vmem_limit_bytes=VMEM_LIMIT),
    )(c_act, dmod_all, dmod_cols)


def _flat_rows(a):
    flat = a.reshape(-1)
    pad = (-flat.shape[0]) % (LANES * SUBLANES)
    if pad:
        flat = jnp.pad(flat, (0, pad))
    return flat.reshape(-1, LANES)


def _gather_start(w, groups, tag, after=None, peers=None):
    shards = [[(w[n] if n in ROW_SHARDED else w[n].T).astype(BF16) for n in names] for names in groups]
    return _exchange_start_groups(shards, f"gather_{tag}_start", after=after, peers=peers)


def _gather_wait(handle, names, tag, after, peers=ALL_PEERS):
    landed = _exchange_wait(handle, f"gather_{tag}_wait", after=after, peers=peers)
    if peers == CHIP_PEERS:
        landed = [_sibling_forward(x, f"gather_{tag}_forward{i}") for i, x in enumerate(landed)]
    return {n: g.reshape(-1, g.shape[2]) for n, g in zip(names, landed)}


def _scatter_start(grads, names, tag, after=None):
    blocks = [grads[n].reshape(N_DEV, -1, grads[n].shape[1]) for n in names]
    return _exchange_start(blocks, f"scatter_{tag}_start", scatter=True, after=after)


def _scatter_wait(handle, names, tag, after):
    landed = _exchange_wait(handle, f"scatter_{tag}_wait", scatter=True, after=after)
    return {n: [x] for n, x in zip(names, landed)}


def _pack_small(vals):
    return jnp.concatenate([_flat_rows(v.astype(F32)) for v in vals], axis=0)


def _unpack_small(packed, like):
    out, row = [], 0
    for v in like:
        rows = _flat_rows(v).shape[0]
        out.append(packed[row:row + rows].reshape(-1)[:v.size].reshape(v.shape))
        row += rows
    return out


def _lanes128(*parts):
    out = jnp.zeros((HEAD_SLAB,), F32)
    for off, v in parts:
        out = lax.dynamic_update_slice(out, v.reshape(-1).astype(F32), (off,))
    return out.reshape(1, HEAD_SLAB)


def _step(x, c, positions, w, m, v, loss_target):
    nseq, seq, _ = x.shape
    tokens = nseq * seq
    me = _index(_my_pos())
    strip = lambda d: {n: (a[0] if a.ndim > 2 else a) for n, a in d.items()}
    shapes = {n: a.shape for n, a in w.items()}
    w, m, v = strip(w), strip(m), strip(v)

    c_all = _all_gather(c.reshape(-1, LANES), "gather_c").reshape(N_DEV * nseq, D_MODEL)
    n_ada = w["w_ada"].shape[1]
    b_cols = lax.dynamic_slice(w["b_ada"], (0, me * n_ada), (1, n_ada))
    c_act, mod_cols = _mod_cols(c_all, w["w_ada"], b_cols)
    mod_all = _all_gather(mod_cols, "gather_mod")
    mod = lax.dynamic_slice(mod_all, (0, me * nseq, 0), (N_DEV, nseq, n_ada))
    mod = mod.transpose(1, 0, 2).reshape(nseq, 3, 3, 1, D_MODEL)

    (h_f1i, h_f1o, h_mix_in, h_mix, h_f2), tok = _gather_start(
        w, (("w_ffn1_in",), ("w_ffn1_out",), MIXER[:1], MIXER[1:], ("w_ffn2_in", "w_ffn2_out")), "weights",
        after=mod_all, peers=[CHIP_PEERS] + [ALL_PEERS] * 4)
    started = tok[0:1, 0:1]

    g_q = _lanes128((0, w["q_norm_nope"]), (QK_NOPE, w["q_norm_rope"]))
    g_kn = _lanes128((0, w["k_norm_nope"]))
    g_kr = _lanes128((QK_NOPE, w["k_norm_rope"]))
    freq = ROPE_THETA ** (-jnp.arange(0, QK_ROPE, 2, dtype=F32) / QK_ROPE)
    inv_freq = _lanes128((QK_NOPE, jnp.concatenate([freq, freq])))
    pos = positions.reshape(tokens, 1).astype(F32)

    def sub(k, gamma, coef):
        return dict(gamma=w[gamma], shift=mod[:, k, 0] + started, scale=mod[:, k, 1], gate=coef * mod[:, k, 2])
    p1, pm, p2 = sub(0, "norm_ffn1", 0.5), sub(1, "norm_mix", 1.0), sub(2, "norm_ffn2", 0.5)
    t_big = _tile(tokens, (2048, 1024, 512))
    t_mid = _tile(tokens, (1024, 512))

    x0 = x.reshape(tokens, D_MODEL)
    h1 = _norm_mod_fwd(x0, p1, seq, "ffn1_norm")
    wt_f1i = _gather_wait(h_f1i, ("w_ffn1_in",), "ffn1_in", h1, peers=CHIP_PEERS)["w_ffn1_in"]
    g1, u1, a1 = _ffn_in_act(h1, wt_f1i, "ffn1_in")
    w_f1o = _gather_wait(h_f1o, ("w_ffn1_out",), "ffn1_out", a1)["w_ffn1_out"]
    x1, f1, h2 = _out_residual(a1, w_f1o, x0, p1["gate"], pm, seq, "ffn1_out")
    saved1 = (x0, h1, g1, u1, a1, wt_f1i, w_f1o)

    wt_in = _gather_wait(h_mix_in, MIXER[:1], "mix_in", h2)["w_in"]
    zero_rows = lambda rows: jnp.zeros((rows, D_MODEL), BF16)
    wt_p = wt_in[:512]
    wt_a = jnp.concatenate([wt_in[512:1152], zero_rows(QK_NOPE), wt_in[1152:1184], zero_rows(32)], axis=0)
    wt_g = wt_in[1184:]
    z_a, z_p, z_g, qn, kvn = _mixer_in(h2, wt_a, wt_p, wt_g, w["q_a_norm"], w["kv_a_norm"], seq)

    full = _gather_wait(h_mix, MIXER[1:], "mix", z_g)
    wtq_pad = jnp.pad(full["w_q_up"].reshape(N_HEADS, 96, Q_LORA), ((0, 0), (0, 32), (0, 0))).reshape(-1, Q_LORA)
    wtmla_pad = jnp.pad(full["w_mla_proj"].reshape(D_MODEL, N_HEADS, 64), ((0, 0), (0, 0), (64, 0))).reshape(D_MODEL, -1)
    wt_pool, wt_kv, w_mix_out = full["w_pool_proj"], full["w_kv_up"], full["w_out"]
    pooled, pg, ps = _pool_fwd(z_p, w["pool_grp"], w["pool_scale"], seq)
    br_pool = _mm(ps, wt_pool, "nt", "pool_proj", tm=t_big, tn=D_MODEL)
    qc, kc, vp, qp, kv = _qk_prep_fwd(qn, wtq_pad, kvn, wt_kv, z_a, pos, g_q, g_kn, g_kr, inv_freq, seq)
    attn, lse = _attn_fwd(qc, kc, vp, seq)
    br_mla = _mm(attn, wtmla_pad, "nt", "mla_proj", tm=t_mid, tn=D_MODEL)

    def merge(rows):
        zg, bp, bm = rows[:3]
        return (_sigmoid(zg[:, :D_MODEL]) * bp + _sigmoid(zg[:, D_MODEL:]) * bm).astype(BF16)
    x2, o_mix, h3, merged = _out_residual([z_g, br_pool, br_mla], w_mix_out, x1, pm["gate"], p2, seq, "mix_out",
                                          lhs=merge)

    ffn2_w = _gather_wait(h_f2, ("w_ffn2_in", "w_ffn2_out"), "ffn2", h3)
    g2, u2, a2 = _ffn_in_act(h3, ffn2_w["w_ffn2_in"], "ffn2_in")
    dy, df2, dgate2, sq_err = _out_loss(a2, ffn2_w["w_ffn2_out"], x2, p2["gate"],
                                        loss_target.reshape(tokens, D_MODEL), seq, "ffn2_out")
    saved2 = (x2, h3, g2, u2, a2, ffn2_w["w_ffn2_in"], ffn2_w["w_ffn2_out"])

    grads = {}
    (dx2, do_mix, dsh2, dsc2, dgate_m, dg_ffn2), ops2 = _ffn_bwd_x(df2, dy, saved2, p2, seq, "ffn2", (o_mix, pm["gate"]))
    grads["w_ffn2_out"], grads["w_ffn2_in"] = _ffn_bwd_wout(ops2[0], ops2[1], "ffn2"), _ffn_bwd_win(ops2, "ffn2")
    s_f2, tok = _scatter_start(grads, ("w_ffn2_in", "w_ffn2_out"), "ffn2")

    grads["w_out"] = _mm(merged, do_mix, "tn", "mix_bwd_wout", out_dtype=BF16, tm=512, tn=D_MODEL)

    def merge_bwd(rows, bats, vecs):
        dmv, zg, bp, bm = rows
        s_p, s_m = _sigmoid(zg[:, :D_MODEL]), _sigmoid(zg[:, D_MODEL:])
        dzg = jnp.concatenate([dmv * bp * s_p * (1.0 - s_p), dmv * bm * s_m * (1.0 - s_m)], axis=1)
        return [dmv * s_p, dmv * s_m, dzg], [], []
    dbr_pool, dbr_mla, dz_g = _rowmap("mix_bwd_dmerged", merge_bwd, seq, [do_mix, z_g, br_pool, br_mla],
                                      row_outs=[(D_MODEL, BF16), (D_MODEL, BF16), (2 * D_MODEL, BF16)],
                                      mm=(w_mix_out, "nt"))

    grads["w_pool_proj"] = _mm(dbr_pool, ps, "tn", "pool_bwd_wproj", out_dtype=BF16, tm=512, tn=POOL_WIDTH)
    dps = _mm(dbr_pool, wt_pool, "nn", "pool_bwd_dps", tm=t_big, tn=POOL_WIDTH)
    dz_p, dgrp, dpool_scale = _pool_bwd(dps, pooled, pg, w["pool_grp"], w["pool_scale"] + tok[0:1, 0:1], seq)

    dwtmla_pad = _mm(dbr_mla, attn, "tn", "mla_bwd_wproj", out_dtype=BF16, tm=512, tn=D_MODEL)
    grads["w_mla_proj"] = dwtmla_pad.reshape(D_MODEL, N_HEADS, HEAD_SLAB)[:, :, 64:].reshape(D_MODEL, -1)
    d_attn = _mm(dbr_mla, wtmla_pad, "nn", "mla_bwd_dattn", out_dtype=BF16, tm=t_mid, tn=D_MODEL)
    dqc, dkc, dvp = _attn_bwd(qc, kc, vp, attn, lse, d_attn, seq)
    dqp, dkv, dkr, dg_q, dg_kn, dg_kr = _qk_prep_bwd(dqc, dkc, dvp, qp, kv, z_a, pos, g_q, g_kn, g_kr, inv_freq, seq)
    dwtq_pad = _mm(dqp, qn, "tn", "q_up_bwd_w", out_dtype=BF16, tm=512, tn=Q_LORA)
    grads["w_q_up"] = dwtq_pad.reshape(N_HEADS, HEAD_SLAB, Q_LORA)[:, :96].reshape(-1, Q_LORA)
    grads["w_kv_up"] = _mm(dkv, kvn, "tn", "kv_up_bwd_w", out_dtype=BF16, tm=512, tn=KV_LORA)
    dz_a, dg_qa, dg_kva = _latent_norm_bwd(dqp, wtq_pad, dkv, wt_kv, dkr, z_a, w["q_a_norm"], w["kv_a_norm"], seq)

    dwt_a = _mm(dz_a, h2, "tn", "mix_in_bwd_wa", out_dtype=BF16, tm=256, tn=D_MODEL)
    dwt_p = _mm(dz_p, h2, "tn", "mix_in_bwd_wp", out_dtype=BF16, tm=512, tn=D_MODEL)
    dwt_g = _mm(dz_g, h2, "tn", "mix_in_bwd_wg", out_dtype=BF16, tm=512, tn=D_MODEL)
    grads["w_in"] = _concat_rows([(dwt_p, 0, 512), (dwt_a, 0, 640), (dwt_a, 704, 32), (dwt_g, 0, 2048)], "w_in_grad")

    small_early = [dg_ffn2.reshape(w["norm_ffn2"].shape), dgrp, dpool_scale, dg_qa, dg_kva, dg_q[:, :QK_NOPE],
                   dg_q[:, QK_NOPE:QK_NOPE + QK_ROPE], dg_kn[:, :QK_NOPE], dg_kr[:, QK_NOPE:QK_NOPE + QK_ROPE]]
    s_small, tok = _exchange_start([_pack_small(small_early)], "gather_small_start")
    s_mix, tok = _scatter_start(grads, MIXER, "mix", after=tok)
    dh2 = [(dz_a, wt_a), (dz_p, wt_p), (dz_g, wt_g)]
    pm_tied = dict(pm, scale=pm["scale"] + tok[0:1, 0:1])
    dx1, df1, dsh_m, dsc_m, dgate1, dg_mix = _norm_mod_bwd(dh2, x1, dx2, pm_tied, seq, "mix_bwd_norm", (f1, p1["gate"]))

    handles = {}

    def ffn1_early(a, df):
        grads["w_ffn1_out"] = _ffn_bwd_wout(a, df, "ffn1")
        handles["f1o"], token = _scatter_start(grads, ("w_ffn1_out",), "ffn1_out")
        return token

    def ffn1_mid(operands):
        first = _ffn_bwd_win(operands, "ffn1", half=0)
        handles["f1i0"], token = _exchange_start([first.reshape(N_DEV, -1, first.shape[1])],
                                                 "scatter_ffn1_in0_start", scatter=True)
        return token

    (dx0, dsh1, dsc1, dg_ffn1), ops1 = _ffn_bwd_x(df1, dx1, saved1, p1, seq, "ffn1", early=ffn1_early,
                                                     mid=ffn1_mid)
    s_f1o = handles["f1o"]

    dmod = jnp.stack([jnp.stack([dsh1, dsc1, 0.5 * dgate1], axis=1),
                      jnp.stack([dsh_m, dsc_m, dgate_m], axis=1),
                      jnp.stack([dsh2, dsc2, 0.5 * dgate2], axis=1)], axis=1)
    n_dmod = nseq * 9 * D_MODEL // LANES
    tail = _all_gather(jnp.concatenate([dmod.reshape(-1, LANES), _flat_rows(dg_ffn1), _flat_rows(dg_mix),
                                        _flat_rows(sq_err)], axis=0), "gather_dmod")
    dmod_all = tail[:, :n_dmod].reshape(N_DEV * nseq, 9 * D_MODEL)

    second = _ffn_bwd_win(ops1, "ffn1", after=tail, half=1)
    s_second, tok = _exchange_start([second.reshape(N_DEV, -1, second.shape[1])], "scatter_ffn1_in1_start",
                                    scatter=True, after=tail)
    s_f1i = (handles["f1i0"], s_second)

    dmod_cols = lax.dynamic_slice(dmod_all, (0, me * n_ada), (N_DEV * nseq, n_ada)) + tok[0:1, 0:1]
    g_w_ada, g_b_ada = _ada_grads(c_act, dmod_all, dmod_cols)
    tail_sum = _sum_blocks(tail[:, n_dmod:], "sum_tail")
    g_norm_ffn1 = tail_sum[:SUBLANES].reshape(1, D_MODEL)
    g_norm_mix = tail_sum[SUBLANES:2 * SUBLANES].reshape(1, D_MODEL)
    loss = 0.5 * jnp.sum(tail_sum[2 * SUBLANES:]) * (1.0 / D_MODEL)
    small_all = _exchange_wait(s_small, "gather_small_wait", after=g_b_ada)[0]
    small_sum = _sum_blocks(small_all, "sum_small")
    small = dict(zip(SMALL[2:], _unpack_small(small_sum, [w[n] for n in SMALL[2:]])))
    grad_w = dict(small, w_ada=g_w_ada, b_ada=g_b_ada, norm_ffn1=g_norm_ffn1, norm_mix=g_norm_mix)

    delta, new_m, new_v = {}, {}, {}

    def update(names, landed=None):
        for n in names:
            if landed is None:
                delta[n], new_m[n], new_v[n] = _adamw(w[n], grad_w[n], m[n], v[n], f"adamw_{n}")
            elif n in KEPT_TRANSPOSED:
                res = _adamw_landed(w[n].T, landed[n], m[n].T, v[n].T, f"adamw_{n}")
                grad_w[n], delta[n], new_m[n], new_v[n] = (r.T for r in res)
            elif n in ROW_SHARDED:
                grad_w[n], delta[n], new_m[n], new_v[n] = _adamw_landed(w[n], landed[n], m[n], v[n], f"adamw_{n}")
            else:
                grad_w[n] = _sum_blocks(landed[n][0], f"sum_{n}").T
                delta[n], new_m[n], new_v[n] = _adamw(w[n], grad_w[n], m[n], v[n], f"adamw_{n}")

    update(("w_ada",))
    rep = ("b_ada",) + SMALL
    d_s, m_s, v_s = _adamw(_pack_small([w[n] for n in rep]), _pack_small([grad_w[n] for n in rep]),
                           _pack_small([m[n] for n in rep]), _pack_small([v[n] for n in rep]), "adamw_small")
    like = [w[n] for n in rep]
    for dst, packed in ((delta, d_s), (new_m, m_s), (new_v, v_s)):
        dst.update(zip(rep, _unpack_small(packed, like)))
    update(("w_ffn2_in", "w_ffn2_out"), _scatter_wait(s_f2, ("w_ffn2_in", "w_ffn2_out"), "ffn2", after=d_s))
    update(MIXER, _scatter_wait(s_mix, MIXER, "mix", after=delta["w_ffn2_out"]))
    update(("w_ffn1_out",), _scatter_wait(s_f1o, ("w_ffn1_out",), "ffn1_out", after=delta["w_out"]))
    halves = [_exchange_wait(h, f"scatter_ffn1_in{i}_wait", scatter=True, after=delta["w_ffn1_out"])[0]
              for i, h in enumerate(s_f1i)]
    update(("w_ffn1_in",), {"w_ffn1_in": halves})

    lead = lambda d: [d[n].reshape(shapes[n]) for n in WEIGHTS]
    return (loss, dx0.reshape(x.shape), *lead(grad_w), *lead(delta), *lead(new_m), *lead(new_v))


def kernel(x, c, positions, w_ada, b_ada, norm_ffn1, w_ffn1_in, w_ffn1_out, norm_mix, w_in, pool_grp, pool_scale, w_pool_proj, q_a_norm, w_q_up, kv_a_norm, w_kv_up, q_norm_nope, q_norm_rope, k_norm_nope, k_norm_rope, w_mla_proj, w_out, norm_ffn2, w_ffn2_in, w_ffn2_out, loss_target, m_w_ada, m_b_ada, m_norm_ffn1, m_w_ffn1_in, m_w_ffn1_out, m_norm_mix, m_w_in, m_pool_grp, m_pool_scale, m_w_pool_proj, m_q_a_norm, m_w_q_up, m_kv_a_norm, m_w_kv_up, m_q_norm_nope, m_q_norm_rope, m_k_norm_nope, m_k_norm_rope, m_w_mla_proj, m_w_out, m_norm_ffn2, m_w_ffn2_in, m_w_ffn2_out, v_w_ada, v_b_ada, v_norm_ffn1, v_w_ffn1_in, v_w_ffn1_out, v_norm_mix, v_w_in, v_pool_grp, v_pool_scale, v_w_pool_proj, v_q_a_norm, v_w_q_up, v_kv_a_norm, v_w_kv_up, v_q_norm_nope, v_q_norm_rope, v_k_norm_nope, v_k_norm_rope, v_w_mla_proj, v_w_out, v_norm_ffn2, v_w_ffn2_in, v_w_ffn2_out):
    w = dict(w_ada=w_ada, b_ada=b_ada, norm_ffn1=norm_ffn1, w_ffn1_in=w_ffn1_in, w_ffn1_out=w_ffn1_out, norm_mix=norm_mix, w_in=w_in, pool_grp=pool_grp, pool_scale=pool_scale, w_pool_proj=w_pool_proj, q_a_norm=q_a_norm, w_q_up=w_q_up, kv_a_norm=kv_a_norm, w_kv_up=w_kv_up, q_norm_nope=q_norm_nope, q_norm_rope=q_norm_rope, k_norm_nope=k_norm_nope, k_norm_rope=k_norm_rope, w_mla_proj=w_mla_proj, w_out=w_out, norm_ffn2=norm_ffn2, w_ffn2_in=w_ffn2_in, w_ffn2_out=w_ffn2_out)
    m = dict(w_ada=m_w_ada, b_ada=m_b_ada, norm_ffn1=m_norm_ffn1, w_ffn1_in=m_w_ffn1_in, w_ffn1_out=m_w_ffn1_out, norm_mix=m_norm_mix, w_in=m_w_in, pool_grp=m_pool_grp, pool_scale=m_pool_scale, w_pool_proj=m_w_pool_proj, q_a_norm=m_q_a_norm, w_q_up=m_w_q_up, kv_a_norm=m_kv_a_norm, w_kv_up=m_w_kv_up, q_norm_nope=m_q_norm_nope, q_norm_rope=m_q_norm_rope, k_norm_nope=m_k_norm_nope, k_norm_rope=m_k_norm_rope, w_mla_proj=m_w_mla_proj, w_out=m_w_out, norm_ffn2=m_norm_ffn2, w_ffn2_in=m_w_ffn2_in, w_ffn2_out=m_w_ffn2_out)
    v = dict(w_ada=v_w_ada, b_ada=v_b_ada, norm_ffn1=v_norm_ffn1, w_ffn1_in=v_w_ffn1_in, w_ffn1_out=v_w_ffn1_out, norm_mix=v_norm_mix, w_in=v_w_in, pool_grp=v_pool_grp, pool_scale=v_pool_scale, w_pool_proj=v_w_pool_proj, q_a_norm=v_q_a_norm, w_q_up=v_w_q_up, kv_a_norm=v_kv_a_norm, w_kv_up=v_w_kv_up, q_norm_nope=v_q_norm_nope, q_norm_rope=v_q_norm_rope, k_norm_nope=v_k_norm_nope, k_norm_rope=v_k_norm_rope, w_mla_proj=v_w_mla_proj, w_out=v_w_out, norm_ffn2=v_norm_ffn2, w_ffn2_in=v_w_ffn2_in, w_ffn2_out=v_w_ffn2_out)
    return _step(x, c, positions, w, m, v, loss_target)
```

```python
import functools
import math

import jax
import jax.numpy as jnp
from jax import lax
from jax.experimental import pallas as pl
from jax.experimental.pallas import tpu as pltpu

F32 = jnp.float32
BF16 = jnp.bfloat16
MESH = pl.DeviceIdType.MESH
AXES = ("x", "y", "c")
N_DEV = 8

D_MODEL = 1024
D_FF = 2816
N_HEADS = 8
HEAD_SLAB = 128
QK_NOPE = 64
QK_ROPE = 32
POOL_WIDTH = 512
POOL_GROUPS = 4
POOL_GROUP_DIM = 128
Q_LORA = 384
KV_LORA = 256
ROPE_THETA = 10000.0
ATTN_SCALE = 1.0 / math.sqrt(QK_NOPE + QK_ROPE)
NORM_EPS = 1e-6
ADAM_LR, ADAM_B1, ADAM_B2, ADAM_EPS, ADAM_WD, ADAM_STEP = 0.001, 0.9, 0.999, 1e-08, 0.01, 10

LANES = 128
SUBLANES = 8
VMEM_LIMIT = 52 * 1024 * 1024
ADAMW_WHOLE_BYTES = 3 << 19
SUM_WHOLE_BYTES = 4 << 20

BIG = ("w_ffn1_in", "w_ffn1_out", "w_in", "w_pool_proj", "w_q_up", "w_kv_up",
       "w_mla_proj", "w_out", "w_ffn2_in", "w_ffn2_out")
ROW_SHARDED = ("w_ffn1_out", "w_out", "w_ffn2_out")
MIXER = ("w_in", "w_pool_proj", "w_q_up", "w_kv_up", "w_mla_proj", "w_out")
KEPT_TRANSPOSED = ("w_ffn1_in", "w_ffn2_in", "w_in", "w_q_up")
SMALL = ("norm_ffn1", "norm_mix", "norm_ffn2", "pool_grp", "pool_scale", "q_a_norm",
         "kv_a_norm", "q_norm_nope", "q_norm_rope", "k_norm_nope", "k_norm_rope")
WEIGHTS = ("w_ada", "b_ada", "norm_ffn1", "w_ffn1_in", "w_ffn1_out", "norm_mix", "w_in",
           "pool_grp", "pool_scale", "w_pool_proj", "q_a_norm", "w_q_up", "kv_a_norm",
           "w_kv_up", "q_norm_nope", "q_norm_rope", "k_norm_nope", "k_norm_rope",
           "w_mla_proj", "w_out", "norm_ffn2", "w_ffn2_in", "w_ffn2_out")


def _params(*sem):
    return pltpu.CompilerParams(dimension_semantics=sem, vmem_limit_bytes=VMEM_LIMIT)


def _tile(n, cands):
    for c in cands:
        if n % c == 0:
            return c
    return n


def _my_pos():
    return lax.axis_index("x"), lax.axis_index("y"), lax.axis_index("c")


def _flip(pos, k):
    x, y, c = pos
    fx, fy, fc = (k >> 2) & 1, (k >> 1) & 1, k & 1
    return ((1 - x) if fx else x, (1 - y) if fy else y, (1 - c) if fc else c)


def _index(pos):
    x, y, c = pos
    return 4 * x + 2 * y + c


def _exchange(arrays, name, scatter=False):
    n = len(arrays)

    def body(*refs):
        ins, outs = refs[:n], refs[n:2 * n]
        send_sems, recv_sems, local_sems = refs[2 * n:]
        me = _my_pos()
        mine, sends = [], []
        for a in range(n):
            own = ins[a].at[_index(me)] if scatter else ins[a]
            cp = pltpu.make_async_copy(own, outs[a].at[_index(me)], local_sems.at[a])
            cp.start()
            mine.append(cp)
        for k in range(1, N_DEV):
            peer = _flip(me, k)
            for a in range(n):
                cp = pltpu.make_async_remote_copy(
                    src_ref=ins[a].at[_index(peer)] if scatter else ins[a],
                    dst_ref=outs[a].at[_index(me)],
                    send_sem=send_sems.at[a, k - 1], recv_sem=recv_sems.at[a, k - 1],
                    device_id=peer, device_id_type=MESH)
                cp.start()
                sends.append(cp)
        for k in range(1, N_DEV):
            peer = _flip(me, k)
            for a in range(n):
                pltpu.make_async_remote_copy(
                    src_ref=ins[a].at[_index(me)] if scatter else ins[a],
                    dst_ref=outs[a].at[_index(peer)],
                    send_sem=send_sems.at[a, k - 1], recv_sem=recv_sems.at[a, k - 1],
                    device_id=peer, device_id_type=MESH).wait_recv()
        for cp in sends:
            cp.wait_send()
        for cp in mine:
            cp.wait()

    shape = lambda x: x.shape if scatter else (N_DEV,) + x.shape
    return pl.pallas_call(
        body, name=name,
        out_shape=tuple(jax.ShapeDtypeStruct(shape(x), x.dtype) for x in arrays),
        in_specs=[pl.BlockSpec(memory_space=pl.ANY)] * n,
        out_specs=tuple(pl.BlockSpec(memory_space=pl.ANY) for _ in arrays),
        scratch_shapes=[pltpu.SemaphoreType.DMA((n, N_DEV - 1)),
                        pltpu.SemaphoreType.DMA((n, N_DEV - 1)),
                        pltpu.SemaphoreType.DMA((n,))],
    )(*arrays)


def _all_gather(x, name):
    return _exchange([x], name)[0]


_HBM = pl.BlockSpec(memory_space=pltpu.HBM)
_SEM = pl.BlockSpec(memory_space=pltpu.SEMAPHORE)
_ANY = pl.BlockSpec(memory_space=pl.ANY)
_EFFECT = pltpu.SideEffectType.DATAFLOW_SIDE_EFFECTING


def _split_copy(ins, lands, send_sems, recv_sems, a, k, me, scatter, incoming):
    peer = _flip(me, k)
    block = me if incoming else peer
    return pltpu.make_async_remote_copy(
        src_ref=ins[a].at[_index(block)] if scatter else ins[a],
        dst_ref=lands[a].at[_index(peer if incoming else me)],
        send_sem=send_sems.at[a * (N_DEV - 1) + k - 1], recv_sem=recv_sems.at[a * (N_DEV - 1) + k - 1],
        device_id=peer, device_id_type=MESH)


ALL_PEERS = tuple(range(1, N_DEV))
CHIP_PEERS = (1, 2, 4, 6)


def _exchange_start_groups(groups, name, scatter=False, after=None, peers=None):
    peers = peers or [ALL_PEERS] * len(groups)
    sizes = [len(g) for g in groups]
    first = [sum(sizes[:i]) for i in range(len(sizes))]
    n, ng = sum(sizes), len(sizes)
    after = jnp.zeros((SUBLANES, LANES), F32) if after is None else after

    def body(*refs):
        ins, lands = refs[:n], refs[n:2 * n]
        sems = refs[2 * n + 1:2 * n + 1 + 2 * ng]
        me = _my_pos()
        for g in range(ng):
            part = slice(first[g], first[g] + sizes[g])
            for k in peers[g]:
                for a in range(sizes[g]):
                    _split_copy(ins[part], lands[part], sems[2 * g], sems[2 * g + 1], a, k, me, scatter, False).start()
        refs[-1][...] = jnp.zeros((SUBLANES, LANES), F32)

    shape = lambda x: x.shape if scatter else (N_DEV,) + x.shape
    hbm = lambda x: pltpu.with_memory_space_constraint(x, pltpu.HBM)
    srcs = [hbm(x) for g in groups for x in g]
    zones = [hbm(lax.empty(shape(x), x.dtype)) for g in groups for x in g]
    sem_shapes = [pltpu.SemaphoreType.DMA((s * (N_DEV - 1),)) for s in sizes for _ in range(2)]
    out = pl.pallas_call(
        body, name=name,
        out_shape=(*sem_shapes, *[pltpu.HBM(x.shape, x.dtype) for x in srcs + zones],
                   jax.ShapeDtypeStruct((SUBLANES, LANES), F32)),
        in_specs=[_HBM] * (2 * n) + [_ANY],
        out_specs=(*[_SEM] * (2 * ng), *[_HBM] * (2 * n), pl.BlockSpec(memory_space=pltpu.VMEM)),
        input_output_aliases={i: 2 * ng + i for i in range(2 * n)},
        compiler_params=pltpu.CompilerParams(has_side_effects=_EFFECT),
    )(*srcs, *zones, after)
    bufs = out[2 * ng:-1]
    handles = [(out[2 * g], out[2 * g + 1], *bufs[first[g]:first[g] + sizes[g]],
                *bufs[n + first[g]:n + first[g] + sizes[g]]) for g in range(ng)]
    return handles, out[-1]


def _exchange_start(arrays, name, scatter=False, after=None):
    handles, token = _exchange_start_groups([arrays], name, scatter, after)
    return handles[0], token


def _exchange_wait(handle, name, scatter=False, after=None, peers=ALL_PEERS):
    send_sems, recv_sems = handle[0], handle[1]
    n = (len(handle) - 2) // 2
    after = jnp.zeros((SUBLANES, LANES), F32) if after is None else after

    def body(*refs):
        ins, lands = refs[:n], refs[n:2 * n]
        send, recv = refs[2 * n], refs[2 * n + 1]
        me = _my_pos()
        for k in peers:
            for a in range(n):
                _split_copy(ins, lands, send, recv, a, k, me, scatter, False).wait_send()
                _split_copy(ins, lands, send, recv, a, k, me, scatter, True).wait_recv()

    bufs = handle[2:]
    out = pl.pallas_call(
        body, name=name,
        out_shape=tuple(pltpu.HBM(x.shape, x.dtype) for x in bufs),
        in_specs=[_HBM] * (2 * n) + [_SEM, _SEM, _ANY],
        out_specs=tuple([_HBM] * (2 * n)),
        input_output_aliases={i: i for i in range(2 * n)},
        compiler_params=pltpu.CompilerParams(has_side_effects=_EFFECT),
    )(*bufs, send_sems, recv_sems, after)
    me = _index(_my_pos())
    landed = []
    for src, land in zip(out[:n], out[n:]):
        own = lax.dynamic_slice_in_dim(src, me, 1, axis=0) if scatter else src[None]
        landed.append(lax.dynamic_update_slice_in_dim(land, own, me, axis=0))
    return landed


def _sibling_forward(x, name):
    flips = [k for k in CHIP_PEERS if k != 1]

    def body(x_ref, o_ref, send_sems, recv_sems):
        me = _my_pos()
        sibling = _flip(me, 1)
        sends = []
        for i, k in enumerate(flips):
            block = o_ref.at[_index(_flip(me, k))]
            cp = pltpu.make_async_remote_copy(src_ref=block, dst_ref=block, send_sem=send_sems.at[i],
                                              recv_sem=recv_sems.at[i], device_id=sibling, device_id_type=MESH)
            cp.start()
            sends.append(cp)
        for i, k in enumerate(flips):
            block = o_ref.at[_index(_flip(sibling, k))]
            pltpu.make_async_remote_copy(src_ref=block, dst_ref=block, send_sem=send_sems.at[i],
                                         recv_sem=recv_sems.at[i], device_id=sibling, device_id_type=MESH).wait_recv()
        for cp in sends:
            cp.wait_send()

    return pl.pallas_call(
        body, name=name, out_shape=jax.ShapeDtypeStruct(x.shape, x.dtype),
        in_specs=[_ANY], out_specs=_ANY, input_output_aliases={0: 0},
        scratch_shapes=[pltpu.SemaphoreType.DMA((len(flips),)), pltpu.SemaphoreType.DMA((len(flips),))],
    )(x)


def _sum_blocks(x, name):
    n, rows, cols = x.shape
    whole = x.size * x.dtype.itemsize <= SUM_WHOLE_BYTES
    tr = rows if whole else _tile(rows, (512, 256, 128, 64, 32, 16, 8))

    def body(x_ref, o_ref):
        acc = x_ref[0].astype(F32)
        for d in range(1, n):
            acc = acc + x_ref[d].astype(F32)
        o_ref[...] = acc

    return pl.pallas_call(
        body, name=name,
        out_shape=jax.ShapeDtypeStruct((rows, cols), F32),
        grid=(rows // tr,),
        in_specs=[pl.BlockSpec((n, tr, cols), lambda i: (0, i, 0))],
        out_specs=pl.BlockSpec((tr, cols), lambda i: (i, 0)),
        compiler_params=_params("parallel"),
    )(x)


_DIMS = {"nn": (((1,), (0,)), ((), ())), "nt": (((1,), (1,)), ((), ())), "tn": (((0,), (0,)), ((), ()))}


def _mm(a, b, mode, name, out_dtype=F32, tm=None, tn=None, add=None, after=None, b_cols=None):
    if mode == "tn":
        kdim, m = a.shape
    else:
        m, kdim = a.shape
    n = b.shape[0] if mode == "nt" else b.shape[1]
    tm = tm or _tile(m, (512, 256, 128))
    tn = tn or _tile(n, (512, 256, 128))
    j0 = 0
    if b_cols is not None:
        j0, n = b_cols[0], b_cols[1] * tn
    dims = _DIMS[mode]

    def body(*refs):
        refs = refs if after is None else refs[1:]
        acc = lax.dot_general(refs[0][...].astype(BF16), refs[1][...].astype(BF16), dims,
                              preferred_element_type=F32)
        if add is not None:
            acc = acc + refs[2][...]
        refs[-1][...] = acc.astype(out_dtype)

    a_spec = (pl.BlockSpec((kdim, tm), lambda i, j: (0, i)) if mode == "tn"
              else pl.BlockSpec((tm, kdim), lambda i, j: (i, 0)))
    b_spec = (pl.BlockSpec((tn, kdim), lambda i, j: (j, 0)) if mode == "nt"
              else pl.BlockSpec((kdim, tn), lambda i, j: (0, j + j0)))
    o_spec = pl.BlockSpec((tm, tn), lambda i, j: (i, j))
    in_specs, args = [a_spec, b_spec], [a, b]
    if add is not None:
        in_specs.append(o_spec)
        args.append(add)
    if after is not None:
        in_specs.insert(0, _ANY)
        args.insert(0, after)
    return pl.pallas_call(
        body, name=name, out_shape=jax.ShapeDtypeStruct((m, n), out_dtype), grid=(m // tm, n // tn),
        in_specs=in_specs, out_specs=o_spec,
        compiler_params=_params("parallel", "parallel"),
    )(*args)


def _rowmap(name, fn, seq, rows, bats=(), vecs=(), row_outs=(), bat_outs=(), vec_outs=(), ts=None, mm=None, lhs=None,
            after=None, mm_sum=True):
    mms = [] if mm is None else (mm if isinstance(mm, list) else [mm])
    rows = [r if isinstance(r, tuple) else (r, r.shape[1], 0) for r in rows]
    tokens = rows[0][0].shape[0]
    nseq = tokens // seq
    ts = ts or _tile(seq, (512, 256, 128, 64, 32, 16, 8))
    nt = seq // ts
    n_r, n_b, n_v = len(rows), len(bats), len(vecs)
    n_ro, n_bo = len(row_outs), len(bat_outs)

    def accumulate(ref, val, first):
        @pl.when(first)
        def _():
            ref[...] = val.reshape(ref.shape)

        @pl.when(jnp.logical_not(first))
        def _():
            ref[...] += val.reshape(ref.shape)

    def body(*refs):
        n_in = n_r + n_b + n_v + len(mms) + (after is not None)
        ins, outs = refs[:n_in], refs[n_in:]
        b_vals = [r[0] for r in ins[n_r:n_r + n_b]]
        v_vals = [r[...] for r in ins[n_r + n_b:n_r + n_b + n_v]]
        r_vals = [r[...] for r in ins[:n_r]]
        if mms:
            lefts = r_vals[:len(mms)] if lhs is None else [lhs(r_vals)] * len(mms)
            parts = [lax.dot_general(left.astype(BF16), b_ref[...].astype(BF16), _DIMS[mode],
                                     preferred_element_type=F32)
                     for left, b_ref, (_, mode) in zip(lefts, ins[n_r + n_b + n_v:], mms)]
            accs = [functools.reduce(lambda x, y: x + y, parts)] if mm_sum else parts
            r_vals = accs + r_vals[len(mms):] if lhs is None else accs + [lefts[0]] + r_vals
        ro, bo, vo = fn(r_vals, b_vals, v_vals)
        for ref, val in zip(outs[:n_ro], ro):
            ref[...] = val.astype(ref.dtype)
        b, i = pl.program_id(0), pl.program_id(1)
        for ref, val in zip(outs[n_ro:n_ro + n_bo], bo):
            accumulate(ref, val, i == 0)
        for ref, val in zip(outs[n_ro + n_bo:], vo):
            accumulate(ref, val, jnp.logical_and(i == 0, b == 0))

    in_specs = [pl.BlockSpec((ts, w), functools.partial(lambda b, i, cb: (b * nt + i, cb), cb=cb))
                for _, w, cb in rows]
    in_specs += [pl.BlockSpec((1, 1, v.shape[2]), lambda b, i: (b, 0, 0)) for v in bats]
    in_specs += [pl.BlockSpec((1, v.shape[1]), lambda b, i: (0, 0)) for v in vecs]
    extra = [b_arr for b_arr, _ in mms]
    in_specs += [pl.BlockSpec(b_arr.shape, lambda b, i: (0, 0), pipeline_mode=pl.Buffered(1)) for b_arr in extra]
    if after is not None:
        in_specs.append(_ANY)
        extra.append(after)
    out_shape = [jax.ShapeDtypeStruct((tokens, f), dt) for f, dt in row_outs]
    out_specs = [pl.BlockSpec((ts, f), lambda b, i: (b * nt + i, 0)) for f, _ in row_outs]
    out_shape += [jax.ShapeDtypeStruct((nseq, 1, f), F32) for f in bat_outs]
    out_specs += [pl.BlockSpec((1, 1, f), lambda b, i: (b, 0, 0)) for f in bat_outs]
    out_shape += [jax.ShapeDtypeStruct((1, f), F32) for f in vec_outs]
    out_specs += [pl.BlockSpec((1, f), lambda b, i: (0, 0)) for f in vec_outs]
    return pl.pallas_call(
        body, name=name, out_shape=tuple(out_shape), grid=(nseq, nt),
        in_specs=in_specs, out_specs=tuple(out_specs),
        compiler_params=_params("arbitrary", "arbitrary"),
    )(*([r[0] for r in rows] + list(bats) + list(vecs) + extra))


def _colsum(v):
    return jnp.sum(v, axis=0, keepdims=True)


def _rstd(x, width=None):
    width = width or x.shape[-1]
    return lax.rsqrt(jnp.sum(x * x, axis=-1, keepdims=True) * (1.0 / width) + NORM_EPS)


def _norm_bwd(dy, x, r, g, width=None):
    width = width or x.shape[-1]
    xhat = x * r
    dxhat = dy * g
    dx = r * (dxhat - xhat * (jnp.sum(dxhat * xhat, axis=-1, keepdims=True) * (1.0 / width)))
    return dx, dy * xhat


def _sigmoid(x):
    return 0.5 * jnp.tanh(0.5 * x) + 0.5


def _norm_mod(xv, g, sh, sc):
    return xv * _rstd(xv) * g * (1.0 + sc) + sh


def _norm_mod_fwd(x, p, seq, name):
    def fn(rows, bats, vecs):
        return [_norm_mod(rows[0], vecs[0], bats[0], bats[1])], [], []
    return _rowmap(name, fn, seq, [x], [p["shift"], p["scale"]], [p["gamma"]], row_outs=[(D_MODEL, BF16)])[0]


def _norm_mod_bwd(dh, x, dres, p, seq, name, prev=None, after=None):
    products = dh if isinstance(dh, list) else None
    lefts = [l for l, _ in products] if products else [dh]
    def fn(rows, bats, vecs):
        dhv, xv, dr = rows[:3]
        sc, g = bats[0], vecs[0]
        r = _rstd(xv)
        dxn, dg = _norm_bwd(dhv * (1.0 + sc), xv, r, g)
        dx = dr + dxn
        ro, bo = [dx], [_colsum(dhv), _colsum(dhv * (xv * r * g))]
        if prev is not None:
            ro.append(bats[1] * dx)
            bo.append(_colsum(dx * rows[3].astype(F32)))
        return ro, bo, [_colsum(dg)]
    more = prev is not None
    return _rowmap(name, fn, seq, lefts + [x, dres] + ([prev[0]] if more else []),
                   [p["scale"]] + ([prev[1]] if more else []), [p["gamma"]],
                   row_outs=[(D_MODEL, F32)] + ([(D_MODEL, BF16)] if more else []),
                   bat_outs=[D_MODEL] * (3 if more else 2), vec_outs=[D_MODEL],
                   mm=[(r, "nn") for _, r in products] if products else None, after=after)


def _ffn_in_act(h, wt_in, name):
    tokens = h.shape[0]
    tm, tn = _tile(tokens, (2048, 1024, 512)), 256
    nj = D_FF // tn

    def body(h_ref, wg_ref, wu_ref, g_ref, u_ref, a_ref):
        hv = h_ref[...]
        g = lax.dot_general(hv, wg_ref[...], _DIMS["nt"], preferred_element_type=F32)
        u = lax.dot_general(hv, wu_ref[...], _DIMS["nt"], preferred_element_type=F32)
        g_ref[...] = g.astype(BF16)
        u_ref[...] = u.astype(BF16)
        a_ref[...] = (g * _sigmoid(g) * u).astype(BF16)

    o_spec = pl.BlockSpec((tm, tn), lambda i, j: (i, j))
    out = jax.ShapeDtypeStruct((tokens, D_FF), BF16)
    return pl.pallas_call(
        body, name=name, grid=(tokens // tm, nj), out_shape=(out, out, out),
        in_specs=[pl.BlockSpec((tm, D_MODEL), lambda i, j: (i, 0)),
                  pl.BlockSpec((tn, D_MODEL), lambda i, j: (j, 0)),
                  pl.BlockSpec((tn, D_MODEL), lambda i, j: (j + nj, 0))],
        out_specs=(o_spec, o_spec, o_spec),
        compiler_params=_params("parallel", "parallel"),
    )(h, wt_in, wt_in)


def _out_residual(a, w_out, res, gate, nxt, seq, name, lhs=None):
    def fn(rows, bats, vecs):
        acc, rv = rows[0], rows[-1]
        x_new = rv + bats[0] * acc
        made = [] if lhs is None else [rows[1]]
        return [x_new, acc, _norm_mod(x_new, vecs[0], bats[1], bats[2])] + made, [], []
    outs = [(D_MODEL, F32), (D_MODEL, BF16), (D_MODEL, BF16)] + ([] if lhs is None else [(D_MODEL, BF16)])
    return _rowmap(name, fn, seq, (a if lhs is not None else [a]) + [res], [gate, nxt["shift"], nxt["scale"]],
                   [nxt["gamma"]], row_outs=outs, ts=_tile(seq, (512, 256, 128)), mm=(w_out, "nn"), lhs=lhs)


def _out_loss(a, w_out, res, gate, target, seq, name):
    def fn(rows, bats, vecs):
        acc, rv, tv = rows
        err = rv + bats[0] * acc - tv
        dy = err * (1.0 / D_MODEL)
        return [dy, bats[0] * dy], [_colsum(dy * acc)], [_colsum(err * err)]
    return _rowmap(name, fn, seq, [a, res, target], [gate], row_outs=[(D_MODEL, F32), (D_MODEL, BF16)],
                   bat_outs=[D_MODEL], vec_outs=[D_MODEL], ts=_tile(seq, (512, 256, 128)), mm=(w_out, "nn"))


def _ffn_bwd_x(df, dres, saved, p, seq, tag, prev=None, early=None, mid=None):
    x, h, g, u, a, w_in, w_out = saved
    first = None if early is None else early(a, df)

    def act_bwd(rows, bats, vecs):
        dav, gv, uv = rows[0], rows[1].astype(F32), rows[2].astype(F32)
        sg = _sigmoid(gv)
        silu = gv * sg
        dg = dav * uv * (sg * (1.0 + gv * (1.0 - sg)))
        return [jnp.concatenate([dg, dav * silu], axis=1)], [], []
    dgu = _rowmap(f"{tag}_bwd_da", act_bwd, seq, [df, g, u], row_outs=[(2 * D_FF, BF16)],
                  ts=_tile(seq, (512, 256, 128)), mm=(w_out, "nt"), after=first)[0]
    operands = (a, df, dgu, h)
    after = None if mid is None else mid(operands)
    return _norm_mod_bwd([(dgu, w_in)], x, dres, p, seq, f"{tag}_bwd_norm", prev, after=after), operands


def _ffn_bwd_wout(a, df, tag):
    return _mm(a, df, "tn", f"{tag}_bwd_wout", out_dtype=BF16, tm=256, tn=D_MODEL)


def _ffn_bwd_win(operands, tag, after=None, half=None):
    _, _, dgu, h = operands
    if half is None:
        return _mm(dgu, h, "tn", f"{tag}_bwd_win", out_dtype=BF16, tm=512, tn=D_MODEL, after=after)
    return _mm(dgu, h, "tn", f"{tag}_bwd_win{half}", out_dtype=BF16, tm=512, tn=D_MODEL // 2, after=after,
               b_cols=(half, 1))


def _shift_rows(v, k, forward):
    n = v.shape[0]
    row = lax.broadcasted_iota(jnp.int32, v.shape, 0)
    if forward:
        return jnp.where(row >= k, pltpu.roll(v, k, 0), 0.0)
    return jnp.where(row < n - k, pltpu.roll(v, n - k, 0), 0.0)


def _window_sums(v, forward):
    out, s, k = [], v, 1
    for _ in range(POOL_GROUPS):
        s = s + _shift_rows(s, k, forward)
        out.append(s)
        k *= 2
    return out


def _by_group(vals, g):
    out = vals[-1]
    for idx in range(len(vals) - 2, -1, -1):
        out = jnp.where(g == idx, vals[idx], out)
    return out


def _inv_count(shape, g):
    t1 = lax.broadcasted_iota(jnp.int32, shape, 0) + 1
    window = _by_group([jnp.int32(2 ** (i + 1)) for i in range(POOL_GROUPS)], g)
    return 1.0 / jnp.minimum(t1, window).astype(F32)


def _pool_fwd(u, grp, scale, seq):
    tokens = u.shape[0]

    def body(u_ref, grp_ref, sc_ref, pooled_ref, pg_ref, ps_ref):
        g = pl.program_id(1)
        uv = u_ref[...]
        sums = _by_group(_window_sums(uv, True), g)
        pooled = (sums * _inv_count(uv.shape, g) - uv).astype(BF16)
        pg = jnp.dot(pooled, grp_ref[0].astype(BF16), preferred_element_type=F32)
        pooled_ref[...] = pooled
        pg_ref[...] = pg
        ps_ref[...] = (pg * sc_ref[...]).astype(BF16)

    blk = pl.BlockSpec((seq, POOL_GROUP_DIM), lambda b, g: (b, g))
    return pl.pallas_call(
        body, name="pool_fwd", grid=(tokens // seq, POOL_GROUPS),
        out_shape=(jax.ShapeDtypeStruct(u.shape, BF16), jax.ShapeDtypeStruct(u.shape, F32),
                   jax.ShapeDtypeStruct(u.shape, BF16)),
        in_specs=[blk, pl.BlockSpec((1, POOL_GROUP_DIM, POOL_GROUP_DIM), lambda b, g: (g, 0, 0)),
                  pl.BlockSpec((1, POOL_GROUP_DIM), lambda b, g: (0, g))],
        out_specs=(blk, blk, blk),
        compiler_params=_params("parallel", "parallel"),
    )(u, grp, scale)


def _pool_bwd(dps, pooled, pg, grp, scale, seq):
    tokens = dps.shape[0]

    def body(dps_ref, pooled_ref, pg_ref, grp_ref, sc_ref, du_ref, dgrp_ref, dsc_ref):
        g, b = pl.program_id(0), pl.program_id(1)
        dpsv = dps_ref[...]
        dpg = (dpsv * sc_ref[...]).astype(BF16)
        dsc = _colsum(dpsv * pg_ref[...])
        dgrp = lax.dot_general(pooled_ref[...], dpg, _DIMS["tn"], preferred_element_type=F32)

        @pl.when(b == 0)
        def _():
            dsc_ref[...] = dsc
            dgrp_ref[0] = dgrp

        @pl.when(b > 0)
        def _():
            dsc_ref[...] += dsc
            dgrp_ref[0] += dgrp

        dpool = lax.dot_general(dpg, grp_ref[0].astype(BF16), _DIMS["nt"], preferred_element_type=F32)
        sums = _by_group(_window_sums(dpool * _inv_count(dpool.shape, g), False), g)
        du_ref[...] = (sums - dpool).astype(BF16)

    blk = pl.BlockSpec((seq, POOL_GROUP_DIM), lambda g, b: (b, g))
    grp_spec = pl.BlockSpec((1, POOL_GROUP_DIM, POOL_GROUP_DIM), lambda g, b: (g, 0, 0))
    vec_spec = pl.BlockSpec((1, POOL_GROUP_DIM), lambda g, b: (0, g))
    return pl.pallas_call(
        body, name="pool_bwd", grid=(POOL_GROUPS, tokens // seq),
        out_shape=(jax.ShapeDtypeStruct(dps.shape, BF16), jax.ShapeDtypeStruct(grp.shape, F32),
                   jax.ShapeDtypeStruct(scale.shape, F32)),
        in_specs=[blk, blk, blk, grp_spec, vec_spec],
        out_specs=(blk, grp_spec, vec_spec),
        compiler_params=_params("arbitrary", "arbitrary"),
    )(dps, pooled, pg, grp, scale)


def _lane(shape):
    return lax.broadcasted_iota(jnp.int32, shape, len(shape) - 1)


def _rot(y):
    lane = _lane(y.shape)
    r = jnp.where(lane < QK_NOPE + QK_ROPE // 2,
                  -pltpu.roll(y, HEAD_SLAB - QK_ROPE // 2, 1), pltpu.roll(y, QK_ROPE // 2, 1))
    return jnp.where(jnp.logical_and(lane >= QK_NOPE, lane < QK_NOPE + QK_ROPE), r, 0.0)


def _part_rstd(x):
    sq = x * x
    nope = _lane(x.shape) < QK_NOPE
    s_nope = jnp.sum(jnp.where(nope, sq, 0.0), axis=-1, keepdims=True)
    s_rope = jnp.sum(sq, axis=-1, keepdims=True) - s_nope
    return jnp.where(nope, lax.rsqrt(s_nope * (1.0 / QK_NOPE) + NORM_EPS),
                     lax.rsqrt(s_rope * (1.0 / QK_ROPE) + NORM_EPS))


def _part_norm_bwd(dy, x, r, g):
    nope = _lane(x.shape) < QK_NOPE
    xhat = x * r
    dxhat = dy * g
    prod = dxhat * xhat
    m_nope = jnp.sum(jnp.where(nope, prod, 0.0), axis=-1, keepdims=True)
    m_rope = jnp.sum(prod, axis=-1, keepdims=True) - m_nope
    mean = jnp.where(nope, m_nope * (1.0 / QK_NOPE), m_rope * (1.0 / QK_ROPE))
    return r * (dxhat - xhat * mean), dy * xhat


def _mixer_in(h, wt_a, wt_p, wt_g, g_q, g_kv, seq):
    def fn(rows, bats, vecs):
        z_a, z_p, z_g = rows[:3]
        q, kv = z_a[:, :Q_LORA], z_a[:, Q_LORA:Q_LORA + KV_LORA]
        return [z_a, z_p, z_g, q * _rstd(q) * vecs[0], kv * _rstd(kv) * vecs[1]], [], []
    return _rowmap("mix_in", fn, seq, [h], vecs=[g_q, g_kv], lhs=lambda rows: rows[0],
                   mm=[(wt_a, "nt"), (wt_p, "nt"), (wt_g, "nt")], mm_sum=False,
                   row_outs=[(wt_a.shape[0], F32), (wt_p.shape[0], F32), (wt_g.shape[0], BF16),
                             (Q_LORA, BF16), (KV_LORA, BF16)])


def _latent_norm_bwd(dqp, wtq_pad, dkv, wt_kv, dkr, z_a, g_q, g_kv, seq):
    def fn(rows, bats, vecs):
        dq, dkv, dkrv, z = rows
        q, kv = z[:, :Q_LORA], z[:, Q_LORA:Q_LORA + KV_LORA]
        dxq, dgq = _norm_bwd(dq, q, _rstd(q), vecs[0])
        dxkv, dgkv = _norm_bwd(dkv, kv, _rstd(kv), vecs[1])
        return [jnp.concatenate([dxq, dxkv, dkrv], axis=1)], [], [_colsum(dgq), _colsum(dgkv)]
    return _rowmap("latent_norm_bwd", fn, seq, [dqp, dkv, dkr, z_a], vecs=[g_q, g_kv],
                   row_outs=[(Q_LORA + KV_LORA + HEAD_SLAB, BF16)], vec_outs=[Q_LORA, KV_LORA],
                   mm=[(wtq_pad, "nn"), (wt_kv, "nn")], mm_sum=False)


def _qk_prep_fwd(qn, wtq_pad, kvn, wt_kv, z_a, pos, g_q, g_kn, g_kr, inv_freq, seq):
    def fn(rows, bats, vecs):
        qv, kvv, kr, p = rows
        gq, gkn, gkr, invf = vecs
        ang = p * invf
        cos, sin = jnp.cos(ang), jnp.sin(ang)
        nope = _lane(kr.shape) < QK_NOPE
        krn = kr * _rstd(kr, QK_ROPE) * gkr
        krr = krn * cos + _rot(krn) * sin
        qs, ks, vs = [], [], []
        for h in range(N_HEADS):
            xq = qv[:, h * HEAD_SLAB:(h + 1) * HEAD_SLAB]
            y = xq * _part_rstd(xq) * gq
            qs.append(y * cos + _rot(y) * sin)
            xk = kvv[:, h * HEAD_SLAB:(h + 1) * HEAD_SLAB]
            kn = jnp.where(nope, xk, 0.0)
            ks.append(jnp.where(nope, kn * _rstd(kn, QK_NOPE) * gkn, krr))
            vs.append(jnp.where(nope, 0.0, xk))
        return [jnp.concatenate(v, axis=1) for v in (qs, ks, vs)] + [qv, kvv], [], []
    width = N_HEADS * HEAD_SLAB
    return _rowmap("qk_prep", fn, seq, [qn, kvn, (z_a, HEAD_SLAB, 5), pos], vecs=[g_q, g_kn, g_kr, inv_freq],
                   row_outs=[(width, BF16)] * 3 + [(width, F32)] * 2, mm=[(wtq_pad, "nt"), (wt_kv, "nt")],
                   mm_sum=False)


def _qk_prep_bwd(dqc, dkc, dvp, qp, kv, z_a, pos, g_q, g_kn, g_kr, inv_freq, seq):
    def fn(rows, bats, vecs):
        dq, dk, dv, qv, kvv, kr, p = rows
        gq, gkn, gkr, invf = vecs
        ang = p * invf
        cos, sin = jnp.cos(ang), jnp.sin(ang)
        nope = _lane(kr.shape) < QK_NOPE
        dqs, dkvs = [], []
        dgq = jnp.zeros((1, HEAD_SLAB), F32)
        dgkn = jnp.zeros((1, HEAD_SLAB), F32)
        dkrr = jnp.zeros(kr.shape, F32)
        for h in range(N_HEADS):
            sl = slice(h * HEAD_SLAB, (h + 1) * HEAD_SLAB)
            dyr = dq[:, sl]
            dy = dyr * cos - _rot(dyr * sin)
            xq = qv[:, sl]
            dx, dg = _part_norm_bwd(dy, xq, _part_rstd(xq), gq)
            dqs.append(dx)
            dgq = dgq + _colsum(dg)
            dkh = dk[:, sl]
            dkrr = dkrr + jnp.where(nope, 0.0, dkh)
            kn = jnp.where(nope, kvv[:, sl], 0.0)
            dxk, dgk = _norm_bwd(jnp.where(nope, dkh, 0.0), kn, _rstd(kn, QK_NOPE), gkn, QK_NOPE)
            dgkn = dgkn + _colsum(dgk)
            dkvs.append(jnp.where(nope, dxk, dv[:, sl]))
        dkrn = dkrr * cos - _rot(dkrr * sin)
        dkr, dgkr = _norm_bwd(dkrn, kr, _rstd(kr, QK_ROPE), gkr, QK_ROPE)
        return ([jnp.concatenate(dqs, axis=1), jnp.concatenate(dkvs, axis=1), dkr], [],
                [dgq, dgkn, _colsum(dgkr)])
    width = N_HEADS * HEAD_SLAB
    return _rowmap("qk_prep_bwd", fn, seq, [dqc, dkc, dvp, qp, kv, (z_a, HEAD_SLAB, 5), pos],
                   vecs=[g_q, g_kn, g_kr, inv_freq],
                   row_outs=[(width, BF16), (width, BF16), (HEAD_SLAB, F32)],
                   vec_outs=[HEAD_SLAB] * 3, ts=_tile(seq, (512, 256, 128, 64, 32, 16, 8)))


def _scores(q, k_ref, keys, tq):
    s = lax.dot_general(q, k_ref[0:keys, :], _DIMS["nt"], preferred_element_type=F32) * ATTN_SCALE
    row = lax.broadcasted_iota(jnp.int32, (tq, tq), 0)
    col = lax.broadcasted_iota(jnp.int32, (tq, tq), 1)
    diag = jnp.where(col <= row, s[:, keys - tq:], -1e30)
    return diag if keys == tq else jnp.concatenate([s[:, :keys - tq], diag], axis=1)


def _attn_fwd(qc, kc, vp, seq):
    tokens = qc.shape[0]
    tq = _tile(seq, (256, 128))
    nq = seq // tq

    def body(q_ref, k_ref, v_ref, o_ref, lse_ref):
        for i in range(nq):
            rows, keys = slice(i * tq, (i + 1) * tq), (i + 1) * tq
            s = _scores(q_ref[rows, :], k_ref, keys, tq)
            m = jnp.max(s, axis=-1, keepdims=True)
            p = jnp.exp(s - m)
            l = jnp.sum(p, axis=-1, keepdims=True)
            acc = jnp.dot(p.astype(BF16), v_ref[0:keys, :], preferred_element_type=F32)
            o_ref[rows, :] = (acc / l).astype(BF16)
            lse_ref[rows, :] = jnp.broadcast_to(m + jnp.log(l), (tq, HEAD_SLAB))

    spec = pl.BlockSpec((seq, HEAD_SLAB), lambda b, h: (b, h))
    return pl.pallas_call(
        body, name="attn_fwd", grid=(tokens // seq, N_HEADS),
        out_shape=(jax.ShapeDtypeStruct(qc.shape, BF16), jax.ShapeDtypeStruct(qc.shape, F32)),
        in_specs=[spec] * 3, out_specs=(spec, spec),
        compiler_params=_params("parallel", "parallel"),
    )(qc, kc, vp)


def _attn_bwd(qc, kc, vp, o, lse, do, seq):
    tokens = qc.shape[0]
    tq = _tile(seq, (256, 128))
    nq = seq // tq

    def body(q_ref, k_ref, v_ref, o_ref, lse_ref, do_ref, dq_ref, dk_ref, dv_ref):
        dk_ref[...] = jnp.zeros(dk_ref.shape, F32)
        dv_ref[...] = jnp.zeros(dv_ref.shape, F32)
        for i in range(nq):
            rows, keys = slice(i * tq, (i + 1) * tq), (i + 1) * tq
            q, dov = q_ref[rows, :], do_ref[rows, :]
            delta = jnp.sum(dov.astype(F32) * o_ref[rows, :].astype(F32), axis=-1, keepdims=True)
            s = _scores(q, k_ref, keys, tq)
            p = jnp.exp(s - jnp.tile(lse_ref[rows, :], (1, keys // HEAD_SLAB)))
            dp = lax.dot_general(dov, v_ref[0:keys, :], _DIMS["nt"], preferred_element_type=F32)
            ds = (p * (dp - delta) * ATTN_SCALE).astype(BF16)
            dq_ref[rows, :] = jnp.dot(ds, k_ref[0:keys, :], preferred_element_type=F32)
            dk_ref[0:keys, :] += lax.dot_general(ds, q, _DIMS["tn"], preferred_element_type=F32)
            dv_ref[0:keys, :] += lax.dot_general(p.astype(BF16), dov, _DIMS["tn"], preferred_element_type=F32)

    spec = pl.BlockSpec((seq, HEAD_SLAB), lambda b, h: (b, h))
    out = jax.ShapeDtypeStruct(qc.shape, F32)
    return pl.pallas_call(
        body, name="attn_bwd", grid=(tokens // seq, N_HEADS),
        out_shape=(out, out, out), in_specs=[spec] * 6, out_specs=(spec, spec, spec),
        compiler_params=_params("parallel", "parallel"),
    )(qc, kc, vp, o, lse, do)


def _adamw(w, g, m, v, name):
    rows, cols = w.shape
    whole = rows * cols * 4 <= ADAMW_WHOLE_BYTES
    tr = rows if whole else _tile(rows, (256, 128, 64, 32, 16, 8))
    c1 = 1.0 - ADAM_B1 ** ADAM_STEP
    c2 = 1.0 - ADAM_B2 ** ADAM_STEP

    def body(w_ref, g_ref, m_ref, v_ref, d_ref, nm_ref, nv_ref):
        gv = g_ref[...]
        nm = ADAM_B1 * m_ref[...] + (1.0 - ADAM_B1) * gv
        nv = ADAM_B2 * v_ref[...] + (1.0 - ADAM_B2) * (gv * gv)
        d_ref[...] = -ADAM_LR * ((nm / c1) / (jnp.sqrt(nv / c2) + ADAM_EPS) + ADAM_WD * w_ref[...])
        nm_ref[...] = nm
        nv_ref[...] = nv

    spec = pl.BlockSpec((tr, cols), lambda i: (i, 0))
    out = jax.ShapeDtypeStruct(w.shape, F32)
    return pl.pallas_call(
        body, name=name, grid=(rows // tr,), out_shape=(out, out, out),
        in_specs=[spec] * 4, out_specs=(spec, spec, spec),
        compiler_params=_params("parallel"),
    )(w, g, m, v)


def _adamw_landed(w, landed, m, v, name):
    rows, cols = w.shape
    tr = _tile(rows, (176, 128, 96, 64, 32, 16, 8))
    c1 = 1.0 - ADAM_B1 ** ADAM_STEP
    c2 = 1.0 - ADAM_B2 ** ADAM_STEP
    n_parts = len(landed)

    def body(*refs):
        w_ref, m_ref, v_ref = refs[:3]
        g_ref, d_ref, nm_ref, nv_ref = refs[3 + n_parts:]
        parts = []
        for x_ref in refs[3:3 + n_parts]:
            acc = x_ref[0].astype(F32)
            for d in range(1, N_DEV):
                acc = acc + x_ref[d].astype(F32)
            parts.append(acc)
        gv = parts[0] if n_parts == 1 else jnp.concatenate(parts, axis=1)
        nm = ADAM_B1 * m_ref[...] + (1.0 - ADAM_B1) * gv
        nv = ADAM_B2 * v_ref[...] + (1.0 - ADAM_B2) * (gv * gv)
        g_ref[...] = gv
        d_ref[...] = -ADAM_LR * ((nm / c1) / (jnp.sqrt(nv / c2) + ADAM_EPS) + ADAM_WD * w_ref[...])
        nm_ref[...] = nm
        nv_ref[...] = nv

    spec = pl.BlockSpec((tr, cols), lambda i: (i, 0))
    out = jax.ShapeDtypeStruct(w.shape, F32)
    return pl.pallas_call(
        body, name=name, grid=(rows // tr,), out_shape=(out, out, out, out),
        in_specs=[spec] * 3 + [pl.BlockSpec((N_DEV, tr, x.shape[2]), lambda i: (0, i, 0)) for x in landed],
        out_specs=(spec, spec, spec, spec),
        compiler_params=_params("parallel"),
    )(w, m, v, *landed)


def _mod_cols(c_all, w_ada, b_cols):
    def body(c_ref, w_ref, b_ref, act_ref, mod_ref):
        cv = c_ref[...]
        act = cv * _sigmoid(cv)
        act_ref[...] = act
        mod_ref[...] = jnp.dot(act.astype(BF16), w_ref[...].astype(BF16),
                               preferred_element_type=F32) + b_ref[...]

    n = w_ada.shape[1]
    return pl.pallas_call(
        body, name="mod_cols",
        out_shape=(jax.ShapeDtypeStruct(c_all.shape, F32), jax.ShapeDtypeStruct((c_all.shape[0], n), F32)),
        compiler_params=pltpu.CompilerParams(vmem_limit_bytes=VMEM_LIMIT),
    )(c_all, w_ada, b_cols)


def _ada_grads(c_act, dmod_all, dmod_cols):
    def body(c_ref, d_ref, dc_ref, gw_ref, gb_ref):
        gw_ref[...] = lax.dot_general(c_ref[...].astype(BF16), dc_ref[...].astype(BF16), _DIMS["tn"],
                                      preferred_element_type=F32)
        gb_ref[...] = _colsum(d_ref[...])

    return pl.pallas_call(
        body, name="ada_grads",
        out_shape=(jax.ShapeDtypeStruct((c_act.shape[1], dmod_cols.shape[1]), F32),
                   jax.ShapeDtypeStruct((1, dmod_all.shape[1]), F32)),
        compiler_params=pltpu.CompilerParams(vmem_limit_bytes=VMEM_LIMIT),
    )(c_act, dmod_all, dmod_cols)


def _flat_rows(a):
    flat = a.reshape(-1)
    pad = (-flat.shape[0]) % (LANES * SUBLANES)
    if pad:
        flat = jnp.pad(flat, (0, pad))
    return flat.reshape(-1, LANES)


def _gather_start(w, groups, tag, after=None, peers=None):
    shards = [[(w[n] if n in ROW_SHARDED else w[n].T).astype(BF16) for n in names] for names in groups]
    return _exchange_start_groups(shards, f"gather_{tag}_start", after=after, peers=peers)


def _gather_wait(handle, names, tag, after, peers=ALL_PEERS):
    landed = _exchange_wait(handle, f"gather_{tag}_wait", after=after, peers=peers)
    if peers == CHIP_PEERS:
        landed = [_sibling_forward(x, f"gather_{tag}_forward{i}") for i, x in enumerate(landed)]
    return {n: g.reshape(-1, g.shape[2]) for n, g in zip(names, landed)}


def _scatter_start(grads, names, tag, after=None):
    blocks = [grads[n].reshape(N_DEV, -1, grads[n].shape[1]) for n in names]
    return _exchange_start(blocks, f"scatter_{tag}_start", scatter=True, after=after)


def _scatter_wait(handle, names, tag, after):
    landed = _exchange_wait(handle, f"scatter_{tag}_wait", scatter=True, after=after)
    return {n: [x] for n, x in zip(names, landed)}


def _pack_small(vals):
    return jnp.concatenate([_flat_rows(v.astype(F32)) for v in vals], axis=0)


def _unpack_small(packed, like):
    out, row = [], 0
    for v in like:
        rows = _flat_rows(v).shape[0]
        out.append(packed[row:row + rows].reshape(-1)[:v.size].reshape(v.shape))
        row += rows
    return out


def _lanes128(*parts):
    out = jnp.zeros((HEAD_SLAB,), F32)
    for off, v in parts:
        out = lax.dynamic_update_slice(out, v.reshape(-1).astype(F32), (off,))
    return out.reshape(1, HEAD_SLAB)


def _step(x, c, positions, w, m, v, loss_target):
    nseq, seq, _ = x.shape
    tokens = nseq * seq
    me = _index(_my_pos())
    strip = lambda d: {n: (a[0] if a.ndim > 2 else a) for n, a in d.items()}
    shapes = {n: a.shape for n, a in w.items()}
    w, m, v = strip(w), strip(m), strip(v)

    c_all = _all_gather(c.reshape(-1, LANES), "gather_c").reshape(N_DEV * nseq, D_MODEL)
    n_ada = w["w_ada"].shape[1]
    b_cols = lax.dynamic_slice(w["b_ada"], (0, me * n_ada), (1, n_ada))
    c_act, mod_cols = _mod_cols(c_all, w["w_ada"], b_cols)
    mod_all = _all_gather(mod_cols, "gather_mod")
    mod = lax.dynamic_slice(mod_all, (0, me * nseq, 0), (N_DEV, nseq, n_ada))
    mod = mod.transpose(1, 0, 2).reshape(nseq, 3, 3, 1, D_MODEL)

    (h_f1i, h_f1o, h_mix_in, h_mix, h_f2), tok = _gather_start(
        w, (("w_ffn1_in",), ("w_ffn1_out",), MIXER[:1], MIXER[1:], ("w_ffn2_in", "w_ffn2_out")), "weights",
        after=mod_all, peers=[CHIP_PEERS] + [ALL_PEERS] * 4)
    started = tok[0:1, 0:1]

    g_q = _lanes128((0, w["q_norm_nope"]), (QK_NOPE, w["q_norm_rope"]))
    g_kn = _lanes128((0, w["k_norm_nope"]))
    g_kr = _lanes128((QK_NOPE, w["k_norm_rope"]))
    freq = ROPE_THETA ** (-jnp.arange(0, QK_ROPE, 2, dtype=F32) / QK_ROPE)
    inv_freq = _lanes128((QK_NOPE, jnp.concatenate([freq, freq])))
    pos = positions.reshape(tokens, 1).astype(F32)

    def sub(k, gamma, coef):
        return dict(gamma=w[gamma], shift=mod[:, k, 0] + started, scale=mod[:, k, 1], gate=coef * mod[:, k, 2])
    p1, pm, p2 = sub(0, "norm_ffn1", 0.5), sub(1, "norm_mix", 1.0), sub(2, "norm_ffn2", 0.5)
    t_big = _tile(tokens, (2048, 1024, 512))
    t_mid = _tile(tokens, (1024, 512))

    x0 = x.reshape(tokens, D_MODEL)
    h1 = _norm_mod_fwd(x0, p1, seq, "ffn1_norm")
    wt_f1i = _gather_wait(h_f1i, ("w_ffn1_in",), "ffn1_in", h1, peers=CHIP_PEERS)["w_ffn1_in"]
    g1, u1, a1 = _ffn_in_act(h1, wt_f1i, "ffn1_in")
    w_f1o = _gather_wait(h_f1o, ("w_ffn1_out",), "ffn1_out", a1)["w_ffn1_out"]
    x1, f1, h2 = _out_residual(a1, w_f1o, x0, p1["gate"], pm, seq, "ffn1_out")
    saved1 = (x0, h1, g1, u1, a1, wt_f1i, w_f1o)

    wt_in = _gather_wait(h_mix_in, MIXER[:1], "mix_in", h2)["w_in"]
    zero_rows = lambda rows: jnp.zeros((rows, D_MODEL), BF16)
    wt_p = wt_in[:512]
    wt_a = jnp.concatenate([wt_in[512:1152], zero_rows(QK_NOPE), wt_in[1152:1184], zero_rows(32)], axis=0)
    wt_g = wt_in[1184:]
    z_a, z_p, z_g, qn, kvn = _mixer_in(h2, wt_a, wt_p, wt_g, w["q_a_norm"], w["kv_a_norm"], seq)

    full = _gather_wait(h_mix, MIXER[1:], "mix", z_g)
    wtq_pad = jnp.pad(full["w_q_up"].reshape(N_HEADS, 96, Q_LORA), ((0, 0), (0, 32), (0, 0))).reshape(-1, Q_LORA)
    wtmla_pad = jnp.pad(full["w_mla_proj"].reshape(D_MODEL, N_HEADS, 64), ((0, 0), (0, 0), (64, 0))).reshape(D_MODEL, -1)
    wt_pool, wt_kv, w_mix_out = full["w_pool_proj"], full["w_kv_up"], full["w_out"]
    pooled, pg, ps = _pool_fwd(z_p, w["pool_grp"], w["pool_scale"], seq)
    br_pool = _mm(ps, wt_pool, "nt", "pool_proj", out_dtype=BF16, tm=t_big, tn=D_MODEL)
    qc, kc, vp, qp, kv = _qk_prep_fwd(qn, wtq_pad, kvn, wt_kv, z_a, pos, g_q, g_kn, g_kr, inv_freq, seq)
    attn, lse = _attn_fwd(qc, kc, vp, seq)
    br_mla = _mm(attn, wtmla_pad, "nt", "mla_proj", out_dtype=BF16, tm=t_mid, tn=D_MODEL)

    def merge(rows):
        zg, bp, bm = (r.astype(F32) for r in rows[:3])
        return (_sigmoid(zg[:, :D_MODEL]) * bp + _sigmoid(zg[:, D_MODEL:]) * bm).astype(BF16)
    x2, o_mix, h3, merged = _out_residual([z_g, br_pool, br_mla], w_mix_out, x1, pm["gate"], p2, seq, "mix_out",
                                          lhs=merge)

    ffn2_w = _gather_wait(h_f2, ("w_ffn2_in", "w_ffn2_out"), "ffn2", h3)
    g2, u2, a2 = _ffn_in_act(h3, ffn2_w["w_ffn2_in"], "ffn2_in")
    dy, df2, dgate2, sq_err = _out_loss(a2, ffn2_w["w_ffn2_out"], x2, p2["gate"],
                                        loss_target.reshape(tokens, D_MODEL), seq, "ffn2_out")
    saved2 = (x2, h3, g2, u2, a2, ffn2_w["w_ffn2_in"], ffn2_w["w_ffn2_out"])

    grads = {}
    (dx2, do_mix, dsh2, dsc2, dgate_m, dg_ffn2), ops2 = _ffn_bwd_x(df2, dy, saved2, p2, seq, "ffn2", (o_mix, pm["gate"]))
    grads["w_ffn2_out"], grads["w_ffn2_in"] = _ffn_bwd_wout(ops2[0], ops2[1], "ffn2"), _ffn_bwd_win(ops2, "ffn2")
    s_f2, tok = _scatter_start(grads, ("w_ffn2_in", "w_ffn2_out"), "ffn2")

    grads["w_out"] = _mm(merged, do_mix, "tn", "mix_bwd_wout", out_dtype=BF16, tm=512, tn=D_MODEL)

    def merge_bwd(rows, bats, vecs):
        dmv, zg, bp, bm = (r.astype(F32) for r in rows)
        s_p, s_m = _sigmoid(zg[:, :D_MODEL]), _sigmoid(zg[:, D_MODEL:])
        dzg = jnp.concatenate([dmv * bp * s_p * (1.0 - s_p), dmv * bm * s_m * (1.0 - s_m)], axis=1)
        return [dmv * s_p, dmv * s_m, dzg], [], []
    dbr_pool, dbr_mla, dz_g = _rowmap("mix_bwd_dmerged", merge_bwd, seq, [do_mix, z_g, br_pool, br_mla],
                                      row_outs=[(D_MODEL, BF16), (D_MODEL, BF16), (2 * D_MODEL, BF16)],
                                      mm=(w_mix_out, "nt"))

    grads["w_pool_proj"] = _mm(dbr_pool, ps, "tn", "pool_bwd_wproj", out_dtype=BF16, tm=512, tn=POOL_WIDTH)
    dps = _mm(dbr_pool, wt_pool, "nn", "pool_bwd_dps", tm=t_big, tn=POOL_WIDTH)
    dz_p, dgrp, dpool_scale = _pool_bwd(dps, pooled, pg, w["pool_grp"], w["pool_scale"] + tok[0:1, 0:1], seq)

    dwtmla_pad = _mm(dbr_mla, attn, "tn", "mla_bwd_wproj", out_dtype=BF16, tm=512, tn=D_MODEL)
    grads["w_mla_proj"] = dwtmla_pad.reshape(D_MODEL, N_HEADS, HEAD_SLAB)[:, :, 64:].reshape(D_MODEL, -1)
    d_attn = _mm(dbr_mla, wtmla_pad, "nn", "mla_bwd_dattn", out_dtype=BF16, tm=t_mid, tn=D_MODEL)
    dqc, dkc, dvp = _attn_bwd(qc, kc, vp, attn, lse, d_attn, seq)
    dqp, dkv, dkr, dg_q, dg_kn, dg_kr = _qk_prep_bwd(dqc, dkc, dvp, qp, kv, z_a, pos, g_q, g_kn, g_kr, inv_freq, seq)
    dwtq_pad = _mm(dqp, qn, "tn", "q_up_bwd_w", out_dtype=BF16, tm=512, tn=Q_LORA)
    grads["w_q_up"] = dwtq_pad.reshape(N_HEADS, HEAD_SLAB, Q_LORA)[:, :96].reshape(-1, Q_LORA)
    grads["w_kv_up"] = _mm(dkv, kvn, "tn", "kv_up_bwd_w", out_dtype=BF16, tm=512, tn=KV_LORA)
    dz_a, dg_qa, dg_kva = _latent_norm_bwd(dqp, wtq_pad, dkv, wt_kv, dkr, z_a, w["q_a_norm"], w["kv_a_norm"], seq)

    dwt_a = _mm(dz_a, h2, "tn", "mix_in_bwd_wa", out_dtype=BF16, tm=256, tn=D_MODEL)
    dwt_p = _mm(dz_p, h2, "tn", "mix_in_bwd_wp", out_dtype=BF16, tm=512, tn=D_MODEL)
    dwt_g = _mm(dz_g, h2, "tn", "mix_in_bwd_wg", out_dtype=BF16, tm=512, tn=D_MODEL)
    grads["w_in"] = jnp.concatenate([dwt_p, dwt_a[:640], dwt_a[704:736], dwt_g], axis=0)

    small_early = [dg_ffn2.reshape(w["norm_ffn2"].shape), dgrp, dpool_scale, dg_qa, dg_kva, dg_q[:, :QK_NOPE],
                   dg_q[:, QK_NOPE:QK_NOPE + QK_ROPE], dg_kn[:, :QK_NOPE], dg_kr[:, QK_NOPE:QK_NOPE + QK_ROPE]]
    s_small, tok = _exchange_start([_pack_small(small_early)], "gather_small_start")
    s_mix, tok = _scatter_start(grads, MIXER, "mix", after=tok)
    dh2 = [(dz_a, wt_a), (dz_p, wt_p), (dz_g, wt_g)]
    pm_tied = dict(pm, scale=pm["scale"] + tok[0:1, 0:1])
    dx1, df1, dsh_m, dsc_m, dgate1, dg_mix = _norm_mod_bwd(dh2, x1, dx2, pm_tied, seq, "mix_bwd_norm", (f1, p1["gate"]))

    handles = {}

    def ffn1_early(a, df):
        grads["w_ffn1_out"] = _ffn_bwd_wout(a, df, "ffn1")
        handles["f1o"], token = _scatter_start(grads, ("w_ffn1_out",), "ffn1_out")
        return token

    def ffn1_mid(operands):
        first = _ffn_bwd_win(operands, "ffn1", half=0)
        handles["f1i0"], token = _exchange_start([first.reshape(N_DEV, -1, first.shape[1])],
                                                 "scatter_ffn1_in0_start", scatter=True)
        return token

    (dx0, dsh1, dsc1, dg_ffn1), ops1 = _ffn_bwd_x(df1, dx1, saved1, p1, seq, "ffn1", early=ffn1_early,
                                                     mid=ffn1_mid)
    s_f1o = handles["f1o"]

    dmod = jnp.stack([jnp.stack([dsh1, dsc1, 0.5 * dgate1], axis=1),
                      jnp.stack([dsh_m, dsc_m, dgate_m], axis=1),
                      jnp.stack([dsh2, dsc2, 0.5 * dgate2], axis=1)], axis=1)
    n_dmod = nseq * 9 * D_MODEL // LANES
    tail = _all_gather(jnp.concatenate([dmod.reshape(-1, LANES), _flat_rows(dg_ffn1), _flat_rows(dg_mix),
                                        _flat_rows(sq_err)], axis=0), "gather_dmod")
    dmod_all = tail[:, :n_dmod].reshape(N_DEV * nseq, 9 * D_MODEL)

    second = _ffn_bwd_win(ops1, "ffn1", after=tail, half=1)
    s_second, tok = _exchange_start([second.reshape(N_DEV, -1, second.shape[1])], "scatter_ffn1_in1_start",
                                    scatter=True, after=tail)
    s_f1i = (handles["f1i0"], s_second)

    dmod_cols = lax.dynamic_slice(dmod_all, (0, me * n_ada), (N_DEV * nseq, n_ada)) + tok[0:1, 0:1]
    g_w_ada, g_b_ada = _ada_grads(c_act, dmod_all, dmod_cols)
    tail_sum = _sum_blocks(tail[:, n_dmod:], "sum_tail")
    g_norm_ffn1 = tail_sum[:SUBLANES].reshape(1, D_MODEL)
    g_norm_mix = tail_sum[SUBLANES:2 * SUBLANES].reshape(1, D_MODEL)
    loss = 0.5 * jnp.sum(tail_sum[2 * SUBLANES:]) * (1.0 / D_MODEL)
    small_all = _exchange_wait(s_small, "gather_small_wait", after=g_b_ada)[0]
    small_sum = _sum_blocks(small_all, "sum_small")
    small = dict(zip(SMALL[2:], _unpack_small(small_sum, [w[n] for n in SMALL[2:]])))
    grad_w = dict(small, w_ada=g_w_ada, b_ada=g_b_ada, norm_ffn1=g_norm_ffn1, norm_mix=g_norm_mix)

    delta, new_m, new_v = {}, {}, {}

    def update(names, landed=None):
        for n in names:
            if landed is None:
                delta[n], new_m[n], new_v[n] = _adamw(w[n], grad_w[n], m[n], v[n], f"adamw_{n}")
            elif n in KEPT_TRANSPOSED:
                res = _adamw_landed(w[n].T, landed[n], m[n].T, v[n].T, f"adamw_{n}")
                grad_w[n], delta[n], new_m[n], new_v[n] = (r.T for r in res)
            elif n in ROW_SHARDED:
                grad_w[n], delta[n], new_m[n], new_v[n] = _adamw_landed(w[n], landed[n], m[n], v[n], f"adamw_{n}")
            else:
                grad_w[n] = _sum_blocks(landed[n][0], f"sum_{n}").T
                delta[n], new_m[n], new_v[n] = _adamw(w[n], grad_w[n], m[n], v[n], f"adamw_{n}")

    update(("w_ada",))
    rep = ("b_ada",) + SMALL
    d_s, m_s, v_s = _adamw(_pack_small([w[n] for n in rep]), _pack_small([grad_w[n] for n in rep]),
                           _pack_small([m[n] for n in rep]), _pack_small([v[n] for n in rep]), "adamw_small")
    like = [w[n] for n in rep]
    for dst, packed in ((delta, d_s), (new_m, m_s), (new_v, v_s)):
        dst.update(zip(rep, _unpack_small(packed, like)))
    update(("w_ffn2_in", "w_ffn2_out"), _scatter_wait(s_f2, ("w_ffn2_in", "w_ffn2_out"), "ffn2", after=d_s))
    update(MIXER, _scatter_wait(s_mix, MIXER, "mix", after=delta["w_ffn2_out"]))
    update(("w_ffn1_out",), _scatter_wait(s_f1o, ("w_ffn1_out",), "ffn1_out", after=delta["w_out"]))
    halves = [_exchange_wait(h, f"scatter_ffn1_in{i}_wait", scatter=True, after=delta["w_ffn1_out"])[0]
              for i, h in enumerate(s_f1i)]
    update(("w_ffn1_in",), {"w_ffn1_in": halves})

    lead = lambda d: [d[n].reshape(shapes[n]) for n in WEIGHTS]
    return (loss, dx0.reshape(x.shape), *lead(grad_w), *lead(delta), *lead(new_m), *lead(new_v))


def kernel(x, c, positions, w_ada, b_ada, norm_ffn1, w_ffn1_in, w_ffn1_out, norm_mix, w_in, pool_grp, pool_scale, w_pool_proj, q_a_norm, w_q_up, kv_a_norm, w_kv_up, q_norm_nope, q_norm_rope, k_norm_nope, k_norm_rope, w_mla_proj, w_out, norm_ffn2, w_ffn2_in, w_ffn2_out, loss_target, m_w_ada, m_b_ada, m_norm_ffn1, m_w_ffn1_in, m_w_ffn1_out, m_norm_mix, m_w_in, m_pool_grp, m_pool_scale, m_w_pool_proj, m_q_a_norm, m_w_q_up, m_kv_a_norm, m_w_kv_up, m_q_norm_nope, m_q_norm_rope, m_k_norm_nope, m_k_norm_rope, m_w_mla_proj, m_w_out, m_norm_ffn2, m_w_ffn2_in, m_w_ffn2_out, v_w_ada, v_b_ada, v_norm_ffn1, v_w_ffn1_in, v_w_ffn1_out, v_norm_mix, v_w_in, v_pool_grp, v_pool_scale, v_w_pool_proj, v_q_a_norm, v_w_q_up, v_kv_a_norm, v_w_kv_up, v_q_norm_nope, v_q_norm_rope, v_k_norm_nope, v_k_norm_rope, v_w_mla_proj, v_w_out, v_norm_ffn2, v_w_ffn2_in, v_w_ffn2_out):
    w = dict(w_ada=w_ada, b_ada=b_ada, norm_ffn1=norm_ffn1, w_ffn1_in=w_ffn1_in, w_ffn1_out=w_ffn1_out, norm_mix=norm_mix, w_in=w_in, pool_grp=pool_grp, pool_scale=pool_scale, w_pool_proj=w_pool_proj, q_a_norm=q_a_norm, w_q_up=w_q_up, kv_a_norm=kv_a_norm, w_kv_up=w_kv_up, q_norm_nope=q_norm_nope, q_norm_rope=q_norm_rope, k_norm_nope=k_norm_nope, k_norm_rope=k_norm_rope, w_mla_proj=w_mla_proj, w_out=w_out, norm_ffn2=norm_ffn2, w_ffn2_in=w_ffn2_in, w_ffn2_out=w_ffn2_out)
    m = dict(w_ada=m_w_ada, b_ada=m_b_ada, norm_ffn1=m_norm_ffn1, w_ffn1_in=m_w_ffn1_in, w_ffn1_out=m_w_ffn1_out, norm_mix=m_norm_mix, w_in=m_w_in, pool_grp=m_pool_grp, pool_scale=m_pool_scale, w_pool_proj=m_w_pool_proj, q_a_norm=m_q_a_norm, w_q_up=m_w_q_up, kv_a_norm=m_kv_a_norm, w_kv_up=m_w_kv_up, q_norm_nope=m_q_norm_nope, q_norm_rope=m_q_norm_rope, k_norm_nope=m_k_norm_nope, k_norm_rope=m_k_norm_rope, w_mla_proj=m_w_mla_proj, w_out=m_w_out, norm_ffn2=m_norm_ffn2, w_ffn2_in=m_w_ffn2_in, w_ffn2_out=m_w_ffn2_out)
    v = dict(w_ada=v_w_ada, b_ada=v_b_ada, norm_ffn1=v_norm_ffn1, w_ffn1_in=v_w_ffn1_in, w_ffn1_out=v_w_ffn1_out, norm_mix=v_norm_mix, w_in=v_w_in, pool_grp=v_pool_grp, pool_scale=v_pool_scale, w_pool_proj=v_w_pool_proj, q_a_norm=v_q_a_norm, w_q_up=v_w_q_up, kv_a_norm=v_kv_a_norm, w_kv_up=v_w_kv_up, q_norm_nope=v_q_norm_nope, q_norm_rope=v_q_norm_rope, k_norm_nope=v_k_norm_nope, k_norm_rope=v_k_norm_rope, w_mla_proj=v_w_mla_proj, w_out=v_w_out, norm_ffn2=v_norm_ffn2, w_ffn2_in=v_w_ffn2_in, w_ffn2_out=v_w_ffn2_out)
    return _step(x, c, positions, w, m, v, loss_target)
```

```python
import functools
import math

import jax
import jax.numpy as jnp
from jax import lax
from jax.experimental import pallas as pl
from jax.experimental.pallas import tpu as pltpu

F32 = jnp.float32
BF16 = jnp.bfloat16
MESH = pl.DeviceIdType.MESH
AXES = ("x", "y", "c")
N_DEV = 8

D_MODEL = 1024
D_FF = 2816
N_HEADS = 8
HEAD_SLAB = 128
QK_NOPE = 64
QK_ROPE = 32
POOL_WIDTH = 512
POOL_GROUPS = 4
POOL_GROUP_DIM = 128
Q_LORA = 384
KV_LORA = 256
ROPE_THETA = 10000.0
ATTN_SCALE = 1.0 / math.sqrt(QK_NOPE + QK_ROPE)
NORM_EPS = 1e-6
ADAM_LR, ADAM_B1, ADAM_B2, ADAM_EPS, ADAM_WD, ADAM_STEP = 0.001, 0.9, 0.999, 1e-08, 0.01, 10

LANES = 128
SUBLANES = 8
VMEM_LIMIT = 52 * 1024 * 1024
ADAMW_WHOLE_BYTES = 3 << 19
SUM_WHOLE_BYTES = 4 << 20

BIG = ("w_ffn1_in", "w_ffn1_out", "w_in", "w_pool_proj", "w_q_up", "w_kv_up",
       "w_mla_proj", "w_out", "w_ffn2_in", "w_ffn2_out")
ROW_SHARDED = ("w_ffn1_out", "w_out", "w_ffn2_out")
MIXER = ("w_in", "w_pool_proj", "w_q_up", "w_kv_up", "w_mla_proj", "w_out")
KEPT_TRANSPOSED = ("w_ffn1_in", "w_ffn2_in", "w_in", "w_q_up")
SMALL = ("norm_ffn1", "norm_mix", "norm_ffn2", "pool_grp", "pool_scale", "q_a_norm",
         "kv_a_norm", "q_norm_nope", "q_norm_rope", "k_norm_nope", "k_norm_rope")
WEIGHTS = ("w_ada", "b_ada", "norm_ffn1", "w_ffn1_in", "w_ffn1_out", "norm_mix", "w_in",
           "pool_grp", "pool_scale", "w_pool_proj", "q_a_norm", "w_q_up", "kv_a_norm",
           "w_kv_up", "q_norm_nope", "q_norm_rope", "k_norm_nope", "k_norm_rope",
           "w_mla_proj", "w_out", "norm_ffn2", "w_ffn2_in", "w_ffn2_out")


def _params(*sem):
    return pltpu.CompilerParams(dimension_semantics=sem, vmem_limit_bytes=VMEM_LIMIT)


def _tile(n, cands):
    for c in cands:
        if n % c == 0:
            return c
    return n


def _my_pos():
    return lax.axis_index("x"), lax.axis_index("y"), lax.axis_index("c")


def _flip(pos, k):
    x, y, c = pos
    fx, fy, fc = (k >> 2) & 1, (k >> 1) & 1, k & 1
    return ((1 - x) if fx else x, (1 - y) if fy else y, (1 - c) if fc else c)


def _index(pos):
    x, y, c = pos
    return 4 * x + 2 * y + c


def _exchange(arrays, name, scatter=False):
    n = len(arrays)

    def body(*refs):
        ins, outs = refs[:n], refs[n:2 * n]
        send_sems, recv_sems, local_sems = refs[2 * n:]
        me = _my_pos()
        mine, sends = [], []
        for a in range(n):
            own = ins[a].at[_index(me)] if scatter else ins[a]
            cp = pltpu.make_async_copy(own, outs[a].at[_index(me)], local_sems.at[a])
            cp.start()
            mine.append(cp)
        for k in range(1, N_DEV):
            peer = _flip(me, k)
            for a in range(n):
                cp = pltpu.make_async_remote_copy(
                    src_ref=ins[a].at[_index(peer)] if scatter else ins[a],
                    dst_ref=outs[a].at[_index(me)],
                    send_sem=send_sems.at[a, k - 1], recv_sem=recv_sems.at[a, k - 1],
                    device_id=peer, device_id_type=MESH)
                cp.start()
                sends.append(cp)
        for k in range(1, N_DEV):
            peer = _flip(me, k)
            for a in range(n):
                pltpu.make_async_remote_copy(
                    src_ref=ins[a].at[_index(me)] if scatter else ins[a],
                    dst_ref=outs[a].at[_index(peer)],
                    send_sem=send_sems.at[a, k - 1], recv_sem=recv_sems.at[a, k - 1],
                    device_id=peer, device_id_type=MESH).wait_recv()
        for cp in sends:
            cp.wait_send()
        for cp in mine:
            cp.wait()

    shape = lambda x: x.shape if scatter else (N_DEV,) + x.shape
    return pl.pallas_call(
        body, name=name,
        out_shape=tuple(jax.ShapeDtypeStruct(shape(x), x.dtype) for x in arrays),
        in_specs=[pl.BlockSpec(memory_space=pl.ANY)] * n,
        out_specs=tuple(pl.BlockSpec(memory_space=pl.ANY) for _ in arrays),
        scratch_shapes=[pltpu.SemaphoreType.DMA((n, N_DEV - 1)),
                        pltpu.SemaphoreType.DMA((n, N_DEV - 1)),
                        pltpu.SemaphoreType.DMA((n,))],
    )(*arrays)


def _all_gather(x, name):
    return _exchange([x], name)[0]


_HBM = pl.BlockSpec(memory_space=pltpu.HBM)
_SEM = pl.BlockSpec(memory_space=pltpu.SEMAPHORE)
_ANY = pl.BlockSpec(memory_space=pl.ANY)
_EFFECT = pltpu.SideEffectType.DATAFLOW_SIDE_EFFECTING


def _split_copy(ins, lands, send_sems, recv_sems, a, k, me, scatter, incoming):
    peer = _flip(me, k)
    block = me if incoming else peer
    return pltpu.make_async_remote_copy(
        src_ref=ins[a].at[_index(block)] if scatter else ins[a],
        dst_ref=lands[a].at[_index(peer if incoming else me)],
        send_sem=send_sems.at[a * (N_DEV - 1) + k - 1], recv_sem=recv_sems.at[a * (N_DEV - 1) + k - 1],
        device_id=peer, device_id_type=MESH)


ALL_PEERS = tuple(range(1, N_DEV))
CHIP_PEERS = (1, 2, 4, 6)


def _exchange_start_groups(groups, name, scatter=False, after=None, peers=None):
    peers = peers or [ALL_PEERS] * len(groups)
    sizes = [len(g) for g in groups]
    first = [sum(sizes[:i]) for i in range(len(sizes))]
    n, ng = sum(sizes), len(sizes)
    after = jnp.zeros((SUBLANES, LANES), F32) if after is None else after

    def body(*refs):
        ins, lands = refs[:n], refs[n:2 * n]
        sems = refs[2 * n + 1:2 * n + 1 + 2 * ng]
        me = _my_pos()
        for g in range(ng):
            part = slice(first[g], first[g] + sizes[g])
            for k in peers[g]:
                for a in range(sizes[g]):
                    _split_copy(ins[part], lands[part], sems[2 * g], sems[2 * g + 1], a, k, me, scatter, False).start()
        refs[-1][...] = jnp.zeros((SUBLANES, LANES), F32)

    shape = lambda x: x.shape if scatter else (N_DEV,) + x.shape
    hbm = lambda x: pltpu.with_memory_space_constraint(x, pltpu.HBM)
    srcs = [hbm(x) for g in groups for x in g]
    zones = [hbm(lax.empty(shape(x), x.dtype)) for g in groups for x in g]
    sem_shapes = [pltpu.SemaphoreType.DMA((s * (N_DEV - 1),)) for s in sizes for _ in range(2)]
    out = pl.pallas_call(
        body, name=name,
        out_shape=(*sem_shapes, *[pltpu.HBM(x.shape, x.dtype) for x in srcs + zones],
                   jax.ShapeDtypeStruct((SUBLANES, LANES), F32)),
        in_specs=[_HBM] * (2 * n) + [_ANY],
        out_specs=(*[_SEM] * (2 * ng), *[_HBM] * (2 * n), pl.BlockSpec(memory_space=pltpu.VMEM)),
        input_output_aliases={i: 2 * ng + i for i in range(2 * n)},
        compiler_params=pltpu.CompilerParams(has_side_effects=_EFFECT),
    )(*srcs, *zones, after)
    bufs = out[2 * ng:-1]
    handles = [(out[2 * g], out[2 * g + 1], *bufs[first[g]:first[g] + sizes[g]],
                *bufs[n + first[g]:n + first[g] + sizes[g]]) for g in range(ng)]
    return handles, out[-1]


def _exchange_start(arrays, name, scatter=False, after=None):
    handles, token = _exchange_start_groups([arrays], name, scatter, after)
    return handles[0], token


def _exchange_wait(handle, name, scatter=False, after=None, peers=ALL_PEERS):
    send_sems, recv_sems = handle[0], handle[1]
    n = (len(handle) - 2) // 2
    after = jnp.zeros((SUBLANES, LANES), F32) if after is None else after

    def body(*refs):
        ins, lands = refs[:n], refs[n:2 * n]
        send, recv = refs[2 * n], refs[2 * n + 1]
        me = _my_pos()
        for k in peers:
            for a in range(n):
                _split_copy(ins, lands, send, recv, a, k, me, scatter, False).wait_send()
                _split_copy(ins, lands, send, recv, a, k, me, scatter, True).wait_recv()

    bufs = handle[2:]
    out = pl.pallas_call(
        body, name=name,
        out_shape=tuple(pltpu.HBM(x.shape, x.dtype) for x in bufs),
        in_specs=[_HBM] * (2 * n) + [_SEM, _SEM, _ANY],
        out_specs=tuple([_HBM] * (2 * n)),
        input_output_aliases={i: i for i in range(2 * n)},
        compiler_params=pltpu.CompilerParams(has_side_effects=_EFFECT),
    )(*bufs, send_sems, recv_sems, after)
    me = _index(_my_pos())
    landed = []
    for src, land in zip(out[:n], out[n:]):
        own = lax.dynamic_slice_in_dim(src, me, 1, axis=0) if scatter else src[None]
        landed.append(lax.dynamic_update_slice_in_dim(land, own, me, axis=0))
    return landed


def _sibling_forward(x, name):
    flips = [k for k in CHIP_PEERS if k != 1]

    def body(x_ref, o_ref, send_sems, recv_sems):
        me = _my_pos()
        sibling = _flip(me, 1)
        sends = []
        for i, k in enumerate(flips):
            block = o_ref.at[_index(_flip(me, k))]
            cp = pltpu.make_async_remote_copy(src_ref=block, dst_ref=block, send_sem=send_sems.at[i],
                                              recv_sem=recv_sems.at[i], device_id=sibling, device_id_type=MESH)
            cp.start()
            sends.append(cp)
        for i, k in enumerate(flips):
            block = o_ref.at[_index(_flip(sibling, k))]
            pltpu.make_async_remote_copy(src_ref=block, dst_ref=block, send_sem=send_sems.at[i],
                                         recv_sem=recv_sems.at[i], device_id=sibling, device_id_type=MESH).wait_recv()
        for cp in sends:
            cp.wait_send()

    return pl.pallas_call(
        body, name=name, out_shape=jax.ShapeDtypeStruct(x.shape, x.dtype),
        in_specs=[_ANY], out_specs=_ANY, input_output_aliases={0: 0},
        scratch_shapes=[pltpu.SemaphoreType.DMA((len(flips),)), pltpu.SemaphoreType.DMA((len(flips),))],
    )(x)


def _sum_blocks(x, name):
    n, rows, cols = x.shape
    whole = x.size * x.dtype.itemsize <= SUM_WHOLE_BYTES
    tr = rows if whole else _tile(rows, (512, 256, 128, 64, 32, 16, 8))

    def body(x_ref, o_ref):
        acc = x_ref[0].astype(F32)
        for d in range(1, n):
            acc = acc + x_ref[d].astype(F32)
        o_ref[...] = acc

    return pl.pallas_call(
        body, name=name,
        out_shape=jax.ShapeDtypeStruct((rows, cols), F32),
        grid=(rows // tr,),
        in_specs=[pl.BlockSpec((n, tr, cols), lambda i: (0, i, 0))],
        out_specs=pl.BlockSpec((tr, cols), lambda i: (i, 0)),
        compiler_params=_params("parallel"),
    )(x)


_DIMS = {"nn": (((1,), (0,)), ((), ())), "nt": (((1,), (1,)), ((), ())), "tn": (((0,), (0,)), ((), ()))}


def _mm(a, b, mode, name, out_dtype=F32, tm=None, tn=None, add=None, after=None, b_cols=None):
    if mode == "tn":
        kdim, m = a.shape
    else:
        m, kdim = a.shape
    n = b.shape[0] if mode == "nt" else b.shape[1]
    tm = tm or _tile(m, (512, 256, 128))
    tn = tn or _tile(n, (512, 256, 128))
    j0 = 0
    if b_cols is not None:
        j0, n = b_cols[0], b_cols[1] * tn
    dims = _DIMS[mode]

    def body(*refs):
        refs = refs if after is None else refs[1:]
        acc = lax.dot_general(refs[0][...].astype(BF16), refs[1][...].astype(BF16), dims,
                              preferred_element_type=F32)
        if add is not None:
            acc = acc + refs[2][...]
        refs[-1][...] = acc.astype(out_dtype)

    once_a = dict(pipeline_mode=pl.Buffered(1)) if m == tm else {}
    once_b = dict(pipeline_mode=pl.Buffered(1)) if n == tn else {}
    a_spec = (pl.BlockSpec((kdim, tm), lambda i, j: (0, i), **once_a) if mode == "tn"
              else pl.BlockSpec((tm, kdim), lambda i, j: (i, 0), **once_a))
    b_spec = (pl.BlockSpec((tn, kdim), lambda i, j: (j, 0), **once_b) if mode == "nt"
              else pl.BlockSpec((kdim, tn), lambda i, j: (0, j + j0), **once_b))
    o_spec = pl.BlockSpec((tm, tn), lambda i, j: (i, j))
    in_specs, args = [a_spec, b_spec], [a, b]
    if add is not None:
        in_specs.append(o_spec)
        args.append(add)
    if after is not None:
        in_specs.insert(0, _ANY)
        args.insert(0, after)
    return pl.pallas_call(
        body, name=name, out_shape=jax.ShapeDtypeStruct((m, n), out_dtype), grid=(m // tm, n // tn),
        in_specs=in_specs, out_specs=o_spec,
        compiler_params=_params("parallel", "parallel"),
    )(*args)


def _rowmap(name, fn, seq, rows, bats=(), vecs=(), row_outs=(), bat_outs=(), vec_outs=(), ts=None, mm=None, lhs=None,
            after=None, mm_sum=True):
    mms = [] if mm is None else (mm if isinstance(mm, list) else [mm])
    rows = [r if isinstance(r, tuple) else (r, r.shape[1], 0) for r in rows]
    tokens = rows[0][0].shape[0]
    nseq = tokens // seq
    ts = ts or _tile(seq, (512, 256, 128, 64, 32, 16, 8))
    nt = seq // ts
    n_r, n_b, n_v = len(rows), len(bats), len(vecs)
    n_ro, n_bo = len(row_outs), len(bat_outs)

    def accumulate(ref, val, first):
        @pl.when(first)
        def _():
            ref[...] = val.reshape(ref.shape)

        @pl.when(jnp.logical_not(first))
        def _():
            ref[...] += val.reshape(ref.shape)

    def body(*refs):
        n_in = n_r + n_b + n_v + len(mms) + (after is not None)
        ins, outs = refs[:n_in], refs[n_in:]
        b_vals = [r[0] for r in ins[n_r:n_r + n_b]]
        v_vals = [r[...] for r in ins[n_r + n_b:n_r + n_b + n_v]]
        r_vals = [r[...] for r in ins[:n_r]]
        if mms:
            lefts = r_vals[:len(mms)] if lhs is None else [lhs(r_vals)] * len(mms)
            parts = [lax.dot_general(left.astype(BF16), b_ref[...].astype(BF16), _DIMS[mode],
                                     preferred_element_type=F32)
                     for left, b_ref, (_, mode) in zip(lefts, ins[n_r + n_b + n_v:], mms)]
            accs = [functools.reduce(lambda x, y: x + y, parts)] if mm_sum else parts
            r_vals = accs + r_vals[len(mms):] if lhs is None else accs + [lefts[0]] + r_vals
        ro, bo, vo = fn(r_vals, b_vals, v_vals)
        for ref, val in zip(outs[:n_ro], ro):
            ref[...] = val.astype(ref.dtype)
        b, i = pl.program_id(0), pl.program_id(1)
        for ref, val in zip(outs[n_ro:n_ro + n_bo], bo):
            accumulate(ref, val, i == 0)
        for ref, val in zip(outs[n_ro + n_bo:], vo):
            accumulate(ref, val, jnp.logical_and(i == 0, b == 0))

    in_specs = [pl.BlockSpec((ts, w), functools.partial(lambda b, i, cb: (b * nt + i, cb), cb=cb))
                for _, w, cb in rows]
    in_specs += [pl.BlockSpec((1, 1, v.shape[2]), lambda b, i: (b, 0, 0)) for v in bats]
    in_specs += [pl.BlockSpec((1, v.shape[1]), lambda b, i: (0, 0)) for v in vecs]
    extra = [b_arr for b_arr, _ in mms]
    in_specs += [pl.BlockSpec(b_arr.shape, lambda b, i: (0, 0), pipeline_mode=pl.Buffered(1)) for b_arr in extra]
    if after is not None:
        in_specs.append(_ANY)
        extra.append(after)
    out_shape = [jax.ShapeDtypeStruct((tokens, f), dt) for f, dt in row_outs]
    out_specs = [pl.BlockSpec((ts, f), lambda b, i: (b * nt + i, 0)) for f, _ in row_outs]
    out_shape += [jax.ShapeDtypeStruct((nseq, 1, f), F32) for f in bat_outs]
    out_specs += [pl.BlockSpec((1, 1, f), lambda b, i: (b, 0, 0)) for f in bat_outs]
    out_shape += [jax.ShapeDtypeStruct((1, f), F32) for f in vec_outs]
    out_specs += [pl.BlockSpec((1, f), lambda b, i: (0, 0)) for f in vec_outs]
    return pl.pallas_call(
        body, name=name, out_shape=tuple(out_shape), grid=(nseq, nt),
        in_specs=in_specs, out_specs=tuple(out_specs),
        compiler_params=_params("arbitrary", "arbitrary"),
    )(*([r[0] for r in rows] + list(bats) + list(vecs) + extra))


def _colsum(v):
    return jnp.sum(v, axis=0, keepdims=True)


def _rstd(x, width=None):
    width = width or x.shape[-1]
    return lax.rsqrt(jnp.sum(x * x, axis=-1, keepdims=True) * (1.0 / width) + NORM_EPS)


def _norm_bwd(dy, x, r, g, width=None):
    width = width or x.shape[-1]
    xhat = x * r
    dxhat = dy * g
    dx = r * (dxhat - xhat * (jnp.sum(dxhat * xhat, axis=-1, keepdims=True) * (1.0 / width)))
    return dx, dy * xhat


def _sigmoid(x):
    return 0.5 * jnp.tanh(0.5 * x) + 0.5


def _norm_mod(xv, g, sh, sc):
    return xv * _rstd(xv) * g * (1.0 + sc) + sh


def _norm_mod_fwd(x, p, seq, name):
    def fn(rows, bats, vecs):
        return [_norm_mod(rows[0], vecs[0], bats[0], bats[1])], [], []
    return _rowmap(name, fn, seq, [x], [p["shift"], p["scale"]], [p["gamma"]], row_outs=[(D_MODEL, BF16)])[0]


def _norm_mod_bwd(dh, x, dres, p, seq, name, prev=None, after=None):
    products = dh if isinstance(dh, list) else None
    lefts = [l for l, _ in products] if products else [dh]
    def fn(rows, bats, vecs):
        dhv, xv, dr = rows[:3]
        sc, g = bats[0], vecs[0]
        r = _rstd(xv)
        dxn, dg = _norm_bwd(dhv * (1.0 + sc), xv, r, g)
        dx = dr + dxn
        ro, bo = [dx], [_colsum(dhv), _colsum(dhv * (xv * r * g))]
        if prev is not None:
            ro.append(bats[1] * dx)
            bo.append(_colsum(dx * rows[3].astype(F32)))
        return ro, bo, [_colsum(dg)]
    more = prev is not None
    return _rowmap(name, fn, seq, lefts + [x, dres] + ([prev[0]] if more else []),
                   [p["scale"]] + ([prev[1]] if more else []), [p["gamma"]],
                   row_outs=[(D_MODEL, F32)] + ([(D_MODEL, BF16)] if more else []),
                   bat_outs=[D_MODEL] * (3 if more else 2), vec_outs=[D_MODEL],
                   mm=[(r, "nn") for _, r in products] if products else None, after=after)


def _ffn_in_act(h, wt_in, name):
    tokens = h.shape[0]
    tm, tn = _tile(tokens, (2048, 1024, 512)), 256
    nj = D_FF // tn

    def body(h_ref, wg_ref, wu_ref, g_ref, u_ref, a_ref):
        hv = h_ref[...]
        g = lax.dot_general(hv, wg_ref[...], _DIMS["nt"], preferred_element_type=F32)
        u = lax.dot_general(hv, wu_ref[...], _DIMS["nt"], preferred_element_type=F32)
        g_ref[...] = g.astype(BF16)
        u_ref[...] = u.astype(BF16)
        a_ref[...] = (g * _sigmoid(g) * u).astype(BF16)

    o_spec = pl.BlockSpec((tm, tn), lambda i, j: (i, j))
    out = jax.ShapeDtypeStruct((tokens, D_FF), BF16)
    return pl.pallas_call(
        body, name=name, grid=(tokens // tm, nj), out_shape=(out, out, out),
        in_specs=[pl.BlockSpec((tm, D_MODEL), lambda i, j: (i, 0)),
                  pl.BlockSpec((tn, D_MODEL), lambda i, j: (j, 0)),
                  pl.BlockSpec((tn, D_MODEL), lambda i, j: (j + nj, 0))],
        out_specs=(o_spec, o_spec, o_spec),
        compiler_params=_params("parallel", "parallel"),
    )(h, wt_in, wt_in)


def _out_residual(a, w_out, res, gate, nxt, seq, name, lhs=None):
    def fn(rows, bats, vecs):
        acc, rv = rows[0], rows[-1]
        x_new = rv + bats[0] * acc
        made = [] if lhs is None else [rows[1]]
        return [x_new, acc, _norm_mod(x_new, vecs[0], bats[1], bats[2])] + made, [], []
    outs = [(D_MODEL, F32), (D_MODEL, BF16), (D_MODEL, BF16)] + ([] if lhs is None else [(D_MODEL, BF16)])
    return _rowmap(name, fn, seq, (a if lhs is not None else [a]) + [res], [gate, nxt["shift"], nxt["scale"]],
                   [nxt["gamma"]], row_outs=outs, ts=_tile(seq, (512, 256, 128)), mm=(w_out, "nn"), lhs=lhs)


def _out_loss(a, w_out, res, gate, target, seq, name):
    def fn(rows, bats, vecs):
        acc, rv, tv = rows
        err = rv + bats[0] * acc - tv
        dy = err * (1.0 / D_MODEL)
        return [dy, bats[0] * dy], [_colsum(dy * acc)], [_colsum(err * err)]
    return _rowmap(name, fn, seq, [a, res, target], [gate], row_outs=[(D_MODEL, F32), (D_MODEL, BF16)],
                   bat_outs=[D_MODEL], vec_outs=[D_MODEL], ts=_tile(seq, (512, 256, 128)), mm=(w_out, "nn"))


def _ffn_bwd_x(df, dres, saved, p, seq, tag, prev=None, early=None, mid=None):
    x, h, g, u, a, w_in, w_out = saved
    first = None if early is None else early(a, df)

    def act_bwd(rows, bats, vecs):
        dav, gv, uv = rows[0], rows[1].astype(F32), rows[2].astype(F32)
        sg = _sigmoid(gv)
        silu = gv * sg
        dg = dav * uv * (sg * (1.0 + gv * (1.0 - sg)))
        return [jnp.concatenate([dg, dav * silu], axis=1)], [], []
    dgu = _rowmap(f"{tag}_bwd_da", act_bwd, seq, [df, g, u], row_outs=[(2 * D_FF, BF16)],
                  ts=_tile(seq, (512, 256, 128)), mm=(w_out, "nt"), after=first)[0]
    operands = (a, df, dgu, h)
    after = None if mid is None else mid(operands)
    return _norm_mod_bwd([(dgu, w_in)], x, dres, p, seq, f"{tag}_bwd_norm", prev, after=after), operands


def _ffn_bwd_wout(a, df, tag):
    return _mm(a, df, "tn", f"{tag}_bwd_wout", out_dtype=BF16, tm=D_FF // 2, tn=D_MODEL)


def _ffn_bwd_win(operands, tag, after=None, half=None):
    _, _, dgu, h = operands
    if half is None:
        return _mm(dgu, h, "tn", f"{tag}_bwd_win", out_dtype=BF16, tm=D_FF // 2, tn=D_MODEL, after=after)
    return _mm(dgu, h, "tn", f"{tag}_bwd_win{half}", out_dtype=BF16, tm=D_FF // 2, tn=D_MODEL // 2, after=after,
               b_cols=(half, 1))


def _shift_rows(v, k, forward):
    n = v.shape[0]
    row = lax.broadcasted_iota(jnp.int32, v.shape, 0)
    if forward:
        return jnp.where(row >= k, pltpu.roll(v, k, 0), 0.0)
    return jnp.where(row < n - k, pltpu.roll(v, n - k, 0), 0.0)


def _window_sums(v, forward):
    out, s, k = [], v, 1
    for _ in range(POOL_GROUPS):
        s = s + _shift_rows(s, k, forward)
        out.append(s)
        k *= 2
    return out


def _by_group(vals, g):
    out = vals[-1]
    for idx in range(len(vals) - 2, -1, -1):
        out = jnp.where(g == idx, vals[idx], out)
    return out


def _inv_count(shape, g):
    t1 = lax.broadcasted_iota(jnp.int32, shape, 0) + 1
    window = _by_group([jnp.int32(2 ** (i + 1)) for i in range(POOL_GROUPS)], g)
    return 1.0 / jnp.minimum(t1, window).astype(F32)


def _pool_fwd(u, grp, scale, seq):
    tokens = u.shape[0]

    def body(u_ref, grp_ref, sc_ref, pooled_ref, pg_ref, ps_ref):
        g = pl.program_id(1)
        uv = u_ref[...]
        sums = _by_group(_window_sums(uv, True), g)
        pooled = (sums * _inv_count(uv.shape, g) - uv).astype(BF16)
        pg = jnp.dot(pooled, grp_ref[0].astype(BF16), preferred_element_type=F32)
        pooled_ref[...] = pooled
        pg_ref[...] = pg
        ps_ref[...] = (pg * sc_ref[...]).astype(BF16)

    blk = pl.BlockSpec((seq, POOL_GROUP_DIM), lambda b, g: (b, g))
    return pl.pallas_call(
        body, name="pool_fwd", grid=(tokens // seq, POOL_GROUPS),
        out_shape=(jax.ShapeDtypeStruct(u.shape, BF16), jax.ShapeDtypeStruct(u.shape, F32),
                   jax.ShapeDtypeStruct(u.shape, BF16)),
        in_specs=[blk, pl.BlockSpec((1, POOL_GROUP_DIM, POOL_GROUP_DIM), lambda b, g: (g, 0, 0)),
                  pl.BlockSpec((1, POOL_GROUP_DIM), lambda b, g: (0, g))],
        out_specs=(blk, blk, blk),
        compiler_params=_params("parallel", "parallel"),
    )(u, grp, scale)


def _pool_bwd(dps, pooled, pg, grp, scale, seq):
    tokens = dps.shape[0]

    def body(dps_ref, pooled_ref, pg_ref, grp_ref, sc_ref, du_ref, dgrp_ref, dsc_ref):
        g, b = pl.program_id(0), pl.program_id(1)
        dpsv = dps_ref[...]
        dpg = (dpsv * sc_ref[...]).astype(BF16)
        dsc = _colsum(dpsv * pg_ref[...])
        dgrp = lax.dot_general(pooled_ref[...], dpg, _DIMS["tn"], preferred_element_type=F32)

        @pl.when(b == 0)
        def _():
            dsc_ref[...] = dsc
            dgrp_ref[0] = dgrp

        @pl.when(b > 0)
        def _():
            dsc_ref[...] += dsc
            dgrp_ref[0] += dgrp

        dpool = lax.dot_general(dpg, grp_ref[0].astype(BF16), _DIMS["nt"], preferred_element_type=F32)
        sums = _by_group(_window_sums(dpool * _inv_count(dpool.shape, g), False), g)
        du_ref[...] = (sums - dpool).astype(BF16)

    blk = pl.BlockSpec((seq, POOL_GROUP_DIM), lambda g, b: (b, g))
    grp_spec = pl.BlockSpec((1, POOL_GROUP_DIM, POOL_GROUP_DIM), lambda g, b: (g, 0, 0))
    vec_spec = pl.BlockSpec((1, POOL_GROUP_DIM), lambda g, b: (0, g))
    return pl.pallas_call(
        body, name="pool_bwd", grid=(POOL_GROUPS, tokens // seq),
        out_shape=(jax.ShapeDtypeStruct(dps.shape, BF16), jax.ShapeDtypeStruct(grp.shape, F32),
                   jax.ShapeDtypeStruct(scale.shape, F32)),
        in_specs=[blk, blk, blk, grp_spec, vec_spec],
        out_specs=(blk, grp_spec, vec_spec),
        compiler_params=_params("arbitrary", "arbitrary"),
    )(dps, pooled, pg, grp, scale)


def _lane(shape):
    return lax.broadcasted_iota(jnp.int32, shape, len(shape) - 1)


def _rot(y):
    lane = _lane(y.shape)
    r = jnp.where(lane < QK_NOPE + QK_ROPE // 2,
                  -pltpu.roll(y, HEAD_SLAB - QK_ROPE // 2, 1), pltpu.roll(y, QK_ROPE // 2, 1))
    return jnp.where(jnp.logical_and(lane >= QK_NOPE, lane < QK_NOPE + QK_ROPE), r, 0.0)


def _part_rstd(x):
    sq = x * x
    nope = _lane(x.shape) < QK_NOPE
    s_nope = jnp.sum(jnp.where(nope, sq, 0.0), axis=-1, keepdims=True)
    s_rope = jnp.sum(sq, axis=-1, keepdims=True) - s_nope
    return jnp.where(nope, lax.rsqrt(s_nope * (1.0 / QK_NOPE) + NORM_EPS),
                     lax.rsqrt(s_rope * (1.0 / QK_ROPE) + NORM_EPS))


def _part_norm_bwd(dy, x, r, g):
    nope = _lane(x.shape) < QK_NOPE
    xhat = x * r
    dxhat = dy * g
    prod = dxhat * xhat
    m_nope = jnp.sum(jnp.where(nope, prod, 0.0), axis=-1, keepdims=True)
    m_rope = jnp.sum(prod, axis=-1, keepdims=True) - m_nope
    mean = jnp.where(nope, m_nope * (1.0 / QK_NOPE), m_rope * (1.0 / QK_ROPE))
    return r * (dxhat - xhat * mean), dy * xhat


def _mixer_in(h, wt_a, wt_p, wt_g, g_q, g_kv, seq):
    def fn(rows, bats, vecs):
        z_a, z_p, z_g = rows[:3]
        q, kv = z_a[:, :Q_LORA], z_a[:, Q_LORA:Q_LORA + KV_LORA]
        return [z_a, z_p, z_g, q * _rstd(q) * vecs[0], kv * _rstd(kv) * vecs[1]], [], []
    return _rowmap("mix_in", fn, seq, [h], vecs=[g_q, g_kv], lhs=lambda rows: rows[0],
                   mm=[(wt_a, "nt"), (wt_p, "nt"), (wt_g, "nt")], mm_sum=False,
                   row_outs=[(wt_a.shape[0], F32), (wt_p.shape[0], F32), (wt_g.shape[0], BF16),
                             (Q_LORA, BF16), (KV_LORA, BF16)])


def _latent_norm_bwd(dqp, wtq_pad, dkv, wt_kv, dkr, z_a, g_q, g_kv, seq):
    def fn(rows, bats, vecs):
        dq, dkv, dkrv, z = rows
        q, kv = z[:, :Q_LORA], z[:, Q_LORA:Q_LORA + KV_LORA]
        dxq, dgq = _norm_bwd(dq, q, _rstd(q), vecs[0])
        dxkv, dgkv = _norm_bwd(dkv, kv, _rstd(kv), vecs[1])
        return [jnp.concatenate([dxq, dxkv, dkrv], axis=1)], [], [_colsum(dgq), _colsum(dgkv)]
    return _rowmap("latent_norm_bwd", fn, seq, [dqp, dkv, dkr, z_a], vecs=[g_q, g_kv],
                   row_outs=[(Q_LORA + KV_LORA + HEAD_SLAB, BF16)], vec_outs=[Q_LORA, KV_LORA],
                   mm=[(wtq_pad, "nn"), (wt_kv, "nn")], mm_sum=False)


def _qk_prep_fwd(qn, wtq_pad, kvn, wt_kv, z_a, pos, g_q, g_kn, g_kr, inv_freq, seq):
    def fn(rows, bats, vecs):
        qv, kvv, kr, p = rows
        gq, gkn, gkr, invf = vecs
        ang = p * invf
        cos, sin = jnp.cos(ang), jnp.sin(ang)
        nope = _lane(kr.shape) < QK_NOPE
        krn = kr * _rstd(kr, QK_ROPE) * gkr
        krr = krn * cos + _rot(krn) * sin
        qs, ks, vs = [], [], []
        for h in range(N_HEADS):
            xq = qv[:, h * HEAD_SLAB:(h + 1) * HEAD_SLAB]
            y = xq * _part_rstd(xq) * gq
            qs.append(y * cos + _rot(y) * sin)
            xk = kvv[:, h * HEAD_SLAB:(h + 1) * HEAD_SLAB]
            kn = jnp.where(nope, xk, 0.0)
            ks.append(jnp.where(nope, kn * _rstd(kn, QK_NOPE) * gkn, krr))
            vs.append(jnp.where(nope, 0.0, xk))
        return [jnp.concatenate(v, axis=1) for v in (qs, ks, vs)] + [qv, kvv], [], []
    width = N_HEADS * HEAD_SLAB
    return _rowmap("qk_prep", fn, seq, [qn, kvn, (z_a, HEAD_SLAB, 5), pos], vecs=[g_q, g_kn, g_kr, inv_freq],
                   row_outs=[(width, BF16)] * 3 + [(width, F32)] * 2, mm=[(wtq_pad, "nt"), (wt_kv, "nt")],
                   mm_sum=False)


def _qk_prep_bwd(dqc, dkc, dvp, qp, kv, z_a, pos, g_q, g_kn, g_kr, inv_freq, seq):
    def fn(rows, bats, vecs):
        dq, dk, dv = (r.astype(F32) for r in rows[:3])
        qv, kvv, kr, p = rows[3:]
        gq, gkn, gkr, invf = vecs
        ang = p * invf
        cos, sin = jnp.cos(ang), jnp.sin(ang)
        nope = _lane(kr.shape) < QK_NOPE
        dqs, dkvs = [], []
        dgq = jnp.zeros((1, HEAD_SLAB), F32)
        dgkn = jnp.zeros((1, HEAD_SLAB), F32)
        dkrr = jnp.zeros(kr.shape, F32)
        for h in range(N_HEADS):
            sl = slice(h * HEAD_SLAB, (h + 1) * HEAD_SLAB)
            dyr = dq[:, sl]
            dy = dyr * cos - _rot(dyr * sin)
            xq = qv[:, sl]
            dx, dg = _part_norm_bwd(dy, xq, _part_rstd(xq), gq)
            dqs.append(dx)
            dgq = dgq + _colsum(dg)
            dkh = dk[:, sl]
            dkrr = dkrr + jnp.where(nope, 0.0, dkh)
            kn = jnp.where(nope, kvv[:, sl], 0.0)
            dxk, dgk = _norm_bwd(jnp.where(nope, dkh, 0.0), kn, _rstd(kn, QK_NOPE), gkn, QK_NOPE)
            dgkn = dgkn + _colsum(dgk)
            dkvs.append(jnp.where(nope, dxk, dv[:, sl]))
        dkrn = dkrr * cos - _rot(dkrr * sin)
        dkr, dgkr = _norm_bwd(dkrn, kr, _rstd(kr, QK_ROPE), gkr, QK_ROPE)
        return ([jnp.concatenate(dqs, axis=1), jnp.concatenate(dkvs, axis=1), dkr], [],
                [dgq, dgkn, _colsum(dgkr)])
    width = N_HEADS * HEAD_SLAB
    return _rowmap("qk_prep_bwd", fn, seq, [dqc, dkc, dvp, qp, kv, (z_a, HEAD_SLAB, 5), pos],
                   vecs=[g_q, g_kn, g_kr, inv_freq],
                   row_outs=[(width, BF16), (width, BF16), (HEAD_SLAB, F32)],
                   vec_outs=[HEAD_SLAB] * 3, ts=_tile(seq, (512, 256, 128, 64, 32, 16, 8)))


def _scores(q, k_ref, keys, tq):
    s = lax.dot_general(q, k_ref[0:keys, :], _DIMS["nt"], preferred_element_type=F32) * ATTN_SCALE
    row = lax.broadcasted_iota(jnp.int32, (tq, tq), 0)
    col = lax.broadcasted_iota(jnp.int32, (tq, tq), 1)
    diag = jnp.where(col <= row, s[:, keys - tq:], -1e30)
    return diag if keys == tq else jnp.concatenate([s[:, :keys - tq], diag], axis=1)


def _attn_fwd(qc, kc, vp, seq):
    tokens = qc.shape[0]
    tq = _tile(seq, (256, 128))
    nq = seq // tq

    def body(q_ref, k_ref, v_ref, o_ref, lse_ref):
        for i in range(nq):
            rows, keys = slice(i * tq, (i + 1) * tq), (i + 1) * tq
            s = _scores(q_ref[rows, :], k_ref, keys, tq)
            m = jnp.max(s, axis=-1, keepdims=True)
            p = jnp.exp(s - m)
            l = jnp.sum(p, axis=-1, keepdims=True)
            acc = jnp.dot(p.astype(BF16), v_ref[0:keys, :], preferred_element_type=F32)
            o_ref[rows, :] = (acc / l).astype(BF16)
            lse_ref[rows, :] = jnp.broadcast_to(m + jnp.log(l), (tq, HEAD_SLAB))

    spec = pl.BlockSpec((seq, HEAD_SLAB), lambda b, h: (b, h))
    return pl.pallas_call(
        body, name="attn_fwd", grid=(tokens // seq, N_HEADS),
        out_shape=(jax.ShapeDtypeStruct(qc.shape, BF16), jax.ShapeDtypeStruct(qc.shape, F32)),
        in_specs=[spec] * 3, out_specs=(spec, spec),
        compiler_params=_params("parallel", "parallel"),
    )(qc, kc, vp)


def _attn_bwd(qc, kc, vp, o, lse, do, seq):
    tokens = qc.shape[0]
    tq = _tile(seq, (256, 128))
    nq = seq // tq

    def body(q_ref, k_ref, v_ref, o_ref, lse_ref, do_ref, dq_ref, dk_ref, dv_ref, dk_acc, dv_acc):
        dk_acc[...] = jnp.zeros(dk_acc.shape, F32)
        dv_acc[...] = jnp.zeros(dv_acc.shape, F32)
        for i in range(nq):
            rows, keys = slice(i * tq, (i + 1) * tq), (i + 1) * tq
            q, dov = q_ref[rows, :], do_ref[rows, :]
            delta = jnp.sum(dov.astype(F32) * o_ref[rows, :].astype(F32), axis=-1, keepdims=True)
            s = _scores(q, k_ref, keys, tq)
            p = jnp.exp(s - jnp.tile(lse_ref[rows, :], (1, keys // HEAD_SLAB)))
            dp = lax.dot_general(dov, v_ref[0:keys, :], _DIMS["nt"], preferred_element_type=F32)
            ds = (p * (dp - delta) * ATTN_SCALE).astype(BF16)
            dq_ref[rows, :] = jnp.dot(ds, k_ref[0:keys, :], preferred_element_type=F32).astype(BF16)
            dk_acc[0:keys, :] += lax.dot_general(ds, q, _DIMS["tn"], preferred_element_type=F32)
            dv_acc[0:keys, :] += lax.dot_general(p.astype(BF16), dov, _DIMS["tn"], preferred_element_type=F32)
        dk_ref[...] = dk_acc[...].astype(BF16)
        dv_ref[...] = dv_acc[...].astype(BF16)

    spec = pl.BlockSpec((seq, HEAD_SLAB), lambda b, h: (b, h))
    out = jax.ShapeDtypeStruct(qc.shape, BF16)
    return pl.pallas_call(
        body, name="attn_bwd", grid=(tokens // seq, N_HEADS),
        out_shape=(out, out, out), in_specs=[spec] * 6, out_specs=(spec, spec, spec),
        scratch_shapes=[pltpu.VMEM((seq, HEAD_SLAB), F32), pltpu.VMEM((seq, HEAD_SLAB), F32)],
        compiler_params=_params("parallel", "parallel"),
    )(qc, kc, vp, o, lse, do)


def _adamw(w, g, m, v, name):
    rows, cols = w.shape
    whole = rows * cols * 4 <= ADAMW_WHOLE_BYTES
    tr = rows if whole else _tile(rows, (256, 128, 64, 32, 16, 8))
    c1 = 1.0 - ADAM_B1 ** ADAM_STEP
    c2 = 1.0 - ADAM_B2 ** ADAM_STEP

    def body(w_ref, g_ref, m_ref, v_ref, d_ref, nm_ref, nv_ref):
        gv = g_ref[...]
        nm = ADAM_B1 * m_ref[...] + (1.0 - ADAM_B1) * gv
        nv = ADAM_B2 * v_ref[...] + (1.0 - ADAM_B2) * (gv * gv)
        d_ref[...] = -ADAM_LR * ((nm / c1) / (jnp.sqrt(nv / c2) + ADAM_EPS) + ADAM_WD * w_ref[...])
        nm_ref[...] = nm
        nv_ref[...] = nv

    spec = pl.BlockSpec((tr, cols), lambda i: (i, 0))
    out = jax.ShapeDtypeStruct(w.shape, F32)
    return pl.pallas_call(
        body, name=name, grid=(rows // tr,), out_shape=(out, out, out),
        in_specs=[spec] * 4, out_specs=(spec, spec, spec),
        compiler_params=_params("parallel"),
    )(w, g, m, v)


def _adamw_landed(w, landed, m, v, name):
    rows, cols = w.shape
    tr = _tile(rows, (176, 128, 96, 64, 32, 16, 8))
    c1 = 1.0 - ADAM_B1 ** ADAM_STEP
    c2 = 1.0 - ADAM_B2 ** ADAM_STEP
    n_parts = len(landed)

    def body(*refs):
        w_ref, m_ref, v_ref = refs[:3]
        g_ref, d_ref, nm_ref, nv_ref = refs[3 + n_parts:]
        parts = []
        for x_ref in refs[3:3 + n_parts]:
            acc = x_ref[0].astype(F32)
            for d in range(1, N_DEV):
                acc = acc + x_ref[d].astype(F32)
            parts.append(acc)
        gv = parts[0] if n_parts == 1 else jnp.concatenate(parts, axis=1)
        nm = ADAM_B1 * m_ref[...] + (1.0 - ADAM_B1) * gv
        nv = ADAM_B2 * v_ref[...] + (1.0 - ADAM_B2) * (gv * gv)
        g_ref[...] = gv
        d_ref[...] = -ADAM_LR * ((nm / c1) / (jnp.sqrt(nv / c2) + ADAM_EPS) + ADAM_WD * w_ref[...])
        nm_ref[...] = nm
        nv_ref[...] = nv

    spec = pl.BlockSpec((tr, cols), lambda i: (i, 0))
    out = jax.ShapeDtypeStruct(w.shape, F32)
    return pl.pallas_call(
        body, name=name, grid=(rows // tr,), out_shape=(out, out, out, out),
        in_specs=[spec] * 3 + [pl.BlockSpec((N_DEV, tr, x.shape[2]), lambda i: (0, i, 0)) for x in landed],
        out_specs=(spec, spec, spec, spec),
        compiler_params=_params("parallel"),
    )(w, m, v, *landed)


def _mod_cols(c_all, w_ada, b_cols):
    def body(c_ref, w_ref, b_ref, act_ref, mod_ref):
        cv = c_ref[...]
        act = cv * _sigmoid(cv)
        act_ref[...] = act
        mod_ref[...] = jnp.dot(act.astype(BF16), w_ref[...].astype(BF16),
                               preferred_element_type=F32) + b_ref[...]

    n = w_ada.shape[1]
    return pl.pallas_call(
        body, name="mod_cols",
        out_shape=(jax.ShapeDtypeStruct(c_all.shape, F32), jax.ShapeDtypeStruct((c_all.shape[0], n), F32)),
        compiler_params=pltpu.CompilerParams(vmem_limit_bytes=VMEM_LIMIT),
    )(c_all, w_ada, b_cols)


def _ada_grads(c_act, dmod_all, dmod_cols):
    def body(c_ref, d_ref, dc_ref, gw_ref, gb_ref):
        gw_ref[...] = lax.dot_general(c_ref[...].astype(BF16), dc_ref[...].astype(BF16), _DIMS["tn"],
                                      preferred_element_type=F32)
        gb_ref[...] = _colsum(d_ref[...])

    return pl.pallas_call(
        body, name="ada_grads",
        out_shape=(jax.ShapeDtypeStruct((c_act.shape[1], dmod_cols.shape[1]), F32),
                   jax.ShapeDtypeStruct((1, dmod_all.shape[1]), F32)),
        compiler_params=pltpu.CompilerParams(vmem_limit_bytes=VMEM_LIMIT),
    )(c_act, dmod_all, dmod_cols)


def _flat_rows(a):
    flat = a.reshape(-1)
    pad = (-flat.shape[0]) % (LANES * SUBLANES)
    if pad:
        flat = jnp.pad(flat, (0, pad))
    return flat.reshape(-1, LANES)


def _gather_start(w, groups, tag, after=None, peers=None):
    shards = [[(w[n] if n in ROW_SHARDED else w[n].T).astype(BF16) for n in names] for names in groups]
    return _exchange_start_groups(shards, f"gather_{tag}_start", after=after, peers=peers)


def _gather_wait(handle, names, tag, after, peers=ALL_PEERS):
    landed = _exchange_wait(handle, f"gather_{tag}_wait", after=after, peers=peers)
    if peers == CHIP_PEERS:
        landed = [_sibling_forward(x, f"gather_{tag}_forward{i}") for i, x in enumerate(landed)]
    return {n: g.reshape(-1, g.shape[2]) for n, g in zip(names, landed)}


def _scatter_start(grads, names, tag, after=None):
    blocks = [grads[n].reshape(N_DEV, -1, grads[n].shape[1]) for n in names]
    return _exchange_start(blocks, f"scatter_{tag}_start", scatter=True, after=after)


def _scatter_wait(handle, names, tag, after):
    landed = _exchange_wait(handle, f"scatter_{tag}_wait", scatter=True, after=after)
    return {n: [x] for n, x in zip(names, landed)}


def _pack_small(vals):
    return jnp.concatenate([_flat_rows(v.astype(F32)) for v in vals], axis=0)


def _unpack_small(packed, like):
    out, row = [], 0
    for v in like:
        rows = _flat_rows(v).shape[0]
        out.append(packed[row:row + rows].reshape(-1)[:v.size].reshape(v.shape))
        row += rows
    return out


def _lanes128(*parts):
    out = jnp.zeros((HEAD_SLAB,), F32)
    for off, v in parts:
        out = lax.dynamic_update_slice(out, v.reshape(-1).astype(F32), (off,))
    return out.reshape(1, HEAD_SLAB)


def _step(x, c, positions, w, m, v, loss_target):
    nseq, seq, _ = x.shape
    tokens = nseq * seq
    me = _index(_my_pos())
    strip = lambda d: {n: (a[0] if a.ndim > 2 else a) for n, a in d.items()}
    shapes = {n: a.shape for n, a in w.items()}
    w, m, v = strip(w), strip(m), strip(v)

    c_all = _all_gather(c.reshape(-1, LANES), "gather_c").reshape(N_DEV * nseq, D_MODEL)
    n_ada = w["w_ada"].shape[1]
    b_cols = lax.dynamic_slice(w["b_ada"], (0, me * n_ada), (1, n_ada))
    c_act, mod_cols = _mod_cols(c_all, w["w_ada"], b_cols)
    mod_all = _all_gather(mod_cols, "gather_mod")
    mod = lax.dynamic_slice(mod_all, (0, me * nseq, 0), (N_DEV, nseq, n_ada))
    mod = mod.transpose(1, 0, 2).reshape(nseq, 3, 3, 1, D_MODEL)

    (h_f1i, h_f1o, h_mix_in, h_mix, h_f2), tok = _gather_start(
        w, (("w_ffn1_in",), ("w_ffn1_out",), MIXER[:1], MIXER[1:], ("w_ffn2_in", "w_ffn2_out")), "weights",
        after=mod_all, peers=[CHIP_PEERS] + [ALL_PEERS] * 4)
    started = tok[0:1, 0:1]

    g_q = _lanes128((0, w["q_norm_nope"]), (QK_NOPE, w["q_norm_rope"]))
    g_kn = _lanes128((0, w["k_norm_nope"]))
    g_kr = _lanes128((QK_NOPE, w["k_norm_rope"]))
    freq = ROPE_THETA ** (-jnp.arange(0, QK_ROPE, 2, dtype=F32) / QK_ROPE)
    inv_freq = _lanes128((QK_NOPE, jnp.concatenate([freq, freq])))
    pos = positions.reshape(tokens, 1).astype(F32)

    def sub(k, gamma, coef):
        return dict(gamma=w[gamma], shift=mod[:, k, 0] + started, scale=mod[:, k, 1], gate=coef * mod[:, k, 2])
    p1, pm, p2 = sub(0, "norm_ffn1", 0.5), sub(1, "norm_mix", 1.0), sub(2, "norm_ffn2", 0.5)
    t_big = _tile(tokens, (2048, 1024, 512))
    t_mid = _tile(tokens, (1024, 512))

    x0 = x.reshape(tokens, D_MODEL)
    h1 = _norm_mod_fwd(x0, p1, seq, "ffn1_norm")
    wt_f1i = _gather_wait(h_f1i, ("w_ffn1_in",), "ffn1_in", h1, peers=CHIP_PEERS)["w_ffn1_in"]
    g1, u1, a1 = _ffn_in_act(h1, wt_f1i, "ffn1_in")
    w_f1o = _gather_wait(h_f1o, ("w_ffn1_out",), "ffn1_out", a1)["w_ffn1_out"]
    x1, f1, h2 = _out_residual(a1, w_f1o, x0, p1["gate"], pm, seq, "ffn1_out")
    saved1 = (x0, h1, g1, u1, a1, wt_f1i, w_f1o)

    wt_in = _gather_wait(h_mix_in, MIXER[:1], "mix_in", h2)["w_in"]
    zero_rows = lambda rows: jnp.zeros((rows, D_MODEL), BF16)
    wt_p = wt_in[:512]
    wt_a = jnp.concatenate([wt_in[512:1152], zero_rows(QK_NOPE), wt_in[1152:1184], zero_rows(32)], axis=0)
    wt_g = wt_in[1184:]
    z_a, z_p, z_g, qn, kvn = _mixer_in(h2, wt_a, wt_p, wt_g, w["q_a_norm"], w["kv_a_norm"], seq)

    full = _gather_wait(h_mix, MIXER[1:], "mix", z_g)
    wtq_pad = jnp.pad(full["w_q_up"].reshape(N_HEADS, 96, Q_LORA), ((0, 0), (0, 32), (0, 0))).reshape(-1, Q_LORA)
    wtmla_pad = jnp.pad(full["w_mla_proj"].reshape(D_MODEL, N_HEADS, 64), ((0, 0), (0, 0), (64, 0))).reshape(D_MODEL, -1)
    wt_pool, wt_kv, w_mix_out = full["w_pool_proj"], full["w_kv_up"], full["w_out"]
    pooled, pg, ps = _pool_fwd(z_p, w["pool_grp"], w["pool_scale"], seq)
    br_pool = _mm(ps, wt_pool, "nt", "pool_proj", out_dtype=BF16, tm=t_big, tn=D_MODEL)
    qc, kc, vp, qp, kv = _qk_prep_fwd(qn, wtq_pad, kvn, wt_kv, z_a, pos, g_q, g_kn, g_kr, inv_freq, seq)
    attn, lse = _attn_fwd(qc, kc, vp, seq)
    br_mla = _mm(attn, wtmla_pad, "nt", "mla_proj", out_dtype=BF16, tm=t_mid, tn=D_MODEL)

    def merge(rows):
        zg, bp, bm = (r.astype(F32) for r in rows[:3])
        return (_sigmoid(zg[:, :D_MODEL]) * bp + _sigmoid(zg[:, D_MODEL:]) * bm).astype(BF16)
    x2, o_mix, h3, merged = _out_residual([z_g, br_pool, br_mla], w_mix_out, x1, pm["gate"], p2, seq, "mix_out",
                                          lhs=merge)

    ffn2_w = _gather_wait(h_f2, ("w_ffn2_in", "w_ffn2_out"), "ffn2", h3)
    g2, u2, a2 = _ffn_in_act(h3, ffn2_w["w_ffn2_in"], "ffn2_in")
    dy, df2, dgate2, sq_err = _out_loss(a2, ffn2_w["w_ffn2_out"], x2, p2["gate"],
                                        loss_target.reshape(tokens, D_MODEL), seq, "ffn2_out")
    saved2 = (x2, h3, g2, u2, a2, ffn2_w["w_ffn2_in"], ffn2_w["w_ffn2_out"])

    grads = {}
    (dx2, do_mix, dsh2, dsc2, dgate_m, dg_ffn2), ops2 = _ffn_bwd_x(df2, dy, saved2, p2, seq, "ffn2", (o_mix, pm["gate"]))
    grads["w_ffn2_out"], grads["w_ffn2_in"] = _ffn_bwd_wout(ops2[0], ops2[1], "ffn2"), _ffn_bwd_win(ops2, "ffn2")
    s_f2, tok = _scatter_start(grads, ("w_ffn2_in", "w_ffn2_out"), "ffn2")

    grads["w_out"] = _mm(merged, do_mix, "tn", "mix_bwd_wout", out_dtype=BF16, tm=512, tn=D_MODEL)

    def merge_bwd(rows, bats, vecs):
        dmv, zg, bp, bm = (r.astype(F32) for r in rows)
        s_p, s_m = _sigmoid(zg[:, :D_MODEL]), _sigmoid(zg[:, D_MODEL:])
        dzg = jnp.concatenate([dmv * bp * s_p * (1.0 - s_p), dmv * bm * s_m * (1.0 - s_m)], axis=1)
        return [dmv * s_p, dmv * s_m, dzg], [], []
    dbr_pool, dbr_mla, dz_g = _rowmap("mix_bwd_dmerged", merge_bwd, seq, [do_mix, z_g, br_pool, br_mla],
                                      row_outs=[(D_MODEL, BF16), (D_MODEL, BF16), (2 * D_MODEL, BF16)],
                                      mm=(w_mix_out, "nt"))

    grads["w_pool_proj"] = _mm(dbr_pool, ps, "tn", "pool_bwd_wproj", out_dtype=BF16, tm=512, tn=POOL_WIDTH)
    dps = _mm(dbr_pool, wt_pool, "nn", "pool_bwd_dps", tm=t_big, tn=POOL_WIDTH)
    dz_p, dgrp, dpool_scale = _pool_bwd(dps, pooled, pg, w["pool_grp"], w["pool_scale"] + tok[0:1, 0:1], seq)

    dwtmla_pad = _mm(dbr_mla, attn, "tn", "mla_bwd_wproj", out_dtype=BF16, tm=512, tn=D_MODEL)
    grads["w_mla_proj"] = dwtmla_pad.reshape(D_MODEL, N_HEADS, HEAD_SLAB)[:, :, 64:].reshape(D_MODEL, -1)
    d_attn = _mm(dbr_mla, wtmla_pad, "nn", "mla_bwd_dattn", out_dtype=BF16, tm=t_mid, tn=D_MODEL)
    dqc, dkc, dvp = _attn_bwd(qc, kc, vp, attn, lse, d_attn, seq)
    dqp, dkv, dkr, dg_q, dg_kn, dg_kr = _qk_prep_bwd(dqc, dkc, dvp, qp, kv, z_a, pos, g_q, g_kn, g_kr, inv_freq, seq)
    dwtq_pad = _mm(dqp, qn, "tn", "q_up_bwd_w", out_dtype=BF16, tm=512, tn=Q_LORA)
    grads["w_q_up"] = dwtq_pad.reshape(N_HEADS, HEAD_SLAB, Q_LORA)[:, :96].reshape(-1, Q_LORA)
    grads["w_kv_up"] = _mm(dkv, kvn, "tn", "kv_up_bwd_w", out_dtype=BF16, tm=512, tn=KV_LORA)
    dz_a, dg_qa, dg_kva = _latent_norm_bwd(dqp, wtq_pad, dkv, wt_kv, dkr, z_a, w["q_a_norm"], w["kv_a_norm"], seq)

    dwt_a = _mm(dz_a, h2, "tn", "mix_in_bwd_wa", out_dtype=BF16, tm=256, tn=D_MODEL)
    dwt_p = _mm(dz_p, h2, "tn", "mix_in_bwd_wp", out_dtype=BF16, tm=512, tn=D_MODEL)
    dwt_g = _mm(dz_g, h2, "tn", "mix_in_bwd_wg", out_dtype=BF16, tm=512, tn=D_MODEL)
    grads["w_in"] = jnp.concatenate([dwt_p, dwt_a[:640], dwt_a[704:736], dwt_g], axis=0)

    small_early = [dg_ffn2.reshape(w["norm_ffn2"].shape), dgrp, dpool_scale, dg_qa, dg_kva, dg_q[:, :QK_NOPE],
                   dg_q[:, QK_NOPE:QK_NOPE + QK_ROPE], dg_kn[:, :QK_NOPE], dg_kr[:, QK_NOPE:QK_NOPE + QK_ROPE]]
    s_small, tok = _exchange_start([_pack_small(small_early)], "gather_small_start")
    s_mix, tok = _scatter_start(grads, MIXER, "mix", after=tok)
    dh2 = [(dz_a, wt_a), (dz_p, wt_p), (dz_g, wt_g)]
    pm_tied = dict(pm, scale=pm["scale"] + tok[0:1, 0:1])
    dx1, df1, dsh_m, dsc_m, dgate1, dg_mix = _norm_mod_bwd(dh2, x1, dx2, pm_tied, seq, "mix_bwd_norm", (f1, p1["gate"]))

    handles = {}

    def ffn1_early(a, df):
        grads["w_ffn1_out"] = _ffn_bwd_wout(a, df, "ffn1")
        handles["f1o"], token = _scatter_start(grads, ("w_ffn1_out",), "ffn1_out")
        return token

    def ffn1_mid(operands):
        first = _ffn_bwd_win(operands, "ffn1", half=0)
        handles["f1i0"], token = _exchange_start([first.reshape(N_DEV, -1, first.shape[1])],
                                                 "scatter_ffn1_in0_start", scatter=True)
        return token

    (dx0, dsh1, dsc1, dg_ffn1), ops1 = _ffn_bwd_x(df1, dx1, saved1, p1, seq, "ffn1", early=ffn1_early,
                                                     mid=ffn1_mid)
    s_f1o = handles["f1o"]

    dmod = jnp.stack([jnp.stack([dsh1, dsc1, 0.5 * dgate1], axis=1),
                      jnp.stack([dsh_m, dsc_m, dgate_m], axis=1),
                      jnp.stack([dsh2, dsc2, 0.5 * dgate2], axis=1)], axis=1)
    n_dmod = nseq * 9 * D_MODEL // LANES
    tail = _all_gather(jnp.concatenate([dmod.reshape(-1, LANES), _flat_rows(dg_ffn1), _flat_rows(dg_mix),
                                        _flat_rows(sq_err)], axis=0), "gather_dmod")
    dmod_all = tail[:, :n_dmod].reshape(N_DEV * nseq, 9 * D_MODEL)

    second = _ffn_bwd_win(ops1, "ffn1", after=tail, half=1)
    s_second, tok = _exchange_start([second.reshape(N_DEV, -1, second.shape[1])], "scatter_ffn1_in1_start",
                                    scatter=True, after=tail)
    s_f1i = (handles["f1i0"], s_second)

    dmod_cols = lax.dynamic_slice(dmod_all, (0, me * n_ada), (N_DEV * nseq, n_ada)) + tok[0:1, 0:1]
    g_w_ada, g_b_ada = _ada_grads(c_act, dmod_all, dmod_cols)
    tail_sum = _sum_blocks(tail[:, n_dmod:], "sum_tail")
    g_norm_ffn1 = tail_sum[:SUBLANES].reshape(1, D_MODEL)
    g_norm_mix = tail_sum[SUBLANES:2 * SUBLANES].reshape(1, D_MODEL)
    loss = 0.5 * jnp.sum(tail_sum[2 * SUBLANES:]) * (1.0 / D_MODEL)
    small_all = _exchange_wait(s_small, "gather_small_wait", after=g_b_ada)[0]
    small_sum = _sum_blocks(small_all, "sum_small")
    small = dict(zip(SMALL[2:], _unpack_small(small_sum, [w[n] for n in SMALL[2:]])))
    grad_w = dict(small, w_ada=g_w_ada, b_ada=g_b_ada, norm_ffn1=g_norm_ffn1, norm_mix=g_norm_mix)

    delta, new_m, new_v = {}, {}, {}

    def update(names, landed=None):
        for n in names:
            if landed is None:
                delta[n], new_m[n], new_v[n] = _adamw(w[n], grad_w[n], m[n], v[n], f"adamw_{n}")
            elif n in KEPT_TRANSPOSED:
                res = _adamw_landed(w[n].T, landed[n], m[n].T, v[n].T, f"adamw_{n}")
                grad_w[n], delta[n], new_m[n], new_v[n] = (r.T for r in res)
            elif n in ROW_SHARDED:
                grad_w[n], delta[n], new_m[n], new_v[n] = _adamw_landed(w[n], landed[n], m[n], v[n], f"adamw_{n}")
            else:
                grad_w[n] = _sum_blocks(landed[n][0], f"sum_{n}").T
                delta[n], new_m[n], new_v[n] = _adamw(w[n], grad_w[n], m[n], v[n], f"adamw_{n}")

    update(("w_ada",))
    rep = ("b_ada",) + SMALL
    d_s, m_s, v_s = _adamw(_pack_small([w[n] for n in rep]), _pack_small([grad_w[n] for n in rep]),
                           _pack_small([m[n] for n in rep]), _pack_small([v[n] for n in rep]), "adamw_small")
    like = [w[n] for n in rep]
    for dst, packed in ((delta, d_s), (new_m, m_s), (new_v, v_s)):
        dst.update(zip(rep, _unpack_small(packed, like)))
    update(("w_ffn2_in", "w_ffn2_out"), _scatter_wait(s_f2, ("w_ffn2_in", "w_ffn2_out"), "ffn2", after=d_s))
    update(MIXER, _scatter_wait(s_mix, MIXER, "mix", after=delta["w_ffn2_out"]))
    update(("w_ffn1_out",), _scatter_wait(s_f1o, ("w_ffn1_out",), "ffn1_out", after=delta["w_out"]))
    halves = [_exchange_wait(h, f"scatter_ffn1_in{i}_wait", scatter=True, after=delta["w_ffn1_out"])[0]
              for i, h in enumerate(s_f1i)]
    update(("w_ffn1_in",), {"w_ffn1_in": halves})

    lead = lambda d: [d[n].reshape(shapes[n]) for n in WEIGHTS]
    return (loss, dx0.reshape(x.shape), *lead(grad_w), *lead(delta), *lead(new_m), *lead(new_v))


def kernel(x, c, positions, w_ada, b_ada, norm_ffn1, w_ffn1_in, w_ffn1_out, norm_mix, w_in, pool_grp, pool_scale, w_pool_proj, q_a_norm, w_q_up, kv_a_norm, w_kv_up, q_norm_nope, q_norm_rope, k_norm_nope, k_norm_rope, w_mla_proj, w_out, norm_ffn2, w_ffn2_in, w_ffn2_out, loss_target, m_w_ada, m_b_ada, m_norm_ffn1, m_w_ffn1_in, m_w_ffn1_out, m_norm_mix, m_w_in, m_pool_grp, m_pool_scale, m_w_pool_proj, m_q_a_norm, m_w_q_up, m_kv_a_norm, m_w_kv_up, m_q_norm_nope, m_q_norm_rope, m_k_norm_nope, m_k_norm_rope, m_w_mla_proj, m_w_out, m_norm_ffn2, m_w_ffn2_in, m_w_ffn2_out, v_w_ada, v_b_ada, v_norm_ffn1, v_w_ffn1_in, v_w_ffn1_out, v_norm_mix, v_w_in, v_pool_grp, v_pool_scale, v_w_pool_proj, v_q_a_norm, v_w_q_up, v_kv_a_norm, v_w_kv_up, v_q_norm_nope, v_q_norm_rope, v_k_norm_nope, v_k_norm_rope, v_w_mla_proj, v_w_out, v_norm_ffn2, v_w_ffn2_in, v_w_ffn2_out):
    w = dict(w_ada=w_ada, b_ada=b_ada, norm_ffn1=norm_ffn1, w_ffn1_in=w_ffn1_in, w_ffn1_out=w_ffn1_out, norm_mix=norm_mix, w_in=w_in, pool_grp=pool_grp, pool_scale=pool_scale, w_pool_proj=w_pool_proj, q_a_norm=q_a_norm, w_q_up=w_q_up, kv_a_norm=kv_a_norm, w_kv_up=w_kv_up, q_norm_nope=q_norm_nope, q_norm_rope=q_norm_rope, k_norm_nope=k_norm_nope, k_norm_rope=k_norm_rope, w_mla_proj=w_mla_proj, w_out=w_out, norm_ffn2=norm_ffn2, w_ffn2_in=w_ffn2_in, w_ffn2_out=w_ffn2_out)
    m = dict(w_ada=m_w_ada, b_ada=m_b_ada, norm_ffn1=m_norm_ffn1, w_ffn1_in=m_w_ffn1_in, w_ffn1_out=m_w_ffn1_out, norm_mix=m_norm_mix, w_in=m_w_in, pool_grp=m_pool_grp, pool_scale=m_pool_scale, w_pool_proj=m_w_pool_proj, q_a_norm=m_q_a_norm, w_q_up=m_w_q_up, kv_a_norm=m_kv_a_norm, w_kv_up=m_w_kv_up, q_norm_nope=m_q_norm_nope, q_norm_rope=m_q_norm_rope, k_norm_nope=m_k_norm_nope, k_norm_rope=m_k_norm_rope, w_mla_proj=m_w_mla_proj, w_out=m_w_out, norm_ffn2=m_norm_ffn2, w_ffn2_in=m_w_ffn2_in, w_ffn2_out=m_w_ffn2_out)
    v = dict(w_ada=v_w_ada, b_ada=v_b_ada, norm_ffn1=v_norm_ffn1, w_ffn1_in=v_w_ffn1_in, w_ffn1_out=v_w_ffn1_out, norm_mix=v_norm_mix, w_in=v_w_in, pool_grp=v_pool_grp, pool_scale=v_pool_scale, w_pool_proj=v_w_pool_proj, q_a_norm=v_q_a_norm, w_q_up=v_w_q_up, kv_a_norm=v_kv_a_norm, w_kv_up=v_w_kv_up, q_norm_nope=v_q_norm_nope, q_norm_rope=v_q_norm_rope, k_norm_nope=v_k_norm_nope, k_norm_rope=v_k_norm_rope, w_mla_proj=v_w_mla_proj, w_out=v_w_out, norm_ffn2=v_norm_ffn2, w_ffn2_in=v_w_ffn2_in, w_ffn2_out=v_w_ffn2_out)
    return _step(x, c, positions, w, m, v, loss_target)
```

```python
import functools
import math

import jax
import jax.numpy as jnp
from jax import lax
from jax.experimental import pallas as pl
from jax.experimental.pallas import tpu as pltpu

F32 = jnp.float32
BF16 = jnp.bfloat16
MESH = pl.DeviceIdType.MESH
AXES = ("x", "y", "c")
N_DEV = 8

D_MODEL = 1024
D_FF = 2816
N_HEADS = 8
HEAD_SLAB = 128
QK_NOPE = 64
QK_ROPE = 32
POOL_WIDTH = 512
POOL_GROUPS = 4
POOL_GROUP_DIM = 128
Q_LORA = 384
KV_LORA = 256
ROPE_THETA = 10000.0
ATTN_SCALE = 1.0 / math.sqrt(QK_NOPE + QK_ROPE)
NORM_EPS = 1e-6
ADAM_LR, ADAM_B1, ADAM_B2, ADAM_EPS, ADAM_WD, ADAM_STEP = 0.001, 0.9, 0.999, 1e-08, 0.01, 10

LANES = 128
SUBLANES = 8
VMEM_LIMIT = 52 * 1024 * 1024
ADAMW_WHOLE_BYTES = 3 << 19
SUM_WHOLE_BYTES = 4 << 20

BIG = ("w_ffn1_in", "w_ffn1_out", "w_in", "w_pool_proj", "w_q_up", "w_kv_up",
       "w_mla_proj", "w_out", "w_ffn2_in", "w_ffn2_out")
ROW_SHARDED = ("w_ffn1_out", "w_out", "w_ffn2_out")
MIXER = ("w_in", "w_pool_proj", "w_q_up", "w_kv_up", "w_mla_proj", "w_out")
KEPT_TRANSPOSED = ("w_ffn1_in", "w_ffn2_in", "w_in", "w_q_up")
SMALL = ("norm_ffn1", "norm_mix", "norm_ffn2", "pool_grp", "pool_scale", "q_a_norm",
         "kv_a_norm", "q_norm_nope", "q_norm_rope", "k_norm_nope", "k_norm_rope")
WEIGHTS = ("w_ada", "b_ada", "norm_ffn1", "w_ffn1_in", "w_ffn1_out", "norm_mix", "w_in",
           "pool_grp", "pool_scale", "w_pool_proj", "q_a_norm", "w_q_up", "kv_a_norm",
           "w_kv_up", "q_norm_nope", "q_norm_rope", "k_norm_nope", "k_norm_rope",
           "w_mla_proj", "w_out", "norm_ffn2", "w_ffn2_in", "w_ffn2_out")


def _params(*sem):
    return pltpu.CompilerParams(dimension_semantics=sem, vmem_limit_bytes=VMEM_LIMIT)


def _tile(n, cands):
    for c in cands:
        if n % c == 0:
            return c
    return n


def _my_pos():
    return lax.axis_index("x"), lax.axis_index("y"), lax.axis_index("c")


def _flip(pos, k):
    x, y, c = pos
    fx, fy, fc = (k >> 2) & 1, (k >> 1) & 1, k & 1
    return ((1 - x) if fx else x, (1 - y) if fy else y, (1 - c) if fc else c)


def _index(pos):
    x, y, c = pos
    return 4 * x + 2 * y + c


def _exchange(arrays, name, scatter=False):
    n = len(arrays)

    def body(*refs):
        ins, outs = refs[:n], refs[n:2 * n]
        send_sems, recv_sems, local_sems = refs[2 * n:]
        me = _my_pos()
        mine, sends = [], []
        for a in range(n):
            own = ins[a].at[_index(me)] if scatter else ins[a]
            cp = pltpu.make_async_copy(own, outs[a].at[_index(me)], local_sems.at[a])
            cp.start()
            mine.append(cp)
        for k in range(1, N_DEV):
            peer = _flip(me, k)
            for a in range(n):
                cp = pltpu.make_async_remote_copy(
                    src_ref=ins[a].at[_index(peer)] if scatter else ins[a],
                    dst_ref=outs[a].at[_index(me)],
                    send_sem=send_sems.at[a, k - 1], recv_sem=recv_sems.at[a, k - 1],
                    device_id=peer, device_id_type=MESH)
                cp.start()
                sends.append(cp)
        for k in range(1, N_DEV):
            peer = _flip(me, k)
            for a in range(n):
                pltpu.make_async_remote_copy(
                    src_ref=ins[a].at[_index(me)] if scatter else ins[a],
                    dst_ref=outs[a].at[_index(peer)],
                    send_sem=send_sems.at[a, k - 1], recv_sem=recv_sems.at[a, k - 1],
                    device_id=peer, device_id_type=MESH).wait_recv()
        for cp in sends:
            cp.wait_send()
        for cp in mine:
            cp.wait()

    shape = lambda x: x.shape if scatter else (N_DEV,) + x.shape
    return pl.pallas_call(
        body, name=name,
        out_shape=tuple(jax.ShapeDtypeStruct(shape(x), x.dtype) for x in arrays),
        in_specs=[pl.BlockSpec(memory_space=pl.ANY)] * n,
        out_specs=tuple(pl.BlockSpec(memory_space=pl.ANY) for _ in arrays),
        scratch_shapes=[pltpu.SemaphoreType.DMA((n, N_DEV - 1)),
                        pltpu.SemaphoreType.DMA((n, N_DEV - 1)),
                        pltpu.SemaphoreType.DMA((n,))],
    )(*arrays)


def _all_gather(x, name):
    return _exchange([x], name)[0]


_HBM = pl.BlockSpec(memory_space=pltpu.HBM)
_SEM = pl.BlockSpec(memory_space=pltpu.SEMAPHORE)
_ANY = pl.BlockSpec(memory_space=pl.ANY)
_EFFECT = pltpu.SideEffectType.DATAFLOW_SIDE_EFFECTING


def _split_copy(ins, lands, send_sems, recv_sems, a, k, me, scatter, incoming):
    peer = _flip(me, k)
    block = me if incoming else peer
    return pltpu.make_async_remote_copy(
        src_ref=ins[a].at[_index(block)] if scatter else ins[a],
        dst_ref=lands[a].at[_index(peer if incoming else me)],
        send_sem=send_sems.at[a * (N_DEV - 1) + k - 1], recv_sem=recv_sems.at[a * (N_DEV - 1) + k - 1],
        device_id=peer, device_id_type=MESH)


ALL_PEERS = tuple(range(1, N_DEV))
CHIP_PEERS = (1, 2, 4, 6)


def _exchange_start_groups(groups, name, scatter=False, after=None, peers=None):
    peers = peers or [ALL_PEERS] * len(groups)
    sizes = [len(g) for g in groups]
    first = [sum(sizes[:i]) for i in range(len(sizes))]
    n, ng = sum(sizes), len(sizes)
    after = jnp.zeros((SUBLANES, LANES), F32) if after is None else after

    def body(*refs):
        ins, lands = refs[:n], refs[n:2 * n]
        sems = refs[2 * n + 1:2 * n + 1 + 2 * ng]
        me = _my_pos()
        for g in range(ng):
            part = slice(first[g], first[g] + sizes[g])
            for k in peers[g]:
                for a in range(sizes[g]):
                    _split_copy(ins[part], lands[part], sems[2 * g], sems[2 * g + 1], a, k, me, scatter, False).start()
        refs[-1][...] = jnp.zeros((SUBLANES, LANES), F32)

    shape = lambda x: x.shape if scatter else (N_DEV,) + x.shape
    hbm = lambda x: pltpu.with_memory_space_constraint(x, pltpu.HBM)
    srcs = [hbm(x) for g in groups for x in g]
    zones = [hbm(lax.empty(shape(x), x.dtype)) for g in groups for x in g]
    sem_shapes = [pltpu.SemaphoreType.DMA((s * (N_DEV - 1),)) for s in sizes for _ in range(2)]
    out = pl.pallas_call(
        body, name=name,
        out_shape=(*sem_shapes, *[pltpu.HBM(x.shape, x.dtype) for x in srcs + zones],
                   jax.ShapeDtypeStruct((SUBLANES, LANES), F32)),
        in_specs=[_HBM] * (2 * n) + [_ANY],
        out_specs=(*[_SEM] * (2 * ng), *[_HBM] * (2 * n), pl.BlockSpec(memory_space=pltpu.VMEM)),
        input_output_aliases={i: 2 * ng + i for i in range(2 * n)},
        compiler_params=pltpu.CompilerParams(has_side_effects=_EFFECT),
    )(*srcs, *zones, after)
    bufs = out[2 * ng:-1]
    handles = [(out[2 * g], out[2 * g + 1], *bufs[first[g]:first[g] + sizes[g]],
                *bufs[n + first[g]:n + first[g] + sizes[g]]) for g in range(ng)]
    return handles, out[-1]


def _exchange_start(arrays, name, scatter=False, after=None):
    handles, token = _exchange_start_groups([arrays], name, scatter, after)
    return handles[0], token


def _exchange_wait(handle, name, scatter=False, after=None, peers=ALL_PEERS):
    send_sems, recv_sems = handle[0], handle[1]
    n = (len(handle) - 2) // 2
    after = jnp.zeros((SUBLANES, LANES), F32) if after is None else after

    def body(*refs):
        ins, lands = refs[:n], refs[n:2 * n]
        send, recv = refs[2 * n], refs[2 * n + 1]
        me = _my_pos()
        for k in peers:
            for a in range(n):
                _split_copy(ins, lands, send, recv, a, k, me, scatter, False).wait_send()
                _split_copy(ins, lands, send, recv, a, k, me, scatter, True).wait_recv()

    bufs = handle[2:]
    out = pl.pallas_call(
        body, name=name,
        out_shape=tuple(pltpu.HBM(x.shape, x.dtype) for x in bufs),
        in_specs=[_HBM] * (2 * n) + [_SEM, _SEM, _ANY],
        out_specs=tuple([_HBM] * (2 * n)),
        input_output_aliases={i: i for i in range(2 * n)},
        compiler_params=pltpu.CompilerParams(has_side_effects=_EFFECT),
    )(*bufs, send_sems, recv_sems, after)
    me = _index(_my_pos())
    landed = []
    for src, land in zip(out[:n], out[n:]):
        own = lax.dynamic_slice_in_dim(src, me, 1, axis=0) if scatter else src[None]
        landed.append(lax.dynamic_update_slice_in_dim(land, own, me, axis=0))
    return landed


def _sibling_forward(x, name):
    flips = [k for k in CHIP_PEERS if k != 1]

    def body(x_ref, o_ref, send_sems, recv_sems):
        me = _my_pos()
        sibling = _flip(me, 1)
        sends = []
        for i, k in enumerate(flips):
            block = o_ref.at[_index(_flip(me, k))]
            cp = pltpu.make_async_remote_copy(src_ref=block, dst_ref=block, send_sem=send_sems.at[i],
                                              recv_sem=recv_sems.at[i], device_id=sibling, device_id_type=MESH)
            cp.start()
            sends.append(cp)
        for i, k in enumerate(flips):
            block = o_ref.at[_index(_flip(sibling, k))]
            pltpu.make_async_remote_copy(src_ref=block, dst_ref=block, send_sem=send_sems.at[i],
                                         recv_sem=recv_sems.at[i], device_id=sibling, device_id_type=MESH).wait_recv()
        for cp in sends:
            cp.wait_send()

    return pl.pallas_call(
        body, name=name, out_shape=jax.ShapeDtypeStruct(x.shape, x.dtype),
        in_specs=[_ANY], out_specs=_ANY, input_output_aliases={0: 0},
        scratch_shapes=[pltpu.SemaphoreType.DMA((len(flips),)), pltpu.SemaphoreType.DMA((len(flips),))],
    )(x)


def _sum_blocks(x, name):
    n, rows, cols = x.shape
    whole = x.size * x.dtype.itemsize <= SUM_WHOLE_BYTES
    tr = rows if whole else _tile(rows, (512, 256, 128, 64, 32, 16, 8))

    def body(x_ref, o_ref):
        acc = x_ref[0].astype(F32)
        for d in range(1, n):
            acc = acc + x_ref[d].astype(F32)
        o_ref[...] = acc

    return pl.pallas_call(
        body, name=name,
        out_shape=jax.ShapeDtypeStruct((rows, cols), F32),
        grid=(rows // tr,),
        in_specs=[pl.BlockSpec((n, tr, cols), lambda i: (0, i, 0))],
        out_specs=pl.BlockSpec((tr, cols), lambda i: (i, 0)),
        compiler_params=_params("parallel"),
    )(x)


_DIMS = {"nn": (((1,), (0,)), ((), ())), "nt": (((1,), (1,)), ((), ())), "tn": (((0,), (0,)), ((), ()))}


def _mm(a, b, mode, name, out_dtype=F32, tm=None, tn=None, add=None, after=None, b_cols=None):
    if mode == "tn":
        kdim, m = a.shape
    else:
        m, kdim = a.shape
    n = b.shape[0] if mode == "nt" else b.shape[1]
    tm = tm or _tile(m, (512, 256, 128))
    tn = tn or _tile(n, (512, 256, 128))
    j0 = 0
    if b_cols is not None:
        j0, n = b_cols[0], b_cols[1] * tn
    dims = _DIMS[mode]

    def body(*refs):
        refs = refs if after is None else refs[1:]
        acc = lax.dot_general(refs[0][...].astype(BF16), refs[1][...].astype(BF16), dims,
                              preferred_element_type=F32)
        if add is not None:
            acc = acc + refs[2][...]
        refs[-1][...] = acc.astype(out_dtype)

    once_a = dict(pipeline_mode=pl.Buffered(1)) if m == tm else {}
    once_b = dict(pipeline_mode=pl.Buffered(1)) if n == tn else {}
    a_spec = (pl.BlockSpec((kdim, tm), lambda i, j: (0, i), **once_a) if mode == "tn"
              else pl.BlockSpec((tm, kdim), lambda i, j: (i, 0), **once_a))
    b_spec = (pl.BlockSpec((tn, kdim), lambda i, j: (j, 0), **once_b) if mode == "nt"
              else pl.BlockSpec((kdim, tn), lambda i, j: (0, j + j0), **once_b))
    o_spec = pl.BlockSpec((tm, tn), lambda i, j: (i, j))
    in_specs, args = [a_spec, b_spec], [a, b]
    if add is not None:
        in_specs.append(o_spec)
        args.append(add)
    if after is not None:
        in_specs.insert(0, _ANY)
        args.insert(0, after)
    return pl.pallas_call(
        body, name=name, out_shape=jax.ShapeDtypeStruct((m, n), out_dtype), grid=(m // tm, n // tn),
        in_specs=in_specs, out_specs=o_spec,
        compiler_params=_params("parallel", "parallel"),
    )(*args)


def _rowmap(name, fn, seq, rows, bats=(), vecs=(), row_outs=(), bat_outs=(), vec_outs=(), ts=None, mm=None, lhs=None,
            after=None, mm_sum=True):
    mms = [] if mm is None else (mm if isinstance(mm, list) else [mm])
    rows = [r if isinstance(r, tuple) else (r, r.shape[1], 0) for r in rows]
    tokens = rows[0][0].shape[0]
    nseq = tokens // seq
    ts = ts or _tile(seq, (512, 256, 128, 64, 32, 16, 8))
    nt = seq // ts
    n_r, n_b, n_v = len(rows), len(bats), len(vecs)
    n_ro, n_bo = len(row_outs), len(bat_outs)

    def accumulate(ref, val, first):
        @pl.when(first)
        def _():
            ref[...] = val.reshape(ref.shape)

        @pl.when(jnp.logical_not(first))
        def _():
            ref[...] += val.reshape(ref.shape)

    def body(*refs):
        n_in = n_r + n_b + n_v + len(mms) + (after is not None)
        ins, outs = refs[:n_in], refs[n_in:]
        b_vals = [r[0] for r in ins[n_r:n_r + n_b]]
        v_vals = [r[...] for r in ins[n_r + n_b:n_r + n_b + n_v]]
        r_vals = [r[...] for r in ins[:n_r]]
        if mms:
            lefts = r_vals[:len(mms)] if lhs is None else [lhs(r_vals)] * len(mms)
            parts = [lax.dot_general(left.astype(BF16), b_ref[...].astype(BF16), _DIMS[mode],
                                     preferred_element_type=F32)
                     for left, b_ref, (_, mode) in zip(lefts, ins[n_r + n_b + n_v:], mms)]
            accs = [functools.reduce(lambda x, y: x + y, parts)] if mm_sum else parts
            r_vals = accs + r_vals[len(mms):] if lhs is None else accs + [lefts[0]] + r_vals
        ro, bo, vo = fn(r_vals, b_vals, v_vals)
        for ref, val in zip(outs[:n_ro], ro):
            ref[...] = val.astype(ref.dtype)
        b, i = pl.program_id(0), pl.program_id(1)
        for ref, val in zip(outs[n_ro:n_ro + n_bo], bo):
            accumulate(ref, val, i == 0)
        for ref, val in zip(outs[n_ro + n_bo:], vo):
            accumulate(ref, val, jnp.logical_and(i == 0, b == 0))

    in_specs = [pl.BlockSpec((ts, w), functools.partial(lambda b, i, cb: (b * nt + i, cb), cb=cb))
                for _, w, cb in rows]
    in_specs += [pl.BlockSpec((1, 1, v.shape[2]), lambda b, i: (b, 0, 0)) for v in bats]
    in_specs += [pl.BlockSpec((1, v.shape[1]), lambda b, i: (0, 0)) for v in vecs]
    extra = [b_arr for b_arr, _ in mms]
    in_specs += [pl.BlockSpec(b_arr.shape, lambda b, i: (0, 0), pipeline_mode=pl.Buffered(1)) for b_arr in extra]
    if after is not None:
        in_specs.append(_ANY)
        extra.append(after)
    out_shape = [jax.ShapeDtypeStruct((tokens, f), dt) for f, dt in row_outs]
    out_specs = [pl.BlockSpec((ts, f), lambda b, i: (b * nt + i, 0)) for f, _ in row_outs]
    out_shape += [jax.ShapeDtypeStruct((nseq, 1, f), F32) for f in bat_outs]
    out_specs += [pl.BlockSpec((1, 1, f), lambda b, i: (b, 0, 0)) for f in bat_outs]
    out_shape += [jax.ShapeDtypeStruct((1, f), F32) for f in vec_outs]
    out_specs += [pl.BlockSpec((1, f), lambda b, i: (0, 0)) for f in vec_outs]
    return pl.pallas_call(
        body, name=name, out_shape=tuple(out_shape), grid=(nseq, nt),
        in_specs=in_specs, out_specs=tuple(out_specs),
        compiler_params=_params("arbitrary", "arbitrary"),
    )(*([r[0] for r in rows] + list(bats) + list(vecs) + extra))


def _colsum(v):
    return jnp.sum(v, axis=0, keepdims=True)


def _rstd(x, width=None):
    width = width or x.shape[-1]
    return lax.rsqrt(jnp.sum(x * x, axis=-1, keepdims=True) * (1.0 / width) + NORM_EPS)


def _norm_bwd(dy, x, r, g, width=None):
    width = width or x.shape[-1]
    xhat = x * r
    dxhat = dy * g
    dx = r * (dxhat - xhat * (jnp.sum(dxhat * xhat, axis=-1, keepdims=True) * (1.0 / width)))
    return dx, dy * xhat


def _sigmoid(x):
    return 0.5 * jnp.tanh(0.5 * x) + 0.5


def _norm_mod(xv, g, sh, sc):
    return xv * _rstd(xv) * g * (1.0 + sc) + sh


def _norm_mod_fwd(x, p, seq, name):
    def fn(rows, bats, vecs):
        return [_norm_mod(rows[0], vecs[0], bats[0], bats[1])], [], []
    return _rowmap(name, fn, seq, [x], [p["shift"], p["scale"]], [p["gamma"]], row_outs=[(D_MODEL, BF16)])[0]


def _norm_mod_bwd(dh, x, dres, p, seq, name, prev=None, after=None):
    products = dh if isinstance(dh, list) else None
    lefts = [l for l, _ in products] if products else [dh]
    def fn(rows, bats, vecs):
        dhv, xv, dr = rows[:3]
        sc, g = bats[0], vecs[0]
        r = _rstd(xv)
        dxn, dg = _norm_bwd(dhv * (1.0 + sc), xv, r, g)
        dx = dr + dxn
        ro, bo = [dx], [_colsum(dhv), _colsum(dhv * (xv * r * g))]
        if prev is not None:
            ro.append(bats[1] * dx)
            bo.append(_colsum(dx * rows[3].astype(F32)))
        return ro, bo, [_colsum(dg)]
    more = prev is not None
    return _rowmap(name, fn, seq, lefts + [x, dres] + ([prev[0]] if more else []),
                   [p["scale"]] + ([prev[1]] if more else []), [p["gamma"]],
                   row_outs=[(D_MODEL, F32)] + ([(D_MODEL, BF16)] if more else []),
                   bat_outs=[D_MODEL] * (3 if more else 2), vec_outs=[D_MODEL],
                   mm=[(r, "nn") for _, r in products] if products else None, after=after)


def _ffn_in_act(h, wt_in, name):
    tokens = h.shape[0]
    tm, tn = _tile(tokens, (2048, 1024, 512)), 256
    nj = D_FF // tn

    def body(h_ref, wg_ref, wu_ref, g_ref, u_ref, a_ref):
        hv = h_ref[...]
        g = lax.dot_general(hv, wg_ref[...], _DIMS["nt"], preferred_element_type=F32)
        u = lax.dot_general(hv, wu_ref[...], _DIMS["nt"], preferred_element_type=F32)
        g_ref[...] = g.astype(BF16)
        u_ref[...] = u.astype(BF16)
        a_ref[...] = (g * _sigmoid(g) * u).astype(BF16)

    o_spec = pl.BlockSpec((tm, tn), lambda i, j: (i, j))
    out = jax.ShapeDtypeStruct((tokens, D_FF), BF16)
    return pl.pallas_call(
        body, name=name, grid=(tokens // tm, nj), out_shape=(out, out, out),
        in_specs=[pl.BlockSpec((tm, D_MODEL), lambda i, j: (i, 0)),
                  pl.BlockSpec((tn, D_MODEL), lambda i, j: (j, 0)),
                  pl.BlockSpec((tn, D_MODEL), lambda i, j: (j + nj, 0))],
        out_specs=(o_spec, o_spec, o_spec),
        compiler_params=_params("parallel", "parallel"),
    )(h, wt_in, wt_in)


def _out_residual(a, w_out, res, gate, nxt, seq, name, lhs=None):
    def fn(rows, bats, vecs):
        acc, rv = rows[0], rows[-1]
        x_new = rv + bats[0] * acc
        made = [] if lhs is None else [rows[1]]
        return [x_new, acc, _norm_mod(x_new, vecs[0], bats[1], bats[2])] + made, [], []
    outs = [(D_MODEL, F32), (D_MODEL, BF16), (D_MODEL, BF16)] + ([] if lhs is None else [(D_MODEL, BF16)])
    return _rowmap(name, fn, seq, (a if lhs is not None else [a]) + [res], [gate, nxt["shift"], nxt["scale"]],
                   [nxt["gamma"]], row_outs=outs, ts=_tile(seq, (512, 256, 128)), mm=(w_out, "nn"), lhs=lhs)


def _out_loss(a, w_out, res, gate, target, seq, name):
    def fn(rows, bats, vecs):
        acc, rv, tv = rows
        err = rv + bats[0] * acc - tv
        dy = err * (1.0 / D_MODEL)
        return [dy, bats[0] * dy], [_colsum(dy * acc)], [_colsum(err * err)]
    return _rowmap(name, fn, seq, [a, res, target], [gate], row_outs=[(D_MODEL, F32), (D_MODEL, BF16)],
                   bat_outs=[D_MODEL], vec_outs=[D_MODEL], ts=_tile(seq, (512, 256, 128)), mm=(w_out, "nn"))


def _ffn_bwd_x(df, dres, saved, p, seq, tag, prev=None, early=None, mid=None):
    x, h, g, u, a, w_in, w_out = saved
    first = None if early is None else early(a, df)

    def act_bwd(rows, bats, vecs):
        dav, gv, uv = rows[0], rows[1].astype(F32), rows[2].astype(F32)
        sg = _sigmoid(gv)
        silu = gv * sg
        dg = dav * uv * (sg * (1.0 + gv * (1.0 - sg)))
        return [jnp.concatenate([dg, dav * silu], axis=1)], [], []
    dgu = _rowmap(f"{tag}_bwd_da", act_bwd, seq, [df, g, u], row_outs=[(2 * D_FF, BF16)],
                  ts=_tile(seq, (512, 256, 128)), mm=(w_out, "nt"), after=first)[0]
    operands = (a, df, dgu, h)
    after = None if mid is None else mid(operands)
    return _norm_mod_bwd([(dgu, w_in)], x, dres, p, seq, f"{tag}_bwd_norm", prev, after=after), operands


def _ffn_bwd_wout(a, df, tag):
    return _mm(a, df, "tn", f"{tag}_bwd_wout", out_dtype=BF16, tm=256, tn=D_MODEL)


def _ffn_bwd_win(operands, tag, after=None, half=None):
    _, _, dgu, h = operands
    if half is None:
        return _mm(dgu, h, "tn", f"{tag}_bwd_win", out_dtype=BF16, tm=512, tn=D_MODEL, after=after)
    return _mm(dgu, h, "tn", f"{tag}_bwd_win{half}", out_dtype=BF16, tm=512, tn=D_MODEL // 2, after=after,
               b_cols=(half, 1))


def _shift_rows(v, k, forward):
    n = v.shape[0]
    row = lax.broadcasted_iota(jnp.int32, v.shape, 0)
    if forward:
        return jnp.where(row >= k, pltpu.roll(v, k, 0), 0.0)
    return jnp.where(row < n - k, pltpu.roll(v, n - k, 0), 0.0)


def _window_sums(v, forward):
    out, s, k = [], v, 1
    for _ in range(POOL_GROUPS):
        s = s + _shift_rows(s, k, forward)
        out.append(s)
        k *= 2
    return out


def _by_group(vals, g):
    out = vals[-1]
    for idx in range(len(vals) - 2, -1, -1):
        out = jnp.where(g == idx, vals[idx], out)
    return out


def _inv_count(shape, g):
    t1 = lax.broadcasted_iota(jnp.int32, shape, 0) + 1
    window = _by_group([jnp.int32(2 ** (i + 1)) for i in range(POOL_GROUPS)], g)
    return 1.0 / jnp.minimum(t1, window).astype(F32)


def _pool_fwd(u, grp, scale, seq):
    tokens = u.shape[0]

    def body(u_ref, grp_ref, sc_ref, pooled_ref, pg_ref, ps_ref):
        g = pl.program_id(1)
        uv = u_ref[...]
        sums = _by_group(_window_sums(uv, True), g)
        pooled = (sums * _inv_count(uv.shape, g) - uv).astype(BF16)
        pg = jnp.dot(pooled, grp_ref[0].astype(BF16), preferred_element_type=F32)
        pooled_ref[...] = pooled
        pg_ref[...] = pg
        ps_ref[...] = (pg * sc_ref[...]).astype(BF16)

    blk = pl.BlockSpec((seq, POOL_GROUP_DIM), lambda b, g: (b, g))
    return pl.pallas_call(
        body, name="pool_fwd", grid=(tokens // seq, POOL_GROUPS),
        out_shape=(jax.ShapeDtypeStruct(u.shape, BF16), jax.ShapeDtypeStruct(u.shape, F32),
                   jax.ShapeDtypeStruct(u.shape, BF16)),
        in_specs=[blk, pl.BlockSpec((1, POOL_GROUP_DIM, POOL_GROUP_DIM), lambda b, g: (g, 0, 0)),
                  pl.BlockSpec((1, POOL_GROUP_DIM), lambda b, g: (0, g))],
        out_specs=(blk, blk, blk),
        compiler_params=_params("parallel", "parallel"),
    )(u, grp, scale)


def _pool_bwd(dps, pooled, pg, grp, scale, seq):
    tokens = dps.shape[0]

    def body(dps_ref, pooled_ref, pg_ref, grp_ref, sc_ref, du_ref, dgrp_ref, dsc_ref):
        g, b = pl.program_id(0), pl.program_id(1)
        dpsv = dps_ref[...]
        dpg = (dpsv * sc_ref[...]).astype(BF16)
        dsc = _colsum(dpsv * pg_ref[...])
        dgrp = lax.dot_general(pooled_ref[...], dpg, _DIMS["tn"], preferred_element_type=F32)

        @pl.when(b == 0)
        def _():
            dsc_ref[...] = dsc
            dgrp_ref[0] = dgrp

        @pl.when(b > 0)
        def _():
            dsc_ref[...] += dsc
            dgrp_ref[0] += dgrp

        dpool = lax.dot_general(dpg, grp_ref[0].astype(BF16), _DIMS["nt"], preferred_element_type=F32)
        sums = _by_group(_window_sums(dpool * _inv_count(dpool.shape, g), False), g)
        du_ref[...] = (sums - dpool).astype(BF16)

    blk = pl.BlockSpec((seq, POOL_GROUP_DIM), lambda g, b: (b, g))
    grp_spec = pl.BlockSpec((1, POOL_GROUP_DIM, POOL_GROUP_DIM), lambda g, b: (g, 0, 0))
    vec_spec = pl.BlockSpec((1, POOL_GROUP_DIM), lambda g, b: (0, g))
    return pl.pallas_call(
        body, name="pool_bwd", grid=(POOL_GROUPS, tokens // seq),
        out_shape=(jax.ShapeDtypeStruct(dps.shape, BF16), jax.ShapeDtypeStruct(grp.shape, F32),
                   jax.ShapeDtypeStruct(scale.shape, F32)),
        in_specs=[blk, blk, blk, grp_spec, vec_spec],
        out_specs=(blk, grp_spec, vec_spec),
        compiler_params=_params("arbitrary", "arbitrary"),
    )(dps, pooled, pg, grp, scale)


def _lane(shape):
    return lax.broadcasted_iota(jnp.int32, shape, len(shape) - 1)


def _rot(y):
    lane = _lane(y.shape)
    r = jnp.where(lane < QK_NOPE + QK_ROPE // 2,
                  -pltpu.roll(y, HEAD_SLAB - QK_ROPE // 2, 1), pltpu.roll(y, QK_ROPE // 2, 1))
    return jnp.where(jnp.logical_and(lane >= QK_NOPE, lane < QK_NOPE + QK_ROPE), r, 0.0)


def _part_rstd(x):
    sq = x * x
    nope = _lane(x.shape) < QK_NOPE
    s_nope = jnp.sum(jnp.where(nope, sq, 0.0), axis=-1, keepdims=True)
    s_rope = jnp.sum(sq, axis=-1, keepdims=True) - s_nope
    return jnp.where(nope, lax.rsqrt(s_nope * (1.0 / QK_NOPE) + NORM_EPS),
                     lax.rsqrt(s_rope * (1.0 / QK_ROPE) + NORM_EPS))


def _part_norm_bwd(dy, x, r, g):
    nope = _lane(x.shape) < QK_NOPE
    xhat = x * r
    dxhat = dy * g
    prod = dxhat * xhat
    m_nope = jnp.sum(jnp.where(nope, prod, 0.0), axis=-1, keepdims=True)
    m_rope = jnp.sum(prod, axis=-1, keepdims=True) - m_nope
    mean = jnp.where(nope, m_nope * (1.0 / QK_NOPE), m_rope * (1.0 / QK_ROPE))
    return r * (dxhat - xhat * mean), dy * xhat


def _mixer_in(h, wt_a, wt_p, wt_g, g_q, g_kv, seq):
    def fn(rows, bats, vecs):
        z_a, z_p, z_g = rows[:3]
        q, kv = z_a[:, :Q_LORA], z_a[:, Q_LORA:Q_LORA + KV_LORA]
        return [z_a, z_p, z_g, q * _rstd(q) * vecs[0], kv * _rstd(kv) * vecs[1]], [], []
    return _rowmap("mix_in", fn, seq, [h], vecs=[g_q, g_kv], lhs=lambda rows: rows[0],
                   mm=[(wt_a, "nt"), (wt_p, "nt"), (wt_g, "nt")], mm_sum=False,
                   row_outs=[(wt_a.shape[0], F32), (wt_p.shape[0], F32), (wt_g.shape[0], BF16),
                             (Q_LORA, BF16), (KV_LORA, BF16)])


def _latent_norm_bwd(dqp, wtq_pad, dkv, wt_kv, dkr, z_a, g_q, g_kv, seq):
    def fn(rows, bats, vecs):
        dq, dkv, dkrv, z = rows
        q, kv = z[:, :Q_LORA], z[:, Q_LORA:Q_LORA + KV_LORA]
        dxq, dgq = _norm_bwd(dq, q, _rstd(q), vecs[0])
        dxkv, dgkv = _norm_bwd(dkv, kv, _rstd(kv), vecs[1])
        return [jnp.concatenate([dxq, dxkv, dkrv], axis=1)], [], [_colsum(dgq), _colsum(dgkv)]
    return _rowmap("latent_norm_bwd", fn, seq, [dqp, dkv, dkr, z_a], vecs=[g_q, g_kv],
                   row_outs=[(Q_LORA + KV_LORA + HEAD_SLAB, BF16)], vec_outs=[Q_LORA, KV_LORA],
                   mm=[(wtq_pad, "nn"), (wt_kv, "nn")], mm_sum=False)


def _qk_prep_fwd(qn, wtq_pad, kvn, wt_kv, z_a, pos, g_q, g_kn, g_kr, inv_freq, seq):
    def fn(rows, bats, vecs):
        qv, kvv, kr, p = rows
        gq, gkn, gkr, invf = vecs
        ang = p * invf
        cos, sin = jnp.cos(ang), jnp.sin(ang)
        nope = _lane(kr.shape) < QK_NOPE
        krn = kr * _rstd(kr, QK_ROPE) * gkr
        krr = krn * cos + _rot(krn) * sin
        qs, ks, vs = [], [], []
        for h in range(N_HEADS):
            xq = qv[:, h * HEAD_SLAB:(h + 1) * HEAD_SLAB]
            y = xq * _part_rstd(xq) * gq
            qs.append(y * cos + _rot(y) * sin)
            xk = kvv[:, h * HEAD_SLAB:(h + 1) * HEAD_SLAB]
            kn = jnp.where(nope, xk, 0.0)
            ks.append(jnp.where(nope, kn * _rstd(kn, QK_NOPE) * gkn, krr))
            vs.append(jnp.where(nope, 0.0, xk))
        return [jnp.concatenate(v, axis=1) for v in (qs, ks, vs)] + [qv, kvv], [], []
    width = N_HEADS * HEAD_SLAB
    return _rowmap("qk_prep", fn, seq, [qn, kvn, (z_a, HEAD_SLAB, 5), pos], vecs=[g_q, g_kn, g_kr, inv_freq],
                   row_outs=[(width, BF16)] * 3 + [(width, F32)] * 2, mm=[(wtq_pad, "nt"), (wt_kv, "nt")],
                   mm_sum=False)


def _qk_prep_bwd(dqc, dkc, dvp, qp, kv, z_a, pos, g_q, g_kn, g_kr, inv_freq, seq):
    def fn(rows, bats, vecs):
        dq, dk, dv = (r.astype(F32) for r in rows[:3])
        qv, kvv, kr, p = rows[3:]
        gq, gkn, gkr, invf = vecs
        ang = p * invf
        cos, sin = jnp.cos(ang), jnp.sin(ang)
        nope = _lane(kr.shape) < QK_NOPE
        dqs, dkvs = [], []
        dgq = jnp.zeros((1, HEAD_SLAB), F32)
        dgkn = jnp.zeros((1, HEAD_SLAB), F32)
        dkrr = jnp.zeros(kr.shape, F32)
        for h in range(N_HEADS):
            sl = slice(h * HEAD_SLAB, (h + 1) * HEAD_SLAB)
            dyr = dq[:, sl]
            dy = dyr * cos - _rot(dyr * sin)
            xq = qv[:, sl]
            dx, dg = _part_norm_bwd(dy, xq, _part_rstd(xq), gq)
            dqs.append(dx)
            dgq = dgq + _colsum(dg)
            dkh = dk[:, sl]
            dkrr = dkrr + jnp.where(nope, 0.0, dkh)
            kn = jnp.where(nope, kvv[:, sl], 0.0)
            dxk, dgk = _norm_bwd(jnp.where(nope, dkh, 0.0), kn, _rstd(kn, QK_NOPE), gkn, QK_NOPE)
            dgkn = dgkn + _colsum(dgk)
            dkvs.append(jnp.where(nope, dxk, dv[:, sl]))
        dkrn = dkrr * cos - _rot(dkrr * sin)
        dkr, dgkr = _norm_bwd(dkrn, kr, _rstd(kr, QK_ROPE), gkr, QK_ROPE)
        return ([jnp.concatenate(dqs, axis=1), jnp.concatenate(dkvs, axis=1), dkr], [],
                [dgq, dgkn, _colsum(dgkr)])
    width = N_HEADS * HEAD_SLAB
    return _rowmap("qk_prep_bwd", fn, seq, [dqc, dkc, dvp, qp, kv, (z_a, HEAD_SLAB, 5), pos],
                   vecs=[g_q, g_kn, g_kr, inv_freq],
                   row_outs=[(width, BF16), (width, BF16), (HEAD_SLAB, F32)],
                   vec_outs=[HEAD_SLAB] * 3, ts=_tile(seq, (512, 256, 128, 64, 32, 16, 8)))


def _scores(q, k_ref, keys, tq):
    s = lax.dot_general(q, k_ref[0:keys, :], _DIMS["nt"], preferred_element_type=F32) * ATTN_SCALE
    row = lax.broadcasted_iota(jnp.int32, (tq, tq), 0)
    col = lax.broadcasted_iota(jnp.int32, (tq, tq), 1)
    diag = jnp.where(col <= row, s[:, keys - tq:], -1e30)
    return diag if keys == tq else jnp.concatenate([s[:, :keys - tq], diag], axis=1)


def _attn_fwd(qc, kc, vp, seq):
    tokens = qc.shape[0]
    tq = _tile(seq, (256, 128))
    nq = seq // tq

    def body(q_ref, k_ref, v_ref, o_ref, lse_ref):
        for i in range(nq):
            rows, keys = slice(i * tq, (i + 1) * tq), (i + 1) * tq
            s = _scores(q_ref[rows, :], k_ref, keys, tq)
            m = jnp.max(s, axis=-1, keepdims=True)
            p = jnp.exp(s - m)
            l = jnp.sum(p, axis=-1, keepdims=True)
            acc = jnp.dot(p.astype(BF16), v_ref[0:keys, :], preferred_element_type=F32)
            o_ref[rows, :] = (acc / l).astype(BF16)
            lse_ref[rows, :] = jnp.broadcast_to(m + jnp.log(l), (tq, HEAD_SLAB))

    spec = pl.BlockSpec((seq, HEAD_SLAB), lambda b, h: (b, h))
    return pl.pallas_call(
        body, name="attn_fwd", grid=(tokens // seq, N_HEADS),
        out_shape=(jax.ShapeDtypeStruct(qc.shape, BF16), jax.ShapeDtypeStruct(qc.shape, F32)),
        in_specs=[spec] * 3, out_specs=(spec, spec),
        compiler_params=_params("parallel", "parallel"),
    )(qc, kc, vp)


def _attn_bwd(qc, kc, vp, o, lse, do, seq):
    tokens = qc.shape[0]
    tq = _tile(seq, (256, 128))
    nq = seq // tq

    def body(q_ref, k_ref, v_ref, o_ref, lse_ref, do_ref, dq_ref, dk_ref, dv_ref, dk_acc, dv_acc):
        dk_acc[...] = jnp.zeros(dk_acc.shape, F32)
        dv_acc[...] = jnp.zeros(dv_acc.shape, F32)
        for i in range(nq):
            rows, keys = slice(i * tq, (i + 1) * tq), (i + 1) * tq
            q, dov = q_ref[rows, :], do_ref[rows, :]
            delta = jnp.sum(dov.astype(F32) * o_ref[rows, :].astype(F32), axis=-1, keepdims=True)
            s = _scores(q, k_ref, keys, tq)
            p = jnp.exp(s - jnp.tile(lse_ref[rows, :], (1, keys // HEAD_SLAB)))
            dp = lax.dot_general(dov, v_ref[0:keys, :], _DIMS["nt"], preferred_element_type=F32)
            ds = (p * (dp - delta) * ATTN_SCALE).astype(BF16)
            dq_ref[rows, :] = jnp.dot(ds, k_ref[0:keys, :], preferred_element_type=F32).astype(BF16)
            dk_acc[0:keys, :] += lax.dot_general(ds, q, _DIMS["tn"], preferred_element_type=F32)
            dv_acc[0:keys, :] += lax.dot_general(p.astype(BF16), dov, _DIMS["tn"], preferred_element_type=F32)
        dk_ref[...] = dk_acc[...].astype(BF16)
        dv_ref[...] = dv_acc[...].astype(BF16)

    spec = pl.BlockSpec((seq, HEAD_SLAB), lambda b, h: (b, h))
    out = jax.ShapeDtypeStruct(qc.shape, BF16)
    return pl.pallas_call(
        body, name="attn_bwd", grid=(tokens // seq, N_HEADS),
        out_shape=(out, out, out), in_specs=[spec] * 6, out_specs=(spec, spec, spec),
        scratch_shapes=[pltpu.VMEM((seq, HEAD_SLAB), F32), pltpu.VMEM((seq, HEAD_SLAB), F32)],
        compiler_params=_params("parallel", "parallel"),
    )(qc, kc, vp, o, lse, do)


def _adamw(w, g, m, v, name):
    rows, cols = w.shape
    whole = rows * cols * 4 <= ADAMW_WHOLE_BYTES
    tr = rows if whole else _tile(rows, (256, 128, 64, 32, 16, 8))
    c1 = 1.0 - ADAM_B1 ** ADAM_STEP
    c2 = 1.0 - ADAM_B2 ** ADAM_STEP

    def body(w_ref, g_ref, m_ref, v_ref, d_ref, nm_ref, nv_ref):
        gv = g_ref[...]
        nm = ADAM_B1 * m_ref[...] + (1.0 - ADAM_B1) * gv
        nv = ADAM_B2 * v_ref[...] + (1.0 - ADAM_B2) * (gv * gv)
        d_ref[...] = -ADAM_LR * ((nm / c1) / (jnp.sqrt(nv / c2) + ADAM_EPS) + ADAM_WD * w_ref[...])
        nm_ref[...] = nm
        nv_ref[...] = nv

    spec = pl.BlockSpec((tr, cols), lambda i: (i, 0))
    out = jax.ShapeDtypeStruct(w.shape, F32)
    return pl.pallas_call(
        body, name=name, grid=(rows // tr,), out_shape=(out, out, out),
        in_specs=[spec] * 4, out_specs=(spec, spec, spec),
        compiler_params=_params("parallel"),
    )(w, g, m, v)


def _adamw_landed(w, landed, m, v, name):
    rows, cols = w.shape
    tr = _tile(rows, (176, 128, 96, 64, 32, 16, 8))
    c1 = 1.0 - ADAM_B1 ** ADAM_STEP
    c2 = 1.0 - ADAM_B2 ** ADAM_STEP
    n_parts = len(landed)

    def body(*refs):
        w_ref, m_ref, v_ref = refs[:3]
        g_ref, d_ref, nm_ref, nv_ref = refs[3 + n_parts:]
        parts = []
        for x_ref in refs[3:3 + n_parts]:
            acc = x_ref[0].astype(F32)
            for d in range(1, N_DEV):
                acc = acc + x_ref[d].astype(F32)
            parts.append(acc)
        gv = parts[0] if n_parts == 1 else jnp.concatenate(parts, axis=1)
        nm = ADAM_B1 * m_ref[...] + (1.0 - ADAM_B1) * gv
        nv = ADAM_B2 * v_ref[...] + (1.0 - ADAM_B2) * (gv * gv)
        g_ref[...] = gv
        d_ref[...] = -ADAM_LR * ((nm / c1) / (jnp.sqrt(nv / c2) + ADAM_EPS) + ADAM_WD * w_ref[...])
        nm_ref[...] = nm
        nv_ref[...] = nv

    spec = pl.BlockSpec((tr, cols), lambda i: (i, 0))
    out = jax.ShapeDtypeStruct(w.shape, F32)
    return pl.pallas_call(
        body, name=name, grid=(rows // tr,), out_shape=(out, out, out, out),
        in_specs=[spec] * 3 + [pl.BlockSpec((N_DEV, tr, x.shape[2]), lambda i: (0, i, 0)) for x in landed],
        out_specs=(spec, spec, spec, spec),
        compiler_params=_params("parallel"),
    )(w, m, v, *landed)


def _mod_cols(c_all, w_ada, b_cols):
    def body(c_ref, w_ref, b_ref, act_ref, mod_ref):
        cv = c_ref[...]
        act = cv * _sigmoid(cv)
        act_ref[...] = act
        mod_ref[...] = jnp.dot(act.astype(BF16), w_ref[...].astype(BF16),
                               preferred_element_type=F32) + b_ref[...]

    n = w_ada.shape[1]
    return pl.pallas_call(
        body, name="mod_cols",
        out_shape=(jax.ShapeDtypeStruct(c_all.shape, F32), jax.ShapeDtypeStruct((c_all.shape[0], n), F32)),
        compiler_params=pltpu.CompilerParams(vmem_limit_bytes=VMEM_LIMIT),
    )(c_all, w_ada, b_cols)


def _ada_grads(c_act, dmod_all, dmod_cols):
    def body(c_ref, d_ref, dc_ref, gw_ref, gb_ref):
        gw_ref[...] = lax.dot_general(c_ref[...].astype(BF16), dc_ref[...].astype(BF16), _DIMS["tn"],
                                      preferred_element_type=F32)
        gb_ref[...] = _colsum(d_ref[...])

    return pl.pallas_call(
        body, name="ada_grads",
        out_shape=(jax.ShapeDtypeStruct((c_act.shape[1], dmod_cols.shape[1]), F32),
                   jax.ShapeDtypeStruct((1, dmod_all.shape[1]), F32)),
        compiler_params=pltpu.CompilerParams(vmem_limit_bytes=VMEM_LIMIT),
    )(c_act, dmod_all, dmod_cols)


def _flat_rows(a):
    flat = a.reshape(-1)
    pad = (-flat.shape[0]) % (LANES * SUBLANES)
    if pad:
        flat = jnp.pad(flat, (0, pad))
    return flat.reshape(-1, LANES)


def _gather_start(w, groups, tag, after=None, peers=None):
    shards = [[(w[n] if n in ROW_SHARDED else w[n].T).astype(BF16) for n in names] for names in groups]
    return _exchange_start_groups(shards, f"gather_{tag}_start", after=after, peers=peers)


def _gather_wait(handle, names, tag, after, peers=ALL_PEERS):
    landed = _exchange_wait(handle, f"gather_{tag}_wait", after=after, peers=peers)
    if peers == CHIP_PEERS:
        landed = [_sibling_forward(x, f"gather_{tag}_forward{i}") for i, x in enumerate(landed)]
    return {n: g.reshape(-1, g.shape[2]) for n, g in zip(names, landed)}


def _scatter_start(grads, names, tag, after=None):
    blocks = [grads[n].reshape(N_DEV, -1, grads[n].shape[1]) for n in names]
    return _exchange_start(blocks, f"scatter_{tag}_start", scatter=True, after=after)


def _scatter_wait(handle, names, tag, after):
    landed = _exchange_wait(handle, f"scatter_{tag}_wait", scatter=True, after=after)
    return {n: [x] for n, x in zip(names, landed)}


def _pack_small(vals):
    return jnp.concatenate([_flat_rows(v.astype(F32)) for v in vals], axis=0)


def _unpack_small(packed, like):
    out, row = [], 0
    for v in like:
        rows = _flat_rows(v).shape[0]
        out.append(packed[row:row + rows].reshape(-1)[:v.size].reshape(v.shape))
        row += rows
    return out


def _lanes128(*parts):
    out = jnp.zeros((HEAD_SLAB,), F32)
    for off, v in parts:
        out = lax.dynamic_update_slice(out, v.reshape(-1).astype(F32), (off,))
    return out.reshape(1, HEAD_SLAB)


def _step(x, c, positions, w, m, v, loss_target):
    nseq, seq, _ = x.shape
    tokens = nseq * seq
    me = _index(_my_pos())
    strip = lambda d: {n: (a[0] if a.ndim > 2 else a) for n, a in d.items()}
    shapes = {n: a.shape for n, a in w.items()}
    w, m, v = strip(w), strip(m), strip(v)

    c_all = _all_gather(c.reshape(-1, LANES), "gather_c").reshape(N_DEV * nseq, D_MODEL)
    n_ada = w["w_ada"].shape[1]
    b_cols = lax.dynamic_slice(w["b_ada"], (0, me * n_ada), (1, n_ada))
    c_act, mod_cols = _mod_cols(c_all, w["w_ada"], b_cols)
    mod_all = _all_gather(mod_cols, "gather_mod")
    mod = lax.dynamic_slice(mod_all, (0, me * nseq, 0), (N_DEV, nseq, n_ada))
    mod = mod.transpose(1, 0, 2).reshape(nseq, 3, 3, 1, D_MODEL)

    (h_f1i, h_f1o, h_mix_in, h_mix, h_f2), tok = _gather_start(
        w, (("w_ffn1_in",), ("w_ffn1_out",), MIXER[:1], MIXER[1:], ("w_ffn2_in", "w_ffn2_out")), "weights",
        after=mod_all, peers=[CHIP_PEERS] + [ALL_PEERS] * 4)
    started = tok[0:1, 0:1]

    g_q = _lanes128((0, w["q_norm_nope"]), (QK_NOPE, w["q_norm_rope"]))
    g_kn = _lanes128((0, w["k_norm_nope"]))
    g_kr = _lanes128((QK_NOPE, w["k_norm_rope"]))
    freq = ROPE_THETA ** (-jnp.arange(0, QK_ROPE, 2, dtype=F32) / QK_ROPE)
    inv_freq = _lanes128((QK_NOPE, jnp.concatenate([freq, freq])))
    pos = positions.reshape(tokens, 1).astype(F32)

    def sub(k, gamma, coef):
        return dict(gamma=w[gamma], shift=mod[:, k, 0] + started, scale=mod[:, k, 1], gate=coef * mod[:, k, 2])
    p1, pm, p2 = sub(0, "norm_ffn1", 0.5), sub(1, "norm_mix", 1.0), sub(2, "norm_ffn2", 0.5)
    t_big = _tile(tokens, (2048, 1024, 512))
    t_mid = _tile(tokens, (1024, 512))

    x0 = x.reshape(tokens, D_MODEL)
    h1 = _norm_mod_fwd(x0, p1, seq, "ffn1_norm")
    wt_f1i = _gather_wait(h_f1i, ("w_ffn1_in",), "ffn1_in", h1, peers=CHIP_PEERS)["w_ffn1_in"]
    g1, u1, a1 = _ffn_in_act(h1, wt_f1i, "ffn1_in")
    w_f1o = _gather_wait(h_f1o, ("w_ffn1_out",), "ffn1_out", a1)["w_ffn1_out"]
    x1, f1, h2 = _out_residual(a1, w_f1o, x0, p1["gate"], pm, seq, "ffn1_out")
    saved1 = (x0, h1, g1, u1, a1, wt_f1i, w_f1o)

    wt_in = _gather_wait(h_mix_in, MIXER[:1], "mix_in", h2)["w_in"]
    zero_rows = lambda rows: jnp.zeros((rows, D_MODEL), BF16)
    wt_p = wt_in[:512]
    wt_a = jnp.concatenate([wt_in[512:1152], zero_rows(QK_NOPE), wt_in[1152:1184], zero_rows(32)], axis=0)
    wt_g = wt_in[1184:]
    z_a, z_p, z_g, qn, kvn = _mixer_in(h2, wt_a, wt_p, wt_g, w["q_a_norm"], w["kv_a_norm"], seq)

    full = _gather_wait(h_mix, MIXER[1:], "mix", z_g)
    wtq_pad = jnp.pad(full["w_q_up"].reshape(N_HEADS, 96, Q_LORA), ((0, 0), (0, 32), (0, 0))).reshape(-1, Q_LORA)
    wtmla_pad = jnp.pad(full["w_mla_proj"].reshape(D_MODEL, N_HEADS, 64), ((0, 0), (0, 0), (64, 0))).reshape(D_MODEL, -1)
    wt_pool, wt_kv, w_mix_out = full["w_pool_proj"], full["w_kv_up"], full["w_out"]
    pooled, pg, ps = _pool_fwd(z_p, w["pool_grp"], w["pool_scale"], seq)
    br_pool = _mm(ps, wt_pool, "nt", "pool_proj", out_dtype=BF16, tm=t_big, tn=D_MODEL)
    qc, kc, vp, qp, kv = _qk_prep_fwd(qn, wtq_pad, kvn, wt_kv, z_a, pos, g_q, g_kn, g_kr, inv_freq, seq)
    attn, lse = _attn_fwd(qc, kc, vp, seq)
    br_mla = _mm(attn, wtmla_pad, "nt", "mla_proj", out_dtype=BF16, tm=t_mid, tn=D_MODEL)

    def merge(rows):
        zg, bp, bm = (r.astype(F32) for r in rows[:3])
        return (_sigmoid(zg[:, :D_MODEL]) * bp + _sigmoid(zg[:, D_MODEL:]) * bm).astype(BF16)
    x2, o_mix, h3, merged = _out_residual([z_g, br_pool, br_mla], w_mix_out, x1, pm["gate"], p2, seq, "mix_out",
                                          lhs=merge)

    ffn2_w = _gather_wait(h_f2, ("w_ffn2_in", "w_ffn2_out"), "ffn2", h3)
    g2, u2, a2 = _ffn_in_act(h3, ffn2_w["w_ffn2_in"], "ffn2_in")
    dy, df2, dgate2, sq_err = _out_loss(a2, ffn2_w["w_ffn2_out"], x2, p2["gate"],
                                        loss_target.reshape(tokens, D_MODEL), seq, "ffn2_out")
    saved2 = (x2, h3, g2, u2, a2, ffn2_w["w_ffn2_in"], ffn2_w["w_ffn2_out"])

    grads = {}
    (dx2, do_mix, dsh2, dsc2, dgate_m, dg_ffn2), ops2 = _ffn_bwd_x(df2, dy, saved2, p2, seq, "ffn2", (o_mix, pm["gate"]))
    grads["w_ffn2_out"], grads["w_ffn2_in"] = _ffn_bwd_wout(ops2[0], ops2[1], "ffn2"), _ffn_bwd_win(ops2, "ffn2")
    s_f2, tok = _scatter_start(grads, ("w_ffn2_in", "w_ffn2_out"), "ffn2")

    grads["w_out"] = _mm(merged, do_mix, "tn", "mix_bwd_wout", out_dtype=BF16, tm=512, tn=D_MODEL)

    def merge_bwd(rows, bats, vecs):
        dmv, zg, bp, bm = (r.astype(F32) for r in rows)
        s_p, s_m = _sigmoid(zg[:, :D_MODEL]), _sigmoid(zg[:, D_MODEL:])
        dzg = jnp.concatenate([dmv * bp * s_p * (1.0 - s_p), dmv * bm * s_m * (1.0 - s_m)], axis=1)
        return [dmv * s_p, dmv * s_m, dzg], [], []
    dbr_pool, dbr_mla, dz_g = _rowmap("mix_bwd_dmerged", merge_bwd, seq, [do_mix, z_g, br_pool, br_mla],
                                      row_outs=[(D_MODEL, BF16), (D_MODEL, BF16), (2 * D_MODEL, BF16)],
                                      mm=(w_mix_out, "nt"))

    grads["w_pool_proj"] = _mm(dbr_pool, ps, "tn", "pool_bwd_wproj", out_dtype=BF16, tm=512, tn=POOL_WIDTH)
    dps = _mm(dbr_pool, wt_pool, "nn", "pool_bwd_dps", tm=t_big, tn=POOL_WIDTH)
    dz_p, dgrp, dpool_scale = _pool_bwd(dps, pooled, pg, w["pool_grp"], w["pool_scale"] + tok[0:1, 0:1], seq)

    dwtmla_pad = _mm(dbr_mla, attn, "tn", "mla_bwd_wproj", out_dtype=BF16, tm=512, tn=D_MODEL)
    grads["w_mla_proj"] = dwtmla_pad.reshape(D_MODEL, N_HEADS, HEAD_SLAB)[:, :, 64:].reshape(D_MODEL, -1)
    d_attn = _mm(dbr_mla, wtmla_pad, "nn", "mla_bwd_dattn", out_dtype=BF16, tm=t_mid, tn=D_MODEL)
    dqc, dkc, dvp = _attn_bwd(qc, kc, vp, attn, lse, d_attn, seq)
    dqp, dkv, dkr, dg_q, dg_kn, dg_kr = _qk_prep_bwd(dqc, dkc, dvp, qp, kv, z_a, pos, g_q, g_kn, g_kr, inv_freq, seq)
    dwtq_pad = _mm(dqp, qn, "tn", "q_up_bwd_w", out_dtype=BF16, tm=512, tn=Q_LORA)
    grads["w_q_up"] = dwtq_pad.reshape(N_HEADS, HEAD_SLAB, Q_LORA)[:, :96].reshape(-1, Q_LORA)
    grads["w_kv_up"] = _mm(dkv, kvn, "tn", "kv_up_bwd_w", out_dtype=BF16, tm=512, tn=KV_LORA)
    dz_a, dg_qa, dg_kva = _latent_norm_bwd(dqp, wtq_pad, dkv, wt_kv, dkr, z_a, w["q_a_norm"], w["kv_a_norm"], seq)

    dwt_a = _mm(dz_a, h2, "tn", "mix_in_bwd_wa", out_dtype=BF16, tm=256, tn=D_MODEL)
    dwt_p = _mm(dz_p, h2, "tn", "mix_in_bwd_wp", out_dtype=BF16, tm=512, tn=D_MODEL)
    dwt_g = _mm(dz_g, h2, "tn", "mix_in_bwd_wg", out_dtype=BF16, tm=512, tn=D_MODEL)
    grads["w_in"] = jnp.concatenate([dwt_p, dwt_a[:640], dwt_a[704:736], dwt_g], axis=0)

    small_early = [dg_ffn2.reshape(w["norm_ffn2"].shape), dgrp, dpool_scale, dg_qa, dg_kva, dg_q[:, :QK_NOPE],
                   dg_q[:, QK_NOPE:QK_NOPE + QK_ROPE], dg_kn[:, :QK_NOPE], dg_kr[:, QK_NOPE:QK_NOPE + QK_ROPE]]
    s_small, tok = _exchange_start([_pack_small(small_early)], "gather_small_start")
    s_mix, tok = _scatter_start(grads, MIXER, "mix", after=tok)
    dh2 = [(dz_a, wt_a), (dz_p, wt_p), (dz_g, wt_g)]
    pm_tied = dict(pm, scale=pm["scale"] + tok[0:1, 0:1])
    dx1, df1, dsh_m, dsc_m, dgate1, dg_mix = _norm_mod_bwd(dh2, x1, dx2, pm_tied, seq, "mix_bwd_norm", (f1, p1["gate"]))

    handles = {}

    def ffn1_early(a, df):
        grads["w_ffn1_out"] = _ffn_bwd_wout(a, df, "ffn1")
        handles["f1o"], token = _scatter_start(grads, ("w_ffn1_out",), "ffn1_out")
        return token

    def ffn1_mid(operands):
        first = _ffn_bwd_win(operands, "ffn1", half=0)
        handles["f1i0"], token = _exchange_start([first.reshape(N_DEV, -1, first.shape[1])],
                                                 "scatter_ffn1_in0_start", scatter=True)
        return token

    (dx0, dsh1, dsc1, dg_ffn1), ops1 = _ffn_bwd_x(df1, dx1, saved1, p1, seq, "ffn1", early=ffn1_early,
                                                     mid=ffn1_mid)
    s_f1o = handles["f1o"]

    dmod = jnp.stack([jnp.stack([dsh1, dsc1, 0.5 * dgate1], axis=1),
                      jnp.stack([dsh_m, dsc_m, dgate_m], axis=1),
                      jnp.stack([dsh2, dsc2, 0.5 * dgate2], axis=1)], axis=1)
    n_dmod = nseq * 9 * D_MODEL // LANES
    tail = _all_gather(jnp.concatenate([dmod.reshape(-1, LANES), _flat_rows(dg_ffn1), _flat_rows(dg_mix),
                                        _flat_rows(sq_err)], axis=0), "gather_dmod")
    dmod_all = tail[:, :n_dmod].reshape(N_DEV * nseq, 9 * D_MODEL)

    second = _ffn_bwd_win(ops1, "ffn1", after=tail, half=1)
    s_second, tok = _exchange_start([second.reshape(N_DEV, -1, second.shape[1])], "scatter_ffn1_in1_start",
                                    scatter=True, after=tail)
    s_f1i = (handles["f1i0"], s_second)

    dmod_cols = lax.dynamic_slice(dmod_all, (0, me * n_ada), (N_DEV * nseq, n_ada)) + tok[0:1, 0:1]
    g_w_ada, g_b_ada = _ada_grads(c_act, dmod_all, dmod_cols)
    tail_sum = _sum_blocks(tail[:, n_dmod:], "sum_tail")
    g_norm_ffn1 = tail_sum[:SUBLANES].reshape(1, D_MODEL)
    g_norm_mix = tail_sum[SUBLANES:2 * SUBLANES].reshape(1, D_MODEL)
    loss = 0.5 * jnp.sum(tail_sum[2 * SUBLANES:]) * (1.0 / D_MODEL)
    small_all = _exchange_wait(s_small, "gather_small_wait", after=g_b_ada)[0]
    small_sum = _sum_blocks(small_all, "sum_small")
    small = dict(zip(SMALL[2:], _unpack_small(small_sum, [w[n] for n in SMALL[2:]])))
    grad_w = dict(small, w_ada=g_w_ada, b_ada=g_b_ada, norm_ffn1=g_norm_ffn1, norm_mix=g_norm_mix)

    delta, new_m, new_v = {}, {}, {}

    def update(names, landed=None):
        for n in names:
            if landed is None:
                delta[n], new_m[n], new_v[n] = _adamw(w[n], grad_w[n], m[n], v[n], f"adamw_{n}")
            elif n in KEPT_TRANSPOSED:
                res = _adamw_landed(w[n].T, landed[n], m[n].T, v[n].T, f"adamw_{n}")
                grad_w[n], delta[n], new_m[n], new_v[n] = (r.T for r in res)
            elif n in ROW_SHARDED:
                grad_w[n], delta[n], new_m[n], new_v[n] = _adamw_landed(w[n], landed[n], m[n], v[n], f"adamw_{n}")
            else:
                grad_w[n] = _sum_blocks(landed[n][0], f"sum_{n}").T
                delta[n], new_m[n], new_v[n] = _adamw(w[n], grad_w[n], m[n], v[n], f"adamw_{n}")

    update(("w_ada",))
    rep = ("b_ada",) + SMALL
    d_s, m_s, v_s = _adamw(_pack_small([w[n] for n in rep]), _pack_small([grad_w[n] for n in rep]),
                           _pack_small([m[n] for n in rep]), _pack_small([v[n] for n in rep]), "adamw_small")
    like = [w[n] for n in rep]
    for dst, packed in ((delta, d_s), (new_m, m_s), (new_v, v_s)):
        dst.update(zip(rep, _unpack_small(packed, like)))
    update(("w_ffn2_in", "w_ffn2_out"), _scatter_wait(s_f2, ("w_ffn2_in", "w_ffn2_out"), "ffn2", after=d_s))
    update(MIXER, _scatter_wait(s_mix, MIXER, "mix", after=delta["w_ffn2_out"]))
    update(("w_ffn1_out",), _scatter_wait(s_f1o, ("w_ffn1_out",), "ffn1_out", after=delta["w_out"]))
    halves = [_exchange_wait(h, f"scatter_ffn1_in{i}_wait", scatter=True, after=delta["w_ffn1_out"])[0]
              for i, h in enumerate(s_f1i)]
    update(("w_ffn1_in",), {"w_ffn1_in": halves})

    lead = lambda d: [d[n].reshape(shapes[n]) for n in WEIGHTS]
    return (loss, dx0.reshape(x.shape), *lead(grad_w), *lead(delta), *lead(new_m), *lead(new_v))


def kernel(x, c, positions, w_ada, b_ada, norm_ffn1, w_ffn1_in, w_ffn1_out, norm_mix, w_in, pool_grp, pool_scale, w_pool_proj, q_a_norm, w_q_up, kv_a_norm, w_kv_up, q_norm_nope, q_norm_rope, k_norm_nope, k_norm_rope, w_mla_proj, w_out, norm_ffn2, w_ffn2_in, w_ffn2_out, loss_target, m_w_ada, m_b_ada, m_norm_ffn1, m_w_ffn1_in, m_w_ffn1_out, m_norm_mix, m_w_in, m_pool_grp, m_pool_scale, m_w_pool_proj, m_q_a_norm, m_w_q_up, m_kv_a_norm, m_w_kv_up, m_q_norm_nope, m_q_norm_rope, m_k_norm_nope, m_k_norm_rope, m_w_mla_proj, m_w_out, m_norm_ffn2, m_w_ffn2_in, m_w_ffn2_out, v_w_ada, v_b_ada, v_norm_ffn1, v_w_ffn1_in, v_w_ffn1_out, v_norm_mix, v_w_in, v_pool_grp, v_pool_scale, v_w_pool_proj, v_q_a_norm, v_w_q_up, v_kv_a_norm, v_w_kv_up, v_q_norm_nope, v_q_norm_rope, v_k_norm_nope, v_k_norm_rope, v_w_mla_proj, v_w_out, v_norm_ffn2, v_w_ffn2_in, v_w_ffn2_out):
    w = dict(w_ada=w_ada, b_ada=b_ada, norm_ffn1=norm_ffn1, w_ffn1_in=w_ffn1_in, w_ffn1_out=w_ffn1_out, norm_mix=norm_mix, w_in=w_in, pool_grp=pool_grp, pool_scale=pool_scale, w_pool_proj=w_pool_proj, q_a_norm=q_a_norm, w_q_up=w_q_up, kv_a_norm=kv_a_norm, w_kv_up=w_kv_up, q_norm_nope=q_norm_nope, q_norm_rope=q_norm_rope, k_norm_nope=k_norm_nope, k_norm_rope=k_norm_rope, w_mla_proj=w_mla_proj, w_out=w_out, norm_ffn2=norm_ffn2, w_ffn2_in=w_ffn2_in, w_ffn2_out=w_ffn2_out)
    m = dict(w_ada=m_w_ada, b_ada=m_b_ada, norm_ffn1=m_norm_ffn1, w_ffn1_in=m_w_ffn1_in, w_ffn1_out=m_w_ffn1_out, norm_mix=m_norm_mix, w_in=m_w_in, pool_grp=m_pool_grp, pool_scale=m_pool_scale, w_pool_proj=m_w_pool_proj, q_a_norm=m_q_a_norm, w_q_up=m_w_q_up, kv_a_norm=m_kv_a_norm, w_kv_up=m_w_kv_up, q_norm_nope=m_q_norm_nope, q_norm_rope=m_q_norm_rope, k_norm_nope=m_k_norm_nope, k_norm_rope=m_k_norm_rope, w_mla_proj=m_w_mla_proj, w_out=m_w_out, norm_ffn2=m_norm_ffn2, w_ffn2_in=m_w_ffn2_in, w_ffn2_out=m_w_ffn2_out)
    v = dict(w_ada=v_w_ada, b_ada=v_b_ada, norm_ffn1=v_norm_ffn1, w_ffn1_in=v_w_ffn1_in, w_ffn1_out=v_w_ffn1_out, norm_mix=v_norm_mix, w_in=v_w_in, pool_grp=v_pool_grp, pool_scale=v_pool_scale, w_pool_proj=v_w_pool_proj, q_a_norm=v_q_a_norm, w_q_up=v_w_q_up, kv_a_norm=v_kv_a_norm, w_kv_up=v_w_kv_up, q_norm_nope=v_q_norm_nope, q_norm_rope=v_q_norm_rope, k_norm_nope=v_k_norm_nope, k_norm_rope=v_k_norm_rope, w_mla_proj=v_w_mla_proj, w_out=v_w_out, norm_ffn2=v_norm_ffn2, w_ffn2_in=v_w_ffn2_in, w_ffn2_out=v_w_ffn2_out)
    return _step(x, c, positions, w, m, v, loss_target)
```

```python
import functools
import math

import jax
import jax.numpy as jnp
from jax import lax
from jax.experimental import pallas as pl
from jax.experimental.pallas import tpu as pltpu

F32 = jnp.float32
BF16 = jnp.bfloat16
MESH = pl.DeviceIdType.MESH
AXES = ("x", "y", "c")
N_DEV = 8

D_MODEL = 1024
D_FF = 2816
N_HEADS = 8
HEAD_SLAB = 128
QK_NOPE = 64
QK_ROPE = 32
POOL_WIDTH = 512
POOL_GROUPS = 4
POOL_GROUP_DIM = 128
Q_LORA = 384
KV_LORA = 256
ROPE_THETA = 10000.0
ATTN_SCALE = 1.0 / math.sqrt(QK_NOPE + QK_ROPE)
NORM_EPS = 1e-6
ADAM_LR, ADAM_B1, ADAM_B2, ADAM_EPS, ADAM_WD, ADAM_STEP = 0.001, 0.9, 0.999, 1e-08, 0.01, 10

LANES = 128
SUBLANES = 8
VMEM_LIMIT = 52 * 1024 * 1024
ADAMW_WHOLE_BYTES = 3 << 19
SUM_WHOLE_BYTES = 4 << 20

BIG = ("w_ffn1_in", "w_ffn1_out", "w_in", "w_pool_proj", "w_q_up", "w_kv_up",
       "w_mla_proj", "w_out", "w_ffn2_in", "w_ffn2_out")
ROW_SHARDED = ("w_ffn1_out", "w_out", "w_ffn2_out")
MIXER = ("w_in", "w_pool_proj", "w_q_up", "w_kv_up", "w_mla_proj", "w_out")
KEPT_TRANSPOSED = ("w_ffn1_in", "w_ffn2_in", "w_in", "w_q_up")
SMALL = ("norm_ffn1", "norm_mix", "norm_ffn2", "pool_grp", "pool_scale", "q_a_norm",
         "kv_a_norm", "q_norm_nope", "q_norm_rope", "k_norm_nope", "k_norm_rope")
WEIGHTS = ("w_ada", "b_ada", "norm_ffn1", "w_ffn1_in", "w_ffn1_out", "norm_mix", "w_in",
           "pool_grp", "pool_scale", "w_pool_proj", "q_a_norm", "w_q_up", "kv_a_norm",
           "w_kv_up", "q_norm_nope", "q_norm_rope", "k_norm_nope", "k_norm_rope",
           "w_mla_proj", "w_out", "norm_ffn2", "w_ffn2_in", "w_ffn2_out")


def _params(*sem):
    return pltpu.CompilerParams(dimension_semantics=sem, vmem_limit_bytes=VMEM_LIMIT)


def _tile(n, cands):
    for c in cands:
        if n % c == 0:
            return c
    return n


def _my_pos():
    return lax.axis_index("x"), lax.axis_index("y"), lax.axis_index("c")


def _flip(pos, k):
    x, y, c = pos
    fx, fy, fc = (k >> 2) & 1, (k >> 1) & 1, k & 1
    return ((1 - x) if fx else x, (1 - y) if fy else y, (1 - c) if fc else c)


def _index(pos):
    x, y, c = pos
    return 4 * x + 2 * y + c


def _exchange(arrays, name, scatter=False):
    n = len(arrays)

    def body(*refs):
        ins, outs = refs[:n], refs[n:2 * n]
        send_sems, recv_sems, local_sems = refs[2 * n:]
        me = _my_pos()
        mine, sends = [], []
        for a in range(n):
            own = ins[a].at[_index(me)] if scatter else ins[a]
            cp = pltpu.make_async_copy(own, outs[a].at[_index(me)], local_sems.at[a])
            cp.start()
            mine.append(cp)
        for k in range(1, N_DEV):
            peer = _flip(me, k)
            for a in range(n):
                cp = pltpu.make_async_remote_copy(
                    src_ref=ins[a].at[_index(peer)] if scatter else ins[a],
                    dst_ref=outs[a].at[_index(me)],
                    send_sem=send_sems.at[a, k - 1], recv_sem=recv_sems.at[a, k - 1],
                    device_id=peer, device_id_type=MESH)
                cp.start()
                sends.append(cp)
        for k in range(1, N_DEV):
            peer = _flip(me, k)
            for a in range(n):
                pltpu.make_async_remote_copy(
                    src_ref=ins[a].at[_index(me)] if scatter else ins[a],
                    dst_ref=outs[a].at[_index(peer)],
                    send_sem=send_sems.at[a, k - 1], recv_sem=recv_sems.at[a, k - 1],
                    device_id=peer, device_id_type=MESH).wait_recv()
        for cp in sends:
            cp.wait_send()
        for cp in mine:
            cp.wait()

    shape = lambda x: x.shape if scatter else (N_DEV,) + x.shape
    return pl.pallas_call(
        body, name=name,
        out_shape=tuple(jax.ShapeDtypeStruct(shape(x), x.dtype) for x in arrays),
        in_specs=[pl.BlockSpec(memory_space=pl.ANY)] * n,
        out_specs=tuple(pl.BlockSpec(memory_space=pl.ANY) for _ in arrays),
        scratch_shapes=[pltpu.SemaphoreType.DMA((n, N_DEV - 1)),
                        pltpu.SemaphoreType.DMA((n, N_DEV - 1)),
                        pltpu.SemaphoreType.DMA((n,))],
    )(*arrays)


def _all_gather(x, name):
    return _exchange([x], name)[0]


_HBM = pl.BlockSpec(memory_space=pltpu.HBM)
_SEM = pl.BlockSpec(memory_space=pltpu.SEMAPHORE)
_ANY = pl.BlockSpec(memory_space=pl.ANY)
_EFFECT = pltpu.SideEffectType.DATAFLOW_SIDE_EFFECTING


def _split_copy(ins, lands, send_sems, recv_sems, a, k, me, scatter, incoming):
    peer = _flip(me, k)
    block = me if incoming else peer
    return pltpu.make_async_remote_copy(
        src_ref=ins[a].at[_index(block)] if scatter else ins[a],
        dst_ref=lands[a].at[_index(peer if incoming else me)],
        send_sem=send_sems.at[a * (N_DEV - 1) + k - 1], recv_sem=recv_sems.at[a * (N_DEV - 1) + k - 1],
        device_id=peer, device_id_type=MESH)


ALL_PEERS = tuple(range(1, N_DEV))
CHIP_PEERS = (1, 2, 4, 6)


def _exchange_start_groups(groups, name, scatter=False, after=None, peers=None):
    peers = peers or [ALL_PEERS] * len(groups)
    sizes = [len(g) for g in groups]
    first = [sum(sizes[:i]) for i in range(len(sizes))]
    n, ng = sum(sizes), len(sizes)
    after = jnp.zeros((SUBLANES, LANES), F32) if after is None else after

    def body(*refs):
        ins, lands = refs[:n], refs[n:2 * n]
        sems = refs[2 * n + 1:2 * n + 1 + 2 * ng]
        me = _my_pos()
        for g in range(ng):
            part = slice(first[g], first[g] + sizes[g])
            for k in peers[g]:
                for a in range(sizes[g]):
                    _split_copy(ins[part], lands[part], sems[2 * g], sems[2 * g + 1], a, k, me, scatter, False).start()
        refs[-1][...] = jnp.zeros((SUBLANES, LANES), F32)

    shape = lambda x: x.shape if scatter else (N_DEV,) + x.shape
    hbm = lambda x: pltpu.with_memory_space_constraint(x, pltpu.HBM)
    srcs = [hbm(x) for g in groups for x in g]
    zones = [hbm(lax.empty(shape(x), x.dtype)) for g in groups for x in g]
    sem_shapes = [pltpu.SemaphoreType.DMA((s * (N_DEV - 1),)) for s in sizes for _ in range(2)]
    out = pl.pallas_call(
        body, name=name,
        out_shape=(*sem_shapes, *[pltpu.HBM(x.shape, x.dtype) for x in srcs + zones],
                   jax.ShapeDtypeStruct((SUBLANES, LANES), F32)),
        in_specs=[_HBM] * (2 * n) + [_ANY],
        out_specs=(*[_SEM] * (2 * ng), *[_HBM] * (2 * n), pl.BlockSpec(memory_space=pltpu.VMEM)),
        input_output_aliases={i: 2 * ng + i for i in range(2 * n)},
        compiler_params=pltpu.CompilerParams(has_side_effects=_EFFECT),
    )(*srcs, *zones, after)
    bufs = out[2 * ng:-1]
    handles = [(out[2 * g], out[2 * g + 1], *bufs[first[g]:first[g] + sizes[g]],
                *bufs[n + first[g]:n + first[g] + sizes[g]]) for g in range(ng)]
    return handles, out[-1]


def _exchange_start(arrays, name, scatter=False, after=None):
    handles, token = _exchange_start_groups([arrays], name, scatter, after)
    return handles[0], token


def _exchange_wait(handle, name, scatter=False, after=None, peers=ALL_PEERS):
    send_sems, recv_sems = handle[0], handle[1]
    n = (len(handle) - 2) // 2
    after = jnp.zeros((SUBLANES, LANES), F32) if after is None else after

    def body(*refs):
        ins, lands = refs[:n], refs[n:2 * n]
        send, recv = refs[2 * n], refs[2 * n + 1]
        me = _my_pos()
        for k in peers:
            for a in range(n):
                _split_copy(ins, lands, send, recv, a, k, me, scatter, False).wait_send()
                _split_copy(ins, lands, send, recv, a, k, me, scatter, True).wait_recv()

    bufs = handle[2:]
    out = pl.pallas_call(
        body, name=name,
        out_shape=tuple(pltpu.HBM(x.shape, x.dtype) for x in bufs),
        in_specs=[_HBM] * (2 * n) + [_SEM, _SEM, _ANY],
        out_specs=tuple([_HBM] * (2 * n)),
        input_output_aliases={i: i for i in range(2 * n)},
        compiler_params=pltpu.CompilerParams(has_side_effects=_EFFECT),
    )(*bufs, send_sems, recv_sems, after)
    me = _index(_my_pos())
    landed = []
    for src, land in zip(out[:n], out[n:]):
        own = lax.dynamic_slice_in_dim(src, me, 1, axis=0) if scatter else src[None]
        landed.append(lax.dynamic_update_slice_in_dim(land, own, me, axis=0))
    return landed


def _sibling_forward(x, name):
    flips = [k for k in CHIP_PEERS if k != 1]

    def body(x_ref, o_ref, send_sems, recv_sems):
        me = _my_pos()
        sibling = _flip(me, 1)
        sends = []
        for i, k in enumerate(flips):
            block = o_ref.at[_index(_flip(me, k))]
            cp = pltpu.make_async_remote_copy(src_ref=block, dst_ref=block, send_sem=send_sems.at[i],
                                              recv_sem=recv_sems.at[i], device_id=sibling, device_id_type=MESH)
            cp.start()
            sends.append(cp)
        for i, k in enumerate(flips):
            block = o_ref.at[_index(_flip(sibling, k))]
            pltpu.make_async_remote_copy(src_ref=block, dst_ref=block, send_sem=send_sems.at[i],
                                         recv_sem=recv_sems.at[i], device_id=sibling, device_id_type=MESH).wait_recv()
        for cp in sends:
            cp.wait_send()

    return pl.pallas_call(
        body, name=name, out_shape=jax.ShapeDtypeStruct(x.shape, x.dtype),
        in_specs=[_ANY], out_specs=_ANY, input_output_aliases={0: 0},
        scratch_shapes=[pltpu.SemaphoreType.DMA((len(flips),)), pltpu.SemaphoreType.DMA((len(flips),))],
    )(x)


def _sum_blocks(x, name):
    n, rows, cols = x.shape
    whole = x.size * x.dtype.itemsize <= SUM_WHOLE_BYTES
    tr = rows if whole else _tile(rows, (512, 256, 128, 64, 32, 16, 8))

    def body(x_ref, o_ref):
        acc = x_ref[0].astype(F32)
        for d in range(1, n):
            acc = acc + x_ref[d].astype(F32)
        o_ref[...] = acc

    return pl.pallas_call(
        body, name=name,
        out_shape=jax.ShapeDtypeStruct((rows, cols), F32),
        grid=(rows // tr,),
        in_specs=[pl.BlockSpec((n, tr, cols), lambda i: (0, i, 0))],
        out_specs=pl.BlockSpec((tr, cols), lambda i: (i, 0)),
        compiler_params=_params("parallel"),
    )(x)


_DIMS = {"nn": (((1,), (0,)), ((), ())), "nt": (((1,), (1,)), ((), ())), "tn": (((0,), (0,)), ((), ()))}


def _mm(a, b, mode, name, out_dtype=F32, tm=None, tn=None, add=None, after=None, b_cols=None):
    if mode == "tn":
        kdim, m = a.shape
    else:
        m, kdim = a.shape
    n = b.shape[0] if mode == "nt" else b.shape[1]
    tm = tm or _tile(m, (512, 256, 128))
    tn = tn or _tile(n, (512, 256, 128))
    j0 = 0
    if b_cols is not None:
        j0, n = b_cols[0], b_cols[1] * tn
    dims = _DIMS[mode]

    def body(*refs):
        refs = refs if after is None else refs[1:]
        acc = lax.dot_general(refs[0][...].astype(BF16), refs[1][...].astype(BF16), dims,
                              preferred_element_type=F32)
        if add is not None:
            acc = acc + refs[2][...]
        refs[-1][...] = acc.astype(out_dtype)

    a_spec = (pl.BlockSpec((kdim, tm), lambda i, j: (0, i)) if mode == "tn"
              else pl.BlockSpec((tm, kdim), lambda i, j: (i, 0)))
    b_spec = (pl.BlockSpec((tn, kdim), lambda i, j: (j, 0)) if mode == "nt"
              else pl.BlockSpec((kdim, tn), lambda i, j: (0, j + j0)))
    o_spec = pl.BlockSpec((tm, tn), lambda i, j: (i, j))
    in_specs, args = [a_spec, b_spec], [a, b]
    if add is not None:
        in_specs.append(o_spec)
        args.append(add)
    if after is not None:
        in_specs.insert(0, _ANY)
        args.insert(0, after)
    return pl.pallas_call(
        body, name=name, out_shape=jax.ShapeDtypeStruct((m, n), out_dtype), grid=(m // tm, n // tn),
        in_specs=in_specs, out_specs=o_spec,
        compiler_params=_params("parallel", "parallel"),
    )(*args)


def _rowmap(name, fn, seq, rows, bats=(), vecs=(), row_outs=(), bat_outs=(), vec_outs=(), ts=None, mm=None, lhs=None,
            after=None, mm_sum=True):
    mms = [] if mm is None else (mm if isinstance(mm, list) else [mm])
    rows = [r if isinstance(r, tuple) else (r, r.shape[1], 0) for r in rows]
    tokens = rows[0][0].shape[0]
    nseq = tokens // seq
    ts = ts or _tile(seq, (512, 256, 128, 64, 32, 16, 8))
    nt = seq // ts
    n_r, n_b, n_v = len(rows), len(bats), len(vecs)
    n_ro, n_bo = len(row_outs), len(bat_outs)

    def accumulate(ref, val, first):
        @pl.when(first)
        def _():
            ref[...] = val.reshape(ref.shape)

        @pl.when(jnp.logical_not(first))
        def _():
            ref[...] += val.reshape(ref.shape)

    def body(*refs):
        n_in = n_r + n_b + n_v + len(mms) + (after is not None)
        ins, outs = refs[:n_in], refs[n_in:]
        b_vals = [r[0] for r in ins[n_r:n_r + n_b]]
        v_vals = [r[...] for r in ins[n_r + n_b:n_r + n_b + n_v]]
        r_vals = [r[...] for r in ins[:n_r]]
        if mms:
            lefts = r_vals[:len(mms)] if lhs is None else [lhs(r_vals, v_vals)] * len(mms)
            parts = [lax.dot_general(left.astype(BF16), b_ref[...].astype(BF16), _DIMS[mode],
                                     preferred_element_type=F32)
                     for left, b_ref, (_, mode) in zip(lefts, ins[n_r + n_b + n_v:], mms)]
            accs = [functools.reduce(lambda x, y: x + y, parts)] if mm_sum else parts
            r_vals = accs + r_vals[len(mms):] if lhs is None else accs + [lefts[0]] + r_vals
        ro, bo, vo = fn(r_vals, b_vals, v_vals)
        for ref, val in zip(outs[:n_ro], ro):
            ref[...] = val.astype(ref.dtype)
        b, i = pl.program_id(0), pl.program_id(1)
        for ref, val in zip(outs[n_ro:n_ro + n_bo], bo):
            accumulate(ref, val, i == 0)
        for ref, val in zip(outs[n_ro + n_bo:], vo):
            accumulate(ref, val, jnp.logical_and(i == 0, b == 0))

    in_specs = [pl.BlockSpec((ts, w), functools.partial(lambda b, i, cb: (b * nt + i, cb), cb=cb))
                for _, w, cb in rows]
    in_specs += [pl.BlockSpec((1, 1, v.shape[2]), lambda b, i: (b, 0, 0)) for v in bats]
    in_specs += [pl.BlockSpec(v.shape, lambda b, i: (0, 0)) for v in vecs]
    extra = [b_arr for b_arr, _ in mms]
    in_specs += [pl.BlockSpec(b_arr.shape, lambda b, i: (0, 0), pipeline_mode=pl.Buffered(1)) for b_arr in extra]
    if after is not None:
        in_specs.append(_ANY)
        extra.append(after)
    out_shape = [jax.ShapeDtypeStruct((tokens, f), dt) for f, dt in row_outs]
    out_specs = [pl.BlockSpec((ts, f), lambda b, i: (b * nt + i, 0)) for f, _ in row_outs]
    out_shape += [jax.ShapeDtypeStruct((nseq, 1, f), F32) for f in bat_outs]
    out_specs += [pl.BlockSpec((1, 1, f), lambda b, i: (b, 0, 0)) for f in bat_outs]
    out_shape += [jax.ShapeDtypeStruct((1, f), F32) for f in vec_outs]
    out_specs += [pl.BlockSpec((1, f), lambda b, i: (0, 0)) for f in vec_outs]
    return pl.pallas_call(
        body, name=name, out_shape=tuple(out_shape), grid=(nseq, nt),
        in_specs=in_specs, out_specs=tuple(out_specs),
        compiler_params=_params("arbitrary", "arbitrary"),
    )(*([r[0] for r in rows] + list(bats) + list(vecs) + extra))


def _colsum(v):
    return jnp.sum(v, axis=0, keepdims=True)


def _rstd(x, width=None):
    width = width or x.shape[-1]
    return lax.rsqrt(jnp.sum(x * x, axis=-1, keepdims=True) * (1.0 / width) + NORM_EPS)


def _norm_bwd(dy, x, r, g, width=None):
    width = width or x.shape[-1]
    xhat = x * r
    dxhat = dy * g
    dx = r * (dxhat - xhat * (jnp.sum(dxhat * xhat, axis=-1, keepdims=True) * (1.0 / width)))
    return dx, dy * xhat


def _sigmoid(x):
    return 0.5 * jnp.tanh(0.5 * x) + 0.5


def _norm_mod(xv, g, sh, sc):
    return xv * _rstd(xv) * g * (1.0 + sc) + sh


def _norm_mod_fwd(x, p, seq, name):
    def fn(rows, bats, vecs):
        return [_norm_mod(rows[0], vecs[0], bats[0], bats[1])], [], []
    return _rowmap(name, fn, seq, [x], [p["shift"], p["scale"]], [p["gamma"]], row_outs=[(D_MODEL, BF16)])[0]


def _norm_mod_bwd(dh, x, dres, p, seq, name, prev=None, after=None):
    products = dh if isinstance(dh, list) else None
    lefts = [l for l, _ in products] if products else [dh]
    def fn(rows, bats, vecs):
        dhv, xv, dr = rows[:3]
        sc, g = bats[0], vecs[0]
        r = _rstd(xv)
        dxn, dg = _norm_bwd(dhv * (1.0 + sc), xv, r, g)
        dx = dr + dxn
        ro, bo = [dx], [_colsum(dhv), _colsum(dhv * (xv * r * g))]
        if prev is not None:
            ro.append(bats[1] * dx)
            bo.append(_colsum(dx * rows[3].astype(F32)))
        return ro, bo, [_colsum(dg)]
    more = prev is not None
    return _rowmap(name, fn, seq, lefts + [x, dres] + ([prev[0]] if more else []),
                   [p["scale"]] + ([prev[1]] if more else []), [p["gamma"]],
                   row_outs=[(D_MODEL, F32)] + ([(D_MODEL, BF16)] if more else []),
                   bat_outs=[D_MODEL] * (3 if more else 2), vec_outs=[D_MODEL],
                   mm=[(r, "nn") for _, r in products] if products else None, after=after)


def _ffn_in_act(h, wt_in, name):
    tokens = h.shape[0]
    tm, tn = _tile(tokens, (2048, 1024, 512)), 256
    nj = D_FF // tn

    def body(h_ref, wg_ref, wu_ref, g_ref, u_ref, a_ref):
        hv = h_ref[...]
        g = lax.dot_general(hv, wg_ref[...], _DIMS["nt"], preferred_element_type=F32)
        u = lax.dot_general(hv, wu_ref[...], _DIMS["nt"], preferred_element_type=F32)
        g_ref[...] = g.astype(BF16)
        u_ref[...] = u.astype(BF16)
        a_ref[...] = (g * _sigmoid(g) * u).astype(BF16)

    o_spec = pl.BlockSpec((tm, tn), lambda i, j: (i, j))
    out = jax.ShapeDtypeStruct((tokens, D_FF), BF16)
    return pl.pallas_call(
        body, name=name, grid=(tokens // tm, nj), out_shape=(out, out, out),
        in_specs=[pl.BlockSpec((tm, D_MODEL), lambda i, j: (i, 0)),
                  pl.BlockSpec((tn, D_MODEL), lambda i, j: (j, 0)),
                  pl.BlockSpec((tn, D_MODEL), lambda i, j: (j + nj, 0))],
        out_specs=(o_spec, o_spec, o_spec),
        compiler_params=_params("parallel", "parallel"),
    )(h, wt_in, wt_in)


def _out_residual(a, w_out, res, gate, nxt, seq, name):
    def fn(rows, bats, vecs):
        acc, rv = rows
        x_new = rv + bats[0] * acc
        return [x_new, acc, _norm_mod(x_new, vecs[0], bats[1], bats[2])], [], []
    return _rowmap(name, fn, seq, [a, res], [gate, nxt["shift"], nxt["scale"]], [nxt["gamma"]],
                   row_outs=[(D_MODEL, F32), (D_MODEL, BF16), (D_MODEL, BF16)],
                   ts=_tile(seq, (512, 256, 128)), mm=(w_out, "nn"))


def _mix_out(z_g, ps, attn, res, wt_pool, wtmla_pad, w_out, gate, nxt, seq):
    branches = []

    def lhs(rows, vecs):
        zg = rows[0].astype(F32)
        bp = lax.dot_general(rows[1], vecs[1], _DIMS["nt"], preferred_element_type=F32)
        bm = lax.dot_general(rows[2], vecs[2], _DIMS["nt"], preferred_element_type=F32)
        branches[:] = [bp, bm]
        return (_sigmoid(zg[:, :D_MODEL]) * bp + _sigmoid(zg[:, D_MODEL:]) * bm).astype(BF16)

    def fn(rows, bats, vecs):
        acc, merged, rv = rows[0], rows[1], rows[-1]
        x_new = rv + bats[0] * acc
        return [x_new, acc, _norm_mod(x_new, vecs[0], bats[1], bats[2]), merged] + branches, [], []
    return _rowmap("mix_out", fn, seq, [z_g, ps, attn, res], [gate, nxt["shift"], nxt["scale"]],
                   [nxt["gamma"], wt_pool, wtmla_pad], row_outs=[(D_MODEL, F32)] + [(D_MODEL, BF16)] * 5,
                   ts=_tile(seq, (512, 256, 128)), mm=(w_out, "nn"), lhs=lhs)


def _out_loss(a, w_out, res, gate, target, seq, name):
    def fn(rows, bats, vecs):
        acc, rv, tv = rows
        err = rv + bats[0] * acc - tv
        dy = err * (1.0 / D_MODEL)
        return [dy, bats[0] * dy], [_colsum(dy * acc)], [_colsum(err * err)]
    return _rowmap(name, fn, seq, [a, res, target], [gate], row_outs=[(D_MODEL, F32), (D_MODEL, BF16)],
                   bat_outs=[D_MODEL], vec_outs=[D_MODEL], ts=_tile(seq, (512, 256, 128)), mm=(w_out, "nn"))


def _ffn_bwd_x(df, dres, saved, p, seq, tag, prev=None, early=None, mid=None):
    x, h, g, u, a, w_in, w_out = saved
    first = None if early is None else early(a, df)

    def act_bwd(rows, bats, vecs):
        dav, gv, uv = rows[0], rows[1].astype(F32), rows[2].astype(F32)
        sg = _sigmoid(gv)
        silu = gv * sg
        dg = dav * uv * (sg * (1.0 + gv * (1.0 - sg)))
        return [jnp.concatenate([dg, dav * silu], axis=1)], [], []
    dgu = _rowmap(f"{tag}_bwd_da", act_bwd, seq, [df, g, u], row_outs=[(2 * D_FF, BF16)],
                  ts=_tile(seq, (512, 256, 128)), mm=(w_out, "nt"), after=first)[0]
    operands = (a, df, dgu, h)
    after = None if mid is None else mid(operands)
    return _norm_mod_bwd([(dgu, w_in)], x, dres, p, seq, f"{tag}_bwd_norm", prev, after=after), operands


def _ffn_bwd_wout(a, df, tag):
    return _mm(a, df, "tn", f"{tag}_bwd_wout", out_dtype=BF16, tm=256, tn=D_MODEL)


def _ffn_bwd_win(operands, tag, after=None, half=None):
    _, _, dgu, h = operands
    if half is None:
        return _mm(dgu, h, "tn", f"{tag}_bwd_win", out_dtype=BF16, tm=512, tn=D_MODEL, after=after)
    return _mm(dgu, h, "tn", f"{tag}_bwd_win{half}", out_dtype=BF16, tm=512, tn=D_MODEL // 2, after=after,
               b_cols=(half, 1))


def _shift_rows(v, k, forward):
    n = v.shape[0]
    row = lax.broadcasted_iota(jnp.int32, v.shape, 0)
    if forward:
        return jnp.where(row >= k, pltpu.roll(v, k, 0), 0.0)
    return jnp.where(row < n - k, pltpu.roll(v, n - k, 0), 0.0)


def _window_sums(v, forward):
    out, s, k = [], v, 1
    for _ in range(POOL_GROUPS):
        s = s + _shift_rows(s, k, forward)
        out.append(s)
        k *= 2
    return out


def _by_group(vals, g):
    out = vals[-1]
    for idx in range(len(vals) - 2, -1, -1):
        out = jnp.where(g == idx, vals[idx], out)
    return out


def _inv_count(shape, g):
    t1 = lax.broadcasted_iota(jnp.int32, shape, 0) + 1
    window = _by_group([jnp.int32(2 ** (i + 1)) for i in range(POOL_GROUPS)], g)
    return 1.0 / jnp.minimum(t1, window).astype(F32)


def _pool_fwd(u, grp, scale, seq):
    tokens = u.shape[0]

    def body(u_ref, grp_ref, sc_ref, pooled_ref, pg_ref, ps_ref):
        g = pl.program_id(1)
        uv = u_ref[...]
        sums = _by_group(_window_sums(uv, True), g)
        pooled = (sums * _inv_count(uv.shape, g) - uv).astype(BF16)
        pg = jnp.dot(pooled, grp_ref[0].astype(BF16), preferred_element_type=F32)
        pooled_ref[...] = pooled
        pg_ref[...] = pg
        ps_ref[...] = (pg * sc_ref[...]).astype(BF16)

    blk = pl.BlockSpec((seq, POOL_GROUP_DIM), lambda b, g: (b, g))
    return pl.pallas_call(
        body, name="pool_fwd", grid=(tokens // seq, POOL_GROUPS),
        out_shape=(jax.ShapeDtypeStruct(u.shape, BF16), jax.ShapeDtypeStruct(u.shape, F32),
                   jax.ShapeDtypeStruct(u.shape, BF16)),
        in_specs=[blk, pl.BlockSpec((1, POOL_GROUP_DIM, POOL_GROUP_DIM), lambda b, g: (g, 0, 0)),
                  pl.BlockSpec((1, POOL_GROUP_DIM), lambda b, g: (0, g))],
        out_specs=(blk, blk, blk),
        compiler_params=_params("parallel", "parallel"),
    )(u, grp, scale)


def _pool_bwd(dps, pooled, pg, grp, scale, seq):
    tokens = dps.shape[0]

    def body(dps_ref, pooled_ref, pg_ref, grp_ref, sc_ref, du_ref, dgrp_ref, dsc_ref):
        g, b = pl.program_id(0), pl.program_id(1)
        dpsv = dps_ref[...]
        dpg = (dpsv * sc_ref[...]).astype(BF16)
        dsc = _colsum(dpsv * pg_ref[...])
        dgrp = lax.dot_general(pooled_ref[...], dpg, _DIMS["tn"], preferred_element_type=F32)

        @pl.when(b == 0)
        def _():
            dsc_ref[...] = dsc
            dgrp_ref[0] = dgrp

        @pl.when(b > 0)
        def _():
            dsc_ref[...] += dsc
            dgrp_ref[0] += dgrp

        dpool = lax.dot_general(dpg, grp_ref[0].astype(BF16), _DIMS["nt"], preferred_element_type=F32)
        sums = _by_group(_window_sums(dpool * _inv_count(dpool.shape, g), False), g)
        du_ref[...] = (sums - dpool).astype(BF16)

    blk = pl.BlockSpec((seq, POOL_GROUP_DIM), lambda g, b: (b, g))
    grp_spec = pl.BlockSpec((1, POOL_GROUP_DIM, POOL_GROUP_DIM), lambda g, b: (g, 0, 0))
    vec_spec = pl.BlockSpec((1, POOL_GROUP_DIM), lambda g, b: (0, g))
    return pl.pallas_call(
        body, name="pool_bwd", grid=(POOL_GROUPS, tokens // seq),
        out_shape=(jax.ShapeDtypeStruct(dps.shape, BF16), jax.ShapeDtypeStruct(grp.shape, F32),
                   jax.ShapeDtypeStruct(scale.shape, F32)),
        in_specs=[blk, blk, blk, grp_spec, vec_spec],
        out_specs=(blk, grp_spec, vec_spec),
        compiler_params=_params("arbitrary", "arbitrary"),
    )(dps, pooled, pg, grp, scale)


def _lane(shape):
    return lax.broadcasted_iota(jnp.int32, shape, len(shape) - 1)


def _rot(y):
    lane = _lane(y.shape)
    r = jnp.where(lane < QK_NOPE + QK_ROPE // 2,
                  -pltpu.roll(y, HEAD_SLAB - QK_ROPE // 2, 1), pltpu.roll(y, QK_ROPE // 2, 1))
    return jnp.where(jnp.logical_and(lane >= QK_NOPE, lane < QK_NOPE + QK_ROPE), r, 0.0)


def _part_rstd(x):
    sq = x * x
    nope = _lane(x.shape) < QK_NOPE
    s_nope = jnp.sum(jnp.where(nope, sq, 0.0), axis=-1, keepdims=True)
    s_rope = jnp.sum(sq, axis=-1, keepdims=True) - s_nope
    return jnp.where(nope, lax.rsqrt(s_nope * (1.0 / QK_NOPE) + NORM_EPS),
                     lax.rsqrt(s_rope * (1.0 / QK_ROPE) + NORM_EPS))


def _part_norm_bwd(dy, x, r, g):
    nope = _lane(x.shape) < QK_NOPE
    xhat = x * r
    dxhat = dy * g
    prod = dxhat * xhat
    m_nope = jnp.sum(jnp.where(nope, prod, 0.0), axis=-1, keepdims=True)
    m_rope = jnp.sum(prod, axis=-1, keepdims=True) - m_nope
    mean = jnp.where(nope, m_nope * (1.0 / QK_NOPE), m_rope * (1.0 / QK_ROPE))
    return r * (dxhat - xhat * mean), dy * xhat


def _mixer_in(h, wt_a, wt_p, wt_g, g_q, g_kv, seq):
    def fn(rows, bats, vecs):
        z_a, z_p, z_g = rows[:3]
        q, kv = z_a[:, :Q_LORA], z_a[:, Q_LORA:Q_LORA + KV_LORA]
        return [z_a, z_p, z_g, q * _rstd(q) * vecs[0], kv * _rstd(kv) * vecs[1]], [], []
    return _rowmap("mix_in", fn, seq, [h], vecs=[g_q, g_kv], lhs=lambda rows, vecs: rows[0],
                   mm=[(wt_a, "nt"), (wt_p, "nt"), (wt_g, "nt")], mm_sum=False,
                   row_outs=[(wt_a.shape[0], F32), (wt_p.shape[0], F32), (wt_g.shape[0], BF16),
                             (Q_LORA, BF16), (KV_LORA, BF16)])


def _latent_norm_bwd(dqp, wtq_pad, dkv, wt_kv, dkr, z_a, g_q, g_kv, seq):
    def fn(rows, bats, vecs):
        dq, dkv, dkrv, z = rows
        q, kv = z[:, :Q_LORA], z[:, Q_LORA:Q_LORA + KV_LORA]
        dxq, dgq = _norm_bwd(dq, q, _rstd(q), vecs[0])
        dxkv, dgkv = _norm_bwd(dkv, kv, _rstd(kv), vecs[1])
        return [jnp.concatenate([dxq, dxkv, dkrv], axis=1)], [], [_colsum(dgq), _colsum(dgkv)]
    return _rowmap("latent_norm_bwd", fn, seq, [dqp, dkv, dkr, z_a], vecs=[g_q, g_kv],
                   row_outs=[(Q_LORA + KV_LORA + HEAD_SLAB, BF16)], vec_outs=[Q_LORA, KV_LORA],
                   mm=[(wtq_pad, "nn"), (wt_kv, "nn")], mm_sum=False)


def _qk_prep_fwd(qn, wtq_pad, kvn, wt_kv, z_a, pos, g_q, g_kn, g_kr, inv_freq, seq):
    def fn(rows, bats, vecs):
        qv, kvv, kr, p = rows
        gq, gkn, gkr, invf = vecs
        ang = p * invf
        cos, sin = jnp.cos(ang), jnp.sin(ang)
        nope = _lane(kr.shape) < QK_NOPE
        krn = kr * _rstd(kr, QK_ROPE) * gkr
        krr = krn * cos + _rot(krn) * sin
        qs, ks, vs = [], [], []
        for h in range(N_HEADS):
            xq = qv[:, h * HEAD_SLAB:(h + 1) * HEAD_SLAB]
            y = xq * _part_rstd(xq) * gq
            qs.append(y * cos + _rot(y) * sin)
            xk = kvv[:, h * HEAD_SLAB:(h + 1) * HEAD_SLAB]
            kn = jnp.where(nope, xk, 0.0)
            ks.append(jnp.where(nope, kn * _rstd(kn, QK_NOPE) * gkn, krr))
            vs.append(jnp.where(nope, 0.0, xk))
        return [jnp.concatenate(v, axis=1) for v in (qs, ks, vs)] + [qv, kvv], [], []
    width = N_HEADS * HEAD_SLAB
    return _rowmap("qk_prep", fn, seq, [qn, kvn, (z_a, HEAD_SLAB, 5), pos], vecs=[g_q, g_kn, g_kr, inv_freq],
                   row_outs=[(width, BF16)] * 3 + [(width, F32)] * 2, mm=[(wtq_pad, "nt"), (wt_kv, "nt")],
                   mm_sum=False)


def _qk_prep_bwd(dqc, dkc, dvp, qp, kv, z_a, pos, g_q, g_kn, g_kr, inv_freq, seq):
    def fn(rows, bats, vecs):
        dq, dk, dv, qv, kvv, kr, p = rows
        gq, gkn, gkr, invf = vecs
        ang = p * invf
        cos, sin = jnp.cos(ang), jnp.sin(ang)
        nope = _lane(kr.shape) < QK_NOPE
        dqs, dkvs = [], []
        dgq = jnp.zeros((1, HEAD_SLAB), F32)
        dgkn = jnp.zeros((1, HEAD_SLAB), F32)
        dkrr = jnp.zeros(kr.shape, F32)
        for h in range(N_HEADS):
            sl = slice(h * HEAD_SLAB, (h + 1) * HEAD_SLAB)
            dyr = dq[:, sl]
            dy = dyr * cos - _rot(dyr * sin)
            xq = qv[:, sl]
            dx, dg = _part_norm_bwd(dy, xq, _part_rstd(xq), gq)
            dqs.append(dx)
            dgq = dgq + _colsum(dg)
            dkh = dk[:, sl]
            dkrr = dkrr + jnp.where(nope, 0.0, dkh)
            kn = jnp.where(nope, kvv[:, sl], 0.0)
            dxk, dgk = _norm_bwd(jnp.where(nope, dkh, 0.0), kn, _rstd(kn, QK_NOPE), gkn, QK_NOPE)
            dgkn = dgkn + _colsum(dgk)
            dkvs.append(jnp.where(nope, dxk, dv[:, sl]))
        dkrn = dkrr * cos - _rot(dkrr * sin)
        dkr, dgkr = _norm_bwd(dkrn, kr, _rstd(kr, QK_ROPE), gkr, QK_ROPE)
        return ([jnp.concatenate(dqs, axis=1), jnp.concatenate(dkvs, axis=1), dkr], [],
                [dgq, dgkn, _colsum(dgkr)])
    width = N_HEADS * HEAD_SLAB
    return _rowmap("qk_prep_bwd", fn, seq, [dqc, dkc, dvp, qp, kv, (z_a, HEAD_SLAB, 5), pos],
                   vecs=[g_q, g_kn, g_kr, inv_freq],
                   row_outs=[(width, BF16), (width, BF16), (HEAD_SLAB, F32)],
                   vec_outs=[HEAD_SLAB] * 3, ts=_tile(seq, (512, 256, 128, 64, 32, 16, 8)))


def _scores(q, k_ref, keys, tq):
    s = lax.dot_general(q, k_ref[0:keys, :], _DIMS["nt"], preferred_element_type=F32) * ATTN_SCALE
    row = lax.broadcasted_iota(jnp.int32, (tq, tq), 0)
    col = lax.broadcasted_iota(jnp.int32, (tq, tq), 1)
    diag = jnp.where(col <= row, s[:, keys - tq:], -1e30)
    return diag if keys == tq else jnp.concatenate([s[:, :keys - tq], diag], axis=1)


def _attn_fwd(qc, kc, vp, seq):
    tokens = qc.shape[0]
    tq = _tile(seq, (256, 128))
    nq = seq // tq

    def body(q_ref, k_ref, v_ref, o_ref, lse_ref):
        for i in range(nq):
            rows, keys = slice(i * tq, (i + 1) * tq), (i + 1) * tq
            s = _scores(q_ref[rows, :], k_ref, keys, tq)
            m = jnp.max(s, axis=-1, keepdims=True)
            p = jnp.exp(s - m)
            l = jnp.sum(p, axis=-1, keepdims=True)
            acc = jnp.dot(p.astype(BF16), v_ref[0:keys, :], preferred_element_type=F32)
            o_ref[rows, :] = (acc / l).astype(BF16)
            lse_ref[rows, :] = jnp.broadcast_to(m + jnp.log(l), (tq, HEAD_SLAB))

    spec = pl.BlockSpec((seq, HEAD_SLAB), lambda b, h: (b, h))
    return pl.pallas_call(
        body, name="attn_fwd", grid=(tokens // seq, N_HEADS),
        out_shape=(jax.ShapeDtypeStruct(qc.shape, BF16), jax.ShapeDtypeStruct(qc.shape, F32)),
        in_specs=[spec] * 3, out_specs=(spec, spec),
        compiler_params=_params("parallel", "parallel"),
    )(qc, kc, vp)


def _attn_bwd(qc, kc, vp, o, lse, do, seq):
    tokens = qc.shape[0]
    tq = _tile(seq, (256, 128))
    nq = seq // tq

    def body(q_ref, k_ref, v_ref, o_ref, lse_ref, do_ref, dq_ref, dk_ref, dv_ref):
        dk_ref[...] = jnp.zeros(dk_ref.shape, F32)
        dv_ref[...] = jnp.zeros(dv_ref.shape, F32)
        for i in range(nq):
            rows, keys = slice(i * tq, (i + 1) * tq), (i + 1) * tq
            q, dov = q_ref[rows, :], do_ref[rows, :]
            delta = jnp.sum(dov.astype(F32) * o_ref[rows, :].astype(F32), axis=-1, keepdims=True)
            s = _scores(q, k_ref, keys, tq)
            p = jnp.exp(s - jnp.tile(lse_ref[rows, :], (1, keys // HEAD_SLAB)))
            dp = lax.dot_general(dov, v_ref[0:keys, :], _DIMS["nt"], preferred_element_type=F32)
            ds = (p * (dp - delta) * ATTN_SCALE).astype(BF16)
            dq_ref[rows, :] = jnp.dot(ds, k_ref[0:keys, :], preferred_element_type=F32)
            dk_ref[0:keys, :] += lax.dot_general(ds, q, _DIMS["tn"], preferred_element_type=F32)
            dv_ref[0:keys, :] += lax.dot_general(p.astype(BF16), dov, _DIMS["tn"], preferred_element_type=F32)

    spec = pl.BlockSpec((seq, HEAD_SLAB), lambda b, h: (b, h))
    out = jax.ShapeDtypeStruct(qc.shape, F32)
    return pl.pallas_call(
        body, name="attn_bwd", grid=(tokens // seq, N_HEADS),
        out_shape=(out, out, out), in_specs=[spec] * 6, out_specs=(spec, spec, spec),
        compiler_params=_params("parallel", "parallel"),
    )(qc, kc, vp, o, lse, do)


def _adamw(w, g, m, v, name):
    rows, cols = w.shape
    whole = rows * cols * 4 <= ADAMW_WHOLE_BYTES
    tr = rows if whole else _tile(rows, (256, 128, 64, 32, 16, 8))
    c1 = 1.0 - ADAM_B1 ** ADAM_STEP
    c2 = 1.0 - ADAM_B2 ** ADAM_STEP

    def body(w_ref, g_ref, m_ref, v_ref, d_ref, nm_ref, nv_ref):
        gv = g_ref[...]
        nm = ADAM_B1 * m_ref[...] + (1.0 - ADAM_B1) * gv
        nv = ADAM_B2 * v_ref[...] + (1.0 - ADAM_B2) * (gv * gv)
        d_ref[...] = -ADAM_LR * ((nm / c1) / (jnp.sqrt(nv / c2) + ADAM_EPS) + ADAM_WD * w_ref[...])
        nm_ref[...] = nm
        nv_ref[...] = nv

    spec = pl.BlockSpec((tr, cols), lambda i: (i, 0))
    out = jax.ShapeDtypeStruct(w.shape, F32)
    return pl.pallas_call(
        body, name=name, grid=(rows // tr,), out_shape=(out, out, out),
        in_specs=[spec] * 4, out_specs=(spec, spec, spec),
        compiler_params=_params("parallel"),
    )(w, g, m, v)


def _adamw_landed(w, landed, m, v, name):
    rows, cols = w.shape
    tr = _tile(rows, (176, 128, 96, 64, 32, 16, 8))
    c1 = 1.0 - ADAM_B1 ** ADAM_STEP
    c2 = 1.0 - ADAM_B2 ** ADAM_STEP
    n_parts = len(landed)

    def body(*refs):
        w_ref, m_ref, v_ref = refs[:3]
        g_ref, d_ref, nm_ref, nv_ref = refs[3 + n_parts:]
        parts = []
        for x_ref in refs[3:3 + n_parts]:
            acc = x_ref[0].astype(F32)
            for d in range(1, N_DEV):
                acc = acc + x_ref[d].astype(F32)
            parts.append(acc)
        gv = parts[0] if n_parts == 1 else jnp.concatenate(parts, axis=1)
        nm = ADAM_B1 * m_ref[...] + (1.0 - ADAM_B1) * gv
        nv = ADAM_B2 * v_ref[...] + (1.0 - ADAM_B2) * (gv * gv)
        g_ref[...] = gv
        d_ref[...] = -ADAM_LR * ((nm / c1) / (jnp.sqrt(nv / c2) + ADAM_EPS) + ADAM_WD * w_ref[...])
        nm_ref[...] = nm
        nv_ref[...] = nv

    spec = pl.BlockSpec((tr, cols), lambda i: (i, 0))
    out = jax.ShapeDtypeStruct(w.shape, F32)
    return pl.pallas_call(
        body, name=name, grid=(rows // tr,), out_shape=(out, out, out, out),
        in_specs=[spec] * 3 + [pl.BlockSpec((N_DEV, tr, x.shape[2]), lambda i: (0, i, 0)) for x in landed],
        out_specs=(spec, spec, spec, spec),
        compiler_params=_params("parallel"),
    )(w, m, v, *landed)


def _mod_cols(c_all, w_ada, b_cols):
    def body(c_ref, w_ref, b_ref, act_ref, mod_ref):
        cv = c_ref[...]
        act = cv * _sigmoid(cv)
        act_ref[...] = act
        mod_ref[...] = jnp.dot(act.astype(BF16), w_ref[...].astype(BF16),
                               preferred_element_type=F32) + b_ref[...]

    n = w_ada.shape[1]
    return pl.pallas_call(
        body, name="mod_cols",
        out_shape=(jax.ShapeDtypeStruct(c_all.shape, F32), jax.ShapeDtypeStruct((c_all.shape[0], n), F32)),
        compiler_params=pltpu.CompilerParams(vmem_limit_bytes=VMEM_LIMIT),
    )(c_all, w_ada, b_cols)


def _ada_grads(c_act, dmod_all, dmod_cols):
    def body(c_ref, d_ref, dc_ref, gw_ref, gb_ref):
        gw_ref[...] = lax.dot_general(c_ref[...].astype(BF16), dc_ref[...].astype(BF16), _DIMS["tn"],
                                      preferred_element_type=F32)
        gb_ref[...] = _colsum(d_ref[...])

    return pl.pallas_call(
        body, name="ada_grads",
        out_shape=(jax.ShapeDtypeStruct((c_act.shape[1], dmod_cols.shape[1]), F32),
                   jax.ShapeDtypeStruct((1, dmod_all.shape[1]), F32)),
        compiler_params=pltpu.CompilerParams(vmem_limit_bytes=VMEM_LIMIT),
    )(c_act, dmod_all, dmod_cols)


def _flat_rows(a):
    flat = a.reshape(-1)
    pad = (-flat.shape[0]) % (LANES * SUBLANES)
    if pad:
        flat = jnp.pad(flat, (0, pad))
    return flat.reshape(-1, LANES)


def _gather_start(w, groups, tag, after=None, peers=None):
    shards = [[(w[n] if n in ROW_SHARDED else w[n].T).astype(BF16) for n in names] for names in groups]
    return _exchange_start_groups(shards, f"gather_{tag}_start", after=after, peers=peers)


def _gather_wait(handle, names, tag, after, peers=ALL_PEERS):
    landed = _exchange_wait(handle, f"gather_{tag}_wait", after=after, peers=peers)
    if peers == CHIP_PEERS:
        landed = [_sibling_forward(x, f"gather_{tag}_forward{i}") for i, x in enumerate(landed)]
    return {n: g.reshape(-1, g.shape[2]) for n, g in zip(names, landed)}


def _scatter_start(grads, names, tag, after=None):
    blocks = [grads[n].reshape(N_DEV, -1, grads[n].shape[1]) for n in names]
    return _exchange_start(blocks, f"scatter_{tag}_start", scatter=True, after=after)


def _scatter_wait(handle, names, tag, after):
    landed = _exchange_wait(handle, f"scatter_{tag}_wait", scatter=True, after=after)
    return {n: [x] for n, x in zip(names, landed)}


def _pack_small(vals):
    return jnp.concatenate([_flat_rows(v.astype(F32)) for v in vals], axis=0)


def _unpack_small(packed, like):
    out, row = [], 0
    for v in like:
        rows = _flat_rows(v).shape[0]
        out.append(packed[row:row + rows].reshape(-1)[:v.size].reshape(v.shape))
        row += rows
    return out


def _lanes128(*parts):
    out = jnp.zeros((HEAD_SLAB,), F32)
    for off, v in parts:
        out = lax.dynamic_update_slice(out, v.reshape(-1).astype(F32), (off,))
    return out.reshape(1, HEAD_SLAB)


def _step(x, c, positions, w, m, v, loss_target):
    nseq, seq, _ = x.shape
    tokens = nseq * seq
    me = _index(_my_pos())
    strip = lambda d: {n: (a[0] if a.ndim > 2 else a) for n, a in d.items()}
    shapes = {n: a.shape for n, a in w.items()}
    w, m, v = strip(w), strip(m), strip(v)

    c_all = _all_gather(c.reshape(-1, LANES), "gather_c").reshape(N_DEV * nseq, D_MODEL)
    n_ada = w["w_ada"].shape[1]
    b_cols = lax.dynamic_slice(w["b_ada"], (0, me * n_ada), (1, n_ada))
    c_act, mod_cols = _mod_cols(c_all, w["w_ada"], b_cols)
    mod_all = _all_gather(mod_cols, "gather_mod")
    mod = lax.dynamic_slice(mod_all, (0, me * nseq, 0), (N_DEV, nseq, n_ada))
    mod = mod.transpose(1, 0, 2).reshape(nseq, 3, 3, 1, D_MODEL)

    (h_f1i, h_f1o, h_mix_in, h_mix, h_f2), tok = _gather_start(
        w, (("w_ffn1_in",), ("w_ffn1_out",), MIXER[:1], MIXER[1:], ("w_ffn2_in", "w_ffn2_out")), "weights",
        after=mod_all, peers=[CHIP_PEERS] + [ALL_PEERS] * 4)
    started = tok[0:1, 0:1]

    g_q = _lanes128((0, w["q_norm_nope"]), (QK_NOPE, w["q_norm_rope"]))
    g_kn = _lanes128((0, w["k_norm_nope"]))
    g_kr = _lanes128((QK_NOPE, w["k_norm_rope"]))
    freq = ROPE_THETA ** (-jnp.arange(0, QK_ROPE, 2, dtype=F32) / QK_ROPE)
    inv_freq = _lanes128((QK_NOPE, jnp.concatenate([freq, freq])))
    pos = positions.reshape(tokens, 1).astype(F32)

    def sub(k, gamma, coef):
        return dict(gamma=w[gamma], shift=mod[:, k, 0] + started, scale=mod[:, k, 1], gate=coef * mod[:, k, 2])
    p1, pm, p2 = sub(0, "norm_ffn1", 0.5), sub(1, "norm_mix", 1.0), sub(2, "norm_ffn2", 0.5)

    x0 = x.reshape(tokens, D_MODEL)
    h1 = _norm_mod_fwd(x0, p1, seq, "ffn1_norm")
    wt_f1i = _gather_wait(h_f1i, ("w_ffn1_in",), "ffn1_in", h1, peers=CHIP_PEERS)["w_ffn1_in"]
    g1, u1, a1 = _ffn_in_act(h1, wt_f1i, "ffn1_in")
    w_f1o = _gather_wait(h_f1o, ("w_ffn1_out",), "ffn1_out", a1)["w_ffn1_out"]
    x1, f1, h2 = _out_residual(a1, w_f1o, x0, p1["gate"], pm, seq, "ffn1_out")
    saved1 = (x0, h1, g1, u1, a1, wt_f1i, w_f1o)

    wt_in = _gather_wait(h_mix_in, MIXER[:1], "mix_in", h2)["w_in"]
    zero_rows = lambda rows: jnp.zeros((rows, D_MODEL), BF16)
    wt_p = wt_in[:512]
    wt_a = jnp.concatenate([wt_in[512:1152], zero_rows(QK_NOPE), wt_in[1152:1184], zero_rows(32)], axis=0)
    wt_g = wt_in[1184:]
    z_a, z_p, z_g, qn, kvn = _mixer_in(h2, wt_a, wt_p, wt_g, w["q_a_norm"], w["kv_a_norm"], seq)

    full = _gather_wait(h_mix, MIXER[1:], "mix", z_g)
    wtq_pad = jnp.pad(full["w_q_up"].reshape(N_HEADS, 96, Q_LORA), ((0, 0), (0, 32), (0, 0))).reshape(-1, Q_LORA)
    wtmla_pad = jnp.pad(full["w_mla_proj"].reshape(D_MODEL, N_HEADS, 64), ((0, 0), (0, 0), (64, 0))).reshape(D_MODEL, -1)
    wt_pool, wt_kv, w_mix_out = full["w_pool_proj"], full["w_kv_up"], full["w_out"]
    pooled, pg, ps = _pool_fwd(z_p, w["pool_grp"], w["pool_scale"], seq)
    qc, kc, vp, qp, kv = _qk_prep_fwd(qn, wtq_pad, kvn, wt_kv, z_a, pos, g_q, g_kn, g_kr, inv_freq, seq)
    attn, lse = _attn_fwd(qc, kc, vp, seq)
    x2, o_mix, h3, merged, br_pool, br_mla = _mix_out(z_g, ps, attn, x1, wt_pool, wtmla_pad, w_mix_out, pm["gate"],
                                                      p2, seq)

    ffn2_w = _gather_wait(h_f2, ("w_ffn2_in", "w_ffn2_out"), "ffn2", h3)
    g2, u2, a2 = _ffn_in_act(h3, ffn2_w["w_ffn2_in"], "ffn2_in")
    dy, df2, dgate2, sq_err = _out_loss(a2, ffn2_w["w_ffn2_out"], x2, p2["gate"],
                                        loss_target.reshape(tokens, D_MODEL), seq, "ffn2_out")
    saved2 = (x2, h3, g2, u2, a2, ffn2_w["w_ffn2_in"], ffn2_w["w_ffn2_out"])

    grads = {}
    (dx2, do_mix, dsh2, dsc2, dgate_m, dg_ffn2), ops2 = _ffn_bwd_x(df2, dy, saved2, p2, seq, "ffn2", (o_mix, pm["gate"]))
    grads["w_ffn2_out"], grads["w_ffn2_in"] = _ffn_bwd_wout(ops2[0], ops2[1], "ffn2"), _ffn_bwd_win(ops2, "ffn2")
    s_f2, tok = _scatter_start(grads, ("w_ffn2_in", "w_ffn2_out"), "ffn2")

    grads["w_out"] = _mm(merged, do_mix, "tn", "mix_bwd_wout", out_dtype=BF16, tm=512, tn=D_MODEL)

    def merge_bwd(rows, bats, vecs):
        dmv, zg, bp, bm = (r.astype(F32) for r in rows)
        s_p, s_m = _sigmoid(zg[:, :D_MODEL]), _sigmoid(zg[:, D_MODEL:])
        dzg = jnp.concatenate([dmv * bp * s_p * (1.0 - s_p), dmv * bm * s_m * (1.0 - s_m)], axis=1)
        dbp, dbm = (dmv * s_p).astype(BF16), (dmv * s_m).astype(BF16)
        dps_v = jnp.dot(dbp, vecs[0], preferred_element_type=F32)
        dattn = jnp.dot(dbm, vecs[1], preferred_element_type=F32)
        return [dbp, dbm, dzg, dps_v, dattn], [], []
    dbr_pool, dbr_mla, dz_g, dps, d_attn = _rowmap(
        "mix_bwd_dmerged", merge_bwd, seq, [do_mix, z_g, br_pool, br_mla], vecs=[wt_pool, wtmla_pad],
        row_outs=[(D_MODEL, BF16), (D_MODEL, BF16), (2 * D_MODEL, BF16), (POOL_WIDTH, F32), (D_MODEL, BF16)],
        mm=(w_mix_out, "nt"))

    grads["w_pool_proj"] = _mm(dbr_pool, ps, "tn", "pool_bwd_wproj", out_dtype=BF16, tm=512, tn=POOL_WIDTH)
    dz_p, dgrp, dpool_scale = _pool_bwd(dps, pooled, pg, w["pool_grp"], w["pool_scale"] + tok[0:1, 0:1], seq)

    dwtmla_pad = _mm(dbr_mla, attn, "tn", "mla_bwd_wproj", out_dtype=BF16, tm=512, tn=D_MODEL)
    grads["w_mla_proj"] = dwtmla_pad.reshape(D_MODEL, N_HEADS, HEAD_SLAB)[:, :, 64:].reshape(D_MODEL, -1)
    dqc, dkc, dvp = _attn_bwd(qc, kc, vp, attn, lse, d_attn, seq)
    dqp, dkv, dkr, dg_q, dg_kn, dg_kr = _qk_prep_bwd(dqc, dkc, dvp, qp, kv, z_a, pos, g_q, g_kn, g_kr, inv_freq, seq)
    dwtq_pad = _mm(dqp, qn, "tn", "q_up_bwd_w", out_dtype=BF16, tm=512, tn=Q_LORA)
    grads["w_q_up"] = dwtq_pad.reshape(N_HEADS, HEAD_SLAB, Q_LORA)[:, :96].reshape(-1, Q_LORA)
    grads["w_kv_up"] = _mm(dkv, kvn, "tn", "kv_up_bwd_w", out_dtype=BF16, tm=512, tn=KV_LORA)
    dz_a, dg_qa, dg_kva = _latent_norm_bwd(dqp, wtq_pad, dkv, wt_kv, dkr, z_a, w["q_a_norm"], w["kv_a_norm"], seq)

    dwt_a = _mm(dz_a, h2, "tn", "mix_in_bwd_wa", out_dtype=BF16, tm=256, tn=D_MODEL)
    dwt_p = _mm(dz_p, h2, "tn", "mix_in_bwd_wp", out_dtype=BF16, tm=512, tn=D_MODEL)
    dwt_g = _mm(dz_g, h2, "tn", "mix_in_bwd_wg", out_dtype=BF16, tm=512, tn=D_MODEL)
    grads["w_in"] = jnp.concatenate([dwt_p, dwt_a[:640], dwt_a[704:736], dwt_g], axis=0)

    small_early = [dg_ffn2.reshape(w["norm_ffn2"].shape), dgrp, dpool_scale, dg_qa, dg_kva, dg_q[:, :QK_NOPE],
                   dg_q[:, QK_NOPE:QK_NOPE + QK_ROPE], dg_kn[:, :QK_NOPE], dg_kr[:, QK_NOPE:QK_NOPE + QK_ROPE]]
    s_small, tok = _exchange_start([_pack_small(small_early)], "gather_small_start")
    s_mix, tok = _scatter_start(grads, MIXER, "mix", after=tok)
    dh2 = [(dz_a, wt_a), (dz_p, wt_p), (dz_g, wt_g)]
    pm_tied = dict(pm, scale=pm["scale"] + tok[0:1, 0:1])
    dx1, df1, dsh_m, dsc_m, dgate1, dg_mix = _norm_mod_bwd(dh2, x1, dx2, pm_tied, seq, "mix_bwd_norm", (f1, p1["gate"]))

    handles = {}

    def ffn1_early(a, df):
        grads["w_ffn1_out"] = _ffn_bwd_wout(a, df, "ffn1")
        handles["f1o"], token = _scatter_start(grads, ("w_ffn1_out",), "ffn1_out")
        return token

    def ffn1_mid(operands):
        first = _ffn_bwd_win(operands, "ffn1", half=0)
        handles["f1i0"], token = _exchange_start([first.reshape(N_DEV, -1, first.shape[1])],
                                                 "scatter_ffn1_in0_start", scatter=True)
        return token

    (dx0, dsh1, dsc1, dg_ffn1), ops1 = _ffn_bwd_x(df1, dx1, saved1, p1, seq, "ffn1", early=ffn1_early,
                                                     mid=ffn1_mid)
    s_f1o = handles["f1o"]

    dmod = jnp.stack([jnp.stack([dsh1, dsc1, 0.5 * dgate1], axis=1),
                      jnp.stack([dsh_m, dsc_m, dgate_m], axis=1),
                      jnp.stack([dsh2, dsc2, 0.5 * dgate2], axis=1)], axis=1)
    n_dmod = nseq * 9 * D_MODEL // LANES
    tail = _all_gather(jnp.concatenate([dmod.reshape(-1, LANES), _flat_rows(dg_ffn1), _flat_rows(dg_mix),
                                        _flat_rows(sq_err)], axis=0), "gather_dmod")
    dmod_all = tail[:, :n_dmod].reshape(N_DEV * nseq, 9 * D_MODEL)

    second = _ffn_bwd_win(ops1, "ffn1", after=tail, half=1)
    s_second, tok = _exchange_start([second.reshape(N_DEV, -1, second.shape[1])], "scatter_ffn1_in1_start",
                                    scatter=True, after=tail)
    s_f1i = (handles["f1i0"], s_second)

    dmod_cols = lax.dynamic_slice(dmod_all, (0, me * n_ada), (N_DEV * nseq, n_ada)) + tok[0:1, 0:1]
    g_w_ada, g_b_ada = _ada_grads(c_act, dmod_all, dmod_cols)
    tail_sum = _sum_blocks(tail[:, n_dmod:], "sum_tail")
    g_norm_ffn1 = tail_sum[:SUBLANES].reshape(1, D_MODEL)
    g_norm_mix = tail_sum[SUBLANES:2 * SUBLANES].reshape(1, D_MODEL)
    loss = 0.5 * jnp.sum(tail_sum[2 * SUBLANES:]) * (1.0 / D_MODEL)
    small_all = _exchange_wait(s_small, "gather_small_wait", after=g_b_ada)[0]
    small_sum = _sum_blocks(small_all, "sum_small")
    small = dict(zip(SMALL[2:], _unpack_small(small_sum, [w[n] for n in SMALL[2:]])))
    grad_w = dict(small, w_ada=g_w_ada, b_ada=g_b_ada, norm_ffn1=g_norm_ffn1, norm_mix=g_norm_mix)

    delta, new_m, new_v = {}, {}, {}

    def update(names, landed=None):
        for n in names:
            if landed is None:
                delta[n], new_m[n], new_v[n] = _adamw(w[n], grad_w[n], m[n], v[n], f"adamw_{n}")
            elif n in KEPT_TRANSPOSED:
                res = _adamw_landed(w[n].T, landed[n], m[n].T, v[n].T, f"adamw_{n}")
                grad_w[n], delta[n], new_m[n], new_v[n] = (r.T for r in res)
            elif n in ROW_SHARDED:
                grad_w[n], delta[n], new_m[n], new_v[n] = _adamw_landed(w[n], landed[n], m[n], v[n], f"adamw_{n}")
            else:
                grad_w[n] = _sum_blocks(landed[n][0], f"sum_{n}").T
                delta[n], new_m[n], new_v[n] = _adamw(w[n], grad_w[n], m[n], v[n], f"adamw_{n}")

    update(("w_ada",))
    rep = ("b_ada",) + SMALL
    d_s, m_s, v_s = _adamw(_pack_small([w[n] for n in rep]), _pack_small([grad_w[n] for n in rep]),
                           _pack_small([m[n] for n in rep]), _pack_small([v[n] for n in rep]), "adamw_small")
    like = [w[n] for n in rep]
    for dst, packed in ((delta, d_s), (new_m, m_s), (new_v, v_s)):
        dst.update(zip(rep, _unpack_small(packed, like)))
    update(("w_ffn2_in", "w_ffn2_out"), _scatter_wait(s_f2, ("w_ffn2_in", "w_ffn2_out"), "ffn2", after=d_s))
    update(MIXER, _scatter_wait(s_mix, MIXER, "mix", after=delta["w_ffn2_out"]))
    update(("w_ffn1_out",), _scatter_wait(s_f1o, ("w_ffn1_out",), "ffn1_out", after=delta["w_out"]))
    halves = [_exchange_wait(h, f"scatter_ffn1_in{i}_wait", scatter=True, after=delta["w_ffn1_out"])[0]
              for i, h in enumerate(s_f1i)]
    update(("w_ffn1_in",), {"w_ffn1_in": halves})

    lead = lambda d: [d[n].reshape(shapes[n]) for n in WEIGHTS]
    return (loss, dx0.reshape(x.shape), *lead(grad_w), *lead(delta), *lead(new_m), *lead(new_v))


def kernel(x, c, positions, w_ada, b_ada, norm_ffn1, w_ffn1_in, w_ffn1_out, norm_mix, w_in, pool_grp, pool_scale, w_pool_proj, q_a_norm, w_q_up, kv_a_norm, w_kv_up, q_norm_nope, q_norm_rope, k_norm_nope, k_norm_rope, w_mla_proj, w_out, norm_ffn2, w_ffn2_in, w_ffn2_out, loss_target, m_w_ada, m_b_ada, m_norm_ffn1, m_w_ffn1_in, m_w_ffn1_out, m_norm_mix, m_w_in, m_pool_grp, m_pool_scale, m_w_pool_proj, m_q_a_norm, m_w_q_up, m_kv_a_norm, m_w_kv_up, m_q_norm_nope, m_q_norm_rope, m_k_norm_nope, m_k_norm_rope, m_w_mla_proj, m_w_out, m_norm_ffn2, m_w_ffn2_in, m_w_ffn2_out, v_w_ada, v_b_ada, v_norm_ffn1, v_w_ffn1_in, v_w_ffn1_out, v_norm_mix, v_w_in, v_pool_grp, v_pool_scale, v_w_pool_proj, v_q_a_norm, v_w_q_up, v_kv_a_norm, v_w_kv_up, v_q_norm_nope, v_q_norm_rope, v_k_norm_nope, v_k_norm_rope, v_w_mla_proj, v_w_out, v_norm_ffn2, v_w_ffn2_in, v_w_ffn2_out):
    w = dict(w_ada=w_ada, b_ada=b_ada, norm_ffn1=norm_ffn1, w_ffn1_in=w_ffn1_in, w_ffn1_out=w_ffn1_out, norm_mix=norm_mix, w_in=w_in, pool_grp=pool_grp, pool_scale=pool_scale, w_pool_proj=w_pool_proj, q_a_norm=q_a_norm, w_q_up=w_q_up, kv_a_norm=kv_a_norm, w_kv_up=w_kv_up, q_norm_nope=q_norm_nope, q_norm_rope=q_norm_rope, k_norm_nope=k_norm_nope, k_norm_rope=k_norm_rope, w_mla_proj=w_mla_proj, w_out=w_out, norm_ffn2=norm_ffn2, w_ffn2_in=w_ffn2_in, w_ffn2_out=w_ffn2_out)
    m = dict(w_ada=m_w_ada, b_ada=m_b_ada, norm_ffn1=m_norm_ffn1, w_ffn1_in=m_w_ffn1_in, w_ffn1_out=m_w_ffn1_out, norm_mix=m_norm_mix, w_in=m_w_in, pool_grp=m_pool_grp, pool_scale=m_pool_scale, w_pool_proj=m_w_pool_proj, q_a_norm=m_q_a_norm, w_q_up=m_w_q_up, kv_a_norm=m_kv_a_norm, w_kv_up=m_w_kv_up, q_norm_nope=m_q_norm_nope, q_norm_rope=m_q_norm_rope, k_norm_nope=m_k_norm_nope, k_norm_rope=m_k_norm_rope, w_mla_proj=m_w_mla_proj, w_out=m_w_out, norm_ffn2=m_norm_ffn2, w_ffn2_in=m_w_ffn2_in, w_ffn2_out=m_w_ffn2_out)
    v = dict(w_ada=v_w_ada, b_ada=v_b_ada, norm_ffn1=v_norm_ffn1, w_ffn1_in=v_w_ffn1_in, w_ffn1_out=v_w_ffn1_out, norm_mix=v_norm_mix, w_in=v_w_in, pool_grp=v_pool_grp, pool_scale=v_pool_scale, w_pool_proj=v_w_pool_proj, q_a_norm=v_q_a_norm, w_q_up=v_w_q_up, kv_a_norm=v_kv_a_norm, w_kv_up=v_w_kv_up, q_norm_nope=v_q_norm_nope, q_norm_rope=v_q_norm_rope, k_norm_nope=v_k_norm_nope, k_norm_rope=v_k_norm_rope, w_mla_proj=v_w_mla_proj, w_out=v_w_out, norm_ffn2=v_norm_ffn2, w_ffn2_in=v_w_ffn2_in, w_ffn2_out=v_w_ffn2_out)
    return _step(x, c, positions, w, m, v, loss_target)
```

```python
import functools
import math

import jax
import jax.numpy as jnp
from jax import lax
from jax.experimental import pallas as pl
from jax.experimental.pallas import tpu as pltpu

F32 = jnp.float32
BF16 = jnp.bfloat16
MESH = pl.DeviceIdType.MESH
AXES = ("x", "y", "c")
N_DEV = 8

D_MODEL = 1024
D_FF = 2816
N_HEADS = 8
HEAD_SLAB = 128
QK_NOPE = 64
QK_ROPE = 32
POOL_WIDTH = 512
POOL_GROUPS = 4
POOL_GROUP_DIM = 128
Q_LORA = 384
KV_LORA = 256
ROPE_THETA = 10000.0
ATTN_SCALE = 1.0 / math.sqrt(QK_NOPE + QK_ROPE)
NORM_EPS = 1e-6
ADAM_LR, ADAM_B1, ADAM_B2, ADAM_EPS, ADAM_WD, ADAM_STEP = 0.001, 0.9, 0.999, 1e-08, 0.01, 10

LANES = 128
SUBLANES = 8
VMEM_LIMIT = 52 * 1024 * 1024
ADAMW_WHOLE_BYTES = 3 << 19
SUM_WHOLE_BYTES = 4 << 20

BIG = ("w_ffn1_in", "w_ffn1_out", "w_in", "w_pool_proj", "w_q_up", "w_kv_up",
       "w_mla_proj", "w_out", "w_ffn2_in", "w_ffn2_out")
ROW_SHARDED = ("w_ffn1_out", "w_out", "w_ffn2_out")
MIXER = ("w_in", "w_pool_proj", "w_q_up", "w_kv_up", "w_mla_proj", "w_out")
KEPT_TRANSPOSED = ("w_ffn1_in", "w_ffn2_in", "w_in", "w_q_up")
SMALL = ("norm_ffn1", "norm_mix", "norm_ffn2", "pool_grp", "pool_scale", "q_a_norm",
         "kv_a_norm", "q_norm_nope", "q_norm_rope", "k_norm_nope", "k_norm_rope")
WEIGHTS = ("w_ada", "b_ada", "norm_ffn1", "w_ffn1_in", "w_ffn1_out", "norm_mix", "w_in",
           "pool_grp", "pool_scale", "w_pool_proj", "q_a_norm", "w_q_up", "kv_a_norm",
           "w_kv_up", "q_norm_nope", "q_norm_rope", "k_norm_nope", "k_norm_rope",
           "w_mla_proj", "w_out", "norm_ffn2", "w_ffn2_in", "w_ffn2_out")


def _params(*sem):
    return pltpu.CompilerParams(dimension_semantics=sem, vmem_limit_bytes=VMEM_LIMIT)


def _tile(n, cands):
    for c in cands:
        if n % c == 0:
            return c
    return n


def _my_pos():
    return lax.axis_index("x"), lax.axis_index("y"), lax.axis_index("c")


def _flip(pos, k):
    x, y, c = pos
    fx, fy, fc = (k >> 2) & 1, (k >> 1) & 1, k & 1
    return ((1 - x) if fx else x, (1 - y) if fy else y, (1 - c) if fc else c)


def _index(pos):
    x, y, c = pos
    return 4 * x + 2 * y + c


def _exchange(arrays, name, scatter=False):
    n = len(arrays)

    def body(*refs):
        ins, outs = refs[:n], refs[n:2 * n]
        send_sems, recv_sems, local_sems = refs[2 * n:]
        me = _my_pos()
        mine, sends = [], []
        for a in range(n):
            own = ins[a].at[_index(me)] if scatter else ins[a]
            cp = pltpu.make_async_copy(own, outs[a].at[_index(me)], local_sems.at[a])
            cp.start()
            mine.append(cp)
        for k in range(1, N_DEV):
            peer = _flip(me, k)
            for a in range(n):
                cp = pltpu.make_async_remote_copy(
                    src_ref=ins[a].at[_index(peer)] if scatter else ins[a],
                    dst_ref=outs[a].at[_index(me)],
                    send_sem=send_sems.at[a, k - 1], recv_sem=recv_sems.at[a, k - 1],
                    device_id=peer, device_id_type=MESH)
                cp.start()
                sends.append(cp)
        for k in range(1, N_DEV):
            peer = _flip(me, k)
            for a in range(n):
                pltpu.make_async_remote_copy(
                    src_ref=ins[a].at[_index(me)] if scatter else ins[a],
                    dst_ref=outs[a].at[_index(peer)],
                    send_sem=send_sems.at[a, k - 1], recv_sem=recv_sems.at[a, k - 1],
                    device_id=peer, device_id_type=MESH).wait_recv()
        for cp in sends:
            cp.wait_send()
        for cp in mine:
            cp.wait()

    shape = lambda x: x.shape if scatter else (N_DEV,) + x.shape
    return pl.pallas_call(
        body, name=name,
        out_shape=tuple(jax.ShapeDtypeStruct(shape(x), x.dtype) for x in arrays),
        in_specs=[pl.BlockSpec(memory_space=pl.ANY)] * n,
        out_specs=tuple(pl.BlockSpec(memory_space=pl.ANY) for _ in arrays),
        scratch_shapes=[pltpu.SemaphoreType.DMA((n, N_DEV - 1)),
                        pltpu.SemaphoreType.DMA((n, N_DEV - 1)),
                        pltpu.SemaphoreType.DMA((n,))],
    )(*arrays)


def _all_gather(x, name):
    return _exchange([x], name)[0]


_HBM = pl.BlockSpec(memory_space=pltpu.HBM)
_SEM = pl.BlockSpec(memory_space=pltpu.SEMAPHORE)
_ANY = pl.BlockSpec(memory_space=pl.ANY)
_EFFECT = pltpu.SideEffectType.DATAFLOW_SIDE_EFFECTING


def _split_copy(ins, lands, send_sems, recv_sems, a, k, me, scatter, incoming):
    peer = _flip(me, k)
    block = me if incoming else peer
    return pltpu.make_async_remote_copy(
        src_ref=ins[a].at[_index(block)] if scatter else ins[a],
        dst_ref=lands[a].at[_index(peer if incoming else me)],
        send_sem=send_sems.at[a * (N_DEV - 1) + k - 1], recv_sem=recv_sems.at[a * (N_DEV - 1) + k - 1],
        device_id=peer, device_id_type=MESH)


ALL_PEERS = tuple(range(1, N_DEV))
CHIP_PEERS = (1, 2, 4, 6)


def _exchange_start_groups(groups, name, scatter=False, after=None, peers=None):
    peers = peers or [ALL_PEERS] * len(groups)
    sizes = [len(g) for g in groups]
    first = [sum(sizes[:i]) for i in range(len(sizes))]
    n, ng = sum(sizes), len(sizes)
    after = jnp.zeros((SUBLANES, LANES), F32) if after is None else after

    def body(*refs):
        ins, lands = refs[:n], refs[n:2 * n]
        sems = refs[2 * n + 1:2 * n + 1 + 2 * ng]
        me = _my_pos()
        for g in range(ng):
            part = slice(first[g], first[g] + sizes[g])
            for k in peers[g]:
                for a in range(sizes[g]):
                    _split_copy(ins[part], lands[part], sems[2 * g], sems[2 * g + 1], a, k, me, scatter, False).start()
        refs[-1][...] = jnp.zeros((SUBLANES, LANES), F32)

    shape = lambda x: x.shape if scatter else (N_DEV,) + x.shape
    hbm = lambda x: pltpu.with_memory_space_constraint(x, pltpu.HBM)
    srcs = [hbm(x) for g in groups for x in g]
    zones = [hbm(lax.empty(shape(x), x.dtype)) for g in groups for x in g]
    sem_shapes = [pltpu.SemaphoreType.DMA((s * (N_DEV - 1),)) for s in sizes for _ in range(2)]
    out = pl.pallas_call(
        body, name=name,
        out_shape=(*sem_shapes, *[pltpu.HBM(x.shape, x.dtype) for x in srcs + zones],
                   jax.ShapeDtypeStruct((SUBLANES, LANES), F32)),
        in_specs=[_HBM] * (2 * n) + [_ANY],
        out_specs=(*[_SEM] * (2 * ng), *[_HBM] * (2 * n), pl.BlockSpec(memory_space=pltpu.VMEM)),
        input_output_aliases={i: 2 * ng + i for i in range(2 * n)},
        compiler_params=pltpu.CompilerParams(has_side_effects=_EFFECT),
    )(*srcs, *zones, after)
    bufs = out[2 * ng:-1]
    handles = [(out[2 * g], out[2 * g + 1], *bufs[first[g]:first[g] + sizes[g]],
                *bufs[n + first[g]:n + first[g] + sizes[g]]) for g in range(ng)]
    return handles, out[-1]


def _exchange_start(arrays, name, scatter=False, after=None):
    handles, token = _exchange_start_groups([arrays], name, scatter, after)
    return handles[0], token


def _exchange_wait(handle, name, scatter=False, after=None, peers=ALL_PEERS):
    send_sems, recv_sems = handle[0], handle[1]
    n = (len(handle) - 2) // 2
    after = jnp.zeros((SUBLANES, LANES), F32) if after is None else after

    def body(*refs):
        ins, lands = refs[:n], refs[n:2 * n]
        send, recv = refs[2 * n], refs[2 * n + 1]
        me = _my_pos()
        for k in peers:
            for a in range(n):
                _split_copy(ins, lands, send, recv, a, k, me, scatter, False).wait_send()
                _split_copy(ins, lands, send, recv, a, k, me, scatter, True).wait_recv()

    bufs = handle[2:]
    out = pl.pallas_call(
        body, name=name,
        out_shape=tuple(pltpu.HBM(x.shape, x.dtype) for x in bufs),
        in_specs=[_HBM] * (2 * n) + [_SEM, _SEM, _ANY],
        out_specs=tuple([_HBM] * (2 * n)),
        input_output_aliases={i: i for i in range(2 * n)},
        compiler_params=pltpu.CompilerParams(has_side_effects=_EFFECT),
    )(*bufs, send_sems, recv_sems, after)
    me = _index(_my_pos())
    landed = []
    for src, land in zip(out[:n], out[n:]):
        own = lax.dynamic_slice_in_dim(src, me, 1, axis=0) if scatter else src[None]
        landed.append(lax.dynamic_update_slice_in_dim(land, own, me, axis=0))
    return landed


def _sibling_forward(x, name):
    flips = [k for k in CHIP_PEERS if k != 1]

    def body(x_ref, o_ref, send_sems, recv_sems):
        me = _my_pos()
        sibling = _flip(me, 1)
        sends = []
        for i, k in enumerate(flips):
            block = o_ref.at[_index(_flip(me, k))]
            cp = pltpu.make_async_remote_copy(src_ref=block, dst_ref=block, send_sem=send_sems.at[i],
                                              recv_sem=recv_sems.at[i], device_id=sibling, device_id_type=MESH)
            cp.start()
            sends.append(cp)
        for i, k in enumerate(flips):
            block = o_ref.at[_index(_flip(sibling, k))]
            pltpu.make_async_remote_copy(src_ref=block, dst_ref=block, send_sem=send_sems.at[i],
                                         recv_sem=recv_sems.at[i], device_id=sibling, device_id_type=MESH).wait_recv()
        for cp in sends:
            cp.wait_send()

    return pl.pallas_call(
        body, name=name, out_shape=jax.ShapeDtypeStruct(x.shape, x.dtype),
        in_specs=[_ANY], out_specs=_ANY, input_output_aliases={0: 0},
        scratch_shapes=[pltpu.SemaphoreType.DMA((len(flips),)), pltpu.SemaphoreType.DMA((len(flips),))],
    )(x)


def _sum_blocks(x, name):
    n, rows, cols = x.shape
    whole = x.size * x.dtype.itemsize <= SUM_WHOLE_BYTES
    tr = rows if whole else _tile(rows, (512, 256, 128, 64, 32, 16, 8))

    def body(x_ref, o_ref):
        acc = x_ref[0].astype(F32)
        for d in range(1, n):
            acc = acc + x_ref[d].astype(F32)
        o_ref[...] = acc

    return pl.pallas_call(
        body, name=name,
        out_shape=jax.ShapeDtypeStruct((rows, cols), F32),
        grid=(rows // tr,),
        in_specs=[pl.BlockSpec((n, tr, cols), lambda i: (0, i, 0))],
        out_specs=pl.BlockSpec((tr, cols), lambda i: (i, 0)),
        compiler_params=_params("parallel"),
    )(x)


_DIMS = {"nn": (((1,), (0,)), ((), ())), "nt": (((1,), (1,)), ((), ())), "tn": (((0,), (0,)), ((), ()))}


def _mm(a, b, mode, name, out_dtype=F32, tm=None, tn=None, add=None, after=None, b_cols=None):
    if mode == "tn":
        kdim, m = a.shape
    else:
        m, kdim = a.shape
    n = b.shape[0] if mode == "nt" else b.shape[1]
    tm = tm or _tile(m, (512, 256, 128))
    tn = tn or _tile(n, (512, 256, 128))
    j0 = 0
    if b_cols is not None:
        j0, n = b_cols[0], b_cols[1] * tn
    dims = _DIMS[mode]

    def body(*refs):
        refs = refs if after is None else refs[1:]
        acc = lax.dot_general(refs[0][...].astype(BF16), refs[1][...].astype(BF16), dims,
                              preferred_element_type=F32)
        if add is not None:
            acc = acc + refs[2][...]
        refs[-1][...] = acc.astype(out_dtype)

    a_spec = (pl.BlockSpec((kdim, tm), lambda i, j: (0, i)) if mode == "tn"
              else pl.BlockSpec((tm, kdim), lambda i, j: (i, 0)))
    b_spec = (pl.BlockSpec((tn, kdim), lambda i, j: (j, 0)) if mode == "nt"
              else pl.BlockSpec((kdim, tn), lambda i, j: (0, j + j0)))
    o_spec = pl.BlockSpec((tm, tn), lambda i, j: (i, j))
    in_specs, args = [a_spec, b_spec], [a, b]
    if add is not None:
        in_specs.append(o_spec)
        args.append(add)
    if after is not None:
        in_specs.insert(0, _ANY)
        args.insert(0, after)
    return pl.pallas_call(
        body, name=name, out_shape=jax.ShapeDtypeStruct((m, n), out_dtype), grid=(m // tm, n // tn),
        in_specs=in_specs, out_specs=o_spec,
        compiler_params=_params("parallel", "parallel"),
    )(*args)


def _mm_tn_shared(parts, b, name, tm=256):
    kdim, n = b.shape
    blocks = [a.shape[1] // tm for a in parts]
    first = [sum(blocks[:i]) for i in range(len(parts))]

    def body(*refs):
        a_refs, b_ref, o_refs = refs[:len(parts)], refs[len(parts)], refs[len(parts) + 1:]
        i = pl.program_id(0)
        for a_ref, o_ref, f, nb in zip(a_refs, o_refs, first, blocks):
            @pl.when(jnp.logical_and(i >= f, i < f + nb))
            def _(a_ref=a_ref, o_ref=o_ref):
                o_ref[...] = lax.dot_general(a_ref[...], b_ref[...], _DIMS["tn"],
                                             preferred_element_type=F32).astype(BF16)

    def window(f, nb):
        return lambda i: jnp.clip(i - f, 0, nb - 1)
    a_specs = [pl.BlockSpec((kdim, tm), functools.partial(lambda i, w: (0, w(i)), w=window(f, nb)))
               for f, nb in zip(first, blocks)]
    o_specs = [pl.BlockSpec((tm, n), functools.partial(lambda i, w: (w(i), 0), w=window(f, nb)))
               for f, nb in zip(first, blocks)]
    return pl.pallas_call(
        body, name=name, grid=(sum(blocks),),
        out_shape=tuple(jax.ShapeDtypeStruct((a.shape[1], n), BF16) for a in parts),
        in_specs=a_specs + [pl.BlockSpec((kdim, n), lambda i: (0, 0), pipeline_mode=pl.Buffered(1))],
        out_specs=tuple(o_specs),
        compiler_params=_params("arbitrary"),
    )(*parts, b)


def _rowmap(name, fn, seq, rows, bats=(), vecs=(), row_outs=(), bat_outs=(), vec_outs=(), ts=None, mm=None, lhs=None,
            after=None, mm_sum=True):
    mms = [] if mm is None else (mm if isinstance(mm, list) else [mm])
    rows = [r if isinstance(r, tuple) else (r, r.shape[1], 0) for r in rows]
    tokens = rows[0][0].shape[0]
    nseq = tokens // seq
    ts = ts or _tile(seq, (512, 256, 128, 64, 32, 16, 8))
    nt = seq // ts
    n_r, n_b, n_v = len(rows), len(bats), len(vecs)
    n_ro, n_bo = len(row_outs), len(bat_outs)

    def accumulate(ref, val, first):
        @pl.when(first)
        def _():
            ref[...] = val.reshape(ref.shape)

        @pl.when(jnp.logical_not(first))
        def _():
            ref[...] += val.reshape(ref.shape)

    def body(*refs):
        n_in = n_r + n_b + n_v + len(mms) + (after is not None)
        ins, outs = refs[:n_in], refs[n_in:]
        b_vals = [r[0] for r in ins[n_r:n_r + n_b]]
        v_vals = [r[...] for r in ins[n_r + n_b:n_r + n_b + n_v]]
        r_vals = [r[...] for r in ins[:n_r]]
        if mms:
            lefts = r_vals[:len(mms)] if lhs is None else [lhs(r_vals, v_vals)] * len(mms)
            parts = [lax.dot_general(left.astype(BF16), b_ref[...].astype(BF16), _DIMS[mode],
                                     preferred_element_type=F32)
                     for left, b_ref, (_, mode) in zip(lefts, ins[n_r + n_b + n_v:], mms)]
            accs = [functools.reduce(lambda x, y: x + y, parts)] if mm_sum else parts
            r_vals = accs + r_vals[len(mms):] if lhs is None else accs + [lefts[0]] + r_vals
        ro, bo, vo = fn(r_vals, b_vals, v_vals)
        for ref, val in zip(outs[:n_ro], ro):
            ref[...] = val.astype(ref.dtype)
        b, i = pl.program_id(0), pl.program_id(1)
        for ref, val in zip(outs[n_ro:n_ro + n_bo], bo):
            accumulate(ref, val, i == 0)
        for ref, val in zip(outs[n_ro + n_bo:], vo):
            accumulate(ref, val, jnp.logical_and(i == 0, b == 0))

    in_specs = [pl.BlockSpec((ts, w), functools.partial(lambda b, i, cb: (b * nt + i, cb), cb=cb))
                for _, w, cb in rows]
    in_specs += [pl.BlockSpec((1, 1, v.shape[2]), lambda b, i: (b, 0, 0)) for v in bats]
    in_specs += [pl.BlockSpec(v.shape, lambda b, i: (0, 0)) for v in vecs]
    extra = [b_arr for b_arr, _ in mms]
    in_specs += [pl.BlockSpec(b_arr.shape, lambda b, i: (0, 0), pipeline_mode=pl.Buffered(1)) for b_arr in extra]
    if after is not None:
        in_specs.append(_ANY)
        extra.append(after)
    out_shape = [jax.ShapeDtypeStruct((tokens, f), dt) for f, dt in row_outs]
    out_specs = [pl.BlockSpec((ts, f), lambda b, i: (b * nt + i, 0)) for f, _ in row_outs]
    out_shape += [jax.ShapeDtypeStruct((nseq, 1, f), F32) for f in bat_outs]
    out_specs += [pl.BlockSpec((1, 1, f), lambda b, i: (b, 0, 0)) for f in bat_outs]
    out_shape += [jax.ShapeDtypeStruct((1, f), F32) for f in vec_outs]
    out_specs += [pl.BlockSpec((1, f), lambda b, i: (0, 0)) for f in vec_outs]
    return pl.pallas_call(
        body, name=name, out_shape=tuple(out_shape), grid=(nseq, nt),
        in_specs=in_specs, out_specs=tuple(out_specs),
        compiler_params=_params("arbitrary", "arbitrary"),
    )(*([r[0] for r in rows] + list(bats) + list(vecs) + extra))


def _colsum(v):
    return jnp.sum(v, axis=0, keepdims=True)


def _rstd(x, width=None):
    width = width or x.shape[-1]
    return lax.rsqrt(jnp.sum(x * x, axis=-1, keepdims=True) * (1.0 / width) + NORM_EPS)


def _norm_bwd(dy, x, r, g, width=None):
    width = width or x.shape[-1]
    xhat = x * r
    dxhat = dy * g
    dx = r * (dxhat - xhat * (jnp.sum(dxhat * xhat, axis=-1, keepdims=True) * (1.0 / width)))
    return dx, dy * xhat


def _sigmoid(x):
    return 0.5 * jnp.tanh(0.5 * x) + 0.5


def _norm_mod(xv, g, sh, sc):
    return xv * _rstd(xv) * g * (1.0 + sc) + sh


def _norm_mod_fwd(x, p, seq, name):
    def fn(rows, bats, vecs):
        return [_norm_mod(rows[0], vecs[0], bats[0], bats[1])], [], []
    return _rowmap(name, fn, seq, [x], [p["shift"], p["scale"]], [p["gamma"]], row_outs=[(D_MODEL, BF16)])[0]


def _norm_mod_bwd(dh, x, dres, p, seq, name, prev=None, after=None):
    products = dh if isinstance(dh, list) else None
    lefts = [l for l, _ in products] if products else [dh]
    def fn(rows, bats, vecs):
        dhv, xv, dr = rows[:3]
        sc, g = bats[0], vecs[0]
        r = _rstd(xv)
        dxn, dg = _norm_bwd(dhv * (1.0 + sc), xv, r, g)
        dx = dr + dxn
        ro, bo = [dx], [_colsum(dhv), _colsum(dhv * (xv * r * g))]
        if prev is not None:
            ro.append(bats[1] * dx)
            bo.append(_colsum(dx * rows[3].astype(F32)))
        return ro, bo, [_colsum(dg)]
    more = prev is not None
    return _rowmap(name, fn, seq, lefts + [x, dres] + ([prev[0]] if more else []),
                   [p["scale"]] + ([prev[1]] if more else []), [p["gamma"]],
                   row_outs=[(D_MODEL, F32)] + ([(D_MODEL, BF16)] if more else []),
                   bat_outs=[D_MODEL] * (3 if more else 2), vec_outs=[D_MODEL],
                   mm=[(r, "nn") for _, r in products] if products else None, after=after)


def _ffn_in_act(h, wt_in, name):
    tokens = h.shape[0]
    tm, tn = _tile(tokens, (2048, 1024, 512)), 256
    nj = D_FF // tn

    def body(h_ref, wg_ref, wu_ref, g_ref, u_ref, a_ref):
        hv = h_ref[...]
        g = lax.dot_general(hv, wg_ref[...], _DIMS["nt"], preferred_element_type=F32)
        u = lax.dot_general(hv, wu_ref[...], _DIMS["nt"], preferred_element_type=F32)
        g_ref[...] = g.astype(BF16)
        u_ref[...] = u.astype(BF16)
        a_ref[...] = (g * _sigmoid(g) * u).astype(BF16)

    o_spec = pl.BlockSpec((tm, tn), lambda i, j: (i, j))
    out = jax.ShapeDtypeStruct((tokens, D_FF), BF16)
    return pl.pallas_call(
        body, name=name, grid=(tokens // tm, nj), out_shape=(out, out, out),
        in_specs=[pl.BlockSpec((tm, D_MODEL), lambda i, j: (i, 0)),
                  pl.BlockSpec((tn, D_MODEL), lambda i, j: (j, 0)),
                  pl.BlockSpec((tn, D_MODEL), lambda i, j: (j + nj, 0))],
        out_specs=(o_spec, o_spec, o_spec),
        compiler_params=_params("parallel", "parallel"),
    )(h, wt_in, wt_in)


def _out_residual(a, w_out, res, gate, nxt, seq, name):
    def fn(rows, bats, vecs):
        acc, rv = rows
        x_new = rv + bats[0] * acc
        return [x_new, acc, _norm_mod(x_new, vecs[0], bats[1], bats[2])], [], []
    return _rowmap(name, fn, seq, [a, res], [gate, nxt["shift"], nxt["scale"]], [nxt["gamma"]],
                   row_outs=[(D_MODEL, F32), (D_MODEL, BF16), (D_MODEL, BF16)],
                   ts=_tile(seq, (512, 256, 128)), mm=(w_out, "nn"))


def _mix_out(z_g, ps, attn, res, wt_pool, wtmla_pad, w_out, gate, nxt, seq):
    branches = []

    def lhs(rows, vecs):
        zg = rows[0].astype(F32)
        bp = lax.dot_general(rows[1], vecs[1], _DIMS["nt"], preferred_element_type=F32)
        bm = lax.dot_general(rows[2], vecs[2], _DIMS["nt"], preferred_element_type=F32)
        branches[:] = [bp, bm]
        return (_sigmoid(zg[:, :D_MODEL]) * bp + _sigmoid(zg[:, D_MODEL:]) * bm).astype(BF16)

    def fn(rows, bats, vecs):
        acc, merged, rv = rows[0], rows[1], rows[-1]
        x_new = rv + bats[0] * acc
        return [x_new, acc, _norm_mod(x_new, vecs[0], bats[1], bats[2]), merged] + branches, [], []
    return _rowmap("mix_out", fn, seq, [z_g, ps, attn, res], [gate, nxt["shift"], nxt["scale"]],
                   [nxt["gamma"], wt_pool, wtmla_pad], row_outs=[(D_MODEL, F32)] + [(D_MODEL, BF16)] * 5,
                   ts=_tile(seq, (512, 256, 128)), mm=(w_out, "nn"), lhs=lhs)


def _out_loss(a, w_out, res, gate, target, seq, name):
    def fn(rows, bats, vecs):
        acc, rv, tv = rows
        err = rv + bats[0] * acc - tv
        dy = err * (1.0 / D_MODEL)
        return [dy, bats[0] * dy], [_colsum(dy * acc)], [_colsum(err * err)]
    return _rowmap(name, fn, seq, [a, res, target], [gate], row_outs=[(D_MODEL, F32), (D_MODEL, BF16)],
                   bat_outs=[D_MODEL], vec_outs=[D_MODEL], ts=_tile(seq, (512, 256, 128)), mm=(w_out, "nn"))


def _ffn_bwd_x(df, dres, saved, p, seq, tag, prev=None, early=None, mid=None):
    x, h, g, u, a, w_in, w_out = saved
    first = None if early is None else early(a, df)

    def act_bwd(rows, bats, vecs):
        dav, gv, uv = rows[0], rows[1].astype(F32), rows[2].astype(F32)
        sg = _sigmoid(gv)
        silu = gv * sg
        dg = dav * uv * (sg * (1.0 + gv * (1.0 - sg)))
        return [jnp.concatenate([dg, dav * silu], axis=1)], [], []
    dgu = _rowmap(f"{tag}_bwd_da", act_bwd, seq, [df, g, u], row_outs=[(2 * D_FF, BF16)],
                  ts=_tile(seq, (512, 256, 128)), mm=(w_out, "nt"), after=first)[0]
    operands = (a, df, dgu, h)
    after = None if mid is None else mid(operands)
    return _norm_mod_bwd([(dgu, w_in)], x, dres, p, seq, f"{tag}_bwd_norm", prev, after=after), operands


def _ffn_bwd_wout(a, df, tag):
    return _mm(a, df, "tn", f"{tag}_bwd_wout", out_dtype=BF16, tm=256, tn=D_MODEL)


def _ffn_bwd_win(operands, tag, after=None, half=None):
    _, _, dgu, h = operands
    if half is None:
        return _mm(dgu, h, "tn", f"{tag}_bwd_win", out_dtype=BF16, tm=512, tn=D_MODEL, after=after)
    return _mm(dgu, h, "tn", f"{tag}_bwd_win{half}", out_dtype=BF16, tm=512, tn=D_MODEL // 2, after=after,
               b_cols=(half, 1))


def _shift_rows(v, k, forward):
    n = v.shape[0]
    row = lax.broadcasted_iota(jnp.int32, v.shape, 0)
    if forward:
        return jnp.where(row >= k, pltpu.roll(v, k, 0), 0.0)
    return jnp.where(row < n - k, pltpu.roll(v, n - k, 0), 0.0)


def _window_sums(v, forward):
    out, s, k = [], v, 1
    for _ in range(POOL_GROUPS):
        s = s + _shift_rows(s, k, forward)
        out.append(s)
        k *= 2
    return out


def _by_group(vals, g):
    out = vals[-1]
    for idx in range(len(vals) - 2, -1, -1):
        out = jnp.where(g == idx, vals[idx], out)
    return out


def _inv_count(shape, g):
    t1 = lax.broadcasted_iota(jnp.int32, shape, 0) + 1
    window = _by_group([jnp.int32(2 ** (i + 1)) for i in range(POOL_GROUPS)], g)
    return 1.0 / jnp.minimum(t1, window).astype(F32)


def _pool_fwd(u, grp, scale, seq):
    tokens = u.shape[0]

    def body(u_ref, grp_ref, sc_ref, pooled_ref, pg_ref, ps_ref):
        g = pl.program_id(1)
        uv = u_ref[...]
        sums = _by_group(_window_sums(uv, True), g)
        pooled = (sums * _inv_count(uv.shape, g) - uv).astype(BF16)
        pg = jnp.dot(pooled, grp_ref[0].astype(BF16), preferred_element_type=F32)
        pooled_ref[...] = pooled
        pg_ref[...] = pg
        ps_ref[...] = (pg * sc_ref[...]).astype(BF16)

    blk = pl.BlockSpec((seq, POOL_GROUP_DIM), lambda b, g: (b, g))
    return pl.pallas_call(
        body, name="pool_fwd", grid=(tokens // seq, POOL_GROUPS),
        out_shape=(jax.ShapeDtypeStruct(u.shape, BF16), jax.ShapeDtypeStruct(u.shape, F32),
                   jax.ShapeDtypeStruct(u.shape, BF16)),
        in_specs=[blk, pl.BlockSpec((1, POOL_GROUP_DIM, POOL_GROUP_DIM), lambda b, g: (g, 0, 0)),
                  pl.BlockSpec((1, POOL_GROUP_DIM), lambda b, g: (0, g))],
        out_specs=(blk, blk, blk),
        compiler_params=_params("parallel", "parallel"),
    )(u, grp, scale)


def _pool_bwd(dps, pooled, pg, grp, scale, seq):
    tokens = dps.shape[0]

    def body(dps_ref, pooled_ref, pg_ref, grp_ref, sc_ref, du_ref, dgrp_ref, dsc_ref):
        g, b = pl.program_id(0), pl.program_id(1)
        dpsv = dps_ref[...]
        dpg = (dpsv * sc_ref[...]).astype(BF16)
        dsc = _colsum(dpsv * pg_ref[...])
        dgrp = lax.dot_general(pooled_ref[...], dpg, _DIMS["tn"], preferred_element_type=F32)

        @pl.when(b == 0)
        def _():
            dsc_ref[...] = dsc
            dgrp_ref[0] = dgrp

        @pl.when(b > 0)
        def _():
            dsc_ref[...] += dsc
            dgrp_ref[0] += dgrp

        dpool = lax.dot_general(dpg, grp_ref[0].astype(BF16), _DIMS["nt"], preferred_element_type=F32)
        sums = _by_group(_window_sums(dpool * _inv_count(dpool.shape, g), False), g)
        du_ref[...] = (sums - dpool).astype(BF16)

    blk = pl.BlockSpec((seq, POOL_GROUP_DIM), lambda g, b: (b, g))
    grp_spec = pl.BlockSpec((1, POOL_GROUP_DIM, POOL_GROUP_DIM), lambda g, b: (g, 0, 0))
    vec_spec = pl.BlockSpec((1, POOL_GROUP_DIM), lambda g, b: (0, g))
    return pl.pallas_call(
        body, name="pool_bwd", grid=(POOL_GROUPS, tokens // seq),
        out_shape=(jax.ShapeDtypeStruct(dps.shape, BF16), jax.ShapeDtypeStruct(grp.shape, F32),
                   jax.ShapeDtypeStruct(scale.shape, F32)),
        in_specs=[blk, blk, blk, grp_spec, vec_spec],
        out_specs=(blk, grp_spec, vec_spec),
        compiler_params=_params("arbitrary", "arbitrary"),
    )(dps, pooled, pg, grp, scale)


def _lane(shape):
    return lax.broadcasted_iota(jnp.int32, shape, len(shape) - 1)


def _rot(y):
    lane = _lane(y.shape)
    r = jnp.where(lane < QK_NOPE + QK_ROPE // 2,
                  -pltpu.roll(y, HEAD_SLAB - QK_ROPE // 2, 1), pltpu.roll(y, QK_ROPE // 2, 1))
    return jnp.where(jnp.logical_and(lane >= QK_NOPE, lane < QK_NOPE + QK_ROPE), r, 0.0)


def _part_rstd(x):
    sq = x * x
    nope = _lane(x.shape) < QK_NOPE
    s_nope = jnp.sum(jnp.where(nope, sq, 0.0), axis=-1, keepdims=True)
    s_rope = jnp.sum(sq, axis=-1, keepdims=True) - s_nope
    return jnp.where(nope, lax.rsqrt(s_nope * (1.0 / QK_NOPE) + NORM_EPS),
                     lax.rsqrt(s_rope * (1.0 / QK_ROPE) + NORM_EPS))


def _part_norm_bwd(dy, x, r, g):
    nope = _lane(x.shape) < QK_NOPE
    xhat = x * r
    dxhat = dy * g
    prod = dxhat * xhat
    m_nope = jnp.sum(jnp.where(nope, prod, 0.0), axis=-1, keepdims=True)
    m_rope = jnp.sum(prod, axis=-1, keepdims=True) - m_nope
    mean = jnp.where(nope, m_nope * (1.0 / QK_NOPE), m_rope * (1.0 / QK_ROPE))
    return r * (dxhat - xhat * mean), dy * xhat


def _mixer_in(h, wt_a, wt_p, wt_g, g_q, g_kv, seq):
    def fn(rows, bats, vecs):
        z_a, z_p, z_g = rows[:3]
        q, kv = z_a[:, :Q_LORA], z_a[:, Q_LORA:Q_LORA + KV_LORA]
        return [z_a, z_p, z_g, q * _rstd(q) * vecs[0], kv * _rstd(kv) * vecs[1]], [], []
    return _rowmap("mix_in", fn, seq, [h], vecs=[g_q, g_kv], lhs=lambda rows, vecs: rows[0],
                   mm=[(wt_a, "nt"), (wt_p, "nt"), (wt_g, "nt")], mm_sum=False,
                   row_outs=[(wt_a.shape[0], F32), (wt_p.shape[0], F32), (wt_g.shape[0], BF16),
                             (Q_LORA, BF16), (KV_LORA, BF16)])


def _latent_norm_bwd(dqp, wtq_pad, dkv, wt_kv, dkr, z_a, g_q, g_kv, seq):
    def fn(rows, bats, vecs):
        dq, dkv, dkrv, z = rows
        q, kv = z[:, :Q_LORA], z[:, Q_LORA:Q_LORA + KV_LORA]
        dxq, dgq = _norm_bwd(dq, q, _rstd(q), vecs[0])
        dxkv, dgkv = _norm_bwd(dkv, kv, _rstd(kv), vecs[1])
        return [jnp.concatenate([dxq, dxkv, dkrv], axis=1)], [], [_colsum(dgq), _colsum(dgkv)]
    return _rowmap("latent_norm_bwd", fn, seq, [dqp, dkv, dkr, z_a], vecs=[g_q, g_kv],
                   row_outs=[(Q_LORA + KV_LORA + HEAD_SLAB, BF16)], vec_outs=[Q_LORA, KV_LORA],
                   mm=[(wtq_pad, "nn"), (wt_kv, "nn")], mm_sum=False)


def _qk_prep_fwd(qn, wtq_pad, kvn, wt_kv, z_a, pos, g_q, g_kn, g_kr, inv_freq, seq):
    def fn(rows, bats, vecs):
        qv, kvv, kr, p = rows
        gq, gkn, gkr, invf = vecs
        ang = p * invf
        cos, sin = jnp.cos(ang), jnp.sin(ang)
        nope = _lane(kr.shape) < QK_NOPE
        krn = kr * _rstd(kr, QK_ROPE) * gkr
        krr = krn * cos + _rot(krn) * sin
        qs, ks, vs = [], [], []
        for h in range(N_HEADS):
            xq = qv[:, h * HEAD_SLAB:(h + 1) * HEAD_SLAB]
            y = xq * _part_rstd(xq) * gq
            qs.append(y * cos + _rot(y) * sin)
            xk = kvv[:, h * HEAD_SLAB:(h + 1) * HEAD_SLAB]
            kn = jnp.where(nope, xk, 0.0)
            ks.append(jnp.where(nope, kn * _rstd(kn, QK_NOPE) * gkn, krr))
            vs.append(jnp.where(nope, 0.0, xk))
        return [jnp.concatenate(v, axis=1) for v in (qs, ks, vs)] + [qv, kvv], [], []
    width = N_HEADS * HEAD_SLAB
    return _rowmap("qk_prep", fn, seq, [qn, kvn, (z_a, HEAD_SLAB, 5), pos], vecs=[g_q, g_kn, g_kr, inv_freq],
                   row_outs=[(width, BF16)] * 3 + [(width, F32)] * 2, mm=[(wtq_pad, "nt"), (wt_kv, "nt")],
                   mm_sum=False)


def _qk_prep_bwd(dqc, dkc, dvp, qp, kv, z_a, pos, g_q, g_kn, g_kr, inv_freq, seq):
    def fn(rows, bats, vecs):
        dq, dk, dv, qv, kvv, kr, p = rows
        gq, gkn, gkr, invf = vecs
        ang = p * invf
        cos, sin = jnp.cos(ang), jnp.sin(ang)
        nope = _lane(kr.shape) < QK_NOPE
        dqs, dkvs = [], []
        dgq = jnp.zeros((1, HEAD_SLAB), F32)
        dgkn = jnp.zeros((1, HEAD_SLAB), F32)
        dkrr = jnp.zeros(kr.shape, F32)
        for h in range(N_HEADS):
            sl = slice(h * HEAD_SLAB, (h + 1) * HEAD_SLAB)
            dyr = dq[:, sl]
            dy = dyr * cos - _rot(dyr * sin)
            xq = qv[:, sl]
            dx, dg = _part_norm_bwd(dy, xq, _part_rstd(xq), gq)
            dqs.append(dx)
            dgq = dgq + _colsum(dg)
            dkh = dk[:, sl]
            dkrr = dkrr + jnp.where(nope, 0.0, dkh)
            kn = jnp.where(nope, kvv[:, sl], 0.0)
            dxk, dgk = _norm_bwd(jnp.where(nope, dkh, 0.0), kn, _rstd(kn, QK_NOPE), gkn, QK_NOPE)
            dgkn = dgkn + _colsum(dgk)
            dkvs.append(jnp.where(nope, dxk, dv[:, sl]))
        dkrn = dkrr * cos - _rot(dkrr * sin)
        dkr, dgkr = _norm_bwd(dkrn, kr, _rstd(kr, QK_ROPE), gkr, QK_ROPE)
        return ([jnp.concatenate(dqs, axis=1), jnp.concatenate(dkvs, axis=1), dkr], [],
                [dgq, dgkn, _colsum(dgkr)])
    width = N_HEADS * HEAD_SLAB
    return _rowmap("qk_prep_bwd", fn, seq, [dqc, dkc, dvp, qp, kv, (z_a, HEAD_SLAB, 5), pos],
                   vecs=[g_q, g_kn, g_kr, inv_freq],
                   row_outs=[(width, BF16), (width, BF16), (HEAD_SLAB, F32)],
                   vec_outs=[HEAD_SLAB] * 3, ts=_tile(seq, (512, 256, 128, 64, 32, 16, 8)))


def _scores(q, k_ref, keys, tq):
    s = lax.dot_general(q, k_ref[0:keys, :], _DIMS["nt"], preferred_element_type=F32) * ATTN_SCALE
    row = lax.broadcasted_iota(jnp.int32, (tq, tq), 0)
    col = lax.broadcasted_iota(jnp.int32, (tq, tq), 1)
    diag = jnp.where(col <= row, s[:, keys - tq:], -1e30)
    return diag if keys == tq else jnp.concatenate([s[:, :keys - tq], diag], axis=1)


def _attn_fwd(qc, kc, vp, seq):
    tokens = qc.shape[0]
    tq = _tile(seq, (256, 128))
    nq = seq // tq

    def body(q_ref, k_ref, v_ref, o_ref, lse_ref):
        for i in range(nq):
            rows, keys = slice(i * tq, (i + 1) * tq), (i + 1) * tq
            s = _scores(q_ref[rows, :], k_ref, keys, tq)
            m = jnp.max(s, axis=-1, keepdims=True)
            p = jnp.exp(s - m)
            l = jnp.sum(p, axis=-1, keepdims=True)
            acc = jnp.dot(p.astype(BF16), v_ref[0:keys, :], preferred_element_type=F32)
            o_ref[rows, :] = (acc / l).astype(BF16)
            lse_ref[rows, :] = jnp.broadcast_to(m + jnp.log(l), (tq, HEAD_SLAB))

    spec = pl.BlockSpec((seq, HEAD_SLAB), lambda b, h: (b, h))
    return pl.pallas_call(
        body, name="attn_fwd", grid=(tokens // seq, N_HEADS),
        out_shape=(jax.ShapeDtypeStruct(qc.shape, BF16), jax.ShapeDtypeStruct(qc.shape, F32)),
        in_specs=[spec] * 3, out_specs=(spec, spec),
        compiler_params=_params("parallel", "parallel"),
    )(qc, kc, vp)


def _attn_bwd(qc, kc, vp, o, lse, do, seq):
    tokens = qc.shape[0]
    tq = _tile(seq, (256, 128))
    nq = seq // tq

    def body(q_ref, k_ref, v_ref, o_ref, lse_ref, do_ref, dq_ref, dk_ref, dv_ref):
        dk_ref[...] = jnp.zeros(dk_ref.shape, F32)
        dv_ref[...] = jnp.zeros(dv_ref.shape, F32)
        for i in range(nq):
            rows, keys = slice(i * tq, (i + 1) * tq), (i + 1) * tq
            q, dov = q_ref[rows, :], do_ref[rows, :]
            delta = jnp.sum(dov.astype(F32) * o_ref[rows, :].astype(F32), axis=-1, keepdims=True)
            s = _scores(q, k_ref, keys, tq)
            p = jnp.exp(s - jnp.tile(lse_ref[rows, :], (1, keys // HEAD_SLAB)))
            dp = lax.dot_general(dov, v_ref[0:keys, :], _DIMS["nt"], preferred_element_type=F32)
            ds = (p * (dp - delta) * ATTN_SCALE).astype(BF16)
            dq_ref[rows, :] = jnp.dot(ds, k_ref[0:keys, :], preferred_element_type=F32)
            dk_ref[0:keys, :] += lax.dot_general(ds, q, _DIMS["tn"], preferred_element_type=F32)
            dv_ref[0:keys, :] += lax.dot_general(p.astype(BF16), dov, _DIMS["tn"], preferred_element_type=F32)

    spec = pl.BlockSpec((seq, HEAD_SLAB), lambda b, h: (b, h))
    out = jax.ShapeDtypeStruct(qc.shape, F32)
    return pl.pallas_call(
        body, name="attn_bwd", grid=(tokens // seq, N_HEADS),
        out_shape=(out, out, out), in_specs=[spec] * 6, out_specs=(spec, spec, spec),
        compiler_params=_params("parallel", "parallel"),
    )(qc, kc, vp, o, lse, do)


def _adamw(w, g, m, v, name):
    rows, cols = w.shape
    whole = rows * cols * 4 <= ADAMW_WHOLE_BYTES
    tr = rows if whole else _tile(rows, (256, 128, 64, 32, 16, 8))
    c1 = 1.0 - ADAM_B1 ** ADAM_STEP
    c2 = 1.0 - ADAM_B2 ** ADAM_STEP

    def body(w_ref, g_ref, m_ref, v_ref, d_ref, nm_ref, nv_ref):
        gv = g_ref[...]
        nm = ADAM_B1 * m_ref[...] + (1.0 - ADAM_B1) * gv
        nv = ADAM_B2 * v_ref[...] + (1.0 - ADAM_B2) * (gv * gv)
        d_ref[...] = -ADAM_LR * ((nm / c1) / (jnp.sqrt(nv / c2) + ADAM_EPS) + ADAM_WD * w_ref[...])
        nm_ref[...] = nm
        nv_ref[...] = nv

    spec = pl.BlockSpec((tr, cols), lambda i: (i, 0))
    out = jax.ShapeDtypeStruct(w.shape, F32)
    return pl.pallas_call(
        body, name=name, grid=(rows // tr,), out_shape=(out, out, out),
        in_specs=[spec] * 4, out_specs=(spec, spec, spec),
        compiler_params=_params("parallel"),
    )(w, g, m, v)


def _adamw_landed(w, landed, m, v, name):
    rows, cols = w.shape
    tr = _tile(rows, (176, 128, 96, 64, 32, 16, 8))
    c1 = 1.0 - ADAM_B1 ** ADAM_STEP
    c2 = 1.0 - ADAM_B2 ** ADAM_STEP
    n_parts = len(landed)

    def body(*refs):
        w_ref, m_ref, v_ref = refs[:3]
        g_ref, d_ref, nm_ref, nv_ref = refs[3 + n_parts:]
        parts = []
        for x_ref in refs[3:3 + n_parts]:
            acc = x_ref[0].astype(F32)
            for d in range(1, N_DEV):
                acc = acc + x_ref[d].astype(F32)
            parts.append(acc)
        gv = parts[0] if n_parts == 1 else jnp.concatenate(parts, axis=1)
        nm = ADAM_B1 * m_ref[...] + (1.0 - ADAM_B1) * gv
        nv = ADAM_B2 * v_ref[...] + (1.0 - ADAM_B2) * (gv * gv)
        g_ref[...] = gv
        d_ref[...] = -ADAM_LR * ((nm / c1) / (jnp.sqrt(nv / c2) + ADAM_EPS) + ADAM_WD * w_ref[...])
        nm_ref[...] = nm
        nv_ref[...] = nv

    spec = pl.BlockSpec((tr, cols), lambda i: (i, 0))
    out = jax.ShapeDtypeStruct(w.shape, F32)
    return pl.pallas_call(
        body, name=name, grid=(rows // tr,), out_shape=(out, out, out, out),
        in_specs=[spec] * 3 + [pl.BlockSpec((N_DEV, tr, x.shape[2]), lambda i: (0, i, 0)) for x in landed],
        out_specs=(spec, spec, spec, spec),
        compiler_params=_params("parallel"),
    )(w, m, v, *landed)


def _mod_cols(c_all, w_ada, b_cols):
    def body(c_ref, w_ref, b_ref, act_ref, mod_ref):
        cv = c_ref[...]
        act = cv * _sigmoid(cv)
        act_ref[...] = act
        mod_ref[...] = jnp.dot(act.astype(BF16), w_ref[...].astype(BF16),
                               preferred_element_type=F32) + b_ref[...]

    n = w_ada.shape[1]
    return pl.pallas_call(
        body, name="mod_cols",
        out_shape=(jax.ShapeDtypeStruct(c_all.shape, F32), jax.ShapeDtypeStruct((c_all.shape[0], n), F32)),
        compiler_params=pltpu.CompilerParams(vmem_limit_bytes=VMEM_LIMIT),
    )(c_all, w_ada, b_cols)


def _ada_grads(c_act, dmod_all, dmod_cols):
    def body(c_ref, d_ref, dc_ref, gw_ref, gb_ref):
        gw_ref[...] = lax.dot_general(c_ref[...].astype(BF16), dc_ref[...].astype(BF16), _DIMS["tn"],
                                      preferred_element_type=F32)
        gb_ref[...] = _colsum(d_ref[...])

    return pl.pallas_call(
        body, name="ada_grads",
        out_shape=(jax.ShapeDtypeStruct((c_act.shape[1], dmod_cols.shape[1]), F32),
                   jax.ShapeDtypeStruct((1, dmod_all.shape[1]), F32)),
        compiler_params=pltpu.CompilerParams(vmem_limit_bytes=VMEM_LIMIT),
    )(c_act, dmod_all, dmod_cols)


def _flat_rows(a):
    flat = a.reshape(-1)
    pad = (-flat.shape[0]) % (LANES * SUBLANES)
    if pad:
        flat = jnp.pad(flat, (0, pad))
    return flat.reshape(-1, LANES)


def _gather_start(w, groups, tag, after=None, peers=None):
    shards = [[(w[n] if n in ROW_SHARDED else w[n].T).astype(BF16) for n in names] for names in groups]
    return _exchange_start_groups(shards, f"gather_{tag}_start", after=after, peers=peers)


def _gather_wait(handle, names, tag, after, peers=ALL_PEERS):
    landed = _exchange_wait(handle, f"gather_{tag}_wait", after=after, peers=peers)
    if peers == CHIP_PEERS:
        landed = [_sibling_forward(x, f"gather_{tag}_forward{i}") for i, x in enumerate(landed)]
    return {n: g.reshape(-1, g.shape[2]) for n, g in zip(names, landed)}


def _scatter_start(grads, names, tag, after=None):
    blocks = [grads[n].reshape(N_DEV, -1, grads[n].shape[1]) for n in names]
    return _exchange_start(blocks, f"scatter_{tag}_start", scatter=True, after=after)


def _scatter_wait(handle, names, tag, after):
    landed = _exchange_wait(handle, f"scatter_{tag}_wait", scatter=True, after=after)
    return {n: [x] for n, x in zip(names, landed)}


def _pack_small(vals):
    return jnp.concatenate([_flat_rows(v.astype(F32)) for v in vals], axis=0)


def _unpack_small(packed, like):
    out, row = [], 0
    for v in like:
        rows = _flat_rows(v).shape[0]
        out.append(packed[row:row + rows].reshape(-1)[:v.size].reshape(v.shape))
        row += rows
    return out


def _lanes128(*parts):
    out = jnp.zeros((HEAD_SLAB,), F32)
    for off, v in parts:
        out = lax.dynamic_update_slice(out, v.reshape(-1).astype(F32), (off,))
    return out.reshape(1, HEAD_SLAB)


def _step(x, c, positions, w, m, v, loss_target):
    nseq, seq, _ = x.shape
    tokens = nseq * seq
    me = _index(_my_pos())
    strip = lambda d: {n: (a[0] if a.ndim > 2 else a) for n, a in d.items()}
    shapes = {n: a.shape for n, a in w.items()}
    w, m, v = strip(w), strip(m), strip(v)

    c_all = _all_gather(c.reshape(-1, LANES), "gather_c").reshape(N_DEV * nseq, D_MODEL)
    n_ada = w["w_ada"].shape[1]
    b_cols = lax.dynamic_slice(w["b_ada"], (0, me * n_ada), (1, n_ada))
    c_act, mod_cols = _mod_cols(c_all, w["w_ada"], b_cols)
    mod_all = _all_gather(mod_cols, "gather_mod")
    mod = lax.dynamic_slice(mod_all, (0, me * nseq, 0), (N_DEV, nseq, n_ada))
    mod = mod.transpose(1, 0, 2).reshape(nseq, 3, 3, 1, D_MODEL)

    (h_f1i, h_f1o, h_mix_in, h_mix, h_f2), tok = _gather_start(
        w, (("w_ffn1_in",), ("w_ffn1_out",), MIXER[:1], MIXER[1:], ("w_ffn2_in", "w_ffn2_out")), "weights",
        after=mod_all, peers=[CHIP_PEERS] + [ALL_PEERS] * 4)
    started = tok[0:1, 0:1]

    g_q = _lanes128((0, w["q_norm_nope"]), (QK_NOPE, w["q_norm_rope"]))
    g_kn = _lanes128((0, w["k_norm_nope"]))
    g_kr = _lanes128((QK_NOPE, w["k_norm_rope"]))
    freq = ROPE_THETA ** (-jnp.arange(0, QK_ROPE, 2, dtype=F32) / QK_ROPE)
    inv_freq = _lanes128((QK_NOPE, jnp.concatenate([freq, freq])))
    pos = positions.reshape(tokens, 1).astype(F32)

    def sub(k, gamma, coef):
        return dict(gamma=w[gamma], shift=mod[:, k, 0] + started, scale=mod[:, k, 1], gate=coef * mod[:, k, 2])
    p1, pm, p2 = sub(0, "norm_ffn1", 0.5), sub(1, "norm_mix", 1.0), sub(2, "norm_ffn2", 0.5)

    x0 = x.reshape(tokens, D_MODEL)
    h1 = _norm_mod_fwd(x0, p1, seq, "ffn1_norm")
    wt_f1i = _gather_wait(h_f1i, ("w_ffn1_in",), "ffn1_in", h1, peers=CHIP_PEERS)["w_ffn1_in"]
    g1, u1, a1 = _ffn_in_act(h1, wt_f1i, "ffn1_in")
    w_f1o = _gather_wait(h_f1o, ("w_ffn1_out",), "ffn1_out", a1)["w_ffn1_out"]
    x1, f1, h2 = _out_residual(a1, w_f1o, x0, p1["gate"], pm, seq, "ffn1_out")
    saved1 = (x0, h1, g1, u1, a1, wt_f1i, w_f1o)

    wt_in = _gather_wait(h_mix_in, MIXER[:1], "mix_in", h2)["w_in"]
    zero_rows = lambda rows: jnp.zeros((rows, D_MODEL), BF16)
    wt_p = wt_in[:512]
    wt_a = jnp.concatenate([wt_in[512:1152], zero_rows(QK_NOPE), wt_in[1152:1184], zero_rows(32)], axis=0)
    wt_g = wt_in[1184:]
    z_a, z_p, z_g, qn, kvn = _mixer_in(h2, wt_a, wt_p, wt_g, w["q_a_norm"], w["kv_a_norm"], seq)

    full = _gather_wait(h_mix, MIXER[1:], "mix", z_g)
    wtq_pad = jnp.pad(full["w_q_up"].reshape(N_HEADS, 96, Q_LORA), ((0, 0), (0, 32), (0, 0))).reshape(-1, Q_LORA)
    wtmla_pad = jnp.pad(full["w_mla_proj"].reshape(D_MODEL, N_HEADS, 64), ((0, 0), (0, 0), (64, 0))).reshape(D_MODEL, -1)
    wt_pool, wt_kv, w_mix_out = full["w_pool_proj"], full["w_kv_up"], full["w_out"]
    pooled, pg, ps = _pool_fwd(z_p, w["pool_grp"], w["pool_scale"], seq)
    qc, kc, vp, qp, kv = _qk_prep_fwd(qn, wtq_pad, kvn, wt_kv, z_a, pos, g_q, g_kn, g_kr, inv_freq, seq)
    attn, lse = _attn_fwd(qc, kc, vp, seq)
    x2, o_mix, h3, merged, br_pool, br_mla = _mix_out(z_g, ps, attn, x1, wt_pool, wtmla_pad, w_mix_out, pm["gate"],
                                                      p2, seq)

    ffn2_w = _gather_wait(h_f2, ("w_ffn2_in", "w_ffn2_out"), "ffn2", h3)
    g2, u2, a2 = _ffn_in_act(h3, ffn2_w["w_ffn2_in"], "ffn2_in")
    dy, df2, dgate2, sq_err = _out_loss(a2, ffn2_w["w_ffn2_out"], x2, p2["gate"],
                                        loss_target.reshape(tokens, D_MODEL), seq, "ffn2_out")
    saved2 = (x2, h3, g2, u2, a2, ffn2_w["w_ffn2_in"], ffn2_w["w_ffn2_out"])

    grads = {}
    (dx2, do_mix, dsh2, dsc2, dgate_m, dg_ffn2), ops2 = _ffn_bwd_x(df2, dy, saved2, p2, seq, "ffn2", (o_mix, pm["gate"]))
    grads["w_ffn2_out"], grads["w_ffn2_in"] = _ffn_bwd_wout(ops2[0], ops2[1], "ffn2"), _ffn_bwd_win(ops2, "ffn2")
    s_f2, tok = _scatter_start(grads, ("w_ffn2_in", "w_ffn2_out"), "ffn2")

    grads["w_out"] = _mm(merged, do_mix, "tn", "mix_bwd_wout", out_dtype=BF16, tm=512, tn=D_MODEL)

    def merge_bwd(rows, bats, vecs):
        dmv, zg, bp, bm = (r.astype(F32) for r in rows)
        s_p, s_m = _sigmoid(zg[:, :D_MODEL]), _sigmoid(zg[:, D_MODEL:])
        dzg = jnp.concatenate([dmv * bp * s_p * (1.0 - s_p), dmv * bm * s_m * (1.0 - s_m)], axis=1)
        dbp, dbm = (dmv * s_p).astype(BF16), (dmv * s_m).astype(BF16)
        dps_v = jnp.dot(dbp, vecs[0], preferred_element_type=F32)
        dattn = jnp.dot(dbm, vecs[1], preferred_element_type=F32)
        return [dbp, dbm, dzg, dps_v, dattn], [], []
    dbr_pool, dbr_mla, dz_g, dps, d_attn = _rowmap(
        "mix_bwd_dmerged", merge_bwd, seq, [do_mix, z_g, br_pool, br_mla], vecs=[wt_pool, wtmla_pad],
        row_outs=[(D_MODEL, BF16), (D_MODEL, BF16), (2 * D_MODEL, BF16), (POOL_WIDTH, F32), (D_MODEL, BF16)],
        mm=(w_mix_out, "nt"))

    grads["w_pool_proj"] = _mm(dbr_pool, ps, "tn", "pool_bwd_wproj", out_dtype=BF16, tm=512, tn=POOL_WIDTH)
    dz_p, dgrp, dpool_scale = _pool_bwd(dps, pooled, pg, w["pool_grp"], w["pool_scale"] + tok[0:1, 0:1], seq)

    dwtmla_pad = _mm(dbr_mla, attn, "tn", "mla_bwd_wproj", out_dtype=BF16, tm=512, tn=D_MODEL)
    grads["w_mla_proj"] = dwtmla_pad.reshape(D_MODEL, N_HEADS, HEAD_SLAB)[:, :, 64:].reshape(D_MODEL, -1)
    dqc, dkc, dvp = _attn_bwd(qc, kc, vp, attn, lse, d_attn, seq)
    dqp, dkv, dkr, dg_q, dg_kn, dg_kr = _qk_prep_bwd(dqc, dkc, dvp, qp, kv, z_a, pos, g_q, g_kn, g_kr, inv_freq, seq)
    dwtq_pad = _mm(dqp, qn, "tn", "q_up_bwd_w", out_dtype=BF16, tm=512, tn=Q_LORA)
    grads["w_q_up"] = dwtq_pad.reshape(N_HEADS, HEAD_SLAB, Q_LORA)[:, :96].reshape(-1, Q_LORA)
    grads["w_kv_up"] = _mm(dkv, kvn, "tn", "kv_up_bwd_w", out_dtype=BF16, tm=512, tn=KV_LORA)
    dz_a, dg_qa, dg_kva = _latent_norm_bwd(dqp, wtq_pad, dkv, wt_kv, dkr, z_a, w["q_a_norm"], w["kv_a_norm"], seq)

    dwt_a, dwt_p, dwt_g = _mm_tn_shared([dz_a, dz_p, dz_g], h2, "mix_in_bwd_w")
    grads["w_in"] = jnp.concatenate([dwt_p, dwt_a[:640], dwt_a[704:736], dwt_g], axis=0)

    small_early = [dg_ffn2.reshape(w["norm_ffn2"].shape), dgrp, dpool_scale, dg_qa, dg_kva, dg_q[:, :QK_NOPE],
                   dg_q[:, QK_NOPE:QK_NOPE + QK_ROPE], dg_kn[:, :QK_NOPE], dg_kr[:, QK_NOPE:QK_NOPE + QK_ROPE]]
    s_small, tok = _exchange_start([_pack_small(small_early)], "gather_small_start")
    s_mix, tok = _scatter_start(grads, MIXER, "mix", after=tok)
    dh2 = [(dz_a, wt_a), (dz_p, wt_p), (dz_g, wt_g)]
    pm_tied = dict(pm, scale=pm["scale"] + tok[0:1, 0:1])
    dx1, df1, dsh_m, dsc_m, dgate1, dg_mix = _norm_mod_bwd(dh2, x1, dx2, pm_tied, seq, "mix_bwd_norm", (f1, p1["gate"]))

    handles = {}

    def ffn1_early(a, df):
        grads["w_ffn1_out"] = _ffn_bwd_wout(a, df, "ffn1")
        handles["f1o"], token = _scatter_start(grads, ("w_ffn1_out",), "ffn1_out")
        return token

    def ffn1_mid(operands):
        first = _ffn_bwd_win(operands, "ffn1", half=0)
        handles["f1i0"], token = _exchange_start([first.reshape(N_DEV, -1, first.shape[1])],
                                                 "scatter_ffn1_in0_start", scatter=True)
        return token

    (dx0, dsh1, dsc1, dg_ffn1), ops1 = _ffn_bwd_x(df1, dx1, saved1, p1, seq, "ffn1", early=ffn1_early,
                                                     mid=ffn1_mid)
    s_f1o = handles["f1o"]

    dmod = jnp.stack([jnp.stack([dsh1, dsc1, 0.5 * dgate1], axis=1),
                      jnp.stack([dsh_m, dsc_m, dgate_m], axis=1),
                      jnp.stack([dsh2, dsc2, 0.5 * dgate2], axis=1)], axis=1)
    n_dmod = nseq * 9 * D_MODEL // LANES
    tail = _all_gather(jnp.concatenate([dmod.reshape(-1, LANES), _flat_rows(dg_ffn1), _flat_rows(dg_mix),
                                        _flat_rows(sq_err)], axis=0), "gather_dmod")
    dmod_all = tail[:, :n_dmod].reshape(N_DEV * nseq, 9 * D_MODEL)

    second = _ffn_bwd_win(ops1, "ffn1", after=tail, half=1)
    s_second, tok = _exchange_start([second.reshape(N_DEV, -1, second.shape[1])], "scatter_ffn1_in1_start",
                                    scatter=True, after=tail)
    s_f1i = (handles["f1i0"], s_second)

    dmod_cols = lax.dynamic_slice(dmod_all, (0, me * n_ada), (N_DEV * nseq, n_ada)) + tok[0:1, 0:1]
    g_w_ada, g_b_ada = _ada_grads(c_act, dmod_all, dmod_cols)
    tail_sum = _sum_blocks(tail[:, n_dmod:], "sum_tail")
    g_norm_ffn1 = tail_sum[:SUBLANES].reshape(1, D_MODEL)
    g_norm_mix = tail_sum[SUBLANES:2 * SUBLANES].reshape(1, D_MODEL)
    loss = 0.5 * jnp.sum(tail_sum[2 * SUBLANES:]) * (1.0 / D_MODEL)
    small_all = _exchange_wait(s_small, "gather_small_wait", after=g_b_ada)[0]
    small_sum = _sum_blocks(small_all, "sum_small")
    small = dict(zip(SMALL[2:], _unpack_small(small_sum, [w[n] for n in SMALL[2:]])))
    grad_w = dict(small, w_ada=g_w_ada, b_ada=g_b_ada, norm_ffn1=g_norm_ffn1, norm_mix=g_norm_mix)

    delta, new_m, new_v = {}, {}, {}

    def update(names, landed=None):
        for n in names:
            if landed is None:
                delta[n], new_m[n], new_v[n] = _adamw(w[n], grad_w[n], m[n], v[n], f"adamw_{n}")
            elif n in KEPT_TRANSPOSED:
                res = _adamw_landed(w[n].T, landed[n], m[n].T, v[n].T, f"adamw_{n}")
                grad_w[n], delta[n], new_m[n], new_v[n] = (r.T for r in res)
            elif n in ROW_SHARDED:
                grad_w[n], delta[n], new_m[n], new_v[n] = _adamw_landed(w[n], landed[n], m[n], v[n], f"adamw_{n}")
            else:
                grad_w[n] = _sum_blocks(landed[n][0], f"sum_{n}").T
                delta[n], new_m[n], new_v[n] = _adamw(w[n], grad_w[n], m[n], v[n], f"adamw_{n}")

    update(("w_ada",))
    rep = ("b_ada",) + SMALL
    d_s, m_s, v_s = _adamw(_pack_small([w[n] for n in rep]), _pack_small([grad_w[n] for n in rep]),
                           _pack_small([m[n] for n in rep]), _pack_small([v[n] for n in rep]), "adamw_small")
    like = [w[n] for n in rep]
    for dst, packed in ((delta, d_s), (new_m, m_s), (new_v, v_s)):
        dst.update(zip(rep, _unpack_small(packed, like)))
    update(("w_ffn2_in", "w_ffn2_out"), _scatter_wait(s_f2, ("w_ffn2_in", "w_ffn2_out"), "ffn2", after=d_s))
    update(MIXER, _scatter_wait(s_mix, MIXER, "mix", after=delta["w_ffn2_out"]))
    update(("w_ffn1_out",), _scatter_wait(s_f1o, ("w_ffn1_out",), "ffn1_out", after=delta["w_out"]))
    halves = [_exchange_wait(h, f"scatter_ffn1_in{i}_wait", scatter=True, after=delta["w_ffn1_out"])[0]
              for i, h in enumerate(s_f1i)]
    update(("w_ffn1_in",), {"w_ffn1_in": halves})

    lead = lambda d: [d[n].reshape(shapes[n]) for n in WEIGHTS]
    return (loss, dx0.reshape(x.shape), *lead(grad_w), *lead(delta), *lead(new_m), *lead(new_v))


def kernel(x, c, positions, w_ada, b_ada, norm_ffn1, w_ffn1_in, w_ffn1_out, norm_mix, w_in, pool_grp, pool_scale, w_pool_proj, q_a_norm, w_q_up, kv_a_norm, w_kv_up, q_norm_nope, q_norm_rope, k_norm_nope, k_norm_rope, w_mla_proj, w_out, norm_ffn2, w_ffn2_in, w_ffn2_out, loss_target, m_w_ada, m_b_ada, m_norm_ffn1, m_w_ffn1_in, m_w_ffn1_out, m_norm_mix, m_w_in, m_pool_grp, m_pool_scale, m_w_pool_proj, m_q_a_norm, m_w_q_up, m_kv_a_norm, m_w_kv_up, m_q_norm_nope, m_q_norm_rope, m_k_norm_nope, m_k_norm_rope, m_w_mla_proj, m_w_out, m_norm_ffn2, m_w_ffn2_in, m_w_ffn2_out, v_w_ada, v_b_ada, v_norm_ffn1, v_w_ffn1_in, v_w_ffn1_out, v_norm_mix, v_w_in, v_pool_grp, v_pool_scale, v_w_pool_proj, v_q_a_norm, v_w_q_up, v_kv_a_norm, v_w_kv_up, v_q_norm_nope, v_q_norm_rope, v_k_norm_nope, v_k_norm_rope, v_w_mla_proj, v_w_out, v_norm_ffn2, v_w_ffn2_in, v_w_ffn2_out):
    w = dict(w_ada=w_ada, b_ada=b_ada, norm_ffn1=norm_ffn1, w_ffn1_in=w_ffn1_in, w_ffn1_out=w_ffn1_out, norm_mix=norm_mix, w_in=w_in, pool_grp=pool_grp, pool_scale=pool_scale, w_pool_proj=w_pool_proj, q_a_norm=q_a_norm, w_q_up=w_q_up, kv_a_norm=kv_a_norm, w_kv_up=w_kv_up, q_norm_nope=q_norm_nope, q_norm_rope=q_norm_rope, k_norm_nope=k_norm_nope, k_norm_rope=k_norm_rope, w_mla_proj=w_mla_proj, w_out=w_out, norm_ffn2=norm_ffn2, w_ffn2_in=w_ffn2_in, w_ffn2_out=w_ffn2_out)
    m = dict(w_ada=m_w_ada, b_ada=m_b_ada, norm_ffn1=m_norm_ffn1, w_ffn1_in=m_w_ffn1_in, w_ffn1_out=m_w_ffn1_out, norm_mix=m_norm_mix, w_in=m_w_in, pool_grp=m_pool_grp, pool_scale=m_pool_scale, w_pool_proj=m_w_pool_proj, q_a_norm=m_q_a_norm, w_q_up=m_w_q_up, kv_a_norm=m_kv_a_norm, w_kv_up=m_w_kv_up, q_norm_nope=m_q_norm_nope, q_norm_rope=m_q_norm_rope, k_norm_nope=m_k_norm_nope, k_norm_rope=m_k_norm_rope, w_mla_proj=m_w_mla_proj, w_out=m_w_out, norm_ffn2=m_norm_ffn2, w_ffn2_in=m_w_ffn2_in, w_ffn2_out=m_w_ffn2_out)
    v = dict(w_ada=v_w_ada, b_ada=v_b_ada, norm_ffn1=v_norm_ffn1, w_ffn1_in=v_w_ffn1_in, w_ffn1_out=v_w_ffn1_out, norm_mix=v_norm_mix, w_in=v_w_in, pool_grp=v_pool_grp, pool_scale=v_pool_scale, w_pool_proj=v_w_pool_proj, q_a_norm=v_q_a_norm, w_q_up=v_w_q_up, kv_a_norm=v_kv_a_norm, w_kv_up=v_w_kv_up, q_norm_nope=v_q_norm_nope, q_norm_rope=v_q_norm_rope, k_norm_nope=v_k_norm_nope, k_norm_rope=v_k_norm_rope, w_mla_proj=v_w_mla_proj, w_out=v_w_out, norm_ffn2=v_norm_ffn2, w_ffn2_in=v_w_ffn2_in, w_ffn2_out=v_w_ffn2_out)
    return _step(x, c, positions, w, m, v, loss_target)
```

```python
import functools
import math

import jax
import jax.numpy as jnp
from jax import lax
from jax.experimental import pallas as pl
from jax.experimental.pallas import tpu as pltpu

F32 = jnp.float32
BF16 = jnp.bfloat16
MESH = pl.DeviceIdType.MESH
AXES = ("x", "y", "c")
N_DEV = 8

D_MODEL = 1024
D_FF = 2816
N_HEADS = 8
HEAD_SLAB = 128
QK_NOPE = 64
QK_ROPE = 32
POOL_WIDTH = 512
POOL_GROUPS = 4
POOL_GROUP_DIM = 128
Q_LORA = 384
KV_LORA = 256
ROPE_THETA = 10000.0
ATTN_SCALE = 1.0 / math.sqrt(QK_NOPE + QK_ROPE)
NORM_EPS = 1e-6
ADAM_LR, ADAM_B1, ADAM_B2, ADAM_EPS, ADAM_WD, ADAM_STEP = 0.001, 0.9, 0.999, 1e-08, 0.01, 10

LANES = 128
SUBLANES = 8
VMEM_LIMIT = 52 * 1024 * 1024
ADAMW_WHOLE_BYTES = 3 << 19
SUM_WHOLE_BYTES = 4 << 20

BIG = ("w_ffn1_in", "w_ffn1_out", "w_in", "w_pool_proj", "w_q_up", "w_kv_up",
       "w_mla_proj", "w_out", "w_ffn2_in", "w_ffn2_out")
ROW_SHARDED = ("w_ffn1_out", "w_out", "w_ffn2_out")
MIXER = ("w_in", "w_pool_proj", "w_q_up", "w_kv_up", "w_mla_proj", "w_out")
KEPT_TRANSPOSED = ("w_ffn1_in", "w_ffn2_in", "w_in", "w_q_up")
SMALL = ("norm_ffn1", "norm_mix", "norm_ffn2", "pool_grp", "pool_scale", "q_a_norm",
         "kv_a_norm", "q_norm_nope", "q_norm_rope", "k_norm_nope", "k_norm_rope")
WEIGHTS = ("w_ada", "b_ada", "norm_ffn1", "w_ffn1_in", "w_ffn1_out", "norm_mix", "w_in",
           "pool_grp", "pool_scale", "w_pool_proj", "q_a_norm", "w_q_up", "kv_a_norm",
           "w_kv_up", "q_norm_nope", "q_norm_rope", "k_norm_nope", "k_norm_rope",
           "w_mla_proj", "w_out", "norm_ffn2", "w_ffn2_in", "w_ffn2_out")


def _params(*sem):
    return pltpu.CompilerParams(dimension_semantics=sem, vmem_limit_bytes=VMEM_LIMIT)


def _tile(n, cands):
    for c in cands:
        if n % c == 0:
            return c
    return n


def _my_pos():
    return lax.axis_index("x"), lax.axis_index("y"), lax.axis_index("c")


def _flip(pos, k):
    x, y, c = pos
    fx, fy, fc = (k >> 2) & 1, (k >> 1) & 1, k & 1
    return ((1 - x) if fx else x, (1 - y) if fy else y, (1 - c) if fc else c)


def _index(pos):
    x, y, c = pos
    return 4 * x + 2 * y + c


def _exchange(arrays, name, scatter=False):
    n = len(arrays)

    def body(*refs):
        ins, outs = refs[:n], refs[n:2 * n]
        send_sems, recv_sems, local_sems = refs[2 * n:]
        me = _my_pos()
        mine, sends = [], []
        for a in range(n):
            own = ins[a].at[_index(me)] if scatter else ins[a]
            cp = pltpu.make_async_copy(own, outs[a].at[_index(me)], local_sems.at[a])
            cp.start()
            mine.append(cp)
        for k in range(1, N_DEV):
            peer = _flip(me, k)
            for a in range(n):
                cp = pltpu.make_async_remote_copy(
                    src_ref=ins[a].at[_index(peer)] if scatter else ins[a],
                    dst_ref=outs[a].at[_index(me)],
                    send_sem=send_sems.at[a, k - 1], recv_sem=recv_sems.at[a, k - 1],
                    device_id=peer, device_id_type=MESH)
                cp.start()
                sends.append(cp)
        for k in range(1, N_DEV):
            peer = _flip(me, k)
            for a in range(n):
                pltpu.make_async_remote_copy(
                    src_ref=ins[a].at[_index(me)] if scatter else ins[a],
                    dst_ref=outs[a].at[_index(peer)],
                    send_sem=send_sems.at[a, k - 1], recv_sem=recv_sems.at[a, k - 1],
                    device_id=peer, device_id_type=MESH).wait_recv()
        for cp in sends:
            cp.wait_send()
        for cp in mine:
            cp.wait()

    shape = lambda x: x.shape if scatter else (N_DEV,) + x.shape
    return pl.pallas_call(
        body, name=name,
        out_shape=tuple(jax.ShapeDtypeStruct(shape(x), x.dtype) for x in arrays),
        in_specs=[pl.BlockSpec(memory_space=pl.ANY)] * n,
        out_specs=tuple(pl.BlockSpec(memory_space=pl.ANY) for _ in arrays),
        scratch_shapes=[pltpu.SemaphoreType.DMA((n, N_DEV - 1)),
                        pltpu.SemaphoreType.DMA((n, N_DEV - 1)),
                        pltpu.SemaphoreType.DMA((n,))],
    )(*arrays)


def _all_gather(x, name):
    return _exchange([x], name)[0]


_HBM = pl.BlockSpec(memory_space=pltpu.HBM)
_SEM = pl.BlockSpec(memory_space=pltpu.SEMAPHORE)
_ANY = pl.BlockSpec(memory_space=pl.ANY)
_EFFECT = pltpu.SideEffectType.DATAFLOW_SIDE_EFFECTING


def _split_copy(ins, lands, send_sems, recv_sems, a, k, me, scatter, incoming):
    peer = _flip(me, k)
    block = me if incoming else peer
    return pltpu.make_async_remote_copy(
        src_ref=ins[a].at[_index(block)] if scatter else ins[a],
        dst_ref=lands[a].at[_index(peer if incoming else me)],
        send_sem=send_sems.at[a * (N_DEV - 1) + k - 1], recv_sem=recv_sems.at[a * (N_DEV - 1) + k - 1],
        device_id=peer, device_id_type=MESH)


ALL_PEERS = tuple(range(1, N_DEV))
CHIP_PEERS = (1, 2, 4, 6)


def _exchange_start_groups(groups, name, scatter=False, after=None, peers=None):
    peers = peers or [ALL_PEERS] * len(groups)
    sizes = [len(g) for g in groups]
    first = [sum(sizes[:i]) for i in range(len(sizes))]
    n, ng = sum(sizes), len(sizes)
    after = jnp.zeros((SUBLANES, LANES), F32) if after is None else after

    def body(*refs):
        ins, lands = refs[:n], refs[n:2 * n]
        sems = refs[2 * n + 1:2 * n + 1 + 2 * ng]
        me = _my_pos()
        for g in range(ng):
            part = slice(first[g], first[g] + sizes[g])
            for k in peers[g]:
                for a in range(sizes[g]):
                    _split_copy(ins[part], lands[part], sems[2 * g], sems[2 * g + 1], a, k, me, scatter, False).start()
        refs[-1][...] = jnp.zeros((SUBLANES, LANES), F32)

    shape = lambda x: x.shape if scatter else (N_DEV,) + x.shape
    hbm = lambda x: pltpu.with_memory_space_constraint(x, pltpu.HBM)
    srcs = [hbm(x) for g in groups for x in g]
    zones = [hbm(lax.empty(shape(x), x.dtype)) for g in groups for x in g]
    sem_shapes = [pltpu.SemaphoreType.DMA((s * (N_DEV - 1),)) for s in sizes for _ in range(2)]
    out = pl.pallas_call(
        body, name=name,
        out_shape=(*sem_shapes, *[pltpu.HBM(x.shape, x.dtype) for x in srcs + zones],
                   jax.ShapeDtypeStruct((SUBLANES, LANES), F32)),
        in_specs=[_HBM] * (2 * n) + [_ANY],
        out_specs=(*[_SEM] * (2 * ng), *[_HBM] * (2 * n), pl.BlockSpec(memory_space=pltpu.VMEM)),
        input_output_aliases={i: 2 * ng + i for i in range(2 * n)},
        compiler_params=pltpu.CompilerParams(has_side_effects=_EFFECT),
    )(*srcs, *zones, after)
    bufs = out[2 * ng:-1]
    handles = [(out[2 * g], out[2 * g + 1], *bufs[first[g]:first[g] + sizes[g]],
                *bufs[n + first[g]:n + first[g] + sizes[g]]) for g in range(ng)]
    return handles, out[-1]


def _exchange_start(arrays, name, scatter=False, after=None):
    handles, token = _exchange_start_groups([arrays], name, scatter, after)
    return handles[0], token


def _exchange_wait(handle, name, scatter=False, after=None, peers=ALL_PEERS):
    send_sems, recv_sems = handle[0], handle[1]
    n = (len(handle) - 2) // 2
    after = jnp.zeros((SUBLANES, LANES), F32) if after is None else after

    def body(*refs):
        ins, lands = refs[:n], refs[n:2 * n]
        send, recv = refs[2 * n], refs[2 * n + 1]
        me = _my_pos()
        for k in peers:
            for a in range(n):
                _split_copy(ins, lands, send, recv, a, k, me, scatter, False).wait_send()
                _split_copy(ins, lands, send, recv, a, k, me, scatter, True).wait_recv()

    bufs = handle[2:]
    out = pl.pallas_call(
        body, name=name,
        out_shape=tuple(pltpu.HBM(x.shape, x.dtype) for x in bufs),
        in_specs=[_HBM] * (2 * n) + [_SEM, _SEM, _ANY],
        out_specs=tuple([_HBM] * (2 * n)),
        input_output_aliases={i: i for i in range(2 * n)},
        compiler_params=pltpu.CompilerParams(has_side_effects=_EFFECT),
    )(*bufs, send_sems, recv_sems, after)
    me = _index(_my_pos())
    landed = []
    for src, land in zip(out[:n], out[n:]):
        own = lax.dynamic_slice_in_dim(src, me, 1, axis=0) if scatter else src[None]
        landed.append(lax.dynamic_update_slice_in_dim(land, own, me, axis=0))
    return landed


def _sibling_forward(x, name):
    flips = [k for k in CHIP_PEERS if k != 1]

    def body(x_ref, o_ref, send_sems, recv_sems):
        me = _my_pos()
        sibling = _flip(me, 1)
        sends = []
        for i, k in enumerate(flips):
            block = o_ref.at[_index(_flip(me, k))]
            cp = pltpu.make_async_remote_copy(src_ref=block, dst_ref=block, send_sem=send_sems.at[i],
                                              recv_sem=recv_sems.at[i], device_id=sibling, device_id_type=MESH)
            cp.start()
            sends.append(cp)
        for i, k in enumerate(flips):
            block = o_ref.at[_index(_flip(sibling, k))]
            pltpu.make_async_remote_copy(src_ref=block, dst_ref=block, send_sem=send_sems.at[i],
                                         recv_sem=recv_sems.at[i], device_id=sibling, device_id_type=MESH).wait_recv()
        for cp in sends:
            cp.wait_send()

    return pl.pallas_call(
        body, name=name, out_shape=jax.ShapeDtypeStruct(x.shape, x.dtype),
        in_specs=[_ANY], out_specs=_ANY, input_output_aliases={0: 0},
        scratch_shapes=[pltpu.SemaphoreType.DMA((len(flips),)), pltpu.SemaphoreType.DMA((len(flips),))],
    )(x)


def _sum_blocks(x, name):
    n, rows, cols = x.shape
    whole = x.size * x.dtype.itemsize <= SUM_WHOLE_BYTES
    tr = rows if whole else _tile(rows, (512, 256, 128, 64, 32, 16, 8))

    def body(x_ref, o_ref):
        acc = x_ref[0].astype(F32)
        for d in range(1, n):
            acc = acc + x_ref[d].astype(F32)
        o_ref[...] = acc

    return pl.pallas_call(
        body, name=name,
        out_shape=jax.ShapeDtypeStruct((rows, cols), F32),
        grid=(rows // tr,),
        in_specs=[pl.BlockSpec((n, tr, cols), lambda i: (0, i, 0))],
        out_specs=pl.BlockSpec((tr, cols), lambda i: (i, 0)),
        compiler_params=_params("parallel"),
    )(x)


_DIMS = {"nn": (((1,), (0,)), ((), ())), "nt": (((1,), (1,)), ((), ())), "tn": (((0,), (0,)), ((), ()))}


def _mm(a, b, mode, name, out_dtype=F32, tm=None, tn=None, add=None, after=None, b_cols=None):
    if mode == "tn":
        kdim, m = a.shape
    else:
        m, kdim = a.shape
    n = b.shape[0] if mode == "nt" else b.shape[1]
    tm = tm or _tile(m, (512, 256, 128))
    tn = tn or _tile(n, (512, 256, 128))
    j0 = 0
    if b_cols is not None:
        j0, n = b_cols[0], b_cols[1] * tn
    dims = _DIMS[mode]

    def body(*refs):
        refs = refs if after is None else refs[1:]
        acc = lax.dot_general(refs[0][...].astype(BF16), refs[1][...].astype(BF16), dims,
                              preferred_element_type=F32)
        if add is not None:
            acc = acc + refs[2][...]
        refs[-1][...] = acc.astype(out_dtype)

    a_spec = (pl.BlockSpec((kdim, tm), lambda i, j: (0, i)) if mode == "tn"
              else pl.BlockSpec((tm, kdim), lambda i, j: (i, 0)))
    b_spec = (pl.BlockSpec((tn, kdim), lambda i, j: (j, 0)) if mode == "nt"
              else pl.BlockSpec((kdim, tn), lambda i, j: (0, j + j0)))
    o_spec = pl.BlockSpec((tm, tn), lambda i, j: (i, j))
    in_specs, args = [a_spec, b_spec], [a, b]
    if add is not None:
        in_specs.append(o_spec)
        args.append(add)
    if after is not None:
        in_specs.insert(0, _ANY)
        args.insert(0, after)
    return pl.pallas_call(
        body, name=name, out_shape=jax.ShapeDtypeStruct((m, n), out_dtype), grid=(m // tm, n // tn),
        in_specs=in_specs, out_specs=o_spec,
        compiler_params=_params("parallel", "parallel"),
    )(*args)


def _mm_tn_shared(parts, b, name, tm=256):
    kdim, n = b.shape
    blocks = [a.shape[1] // tm for a in parts]
    first = [sum(blocks[:i]) for i in range(len(parts))]

    def body(*refs):
        a_refs, b_ref, o_refs = refs[:len(parts)], refs[len(parts)], refs[len(parts) + 1:]
        i = pl.program_id(0)
        for a_ref, o_ref, f, nb in zip(a_refs, o_refs, first, blocks):
            @pl.when(jnp.logical_and(i >= f, i < f + nb))
            def _(a_ref=a_ref, o_ref=o_ref):
                o_ref[...] = lax.dot_general(a_ref[...], b_ref[...], _DIMS["tn"],
                                             preferred_element_type=F32).astype(BF16)

    def window(f, nb):
        return lambda i: jnp.clip(i - f, 0, nb - 1)
    a_specs = [pl.BlockSpec((kdim, tm), functools.partial(lambda i, w: (0, w(i)), w=window(f, nb)))
               for f, nb in zip(first, blocks)]
    o_specs = [pl.BlockSpec((tm, n), functools.partial(lambda i, w: (w(i), 0), w=window(f, nb)))
               for f, nb in zip(first, blocks)]
    return pl.pallas_call(
        body, name=name, grid=(sum(blocks),),
        out_shape=tuple(jax.ShapeDtypeStruct((a.shape[1], n), BF16) for a in parts),
        in_specs=a_specs + [pl.BlockSpec((kdim, n), lambda i: (0, 0), pipeline_mode=pl.Buffered(1))],
        out_specs=tuple(o_specs),
        compiler_params=_params("arbitrary"),
    )(*parts, b)


def _rowmap(name, fn, seq, rows, bats=(), vecs=(), row_outs=(), bat_outs=(), vec_outs=(), ts=None, mm=None, lhs=None,
            after=None, mm_sum=True):
    mms = [] if mm is None else (mm if isinstance(mm, list) else [mm])
    rows = [r if isinstance(r, tuple) else (r, r.shape[1], 0) for r in rows]
    tokens = rows[0][0].shape[0]
    nseq = tokens // seq
    ts = ts or _tile(seq, (512, 256, 128, 64, 32, 16, 8))
    nt = seq // ts
    n_r, n_b, n_v = len(rows), len(bats), len(vecs)
    n_ro, n_bo = len(row_outs), len(bat_outs)

    def accumulate(ref, val, first):
        @pl.when(first)
        def _():
            ref[...] = val.reshape(ref.shape)

        @pl.when(jnp.logical_not(first))
        def _():
            ref[...] += val.reshape(ref.shape)

    def body(*refs):
        n_in = n_r + n_b + n_v + len(mms) + (after is not None)
        ins, outs = refs[:n_in], refs[n_in:]
        b_vals = [r[0] for r in ins[n_r:n_r + n_b]]
        v_vals = [r[...] for r in ins[n_r + n_b:n_r + n_b + n_v]]
        r_vals = [r[...] for r in ins[:n_r]]
        if mms:
            lefts = r_vals[:len(mms)] if lhs is None else [lhs(r_vals, v_vals)] * len(mms)
            parts = [lax.dot_general(left.astype(BF16), b_ref[...].astype(BF16), _DIMS[mode],
                                     preferred_element_type=F32)
                     for left, b_ref, (_, mode) in zip(lefts, ins[n_r + n_b + n_v:], mms)]
            accs = [functools.reduce(lambda x, y: x + y, parts)] if mm_sum else parts
            r_vals = accs + r_vals[len(mms):] if lhs is None else accs + [lefts[0]] + r_vals
        ro, bo, vo = fn(r_vals, b_vals, v_vals)
        for ref, val in zip(outs[:n_ro], ro):
            ref[...] = val.astype(ref.dtype)
        b, i = pl.program_id(0), pl.program_id(1)
        for ref, val in zip(outs[n_ro:n_ro + n_bo], bo):
            accumulate(ref, val, i == 0)
        for ref, val in zip(outs[n_ro + n_bo:], vo):
            accumulate(ref, val, jnp.logical_and(i == 0, b == 0))

    in_specs = [pl.BlockSpec((ts, w), functools.partial(lambda b, i, cb: (b * nt + i, cb), cb=cb))
                for _, w, cb in rows]
    in_specs += [pl.BlockSpec((1, 1, v.shape[2]), lambda b, i: (b, 0, 0)) for v in bats]
    in_specs += [pl.BlockSpec(v.shape, lambda b, i: (0, 0)) for v in vecs]
    extra = [b_arr for b_arr, _ in mms]
    in_specs += [pl.BlockSpec(b_arr.shape, lambda b, i: (0, 0), pipeline_mode=pl.Buffered(1)) for b_arr in extra]
    if after is not None:
        in_specs.append(_ANY)
        extra.append(after)
    out_shape = [jax.ShapeDtypeStruct((tokens, f), dt) for f, dt in row_outs]
    out_specs = [pl.BlockSpec((ts, f), lambda b, i: (b * nt + i, 0)) for f, _ in row_outs]
    out_shape += [jax.ShapeDtypeStruct((nseq, 1, f), F32) for f in bat_outs]
    out_specs += [pl.BlockSpec((1, 1, f), lambda b, i: (b, 0, 0)) for f in bat_outs]
    out_shape += [jax.ShapeDtypeStruct((1, f), F32) for f in vec_outs]
    out_specs += [pl.BlockSpec((1, f), lambda b, i: (0, 0)) for f in vec_outs]
    return pl.pallas_call(
        body, name=name, out_shape=tuple(out_shape), grid=(nseq, nt),
        in_specs=in_specs, out_specs=tuple(out_specs),
        compiler_params=_params("arbitrary", "arbitrary"),
    )(*([r[0] for r in rows] + list(bats) + list(vecs) + extra))


def _colsum(v):
    return jnp.sum(v, axis=0, keepdims=True)


def _rstd(x, width=None):
    width = width or x.shape[-1]
    return lax.rsqrt(jnp.sum(x * x, axis=-1, keepdims=True) * (1.0 / width) + NORM_EPS)


def _norm_bwd(dy, x, r, g, width=None):
    width = width or x.shape[-1]
    xhat = x * r
    dxhat = dy * g
    dx = r * (dxhat - xhat * (jnp.sum(dxhat * xhat, axis=-1, keepdims=True) * (1.0 / width)))
    return dx, dy * xhat


def _sigmoid(x):
    return 0.5 * jnp.tanh(0.5 * x) + 0.5


def _norm_mod(xv, g, sh, sc):
    return xv * _rstd(xv) * g * (1.0 + sc) + sh


def _norm_mod_fwd(x, p, seq, name):
    def fn(rows, bats, vecs):
        return [_norm_mod(rows[0], vecs[0], bats[0], bats[1])], [], []
    return _rowmap(name, fn, seq, [x], [p["shift"], p["scale"]], [p["gamma"]], row_outs=[(D_MODEL, BF16)])[0]


def _norm_mod_bwd(dh, x, dres, p, seq, name, prev=None, after=None):
    products = dh if isinstance(dh, list) else None
    lefts = [l for l, _ in products] if products else [dh]
    def fn(rows, bats, vecs):
        dhv, xv, dr = rows[:3]
        sc, g = bats[0], vecs[0]
        r = _rstd(xv)
        dxn, dg = _norm_bwd(dhv * (1.0 + sc), xv, r, g)
        dx = dr + dxn
        ro, bo = [dx], [_colsum(dhv), _colsum(dhv * (xv * r * g))]
        if prev is not None:
            ro.append(bats[1] * dx)
            bo.append(_colsum(dx * rows[3].astype(F32)))
        return ro, bo, [_colsum(dg)]
    more = prev is not None
    return _rowmap(name, fn, seq, lefts + [x, dres] + ([prev[0]] if more else []),
                   [p["scale"]] + ([prev[1]] if more else []), [p["gamma"]],
                   row_outs=[(D_MODEL, F32)] + ([(D_MODEL, BF16)] if more else []),
                   bat_outs=[D_MODEL] * (3 if more else 2), vec_outs=[D_MODEL],
                   mm=[(r, "nn") for _, r in products] if products else None, after=after)


def _ffn_in_act(h, wt_in, name):
    tokens = h.shape[0]
    tm, tn = _tile(tokens, (2048, 1024, 512)), 256
    nj = D_FF // tn

    def body(h_ref, wg_ref, wu_ref, g_ref, u_ref, a_ref):
        hv = h_ref[...]
        g = lax.dot_general(hv, wg_ref[...], _DIMS["nt"], preferred_element_type=F32)
        u = lax.dot_general(hv, wu_ref[...], _DIMS["nt"], preferred_element_type=F32)
        g_ref[...] = g.astype(BF16)
        u_ref[...] = u.astype(BF16)
        a_ref[...] = (g * _sigmoid(g) * u).astype(BF16)

    o_spec = pl.BlockSpec((tm, tn), lambda i, j: (i, j))
    out = jax.ShapeDtypeStruct((tokens, D_FF), BF16)
    return pl.pallas_call(
        body, name=name, grid=(tokens // tm, nj), out_shape=(out, out, out),
        in_specs=[pl.BlockSpec((tm, D_MODEL), lambda i, j: (i, 0)),
                  pl.BlockSpec((tn, D_MODEL), lambda i, j: (j, 0)),
                  pl.BlockSpec((tn, D_MODEL), lambda i, j: (j + nj, 0))],
        out_specs=(o_spec, o_spec, o_spec),
        compiler_params=_params("parallel", "parallel"),
    )(h, wt_in, wt_in)


def _out_residual(a, w_out, res, gate, nxt, seq, name):
    def fn(rows, bats, vecs):
        acc, rv = rows
        x_new = rv + bats[0] * acc
        return [x_new, acc, _norm_mod(x_new, vecs[0], bats[1], bats[2])], [], []
    return _rowmap(name, fn, seq, [a, res], [gate, nxt["shift"], nxt["scale"]], [nxt["gamma"]],
                   row_outs=[(D_MODEL, F32), (D_MODEL, BF16), (D_MODEL, BF16)],
                   ts=_tile(seq, (512, 256, 128)), mm=(w_out, "nn"))


def _mix_out(z_g, ps, attn, res, wt_pool, wtmla_pad, w_out, gate, nxt, seq):
    branches = []

    def lhs(rows, vecs):
        zg = rows[0].astype(F32)
        bp = lax.dot_general(rows[1], vecs[1], _DIMS["nt"], preferred_element_type=F32)
        bm = lax.dot_general(rows[2], vecs[2], _DIMS["nt"], preferred_element_type=F32)
        branches[:] = [bp, bm]
        return (_sigmoid(zg[:, :D_MODEL]) * bp + _sigmoid(zg[:, D_MODEL:]) * bm).astype(BF16)

    def fn(rows, bats, vecs):
        acc, merged, rv = rows[0], rows[1], rows[-1]
        x_new = rv + bats[0] * acc
        return [x_new, acc, _norm_mod(x_new, vecs[0], bats[1], bats[2]), merged] + branches, [], []
    return _rowmap("mix_out", fn, seq, [z_g, ps, attn, res], [gate, nxt["shift"], nxt["scale"]],
                   [nxt["gamma"], wt_pool, wtmla_pad], row_outs=[(D_MODEL, F32)] + [(D_MODEL, BF16)] * 5,
                   ts=_tile(seq, (512, 256, 128)), mm=(w_out, "nn"), lhs=lhs)


def _out_loss(a, w_out, res, gate, target, seq, name):
    def fn(rows, bats, vecs):
        acc, rv, tv = rows
        err = rv + bats[0] * acc - tv
        dy = err * (1.0 / D_MODEL)
        return [dy, bats[0] * dy], [_colsum(dy * acc)], [_colsum(err * err)]
    return _rowmap(name, fn, seq, [a, res, target], [gate], row_outs=[(D_MODEL, F32), (D_MODEL, BF16)],
                   bat_outs=[D_MODEL], vec_outs=[D_MODEL], ts=_tile(seq, (512, 256, 128)), mm=(w_out, "nn"))


def _ffn_bwd_x(df, dres, saved, p, seq, tag, prev=None, early=None, mid=None):
    x, h, g, u, a, w_in, w_out = saved
    first = None if early is None else early(a, df)

    def act_bwd(rows, bats, vecs):
        dav, gv, uv = rows[0], rows[1].astype(F32), rows[2].astype(F32)
        sg = _sigmoid(gv)
        silu = gv * sg
        dg = dav * uv * (sg * (1.0 + gv * (1.0 - sg)))
        return [jnp.concatenate([dg, dav * silu], axis=1)], [], []
    dgu = _rowmap(f"{tag}_bwd_da", act_bwd, seq, [df, g, u], row_outs=[(2 * D_FF, BF16)],
                  ts=_tile(seq, (512, 256, 128)), mm=(w_out, "nt"), after=first)[0]
    operands = (a, df, dgu, h)
    after = None if mid is None else mid(operands)
    return _norm_mod_bwd([(dgu, w_in)], x, dres, p, seq, f"{tag}_bwd_norm", prev, after=after), operands


def _ffn_bwd_wout(a, df, tag):
    return _mm(a, df, "tn", f"{tag}_bwd_wout", out_dtype=BF16, tm=256, tn=D_MODEL)


def _ffn_bwd_win(operands, tag, after=None, half=None):
    _, _, dgu, h = operands
    if half is None:
        return _mm(dgu, h, "tn", f"{tag}_bwd_win", out_dtype=BF16, tm=512, tn=D_MODEL, after=after)
    return _mm(dgu, h, "tn", f"{tag}_bwd_win{half}", out_dtype=BF16, tm=512, tn=D_MODEL // 2, after=after,
               b_cols=(half, 1))


def _shift_rows(v, k, forward):
    n = v.shape[0]
    row = lax.broadcasted_iota(jnp.int32, v.shape, 0)
    if forward:
        return jnp.where(row >= k, pltpu.roll(v, k, 0), 0.0)
    return jnp.where(row < n - k, pltpu.roll(v, n - k, 0), 0.0)


def _window_sums(v, forward):
    out, s, k = [], v, 1
    for _ in range(POOL_GROUPS):
        s = s + _shift_rows(s, k, forward)
        out.append(s)
        k *= 2
    return out


def _by_group(vals, g):
    out = vals[-1]
    for idx in range(len(vals) - 2, -1, -1):
        out = jnp.where(g == idx, vals[idx], out)
    return out


def _inv_count(shape, g):
    t1 = lax.broadcasted_iota(jnp.int32, shape, 0) + 1
    window = _by_group([jnp.int32(2 ** (i + 1)) for i in range(POOL_GROUPS)], g)
    return 1.0 / jnp.minimum(t1, window).astype(F32)


def _pool_fwd(u, grp, scale, seq):
    tokens = u.shape[0]

    def body(u_ref, grp_ref, sc_ref, pooled_ref, pg_ref, ps_ref):
        g = pl.program_id(1)
        uv = u_ref[...]
        sums = _by_group(_window_sums(uv, True), g)
        pooled = (sums * _inv_count(uv.shape, g) - uv).astype(BF16)
        pg = jnp.dot(pooled, grp_ref[0].astype(BF16), preferred_element_type=F32)
        pooled_ref[...] = pooled
        pg_ref[...] = pg
        ps_ref[...] = (pg * sc_ref[...]).astype(BF16)

    blk = pl.BlockSpec((seq, POOL_GROUP_DIM), lambda b, g: (b, g))
    return pl.pallas_call(
        body, name="pool_fwd", grid=(tokens // seq, POOL_GROUPS),
        out_shape=(jax.ShapeDtypeStruct(u.shape, BF16), jax.ShapeDtypeStruct(u.shape, F32),
                   jax.ShapeDtypeStruct(u.shape, BF16)),
        in_specs=[blk, pl.BlockSpec((1, POOL_GROUP_DIM, POOL_GROUP_DIM), lambda b, g: (g, 0, 0)),
                  pl.BlockSpec((1, POOL_GROUP_DIM), lambda b, g: (0, g))],
        out_specs=(blk, blk, blk),
        compiler_params=_params("parallel", "parallel"),
    )(u, grp, scale)


def _pool_bwd(dps, pooled, pg, grp, scale, seq):
    tokens = dps.shape[0]

    def body(dps_ref, pooled_ref, pg_ref, grp_ref, sc_ref, du_ref, dgrp_ref, dsc_ref):
        g, b = pl.program_id(0), pl.program_id(1)
        dpsv = dps_ref[...]
        dpg = (dpsv * sc_ref[...]).astype(BF16)
        dsc = _colsum(dpsv * pg_ref[...])
        dgrp = lax.dot_general(pooled_ref[...], dpg, _DIMS["tn"], preferred_element_type=F32)

        @pl.when(b == 0)
        def _():
            dsc_ref[...] = dsc
            dgrp_ref[0] = dgrp

        @pl.when(b > 0)
        def _():
            dsc_ref[...] += dsc
            dgrp_ref[0] += dgrp

        dpool = lax.dot_general(dpg, grp_ref[0].astype(BF16), _DIMS["nt"], preferred_element_type=F32)
        sums = _by_group(_window_sums(dpool * _inv_count(dpool.shape, g), False), g)
        du_ref[...] = (sums - dpool).astype(BF16)

    blk = pl.BlockSpec((seq, POOL_GROUP_DIM), lambda g, b: (b, g))
    grp_spec = pl.BlockSpec((1, POOL_GROUP_DIM, POOL_GROUP_DIM), lambda g, b: (g, 0, 0))
    vec_spec = pl.BlockSpec((1, POOL_GROUP_DIM), lambda g, b: (0, g))
    return pl.pallas_call(
        body, name="pool_bwd", grid=(POOL_GROUPS, tokens // seq),
        out_shape=(jax.ShapeDtypeStruct(dps.shape, BF16), jax.ShapeDtypeStruct(grp.shape, F32),
                   jax.ShapeDtypeStruct(scale.shape, F32)),
        in_specs=[blk, blk, blk, grp_spec, vec_spec],
        out_specs=(blk, grp_spec, vec_spec),
        compiler_params=_params("arbitrary", "arbitrary"),
    )(dps, pooled, pg, grp, scale)


def _lane(shape):
    return lax.broadcasted_iota(jnp.int32, shape, len(shape) - 1)


def _rot(y):
    lane = _lane(y.shape)
    r = jnp.where(lane < QK_NOPE + QK_ROPE // 2,
                  -pltpu.roll(y, HEAD_SLAB - QK_ROPE // 2, 1), pltpu.roll(y, QK_ROPE // 2, 1))
    return jnp.where(jnp.logical_and(lane >= QK_NOPE, lane < QK_NOPE + QK_ROPE), r, 0.0)


def _part_rstd(x):
    sq = x * x
    nope = _lane(x.shape) < QK_NOPE
    s_nope = jnp.sum(jnp.where(nope, sq, 0.0), axis=-1, keepdims=True)
    s_rope = jnp.sum(sq, axis=-1, keepdims=True) - s_nope
    return jnp.where(nope, lax.rsqrt(s_nope * (1.0 / QK_NOPE) + NORM_EPS),
                     lax.rsqrt(s_rope * (1.0 / QK_ROPE) + NORM_EPS))


def _part_norm_bwd(dy, x, r, g):
    nope = _lane(x.shape) < QK_NOPE
    xhat = x * r
    dxhat = dy * g
    prod = dxhat * xhat
    m_nope = jnp.sum(jnp.where(nope, prod, 0.0), axis=-1, keepdims=True)
    m_rope = jnp.sum(prod, axis=-1, keepdims=True) - m_nope
    mean = jnp.where(nope, m_nope * (1.0 / QK_NOPE), m_rope * (1.0 / QK_ROPE))
    return r * (dxhat - xhat * mean), dy * xhat


def _mixer_in(h, wt_a, wt_p, wt_g, g_q, g_kv, seq):
    def fn(rows, bats, vecs):
        z_a, z_p, z_g = rows[:3]
        q, kv = z_a[:, :Q_LORA], z_a[:, Q_LORA:Q_LORA + KV_LORA]
        return [z_a, z_p, z_g, q * _rstd(q) * vecs[0], kv * _rstd(kv) * vecs[1]], [], []
    return _rowmap("mix_in", fn, seq, [h], vecs=[g_q, g_kv], lhs=lambda rows, vecs: rows[0],
                   mm=[(wt_a, "nt"), (wt_p, "nt"), (wt_g, "nt")], mm_sum=False,
                   row_outs=[(wt_a.shape[0], F32), (wt_p.shape[0], F32), (wt_g.shape[0], BF16),
                             (Q_LORA, BF16), (KV_LORA, BF16)])


def _latent_norm_bwd(dqp, wtq_pad, dkv, wt_kv, dkr, z_a, g_q, g_kv, seq):
    def fn(rows, bats, vecs):
        dq, dkv, dkrv, z = rows
        q, kv = z[:, :Q_LORA], z[:, Q_LORA:Q_LORA + KV_LORA]
        dxq, dgq = _norm_bwd(dq, q, _rstd(q), vecs[0])
        dxkv, dgkv = _norm_bwd(dkv, kv, _rstd(kv), vecs[1])
        return [jnp.concatenate([dxq, dxkv, dkrv], axis=1)], [], [_colsum(dgq), _colsum(dgkv)]
    return _rowmap("latent_norm_bwd", fn, seq, [dqp, dkv, dkr, z_a], vecs=[g_q, g_kv],
                   row_outs=[(Q_LORA + KV_LORA + HEAD_SLAB, BF16)], vec_outs=[Q_LORA, KV_LORA],
                   mm=[(wtq_pad, "nn"), (wt_kv, "nn")], mm_sum=False)


def _qk_prep_fwd(qn, wtq_pad, kvn, wt_kv, z_a, pos, g_q, g_kn, g_kr, inv_freq, seq):
    def fn(rows, bats, vecs):
        qv, kvv, kr, p = rows
        gq, gkn, gkr, invf = vecs
        ang = p * invf
        cos, sin = jnp.cos(ang), jnp.sin(ang)
        nope = _lane(kr.shape) < QK_NOPE
        krn = kr * _rstd(kr, QK_ROPE) * gkr
        krr = krn * cos + _rot(krn) * sin
        qs, ks, vs = [], [], []
        for h in range(N_HEADS):
            xq = qv[:, h * HEAD_SLAB:(h + 1) * HEAD_SLAB]
            y = xq * _part_rstd(xq) * gq
            qs.append(y * cos + _rot(y) * sin)
            xk = kvv[:, h * HEAD_SLAB:(h + 1) * HEAD_SLAB]
            kn = jnp.where(nope, xk, 0.0)
            ks.append(jnp.where(nope, kn * _rstd(kn, QK_NOPE) * gkn, krr))
            vs.append(jnp.where(nope, 0.0, xk))
        return [jnp.concatenate(v, axis=1) for v in (qs, ks, vs)] + [qv, kvv], [], []
    width = N_HEADS * HEAD_SLAB
    return _rowmap("qk_prep", fn, seq, [qn, kvn, (z_a, HEAD_SLAB, 5), pos], vecs=[g_q, g_kn, g_kr, inv_freq],
                   row_outs=[(width, BF16)] * 3 + [(width, F32)] * 2, mm=[(wtq_pad, "nt"), (wt_kv, "nt")],
                   mm_sum=False)


def _qk_prep_bwd(dqc, dkc, dvp, qp, kv, z_a, pos, g_q, g_kn, g_kr, inv_freq, seq):
    def fn(rows, bats, vecs):
        dq, dk, dv, qv, kvv, kr, p = rows
        gq, gkn, gkr, invf = vecs
        ang = p * invf
        cos, sin = jnp.cos(ang), jnp.sin(ang)
        nope = _lane(kr.shape) < QK_NOPE
        dqs, dkvs = [], []
        dgq = jnp.zeros((1, HEAD_SLAB), F32)
        dgkn = jnp.zeros((1, HEAD_SLAB), F32)
        dkrr = jnp.zeros(kr.shape, F32)
        for h in range(N_HEADS):
            sl = slice(h * HEAD_SLAB, (h + 1) * HEAD_SLAB)
            dyr = dq[:, sl]
            dy = dyr * cos - _rot(dyr * sin)
            xq = qv[:, sl]
            dx, dg = _part_norm_bwd(dy, xq, _part_rstd(xq), gq)
            dqs.append(dx)
            dgq = dgq + _colsum(dg)
            dkh = dk[:, sl]
            dkrr = dkrr + jnp.where(nope, 0.0, dkh)
            kn = jnp.where(nope, kvv[:, sl], 0.0)
            dxk, dgk = _norm_bwd(jnp.where(nope, dkh, 0.0), kn, _rstd(kn, QK_NOPE), gkn, QK_NOPE)
            dgkn = dgkn + _colsum(dgk)
            dkvs.append(jnp.where(nope, dxk, dv[:, sl]))
        dkrn = dkrr * cos - _rot(dkrr * sin)
        dkr, dgkr = _norm_bwd(dkrn, kr, _rstd(kr, QK_ROPE), gkr, QK_ROPE)
        return ([jnp.concatenate(dqs, axis=1), jnp.concatenate(dkvs, axis=1), dkr], [],
                [dgq, dgkn, _colsum(dgkr)])
    width = N_HEADS * HEAD_SLAB
    return _rowmap("qk_prep_bwd", fn, seq, [dqc, dkc, dvp, qp, kv, (z_a, HEAD_SLAB, 5), pos],
                   vecs=[g_q, g_kn, g_kr, inv_freq],
                   row_outs=[(width, BF16), (width, BF16), (HEAD_SLAB, F32)],
                   vec_outs=[HEAD_SLAB] * 3, ts=_tile(seq, (512, 256, 128, 64, 32, 16, 8)))


def _scores(q, k_ref, keys, tq):
    s = lax.dot_general(q, k_ref[0:keys, :], _DIMS["nt"], preferred_element_type=F32) * ATTN_SCALE
    row = lax.broadcasted_iota(jnp.int32, (tq, tq), 0)
    col = lax.broadcasted_iota(jnp.int32, (tq, tq), 1)
    diag = jnp.where(col <= row, s[:, keys - tq:], -1e30)
    return diag if keys == tq else jnp.concatenate([s[:, :keys - tq], diag], axis=1)


def _attn_fwd(qc, kc, vp, seq):
    tokens = qc.shape[0]
    tq = _tile(seq, (256, 128))
    nq = seq // tq

    def body(q_ref, k_ref, v_ref, o_ref, lse_ref):
        for i in range(nq):
            rows, keys = slice(i * tq, (i + 1) * tq), (i + 1) * tq
            s = _scores(q_ref[rows, :], k_ref, keys, tq)
            m = jnp.max(s, axis=-1, keepdims=True)
            p = jnp.exp(s - m)
            l = jnp.sum(p, axis=-1, keepdims=True)
            acc = jnp.dot(p.astype(BF16), v_ref[0:keys, :], preferred_element_type=F32)
            o_ref[rows, :] = (acc / l).astype(BF16)
            lse_ref[rows, :] = jnp.broadcast_to(m + jnp.log(l), (tq, HEAD_SLAB))

    spec = pl.BlockSpec((seq, HEAD_SLAB), lambda b, h: (b, h))
    return pl.pallas_call(
        body, name="attn_fwd", grid=(tokens // seq, N_HEADS),
        out_shape=(jax.ShapeDtypeStruct(qc.shape, BF16), jax.ShapeDtypeStruct(qc.shape, F32)),
        in_specs=[spec] * 3, out_specs=(spec, spec),
        compiler_params=_params("parallel", "parallel"),
    )(qc, kc, vp)


def _attn_bwd(qc, kc, vp, o, lse, do, seq):
    tokens = qc.shape[0]
    tq = _tile(seq, (256, 128))
    nq = seq // tq

    def body(q_ref, k_ref, v_ref, o_ref, lse_ref, do_ref, dq_ref, dk_ref, dv_ref):
        dk_ref[...] = jnp.zeros(dk_ref.shape, F32)
        dv_ref[...] = jnp.zeros(dv_ref.shape, F32)
        for i in range(nq):
            rows, keys = slice(i * tq, (i + 1) * tq), (i + 1) * tq
            q, dov = q_ref[rows, :], do_ref[rows, :]
            delta = jnp.sum(dov.astype(F32) * o_ref[rows, :].astype(F32), axis=-1, keepdims=True)
            s = _scores(q, k_ref, keys, tq)
            p = jnp.exp(s - jnp.tile(lse_ref[rows, :], (1, keys // HEAD_SLAB)))
            dp = lax.dot_general(dov, v_ref[0:keys, :], _DIMS["nt"], preferred_element_type=F32)
            ds = (p * (dp - delta) * ATTN_SCALE).astype(BF16)
            dq_ref[rows, :] = jnp.dot(ds, k_ref[0:keys, :], preferred_element_type=F32)
            dk_ref[0:keys, :] += lax.dot_general(ds, q, _DIMS["tn"], preferred_element_type=F32)
            dv_ref[0:keys, :] += lax.dot_general(p.astype(BF16), dov, _DIMS["tn"], preferred_element_type=F32)

    spec = pl.BlockSpec((seq, HEAD_SLAB), lambda b, h: (b, h))
    out = jax.ShapeDtypeStruct(qc.shape, F32)
    return pl.pallas_call(
        body, name="attn_bwd", grid=(tokens // seq, N_HEADS),
        out_shape=(out, out, out), in_specs=[spec] * 6, out_specs=(spec, spec, spec),
        compiler_params=_params("parallel", "parallel"),
    )(qc, kc, vp, o, lse, do)


def _adamw(w, g, m, v, name):
    rows, cols = w.shape
    whole = rows * cols * 4 <= ADAMW_WHOLE_BYTES
    tr = rows if whole else _tile(rows, (256, 128, 64, 32, 16, 8))
    c1 = 1.0 - ADAM_B1 ** ADAM_STEP
    c2 = 1.0 - ADAM_B2 ** ADAM_STEP

    def body(w_ref, g_ref, m_ref, v_ref, d_ref, nm_ref, nv_ref):
        gv = g_ref[...]
        nm = ADAM_B1 * m_ref[...] + (1.0 - ADAM_B1) * gv
        nv = ADAM_B2 * v_ref[...] + (1.0 - ADAM_B2) * (gv * gv)
        d_ref[...] = -ADAM_LR * ((nm / c1) / (jnp.sqrt(nv / c2) + ADAM_EPS) + ADAM_WD * w_ref[...])
        nm_ref[...] = nm
        nv_ref[...] = nv

    spec = pl.BlockSpec((tr, cols), lambda i: (i, 0))
    out = jax.ShapeDtypeStruct(w.shape, F32)
    return pl.pallas_call(
        body, name=name, grid=(rows // tr,), out_shape=(out, out, out),
        in_specs=[spec] * 4, out_specs=(spec, spec, spec),
        compiler_params=_params("parallel"),
    )(w, g, m, v)


def _adamw_landed(w, landed, m, v, name):
    rows, cols = w.shape
    tr = _tile(rows, (176, 128, 96, 64, 32, 16, 8))
    c1 = 1.0 - ADAM_B1 ** ADAM_STEP
    c2 = 1.0 - ADAM_B2 ** ADAM_STEP
    n_parts = len(landed)

    def body(*refs):
        w_ref, m_ref, v_ref = refs[:3]
        g_ref, d_ref, nm_ref, nv_ref = refs[3 + n_parts:]
        parts = []
        for x_ref in refs[3:3 + n_parts]:
            acc = x_ref[0].astype(F32)
            for d in range(1, N_DEV):
                acc = acc + x_ref[d].astype(F32)
            parts.append(acc)
        gv = parts[0] if n_parts == 1 else jnp.concatenate(parts, axis=1)
        nm = ADAM_B1 * m_ref[...] + (1.0 - ADAM_B1) * gv
        nv = ADAM_B2 * v_ref[...] + (1.0 - ADAM_B2) * (gv * gv)
        g_ref[...] = gv
        d_ref[...] = -ADAM_LR * ((nm / c1) / (jnp.sqrt(nv / c2) + ADAM_EPS) + ADAM_WD * w_ref[...])
        nm_ref[...] = nm
        nv_ref[...] = nv

    spec = pl.BlockSpec((tr, cols), lambda i: (i, 0))
    out = jax.ShapeDtypeStruct(w.shape, F32)
    return pl.pallas_call(
        body, name=name, grid=(rows // tr,), out_shape=(out, out, out, out),
        in_specs=[spec] * 3 + [pl.BlockSpec((N_DEV, tr, x.shape[2]), lambda i: (0, i, 0)) for x in landed],
        out_specs=(spec, spec, spec, spec),
        compiler_params=_params("parallel"),
    )(w, m, v, *landed)


def _mod_cols(c_all, w_ada, b_cols):
    def body(c_ref, w_ref, b_ref, act_ref, mod_ref):
        cv = c_ref[...]
        act = cv * _sigmoid(cv)
        act_ref[...] = act
        mod_ref[...] = jnp.dot(act.astype(BF16), w_ref[...].astype(BF16),
                               preferred_element_type=F32) + b_ref[...]

    n = w_ada.shape[1]
    return pl.pallas_call(
        body, name="mod_cols",
        out_shape=(jax.ShapeDtypeStruct(c_all.shape, F32), jax.ShapeDtypeStruct((c_all.shape[0], n), F32)),
        compiler_params=pltpu.CompilerParams(vmem_limit_bytes=VMEM_LIMIT),
    )(c_all, w_ada, b_cols)


def _ada_grads(c_act, dmod_all, dmod_cols):
    def body(c_ref, d_ref, dc_ref, gw_ref, gb_ref):
        gw_ref[...] = lax.dot_general(c_ref[...].astype(BF16), dc_ref[...].astype(BF16), _DIMS["tn"],
                                      preferred_element_type=F32)
        gb_ref[...] = _colsum(d_ref[...])

    return pl.pallas_call(
        body, name="ada_grads",
        out_shape=(jax.ShapeDtypeStruct((c_act.shape[1], dmod_cols.shape[1]), F32),
                   jax.ShapeDtypeStruct((1, dmod_all.shape[1]), F32)),
        compiler_params=pltpu.CompilerParams(vmem_limit_bytes=VMEM_LIMIT),
    )(c_act, dmod_all, dmod_cols)


def _flat_rows(a):
    flat = a.reshape(-1)
    pad = (-flat.shape[0]) % (LANES * SUBLANES)
    if pad:
        flat = jnp.pad(flat, (0, pad))
    return flat.reshape(-1, LANES)


def _gather_start(w, groups, tag, after=None, peers=None):
    shards = [[(w[n] if n in ROW_SHARDED else w[n].T).astype(BF16) for n in names] for names in groups]
    return _exchange_start_groups(shards, f"gather_{tag}_start", after=after, peers=peers)


def _gather_wait(handle, names, tag, after, peers=ALL_PEERS):
    landed = _exchange_wait(handle, f"gather_{tag}_wait", after=after, peers=peers)
    if peers == CHIP_PEERS:
        landed = [_sibling_forward(x, f"gather_{tag}_forward{i}") for i, x in enumerate(landed)]
    return {n: g.reshape(-1, g.shape[2]) for n, g in zip(names, landed)}


def _scatter_start(grads, names, tag, after=None):
    blocks = [grads[n].reshape(N_DEV, -1, grads[n].shape[1]) for n in names]
    return _exchange_start(blocks, f"scatter_{tag}_start", scatter=True, after=after)


def _scatter_wait(handle, names, tag, after):
    landed = _exchange_wait(handle, f"scatter_{tag}_wait", scatter=True, after=after)
    return {n: [x] for n, x in zip(names, landed)}


def _pack_small(vals):
    return jnp.concatenate([_flat_rows(v.astype(F32)) for v in vals], axis=0)


def _unpack_small(packed, like):
    out, row = [], 0
    for v in like:
        rows = _flat_rows(v).shape[0]
        out.append(packed[row:row + rows].reshape(-1)[:v.size].reshape(v.shape))
        row += rows
    return out


def _lanes128(*parts):
    out = jnp.zeros((HEAD_SLAB,), F32)
    for off, v in parts:
        out = lax.dynamic_update_slice(out, v.reshape(-1).astype(F32), (off,))
    return out.reshape(1, HEAD_SLAB)


def _step(x, c, positions, w, m, v, loss_target):
    nseq, seq, _ = x.shape
    tokens = nseq * seq
    me = _index(_my_pos())
    strip = lambda d: {n: (a[0] if a.ndim > 2 else a) for n, a in d.items()}
    shapes = {n: a.shape for n, a in w.items()}
    w, m, v = strip(w), strip(m), strip(v)

    c_all = _all_gather(c.reshape(-1, LANES), "gather_c").reshape(N_DEV * nseq, D_MODEL)
    n_ada = w["w_ada"].shape[1]
    b_cols = lax.dynamic_slice(w["b_ada"], (0, me * n_ada), (1, n_ada))
    c_act, mod_cols = _mod_cols(c_all, w["w_ada"], b_cols)
    mod_all = _all_gather(mod_cols, "gather_mod")
    mod = lax.dynamic_slice(mod_all, (0, me * nseq, 0), (N_DEV, nseq, n_ada))
    mod = mod.transpose(1, 0, 2).reshape(nseq, 3, 3, 1, D_MODEL)

    (h_f1i, h_f1o, h_mix_in, h_mix, h_f2), tok = _gather_start(
        w, (("w_ffn1_in",), ("w_ffn1_out",), MIXER[:1], MIXER[1:], ("w_ffn2_in", "w_ffn2_out")), "weights",
        after=mod_all, peers=[CHIP_PEERS, ALL_PEERS, CHIP_PEERS, ALL_PEERS, ALL_PEERS])
    started = tok[0:1, 0:1]

    g_q = _lanes128((0, w["q_norm_nope"]), (QK_NOPE, w["q_norm_rope"]))
    g_kn = _lanes128((0, w["k_norm_nope"]))
    g_kr = _lanes128((QK_NOPE, w["k_norm_rope"]))
    freq = ROPE_THETA ** (-jnp.arange(0, QK_ROPE, 2, dtype=F32) / QK_ROPE)
    inv_freq = _lanes128((QK_NOPE, jnp.concatenate([freq, freq])))
    pos = positions.reshape(tokens, 1).astype(F32)

    def sub(k, gamma, coef):
        return dict(gamma=w[gamma], shift=mod[:, k, 0] + started, scale=mod[:, k, 1], gate=coef * mod[:, k, 2])
    p1, pm, p2 = sub(0, "norm_ffn1", 0.5), sub(1, "norm_mix", 1.0), sub(2, "norm_ffn2", 0.5)

    x0 = x.reshape(tokens, D_MODEL)
    h1 = _norm_mod_fwd(x0, p1, seq, "ffn1_norm")
    wt_f1i = _gather_wait(h_f1i, ("w_ffn1_in",), "ffn1_in", h1, peers=CHIP_PEERS)["w_ffn1_in"]
    g1, u1, a1 = _ffn_in_act(h1, wt_f1i, "ffn1_in")
    w_f1o = _gather_wait(h_f1o, ("w_ffn1_out",), "ffn1_out", a1)["w_ffn1_out"]
    x1, f1, h2 = _out_residual(a1, w_f1o, x0, p1["gate"], pm, seq, "ffn1_out")
    saved1 = (x0, h1, g1, u1, a1, wt_f1i, w_f1o)

    wt_in = _gather_wait(h_mix_in, MIXER[:1], "mix_in", h2, peers=CHIP_PEERS)["w_in"]
    zero_rows = lambda rows: jnp.zeros((rows, D_MODEL), BF16)
    wt_p = wt_in[:512]
    wt_a = jnp.concatenate([wt_in[512:1152], zero_rows(QK_NOPE), wt_in[1152:1184], zero_rows(32)], axis=0)
    wt_g = wt_in[1184:]
    z_a, z_p, z_g, qn, kvn = _mixer_in(h2, wt_a, wt_p, wt_g, w["q_a_norm"], w["kv_a_norm"], seq)

    full = _gather_wait(h_mix, MIXER[1:], "mix", z_g)
    wtq_pad = jnp.pad(full["w_q_up"].reshape(N_HEADS, 96, Q_LORA), ((0, 0), (0, 32), (0, 0))).reshape(-1, Q_LORA)
    wtmla_pad = jnp.pad(full["w_mla_proj"].reshape(D_MODEL, N_HEADS, 64), ((0, 0), (0, 0), (64, 0))).reshape(D_MODEL, -1)
    wt_pool, wt_kv, w_mix_out = full["w_pool_proj"], full["w_kv_up"], full["w_out"]
    pooled, pg, ps = _pool_fwd(z_p, w["pool_grp"], w["pool_scale"], seq)
    qc, kc, vp, qp, kv = _qk_prep_fwd(qn, wtq_pad, kvn, wt_kv, z_a, pos, g_q, g_kn, g_kr, inv_freq, seq)
    attn, lse = _attn_fwd(qc, kc, vp, seq)
    x2, o_mix, h3, merged, br_pool, br_mla = _mix_out(z_g, ps, attn, x1, wt_pool, wtmla_pad, w_mix_out, pm["gate"],
                                                      p2, seq)

    ffn2_w = _gather_wait(h_f2, ("w_ffn2_in", "w_ffn2_out"), "ffn2", h3)
    g2, u2, a2 = _ffn_in_act(h3, ffn2_w["w_ffn2_in"], "ffn2_in")
    dy, df2, dgate2, sq_err = _out_loss(a2, ffn2_w["w_ffn2_out"], x2, p2["gate"],
                                        loss_target.reshape(tokens, D_MODEL), seq, "ffn2_out")
    saved2 = (x2, h3, g2, u2, a2, ffn2_w["w_ffn2_in"], ffn2_w["w_ffn2_out"])

    grads = {}
    (dx2, do_mix, dsh2, dsc2, dgate_m, dg_ffn2), ops2 = _ffn_bwd_x(df2, dy, saved2, p2, seq, "ffn2", (o_mix, pm["gate"]))
    grads["w_ffn2_out"], grads["w_ffn2_in"] = _ffn_bwd_wout(ops2[0], ops2[1], "ffn2"), _ffn_bwd_win(ops2, "ffn2")
    s_f2, tok = _scatter_start(grads, ("w_ffn2_in", "w_ffn2_out"), "ffn2")

    grads["w_out"] = _mm(merged, do_mix, "tn", "mix_bwd_wout", out_dtype=BF16, tm=512, tn=D_MODEL)

    def merge_bwd(rows, bats, vecs):
        dmv, zg, bp, bm = (r.astype(F32) for r in rows)
        s_p, s_m = _sigmoid(zg[:, :D_MODEL]), _sigmoid(zg[:, D_MODEL:])
        dzg = jnp.concatenate([dmv * bp * s_p * (1.0 - s_p), dmv * bm * s_m * (1.0 - s_m)], axis=1)
        dbp, dbm = (dmv * s_p).astype(BF16), (dmv * s_m).astype(BF16)
        dps_v = jnp.dot(dbp, vecs[0], preferred_element_type=F32)
        dattn = jnp.dot(dbm, vecs[1], preferred_element_type=F32)
        return [dbp, dbm, dzg, dps_v, dattn], [], []
    dbr_pool, dbr_mla, dz_g, dps, d_attn = _rowmap(
        "mix_bwd_dmerged", merge_bwd, seq, [do_mix, z_g, br_pool, br_mla], vecs=[wt_pool, wtmla_pad],
        row_outs=[(D_MODEL, BF16), (D_MODEL, BF16), (2 * D_MODEL, BF16), (POOL_WIDTH, F32), (D_MODEL, BF16)],
        mm=(w_mix_out, "nt"))

    grads["w_pool_proj"] = _mm(dbr_pool, ps, "tn", "pool_bwd_wproj", out_dtype=BF16, tm=512, tn=POOL_WIDTH)
    dz_p, dgrp, dpool_scale = _pool_bwd(dps, pooled, pg, w["pool_grp"], w["pool_scale"] + tok[0:1, 0:1], seq)

    dwtmla_pad = _mm(dbr_mla, attn, "tn", "mla_bwd_wproj", out_dtype=BF16, tm=512, tn=D_MODEL)
    grads["w_mla_proj"] = dwtmla_pad.reshape(D_MODEL, N_HEADS, HEAD_SLAB)[:, :, 64:].reshape(D_MODEL, -1)
    dqc, dkc, dvp = _attn_bwd(qc, kc, vp, attn, lse, d_attn, seq)
    dqp, dkv, dkr, dg_q, dg_kn, dg_kr = _qk_prep_bwd(dqc, dkc, dvp, qp, kv, z_a, pos, g_q, g_kn, g_kr, inv_freq, seq)
    dwtq_pad = _mm(dqp, qn, "tn", "q_up_bwd_w", out_dtype=BF16, tm=512, tn=Q_LORA)
    grads["w_q_up"] = dwtq_pad.reshape(N_HEADS, HEAD_SLAB, Q_LORA)[:, :96].reshape(-1, Q_LORA)
    grads["w_kv_up"] = _mm(dkv, kvn, "tn", "kv_up_bwd_w", out_dtype=BF16, tm=512, tn=KV_LORA)
    dz_a, dg_qa, dg_kva = _latent_norm_bwd(dqp, wtq_pad, dkv, wt_kv, dkr, z_a, w["q_a_norm"], w["kv_a_norm"], seq)

    dwt_a, dwt_p, dwt_g = _mm_tn_shared([dz_a, dz_p, dz_g], h2, "mix_in_bwd_w")
    grads["w_in"] = jnp.concatenate([dwt_p, dwt_a[:640], dwt_a[704:736], dwt_g], axis=0)

    small_early = [dg_ffn2.reshape(w["norm_ffn2"].shape), dgrp, dpool_scale, dg_qa, dg_kva, dg_q[:, :QK_NOPE],
                   dg_q[:, QK_NOPE:QK_NOPE + QK_ROPE], dg_kn[:, :QK_NOPE], dg_kr[:, QK_NOPE:QK_NOPE + QK_ROPE]]
    s_small, tok = _exchange_start([_pack_small(small_early)], "gather_small_start")
    s_mix, tok = _scatter_start(grads, MIXER, "mix", after=tok)
    dh2 = [(dz_a, wt_a), (dz_p, wt_p), (dz_g, wt_g)]
    pm_tied = dict(pm, scale=pm["scale"] + tok[0:1, 0:1])
    dx1, df1, dsh_m, dsc_m, dgate1, dg_mix = _norm_mod_bwd(dh2, x1, dx2, pm_tied, seq, "mix_bwd_norm", (f1, p1["gate"]))

    handles = {}

    def ffn1_early(a, df):
        grads["w_ffn1_out"] = _ffn_bwd_wout(a, df, "ffn1")
        handles["f1o"], token = _scatter_start(grads, ("w_ffn1_out",), "ffn1_out")
        return token

    def ffn1_mid(operands):
        first = _ffn_bwd_win(operands, "ffn1", half=0)
        handles["f1i0"], token = _exchange_start([first.reshape(N_DEV, -1, first.shape[1])],
                                                 "scatter_ffn1_in0_start", scatter=True)
        return token

    (dx0, dsh1, dsc1, dg_ffn1), ops1 = _ffn_bwd_x(df1, dx1, saved1, p1, seq, "ffn1", early=ffn1_early,
                                                     mid=ffn1_mid)
    s_f1o = handles["f1o"]

    dmod = jnp.stack([jnp.stack([dsh1, dsc1, 0.5 * dgate1], axis=1),
                      jnp.stack([dsh_m, dsc_m, dgate_m], axis=1),
                      jnp.stack([dsh2, dsc2, 0.5 * dgate2], axis=1)], axis=1)
    n_dmod = nseq * 9 * D_MODEL // LANES
    tail = _all_gather(jnp.concatenate([dmod.reshape(-1, LANES), _flat_rows(dg_ffn1), _flat_rows(dg_mix),
                                        _flat_rows(sq_err)], axis=0), "gather_dmod")
    dmod_all = tail[:, :n_dmod].reshape(N_DEV * nseq, 9 * D_MODEL)

    second = _ffn_bwd_win(ops1, "ffn1", after=tail, half=1)
    s_second, tok = _exchange_start([second.reshape(N_DEV, -1, second.shape[1])], "scatter_ffn1_in1_start",
                                    scatter=True, after=tail)
    s_f1i = (handles["f1i0"], s_second)

    dmod_cols = lax.dynamic_slice(dmod_all, (0, me * n_ada), (N_DEV * nseq, n_ada)) + tok[0:1, 0:1]
    g_w_ada, g_b_ada = _ada_grads(c_act, dmod_all, dmod_cols)
    tail_sum = _sum_blocks(tail[:, n_dmod:], "sum_tail")
    g_norm_ffn1 = tail_sum[:SUBLANES].reshape(1, D_MODEL)
    g_norm_mix = tail_sum[SUBLANES:2 * SUBLANES].reshape(1, D_MODEL)
    loss = 0.5 * jnp.sum(tail_sum[2 * SUBLANES:]) * (1.0 / D_MODEL)
    small_all = _exchange_wait(s_small, "gather_small_wait", after=g_b_ada)[0]
    small_sum = _sum_blocks(small_all, "sum_small")
    small = dict(zip(SMALL[2:], _unpack_small(small_sum, [w[n] for n in SMALL[2:]])))
    grad_w = dict(small, w_ada=g_w_ada, b_ada=g_b_ada, norm_ffn1=g_norm_ffn1, norm_mix=g_norm_mix)

    delta, new_m, new_v = {}, {}, {}

    def update(names, landed=None):
        for n in names:
            if landed is None:
                delta[n], new_m[n], new_v[n] = _adamw(w[n], grad_w[n], m[n], v[n], f"adamw_{n}")
            elif n in KEPT_TRANSPOSED:
                res = _adamw_landed(w[n].T, landed[n], m[n].T, v[n].T, f"adamw_{n}")
                grad_w[n], delta[n], new_m[n], new_v[n] = (r.T for r in res)
            elif n in ROW_SHARDED:
                grad_w[n], delta[n], new_m[n], new_v[n] = _adamw_landed(w[n], landed[n], m[n], v[n], f"adamw_{n}")
            else:
                grad_w[n] = _sum_blocks(landed[n][0], f"sum_{n}").T
                delta[n], new_m[n], new_v[n] = _adamw(w[n], grad_w[n], m[n], v[n], f"adamw_{n}")

    update(("w_ada",))
    rep = ("b_ada",) + SMALL
    d_s, m_s, v_s = _adamw(_pack_small([w[n] for n in rep]), _pack_small([grad_w[n] for n in rep]),
                           _pack_small([m[n] for n in rep]), _pack_small([v[n] for n in rep]), "adamw_small")
    like = [w[n] for n in rep]
    for dst, packed in ((delta, d_s), (new_m, m_s), (new_v, v_s)):
        dst.update(zip(rep, _unpack_small(packed, like)))
    update(("w_ffn2_in", "w_ffn2_out"), _scatter_wait(s_f2, ("w_ffn2_in", "w_ffn2_out"), "ffn2", after=d_s))
    update(MIXER, _scatter_wait(s_mix, MIXER, "mix", after=delta["w_ffn2_out"]))
    update(("w_ffn1_out",), _scatter_wait(s_f1o, ("w_ffn1_out",), "ffn1_out", after=delta["w_out"]))
    halves = [_exchange_wait(h, f"scatter_ffn1_in{i}_wait", scatter=True, after=delta["w_ffn1_out"])[0]
              for i, h in enumerate(s_f1i)]
    update(("w_ffn1_in",), {"w_ffn1_in": halves})

    lead = lambda d: [d[n].reshape(shapes[n]) for n in WEIGHTS]
    return (loss, dx0.reshape(x.shape), *lead(grad_w), *lead(delta), *lead(new_m), *lead(new_v))


def kernel(x, c, positions, w_ada, b_ada, norm_ffn1, w_ffn1_in, w_ffn1_out, norm_mix, w_in, pool_grp, pool_scale, w_pool_proj, q_a_norm, w_q_up, kv_a_norm, w_kv_up, q_norm_nope, q_norm_rope, k_norm_nope, k_norm_rope, w_mla_proj, w_out, norm_ffn2, w_ffn2_in, w_ffn2_out, loss_target, m_w_ada, m_b_ada, m_norm_ffn1, m_w_ffn1_in, m_w_ffn1_out, m_norm_mix, m_w_in, m_pool_grp, m_pool_scale, m_w_pool_proj, m_q_a_norm, m_w_q_up, m_kv_a_norm, m_w_kv_up, m_q_norm_nope, m_q_norm_rope, m_k_norm_nope, m_k_norm_rope, m_w_mla_proj, m_w_out, m_norm_ffn2, m_w_ffn2_in, m_w_ffn2_out, v_w_ada, v_b_ada, v_norm_ffn1, v_w_ffn1_in, v_w_ffn1_out, v_norm_mix, v_w_in, v_pool_grp, v_pool_scale, v_w_pool_proj, v_q_a_norm, v_w_q_up, v_kv_a_norm, v_w_kv_up, v_q_norm_nope, v_q_norm_rope, v_k_norm_nope, v_k_norm_rope, v_w_mla_proj, v_w_out, v_norm_ffn2, v_w_ffn2_in, v_w_ffn2_out):
    w = dict(w_ada=w_ada, b_ada=b_ada, norm_ffn1=norm_ffn1, w_ffn1_in=w_ffn1_in, w_ffn1_out=w_ffn1_out, norm_mix=norm_mix, w_in=w_in, pool_grp=pool_grp, pool_scale=pool_scale, w_pool_proj=w_pool_proj, q_a_norm=q_a_norm, w_q_up=w_q_up, kv_a_norm=kv_a_norm, w_kv_up=w_kv_up, q_norm_nope=q_norm_nope, q_norm_rope=q_norm_rope, k_norm_nope=k_norm_nope, k_norm_rope=k_norm_rope, w_mla_proj=w_mla_proj, w_out=w_out, norm_ffn2=norm_ffn2, w_ffn2_in=w_ffn2_in, w_ffn2_out=w_ffn2_out)
    m = dict(w_ada=m_w_ada, b_ada=m_b_ada, norm_ffn1=m_norm_ffn1, w_ffn1_in=m_w_ffn1_in, w_ffn1_out=m_w_ffn1_out, norm_mix=m_norm_mix, w_in=m_w_in, pool_grp=m_pool_grp, pool_scale=m_pool_scale, w_pool_proj=m_w_pool_proj, q_a_norm=m_q_a_norm, w_q_up=m_w_q_up, kv_a_norm=m_kv_a_norm, w_kv_up=m_w_kv_up, q_norm_nope=m_q_norm_nope, q_norm_rope=m_q_norm_rope, k_norm_nope=m_k_norm_nope, k_norm_rope=m_k_norm_rope, w_mla_proj=m_w_mla_proj, w_out=m_w_out, norm_ffn2=m_norm_ffn2, w_ffn2_in=m_w_ffn2_in, w_ffn2_out=m_w_ffn2_out)
    v = dict(w_ada=v_w_ada, b_ada=v_b_ada, norm_ffn1=v_norm_ffn1, w_ffn1_in=v_w_ffn1_in, w_ffn1_out=v_w_ffn1_out, norm_mix=v_norm_mix, w_in=v_w_in, pool_grp=v_pool_grp, pool_scale=v_pool_scale, w_pool_proj=v_w_pool_proj, q_a_norm=v_q_a_norm, w_q_up=v_w_q_up, kv_a_norm=v_kv_a_norm, w_kv_up=v_w_kv_up, q_norm_nope=v_q_norm_nope, q_norm_rope=v_q_norm_rope, k_norm_nope=v_k_norm_nope, k_norm_rope=v_k_norm_rope, w_mla_proj=v_w_mla_proj, w_out=v_w_out, norm_ffn2=v_norm_ffn2, w_ffn2_in=v_w_ffn2_in, w_ffn2_out=v_w_ffn2_out)
    return _step(x, c, positions, w, m, v, loss_target)
```

```python
import functools
import math

import jax
import jax.numpy as jnp
from jax import lax
from jax.experimental import pallas as pl
from jax.experimental.pallas import tpu as pltpu

F32 = jnp.float32
BF16 = jnp.bfloat16
MESH = pl.DeviceIdType.MESH
AXES = ("x", "y", "c")
N_DEV = 8

D_MODEL = 1024
D_FF = 2816
N_HEADS = 8
HEAD_SLAB = 128
QK_NOPE = 64
QK_ROPE = 32
POOL_WIDTH = 512
POOL_GROUPS = 4
POOL_GROUP_DIM = 128
Q_LORA = 384
KV_LORA = 256
ROPE_THETA = 10000.0
ATTN_SCALE = 1.0 / math.sqrt(QK_NOPE + QK_ROPE)
NORM_EPS = 1e-6
ADAM_LR, ADAM_B1, ADAM_B2, ADAM_EPS, ADAM_WD, ADAM_STEP = 0.001, 0.9, 0.999, 1e-08, 0.01, 10

LANES = 128
SUBLANES = 8
VMEM_LIMIT = 52 * 1024 * 1024
ADAMW_WHOLE_BYTES = 3 << 19
SUM_WHOLE_BYTES = 4 << 20

BIG = ("w_ffn1_in", "w_ffn1_out", "w_in", "w_pool_proj", "w_q_up", "w_kv_up",
       "w_mla_proj", "w_out", "w_ffn2_in", "w_ffn2_out")
ROW_SHARDED = ("w_ffn1_out", "w_out", "w_ffn2_out")
MIXER = ("w_in", "w_pool_proj", "w_q_up", "w_kv_up", "w_mla_proj", "w_out")
KEPT_TRANSPOSED = ("w_ffn1_in", "w_ffn2_in", "w_in", "w_q_up")
SMALL = ("norm_ffn1", "norm_mix", "norm_ffn2", "pool_grp", "pool_scale", "q_a_norm",
         "kv_a_norm", "q_norm_nope", "q_norm_rope", "k_norm_nope", "k_norm_rope")
WEIGHTS = ("w_ada", "b_ada", "norm_ffn1", "w_ffn1_in", "w_ffn1_out", "norm_mix", "w_in",
           "pool_grp", "pool_scale", "w_pool_proj", "q_a_norm", "w_q_up", "kv_a_norm",
           "w_kv_up", "q_norm_nope", "q_norm_rope", "k_norm_nope", "k_norm_rope",
           "w_mla_proj", "w_out", "norm_ffn2", "w_ffn2_in", "w_ffn2_out")


def _params(*sem):
    return pltpu.CompilerParams(dimension_semantics=sem, vmem_limit_bytes=VMEM_LIMIT)


def _tile(n, cands):
    for c in cands:
        if n % c == 0:
            return c
    return n


def _my_pos():
    return lax.axis_index("x"), lax.axis_index("y"), lax.axis_index("c")


def _flip(pos, k):
    x, y, c = pos
    fx, fy, fc = (k >> 2) & 1, (k >> 1) & 1, k & 1
    return ((1 - x) if fx else x, (1 - y) if fy else y, (1 - c) if fc else c)


def _index(pos):
    x, y, c = pos
    return 4 * x + 2 * y + c


def _exchange(arrays, name, scatter=False):
    n = len(arrays)

    def body(*refs):
        ins, outs = refs[:n], refs[n:2 * n]
        send_sems, recv_sems, local_sems = refs[2 * n:]
        me = _my_pos()
        mine, sends = [], []
        for a in range(n):
            own = ins[a].at[_index(me)] if scatter else ins[a]
            cp = pltpu.make_async_copy(own, outs[a].at[_index(me)], local_sems.at[a])
            cp.start()
            mine.append(cp)
        for k in range(1, N_DEV):
            peer = _flip(me, k)
            for a in range(n):
                cp = pltpu.make_async_remote_copy(
                    src_ref=ins[a].at[_index(peer)] if scatter else ins[a],
                    dst_ref=outs[a].at[_index(me)],
                    send_sem=send_sems.at[a, k - 1], recv_sem=recv_sems.at[a, k - 1],
                    device_id=peer, device_id_type=MESH)
                cp.start()
                sends.append(cp)
        for k in range(1, N_DEV):
            peer = _flip(me, k)
            for a in range(n):
                pltpu.make_async_remote_copy(
                    src_ref=ins[a].at[_index(me)] if scatter else ins[a],
                    dst_ref=outs[a].at[_index(peer)],
                    send_sem=send_sems.at[a, k - 1], recv_sem=recv_sems.at[a, k - 1],
                    device_id=peer, device_id_type=MESH).wait_recv()
        for cp in sends:
            cp.wait_send()
        for cp in mine:
            cp.wait()

    shape = lambda x: x.shape if scatter else (N_DEV,) + x.shape
    return pl.pallas_call(
        body, name=name,
        out_shape=tuple(jax.ShapeDtypeStruct(shape(x), x.dtype) for x in arrays),
        in_specs=[pl.BlockSpec(memory_space=pl.ANY)] * n,
        out_specs=tuple(pl.BlockSpec(memory_space=pl.ANY) for _ in arrays),
        scratch_shapes=[pltpu.SemaphoreType.DMA((n, N_DEV - 1)),
                        pltpu.SemaphoreType.DMA((n, N_DEV - 1)),
                        pltpu.SemaphoreType.DMA((n,))],
    )(*arrays)


def _all_gather(x, name):
    return _exchange([x], name)[0]


_HBM = pl.BlockSpec(memory_space=pltpu.HBM)
_SEM = pl.BlockSpec(memory_space=pltpu.SEMAPHORE)
_ANY = pl.BlockSpec(memory_space=pl.ANY)
_EFFECT = pltpu.SideEffectType.DATAFLOW_SIDE_EFFECTING


def _split_copy(ins, lands, send_sems, recv_sems, a, k, me, scatter, incoming):
    peer = _flip(me, k)
    block = me if incoming else peer
    return pltpu.make_async_remote_copy(
        src_ref=ins[a].at[_index(block)] if scatter else ins[a],
        dst_ref=lands[a].at[_index(peer if incoming else me)],
        send_sem=send_sems.at[a * (N_DEV - 1) + k - 1], recv_sem=recv_sems.at[a * (N_DEV - 1) + k - 1],
        device_id=peer, device_id_type=MESH)


ALL_PEERS = tuple(range(1, N_DEV))
CHIP_PEERS = (1, 2, 4, 6)


def _exchange_start_groups(groups, name, scatter=False, after=None, peers=None):
    peers = peers or [ALL_PEERS] * len(groups)
    sizes = [len(g) for g in groups]
    first = [sum(sizes[:i]) for i in range(len(sizes))]
    n, ng = sum(sizes), len(sizes)
    after = jnp.zeros((SUBLANES, LANES), F32) if after is None else after

    def body(*refs):
        ins, lands = refs[:n], refs[n:2 * n]
        sems = refs[2 * n + 1:2 * n + 1 + 2 * ng]
        me = _my_pos()
        for g in range(ng):
            part = slice(first[g], first[g] + sizes[g])
            for k in peers[g]:
                for a in range(sizes[g]):
                    _split_copy(ins[part], lands[part], sems[2 * g], sems[2 * g + 1], a, k, me, scatter, False).start()
        refs[-1][...] = jnp.zeros((SUBLANES, LANES), F32)

    shape = lambda x: x.shape if scatter else (N_DEV,) + x.shape
    hbm = lambda x: pltpu.with_memory_space_constraint(x, pltpu.HBM)
    srcs = [hbm(x) for g in groups for x in g]
    zones = [hbm(lax.empty(shape(x), x.dtype)) for g in groups for x in g]
    sem_shapes = [pltpu.SemaphoreType.DMA((s * (N_DEV - 1),)) for s in sizes for _ in range(2)]
    out = pl.pallas_call(
        body, name=name,
        out_shape=(*sem_shapes, *[pltpu.HBM(x.shape, x.dtype) for x in srcs + zones],
                   jax.ShapeDtypeStruct((SUBLANES, LANES), F32)),
        in_specs=[_HBM] * (2 * n) + [_ANY],
        out_specs=(*[_SEM] * (2 * ng), *[_HBM] * (2 * n), pl.BlockSpec(memory_space=pltpu.VMEM)),
        input_output_aliases={i: 2 * ng + i for i in range(2 * n)},
        compiler_params=pltpu.CompilerParams(has_side_effects=_EFFECT),
    )(*srcs, *zones, after)
    bufs = out[2 * ng:-1]
    handles = [(out[2 * g], out[2 * g + 1], *bufs[first[g]:first[g] + sizes[g]],
                *bufs[n + first[g]:n + first[g] + sizes[g]]) for g in range(ng)]
    return handles, out[-1]


def _exchange_start(arrays, name, scatter=False, after=None):
    handles, token = _exchange_start_groups([arrays], name, scatter, after)
    return handles[0], token


def _exchange_wait(handle, name, scatter=False, after=None, peers=ALL_PEERS):
    send_sems, recv_sems = handle[0], handle[1]
    n = (len(handle) - 2) // 2
    after = jnp.zeros((SUBLANES, LANES), F32) if after is None else after

    def body(*refs):
        ins, lands = refs[:n], refs[n:2 * n]
        send, recv = refs[2 * n], refs[2 * n + 1]
        me = _my_pos()
        for k in peers:
            for a in range(n):
                _split_copy(ins, lands, send, recv, a, k, me, scatter, False).wait_send()
                _split_copy(ins, lands, send, recv, a, k, me, scatter, True).wait_recv()

    bufs = handle[2:]
    out = pl.pallas_call(
        body, name=name,
        out_shape=tuple(pltpu.HBM(x.shape, x.dtype) for x in bufs),
        in_specs=[_HBM] * (2 * n) + [_SEM, _SEM, _ANY],
        out_specs=tuple([_HBM] * (2 * n)),
        input_output_aliases={i: i for i in range(2 * n)},
        compiler_params=pltpu.CompilerParams(has_side_effects=_EFFECT),
    )(*bufs, send_sems, recv_sems, after)
    me = _index(_my_pos())
    landed = []
    for src, land in zip(out[:n], out[n:]):
        own = lax.dynamic_slice_in_dim(src, me, 1, axis=0) if scatter else src[None]
        landed.append(lax.dynamic_update_slice_in_dim(land, own, me, axis=0))
    return landed


def _sibling_forward(x, name):
    flips = [k for k in CHIP_PEERS if k != 1]

    def body(x_ref, o_ref, send_sems, recv_sems):
        me = _my_pos()
        sibling = _flip(me, 1)
        sends = []
        for i, k in enumerate(flips):
            block = o_ref.at[_index(_flip(me, k))]
            cp = pltpu.make_async_remote_copy(src_ref=block, dst_ref=block, send_sem=send_sems.at[i],
                                              recv_sem=recv_sems.at[i], device_id=sibling, device_id_type=MESH)
            cp.start()
            sends.append(cp)
        for i, k in enumerate(flips):
            block = o_ref.at[_index(_flip(sibling, k))]
            pltpu.make_async_remote_copy(src_ref=block, dst_ref=block, send_sem=send_sems.at[i],
                                         recv_sem=recv_sems.at[i], device_id=sibling, device_id_type=MESH).wait_recv()
        for cp in sends:
            cp.wait_send()

    return pl.pallas_call(
        body, name=name, out_shape=jax.ShapeDtypeStruct(x.shape, x.dtype),
        in_specs=[_ANY], out_specs=_ANY, input_output_aliases={0: 0},
        scratch_shapes=[pltpu.SemaphoreType.DMA((len(flips),)), pltpu.SemaphoreType.DMA((len(flips),))],
    )(x)


def _sum_blocks(x, name):
    n, rows, cols = x.shape
    whole = x.size * x.dtype.itemsize <= SUM_WHOLE_BYTES
    tr = rows if whole else _tile(rows, (512, 256, 128, 64, 32, 16, 8))

    def body(x_ref, o_ref):
        acc = x_ref[0].astype(F32)
        for d in range(1, n):
            acc = acc + x_ref[d].astype(F32)
        o_ref[...] = acc

    return pl.pallas_call(
        body, name=name,
        out_shape=jax.ShapeDtypeStruct((rows, cols), F32),
        grid=(rows // tr,),
        in_specs=[pl.BlockSpec((n, tr, cols), lambda i: (0, i, 0))],
        out_specs=pl.BlockSpec((tr, cols), lambda i: (i, 0)),
        compiler_params=_params("parallel"),
    )(x)


_DIMS = {"nn": (((1,), (0,)), ((), ())), "nt": (((1,), (1,)), ((), ())), "tn": (((0,), (0,)), ((), ()))}


def _mm(a, b, mode, name, out_dtype=F32, tm=None, tn=None, add=None, after=None, b_cols=None):
    if mode == "tn":
        kdim, m = a.shape
    else:
        m, kdim = a.shape
    n = b.shape[0] if mode == "nt" else b.shape[1]
    tm = tm or _tile(m, (512, 256, 128))
    tn = tn or _tile(n, (512, 256, 128))
    j0 = 0
    if b_cols is not None:
        j0, n = b_cols[0], b_cols[1] * tn
    dims = _DIMS[mode]

    def body(*refs):
        refs = refs if after is None else refs[1:]
        acc = lax.dot_general(refs[0][...].astype(BF16), refs[1][...].astype(BF16), dims,
                              preferred_element_type=F32)
        if add is not None:
            acc = acc + refs[2][...]
        refs[-1][...] = acc.astype(out_dtype)

    a_spec = (pl.BlockSpec((kdim, tm), lambda i, j: (0, i)) if mode == "tn"
              else pl.BlockSpec((tm, kdim), lambda i, j: (i, 0)))
    b_spec = (pl.BlockSpec((tn, kdim), lambda i, j: (j, 0)) if mode == "nt"
              else pl.BlockSpec((kdim, tn), lambda i, j: (0, j + j0)))
    o_spec = pl.BlockSpec((tm, tn), lambda i, j: (i, j))
    in_specs, args = [a_spec, b_spec], [a, b]
    if add is not None:
        in_specs.append(o_spec)
        args.append(add)
    if after is not None:
        in_specs.insert(0, _ANY)
        args.insert(0, after)
    return pl.pallas_call(
        body, name=name, out_shape=jax.ShapeDtypeStruct((m, n), out_dtype), grid=(m // tm, n // tn),
        in_specs=in_specs, out_specs=o_spec,
        compiler_params=_params("parallel", "parallel"),
    )(*args)


def _mm_tn_shared(parts, b, name, tm=256):
    kdim, n = b.shape
    blocks = [a.shape[1] // tm for a in parts]
    first = [sum(blocks[:i]) for i in range(len(parts))]

    def body(*refs):
        a_refs, b_ref, o_refs = refs[:len(parts)], refs[len(parts)], refs[len(parts) + 1:]
        i = pl.program_id(0)
        for a_ref, o_ref, f, nb in zip(a_refs, o_refs, first, blocks):
            @pl.when(jnp.logical_and(i >= f, i < f + nb))
            def _(a_ref=a_ref, o_ref=o_ref):
                o_ref[...] = lax.dot_general(a_ref[...], b_ref[...], _DIMS["tn"],
                                             preferred_element_type=F32).astype(BF16)

    def window(f, nb):
        return lambda i: jnp.clip(i - f, 0, nb - 1)
    a_specs = [pl.BlockSpec((kdim, tm), functools.partial(lambda i, w: (0, w(i)), w=window(f, nb)))
               for f, nb in zip(first, blocks)]
    o_specs = [pl.BlockSpec((tm, n), functools.partial(lambda i, w: (w(i), 0), w=window(f, nb)))
               for f, nb in zip(first, blocks)]
    return pl.pallas_call(
        body, name=name, grid=(sum(blocks),),
        out_shape=tuple(jax.ShapeDtypeStruct((a.shape[1], n), BF16) for a in parts),
        in_specs=a_specs + [pl.BlockSpec((kdim, n), lambda i: (0, 0), pipeline_mode=pl.Buffered(1))],
        out_specs=tuple(o_specs),
        compiler_params=_params("arbitrary"),
    )(*parts, b)


def _rowmap(name, fn, seq, rows, bats=(), vecs=(), row_outs=(), bat_outs=(), vec_outs=(), ts=None, mm=None, lhs=None,
            after=None, mm_sum=True):
    mms = [] if mm is None else (mm if isinstance(mm, list) else [mm])
    rows = [r if isinstance(r, tuple) else (r, r.shape[1], 0) for r in rows]
    tokens = rows[0][0].shape[0]
    nseq = tokens // seq
    ts = ts or _tile(seq, (512, 256, 128, 64, 32, 16, 8))
    nt = seq // ts
    n_r, n_b, n_v = len(rows), len(bats), len(vecs)
    n_ro, n_bo = len(row_outs), len(bat_outs)

    def accumulate(ref, val, first):
        @pl.when(first)
        def _():
            ref[...] = val.reshape(ref.shape)

        @pl.when(jnp.logical_not(first))
        def _():
            ref[...] += val.reshape(ref.shape)

    def body(*refs):
        n_in = n_r + n_b + n_v + len(mms) + (after is not None)
        ins, outs = refs[:n_in], refs[n_in:]
        b_vals = [r[0] for r in ins[n_r:n_r + n_b]]
        v_vals = [r[...] for r in ins[n_r + n_b:n_r + n_b + n_v]]
        r_vals = [r[...] for r in ins[:n_r]]
        if mms:
            lefts = r_vals[:len(mms)] if lhs is None else [lhs(r_vals, v_vals)] * len(mms)
            parts = [lax.dot_general(left.astype(BF16), b_ref[...].astype(BF16), _DIMS[mode],
                                     preferred_element_type=F32)
                     for left, b_ref, (_, mode) in zip(lefts, ins[n_r + n_b + n_v:], mms)]
            accs = [functools.reduce(lambda x, y: x + y, parts)] if mm_sum else parts
            r_vals = accs + r_vals[len(mms):] if lhs is None else accs + [lefts[0]] + r_vals
        ro, bo, vo = fn(r_vals, b_vals, v_vals)
        for ref, val in zip(outs[:n_ro], ro):
            ref[...] = val.astype(ref.dtype)
        b, i = pl.program_id(0), pl.program_id(1)
        for ref, val in zip(outs[n_ro:n_ro + n_bo], bo):
            accumulate(ref, val, i == 0)
        for ref, val in zip(outs[n_ro + n_bo:], vo):
            accumulate(ref, val, jnp.logical_and(i == 0, b == 0))

    in_specs = [pl.BlockSpec((ts, w), functools.partial(lambda b, i, cb: (b * nt + i, cb), cb=cb))
                for _, w, cb in rows]
    in_specs += [pl.BlockSpec((1, 1, v.shape[2]), lambda b, i: (b, 0, 0)) for v in bats]
    in_specs += [pl.BlockSpec(v.shape, lambda b, i: (0, 0)) for v in vecs]
    extra = [b_arr for b_arr, _ in mms]
    in_specs += [pl.BlockSpec(b_arr.shape, lambda b, i: (0, 0), pipeline_mode=pl.Buffered(1)) for b_arr in extra]
    if after is not None:
        in_specs.append(_ANY)
        extra.append(after)
    out_shape = [jax.ShapeDtypeStruct((tokens, f), dt) for f, dt in row_outs]
    out_specs = [pl.BlockSpec((ts, f), lambda b, i: (b * nt + i, 0)) for f, _ in row_outs]
    out_shape += [jax.ShapeDtypeStruct((nseq, 1, f), F32) for f in bat_outs]
    out_specs += [pl.BlockSpec((1, 1, f), lambda b, i: (b, 0, 0)) for f in bat_outs]
    out_shape += [jax.ShapeDtypeStruct((1, f), F32) for f in vec_outs]
    out_specs += [pl.BlockSpec((1, f), lambda b, i: (0, 0)) for f in vec_outs]
    return pl.pallas_call(
        body, name=name, out_shape=tuple(out_shape), grid=(nseq, nt),
        in_specs=in_specs, out_specs=tuple(out_specs),
        compiler_params=_params("arbitrary", "arbitrary"),
    )(*([r[0] for r in rows] + list(bats) + list(vecs) + extra))


def _colsum(v):
    return jnp.sum(v, axis=0, keepdims=True)


def _rstd(x, width=None):
    width = width or x.shape[-1]
    return lax.rsqrt(jnp.sum(x * x, axis=-1, keepdims=True) * (1.0 / width) + NORM_EPS)


def _norm_bwd(dy, x, r, g, width=None):
    width = width or x.shape[-1]
    xhat = x * r
    dxhat = dy * g
    dx = r * (dxhat - xhat * (jnp.sum(dxhat * xhat, axis=-1, keepdims=True) * (1.0 / width)))
    return dx, dy * xhat


def _sigmoid(x):
    return 0.5 * jnp.tanh(0.5 * x) + 0.5


def _norm_mod(xv, g, sh, sc):
    return xv * _rstd(xv) * g * (1.0 + sc) + sh


def _norm_mod_fwd(x, p, seq, name):
    def fn(rows, bats, vecs):
        return [_norm_mod(rows[0], vecs[0], bats[0], bats[1])], [], []
    return _rowmap(name, fn, seq, [x], [p["shift"], p["scale"]], [p["gamma"]], row_outs=[(D_MODEL, BF16)])[0]


def _norm_mod_bwd(dh, x, dres, p, seq, name, prev=None, after=None):
    products = dh if isinstance(dh, list) else None
    lefts = [l for l, _ in products] if products else [dh]
    def fn(rows, bats, vecs):
        dhv, xv, dr = rows[:3]
        sc, g = bats[0], vecs[0]
        r = _rstd(xv)
        dxn, dg = _norm_bwd(dhv * (1.0 + sc), xv, r, g)
        dx = dr + dxn
        ro, bo = [dx], [_colsum(dhv), _colsum(dhv * (xv * r * g))]
        if prev is not None:
            ro.append(bats[1] * dx)
            bo.append(_colsum(dx * rows[3].astype(F32)))
        return ro, bo, [_colsum(dg)]
    more = prev is not None
    return _rowmap(name, fn, seq, lefts + [x, dres] + ([prev[0]] if more else []),
                   [p["scale"]] + ([prev[1]] if more else []), [p["gamma"]],
                   row_outs=[(D_MODEL, F32)] + ([(D_MODEL, BF16)] if more else []),
                   bat_outs=[D_MODEL] * (3 if more else 2), vec_outs=[D_MODEL],
                   mm=[(r, "nn") for _, r in products] if products else None, after=after)


def _ffn_in_act(h, wt_in, name):
    tokens = h.shape[0]
    tm, tn = _tile(tokens, (4096, 2048, 1024, 512)), 256
    nj = D_FF // tn

    def body(h_ref, wg_ref, wu_ref, g_ref, u_ref, a_ref):
        hv = h_ref[...]
        g = lax.dot_general(hv, wg_ref[...], _DIMS["nt"], preferred_element_type=F32)
        u = lax.dot_general(hv, wu_ref[...], _DIMS["nt"], preferred_element_type=F32)
        g_ref[...] = g.astype(BF16)
        u_ref[...] = u.astype(BF16)
        a_ref[...] = (g * _sigmoid(g) * u).astype(BF16)

    o_spec = pl.BlockSpec((tm, tn), lambda i, j: (i, j))
    out = jax.ShapeDtypeStruct((tokens, D_FF), BF16)
    return pl.pallas_call(
        body, name=name, grid=(tokens // tm, nj), out_shape=(out, out, out),
        in_specs=[pl.BlockSpec((tm, D_MODEL), lambda i, j: (i, 0)),
                  pl.BlockSpec((tn, D_MODEL), lambda i, j: (j, 0)),
                  pl.BlockSpec((tn, D_MODEL), lambda i, j: (j + nj, 0))],
        out_specs=(o_spec, o_spec, o_spec),
        compiler_params=_params("parallel", "parallel"),
    )(h, wt_in, wt_in)


def _out_residual(a, w_out, res, gate, nxt, seq, name):
    def fn(rows, bats, vecs):
        acc, rv = rows
        x_new = rv + bats[0] * acc
        return [x_new, acc, _norm_mod(x_new, vecs[0], bats[1], bats[2])], [], []
    return _rowmap(name, fn, seq, [a, res], [gate, nxt["shift"], nxt["scale"]], [nxt["gamma"]],
                   row_outs=[(D_MODEL, F32), (D_MODEL, BF16), (D_MODEL, BF16)],
                   ts=_tile(seq, (512, 256, 128)), mm=(w_out, "nn"))


def _mix_out(z_g, ps, attn, res, wt_pool, wtmla_pad, w_out, gate, nxt, seq):
    branches = []

    def lhs(rows, vecs):
        zg = rows[0].astype(F32)
        bp = lax.dot_general(rows[1], vecs[1], _DIMS["nt"], preferred_element_type=F32)
        bm = lax.dot_general(rows[2], vecs[2], _DIMS["nt"], preferred_element_type=F32)
        branches[:] = [bp, bm]
        return (_sigmoid(zg[:, :D_MODEL]) * bp + _sigmoid(zg[:, D_MODEL:]) * bm).astype(BF16)

    def fn(rows, bats, vecs):
        acc, merged, rv = rows[0], rows[1], rows[-1]
        x_new = rv + bats[0] * acc
        return [x_new, acc, _norm_mod(x_new, vecs[0], bats[1], bats[2]), merged] + branches, [], []
    return _rowmap("mix_out", fn, seq, [z_g, ps, attn, res], [gate, nxt["shift"], nxt["scale"]],
                   [nxt["gamma"], wt_pool, wtmla_pad], row_outs=[(D_MODEL, F32)] + [(D_MODEL, BF16)] * 5,
                   ts=_tile(seq, (512, 256, 128)), mm=(w_out, "nn"), lhs=lhs)


def _out_loss(a, w_out, res, gate, target, seq, name):
    def fn(rows, bats, vecs):
        acc, rv, tv = rows
        err = rv + bats[0] * acc - tv
        dy = err * (1.0 / D_MODEL)
        return [dy, bats[0] * dy], [_colsum(dy * acc)], [_colsum(err * err)]
    return _rowmap(name, fn, seq, [a, res, target], [gate], row_outs=[(D_MODEL, F32), (D_MODEL, BF16)],
                   bat_outs=[D_MODEL], vec_outs=[D_MODEL], ts=_tile(seq, (512, 256, 128)), mm=(w_out, "nn"))


def _ffn_bwd_x(df, dres, saved, p, seq, tag, prev=None, early=None, mid=None):
    x, h, g, u, a, w_in, w_out = saved
    first = None if early is None else early(a, df)

    def act_bwd(rows, bats, vecs):
        dav, gv, uv = rows[0], rows[1].astype(F32), rows[2].astype(F32)
        sg = _sigmoid(gv)
        silu = gv * sg
        dg = dav * uv * (sg * (1.0 + gv * (1.0 - sg)))
        return [jnp.concatenate([dg, dav * silu], axis=1)], [], []
    dgu = _rowmap(f"{tag}_bwd_da", act_bwd, seq, [df, g, u], row_outs=[(2 * D_FF, BF16)],
                  ts=_tile(seq, (512, 256, 128)), mm=(w_out, "nt"), after=first)[0]
    operands = (a, df, dgu, h)
    after = None if mid is None else mid(operands)
    return _norm_mod_bwd([(dgu, w_in)], x, dres, p, seq, f"{tag}_bwd_norm", prev, after=after), operands


def _ffn_bwd_wout(a, df, tag):
    return _mm(a, df, "tn", f"{tag}_bwd_wout", out_dtype=BF16, tm=256, tn=D_MODEL)


def _ffn_bwd_win(operands, tag, after=None, half=None):
    _, _, dgu, h = operands
    if half is None:
        return _mm(dgu, h, "tn", f"{tag}_bwd_win", out_dtype=BF16, tm=512, tn=D_MODEL, after=after)
    return _mm(dgu, h, "tn", f"{tag}_bwd_win{half}", out_dtype=BF16, tm=512, tn=D_MODEL // 2, after=after,
               b_cols=(half, 1))


def _shift_rows(v, k, forward):
    n = v.shape[0]
    row = lax.broadcasted_iota(jnp.int32, v.shape, 0)
    if forward:
        return jnp.where(row >= k, pltpu.roll(v, k, 0), 0.0)
    return jnp.where(row < n - k, pltpu.roll(v, n - k, 0), 0.0)


def _window_sums(v, forward):
    out, s, k = [], v, 1
    for _ in range(POOL_GROUPS):
        s = s + _shift_rows(s, k, forward)
        out.append(s)
        k *= 2
    return out


def _by_group(vals, g):
    out = vals[-1]
    for idx in range(len(vals) - 2, -1, -1):
        out = jnp.where(g == idx, vals[idx], out)
    return out


def _inv_count(shape, g):
    t1 = lax.broadcasted_iota(jnp.int32, shape, 0) + 1
    window = _by_group([jnp.int32(2 ** (i + 1)) for i in range(POOL_GROUPS)], g)
    return 1.0 / jnp.minimum(t1, window).astype(F32)


def _pool_fwd(u, grp, scale, seq):
    tokens = u.shape[0]

    def body(u_ref, grp_ref, sc_ref, pooled_ref, pg_ref, ps_ref):
        g = pl.program_id(1)
        uv = u_ref[...]
        sums = _by_group(_window_sums(uv, True), g)
        pooled = (sums * _inv_count(uv.shape, g) - uv).astype(BF16)
        pg = jnp.dot(pooled, grp_ref[0].astype(BF16), preferred_element_type=F32)
        pooled_ref[...] = pooled
        pg_ref[...] = pg
        ps_ref[...] = (pg * sc_ref[...]).astype(BF16)

    blk = pl.BlockSpec((seq, POOL_GROUP_DIM), lambda b, g: (b, g))
    return pl.pallas_call(
        body, name="pool_fwd", grid=(tokens // seq, POOL_GROUPS),
        out_shape=(jax.ShapeDtypeStruct(u.shape, BF16), jax.ShapeDtypeStruct(u.shape, F32),
                   jax.ShapeDtypeStruct(u.shape, BF16)),
        in_specs=[blk, pl.BlockSpec((1, POOL_GROUP_DIM, POOL_GROUP_DIM), lambda b, g: (g, 0, 0)),
                  pl.BlockSpec((1, POOL_GROUP_DIM), lambda b, g: (0, g))],
        out_specs=(blk, blk, blk),
        compiler_params=_params("parallel", "parallel"),
    )(u, grp, scale)


def _pool_bwd(dps, pooled, pg, grp, scale, seq):
    tokens = dps.shape[0]

    def body(dps_ref, pooled_ref, pg_ref, grp_ref, sc_ref, du_ref, dgrp_ref, dsc_ref):
        g, b = pl.program_id(0), pl.program_id(1)
        dpsv = dps_ref[...]
        dpg = (dpsv * sc_ref[...]).astype(BF16)
        dsc = _colsum(dpsv * pg_ref[...])
        dgrp = lax.dot_general(pooled_ref[...], dpg, _DIMS["tn"], preferred_element_type=F32)

        @pl.when(b == 0)
        def _():
            dsc_ref[...] = dsc
            dgrp_ref[0] = dgrp

        @pl.when(b > 0)
        def _():
            dsc_ref[...] += dsc
            dgrp_ref[0] += dgrp

        dpool = lax.dot_general(dpg, grp_ref[0].astype(BF16), _DIMS["nt"], preferred_element_type=F32)
        sums = _by_group(_window_sums(dpool * _inv_count(dpool.shape, g), False), g)
        du_ref[...] = (sums - dpool).astype(BF16)

    blk = pl.BlockSpec((seq, POOL_GROUP_DIM), lambda g, b: (b, g))
    grp_spec = pl.BlockSpec((1, POOL_GROUP_DIM, POOL_GROUP_DIM), lambda g, b: (g, 0, 0))
    vec_spec = pl.BlockSpec((1, POOL_GROUP_DIM), lambda g, b: (0, g))
    return pl.pallas_call(
        body, name="pool_bwd", grid=(POOL_GROUPS, tokens // seq),
        out_shape=(jax.ShapeDtypeStruct(dps.shape, BF16), jax.ShapeDtypeStruct(grp.shape, F32),
                   jax.ShapeDtypeStruct(scale.shape, F32)),
        in_specs=[blk, blk, blk, grp_spec, vec_spec],
        out_specs=(blk, grp_spec, vec_spec),
        compiler_params=_params("arbitrary", "arbitrary"),
    )(dps, pooled, pg, grp, scale)


def _lane(shape):
    return lax.broadcasted_iota(jnp.int32, shape, len(shape) - 1)


def _rot(y):
    lane = _lane(y.shape)
    r = jnp.where(lane < QK_NOPE + QK_ROPE // 2,
                  -pltpu.roll(y, HEAD_SLAB - QK_ROPE // 2, 1), pltpu.roll(y, QK_ROPE // 2, 1))
    return jnp.where(jnp.logical_and(lane >= QK_NOPE, lane < QK_NOPE + QK_ROPE), r, 0.0)


def _part_rstd(x):
    sq = x * x
    nope = _lane(x.shape) < QK_NOPE
    s_nope = jnp.sum(jnp.where(nope, sq, 0.0), axis=-1, keepdims=True)
    s_rope = jnp.sum(sq, axis=-1, keepdims=True) - s_nope
    return jnp.where(nope, lax.rsqrt(s_nope * (1.0 / QK_NOPE) + NORM_EPS),
                     lax.rsqrt(s_rope * (1.0 / QK_ROPE) + NORM_EPS))


def _part_norm_bwd(dy, x, r, g):
    nope = _lane(x.shape) < QK_NOPE
    xhat = x * r
    dxhat = dy * g
    prod = dxhat * xhat
    m_nope = jnp.sum(jnp.where(nope, prod, 0.0), axis=-1, keepdims=True)
    m_rope = jnp.sum(prod, axis=-1, keepdims=True) - m_nope
    mean = jnp.where(nope, m_nope * (1.0 / QK_NOPE), m_rope * (1.0 / QK_ROPE))
    return r * (dxhat - xhat * mean), dy * xhat


def _mixer_in(h, wt_a, wt_p, wt_g, g_q, g_kv, seq):
    def fn(rows, bats, vecs):
        z_a, z_p, z_g = rows[:3]
        q, kv = z_a[:, :Q_LORA], z_a[:, Q_LORA:Q_LORA + KV_LORA]
        return [z_a, z_p, z_g, q * _rstd(q) * vecs[0], kv * _rstd(kv) * vecs[1]], [], []
    return _rowmap("mix_in", fn, seq, [h], vecs=[g_q, g_kv], lhs=lambda rows, vecs: rows[0],
                   mm=[(wt_a, "nt"), (wt_p, "nt"), (wt_g, "nt")], mm_sum=False,
                   row_outs=[(wt_a.shape[0], F32), (wt_p.shape[0], F32), (wt_g.shape[0], BF16),
                             (Q_LORA, BF16), (KV_LORA, BF16)])


def _latent_norm_bwd(dqp, wtq_pad, dkv, wt_kv, dkr, z_a, g_q, g_kv, seq):
    def fn(rows, bats, vecs):
        dq, dkv, dkrv, z = rows
        q, kv = z[:, :Q_LORA], z[:, Q_LORA:Q_LORA + KV_LORA]
        dxq, dgq = _norm_bwd(dq, q, _rstd(q), vecs[0])
        dxkv, dgkv = _norm_bwd(dkv, kv, _rstd(kv), vecs[1])
        return [jnp.concatenate([dxq, dxkv, dkrv], axis=1)], [], [_colsum(dgq), _colsum(dgkv)]
    return _rowmap("latent_norm_bwd", fn, seq, [dqp, dkv, dkr, z_a], vecs=[g_q, g_kv],
                   row_outs=[(Q_LORA + KV_LORA + HEAD_SLAB, BF16)], vec_outs=[Q_LORA, KV_LORA],
                   mm=[(wtq_pad, "nn"), (wt_kv, "nn")], mm_sum=False)


def _qk_prep_fwd(qn, wtq_pad, kvn, wt_kv, z_a, pos, g_q, g_kn, g_kr, inv_freq, seq):
    def fn(rows, bats, vecs):
        qv, kvv, kr, p = rows
        gq, gkn, gkr, invf = vecs
        ang = p * invf
        cos, sin = jnp.cos(ang), jnp.sin(ang)
        nope = _lane(kr.shape) < QK_NOPE
        krn = kr * _rstd(kr, QK_ROPE) * gkr
        krr = krn * cos + _rot(krn) * sin
        qs, ks, vs = [], [], []
        for h in range(N_HEADS):
            xq = qv[:, h * HEAD_SLAB:(h + 1) * HEAD_SLAB]
            y = xq * _part_rstd(xq) * gq
            qs.append(y * cos + _rot(y) * sin)
            xk = kvv[:, h * HEAD_SLAB:(h + 1) * HEAD_SLAB]
            kn = jnp.where(nope, xk, 0.0)
            ks.append(jnp.where(nope, kn * _rstd(kn, QK_NOPE) * gkn, krr))
            vs.append(jnp.where(nope, 0.0, xk))
        return [jnp.concatenate(v, axis=1) for v in (qs, ks, vs)] + [qv, kvv], [], []
    width = N_HEADS * HEAD_SLAB
    return _rowmap("qk_prep", fn, seq, [qn, kvn, (z_a, HEAD_SLAB, 5), pos], vecs=[g_q, g_kn, g_kr, inv_freq],
                   row_outs=[(width, BF16)] * 3 + [(width, F32)] * 2, mm=[(wtq_pad, "nt"), (wt_kv, "nt")],
                   mm_sum=False)


def _qk_prep_bwd(dqc, dkc, dvp, qp, kv, z_a, pos, g_q, g_kn, g_kr, inv_freq, seq):
    def fn(rows, bats, vecs):
        dq, dk, dv, qv, kvv, kr, p = rows
        gq, gkn, gkr, invf = vecs
        ang = p * invf
        cos, sin = jnp.cos(ang), jnp.sin(ang)
        nope = _lane(kr.shape) < QK_NOPE
        dqs, dkvs = [], []
        dgq = jnp.zeros((1, HEAD_SLAB), F32)
        dgkn = jnp.zeros((1, HEAD_SLAB), F32)
        dkrr = jnp.zeros(kr.shape, F32)
        for h in range(N_HEADS):
            sl = slice(h * HEAD_SLAB, (h + 1) * HEAD_SLAB)
            dyr = dq[:, sl]
            dy = dyr * cos - _rot(dyr * sin)
            xq = qv[:, sl]
            dx, dg = _part_norm_bwd(dy, xq, _part_rstd(xq), gq)
            dqs.append(dx)
            dgq = dgq + _colsum(dg)
            dkh = dk[:, sl]
            dkrr = dkrr + jnp.where(nope, 0.0, dkh)
            kn = jnp.where(nope, kvv[:, sl], 0.0)
            dxk, dgk = _norm_bwd(jnp.where(nope, dkh, 0.0), kn, _rstd(kn, QK_NOPE), gkn, QK_NOPE)
            dgkn = dgkn + _colsum(dgk)
            dkvs.append(jnp.where(nope, dxk, dv[:, sl]))
        dkrn = dkrr * cos - _rot(dkrr * sin)
        dkr, dgkr = _norm_bwd(dkrn, kr, _rstd(kr, QK_ROPE), gkr, QK_ROPE)
        return ([jnp.concatenate(dqs, axis=1), jnp.concatenate(dkvs, axis=1), dkr], [],
                [dgq, dgkn, _colsum(dgkr)])
    width = N_HEADS * HEAD_SLAB
    return _rowmap("qk_prep_bwd", fn, seq, [dqc, dkc, dvp, qp, kv, (z_a, HEAD_SLAB, 5), pos],
                   vecs=[g_q, g_kn, g_kr, inv_freq],
                   row_outs=[(width, BF16), (width, BF16), (HEAD_SLAB, F32)],
                   vec_outs=[HEAD_SLAB] * 3, ts=_tile(seq, (512, 256, 128, 64, 32, 16, 8)))


def _scores(q, k_ref, keys, tq):
    s = lax.dot_general(q, k_ref[0:keys, :], _DIMS["nt"], preferred_element_type=F32) * ATTN_SCALE
    row = lax.broadcasted_iota(jnp.int32, (tq, tq), 0)
    col = lax.broadcasted_iota(jnp.int32, (tq, tq), 1)
    diag = jnp.where(col <= row, s[:, keys - tq:], -1e30)
    return diag if keys == tq else jnp.concatenate([s[:, :keys - tq], diag], axis=1)


def _attn_fwd(qc, kc, vp, seq):
    tokens = qc.shape[0]
    tq = _tile(seq, (256, 128))
    nq = seq // tq

    def body(q_ref, k_ref, v_ref, o_ref, lse_ref):
        for i in range(nq):
            rows, keys = slice(i * tq, (i + 1) * tq), (i + 1) * tq
            s = _scores(q_ref[rows, :], k_ref, keys, tq)
            m = jnp.max(s, axis=-1, keepdims=True)
            p = jnp.exp(s - m)
            l = jnp.sum(p, axis=-1, keepdims=True)
            acc = jnp.dot(p.astype(BF16), v_ref[0:keys, :], preferred_element_type=F32)
            o_ref[rows, :] = (acc / l).astype(BF16)
            lse_ref[rows, :] = jnp.broadcast_to(m + jnp.log(l), (tq, HEAD_SLAB))

    spec = pl.BlockSpec((seq, HEAD_SLAB), lambda b, h: (b, h))
    return pl.pallas_call(
        body, name="attn_fwd", grid=(tokens // seq, N_HEADS),
        out_shape=(jax.ShapeDtypeStruct(qc.shape, BF16), jax.ShapeDtypeStruct(qc.shape, F32)),
        in_specs=[spec] * 3, out_specs=(spec, spec),
        compiler_params=_params("parallel", "parallel"),
    )(qc, kc, vp)


def _attn_bwd(qc, kc, vp, o, lse, do, seq):
    tokens = qc.shape[0]
    tq = _tile(seq, (256, 128))
    nq = seq // tq

    def body(q_ref, k_ref, v_ref, o_ref, lse_ref, do_ref, dq_ref, dk_ref, dv_ref):
        dk_ref[...] = jnp.zeros(dk_ref.shape, F32)
        dv_ref[...] = jnp.zeros(dv_ref.shape, F32)
        for i in range(nq):
            rows, keys = slice(i * tq, (i + 1) * tq), (i + 1) * tq
            q, dov = q_ref[rows, :], do_ref[rows, :]
            delta = jnp.sum(dov.astype(F32) * o_ref[rows, :].astype(F32), axis=-1, keepdims=True)
            s = _scores(q, k_ref, keys, tq)
            p = jnp.exp(s - jnp.tile(lse_ref[rows, :], (1, keys // HEAD_SLAB)))
            dp = lax.dot_general(dov, v_ref[0:keys, :], _DIMS["nt"], preferred_element_type=F32)
            ds = (p * (dp - delta) * ATTN_SCALE).astype(BF16)
            dq_ref[rows, :] = jnp.dot(ds, k_ref[0:keys, :], preferred_element_type=F32)
            dk_ref[0:keys, :] += lax.dot_general(ds, q, _DIMS["tn"], preferred_element_type=F32)
            dv_ref[0:keys, :] += lax.dot_general(p.astype(BF16), dov, _DIMS["tn"], preferred_element_type=F32)

    spec = pl.BlockSpec((seq, HEAD_SLAB), lambda b, h: (b, h))
    out = jax.ShapeDtypeStruct(qc.shape, F32)
    return pl.pallas_call(
        body, name="attn_bwd", grid=(tokens // seq, N_HEADS),
        out_shape=(out, out, out), in_specs=[spec] * 6, out_specs=(spec, spec, spec),
        compiler_params=_params("parallel", "parallel"),
    )(qc, kc, vp, o, lse, do)


def _adamw(w, g, m, v, name):
    rows, cols = w.shape
    whole = rows * cols * 4 <= ADAMW_WHOLE_BYTES
    tr = rows if whole else _tile(rows, (256, 128, 64, 32, 16, 8))
    c1 = 1.0 - ADAM_B1 ** ADAM_STEP
    c2 = 1.0 - ADAM_B2 ** ADAM_STEP

    def body(w_ref, g_ref, m_ref, v_ref, d_ref, nm_ref, nv_ref):
        gv = g_ref[...]
        nm = ADAM_B1 * m_ref[...] + (1.0 - ADAM_B1) * gv
        nv = ADAM_B2 * v_ref[...] + (1.0 - ADAM_B2) * (gv * gv)
        d_ref[...] = -ADAM_LR * ((nm / c1) / (jnp.sqrt(nv / c2) + ADAM_EPS) + ADAM_WD * w_ref[...])
        nm_ref[...] = nm
        nv_ref[...] = nv

    spec = pl.BlockSpec((tr, cols), lambda i: (i, 0))
    out = jax.ShapeDtypeStruct(w.shape, F32)
    return pl.pallas_call(
        body, name=name, grid=(rows // tr,), out_shape=(out, out, out),
        in_specs=[spec] * 4, out_specs=(spec, spec, spec),
        compiler_params=_params("parallel"),
    )(w, g, m, v)


def _adamw_landed(w, landed, m, v, name):
    rows, cols = w.shape
    tr = _tile(rows, (176, 128, 96, 64, 32, 16, 8))
    c1 = 1.0 - ADAM_B1 ** ADAM_STEP
    c2 = 1.0 - ADAM_B2 ** ADAM_STEP
    n_parts = len(landed)

    def body(*refs):
        w_ref, m_ref, v_ref = refs[:3]
        g_ref, d_ref, nm_ref, nv_ref = refs[3 + n_parts:]
        parts = []
        for x_ref in refs[3:3 + n_parts]:
            acc = x_ref[0].astype(F32)
            for d in range(1, N_DEV):
                acc = acc + x_ref[d].astype(F32)
            parts.append(acc)
        gv = parts[0] if n_parts == 1 else jnp.concatenate(parts, axis=1)
        nm = ADAM_B1 * m_ref[...] + (1.0 - ADAM_B1) * gv
        nv = ADAM_B2 * v_ref[...] + (1.0 - ADAM_B2) * (gv * gv)
        g_ref[...] = gv
        d_ref[...] = -ADAM_LR * ((nm / c1) / (jnp.sqrt(nv / c2) + ADAM_EPS) + ADAM_WD * w_ref[...])
        nm_ref[...] = nm
        nv_ref[...] = nv

    spec = pl.BlockSpec((tr, cols), lambda i: (i, 0))
    out = jax.ShapeDtypeStruct(w.shape, F32)
    return pl.pallas_call(
        body, name=name, grid=(rows // tr,), out_shape=(out, out, out, out),
        in_specs=[spec] * 3 + [pl.BlockSpec((N_DEV, tr, x.shape[2]), lambda i: (0, i, 0)) for x in landed],
        out_specs=(spec, spec, spec, spec),
        compiler_params=_params("parallel"),
    )(w, m, v, *landed)


def _mod_cols(c_all, w_ada, b_cols):
    def body(c_ref, w_ref, b_ref, act_ref, mod_ref):
        cv = c_ref[...]
        act = cv * _sigmoid(cv)
        act_ref[...] = act
        mod_ref[...] = jnp.dot(act.astype(BF16), w_ref[...].astype(BF16),
                               preferred_element_type=F32) + b_ref[...]

    n = w_ada.shape[1]
    return pl.pallas_call(
        body, name="mod_cols",
        out_shape=(jax.ShapeDtypeStruct(c_all.shape, F32), jax.ShapeDtypeStruct((c_all.shape[0], n), F32)),
        compiler_params=pltpu.CompilerParams(vmem_limit_bytes=VMEM_LIMIT),
    )(c_all, w_ada, b_cols)


def _ada_grads(c_act, dmod_all, dmod_cols):
    def body(c_ref, d_ref, dc_ref, gw_ref, gb_ref):
        gw_ref[...] = lax.dot_general(c_ref[...].astype(BF16), dc_ref[...].astype(BF16), _DIMS["tn"],
                                      preferred_element_type=F32)
        gb_ref[...] = _colsum(d_ref[...])

    return pl.pallas_call(
        body, name="ada_grads",
        out_shape=(jax.ShapeDtypeStruct((c_act.shape[1], dmod_cols.shape[1]), F32),
                   jax.ShapeDtypeStruct((1, dmod_all.shape[1]), F32)),
        compiler_params=pltpu.CompilerParams(vmem_limit_bytes=VMEM_LIMIT),
    )(c_act, dmod_all, dmod_cols)


def _flat_rows(a):
    flat = a.reshape(-1)
    pad = (-flat.shape[0]) % (LANES * SUBLANES)
    if pad:
        flat = jnp.pad(flat, (0, pad))
    return flat.reshape(-1, LANES)


def _gather_start(w, groups, tag, after=None, peers=None):
    shards = [[(w[n] if n in ROW_SHARDED else w[n].T).astype(BF16) for n in names] for names in groups]
    return _exchange_start_groups(shards, f"gather_{tag}_start", after=after, peers=peers)


def _gather_wait(handle, names, tag, after, peers=ALL_PEERS):
    landed = _exchange_wait(handle, f"gather_{tag}_wait", after=after, peers=peers)
    if peers == CHIP_PEERS:
        landed = [_sibling_forward(x, f"gather_{tag}_forward{i}") for i, x in enumerate(landed)]
    return {n: g.reshape(-1, g.shape[2]) for n, g in zip(names, landed)}


def _scatter_start(grads, names, tag, after=None):
    blocks = [grads[n].reshape(N_DEV, -1, grads[n].shape[1]) for n in names]
    return _exchange_start(blocks, f"scatter_{tag}_start", scatter=True, after=after)


def _scatter_wait(handle, names, tag, after):
    landed = _exchange_wait(handle, f"scatter_{tag}_wait", scatter=True, after=after)
    return {n: [x] for n, x in zip(names, landed)}


def _pack_small(vals):
    return jnp.concatenate([_flat_rows(v.astype(F32)) for v in vals], axis=0)


def _unpack_small(packed, like):
    out, row = [], 0
    for v in like:
        rows = _flat_rows(v).shape[0]
        out.append(packed[row:row + rows].reshape(-1)[:v.size].reshape(v.shape))
        row += rows
    return out


def _lanes128(*parts):
    out = jnp.zeros((HEAD_SLAB,), F32)
    for off, v in parts:
        out = lax.dynamic_update_slice(out, v.reshape(-1).astype(F32), (off,))
    return out.reshape(1, HEAD_SLAB)


def _step(x, c, positions, w, m, v, loss_target):
    nseq, seq, _ = x.shape
    tokens = nseq * seq
    me = _index(_my_pos())
    strip = lambda d: {n: (a[0] if a.ndim > 2 else a) for n, a in d.items()}
    shapes = {n: a.shape for n, a in w.items()}
    w, m, v = strip(w), strip(m), strip(v)

    c_all = _all_gather(c.reshape(-1, LANES), "gather_c").reshape(N_DEV * nseq, D_MODEL)
    n_ada = w["w_ada"].shape[1]
    b_cols = lax.dynamic_slice(w["b_ada"], (0, me * n_ada), (1, n_ada))
    c_act, mod_cols = _mod_cols(c_all, w["w_ada"], b_cols)
    mod_all = _all_gather(mod_cols, "gather_mod")
    mod = lax.dynamic_slice(mod_all, (0, me * nseq, 0), (N_DEV, nseq, n_ada))
    mod = mod.transpose(1, 0, 2).reshape(nseq, 3, 3, 1, D_MODEL)

    (h_f1i, h_f1o, h_mix_in, h_mix, h_f2), tok = _gather_start(
        w, (("w_ffn1_in",), ("w_ffn1_out",), MIXER[:1], MIXER[1:], ("w_ffn2_in", "w_ffn2_out")), "weights",
        after=mod_all, peers=[CHIP_PEERS, ALL_PEERS, CHIP_PEERS, ALL_PEERS, ALL_PEERS])
    started = tok[0:1, 0:1]

    g_q = _lanes128((0, w["q_norm_nope"]), (QK_NOPE, w["q_norm_rope"]))
    g_kn = _lanes128((0, w["k_norm_nope"]))
    g_kr = _lanes128((QK_NOPE, w["k_norm_rope"]))
    freq = ROPE_THETA ** (-jnp.arange(0, QK_ROPE, 2, dtype=F32) / QK_ROPE)
    inv_freq = _lanes128((QK_NOPE, jnp.concatenate([freq, freq])))
    pos = positions.reshape(tokens, 1).astype(F32)

    def sub(k, gamma, coef):
        return dict(gamma=w[gamma], shift=mod[:, k, 0] + started, scale=mod[:, k, 1], gate=coef * mod[:, k, 2])
    p1, pm, p2 = sub(0, "norm_ffn1", 0.5), sub(1, "norm_mix", 1.0), sub(2, "norm_ffn2", 0.5)

    x0 = x.reshape(tokens, D_MODEL)
    h1 = _norm_mod_fwd(x0, p1, seq, "ffn1_norm")
    wt_f1i = _gather_wait(h_f1i, ("w_ffn1_in",), "ffn1_in", h1, peers=CHIP_PEERS)["w_ffn1_in"]
    g1, u1, a1 = _ffn_in_act(h1, wt_f1i, "ffn1_in")
    w_f1o = _gather_wait(h_f1o, ("w_ffn1_out",), "ffn1_out", a1)["w_ffn1_out"]
    x1, f1, h2 = _out_residual(a1, w_f1o, x0, p1["gate"], pm, seq, "ffn1_out")
    saved1 = (x0, h1, g1, u1, a1, wt_f1i, w_f1o)

    wt_in = _gather_wait(h_mix_in, MIXER[:1], "mix_in", h2, peers=CHIP_PEERS)["w_in"]
    zero_rows = lambda rows: jnp.zeros((rows, D_MODEL), BF16)
    wt_p = wt_in[:512]
    wt_a = jnp.concatenate([wt_in[512:1152], zero_rows(QK_NOPE), wt_in[1152:1184], zero_rows(32)], axis=0)
    wt_g = wt_in[1184:]
    z_a, z_p, z_g, qn, kvn = _mixer_in(h2, wt_a, wt_p, wt_g, w["q_a_norm"], w["kv_a_norm"], seq)

    full = _gather_wait(h_mix, MIXER[1:], "mix", z_g)
    wtq_pad = jnp.pad(full["w_q_up"].reshape(N_HEADS, 96, Q_LORA), ((0, 0), (0, 32), (0, 0))).reshape(-1, Q_LORA)
    wtmla_pad = jnp.pad(full["w_mla_proj"].reshape(D_MODEL, N_HEADS, 64), ((0, 0), (0, 0), (64, 0))).reshape(D_MODEL, -1)
    wt_pool, wt_kv, w_mix_out = full["w_pool_proj"], full["w_kv_up"], full["w_out"]
    pooled, pg, ps = _pool_fwd(z_p, w["pool_grp"], w["pool_scale"], seq)
    qc, kc, vp, qp, kv = _qk_prep_fwd(qn, wtq_pad, kvn, wt_kv, z_a, pos, g_q, g_kn, g_kr, inv_freq, seq)
    attn, lse = _attn_fwd(qc, kc, vp, seq)
    x2, o_mix, h3, merged, br_pool, br_mla = _mix_out(z_g, ps, attn, x1, wt_pool, wtmla_pad, w_mix_out, pm["gate"],
                                                      p2, seq)

    ffn2_w = _gather_wait(h_f2, ("w_ffn2_in", "w_ffn2_out"), "ffn2", h3)
    g2, u2, a2 = _ffn_in_act(h3, ffn2_w["w_ffn2_in"], "ffn2_in")
    dy, df2, dgate2, sq_err = _out_loss(a2, ffn2_w["w_ffn2_out"], x2, p2["gate"],
                                        loss_target.reshape(tokens, D_MODEL), seq, "ffn2_out")
    saved2 = (x2, h3, g2, u2, a2, ffn2_w["w_ffn2_in"], ffn2_w["w_ffn2_out"])

    grads = {}
    (dx2, do_mix, dsh2, dsc2, dgate_m, dg_ffn2), ops2 = _ffn_bwd_x(df2, dy, saved2, p2, seq, "ffn2", (o_mix, pm["gate"]))
    grads["w_ffn2_out"], grads["w_ffn2_in"] = _ffn_bwd_wout(ops2[0], ops2[1], "ffn2"), _ffn_bwd_win(ops2, "ffn2")
    s_f2, tok = _scatter_start(grads, ("w_ffn2_in", "w_ffn2_out"), "ffn2")

    grads["w_out"] = _mm(merged, do_mix, "tn", "mix_bwd_wout", out_dtype=BF16, tm=512, tn=D_MODEL)

    def merge_bwd(rows, bats, vecs):
        dmv, zg, bp, bm = (r.astype(F32) for r in rows)
        s_p, s_m = _sigmoid(zg[:, :D_MODEL]), _sigmoid(zg[:, D_MODEL:])
        dzg = jnp.concatenate([dmv * bp * s_p * (1.0 - s_p), dmv * bm * s_m * (1.0 - s_m)], axis=1)
        dbp, dbm = (dmv * s_p).astype(BF16), (dmv * s_m).astype(BF16)
        dps_v = jnp.dot(dbp, vecs[0], preferred_element_type=F32)
        dattn = jnp.dot(dbm, vecs[1], preferred_element_type=F32)
        return [dbp, dbm, dzg, dps_v, dattn], [], []
    dbr_pool, dbr_mla, dz_g, dps, d_attn = _rowmap(
        "mix_bwd_dmerged", merge_bwd, seq, [do_mix, z_g, br_pool, br_mla], vecs=[wt_pool, wtmla_pad],
        row_outs=[(D_MODEL, BF16), (D_MODEL, BF16), (2 * D_MODEL, BF16), (POOL_WIDTH, F32), (D_MODEL, BF16)],
        mm=(w_mix_out, "nt"))

    grads["w_pool_proj"] = _mm(dbr_pool, ps, "tn", "pool_bwd_wproj", out_dtype=BF16, tm=512, tn=POOL_WIDTH)
    dz_p, dgrp, dpool_scale = _pool_bwd(dps, pooled, pg, w["pool_grp"], w["pool_scale"] + tok[0:1, 0:1], seq)

    dwtmla_pad = _mm(dbr_mla, attn, "tn", "mla_bwd_wproj", out_dtype=BF16, tm=512, tn=D_MODEL)
    grads["w_mla_proj"] = dwtmla_pad.reshape(D_MODEL, N_HEADS, HEAD_SLAB)[:, :, 64:].reshape(D_MODEL, -1)
    dqc, dkc, dvp = _attn_bwd(qc, kc, vp, attn, lse, d_attn, seq)
    dqp, dkv, dkr, dg_q, dg_kn, dg_kr = _qk_prep_bwd(dqc, dkc, dvp, qp, kv, z_a, pos, g_q, g_kn, g_kr, inv_freq, seq)
    dwtq_pad = _mm(dqp, qn, "tn", "q_up_bwd_w", out_dtype=BF16, tm=512, tn=Q_LORA)
    grads["w_q_up"] = dwtq_pad.reshape(N_HEADS, HEAD_SLAB, Q_LORA)[:, :96].reshape(-1, Q_LORA)
    grads["w_kv_up"] = _mm(dkv, kvn, "tn", "kv_up_bwd_w", out_dtype=BF16, tm=512, tn=KV_LORA)
    dz_a, dg_qa, dg_kva = _latent_norm_bwd(dqp, wtq_pad, dkv, wt_kv, dkr, z_a, w["q_a_norm"], w["kv_a_norm"], seq)

    dwt_a, dwt_p, dwt_g = _mm_tn_shared([dz_a, dz_p, dz_g], h2, "mix_in_bwd_w")
    grads["w_in"] = jnp.concatenate([dwt_p, dwt_a[:640], dwt_a[704:736], dwt_g], axis=0)

    small_early = [dg_ffn2.reshape(w["norm_ffn2"].shape), dgrp, dpool_scale, dg_qa, dg_kva, dg_q[:, :QK_NOPE],
                   dg_q[:, QK_NOPE:QK_NOPE + QK_ROPE], dg_kn[:, :QK_NOPE], dg_kr[:, QK_NOPE:QK_NOPE + QK_ROPE]]
    s_small, tok = _exchange_start([_pack_small(small_early)], "gather_small_start")
    s_mix, tok = _scatter_start(grads, MIXER, "mix", after=tok)
    dh2 = [(dz_a, wt_a), (dz_p, wt_p), (dz_g, wt_g)]
    pm_tied = dict(pm, scale=pm["scale"] + tok[0:1, 0:1])
    dx1, df1, dsh_m, dsc_m, dgate1, dg_mix = _norm_mod_bwd(dh2, x1, dx2, pm_tied, seq, "mix_bwd_norm", (f1, p1["gate"]))

    handles = {}

    def ffn1_early(a, df):
        grads["w_ffn1_out"] = _ffn_bwd_wout(a, df, "ffn1")
        handles["f1o"], token = _scatter_start(grads, ("w_ffn1_out",), "ffn1_out")
        return token

    def ffn1_mid(operands):
        first = _ffn_bwd_win(operands, "ffn1", half=0)
        handles["f1i0"], token = _exchange_start([first.reshape(N_DEV, -1, first.shape[1])],
                                                 "scatter_ffn1_in0_start", scatter=True)
        return token

    (dx0, dsh1, dsc1, dg_ffn1), ops1 = _ffn_bwd_x(df1, dx1, saved1, p1, seq, "ffn1", early=ffn1_early,
                                                     mid=ffn1_mid)
    s_f1o = handles["f1o"]

    dmod = jnp.stack([jnp.stack([dsh1, dsc1, 0.5 * dgate1], axis=1),
                      jnp.stack([dsh_m, dsc_m, dgate_m], axis=1),
                      jnp.stack([dsh2, dsc2, 0.5 * dgate2], axis=1)], axis=1)
    n_dmod = nseq * 9 * D_MODEL // LANES
    tail = _all_gather(jnp.concatenate([dmod.reshape(-1, LANES), _flat_rows(dg_ffn1), _flat_rows(dg_mix),
                                        _flat_rows(sq_err)], axis=0), "gather_dmod")
    dmod_all = tail[:, :n_dmod].reshape(N_DEV * nseq, 9 * D_MODEL)

    second = _ffn_bwd_win(ops1, "ffn1", after=tail, half=1)
    s_second, tok = _exchange_start([second.reshape(N_DEV, -1, second.shape[1])], "scatter_ffn1_in1_start",
                                    scatter=True, after=tail)
    s_f1i = (handles["f1i0"], s_second)

    dmod_cols = lax.dynamic_slice(dmod_all, (0, me * n_ada), (N_DEV * nseq, n_ada)) + tok[0:1, 0:1]
    g_w_ada, g_b_ada = _ada_grads(c_act, dmod_all, dmod_cols)
    tail_sum = _sum_blocks(tail[:, n_dmod:], "sum_tail")
    g_norm_ffn1 = tail_sum[:SUBLANES].reshape(1, D_MODEL)
    g_norm_mix = tail_sum[SUBLANES:2 * SUBLANES].reshape(1, D_MODEL)
    loss = 0.5 * jnp.sum(tail_sum[2 * SUBLANES:]) * (1.0 / D_MODEL)
    small_all = _exchange_wait(s_small, "gather_small_wait", after=g_b_ada)[0]
    small_sum = _sum_blocks(small_all, "sum_small")
    small = dict(zip(SMALL[2:], _unpack_small(small_sum, [w[n] for n in SMALL[2:]])))
    grad_w = dict(small, w_ada=g_w_ada, b_ada=g_b_ada, norm_ffn1=g_norm_ffn1, norm_mix=g_norm_mix)

    delta, new_m, new_v = {}, {}, {}

    def update(names, landed=None):
        for n in names:
            if landed is None:
                delta[n], new_m[n], new_v[n] = _adamw(w[n], grad_w[n], m[n], v[n], f"adamw_{n}")
            elif n in KEPT_TRANSPOSED:
                res = _adamw_landed(w[n].T, landed[n], m[n].T, v[n].T, f"adamw_{n}")
                grad_w[n], delta[n], new_m[n], new_v[n] = (r.T for r in res)
            elif n in ROW_SHARDED:
                grad_w[n], delta[n], new_m[n], new_v[n] = _adamw_landed(w[n], landed[n], m[n], v[n], f"adamw_{n}")
            else:
                grad_w[n] = _sum_blocks(landed[n][0], f"sum_{n}").T
                delta[n], new_m[n], new_v[n] = _adamw(w[n], grad_w[n], m[n], v[n], f"adamw_{n}")

    update(("w_ada",))
    rep = ("b_ada",) + SMALL
    d_s, m_s, v_s = _adamw(_pack_small([w[n] for n in rep]), _pack_small([grad_w[n] for n in rep]),
                           _pack_small([m[n] for n in rep]), _pack_small([v[n] for n in rep]), "adamw_small")
    like = [w[n] for n in rep]
    for dst, packed in ((delta, d_s), (new_m, m_s), (new_v, v_s)):
        dst.update(zip(rep, _unpack_small(packed, like)))
    update(("w_ffn2_in", "w_ffn2_out"), _scatter_wait(s_f2, ("w_ffn2_in", "w_ffn2_out"), "ffn2", after=d_s))
    update(MIXER, _scatter_wait(s_mix, MIXER, "mix", after=delta["w_ffn2_out"]))
    update(("w_ffn1_out",), _scatter_wait(s_f1o, ("w_ffn1_out",), "ffn1_out", after=delta["w_out"]))
    halves = [_exchange_wait(h, f"scatter_ffn1_in{i}_wait", scatter=True, after=delta["w_ffn1_out"])[0]
              for i, h in enumerate(s_f1i)]
    update(("w_ffn1_in",), {"w_ffn1_in": halves})

    lead = lambda d: [d[n].reshape(shapes[n]) for n in WEIGHTS]
    return (loss, dx0.reshape(x.shape), *lead(grad_w), *lead(delta), *lead(new_m), *lead(new_v))


def kernel(x, c, positions, w_ada, b_ada, norm_ffn1, w_ffn1_in, w_ffn1_out, norm_mix, w_in, pool_grp, pool_scale, w_pool_proj, q_a_norm, w_q_up, kv_a_norm, w_kv_up, q_norm_nope, q_norm_rope, k_norm_nope, k_norm_rope, w_mla_proj, w_out, norm_ffn2, w_ffn2_in, w_ffn2_out, loss_target, m_w_ada, m_b_ada, m_norm_ffn1, m_w_ffn1_in, m_w_ffn1_out, m_norm_mix, m_w_in, m_pool_grp, m_pool_scale, m_w_pool_proj, m_q_a_norm, m_w_q_up, m_kv_a_norm, m_w_kv_up, m_q_norm_nope, m_q_norm_rope, m_k_norm_nope, m_k_norm_rope, m_w_mla_proj, m_w_out, m_norm_ffn2, m_w_ffn2_in, m_w_ffn2_out, v_w_ada, v_b_ada, v_norm_ffn1, v_w_ffn1_in, v_w_ffn1_out, v_norm_mix, v_w_in, v_pool_grp, v_pool_scale, v_w_pool_proj, v_q_a_norm, v_w_q_up, v_kv_a_norm, v_w_kv_up, v_q_norm_nope, v_q_norm_rope, v_k_norm_nope, v_k_norm_rope, v_w_mla_proj, v_w_out, v_norm_ffn2, v_w_ffn2_in, v_w_ffn2_out):
    w = dict(w_ada=w_ada, b_ada=b_ada, norm_ffn1=norm_ffn1, w_ffn1_in=w_ffn1_in, w_ffn1_out=w_ffn1_out, norm_mix=norm_mix, w_in=w_in, pool_grp=pool_grp, pool_scale=pool_scale, w_pool_proj=w_pool_proj, q_a_norm=q_a_norm, w_q_up=w_q_up, kv_a_norm=kv_a_norm, w_kv_up=w_kv_up, q_norm_nope=q_norm_nope, q_norm_rope=q_norm_rope, k_norm_nope=k_norm_nope, k_norm_rope=k_norm_rope, w_mla_proj=w_mla_proj, w_out=w_out, norm_ffn2=norm_ffn2, w_ffn2_in=w_ffn2_in, w_ffn2_out=w_ffn2_out)
    m = dict(w_ada=m_w_ada, b_ada=m_b_ada, norm_ffn1=m_norm_ffn1, w_ffn1_in=m_w_ffn1_in, w_ffn1_out=m_w_ffn1_out, norm_mix=m_norm_mix, w_in=m_w_in, pool_grp=m_pool_grp, pool_scale=m_pool_scale, w_pool_proj=m_w_pool_proj, q_a_norm=m_q_a_norm, w_q_up=m_w_q_up, kv_a_norm=m_kv_a_norm, w_kv_up=m_w_kv_up, q_norm_nope=m_q_norm_nope, q_norm_rope=m_q_norm_rope, k_norm_nope=m_k_norm_nope, k_norm_rope=m_k_norm_rope, w_mla_proj=m_w_mla_proj, w_out=m_w_out, norm_ffn2=m_norm_ffn2, w_ffn2_in=m_w_ffn2_in, w_ffn2_out=m_w_ffn2_out)
    v = dict(w_ada=v_w_ada, b_ada=v_b_ada, norm_ffn1=v_norm_ffn1, w_ffn1_in=v_w_ffn1_in, w_ffn1_out=v_w_ffn1_out, norm_mix=v_norm_mix, w_in=v_w_in, pool_grp=v_pool_grp, pool_scale=v_pool_scale, w_pool_proj=v_w_pool_proj, q_a_norm=v_q_a_norm, w_q_up=v_w_q_up, kv_a_norm=v_kv_a_norm, w_kv_up=v_w_kv_up, q_norm_nope=v_q_norm_nope, q_norm_rope=v_q_norm_rope, k_norm_nope=v_k_norm_nope, k_norm_rope=v_k_norm_rope, w_mla_proj=v_w_mla_proj, w_out=v_w_out, norm_ffn2=v_norm_ffn2, w_ffn2_in=v_w_ffn2_in, w_ffn2_out=v_w_ffn2_out)
    return _step(x, c, positions, w, m, v, loss_target)
```

```python
import functools
import math

import jax
import jax.numpy as jnp
from jax import lax
from jax.experimental import pallas as pl
from jax.experimental.pallas import tpu as pltpu

F32 = jnp.float32
BF16 = jnp.bfloat16
MESH = pl.DeviceIdType.MESH
AXES = ("x", "y", "c")
N_DEV = 8

D_MODEL = 1024
D_FF = 2816
N_HEADS = 8
HEAD_SLAB = 128
QK_NOPE = 64
QK_ROPE = 32
POOL_WIDTH = 512
POOL_GROUPS = 4
POOL_GROUP_DIM = 128
Q_LORA = 384
KV_LORA = 256
ROPE_THETA = 10000.0
ATTN_SCALE = 1.0 / math.sqrt(QK_NOPE + QK_ROPE)
NORM_EPS = 1e-6
ADAM_LR, ADAM_B1, ADAM_B2, ADAM_EPS, ADAM_WD, ADAM_STEP = 0.001, 0.9, 0.999, 1e-08, 0.01, 10

LANES = 128
SUBLANES = 8
VMEM_LIMIT = 52 * 1024 * 1024
ADAMW_WHOLE_BYTES = 3 << 19
SUM_WHOLE_BYTES = 4 << 20

BIG = ("w_ffn1_in", "w_ffn1_out", "w_in", "w_pool_proj", "w_q_up", "w_kv_up",
       "w_mla_proj", "w_out", "w_ffn2_in", "w_ffn2_out")
ROW_SHARDED = ("w_ffn1_out", "w_out", "w_ffn2_out")
MIXER = ("w_in", "w_pool_proj", "w_q_up", "w_kv_up", "w_mla_proj", "w_out")
KEPT_TRANSPOSED = ("w_ffn1_in", "w_ffn2_in", "w_in", "w_q_up")
SMALL = ("norm_ffn1", "norm_mix", "norm_ffn2", "pool_grp", "pool_scale", "q_a_norm",
         "kv_a_norm", "q_norm_nope", "q_norm_rope", "k_norm_nope", "k_norm_rope")
WEIGHTS = ("w_ada", "b_ada", "norm_ffn1", "w_ffn1_in", "w_ffn1_out", "norm_mix", "w_in",
           "pool_grp", "pool_scale", "w_pool_proj", "q_a_norm", "w_q_up", "kv_a_norm",
           "w_kv_up", "q_norm_nope", "q_norm_rope", "k_norm_nope", "k_norm_rope",
           "w_mla_proj", "w_out", "norm_ffn2", "w_ffn2_in", "w_ffn2_out")


def _params(*sem):
    return pltpu.CompilerParams(dimension_semantics=sem, vmem_limit_bytes=VMEM_LIMIT)


def _tile(n, cands):
    for c in cands:
        if n % c == 0:
            return c
    return n


def _my_pos():
    return lax.axis_index("x"), lax.axis_index("y"), lax.axis_index("c")


def _flip(pos, k):
    x, y, c = pos
    fx, fy, fc = (k >> 2) & 1, (k >> 1) & 1, k & 1
    return ((1 - x) if fx else x, (1 - y) if fy else y, (1 - c) if fc else c)


def _index(pos):
    x, y, c = pos
    return 4 * x + 2 * y + c


def _exchange(arrays, name, scatter=False):
    n = len(arrays)

    def body(*refs):
        ins, outs = refs[:n], refs[n:2 * n]
        send_sems, recv_sems, local_sems = refs[2 * n:]
        me = _my_pos()
        mine, sends = [], []
        for a in range(n):
            own = ins[a].at[_index(me)] if scatter else ins[a]
            cp = pltpu.make_async_copy(own, outs[a].at[_index(me)], local_sems.at[a])
            cp.start()
            mine.append(cp)
        for k in range(1, N_DEV):
            peer = _flip(me, k)
            for a in range(n):
                cp = pltpu.make_async_remote_copy(
                    src_ref=ins[a].at[_index(peer)] if scatter else ins[a],
                    dst_ref=outs[a].at[_index(me)],
                    send_sem=send_sems.at[a, k - 1], recv_sem=recv_sems.at[a, k - 1],
                    device_id=peer, device_id_type=MESH)
                cp.start()
                sends.append(cp)
        for k in range(1, N_DEV):
            peer = _flip(me, k)
            for a in range(n):
                pltpu.make_async_remote_copy(
                    src_ref=ins[a].at[_index(me)] if scatter else ins[a],
                    dst_ref=outs[a].at[_index(peer)],
                    send_sem=send_sems.at[a, k - 1], recv_sem=recv_sems.at[a, k - 1],
                    device_id=peer, device_id_type=MESH).wait_recv()
        for cp in sends:
            cp.wait_send()
        for cp in mine:
            cp.wait()

    shape = lambda x: x.shape if scatter else (N_DEV,) + x.shape
    return pl.pallas_call(
        body, name=name,
        out_shape=tuple(jax.ShapeDtypeStruct(shape(x), x.dtype) for x in arrays),
        in_specs=[pl.BlockSpec(memory_space=pl.ANY)] * n,
        out_specs=tuple(pl.BlockSpec(memory_space=pl.ANY) for _ in arrays),
        scratch_shapes=[pltpu.SemaphoreType.DMA((n, N_DEV - 1)),
                        pltpu.SemaphoreType.DMA((n, N_DEV - 1)),
                        pltpu.SemaphoreType.DMA((n,))],
    )(*arrays)


def _all_gather(x, name):
    return _exchange([x], name)[0]


_HBM = pl.BlockSpec(memory_space=pltpu.HBM)
_SEM = pl.BlockSpec(memory_space=pltpu.SEMAPHORE)
_ANY = pl.BlockSpec(memory_space=pl.ANY)
_EFFECT = pltpu.SideEffectType.DATAFLOW_SIDE_EFFECTING


def _split_copy(ins, lands, send_sems, recv_sems, a, k, me, scatter, incoming):
    peer = _flip(me, k)
    block = me if incoming else peer
    return pltpu.make_async_remote_copy(
        src_ref=ins[a].at[_index(block)] if scatter else ins[a],
        dst_ref=lands[a].at[_index(peer if incoming else me)],
        send_sem=send_sems.at[a * (N_DEV - 1) + k - 1], recv_sem=recv_sems.at[a * (N_DEV - 1) + k - 1],
        device_id=peer, device_id_type=MESH)


ALL_PEERS = tuple(range(1, N_DEV))
CHIP_PEERS = (1, 2, 4, 6)


def _exchange_start_groups(groups, name, scatter=False, after=None, peers=None):
    peers = peers or [ALL_PEERS] * len(groups)
    sizes = [len(g) for g in groups]
    first = [sum(sizes[:i]) for i in range(len(sizes))]
    n, ng = sum(sizes), len(sizes)
    after = jnp.zeros((SUBLANES, LANES), F32) if after is None else after

    def body(*refs):
        ins, lands = refs[:n], refs[n:2 * n]
        sems = refs[2 * n + 1:2 * n + 1 + 2 * ng]
        me = _my_pos()
        for g in range(ng):
            part = slice(first[g], first[g] + sizes[g])
            for k in peers[g]:
                for a in range(sizes[g]):
                    _split_copy(ins[part], lands[part], sems[2 * g], sems[2 * g + 1], a, k, me, scatter, False).start()
        refs[-1][...] = jnp.zeros((SUBLANES, LANES), F32)

    shape = lambda x: x.shape if scatter else (N_DEV,) + x.shape
    hbm = lambda x: pltpu.with_memory_space_constraint(x, pltpu.HBM)
    srcs = [hbm(x) for g in groups for x in g]
    zones = [hbm(lax.empty(shape(x), x.dtype)) for g in groups for x in g]
    sem_shapes = [pltpu.SemaphoreType.DMA((s * (N_DEV - 1),)) for s in sizes for _ in range(2)]
    out = pl.pallas_call(
        body, name=name,
        out_shape=(*sem_shapes, *[pltpu.HBM(x.shape, x.dtype) for x in srcs + zones],
                   jax.ShapeDtypeStruct((SUBLANES, LANES), F32)),
        in_specs=[_HBM] * (2 * n) + [_ANY],
        out_specs=(*[_SEM] * (2 * ng), *[_HBM] * (2 * n), pl.BlockSpec(memory_space=pltpu.VMEM)),
        input_output_aliases={i: 2 * ng + i for i in range(2 * n)},
        compiler_params=pltpu.CompilerParams(has_side_effects=_EFFECT),
    )(*srcs, *zones, after)
    bufs = out[2 * ng:-1]
    handles = [(out[2 * g], out[2 * g + 1], *bufs[first[g]:first[g] + sizes[g]],
                *bufs[n + first[g]:n + first[g] + sizes[g]]) for g in range(ng)]
    return handles, out[-1]


def _exchange_start(arrays, name, scatter=False, after=None):
    handles, token = _exchange_start_groups([arrays], name, scatter, after)
    return handles[0], token


def _exchange_wait(handle, name, scatter=False, after=None, peers=ALL_PEERS):
    send_sems, recv_sems = handle[0], handle[1]
    n = (len(handle) - 2) // 2
    after = jnp.zeros((SUBLANES, LANES), F32) if after is None else after

    def body(*refs):
        ins, lands = refs[:n], refs[n:2 * n]
        send, recv = refs[2 * n], refs[2 * n + 1]
        me = _my_pos()
        for k in peers:
            for a in range(n):
                _split_copy(ins, lands, send, recv, a, k, me, scatter, False).wait_send()
                _split_copy(ins, lands, send, recv, a, k, me, scatter, True).wait_recv()

    bufs = handle[2:]
    out = pl.pallas_call(
        body, name=name,
        out_shape=tuple(pltpu.HBM(x.shape, x.dtype) for x in bufs),
        in_specs=[_HBM] * (2 * n) + [_SEM, _SEM, _ANY],
        out_specs=tuple([_HBM] * (2 * n)),
        input_output_aliases={i: i for i in range(2 * n)},
        compiler_params=pltpu.CompilerParams(has_side_effects=_EFFECT),
    )(*bufs, send_sems, recv_sems, after)
    me = _index(_my_pos())
    landed = []
    for src, land in zip(out[:n], out[n:]):
        own = lax.dynamic_slice_in_dim(src, me, 1, axis=0) if scatter else src[None]
        landed.append(lax.dynamic_update_slice_in_dim(land, own, me, axis=0))
    return landed


def _sibling_forward(x, name):
    flips = [k for k in CHIP_PEERS if k != 1]

    def body(x_ref, o_ref, send_sems, recv_sems):
        me = _my_pos()
        sibling = _flip(me, 1)
        sends = []
        for i, k in enumerate(flips):
            block = o_ref.at[_index(_flip(me, k))]
            cp = pltpu.make_async_remote_copy(src_ref=block, dst_ref=block, send_sem=send_sems.at[i],
                                              recv_sem=recv_sems.at[i], device_id=sibling, device_id_type=MESH)
            cp.start()
            sends.append(cp)
        for i, k in enumerate(flips):
            block = o_ref.at[_index(_flip(sibling, k))]
            pltpu.make_async_remote_copy(src_ref=block, dst_ref=block, send_sem=send_sems.at[i],
                                         recv_sem=recv_sems.at[i], device_id=sibling, device_id_type=MESH).wait_recv()
        for cp in sends:
            cp.wait_send()

    return pl.pallas_call(
        body, name=name, out_shape=jax.ShapeDtypeStruct(x.shape, x.dtype),
        in_specs=[_ANY], out_specs=_ANY, input_output_aliases={0: 0},
        scratch_shapes=[pltpu.SemaphoreType.DMA((len(flips),)), pltpu.SemaphoreType.DMA((len(flips),))],
    )(x)


def _sum_blocks(x, name):
    n, rows, cols = x.shape
    whole = x.size * x.dtype.itemsize <= SUM_WHOLE_BYTES
    tr = rows if whole else _tile(rows, (512, 256, 128, 64, 32, 16, 8))

    def body(x_ref, o_ref):
        acc = x_ref[0].astype(F32)
        for d in range(1, n):
            acc = acc + x_ref[d].astype(F32)
        o_ref[...] = acc

    return pl.pallas_call(
        body, name=name,
        out_shape=jax.ShapeDtypeStruct((rows, cols), F32),
        grid=(rows // tr,),
        in_specs=[pl.BlockSpec((n, tr, cols), lambda i: (0, i, 0))],
        out_specs=pl.BlockSpec((tr, cols), lambda i: (i, 0)),
        compiler_params=_params("parallel"),
    )(x)


_DIMS = {"nn": (((1,), (0,)), ((), ())), "nt": (((1,), (1,)), ((), ())), "tn": (((0,), (0,)), ((), ()))}


def _mm(a, b, mode, name, out_dtype=F32, tm=None, tn=None, add=None, after=None, b_cols=None):
    if mode == "tn":
        kdim, m = a.shape
    else:
        m, kdim = a.shape
    n = b.shape[0] if mode == "nt" else b.shape[1]
    tm = tm or _tile(m, (512, 256, 128))
    tn = tn or _tile(n, (512, 256, 128))
    j0 = 0
    if b_cols is not None:
        j0, n = b_cols[0], b_cols[1] * tn
    dims = _DIMS[mode]

    def body(*refs):
        refs = refs if after is None else refs[1:]
        acc = lax.dot_general(refs[0][...].astype(BF16), refs[1][...].astype(BF16), dims,
                              preferred_element_type=F32)
        if add is not None:
            acc = acc + refs[2][...]
        refs[-1][...] = acc.astype(out_dtype)

    a_spec = (pl.BlockSpec((kdim, tm), lambda i, j: (0, i)) if mode == "tn"
              else pl.BlockSpec((tm, kdim), lambda i, j: (i, 0)))
    b_spec = (pl.BlockSpec((tn, kdim), lambda i, j: (j, 0)) if mode == "nt"
              else pl.BlockSpec((kdim, tn), lambda i, j: (0, j + j0)))
    o_spec = pl.BlockSpec((tm, tn), lambda i, j: (i, j))
    in_specs, args = [a_spec, b_spec], [a, b]
    if add is not None:
        in_specs.append(o_spec)
        args.append(add)
    if after is not None:
        in_specs.insert(0, _ANY)
        args.insert(0, after)
    return pl.pallas_call(
        body, name=name, out_shape=jax.ShapeDtypeStruct((m, n), out_dtype), grid=(m // tm, n // tn),
        in_specs=in_specs, out_specs=o_spec,
        compiler_params=_params("parallel", "parallel"),
    )(*args)


def _mm_tn_shared(parts, b, name, tm=256):
    kdim, n = b.shape
    blocks = [a.shape[1] // tm for a in parts]
    first = [sum(blocks[:i]) for i in range(len(parts))]

    def body(*refs):
        a_refs, b_ref, o_refs = refs[:len(parts)], refs[len(parts)], refs[len(parts) + 1:]
        i = pl.program_id(0)
        for a_ref, o_ref, f, nb in zip(a_refs, o_refs, first, blocks):
            @pl.when(jnp.logical_and(i >= f, i < f + nb))
            def _(a_ref=a_ref, o_ref=o_ref):
                o_ref[...] = lax.dot_general(a_ref[...], b_ref[...], _DIMS["tn"],
                                             preferred_element_type=F32).astype(BF16)

    def window(f, nb):
        return lambda i: jnp.clip(i - f, 0, nb - 1)
    a_specs = [pl.BlockSpec((kdim, tm), functools.partial(lambda i, w: (0, w(i)), w=window(f, nb)))
               for f, nb in zip(first, blocks)]
    o_specs = [pl.BlockSpec((tm, n), functools.partial(lambda i, w: (w(i), 0), w=window(f, nb)))
               for f, nb in zip(first, blocks)]
    return pl.pallas_call(
        body, name=name, grid=(sum(blocks),),
        out_shape=tuple(jax.ShapeDtypeStruct((a.shape[1], n), BF16) for a in parts),
        in_specs=a_specs + [pl.BlockSpec((kdim, n), lambda i: (0, 0), pipeline_mode=pl.Buffered(1))],
        out_specs=tuple(o_specs),
        compiler_params=_params("arbitrary"),
    )(*parts, b)


def _rowmap(name, fn, seq, rows, bats=(), vecs=(), row_outs=(), bat_outs=(), vec_outs=(), ts=None, mm=None, lhs=None,
            after=None, mm_sum=True):
    mms = [] if mm is None else (mm if isinstance(mm, list) else [mm])
    rows = [r if isinstance(r, tuple) else (r, r.shape[1], 0) for r in rows]
    tokens = rows[0][0].shape[0]
    nseq = tokens // seq
    ts = ts or _tile(seq, (512, 256, 128, 64, 32, 16, 8))
    nt = seq // ts
    n_r, n_b, n_v = len(rows), len(bats), len(vecs)
    n_ro, n_bo = len(row_outs), len(bat_outs)

    def accumulate(ref, val, first):
        @pl.when(first)
        def _():
            ref[...] = val.reshape(ref.shape)

        @pl.when(jnp.logical_not(first))
        def _():
            ref[...] += val.reshape(ref.shape)

    def body(*refs):
        n_in = n_r + n_b + n_v + len(mms) + (after is not None)
        ins, outs = refs[:n_in], refs[n_in:]
        b_vals = [r[0] for r in ins[n_r:n_r + n_b]]
        v_vals = [r[...] for r in ins[n_r + n_b:n_r + n_b + n_v]]
        r_vals = [r[...] for r in ins[:n_r]]
        if mms:
            lefts = r_vals[:len(mms)] if lhs is None else [lhs(r_vals, v_vals)] * len(mms)
            parts = [lax.dot_general(left.astype(BF16), b_ref[...].astype(BF16), _DIMS[mode],
                                     preferred_element_type=F32)
                     for left, b_ref, (_, mode) in zip(lefts, ins[n_r + n_b + n_v:], mms)]
            accs = [functools.reduce(lambda x, y: x + y, parts)] if mm_sum else parts
            r_vals = accs + r_vals[len(mms):] if lhs is None else accs + [lefts[0]] + r_vals
        ro, bo, vo = fn(r_vals, b_vals, v_vals)
        for ref, val in zip(outs[:n_ro], ro):
            ref[...] = val.astype(ref.dtype)
        b, i = pl.program_id(0), pl.program_id(1)
        for ref, val in zip(outs[n_ro:n_ro + n_bo], bo):
            accumulate(ref, val, i == 0)
        for ref, val in zip(outs[n_ro + n_bo:], vo):
            accumulate(ref, val, jnp.logical_and(i == 0, b == 0))

    in_specs = [pl.BlockSpec((ts, w), functools.partial(lambda b, i, cb: (b * nt + i, cb), cb=cb))
                for _, w, cb in rows]
    in_specs += [pl.BlockSpec((1, 1, v.shape[2]), lambda b, i: (b, 0, 0)) for v in bats]
    in_specs += [pl.BlockSpec(v.shape, lambda b, i: (0, 0)) for v in vecs]
    extra = [b_arr for b_arr, _ in mms]
    in_specs += [pl.BlockSpec(b_arr.shape, lambda b, i: (0, 0), pipeline_mode=pl.Buffered(1)) for b_arr in extra]
    if after is not None:
        in_specs.append(_ANY)
        extra.append(after)
    out_shape = [jax.ShapeDtypeStruct((tokens, f), dt) for f, dt in row_outs]
    out_specs = [pl.BlockSpec((ts, f), lambda b, i: (b * nt + i, 0)) for f, _ in row_outs]
    out_shape += [jax.ShapeDtypeStruct((nseq, 1, f), F32) for f in bat_outs]
    out_specs += [pl.BlockSpec((1, 1, f), lambda b, i: (b, 0, 0)) for f in bat_outs]
    out_shape += [jax.ShapeDtypeStruct((1, f), F32) for f in vec_outs]
    out_specs += [pl.BlockSpec((1, f), lambda b, i: (0, 0)) for f in vec_outs]
    return pl.pallas_call(
        body, name=name, out_shape=tuple(out_shape), grid=(nseq, nt),
        in_specs=in_specs, out_specs=tuple(out_specs),
        compiler_params=_params("arbitrary", "arbitrary"),
    )(*([r[0] for r in rows] + list(bats) + list(vecs) + extra))


def _colsum(v):
    return jnp.sum(v, axis=0, keepdims=True)


def _rstd(x, width=None):
    width = width or x.shape[-1]
    return lax.rsqrt(jnp.sum(x * x, axis=-1, keepdims=True) * (1.0 / width) + NORM_EPS)


def _norm_bwd(dy, x, r, g, width=None):
    width = width or x.shape[-1]
    xhat = x * r
    dxhat = dy * g
    dx = r * (dxhat - xhat * (jnp.sum(dxhat * xhat, axis=-1, keepdims=True) * (1.0 / width)))
    return dx, dy * xhat


def _sigmoid(x):
    return 0.5 * jnp.tanh(0.5 * x) + 0.5


def _norm_mod(xv, g, sh, sc):
    return xv * _rstd(xv) * g * (1.0 + sc) + sh


def _norm_mod_fwd(x, p, seq, name):
    def fn(rows, bats, vecs):
        return [_norm_mod(rows[0], vecs[0], bats[0], bats[1])], [], []
    return _rowmap(name, fn, seq, [x], [p["shift"], p["scale"]], [p["gamma"]], row_outs=[(D_MODEL, BF16)])[0]


def _norm_mod_bwd(dh, x, dres, p, seq, name, prev=None, after=None):
    products = dh if isinstance(dh, list) else None
    lefts = [l for l, _ in products] if products else [dh]
    def fn(rows, bats, vecs):
        dhv, xv, dr = rows[:3]
        sc, g = bats[0], vecs[0]
        r = _rstd(xv)
        dxn, dg = _norm_bwd(dhv * (1.0 + sc), xv, r, g)
        dx = dr + dxn
        ro, bo = [dx], [_colsum(dhv), _colsum(dhv * (xv * r * g))]
        if prev is not None:
            ro.append(bats[1] * dx)
            bo.append(_colsum(dx * rows[3].astype(F32)))
        return ro, bo, [_colsum(dg)]
    more = prev is not None
    return _rowmap(name, fn, seq, lefts + [x, dres] + ([prev[0]] if more else []),
                   [p["scale"]] + ([prev[1]] if more else []), [p["gamma"]],
                   row_outs=[(D_MODEL, F32)] + ([(D_MODEL, BF16)] if more else []),
                   bat_outs=[D_MODEL] * (3 if more else 2), vec_outs=[D_MODEL],
                   mm=[(r, "nn") for _, r in products] if products else None, after=after)


def _ffn_in_act(h, wt_in, name):
    tokens = h.shape[0]
    tm, tn = _tile(tokens, (4096, 2048, 1024, 512)), 256
    nj = D_FF // tn

    def body(h_ref, wg_ref, wu_ref, g_ref, u_ref, a_ref):
        hv = h_ref[...]
        g = lax.dot_general(hv, wg_ref[...], _DIMS["nt"], preferred_element_type=F32)
        u = lax.dot_general(hv, wu_ref[...], _DIMS["nt"], preferred_element_type=F32)
        g_ref[...] = g.astype(BF16)
        u_ref[...] = u.astype(BF16)
        a_ref[...] = (g * _sigmoid(g) * u).astype(BF16)

    o_spec = pl.BlockSpec((tm, tn), lambda i, j: (i, j))
    out = jax.ShapeDtypeStruct((tokens, D_FF), BF16)
    return pl.pallas_call(
        body, name=name, grid=(tokens // tm, nj), out_shape=(out, out, out),
        in_specs=[pl.BlockSpec((tm, D_MODEL), lambda i, j: (i, 0)),
                  pl.BlockSpec((tn, D_MODEL), lambda i, j: (j, 0)),
                  pl.BlockSpec((tn, D_MODEL), lambda i, j: (j + nj, 0))],
        out_specs=(o_spec, o_spec, o_spec),
        compiler_params=_params("parallel", "parallel"),
    )(h, wt_in, wt_in)


def _out_residual(a, w_out, res, gate, nxt, seq, name):
    def fn(rows, bats, vecs):
        acc, rv = rows
        x_new = rv + bats[0] * acc
        return [x_new, acc, _norm_mod(x_new, vecs[0], bats[1], bats[2])], [], []
    return _rowmap(name, fn, seq, [a, res], [gate, nxt["shift"], nxt["scale"]], [nxt["gamma"]],
                   row_outs=[(D_MODEL, F32), (D_MODEL, BF16), (D_MODEL, BF16)],
                   ts=_tile(seq, (512, 256, 128)), mm=(w_out, "nn"))


def _mix_out(z_g, ps, attn, res, wt_pool, wtmla_pad, w_out, gate, nxt, seq):
    branches = []

    def lhs(rows, vecs):
        zg = rows[0].astype(F32)
        bp = lax.dot_general(rows[1], vecs[1], _DIMS["nt"], preferred_element_type=F32)
        bm = lax.dot_general(rows[2], vecs[2], _DIMS["nt"], preferred_element_type=F32)
        branches[:] = [bp, bm]
        return (_sigmoid(zg[:, :D_MODEL]) * bp + _sigmoid(zg[:, D_MODEL:]) * bm).astype(BF16)

    def fn(rows, bats, vecs):
        acc, merged, rv = rows[0], rows[1], rows[-1]
        x_new = rv + bats[0] * acc
        return [x_new, acc, _norm_mod(x_new, vecs[0], bats[1], bats[2]), merged] + branches, [], []
    return _rowmap("mix_out", fn, seq, [z_g, ps, attn, res], [gate, nxt["shift"], nxt["scale"]],
                   [nxt["gamma"], wt_pool, wtmla_pad], row_outs=[(D_MODEL, F32)] + [(D_MODEL, BF16)] * 5,
                   ts=_tile(seq, (512, 256, 128)), mm=(w_out, "nn"), lhs=lhs)


def _out_loss(a, w_out, res, gate, target, seq, name):
    def fn(rows, bats, vecs):
        acc, rv, tv = rows
        err = rv + bats[0] * acc - tv
        dy = err * (1.0 / D_MODEL)
        return [dy, bats[0] * dy], [_colsum(dy * acc)], [_colsum(err * err)]
    return _rowmap(name, fn, seq, [a, res, target], [gate], row_outs=[(D_MODEL, F32), (D_MODEL, BF16)],
                   bat_outs=[D_MODEL], vec_outs=[D_MODEL], ts=_tile(seq, (512, 256, 128)), mm=(w_out, "nn"))


def _hidden_grad(df, w_out, g, u, name, after):
    tokens = df.shape[0]
    tm, chunk = _tile(tokens, (512, 256, 128)), 256
    after = jnp.zeros((SUBLANES, LANES), F32) if after is None else after

    def body(after_ref, df_ref, w_ref, g_ref, u_ref, o_ref):
        dfv = df_ref[...]
        for c in range(0, D_FF, chunk):
            cols = slice(c, c + chunk)
            dav = lax.dot_general(dfv, w_ref[cols, :], _DIMS["nt"], preferred_element_type=F32)
            gv, uv = g_ref[:, cols].astype(F32), u_ref[:, cols].astype(F32)
            sg = _sigmoid(gv)
            silu = gv * sg
            o_ref[:, cols] = (dav * uv * (sg * (1.0 + gv * (1.0 - sg)))).astype(BF16)
            o_ref[:, D_FF + c:D_FF + c + chunk] = (dav * silu).astype(BF16)

    row = lambda width: pl.BlockSpec((tm, width), lambda i: (i, 0))
    return pl.pallas_call(
        body, name=name, grid=(tokens // tm,), out_shape=jax.ShapeDtypeStruct((tokens, 2 * D_FF), BF16),
        in_specs=[_ANY, row(D_MODEL), pl.BlockSpec(w_out.shape, lambda i: (0, 0), pipeline_mode=pl.Buffered(1)),
                  row(D_FF), row(D_FF)],
        out_specs=row(2 * D_FF), compiler_params=_params("parallel"),
    )(after, df, w_out, g, u)


def _ffn_bwd_x(df, dres, saved, p, seq, tag, prev=None, early=None, mid=None):
    x, h, g, u, a, w_in, w_out = saved
    first = None if early is None else early(a, df)

    dgu = _hidden_grad(df, w_out, g, u, f"{tag}_bwd_da", first)
    operands = (a, df, dgu, h)
    after = None if mid is None else mid(operands)
    return _norm_mod_bwd([(dgu, w_in)], x, dres, p, seq, f"{tag}_bwd_norm", prev, after=after), operands


def _ffn_bwd_wout(a, df, tag):
    return _mm(a, df, "tn", f"{tag}_bwd_wout", out_dtype=BF16, tm=256, tn=D_MODEL)


def _ffn_bwd_win(operands, tag, after=None, half=None):
    _, _, dgu, h = operands
    if half is None:
        return _mm(dgu, h, "tn", f"{tag}_bwd_win", out_dtype=BF16, tm=512, tn=D_MODEL, after=after)
    return _mm(dgu, h, "tn", f"{tag}_bwd_win{half}", out_dtype=BF16, tm=512, tn=D_MODEL // 2, after=after,
               b_cols=(half, 1))


def _shift_rows(v, k, forward):
    n = v.shape[0]
    row = lax.broadcasted_iota(jnp.int32, v.shape, 0)
    if forward:
        return jnp.where(row >= k, pltpu.roll(v, k, 0), 0.0)
    return jnp.where(row < n - k, pltpu.roll(v, n - k, 0), 0.0)


def _window_sums(v, forward):
    out, s, k = [], v, 1
    for _ in range(POOL_GROUPS):
        s = s + _shift_rows(s, k, forward)
        out.append(s)
        k *= 2
    return out


def _by_group(vals, g):
    out = vals[-1]
    for idx in range(len(vals) - 2, -1, -1):
        out = jnp.where(g == idx, vals[idx], out)
    return out


def _inv_count(shape, g):
    t1 = lax.broadcasted_iota(jnp.int32, shape, 0) + 1
    window = _by_group([jnp.int32(2 ** (i + 1)) for i in range(POOL_GROUPS)], g)
    return 1.0 / jnp.minimum(t1, window).astype(F32)


def _pool_fwd(u, grp, scale, seq):
    tokens = u.shape[0]

    def body(u_ref, grp_ref, sc_ref, pooled_ref, pg_ref, ps_ref):
        g = pl.program_id(1)
        uv = u_ref[...]
        sums = _by_group(_window_sums(uv, True), g)
        pooled = (sums * _inv_count(uv.shape, g) - uv).astype(BF16)
        pg = jnp.dot(pooled, grp_ref[0].astype(BF16), preferred_element_type=F32)
        pooled_ref[...] = pooled
        pg_ref[...] = pg
        ps_ref[...] = (pg * sc_ref[...]).astype(BF16)

    blk = pl.BlockSpec((seq, POOL_GROUP_DIM), lambda b, g: (b, g))
    return pl.pallas_call(
        body, name="pool_fwd", grid=(tokens // seq, POOL_GROUPS),
        out_shape=(jax.ShapeDtypeStruct(u.shape, BF16), jax.ShapeDtypeStruct(u.shape, F32),
                   jax.ShapeDtypeStruct(u.shape, BF16)),
        in_specs=[blk, pl.BlockSpec((1, POOL_GROUP_DIM, POOL_GROUP_DIM), lambda b, g: (g, 0, 0)),
                  pl.BlockSpec((1, POOL_GROUP_DIM), lambda b, g: (0, g))],
        out_specs=(blk, blk, blk),
        compiler_params=_params("parallel", "parallel"),
    )(u, grp, scale)


def _pool_bwd(dps, pooled, pg, grp, scale, seq):
    tokens = dps.shape[0]

    def body(dps_ref, pooled_ref, pg_ref, grp_ref, sc_ref, du_ref, dgrp_ref, dsc_ref):
        g, b = pl.program_id(0), pl.program_id(1)
        dpsv = dps_ref[...]
        dpg = (dpsv * sc_ref[...]).astype(BF16)
        dsc = _colsum(dpsv * pg_ref[...])
        dgrp = lax.dot_general(pooled_ref[...], dpg, _DIMS["tn"], preferred_element_type=F32)

        @pl.when(b == 0)
        def _():
            dsc_ref[...] = dsc
            dgrp_ref[0] = dgrp

        @pl.when(b > 0)
        def _():
            dsc_ref[...] += dsc
            dgrp_ref[0] += dgrp

        dpool = lax.dot_general(dpg, grp_ref[0].astype(BF16), _DIMS["nt"], preferred_element_type=F32)
        sums = _by_group(_window_sums(dpool * _inv_count(dpool.shape, g), False), g)
        du_ref[...] = (sums - dpool).astype(BF16)

    blk = pl.BlockSpec((seq, POOL_GROUP_DIM), lambda g, b: (b, g))
    grp_spec = pl.BlockSpec((1, POOL_GROUP_DIM, POOL_GROUP_DIM), lambda g, b: (g, 0, 0))
    vec_spec = pl.BlockSpec((1, POOL_GROUP_DIM), lambda g, b: (0, g))
    return pl.pallas_call(
        body, name="pool_bwd", grid=(POOL_GROUPS, tokens // seq),
        out_shape=(jax.ShapeDtypeStruct(dps.shape, BF16), jax.ShapeDtypeStruct(grp.shape, F32),
                   jax.ShapeDtypeStruct(scale.shape, F32)),
        in_specs=[blk, blk, blk, grp_spec, vec_spec],
        out_specs=(blk, grp_spec, vec_spec),
        compiler_params=_params("arbitrary", "arbitrary"),
    )(dps, pooled, pg, grp, scale)


def _lane(shape):
    return lax.broadcasted_iota(jnp.int32, shape, len(shape) - 1)


def _rot(y):
    lane = _lane(y.shape)
    r = jnp.where(lane < QK_NOPE + QK_ROPE // 2,
                  -pltpu.roll(y, HEAD_SLAB - QK_ROPE // 2, 1), pltpu.roll(y, QK_ROPE // 2, 1))
    return jnp.where(jnp.logical_and(lane >= QK_NOPE, lane < QK_NOPE + QK_ROPE), r, 0.0)


def _part_rstd(x):
    sq = x * x
    nope = _lane(x.shape) < QK_NOPE
    s_nope = jnp.sum(jnp.where(nope, sq, 0.0), axis=-1, keepdims=True)
    s_rope = jnp.sum(sq, axis=-1, keepdims=True) - s_nope
    return jnp.where(nope, lax.rsqrt(s_nope * (1.0 / QK_NOPE) + NORM_EPS),
                     lax.rsqrt(s_rope * (1.0 / QK_ROPE) + NORM_EPS))


def _part_norm_bwd(dy, x, r, g):
    nope = _lane(x.shape) < QK_NOPE
    xhat = x * r
    dxhat = dy * g
    prod = dxhat * xhat
    m_nope = jnp.sum(jnp.where(nope, prod, 0.0), axis=-1, keepdims=True)
    m_rope = jnp.sum(prod, axis=-1, keepdims=True) - m_nope
    mean = jnp.where(nope, m_nope * (1.0 / QK_NOPE), m_rope * (1.0 / QK_ROPE))
    return r * (dxhat - xhat * mean), dy * xhat


def _mixer_in(h, wt_a, wt_p, wt_g, g_q, g_kv, seq):
    def fn(rows, bats, vecs):
        z_a, z_p, z_g = rows[:3]
        q, kv = z_a[:, :Q_LORA], z_a[:, Q_LORA:Q_LORA + KV_LORA]
        return [z_a, z_p, z_g, q * _rstd(q) * vecs[0], kv * _rstd(kv) * vecs[1]], [], []
    return _rowmap("mix_in", fn, seq, [h], vecs=[g_q, g_kv], lhs=lambda rows, vecs: rows[0],
                   mm=[(wt_a, "nt"), (wt_p, "nt"), (wt_g, "nt")], mm_sum=False,
                   row_outs=[(wt_a.shape[0], F32), (wt_p.shape[0], F32), (wt_g.shape[0], BF16),
                             (Q_LORA, BF16), (KV_LORA, BF16)])


def _latent_norm_bwd(dqp, wtq_pad, dkv, wt_kv, dkr, z_a, g_q, g_kv, seq):
    def fn(rows, bats, vecs):
        dq, dkv, dkrv, z = rows
        q, kv = z[:, :Q_LORA], z[:, Q_LORA:Q_LORA + KV_LORA]
        dxq, dgq = _norm_bwd(dq, q, _rstd(q), vecs[0])
        dxkv, dgkv = _norm_bwd(dkv, kv, _rstd(kv), vecs[1])
        return [jnp.concatenate([dxq, dxkv, dkrv], axis=1)], [], [_colsum(dgq), _colsum(dgkv)]
    return _rowmap("latent_norm_bwd", fn, seq, [dqp, dkv, dkr, z_a], vecs=[g_q, g_kv],
                   row_outs=[(Q_LORA + KV_LORA + HEAD_SLAB, BF16)], vec_outs=[Q_LORA, KV_LORA],
                   mm=[(wtq_pad, "nn"), (wt_kv, "nn")], mm_sum=False)


def _qk_prep_fwd(qn, wtq_pad, kvn, wt_kv, z_a, pos, g_q, g_kn, g_kr, inv_freq, seq):
    def fn(rows, bats, vecs):
        qv, kvv, kr, p = rows
        gq, gkn, gkr, invf = vecs
        ang = p * invf
        cos, sin = jnp.cos(ang), jnp.sin(ang)
        nope = _lane(kr.shape) < QK_NOPE
        krn = kr * _rstd(kr, QK_ROPE) * gkr
        krr = krn * cos + _rot(krn) * sin
        qs, ks, vs = [], [], []
        for h in range(N_HEADS):
            xq = qv[:, h * HEAD_SLAB:(h + 1) * HEAD_SLAB]
            y = xq * _part_rstd(xq) * gq
            qs.append(y * cos + _rot(y) * sin)
            xk = kvv[:, h * HEAD_SLAB:(h + 1) * HEAD_SLAB]
            kn = jnp.where(nope, xk, 0.0)
            ks.append(jnp.where(nope, kn * _rstd(kn, QK_NOPE) * gkn, krr))
            vs.append(jnp.where(nope, 0.0, xk))
        return [jnp.concatenate(v, axis=1) for v in (qs, ks, vs)] + [qv, kvv], [], []
    width = N_HEADS * HEAD_SLAB
    return _rowmap("qk_prep", fn, seq, [qn, kvn, (z_a, HEAD_SLAB, 5), pos], vecs=[g_q, g_kn, g_kr, inv_freq],
                   row_outs=[(width, BF16)] * 3 + [(width, F32)] * 2, mm=[(wtq_pad, "nt"), (wt_kv, "nt")],
                   mm_sum=False)


def _qk_prep_bwd(dqc, dkc, dvp, qp, kv, z_a, pos, g_q, g_kn, g_kr, inv_freq, seq):
    def fn(rows, bats, vecs):
        dq, dk, dv, qv, kvv, kr, p = rows
        gq, gkn, gkr, invf = vecs
        ang = p * invf
        cos, sin = jnp.cos(ang), jnp.sin(ang)
        nope = _lane(kr.shape) < QK_NOPE
        dqs, dkvs = [], []
        dgq = jnp.zeros((1, HEAD_SLAB), F32)
        dgkn = jnp.zeros((1, HEAD_SLAB), F32)
        dkrr = jnp.zeros(kr.shape, F32)
        for h in range(N_HEADS):
            sl = slice(h * HEAD_SLAB, (h + 1) * HEAD_SLAB)
            dyr = dq[:, sl]
            dy = dyr * cos - _rot(dyr * sin)
            xq = qv[:, sl]
            dx, dg = _part_norm_bwd(dy, xq, _part_rstd(xq), gq)
            dqs.append(dx)
            dgq = dgq + _colsum(dg)
            dkh = dk[:, sl]
            dkrr = dkrr + jnp.where(nope, 0.0, dkh)
            kn = jnp.where(nope, kvv[:, sl], 0.0)
            dxk, dgk = _norm_bwd(jnp.where(nope, dkh, 0.0), kn, _rstd(kn, QK_NOPE), gkn, QK_NOPE)
            dgkn = dgkn + _colsum(dgk)
            dkvs.append(jnp.where(nope, dxk, dv[:, sl]))
        dkrn = dkrr * cos - _rot(dkrr * sin)
        dkr, dgkr = _norm_bwd(dkrn, kr, _rstd(kr, QK_ROPE), gkr, QK_ROPE)
        return ([jnp.concatenate(dqs, axis=1), jnp.concatenate(dkvs, axis=1), dkr], [],
                [dgq, dgkn, _colsum(dgkr)])
    width = N_HEADS * HEAD_SLAB
    return _rowmap("qk_prep_bwd", fn, seq, [dqc, dkc, dvp, qp, kv, (z_a, HEAD_SLAB, 5), pos],
                   vecs=[g_q, g_kn, g_kr, inv_freq],
                   row_outs=[(width, BF16), (width, BF16), (HEAD_SLAB, F32)],
                   vec_outs=[HEAD_SLAB] * 3, ts=_tile(seq, (512, 256, 128, 64, 32, 16, 8)))


def _scores(q, k_ref, keys, tq):
    s = lax.dot_general(q, k_ref[0:keys, :], _DIMS["nt"], preferred_element_type=F32) * ATTN_SCALE
    row = lax.broadcasted_iota(jnp.int32, (tq, tq), 0)
    col = lax.broadcasted_iota(jnp.int32, (tq, tq), 1)
    diag = jnp.where(col <= row, s[:, keys - tq:], -1e30)
    return diag if keys == tq else jnp.concatenate([s[:, :keys - tq], diag], axis=1)


def _attn_fwd(qc, kc, vp, seq):
    tokens = qc.shape[0]
    tq = _tile(seq, (256, 128))
    nq = seq // tq

    def body(q_ref, k_ref, v_ref, o_ref, lse_ref):
        for i in range(nq):
            rows, keys = slice(i * tq, (i + 1) * tq), (i + 1) * tq
            s = _scores(q_ref[rows, :], k_ref, keys, tq)
            m = jnp.max(s, axis=-1, keepdims=True)
            p = jnp.exp(s - m)
            l = jnp.sum(p, axis=-1, keepdims=True)
            acc = jnp.dot(p.astype(BF16), v_ref[0:keys, :], preferred_element_type=F32)
            o_ref[rows, :] = (acc / l).astype(BF16)
            lse_ref[rows, :] = jnp.broadcast_to(m + jnp.log(l), (tq, HEAD_SLAB))

    spec = pl.BlockSpec((seq, HEAD_SLAB), lambda b, h: (b, h))
    return pl.pallas_call(
        body, name="attn_fwd", grid=(tokens // seq, N_HEADS),
        out_shape=(jax.ShapeDtypeStruct(qc.shape, BF16), jax.ShapeDtypeStruct(qc.shape, F32)),
        in_specs=[spec] * 3, out_specs=(spec, spec),
        compiler_params=_params("parallel", "parallel"),
    )(qc, kc, vp)


def _attn_bwd(qc, kc, vp, o, lse, do, seq):
    tokens = qc.shape[0]
    tq = _tile(seq, (256, 128))
    nq = seq // tq

    def body(q_ref, k_ref, v_ref, o_ref, lse_ref, do_ref, dq_ref, dk_ref, dv_ref):
        dk_ref[...] = jnp.zeros(dk_ref.shape, F32)
        dv_ref[...] = jnp.zeros(dv_ref.shape, F32)
        for i in range(nq):
            rows, keys = slice(i * tq, (i + 1) * tq), (i + 1) * tq
            q, dov = q_ref[rows, :], do_ref[rows, :]
            delta = jnp.sum(dov.astype(F32) * o_ref[rows, :].astype(F32), axis=-1, keepdims=True)
            s = _scores(q, k_ref, keys, tq)
            p = jnp.exp(s - jnp.tile(lse_ref[rows, :], (1, keys // HEAD_SLAB)))
            dp = lax.dot_general(dov, v_ref[0:keys, :], _DIMS["nt"], preferred_element_type=F32)
            ds = (p * (dp - delta) * ATTN_SCALE).astype(BF16)
            dq_ref[rows, :] = jnp.dot(ds, k_ref[0:keys, :], preferred_element_type=F32)
            dk_ref[0:keys, :] += lax.dot_general(ds, q, _DIMS["tn"], preferred_element_type=F32)
            dv_ref[0:keys, :] += lax.dot_general(p.astype(BF16), dov, _DIMS["tn"], preferred_element_type=F32)

    spec = pl.BlockSpec((seq, HEAD_SLAB), lambda b, h: (b, h))
    out = jax.ShapeDtypeStruct(qc.shape, F32)
    return pl.pallas_call(
        body, name="attn_bwd", grid=(tokens // seq, N_HEADS),
        out_shape=(out, out, out), in_specs=[spec] * 6, out_specs=(spec, spec, spec),
        compiler_params=_params("parallel", "parallel"),
    )(qc, kc, vp, o, lse, do)


def _adamw(w, g, m, v, name):
    rows, cols = w.shape
    whole = rows * cols * 4 <= ADAMW_WHOLE_BYTES
    tr = rows if whole else _tile(rows, (256, 128, 64, 32, 16, 8))
    c1 = 1.0 - ADAM_B1 ** ADAM_STEP
    c2 = 1.0 - ADAM_B2 ** ADAM_STEP

    def body(w_ref, g_ref, m_ref, v_ref, d_ref, nm_ref, nv_ref):
        gv = g_ref[...]
        nm = ADAM_B1 * m_ref[...] + (1.0 - ADAM_B1) * gv
        nv = ADAM_B2 * v_ref[...] + (1.0 - ADAM_B2) * (gv * gv)
        d_ref[...] = -ADAM_LR * ((nm / c1) / (jnp.sqrt(nv / c2) + ADAM_EPS) + ADAM_WD * w_ref[...])
        nm_ref[...] = nm
        nv_ref[...] = nv

    spec = pl.BlockSpec((tr, cols), lambda i: (i, 0))
    out = jax.ShapeDtypeStruct(w.shape, F32)
    return pl.pallas_call(
        body, name=name, grid=(rows // tr,), out_shape=(out, out, out),
        in_specs=[spec] * 4, out_specs=(spec, spec, spec),
        compiler_params=_params("parallel"),
    )(w, g, m, v)


def _adamw_landed(w, landed, m, v, name):
    rows, cols = w.shape
    tr = _tile(rows, (176, 128, 96, 64, 32, 16, 8))
    c1 = 1.0 - ADAM_B1 ** ADAM_STEP
    c2 = 1.0 - ADAM_B2 ** ADAM_STEP
    n_parts = len(landed)

    def body(*refs):
        w_ref, m_ref, v_ref = refs[:3]
        g_ref, d_ref, nm_ref, nv_ref = refs[3 + n_parts:]
        parts = []
        for x_ref in refs[3:3 + n_parts]:
            acc = x_ref[0].astype(F32)
            for d in range(1, N_DEV):
                acc = acc + x_ref[d].astype(F32)
            parts.append(acc)
        gv = parts[0] if n_parts == 1 else jnp.concatenate(parts, axis=1)
        nm = ADAM_B1 * m_ref[...] + (1.0 - ADAM_B1) * gv
        nv = ADAM_B2 * v_ref[...] + (1.0 - ADAM_B2) * (gv * gv)
        g_ref[...] = gv
        d_ref[...] = -ADAM_LR * ((nm / c1) / (jnp.sqrt(nv / c2) + ADAM_EPS) + ADAM_WD * w_ref[...])
        nm_ref[...] = nm
        nv_ref[...] = nv

    spec = pl.BlockSpec((tr, cols), lambda i: (i, 0))
    out = jax.ShapeDtypeStruct(w.shape, F32)
    return pl.pallas_call(
        body, name=name, grid=(rows // tr,), out_shape=(out, out, out, out),
        in_specs=[spec] * 3 + [pl.BlockSpec((N_DEV, tr, x.shape[2]), lambda i: (0, i, 0)) for x in landed],
        out_specs=(spec, spec, spec, spec),
        compiler_params=_params("parallel"),
    )(w, m, v, *landed)


def _mod_cols(c_all, w_ada, b_cols):
    def body(c_ref, w_ref, b_ref, act_ref, mod_ref):
        cv = c_ref[...]
        act = cv * _sigmoid(cv)
        act_ref[...] = act
        mod_ref[...] = jnp.dot(act.astype(BF16), w_ref[...].astype(BF16),
                               preferred_element_type=F32) + b_ref[...]

    n = w_ada.shape[1]
    return pl.pallas_call(
        body, name="mod_cols",
        out_shape=(jax.ShapeDtypeStruct(c_all.shape, F32), jax.ShapeDtypeStruct((c_all.shape[0], n), F32)),
        compiler_params=pltpu.CompilerParams(vmem_limit_bytes=VMEM_LIMIT),
    )(c_all, w_ada, b_cols)


def _ada_grads(c_act, dmod_all, dmod_cols):
    def body(c_ref, d_ref, dc_ref, gw_ref, gb_ref):
        gw_ref[...] = lax.dot_general(c_ref[...].astype(BF16), dc_ref[...].astype(BF16), _DIMS["tn"],
                                      preferred_element_type=F32)
        gb_ref[...] = _colsum(d_ref[...])

    return pl.pallas_call(
        body, name="ada_grads",
        out_shape=(jax.ShapeDtypeStruct((c_act.shape[1], dmod_cols.shape[1]), F32),
                   jax.ShapeDtypeStruct((1, dmod_all.shape[1]), F32)),
        compiler_params=pltpu.CompilerParams(vmem_limit_bytes=VMEM_LIMIT),
    )(c_act, dmod_all, dmod_cols)


def _flat_rows(a):
    flat = a.reshape(-1)
    pad = (-flat.shape[0]) % (LANES * SUBLANES)
    if pad:
        flat = jnp.pad(flat, (0, pad))
    return flat.reshape(-1, LANES)


def _gather_start(w, groups, tag, after=None, peers=None):
    shards = [[(w[n] if n in ROW_SHARDED else w[n].T).astype(BF16) for n in names] for names in groups]
    return _exchange_start_groups(shards, f"gather_{tag}_start", after=after, peers=peers)


def _gather_wait(handle, names, tag, after, peers=ALL_PEERS):
    landed = _exchange_wait(handle, f"gather_{tag}_wait", after=after, peers=peers)
    if peers == CHIP_PEERS:
        landed = [_sibling_forward(x, f"gather_{tag}_forward{i}") for i, x in enumerate(landed)]
    return {n: g.reshape(-1, g.shape[2]) for n, g in zip(names, landed)}


def _scatter_start(grads, names, tag, after=None):
    blocks = [grads[n].reshape(N_DEV, -1, grads[n].shape[1]) for n in names]
    return _exchange_start(blocks, f"scatter_{tag}_start", scatter=True, after=after)


def _scatter_wait(handle, names, tag, after):
    landed = _exchange_wait(handle, f"scatter_{tag}_wait", scatter=True, after=after)
    return {n: [x] for n, x in zip(names, landed)}


def _pack_small(vals):
    return jnp.concatenate([_flat_rows(v.astype(F32)) for v in vals], axis=0)


def _unpack_small(packed, like):
    out, row = [], 0
    for v in like:
        rows = _flat_rows(v).shape[0]
        out.append(packed[row:row + rows].reshape(-1)[:v.size].reshape(v.shape))
        row += rows
    return out


def _lanes128(*parts):
    out = jnp.zeros((HEAD_SLAB,), F32)
    for off, v in parts:
        out = lax.dynamic_update_slice(out, v.reshape(-1).astype(F32), (off,))
    return out.reshape(1, HEAD_SLAB)


def _step(x, c, positions, w, m, v, loss_target):
    nseq, seq, _ = x.shape
    tokens = nseq * seq
    me = _index(_my_pos())
    strip = lambda d: {n: (a[0] if a.ndim > 2 else a) for n, a in d.items()}
    shapes = {n: a.shape for n, a in w.items()}
    w, m, v = strip(w), strip(m), strip(v)

    c_all = _all_gather(c.reshape(-1, LANES), "gather_c").reshape(N_DEV * nseq, D_MODEL)
    n_ada = w["w_ada"].shape[1]
    b_cols = lax.dynamic_slice(w["b_ada"], (0, me * n_ada), (1, n_ada))
    c_act, mod_cols = _mod_cols(c_all, w["w_ada"], b_cols)
    mod_all = _all_gather(mod_cols, "gather_mod")
    mod = lax.dynamic_slice(mod_all, (0, me * nseq, 0), (N_DEV, nseq, n_ada))
    mod = mod.transpose(1, 0, 2).reshape(nseq, 3, 3, 1, D_MODEL)

    (h_f1i, h_f1o, h_mix_in, h_mix, h_f2), tok = _gather_start(
        w, (("w_ffn1_in",), ("w_ffn1_out",), MIXER[:1], MIXER[1:], ("w_ffn2_in", "w_ffn2_out")), "weights",
        after=mod_all, peers=[CHIP_PEERS, ALL_PEERS, CHIP_PEERS, ALL_PEERS, ALL_PEERS])
    started = tok[0:1, 0:1]

    g_q = _lanes128((0, w["q_norm_nope"]), (QK_NOPE, w["q_norm_rope"]))
    g_kn = _lanes128((0, w["k_norm_nope"]))
    g_kr = _lanes128((QK_NOPE, w["k_norm_rope"]))
    freq = ROPE_THETA ** (-jnp.arange(0, QK_ROPE, 2, dtype=F32) / QK_ROPE)
    inv_freq = _lanes128((QK_NOPE, jnp.concatenate([freq, freq])))
    pos = positions.reshape(tokens, 1).astype(F32)

    def sub(k, gamma, coef):
        return dict(gamma=w[gamma], shift=mod[:, k, 0] + started, scale=mod[:, k, 1], gate=coef * mod[:, k, 2])
    p1, pm, p2 = sub(0, "norm_ffn1", 0.5), sub(1, "norm_mix", 1.0), sub(2, "norm_ffn2", 0.5)

    x0 = x.reshape(tokens, D_MODEL)
    h1 = _norm_mod_fwd(x0, p1, seq, "ffn1_norm")
    wt_f1i = _gather_wait(h_f1i, ("w_ffn1_in",), "ffn1_in", h1, peers=CHIP_PEERS)["w_ffn1_in"]
    g1, u1, a1 = _ffn_in_act(h1, wt_f1i, "ffn1_in")
    w_f1o = _gather_wait(h_f1o, ("w_ffn1_out",), "ffn1_out", a1)["w_ffn1_out"]
    x1, f1, h2 = _out_residual(a1, w_f1o, x0, p1["gate"], pm, seq, "ffn1_out")
    saved1 = (x0, h1, g1, u1, a1, wt_f1i, w_f1o)

    wt_in = _gather_wait(h_mix_in, MIXER[:1], "mix_in", h2, peers=CHIP_PEERS)["w_in"]
    zero_rows = lambda rows: jnp.zeros((rows, D_MODEL), BF16)
    wt_p = wt_in[:512]
    wt_a = jnp.concatenate([wt_in[512:1152], zero_rows(QK_NOPE), wt_in[1152:1184], zero_rows(32)], axis=0)
    wt_g = wt_in[1184:]
    z_a, z_p, z_g, qn, kvn = _mixer_in(h2, wt_a, wt_p, wt_g, w["q_a_norm"], w["kv_a_norm"], seq)

    full = _gather_wait(h_mix, MIXER[1:], "mix", z_g)
    wtq_pad = jnp.pad(full["w_q_up"].reshape(N_HEADS, 96, Q_LORA), ((0, 0), (0, 32), (0, 0))).reshape(-1, Q_LORA)
    wtmla_pad = jnp.pad(full["w_mla_proj"].reshape(D_MODEL, N_HEADS, 64), ((0, 0), (0, 0), (64, 0))).reshape(D_MODEL, -1)
    wt_pool, wt_kv, w_mix_out = full["w_pool_proj"], full["w_kv_up"], full["w_out"]
    pooled, pg, ps = _pool_fwd(z_p, w["pool_grp"], w["pool_scale"], seq)
    qc, kc, vp, qp, kv = _qk_prep_fwd(qn, wtq_pad, kvn, wt_kv, z_a, pos, g_q, g_kn, g_kr, inv_freq, seq)
    attn, lse = _attn_fwd(qc, kc, vp, seq)
    x2, o_mix, h3, merged, br_pool, br_mla = _mix_out(z_g, ps, attn, x1, wt_pool, wtmla_pad, w_mix_out, pm["gate"],
                                                      p2, seq)

    ffn2_w = _gather_wait(h_f2, ("w_ffn2_in", "w_ffn2_out"), "ffn2", h3)
    g2, u2, a2 = _ffn_in_act(h3, ffn2_w["w_ffn2_in"], "ffn2_in")
    dy, df2, dgate2, sq_err = _out_loss(a2, ffn2_w["w_ffn2_out"], x2, p2["gate"],
                                        loss_target.reshape(tokens, D_MODEL), seq, "ffn2_out")
    saved2 = (x2, h3, g2, u2, a2, ffn2_w["w_ffn2_in"], ffn2_w["w_ffn2_out"])

    grads = {}
    (dx2, do_mix, dsh2, dsc2, dgate_m, dg_ffn2), ops2 = _ffn_bwd_x(df2, dy, saved2, p2, seq, "ffn2", (o_mix, pm["gate"]))
    grads["w_ffn2_out"], grads["w_ffn2_in"] = _ffn_bwd_wout(ops2[0], ops2[1], "ffn2"), _ffn_bwd_win(ops2, "ffn2")
    s_f2, tok = _scatter_start(grads, ("w_ffn2_in", "w_ffn2_out"), "ffn2")

    grads["w_out"] = _mm(merged, do_mix, "tn", "mix_bwd_wout", out_dtype=BF16, tm=512, tn=D_MODEL)

    def merge_bwd(rows, bats, vecs):
        dmv, zg, bp, bm = (r.astype(F32) for r in rows)
        s_p, s_m = _sigmoid(zg[:, :D_MODEL]), _sigmoid(zg[:, D_MODEL:])
        dzg = jnp.concatenate([dmv * bp * s_p * (1.0 - s_p), dmv * bm * s_m * (1.0 - s_m)], axis=1)
        dbp, dbm = (dmv * s_p).astype(BF16), (dmv * s_m).astype(BF16)
        dps_v = jnp.dot(dbp, vecs[0], preferred_element_type=F32)
        dattn = jnp.dot(dbm, vecs[1], preferred_element_type=F32)
        return [dbp, dbm, dzg, dps_v, dattn], [], []
    dbr_pool, dbr_mla, dz_g, dps, d_attn = _rowmap(
        "mix_bwd_dmerged", merge_bwd, seq, [do_mix, z_g, br_pool, br_mla], vecs=[wt_pool, wtmla_pad],
        row_outs=[(D_MODEL, BF16), (D_MODEL, BF16), (2 * D_MODEL, BF16), (POOL_WIDTH, F32), (D_MODEL, BF16)],
        mm=(w_mix_out, "nt"))

    grads["w_pool_proj"] = _mm(dbr_pool, ps, "tn", "pool_bwd_wproj", out_dtype=BF16, tm=512, tn=POOL_WIDTH)
    dz_p, dgrp, dpool_scale = _pool_bwd(dps, pooled, pg, w["pool_grp"], w["pool_scale"] + tok[0:1, 0:1], seq)

    dwtmla_pad = _mm(dbr_mla, attn, "tn", "mla_bwd_wproj", out_dtype=BF16, tm=512, tn=D_MODEL)
    grads["w_mla_proj"] = dwtmla_pad.reshape(D_MODEL, N_HEADS, HEAD_SLAB)[:, :, 64:].reshape(D_MODEL, -1)
    dqc, dkc, dvp = _attn_bwd(qc, kc, vp, attn, lse, d_attn, seq)
    dqp, dkv, dkr, dg_q, dg_kn, dg_kr = _qk_prep_bwd(dqc, dkc, dvp, qp, kv, z_a, pos, g_q, g_kn, g_kr, inv_freq, seq)
    dwtq_pad = _mm(dqp, qn, "tn", "q_up_bwd_w", out_dtype=BF16, tm=512, tn=Q_LORA)
    grads["w_q_up"] = dwtq_pad.reshape(N_HEADS, HEAD_SLAB, Q_LORA)[:, :96].reshape(-1, Q_LORA)
    grads["w_kv_up"] = _mm(dkv, kvn, "tn", "kv_up_bwd_w", out_dtype=BF16, tm=512, tn=KV_LORA)
    dz_a, dg_qa, dg_kva = _latent_norm_bwd(dqp, wtq_pad, dkv, wt_kv, dkr, z_a, w["q_a_norm"], w["kv_a_norm"], seq)

    dwt_a, dwt_p, dwt_g = _mm_tn_shared([dz_a, dz_p, dz_g], h2, "mix_in_bwd_w")
    grads["w_in"] = jnp.concatenate([dwt_p, dwt_a[:640], dwt_a[704:736], dwt_g], axis=0)

    small_early = [dg_ffn2.reshape(w["norm_ffn2"].shape), dgrp, dpool_scale, dg_qa, dg_kva, dg_q[:, :QK_NOPE],
                   dg_q[:, QK_NOPE:QK_NOPE + QK_ROPE], dg_kn[:, :QK_NOPE], dg_kr[:, QK_NOPE:QK_NOPE + QK_ROPE]]
    s_small, tok = _exchange_start([_pack_small(small_early)], "gather_small_start")
    s_mix, tok = _scatter_start(grads, MIXER, "mix", after=tok)
    dh2 = [(dz_a, wt_a), (dz_p, wt_p), (dz_g, wt_g)]
    pm_tied = dict(pm, scale=pm["scale"] + tok[0:1, 0:1])
    dx1, df1, dsh_m, dsc_m, dgate1, dg_mix = _norm_mod_bwd(dh2, x1, dx2, pm_tied, seq, "mix_bwd_norm", (f1, p1["gate"]))

    handles = {}

    def ffn1_early(a, df):
        grads["w_ffn1_out"] = _ffn_bwd_wout(a, df, "ffn1")
        handles["f1o"], token = _scatter_start(grads, ("w_ffn1_out",), "ffn1_out")
        return token

    def ffn1_mid(operands):
        first = _ffn_bwd_win(operands, "ffn1", half=0)
        handles["f1i0"], token = _exchange_start([first.reshape(N_DEV, -1, first.shape[1])],
                                                 "scatter_ffn1_in0_start", scatter=True)
        return token

    (dx0, dsh1, dsc1, dg_ffn1), ops1 = _ffn_bwd_x(df1, dx1, saved1, p1, seq, "ffn1", early=ffn1_early,
                                                     mid=ffn1_mid)
    s_f1o = handles["f1o"]

    dmod = jnp.stack([jnp.stack([dsh1, dsc1, 0.5 * dgate1], axis=1),
                      jnp.stack([dsh_m, dsc_m, dgate_m], axis=1),
                      jnp.stack([dsh2, dsc2, 0.5 * dgate2], axis=1)], axis=1)
    n_dmod = nseq * 9 * D_MODEL // LANES
    tail = _all_gather(jnp.concatenate([dmod.reshape(-1, LANES), _flat_rows(dg_ffn1), _flat_rows(dg_mix),
                                        _flat_rows(sq_err)], axis=0), "gather_dmod")
    dmod_all = tail[:, :n_dmod].reshape(N_DEV * nseq, 9 * D_MODEL)

    second = _ffn_bwd_win(ops1, "ffn1", after=tail, half=1)
    s_second, tok = _exchange_start([second.reshape(N_DEV, -1, second.shape[1])], "scatter_ffn1_in1_start",
                                    scatter=True, after=tail)
    s_f1i = (handles["f1i0"], s_second)

    dmod_cols = lax.dynamic_slice(dmod_all, (0, me * n_ada), (N_DEV * nseq, n_ada)) + tok[0:1, 0:1]
    g_w_ada, g_b_ada = _ada_grads(c_act, dmod_all, dmod_cols)
    tail_sum = _sum_blocks(tail[:, n_dmod:], "sum_tail")
    g_norm_ffn1 = tail_sum[:SUBLANES].reshape(1, D_MODEL)
    g_norm_mix = tail_sum[SUBLANES:2 * SUBLANES].reshape(1, D_MODEL)
    loss = 0.5 * jnp.sum(tail_sum[2 * SUBLANES:]) * (1.0 / D_MODEL)
    small_all = _exchange_wait(s_small, "gather_small_wait", after=g_b_ada)[0]
    small_sum = _sum_blocks(small_all, "sum_small")
    small = dict(zip(SMALL[2:], _unpack_small(small_sum, [w[n] for n in SMALL[2:]])))
    grad_w = dict(small, w_ada=g_w_ada, b_ada=g_b_ada, norm_ffn1=g_norm_ffn1, norm_mix=g_norm_mix)

    delta, new_m, new_v = {}, {}, {}

    def update(names, landed=None):
        for n in names:
            if landed is None:
                delta[n], new_m[n], new_v[n] = _adamw(w[n], grad_w[n], m[n], v[n], f"adamw_{n}")
            elif n in KEPT_TRANSPOSED:
                res = _adamw_landed(w[n].T, landed[n], m[n].T, v[n].T, f"adamw_{n}")
                grad_w[n], delta[n], new_m[n], new_v[n] = (r.T for r in res)
            elif n in ROW_SHARDED:
                grad_w[n], delta[n], new_m[n], new_v[n] = _adamw_landed(w[n], landed[n], m[n], v[n], f"adamw_{n}")
            else:
                grad_w[n] = _sum_blocks(landed[n][0], f"sum_{n}").T
                delta[n], new_m[n], new_v[n] = _adamw(w[n], grad_w[n], m[n], v[n], f"adamw_{n}")

    update(("w_ada",))
    rep = ("b_ada",) + SMALL
    d_s, m_s, v_s = _adamw(_pack_small([w[n] for n in rep]), _pack_small([grad_w[n] for n in rep]),
                           _pack_small([m[n] for n in rep]), _pack_small([v[n] for n in rep]), "adamw_small")
    like = [w[n] for n in rep]
    for dst, packed in ((delta, d_s), (new_m, m_s), (new_v, v_s)):
        dst.update(zip(rep, _unpack_small(packed, like)))
    update(("w_ffn2_in", "w_ffn2_out"), _scatter_wait(s_f2, ("w_ffn2_in", "w_ffn2_out"), "ffn2", after=d_s))
    update(MIXER, _scatter_wait(s_mix, MIXER, "mix", after=delta["w_ffn2_out"]))
    update(("w_ffn1_out",), _scatter_wait(s_f1o, ("w_ffn1_out",), "ffn1_out", after=delta["w_out"]))
    halves = [_exchange_wait(h, f"scatter_ffn1_in{i}_wait", scatter=True, after=delta["w_ffn1_out"])[0]
              for i, h in enumerate(s_f1i)]
    update(("w_ffn1_in",), {"w_ffn1_in": halves})

    lead = lambda d: [d[n].reshape(shapes[n]) for n in WEIGHTS]
    return (loss, dx0.reshape(x.shape), *lead(grad_w), *lead(delta), *lead(new_m), *lead(new_v))


def kernel(x, c, positions, w_ada, b_ada, norm_ffn1, w_ffn1_in, w_ffn1_out, norm_mix, w_in, pool_grp, pool_scale, w_pool_proj, q_a_norm, w_q_up, kv_a_norm, w_kv_up, q_norm_nope, q_norm_rope, k_norm_nope, k_norm_rope, w_mla_proj, w_out, norm_ffn2, w_ffn2_in, w_ffn2_out, loss_target, m_w_ada, m_b_ada, m_norm_ffn1, m_w_ffn1_in, m_w_ffn1_out, m_norm_mix, m_w_in, m_pool_grp, m_pool_scale, m_w_pool_proj, m_q_a_norm, m_w_q_up, m_kv_a_norm, m_w_kv_up, m_q_norm_nope, m_q_norm_rope, m_k_norm_nope, m_k_norm_rope, m_w_mla_proj, m_w_out, m_norm_ffn2, m_w_ffn2_in, m_w_ffn2_out, v_w_ada, v_b_ada, v_norm_ffn1, v_w_ffn1_in, v_w_ffn1_out, v_norm_mix, v_w_in, v_pool_grp, v_pool_scale, v_w_pool_proj, v_q_a_norm, v_w_q_up, v_kv_a_norm, v_w_kv_up, v_q_norm_nope, v_q_norm_rope, v_k_norm_nope, v_k_norm_rope, v_w_mla_proj, v_w_out, v_norm_ffn2, v_w_ffn2_in, v_w_ffn2_out):
    w = dict(w_ada=w_ada, b_ada=b_ada, norm_ffn1=norm_ffn1, w_ffn1_in=w_ffn1_in, w_ffn1_out=w_ffn1_out, norm_mix=norm_mix, w_in=w_in, pool_grp=pool_grp, pool_scale=pool_scale, w_pool_proj=w_pool_proj, q_a_norm=q_a_norm, w_q_up=w_q_up, kv_a_norm=kv_a_norm, w_kv_up=w_kv_up, q_norm_nope=q_norm_nope, q_norm_rope=q_norm_rope, k_norm_nope=k_norm_nope, k_norm_rope=k_norm_rope, w_mla_proj=w_mla_proj, w_out=w_out, norm_ffn2=norm_ffn2, w_ffn2_in=w_ffn2_in, w_ffn2_out=w_ffn2_out)
    m = dict(w_ada=m_w_ada, b_ada=m_b_ada, norm_ffn1=m_norm_ffn1, w_ffn1_in=m_w_ffn1_in, w_ffn1_out=m_w_ffn1_out, norm_mix=m_norm_mix, w_in=m_w_in, pool_grp=m_pool_grp, pool_scale=m_pool_scale, w_pool_proj=m_w_pool_proj, q_a_norm=m_q_a_norm, w_q_up=m_w_q_up, kv_a_norm=m_kv_a_norm, w_kv_up=m_w_kv_up, q_norm_nope=m_q_norm_nope, q_norm_rope=m_q_norm_rope, k_norm_nope=m_k_norm_nope, k_norm_rope=m_k_norm_rope, w_mla_proj=m_w_mla_proj, w_out=m_w_out, norm_ffn2=m_norm_ffn2, w_ffn2_in=m_w_ffn2_in, w_ffn2_out=m_w_ffn2_out)
    v = dict(w_ada=v_w_ada, b_ada=v_b_ada, norm_ffn1=v_norm_ffn1, w_ffn1_in=v_w_ffn1_in, w_ffn1_out=v_w_ffn1_out, norm_mix=v_norm_mix, w_in=v_w_in, pool_grp=v_pool_grp, pool_scale=v_pool_scale, w_pool_proj=v_w_pool_proj, q_a_norm=v_q_a_norm, w_q_up=v_w_q_up, kv_a_norm=v_kv_a_norm, w_kv_up=v_w_kv_up, q_norm_nope=v_q_norm_nope, q_norm_rope=v_q_norm_rope, k_norm_nope=v_k_norm_nope, k_norm_rope=v_k_norm_rope, w_mla_proj=v_w_mla_proj, w_out=v_w_out, norm_ffn2=v_norm_ffn2, w_ffn2_in=v_w_ffn2_in, w_ffn2_out=v_w_ffn2_out)
    return _step(x, c, positions, w, m, v, loss_target)
```
